```python
import math
import jax
import jax.numpy as jnp
from jax import lax
import numpy as np

D_MODEL = 1024
BATCH = 2
SEQ = 8192
DEPTH = 1

N_HEADS = 8
HEAD_DIM = 64
ATTN_WIDTH = N_HEADS * HEAD_DIM
ROPE_THETA = 10000.0
MOBA_BLOCK = 256
MOBA_TOPK = 3
Q_CHUNK = 64
SSM_GROUP_SIZE = 16
SSM_GROUPS = 32
SSM_WIDTH = SSM_GROUPS * SSM_GROUP_SIZE
SSM_STATE = 64
DT_MIN = 0.001
DT_MAX = 0.1
IN_WIDTH = SSM_WIDTH + 3 * ATTN_WIDTH + 2 * D_MODEL
N_EXPERTS = 32
TOP_K = 4
D_FF = D_MODEL
SWIGLU_ALPHA = 1.702
SWIGLU_LIMIT = 7.0
MOE_BLOCK = 128
NORM_EPS = 1e-6
NEG_INF = -1e30

kernel_name = "hybrid_s5_moba_moe_block"


def rms_norm(t, g):
    tf = t.astype(jnp.float32)
    tf = tf * lax.rsqrt(jnp.mean(tf * tf, axis=-1, keepdims=True) + NORM_EPS)
    return tf * g.astype(jnp.float32)


def apply_rope(t, positions):
    half = HEAD_DIM // 2
    inv_freq = ROPE_THETA ** (-jnp.arange(half, dtype=jnp.float32) / half)
    ang = positions.astype(jnp.float32)[..., None] * inv_freq
    cos = jnp.cos(ang)[:, :, None, :]
    sin = jnp.sin(ang)[:, :, None, :]
    tf = t.astype(jnp.float32)
    t1, t2 = tf[..., :half], tf[..., half:]
    return jnp.concatenate([t1 * cos - t2 * sin, t2 * cos + t1 * sin], axis=-1)


def _linear_recurrence(left, right):
    a_l, b_l = left
    a_r, b_r = right
    return a_r * a_l, a_r * b_l + b_r


def s5_mixer(u, lam_re, lam_im, log_dt, b_re, b_im, c_re, c_im, d_skip, w_glu):
    bsz, seq, _ = u.shape
    uf = u.astype(jnp.float32)
    ug = uf.reshape(bsz, seq, SSM_GROUPS, SSM_GROUP_SIZE)
    lam = lax.complex(lam_re.astype(jnp.float32), lam_im.astype(jnp.float32))
    dt = jnp.exp(log_dt.astype(jnp.float32))[:, None]
    lam_bar = jnp.exp(lam * dt)
    b_mat = lax.complex(b_re.astype(jnp.float32), b_im.astype(jnp.float32))
    c_mat = lax.complex(c_re.astype(jnp.float32), c_im.astype(jnp.float32))
    b_bar = ((lam_bar - 1.0) / lam)[..., None] * b_mat
    bu = jnp.einsum('gpc,bsgc->bsgp', b_bar, ug.astype(jnp.complex64))
    a = jnp.broadcast_to(lam_bar, bu.shape)
    _, states = lax.associative_scan(_linear_recurrence, (a, bu), axis=1)
    y = jnp.einsum('gcp,bsgp->bsgc', c_mat, states).real.reshape(bsz, seq, SSM_WIDTH)
    y = y + d_skip.astype(jnp.float32) * uf
    y = jax.nn.gelu(y)
    y = y * jax.nn.sigmoid(y @ w_glu.astype(jnp.float32))
    return y


def moba_attention(q, k, v):
    bsz, nh, seq, hd = q.shape
    qf, kf, vf = q.astype(jnp.float32), k.astype(jnp.float32), v.astype(jnp.float32)
    nb = -(-seq // MOBA_BLOCK)
    pad = nb * MOBA_BLOCK - seq
    kb = jnp.pad(kf, ((0, 0), (0, 0), (0, pad), (0, 0))).reshape(bsz, nh, nb, MOBA_BLOCK, hd)
    vb = jnp.pad(vf, ((0, 0), (0, 0), (0, pad), (0, 0))).reshape(bsz, nh, nb, MOBA_BLOCK, hd)
    k_mean = jnp.mean(kb, axis=3)
    n_sel = min(MOBA_TOPK, nb)
    scale = HEAD_DIM ** -0.5
    n_chunks = seq // Q_CHUNK
    q_chunks = qf.reshape(bsz, nh, n_chunks, Q_CHUNK, hd).transpose(2, 0, 1, 3, 4)
    gather_blocks = jax.vmap(jax.vmap(lambda blocks, idx: blocks[idx]))
    block_ids = jnp.arange(nb)
    offs = jnp.arange(MOBA_BLOCK)

    def chunk_attend(args):
        ci, qi = args
        q_pos = ci * Q_CHUNK + jnp.arange(Q_CHUNK)
        q_blk = q_pos // MOBA_BLOCK
        own = (ci * Q_CHUNK) // MOBA_BLOCK
        gate = jnp.einsum('bhqd,bhnd->bhqn', qi, k_mean)
        gate = jnp.where(block_ids[None, :] < q_blk[:, None], gate, NEG_INF)
        _, sel = lax.top_k(gate, n_sel)
        sel_ok = sel < q_blk[:, None]
        k_sel = gather_blocks(kb, sel)
        v_sel = gather_blocks(vb, sel)
        s_sel = jnp.einsum('bhqd,bhqnld->bhqnl', qi, k_sel) * scale
        s_sel = jnp.where(sel_ok[..., None], s_sel, NEG_INF).reshape(bsz, nh, Q_CHUNK, n_sel * MOBA_BLOCK)
        k_own = lax.dynamic_index_in_dim(kb, own, axis=2, keepdims=False)
        v_own = lax.dynamic_index_in_dim(vb, own, axis=2, keepdims=False)
        s_own = jnp.einsum('bhqd,bhld->bhql', qi, k_own) * scale
        k_pos = own * MOBA_BLOCK + offs
        s_own = jnp.where(k_pos[None, :] <= q_pos[:, None], s_own, NEG_INF)
        p = jax.nn.softmax(jnp.concatenate([s_sel, s_own], axis=-1), axis=-1)
        p_sel = p[..., :n_sel * MOBA_BLOCK].reshape(bsz, nh, Q_CHUNK, n_sel, MOBA_BLOCK)
        p_own = p[..., n_sel * MOBA_BLOCK:]
        return (jnp.einsum('bhqnl,bhqnld->bhqd', p_sel, v_sel)
                + jnp.einsum('bhql,bhld->bhqd', p_own, v_own))

    out = lax.map(chunk_attend, (jnp.arange(n_chunks), q_chunks))
    return out.transpose(1, 2, 0, 3, 4).reshape(bsz, nh, seq, hd)


def moe_ffn(h, router_w, router_b, w_gate, b_gate, w_up, b_up, w_down, b_down):
    bsz, seq, d = h.shape
    n_tok = bsz * seq
    hf = h.reshape(n_tok, d).astype(jnp.float32)
    logits = hf @ router_w.astype(jnp.float32) + router_b.astype(jnp.float32)
    top_val, top_idx = lax.top_k(logits, TOP_K)
    top_w = jax.nn.softmax(top_val, axis=-1)
    n_assign = n_tok * TOP_K
    eid = top_idx.reshape(n_assign)
    order = jnp.argsort(eid)
    s_eid = eid[order]
    s_tok = order // TOP_K
    s_w = top_w.reshape(n_assign)[order]
    counts = jnp.bincount(eid, length=N_EXPERTS)
    starts = jnp.cumsum(counts) - counts
    padded = (counts + MOE_BLOCK - 1) // MOE_BLOCK * MOE_BLOCK
    p_ends = jnp.cumsum(padded)
    p_starts = p_ends - padded
    dest = p_starts[s_eid] + jnp.arange(n_assign) - starts[s_eid]
    n_blocks = -(-n_assign // MOE_BLOCK) + N_EXPERTS
    n_rows = n_blocks * MOE_BLOCK
    row_tok = jnp.full((n_rows,), n_tok, dtype=jnp.int32).at[dest].set(s_tok.astype(jnp.int32))
    row_w = jnp.zeros((n_rows,), jnp.float32).at[dest].set(s_w)
    blk_exp = jnp.minimum(jnp.searchsorted(p_ends, jnp.arange(n_blocks) * MOE_BLOCK, side='right'),
                          N_EXPERTS - 1)
    h_pad = jnp.concatenate([hf, jnp.zeros((1, d), jnp.float32)], axis=0)
    xs = h_pad[row_tok].reshape(n_blocks, MOE_BLOCK, d)

    def expert_block(args):
        xb, e = args
        g = xb @ w_gate[e].astype(jnp.float32) + b_gate[e].astype(jnp.float32)
        u = xb @ w_up[e].astype(jnp.float32) + b_up[e].astype(jnp.float32)
        g = jnp.minimum(g, SWIGLU_LIMIT)
        u = jnp.clip(u, -SWIGLU_LIMIT, SWIGLU_LIMIT)
        act = g * jax.nn.sigmoid(SWIGLU_ALPHA * g) * (u + 1.0)
        return act @ w_down[e].astype(jnp.float32) + b_down[e].astype(jnp.float32)

    yb = lax.map(expert_block, (xs, blk_exp)).reshape(n_rows, d)
    y = jnp.zeros((n_tok + 1, d), jnp.float32).at[row_tok].add(yb * row_w[:, None])
    return y[:n_tok].reshape(bsz, seq, d)


def setup_inputs(seed: int = 0) -> dict:
    key = jax.random.key(seed)
    ks = jax.random.split(key, 32)
    f32 = jnp.float32
    nrm = lambda k, shape, s: jax.random.normal(k, shape, f32) * s
    L, D, G, P, HG, E, F = DEPTH, D_MODEL, SSM_GROUPS, SSM_STATE, SSM_GROUP_SIZE, N_EXPERTS, D_FF
    x = jax.random.normal(ks[0], (BATCH, SEQ, D), f32)
    c = jax.random.normal(ks[1], (BATCH, D), f32)
    offset = jax.random.randint(ks[2], (BATCH, 1), 0, 1024, dtype=jnp.int32)
    positions = (jnp.arange(SEQ, dtype=jnp.int32)[None, :] + offset).astype(jnp.int32)
    lam_im0 = jnp.pi * jnp.arange(P, dtype=f32)
    return {
        'x': x,
        'c': c,
        'positions': positions,
        'ada_w': nrm(ks[3], (L, D, 6 * D), 0.5 * D ** -0.5),
        'ada_b': nrm(ks[4], (L, 6 * D), 0.01),
        'mix_pre_g': 1.0 + nrm(ks[5], (L, D), 0.05),
        'mix_post_g': 1.0 + nrm(ks[6], (L, D), 0.05),
        'ffn_pre_g': 1.0 + nrm(ks[7], (L, D), 0.05),
        'ffn_post_g': 1.0 + nrm(ks[8], (L, D), 0.05),
        'w_in': nrm(ks[9], (L, D, IN_WIDTH), D ** -0.5),
        'ssm_lam_re': -0.5 + nrm(ks[10], (L, G, P), 0.01),
        'ssm_lam_im': lam_im0 + nrm(ks[11], (L, G, P), 0.01),
        'ssm_log_dt': jax.random.uniform(ks[12], (L, G), f32, math.log(DT_MIN), math.log(DT_MAX)),
        'ssm_b_re': nrm(ks[13], (L, G, P, HG), (2 * HG) ** -0.5),
        'ssm_b_im': nrm(ks[14], (L, G, P, HG), (2 * HG) ** -0.5),
        'ssm_c_re': nrm(ks[15], (L, G, HG, P), (2 * P) ** -0.5),
        'ssm_c_im': nrm(ks[16], (L, G, HG, P), (2 * P) ** -0.5),
        'ssm_d': nrm(ks[17], (L, SSM_WIDTH), 1.0),
        'ssm_w_glu': nrm(ks[18], (L, SSM_WIDTH, SSM_WIDTH), SSM_WIDTH ** -0.5),
        'w_ssm_branch': nrm(ks[19], (L, SSM_WIDTH, D), SSM_WIDTH ** -0.5),
        'w_attn_branch': nrm(ks[20], (L, ATTN_WIDTH, D), ATTN_WIDTH ** -0.5),
        'w_out': nrm(ks[21], (L, D, D), D ** -0.5),
        'router_w': nrm(ks[22], (L, D, E), D ** -0.5),
        'router_b': nrm(ks[23], (L, E), 0.01),
        'w_gate': nrm(ks[24], (L, E, D, F), D ** -0.5),
        'b_gate': nrm(ks[25], (L, E, F), 0.01),
        'w_up': nrm(ks[26], (L, E, D, F), D ** -0.5),
        'b_up': nrm(ks[27], (L, E, F), 0.01),
        'w_down': nrm(ks[28], (L, E, F, D), F ** -0.5),
        'b_down': nrm(ks[29], (L, E, D), 0.01),
    }


def reference(x, c, positions, ada_w, ada_b, mix_pre_g, mix_post_g, ffn_pre_g, ffn_post_g,
              w_in, ssm_lam_re, ssm_lam_im, ssm_log_dt, ssm_b_re, ssm_b_im, ssm_c_re, ssm_c_im,
              ssm_d, ssm_w_glu, w_ssm_branch, w_attn_branch, w_out,
              router_w, router_b, w_gate, b_gate, w_up, b_up, w_down, b_down):
    bsz, seq, _ = x.shape
    dtype = x.dtype
    cond = jax.nn.silu(c.astype(jnp.float32))
    splits = [SSM_WIDTH, SSM_WIDTH + ATTN_WIDTH, SSM_WIDTH + 2 * ATTN_WIDTH,
              SSM_WIDTH + 3 * ATTN_WIDTH, SSM_WIDTH + 3 * ATTN_WIDTH + D_MODEL]
    for l in range(DEPTH):
        ada = cond @ ada_w[l].astype(jnp.float32) + ada_b[l].astype(jnp.float32)
        sh_m, sc_m, gt_m, sh_f, sc_f, gt_f = [a[:, None, :] for a in jnp.split(ada, 6, axis=-1)]

        h = (rms_norm(x, mix_pre_g[l]) * (1.0 + sc_m) + sh_m).astype(dtype)
        proj = h @ w_in[l]
        u_s, q, k, v, g_s, g_a = jnp.split(proj, splits, axis=-1)
        y_s = s5_mixer(u_s, ssm_lam_re[l], ssm_lam_im[l], ssm_log_dt[l], ssm_b_re[l], ssm_b_im[l],
                       ssm_c_re[l], ssm_c_im[l], ssm_d[l], ssm_w_glu[l])
        q = apply_rope(q.reshape(bsz, seq, N_HEADS, HEAD_DIM), positions).transpose(0, 2, 1, 3)
        k = apply_rope(k.reshape(bsz, seq, N_HEADS, HEAD_DIM), positions).transpose(0, 2, 1, 3)
        v = v.reshape(bsz, seq, N_HEADS, HEAD_DIM).transpose(0, 2, 1, 3)
        y_a = moba_attention(q, k, v).transpose(0, 2, 1, 3).reshape(bsz, seq, ATTN_WIDTH)
        merged = (jax.nn.sigmoid(g_s.astype(jnp.float32)) * (y_s @ w_ssm_branch[l].astype(jnp.float32))
                  + jax.nn.sigmoid(g_a.astype(jnp.float32)) * (y_a @ w_attn_branch[l].astype(jnp.float32)))
        mix = merged @ w_out[l].astype(jnp.float32)
        x = (x.astype(jnp.float32) + gt_m * rms_norm(mix, mix_post_g[l])).astype(dtype)

        h2 = (rms_norm(x, ffn_pre_g[l]) * (1.0 + sc_f) + sh_f).astype(dtype)
        ff = moe_ffn(h2, router_w[l], router_b[l], w_gate[l], b_gate[l], w_up[l], b_up[l],
                     w_down[l], b_down[l])
        x = (x.astype(jnp.float32) + gt_f * rms_norm(ff, ffn_post_g[l])).astype(dtype)
    return x
```

```python
import functools
import math

import jax
import jax.numpy as jnp
from jax import lax
from jax.experimental import pallas as pl
from jax.experimental.pallas import tpu as pltpu

F32 = jnp.float32
BF16 = jnp.bfloat16
I32 = jnp.int32

N_HEADS = 8
HEAD_DIM = 64
ROPE_THETA = 10000.0
MOBA_BLOCK = 256
MOBA_TOPK = 3
SSM_GROUP_SIZE = 16
SSM_GROUPS = 32
SSM_STATE = 64
N_EXPERTS = 32
TOP_K = 4
SWIGLU_ALPHA = 1.702
SWIGLU_LIMIT = 7.0
NORM_EPS = 1e-6
NEG_INF = -1e30

LANES = 128
SSM_CHUNK = 64
EXPERT_ROWS = 256
VMEM_LIMIT = 56 * 1024 * 1024
HIGHEST = lax.Precision.HIGHEST


def _cparams(sem):
    return pltpu.CompilerParams(dimension_semantics=sem, vmem_limit_bytes=VMEM_LIMIT)


def _dot(a, b, **kw):
    return jnp.dot(a, b, preferred_element_type=F32, **kw)


def _dot_t(a, b, **kw):
    return lax.dot_general(a, b, (((1,), (1,)), ((), ())), preferred_element_type=F32, **kw)


def _ada_kernel(c_ref, w_ref, b_ref, o_ref):
    c = c_ref[...]
    cond = c * jax.nn.sigmoid(c)
    o_ref[...] = _dot(cond, w_ref[...], precision=HIGHEST) + b_ref[...]


def _ada(c, ada_w, ada_b):
    bsz, d = c.shape
    n = ada_w.shape[1]
    c8 = jnp.zeros((8, d), F32).at[:bsz].set(c)
    out = pl.pallas_call(
        _ada_kernel,
        grid=(n // d,),
        in_specs=[pl.BlockSpec((8, d), lambda j: (0, 0)),
                  pl.BlockSpec((d, d), lambda j: (0, j)),
                  pl.BlockSpec((1, d), lambda j: (0, j))],
        out_specs=pl.BlockSpec((8, d), lambda j: (0, j)),
        out_shape=jax.ShapeDtypeStruct((8, n), F32),
        compiler_params=_cparams(("arbitrary",)),
        name="ada",
    )(c8, ada_w, ada_b.reshape(1, n))
    return out[:bsz]


def _inproj_kernel(x_ref, sc_ref, sh_ref, g_ref, pos_ref, invf_ref, w_ref,
                   u_ref, q_ref, k_ref, v_ref, gs_ref, ga_ref, km_ref, *, ssm_w, attn_w, d_model):
    x = x_ref[...]
    ms = jnp.mean(x * x, axis=-1, keepdims=True)
    xn = x * lax.rsqrt(ms + NORM_EPS) * g_ref[...]
    h = (xn * (1.0 + sc_ref[0]) + sh_ref[0]).astype(BF16)

    def proj(lo, width):
        return _dot(h, w_ref[:, lo:lo + width])

    u_ref[...] = proj(0, ssm_w).astype(BF16)

    ang = pos_ref[...].astype(F32) * invf_ref[...]
    reps = attn_w // LANES
    cos = jnp.concatenate([jnp.cos(ang)] * reps, axis=1)
    sin = jnp.concatenate([jnp.sin(ang)] * reps, axis=1)
    lane = lax.broadcasted_iota(I32, (1, attn_w), 1)
    first = (lane % HEAD_DIM) < (HEAD_DIM // 2)
    sin = jnp.where(first, -sin, sin)

    def rope(t):
        rot = jnp.where(first, pltpu.roll(t, attn_w - HEAD_DIM // 2, axis=1),
                        pltpu.roll(t, HEAD_DIM // 2, axis=1))
        return t * cos + rot * sin

    q = rope(proj(ssm_w, attn_w))
    q_ref[...] = (q * (HEAD_DIM ** -0.5)).astype(BF16)
    k = rope(proj(ssm_w + attn_w, attn_w))
    k_ref[...] = k.astype(BF16)
    nblk = k.shape[0] // MOBA_BLOCK
    km_ref[0] = jnp.mean(k.reshape(nblk, MOBA_BLOCK, attn_w), axis=1)
    v_ref[...] = proj(ssm_w + 2 * attn_w, attn_w).astype(BF16)
    gs_ref[...] = jax.nn.sigmoid(proj(ssm_w + 3 * attn_w, d_model)).astype(BF16)
    ga_ref[...] = jax.nn.sigmoid(proj(ssm_w + 3 * attn_w + d_model, d_model)).astype(BF16)


def _inproj(x2, sc, sh, g, pos, w_in, seq, tm):
    n_tok, d = x2.shape
    attn_w = N_HEADS * HEAD_DIM
    ssm_w = SSM_GROUPS * SSM_GROUP_SIZE
    in_w = w_in.shape[1]
    half = HEAD_DIM // 2
    inv_freq = ROPE_THETA ** (-jnp.arange(half, dtype=F32) / half)
    invf = jnp.tile(inv_freq, LANES // half).reshape(1, LANES)
    nt = n_tok // tm
    per_b = seq // tm
    nblk = tm // MOBA_BLOCK
    tok = lambda w: pl.BlockSpec((tm, w), lambda i: (i, 0))
    bvec = pl.BlockSpec((1, 1, d), lambda i: (i // per_b, 0, 0))
    outs = pl.pallas_call(
        functools.partial(_inproj_kernel, ssm_w=ssm_w, attn_w=attn_w, d_model=d),
        grid=(nt,),
        in_specs=[tok(d), bvec, bvec,
                  pl.BlockSpec((1, d), lambda i: (0, 0)),
                  tok(1),
                  pl.BlockSpec((1, LANES), lambda i: (0, 0)),
                  pl.BlockSpec((d, in_w), lambda i: (0, 0))],
        out_specs=[tok(ssm_w), tok(attn_w), tok(attn_w), tok(attn_w), tok(d), tok(d),
                   pl.BlockSpec((1, nblk, attn_w), lambda i: (i, 0, 0))],
        out_shape=[jax.ShapeDtypeStruct((n_tok, ssm_w), BF16),
                   jax.ShapeDtypeStruct((n_tok, attn_w), BF16),
                   jax.ShapeDtypeStruct((n_tok, attn_w), BF16),
                   jax.ShapeDtypeStruct((n_tok, attn_w), BF16),
                   jax.ShapeDtypeStruct((n_tok, d), BF16),
                   jax.ShapeDtypeStruct((n_tok, d), BF16),
                   jax.ShapeDtypeStruct((nt, nblk, attn_w), F32)],
        compiler_params=_cparams(("arbitrary",)),
        name="inproj",
    )(x2, sc, sh, g, pos, invf, w_in.astype(BF16))
    return outs


def _s5_kernel(x_ref, zr_ref, zi_ref, lr_ref, li_ref, ca_ref, cb_ref, ba_ref, bb_ref,
               y_ref, t_ref, *, chunk, n_chunks):
    gs = SSM_GROUP_SIZE
    zr = zr_ref[0]
    zi = zi_ref[0]

    def powers(tau):
        mag = jnp.exp(tau * zr)
        return mag * jnp.cos(tau * zi), mag * jnp.sin(tau * zi)

    tau = lax.broadcasted_iota(I32, (chunk + 8, LANES), 0).astype(F32)
    e_re, e_im = powers(tau)
    lb_re, lb_im = e_re[1:2], e_im[1:2]
    lr, li = lr_ref[0], li_ref[0]
    den = lr * lr + li * li
    a, b = lb_re - 1.0, lb_im
    cf_re = (a * lr + b * li) / den
    cf_im = (b * lr - a * li) / den
    ba, bb = ba_ref[0], bb_ref[0]
    bri = cf_re * ba + cf_im * bb
    bri_sw = cf_re * bb - cf_im * ba

    ca, cb = ca_ref[0], cb_ref[0]
    cpow = (e_re[:chunk + 1, None, :] * ca[None] + e_im[:chunk + 1, None, :] * cb[None])
    cpow = cpow.reshape((chunk + 1) * gs, LANES)
    width = chunk * gs
    r = _dot_t(bri, cpow[:width], precision=HIGHEST)
    col = lax.broadcasted_iota(I32, (gs, width), 1)
    t_ref[0:gs, :] = r.astype(BF16)
    for j in range(1, chunk):
        shifted = jnp.where(col >= gs * j, pltpu.roll(r, gs * j, axis=1), 0.0)
        t_ref[gs * j:gs * (j + 1), :] = shifted.astype(BF16)

    x = x_ref[0]
    y = _dot(x, t_ref[...])

    tau_rev = (chunk - 1) - lax.broadcasted_iota(I32, (chunk, LANES), 0)
    r_re, r_im = powers(tau_rev.astype(F32))
    bst = r_re[:, None, :] * bri[None] + r_im[:, None, :] * bri_sw[None]
    bst = bst.reshape(width, LANES).astype(BF16)
    s = _dot(x, bst)

    rows = s.shape[0]
    n_idx = lax.broadcasted_iota(I32, (rows, LANES), 0) % n_chunks
    lane = lax.broadcasted_iota(I32, (1, LANES), 1)
    half = LANES // 2

    def cmul(v, p_re, p_im):
        return v * p_re + pltpu.roll(v, half, axis=1) * jnp.where(lane < half, -p_im, p_im)

    sh = 1
    while sh < n_chunks:
        p_re, p_im = powers(jnp.full((1, LANES), float(chunk * sh), F32))
        prev = jnp.where(n_idx >= sh, pltpu.roll(s, sh, axis=0), 0.0)
        s = s + cmul(prev, p_re, p_im)
        sh *= 2
    s_in = jnp.where(n_idx >= 1, pltpu.roll(s, 1, axis=0), 0.0)
    y = y + _dot_t(s_in.astype(BF16), cpow[gs:gs + width].astype(BF16))
    y_ref[0] = y


def _s5(u, lam_re, lam_im, log_dt, b_re, b_im, c_re, c_im, bsz, seq):
    g_n, gs, p = SSM_GROUPS, SSM_GROUP_SIZE, SSM_STATE
    chunk = SSM_CHUNK
    n_chunks = seq // chunk
    rows = bsz * n_chunks
    width = chunk * gs
    xg = u.reshape(rows, chunk, g_n, gs).transpose(2, 0, 1, 3).reshape(g_n, rows, width)
    dt = jnp.exp(log_dt.astype(F32))[:, None]
    dup = lambda a: jnp.concatenate([a, a], axis=-1).reshape(g_n, 1, 2 * p)
    zr, zi = dup(lam_re * dt), dup(lam_im * dt)
    lr, li = dup(lam_re), dup(lam_im)
    bt_re, bt_im = b_re.transpose(0, 2, 1), b_im.transpose(0, 2, 1)
    ba = jnp.concatenate([bt_re, bt_im], axis=-1)
    bb = jnp.concatenate([-bt_im, bt_re], axis=-1)
    ca = jnp.concatenate([c_re, -c_im], axis=-1)
    cb = jnp.concatenate([-c_im, -c_re], axis=-1)
    vec = pl.BlockSpec((1, 1, 2 * p), lambda g: (g, 0, 0))
    mat = pl.BlockSpec((1, gs, 2 * p), lambda g: (g, 0, 0))
    y = pl.pallas_call(
        functools.partial(_s5_kernel, chunk=chunk, n_chunks=n_chunks),
        grid=(g_n,),
        in_specs=[pl.BlockSpec((1, rows, width), lambda g: (g, 0, 0)),
                  vec, vec, vec, vec, mat, mat, mat, mat],
        out_specs=pl.BlockSpec((1, rows, width), lambda g: (g, 0, 0)),
        out_shape=jax.ShapeDtypeStruct((g_n, rows, width), F32),
        scratch_shapes=[pltpu.VMEM((width, width), BF16)],
        compiler_params=_cparams(("arbitrary",)),
        name="s5",
    )(xg, zr, zi, lr, li, ca, cb, ba, bb)
    return y.reshape(g_n, rows, chunk, gs).transpose(1, 2, 0, 3).reshape(bsz * seq, g_n * gs)


def _moba_kernel(q_ref, k_ref, v_ref, km_ref, o_ref):
    blk = MOBA_BLOCK
    qb = pl.program_id(2)
    q2 = q_ref[...]
    km = km_ref[...].astype(BF16)
    lane = lax.broadcasted_iota(I32, (1, LANES), 1)
    lane_b = lax.broadcasted_iota(I32, (blk, LANES), 1)
    row = lax.broadcasted_iota(I32, (blk, blk), 0)
    colk = lax.broadcasted_iota(I32, (blk, blk), 1)
    k_own = k_ref[pl.ds(pl.multiple_of(qb * blk, blk), blk), :]
    v_own = v_ref[pl.ds(pl.multiple_of(qb * blk, blk), blk), :]
    outs = []
    for h in range(2):
        head_lanes = (lane >= h * HEAD_DIM) & (lane < (h + 1) * HEAD_DIM)
        qh = jnp.where(head_lanes, q2, jnp.zeros_like(q2))
        gate = _dot_t(qh, km)
        valid = lane_b < qb
        g = jnp.where(valid, gate, NEG_INF)
        sel = jnp.zeros((blk, LANES), jnp.bool_)
        for _ in range(MOBA_TOPK):
            m = jnp.max(g, axis=1, keepdims=True)
            idx = jnp.min(jnp.where(g == m, lane_b, LANES), axis=1, keepdims=True)
            pick = lane_b == idx
            sel = sel | pick
            g = jnp.where(pick, -jnp.inf, g)
        bias = jnp.where(sel & valid, 0.0, NEG_INF).astype(BF16)
        q_aug = jnp.concatenate([qh, bias], axis=1)

        def body(kb, carry, q_aug=q_aug):
            m_i, l_i, acc = carry
            start = pl.multiple_of(kb * blk, blk)
            k_blk = k_ref[pl.ds(start, blk), :]
            v_blk = v_ref[pl.ds(start, blk), :]
            onehot = jnp.where(lane_b == kb, 1.0, 0.0).astype(BF16)
            s = _dot_t(q_aug, jnp.concatenate([k_blk, onehot], axis=1))
            m_new = jnp.maximum(m_i, jnp.max(s, axis=1, keepdims=True))
            alpha = jnp.exp(m_i - m_new)
            p = jnp.exp(s - m_new)
            l_new = alpha * l_i + jnp.sum(p, axis=1, keepdims=True)
            acc = alpha * acc + _dot(p.astype(BF16), v_blk)
            return m_new, l_new, acc

        init = (jnp.full((blk, 1), -jnp.inf, F32), jnp.zeros((blk, 1), F32), jnp.zeros((blk, LANES), F32))
        m_i, l_i, acc = lax.fori_loop(0, qb, body, init)
        s = jnp.where(colk <= row, _dot_t(qh, k_own), NEG_INF)
        m_new = jnp.maximum(m_i, jnp.max(s, axis=1, keepdims=True))
        alpha = jnp.exp(m_i - m_new)
        p = jnp.exp(s - m_new)
        l_new = alpha * l_i + jnp.sum(p, axis=1, keepdims=True)
        acc = alpha * acc + _dot(p.astype(BF16), v_own)
        outs.append(acc / l_new)
    o_ref[...] = jnp.where(lane < HEAD_DIM, outs[0], outs[1]).astype(BF16)


def _moba(q, k, v, km, bsz, seq):
    n_tok, attn_w = q.shape
    blk = MOBA_BLOCK
    nb = seq // blk
    hp = attn_w // LANES
    km_pad = jnp.zeros((bsz, LANES, attn_w), F32).at[:, :nb].set(km.reshape(bsz, nb, attn_w))
    k3 = k.reshape(bsz, seq, attn_w)
    v3 = v.reshape(bsz, seq, attn_w)
    kv_spec = pl.BlockSpec((None, seq, LANES), lambda b, p, i: (b, 0, p))
    return pl.pallas_call(
        _moba_kernel,
        grid=(bsz, hp, nb),
        in_specs=[pl.BlockSpec((blk, LANES), lambda b, p, i: (b * nb + i, p)),
                  kv_spec, kv_spec,
                  pl.BlockSpec((None, LANES, LANES), lambda b, p, i: (b, 0, p))],
        out_specs=pl.BlockSpec((blk, LANES), lambda b, p, i: (b * nb + i, p)),
        out_shape=jax.ShapeDtypeStruct((n_tok, attn_w), BF16),
        compiler_params=_cparams(("arbitrary", "arbitrary", "arbitrary")),
        name="moba",
    )(q, k3, v3, km_pad)


def _rms(t, g):
    return t * lax.rsqrt(jnp.mean(t * t, axis=-1, keepdims=True) + NORM_EPS) * g


def _merge_kernel(ys_ref, u_ref, ya_ref, gs_ref, ga_ref, x_ref, d_ref, wglu_ref, wsb_ref, wab_ref,
                  wout_ref, postg_ref, gtm_ref, preg_ref, scf_ref, shf_ref, rw_ref, rb_ref,
                  x1_ref, h2_ref, eid_ref, tw_ref, rank_ref, cnt_ref, run_ref):
    @pl.when(pl.program_id(0) == 0)
    def _():
        run_ref[...] = jnp.zeros_like(run_ref)

    y = ys_ref[...] + d_ref[...] * u_ref[...].astype(F32)
    y = jax.nn.gelu(y)
    y = y * jax.nn.sigmoid(_dot(y.astype(BF16), wglu_ref[...]))
    bs = _dot(y.astype(BF16), wsb_ref[...])
    ba = _dot(ya_ref[...], wab_ref[...])
    merged = gs_ref[...].astype(F32) * bs + ga_ref[...].astype(F32) * ba
    mix = _dot(merged.astype(BF16), wout_ref[...])
    x1 = x_ref[...] + gtm_ref[0] * _rms(mix, postg_ref[...])
    x1_ref[...] = x1
    h2 = _rms(x1, preg_ref[...]) * (1.0 + scf_ref[0]) + shf_ref[0]
    h2_ref[...] = h2

    logits = _dot(h2, rw_ref[...], precision=HIGHEST) + rb_ref[...]
    tm = logits.shape[0]
    lane = lax.broadcasted_iota(I32, (tm, LANES), 1)
    g = logits
    vals, picks, idxs = [], [], []
    for _ in range(TOP_K):
        m = jnp.max(g, axis=1, keepdims=True)
        idx = jnp.min(jnp.where(g == m, lane, LANES), axis=1, keepdims=True)
        pick = lane == idx
        vals.append(m)
        idxs.append(idx)
        picks.append(pick)
        g = jnp.where(pick, -jnp.inf, g)
    exps = [jnp.exp(v - vals[0]) for v in vals]
    tot = exps[0] + exps[1] + exps[2] + exps[3]

    onehot = jnp.where(picks[0] | picks[1] | picks[2] | picks[3], 1.0, 0.0)
    r_i = lax.broadcasted_iota(I32, (tm, tm), 0)
    c_i = lax.broadcasted_iota(I32, (tm, tm), 1)
    tri = jnp.where(c_i < r_i, 1.0, 0.0).astype(BF16)
    rank_full = _dot(tri, onehot.astype(BF16)) + run_ref[...]
    run_ref[...] = run_ref[...] + jnp.sum(onehot, axis=0, keepdims=True)
    cnt_ref[...] = run_ref[...]

    eid = jnp.zeros((tm, LANES), I32)
    tw = jnp.zeros((tm, LANES), F32)
    rk = jnp.zeros((tm, LANES), F32)
    for r in range(TOP_K):
        eid = jnp.where(lane == r, idxs[r], eid)
        tw = jnp.where(lane == r, exps[r] / tot, tw)
        pos = jnp.sum(jnp.where(picks[r], rank_full, 0.0), axis=1, keepdims=True)
        rk = jnp.where(lane == r, pos, rk)
    eid_ref[...] = eid
    tw_ref[...] = tw
    rank_ref[...] = rk.astype(I32)


def _merge(ys, u, ya, gs, ga, x2, ssm_d, w_glu, w_sb, w_ab, w_out, post_g, gt_m, pre_g, sc_f, sh_f,
           router_w, router_b, seq, tm):
    n_tok, d = x2.shape
    sw = ys.shape[1]
    aw = ya.shape[1]
    ne = router_w.shape[1]
    per_b = seq // tm
    rw = jnp.zeros((d, LANES), F32).at[:, :ne].set(router_w)
    rb = jnp.full((1, LANES), NEG_INF, F32).at[0, :ne].set(router_b)
    tok = lambda w: pl.BlockSpec((tm, w), lambda i: (i, 0))
    full = lambda a, b: pl.BlockSpec((a, b), lambda i: (0, 0))
    bvec = pl.BlockSpec((1, 1, d), lambda i: (i // per_b, 0, 0))
    return pl.pallas_call(
        _merge_kernel,
        grid=(n_tok // tm,),
        in_specs=[tok(sw), tok(sw), tok(aw), tok(d), tok(d), tok(d),
                  full(1, sw), full(sw, sw), full(sw, d), full(aw, d), full(d, d),
                  full(1, d), bvec, full(1, d), bvec, bvec, full(d, LANES), full(1, LANES)],
        out_specs=[tok(d), tok(d), tok(LANES), tok(LANES), tok(LANES), full(1, LANES)],
        out_shape=[jax.ShapeDtypeStruct((n_tok, d), F32),
                   jax.ShapeDtypeStruct((n_tok, d), F32),
                   jax.ShapeDtypeStruct((n_tok, LANES), I32),
                   jax.ShapeDtypeStruct((n_tok, LANES), F32),
                   jax.ShapeDtypeStruct((n_tok, LANES), I32),
                   jax.ShapeDtypeStruct((1, LANES), F32)],
        scratch_shapes=[pltpu.VMEM((1, LANES), F32)],
        compiler_params=_cparams(("arbitrary",)),
        name="merge",
    )(ys, u, ya, gs, ga, x2, ssm_d.reshape(1, sw), w_glu.astype(BF16), w_sb.astype(BF16),
      w_ab.astype(BF16), w_out.astype(BF16), post_g.reshape(1, d), gt_m, pre_g.reshape(1, d),
      sc_f, sh_f, rw, rb)


def _gather_kernel(idx_ref, src_ref, o_ref, sem, *, rows):
    def issue(r, _):
        pltpu.make_async_copy(src_ref.at[idx_ref[0, 0, r]], o_ref.at[r], sem).start()
        return 0

    lax.fori_loop(0, rows, issue, 0)

    def drain(r, _):
        pltpu.make_async_copy(src_ref.at[0], o_ref.at[r], sem).wait()
        return 0

    lax.fori_loop(0, rows, drain, 0)


def _gather_rows(src3, idx, rows):
    m = idx.shape[0]
    _, s, l = src3.shape
    return pl.pallas_call(
        functools.partial(_gather_kernel, rows=rows),
        grid=(m // rows,),
        in_specs=[pl.BlockSpec((1, 1, rows), lambda i: (i, 0, 0), memory_space=pltpu.SMEM),
                  pl.BlockSpec(memory_space=pl.ANY)],
        out_specs=pl.BlockSpec((rows, s, l), lambda i: (i, 0, 0)),
        out_shape=jax.ShapeDtypeStruct((m, s, l), src3.dtype),
        scratch_shapes=[pltpu.SemaphoreType.DMA(())],
        compiler_params=_cparams(("arbitrary",)),
        name="gather",
    )(idx.reshape(m // rows, 1, rows), src3)


def _expert_kernel(be_ref, nv_ref, x_ref, rw_ref, wg_ref, bg_ref, wu_ref, bu_ref, wd_ref, bd_ref,
                   o_ref, wg_s, wu_s, wd_s):
    i = pl.program_id(0)
    prev = be_ref[jnp.maximum(i - 1, 0)]
    changed = (i == 0) | (be_ref[i] != prev)

    @pl.when(changed)
    def _():
        wg_s[...] = wg_ref[0].astype(BF16)
        wu_s[...] = wu_ref[0].astype(BF16)
        wd_s[...] = wd_ref[0].astype(BF16)

    @pl.when(i < nv_ref[0])
    def _():
        xb = x_ref[...].astype(BF16)
        g = _dot(xb, wg_s[...]) + bg_ref[0]
        u = _dot(xb, wu_s[...]) + bu_ref[0]
        g = jnp.minimum(g, SWIGLU_LIMIT)
        u = jnp.clip(u, -SWIGLU_LIMIT, SWIGLU_LIMIT)
        act = g * jax.nn.sigmoid(SWIGLU_ALPHA * g) * (u + 1.0)
        y = _dot(act.astype(BF16), wd_s[...]) + bd_ref[0]
        o_ref[...] = y * rw_ref[...]

    @pl.when(i >= nv_ref[0])
    def _():
        o_ref[...] = jnp.zeros_like(o_ref)


def _experts(xs, row_w, blk_exp, n_valid, w_gate, b_gate, w_up, b_up, w_down, b_down, bm):
    n_rows, d = xs.shape
    ne, _, f = w_gate.shape
    wspec = lambda a, b: pl.BlockSpec((1, a, b), lambda i, be, nv: (be[i], 0, 0))
    grid_spec = pltpu.PrefetchScalarGridSpec(
        num_scalar_prefetch=2,
        grid=(n_rows // bm,),
        in_specs=[pl.BlockSpec((bm, d), lambda i, be, nv: (i, 0)),
                  pl.BlockSpec((bm, 1), lambda i, be, nv: (i, 0)),
                  wspec(d, f), wspec(1, f), wspec(d, f), wspec(1, f), wspec(f, d), wspec(1, d)],
        out_specs=pl.BlockSpec((bm, d), lambda i, be, nv: (i, 0)),
        scratch_shapes=[pltpu.VMEM((d, f), BF16), pltpu.VMEM((d, f), BF16), pltpu.VMEM((f, d), BF16)],
    )
    return pl.pallas_call(
        _expert_kernel,
        grid_spec=grid_spec,
        out_shape=jax.ShapeDtypeStruct((n_rows, d), F32),
        compiler_params=_cparams(("arbitrary",)),
        name="experts",
    )(blk_exp, n_valid, xs, row_w, w_gate, b_gate.reshape(ne, 1, f), w_up, b_up.reshape(ne, 1, f),
      w_down, b_down.reshape(ne, 1, d))


def _combine_kernel(idx_ref, yb_ref, x1_ref, g_ref, gt_ref, o_ref, buf, sem, *, tm):
    n = tm * TOP_K

    def issue(r, _):
        pltpu.make_async_copy(yb_ref.at[idx_ref[0, 0, r]], buf.at[r], sem).start()
        return 0

    lax.fori_loop(0, n, issue, 0)

    def drain(r, _):
        pltpu.make_async_copy(yb_ref.at[0], buf.at[r], sem).wait()
        return 0

    lax.fori_loop(0, n, drain, 0)
    ff = buf[0:tm] + buf[tm:2 * tm] + buf[2 * tm:3 * tm] + buf[3 * tm:4 * tm]
    feat = ff.shape[1] * ff.shape[2]
    ss = jnp.sum(jnp.sum(ff * ff, axis=2, keepdims=True), axis=1, keepdims=True)
    nrm = ff * lax.rsqrt(ss / feat + NORM_EPS) * g_ref[...]
    o_ref[...] = x1_ref[...] + gt_ref[...] * nrm


def _combine(yb3, dest_t, x1_3, post_g3, gt_f3, seq, tm):
    n_tok, s, l = x1_3.shape
    per_b = seq // tm
    return pl.pallas_call(
        functools.partial(_combine_kernel, tm=tm),
        grid=(n_tok // tm,),
        in_specs=[pl.BlockSpec((1, 1, TOP_K * tm), lambda i: (i, 0, 0), memory_space=pltpu.SMEM),
                  pl.BlockSpec(memory_space=pl.ANY),
                  pl.BlockSpec((tm, s, l), lambda i: (i, 0, 0)),
                  pl.BlockSpec((1, s, l), lambda i: (0, 0, 0)),
                  pl.BlockSpec((1, s, l), lambda i: (i // per_b, 0, 0))],
        out_specs=pl.BlockSpec((tm, s, l), lambda i: (i, 0, 0)),
        out_shape=jax.ShapeDtypeStruct((n_tok, s, l), F32),
        scratch_shapes=[pltpu.VMEM((TOP_K * tm, s, l), F32), pltpu.SemaphoreType.DMA(())],
        compiler_params=_cparams(("arbitrary",)),
        name="combine",
    )(dest_t, yb3, x1_3, post_g3, gt_f3)


def kernel(x, c, positions, ada_w, ada_b, mix_pre_g, mix_post_g, ffn_pre_g, ffn_post_g, w_in, ssm_lam_re, ssm_lam_im, ssm_log_dt, ssm_b_re, ssm_b_im, ssm_c_re, ssm_c_im, ssm_d, ssm_w_glu, w_ssm_branch, w_attn_branch, w_out, router_w, router_b, w_gate, b_gate, w_up, b_up, w_down, b_down):
    bsz, seq, d = x.shape
    depth = ada_w.shape[0]
    n_tok = bsz * seq
    bm = EXPERT_ROWS
    sub = d // LANES
    xcur = x.reshape(n_tok, d)
    pos = positions.reshape(n_tok, 1).astype(I32)
    for l in range(depth):
        ada = _ada(c, ada_w[l], ada_b[l])
        sh_m, sc_m, gt_m, sh_f, sc_f, gt_f = [a.reshape(bsz, 1, d) for a in jnp.split(ada, 6, axis=-1)]

        u, q, k, v, gs, ga, km = _inproj(xcur, sc_m, sh_m, mix_pre_g[l].reshape(1, d), pos, w_in[l],
                                         seq, tm=min(512, seq))
        ys = _s5(u, ssm_lam_re[l], ssm_lam_im[l], ssm_log_dt[l], ssm_b_re[l], ssm_b_im[l],
                 ssm_c_re[l], ssm_c_im[l], bsz, seq)
        ya = _moba(q, k, v, km, bsz, seq)
        x1, h2, eid, tw, rank, cnt = _merge(ys, u, ya, gs, ga, xcur, ssm_d[l], ssm_w_glu[l], w_ssm_branch[l],
                                            w_attn_branch[l], w_out[l], mix_post_g[l], gt_m, ffn_pre_g[l],
                                            sc_f, sh_f, router_w[l], router_b[l], seq, tm=256)

        ne = router_w.shape[-1]
        counts = cnt[0, :ne].astype(I32)
        padded = (counts + bm - 1) // bm * bm
        p_ends = jnp.cumsum(padded)
        p_starts = p_ends - padded
        eid4 = eid[:, :TOP_K]
        dest = p_starts[eid4] + rank[:, :TOP_K]
        n_blocks = (n_tok * TOP_K) // bm + ne
        n_rows = n_blocks * bm
        tok_ids = jnp.broadcast_to(jnp.arange(n_tok, dtype=I32)[:, None], (n_tok, TOP_K))
        row_tok = jnp.zeros((n_rows,), I32).at[dest.reshape(-1)].set(tok_ids.reshape(-1))
        row_w = jnp.zeros((n_rows,), F32).at[dest.reshape(-1)].set(tw[:, :TOP_K].reshape(-1))
        blk_exp = jnp.minimum(jnp.searchsorted(p_ends, jnp.arange(n_blocks, dtype=I32) * bm, side='right'),
                              ne - 1).astype(I32)
        n_valid = (p_ends[-1] // bm).astype(I32).reshape(1)

        xs = _gather_rows(h2.reshape(n_tok, sub, LANES), row_tok, rows=bm).reshape(n_rows, d)
        yb = _experts(xs, row_w.reshape(n_rows, 1), blk_exp, n_valid, w_gate[l], b_gate[l], w_up[l], b_up[l],
                      w_down[l], b_down[l], bm)
        tmc = 128
        dest_t = dest.reshape(n_tok // tmc, tmc, TOP_K).transpose(0, 2, 1).reshape(n_tok // tmc, 1, TOP_K * tmc)
        out3 = _combine(yb.reshape(n_rows, sub, LANES), dest_t, x1.reshape(n_tok, sub, LANES),
                        ffn_post_g[l].reshape(1, sub, LANES), gt_f.reshape(bsz, sub, LANES), seq, tmc)
        xcur = out3.reshape(n_tok, d)
    return xcur.reshape(bsz, seq, d).astype(x.dtype)
```

```python
import functools
import math

import jax
import jax.numpy as jnp
from jax import lax
from jax.experimental import pallas as pl
from jax.experimental.pallas import tpu as pltpu

F32 = jnp.float32
BF16 = jnp.bfloat16
I32 = jnp.int32

N_HEADS = 8
HEAD_DIM = 64
ROPE_THETA = 10000.0
MOBA_BLOCK = 256
MOBA_TOPK = 3
SSM_GROUP_SIZE = 16
SSM_GROUPS = 32
SSM_STATE = 64
N_EXPERTS = 32
TOP_K = 4
SWIGLU_ALPHA = 1.702
SWIGLU_LIMIT = 7.0
NORM_EPS = 1e-6
NEG_INF = -1e30

LANES = 128
SSM_CHUNK = 64
EXPERT_ROWS = 256
VMEM_LIMIT = 56 * 1024 * 1024
HIGHEST = lax.Precision.HIGHEST
Q_SCALE = HEAD_DIM ** -0.5 * math.log2(math.e)


def _cparams(sem):
    return pltpu.CompilerParams(dimension_semantics=sem, vmem_limit_bytes=VMEM_LIMIT)


def _dot(a, b, **kw):
    return jnp.dot(a, b, preferred_element_type=F32, **kw)


def _dot_t(a, b, **kw):
    return lax.dot_general(a, b, (((1,), (1,)), ((), ())), preferred_element_type=F32, **kw)


def _ada_kernel(c_ref, w_ref, b_ref, o_ref):
    c = c_ref[...]
    cond = c * jax.nn.sigmoid(c)
    o_ref[...] = _dot(cond, w_ref[...], precision=HIGHEST) + b_ref[...]


def _ada(c, ada_w, ada_b):
    bsz, d = c.shape
    n = ada_w.shape[1]
    c8 = jnp.zeros((8, d), F32).at[:bsz].set(c)
    out = pl.pallas_call(
        _ada_kernel,
        grid=(n // d,),
        in_specs=[pl.BlockSpec((8, d), lambda j: (0, 0)),
                  pl.BlockSpec((d, d), lambda j: (0, j)),
                  pl.BlockSpec((1, d), lambda j: (0, j))],
        out_specs=pl.BlockSpec((8, d), lambda j: (0, j)),
        out_shape=jax.ShapeDtypeStruct((8, n), F32),
        compiler_params=_cparams(("arbitrary",)),
        name="ada",
    )(c8, ada_w, ada_b.reshape(1, n))
    return out[:bsz]


def _inproj_kernel(x_ref, sc_ref, sh_ref, g_ref, pos_ref, invf_ref, w_ref,
                   u_ref, q_ref, k_ref, v_ref, gs_ref, ga_ref, km_ref, *, ssm_w, attn_w, d_model):
    x = x_ref[...]
    ms = jnp.mean(x * x, axis=-1, keepdims=True)
    xn = x * lax.rsqrt(ms + NORM_EPS) * g_ref[...]
    h = (xn * (1.0 + sc_ref[0]) + sh_ref[0]).astype(BF16)

    def proj(lo, width):
        return _dot(h, w_ref[:, lo:lo + width])

    u_ref[...] = proj(0, ssm_w).astype(BF16)

    ang = pos_ref[...].astype(F32) * invf_ref[...]
    reps = attn_w // LANES
    cos = jnp.concatenate([jnp.cos(ang)] * reps, axis=1)
    sin = jnp.concatenate([jnp.sin(ang)] * reps, axis=1)
    lane = lax.broadcasted_iota(I32, (1, attn_w), 1)
    first = (lane % HEAD_DIM) < (HEAD_DIM // 2)
    sin = jnp.where(first, -sin, sin)

    def rope(t):
        rot = jnp.where(first, pltpu.roll(t, attn_w - HEAD_DIM // 2, axis=1),
                        pltpu.roll(t, HEAD_DIM // 2, axis=1))
        return t * cos + rot * sin

    q = rope(proj(ssm_w, attn_w))
    q_ref[...] = (q * Q_SCALE).astype(BF16)
    k = rope(proj(ssm_w + attn_w, attn_w))
    k_ref[...] = k.astype(BF16)
    nblk = k.shape[0] // MOBA_BLOCK
    km_ref[0] = jnp.mean(k.reshape(nblk, MOBA_BLOCK, attn_w), axis=1)
    v_ref[...] = proj(ssm_w + 2 * attn_w, attn_w).astype(BF16)
    gs_ref[...] = jax.nn.sigmoid(proj(ssm_w + 3 * attn_w, d_model)).astype(BF16)
    ga_ref[...] = jax.nn.sigmoid(proj(ssm_w + 3 * attn_w + d_model, d_model)).astype(BF16)


def _inproj(x2, sc, sh, g, pos, w_in, seq, tm):
    n_tok, d = x2.shape
    attn_w = N_HEADS * HEAD_DIM
    ssm_w = SSM_GROUPS * SSM_GROUP_SIZE
    in_w = w_in.shape[1]
    half = HEAD_DIM // 2
    inv_freq = ROPE_THETA ** (-jnp.arange(half, dtype=F32) / half)
    invf = jnp.tile(inv_freq, LANES // half).reshape(1, LANES)
    nt = n_tok // tm
    per_b = seq // tm
    nblk = tm // MOBA_BLOCK
    tok = lambda w: pl.BlockSpec((tm, w), lambda i: (i, 0))
    bvec = pl.BlockSpec((1, 1, d), lambda i: (i // per_b, 0, 0))
    outs = pl.pallas_call(
        functools.partial(_inproj_kernel, ssm_w=ssm_w, attn_w=attn_w, d_model=d),
        grid=(nt,),
        in_specs=[tok(d), bvec, bvec,
                  pl.BlockSpec((1, d), lambda i: (0, 0)),
                  tok(1),
                  pl.BlockSpec((1, LANES), lambda i: (0, 0)),
                  pl.BlockSpec((d, in_w), lambda i: (0, 0))],
        out_specs=[tok(ssm_w), tok(attn_w), tok(attn_w), tok(attn_w), tok(d), tok(d),
                   pl.BlockSpec((1, nblk, attn_w), lambda i: (i, 0, 0))],
        out_shape=[jax.ShapeDtypeStruct((n_tok, ssm_w), BF16),
                   jax.ShapeDtypeStruct((n_tok, attn_w), BF16),
                   jax.ShapeDtypeStruct((n_tok, attn_w), BF16),
                   jax.ShapeDtypeStruct((n_tok, attn_w), BF16),
                   jax.ShapeDtypeStruct((n_tok, d), BF16),
                   jax.ShapeDtypeStruct((n_tok, d), BF16),
                   jax.ShapeDtypeStruct((nt, nblk, attn_w), F32)],
        compiler_params=_cparams(("arbitrary",)),
        name="inproj",
    )(x2, sc, sh, g, pos, invf, w_in.astype(BF16))
    return outs


def _s5_kernel(x_ref, zr_ref, zi_ref, lr_ref, li_ref, ca_ref, cb_ref, ba_ref, bb_ref,
               y_ref, t_ref, *, chunk, n_chunks):
    gs = SSM_GROUP_SIZE
    zr = zr_ref[0]
    zi = zi_ref[0]

    def powers(tau):
        mag = jnp.exp(tau * zr)
        return mag * jnp.cos(tau * zi), mag * jnp.sin(tau * zi)

    tau = lax.broadcasted_iota(I32, (chunk + 8, LANES), 0).astype(F32)
    e_re, e_im = powers(tau)
    lb_re, lb_im = e_re[1:2], e_im[1:2]
    lr, li = lr_ref[0], li_ref[0]
    den = lr * lr + li * li
    a, b = lb_re - 1.0, lb_im
    cf_re = (a * lr + b * li) / den
    cf_im = (b * lr - a * li) / den
    ba, bb = ba_ref[0], bb_ref[0]
    bri = cf_re * ba + cf_im * bb
    bri_sw = cf_re * bb - cf_im * ba

    ca, cb = ca_ref[0], cb_ref[0]
    cpow = (e_re[:chunk + 1, None, :] * ca[None] + e_im[:chunk + 1, None, :] * cb[None])
    cpow = cpow.reshape((chunk + 1) * gs, LANES)
    width = chunk * gs
    r = _dot_t(bri, cpow[:width], precision=HIGHEST)
    col = lax.broadcasted_iota(I32, (gs, width), 1)
    t_ref[0:gs, :] = r.astype(BF16)
    for j in range(1, chunk):
        shifted = jnp.where(col >= gs * j, pltpu.roll(r, gs * j, axis=1), 0.0)
        t_ref[gs * j:gs * (j + 1), :] = shifted.astype(BF16)

    x = x_ref[0]
    y = _dot(x, t_ref[...])

    tau_rev = (chunk - 1) - lax.broadcasted_iota(I32, (chunk, LANES), 0)
    r_re, r_im = powers(tau_rev.astype(F32))
    bst = r_re[:, None, :] * bri[None] + r_im[:, None, :] * bri_sw[None]
    bst = bst.reshape(width, LANES).astype(BF16)
    s = _dot(x, bst)

    rows = s.shape[0]
    n_idx = lax.broadcasted_iota(I32, (rows, LANES), 0) % n_chunks
    lane = lax.broadcasted_iota(I32, (1, LANES), 1)
    half = LANES // 2

    def cmul(v, p_re, p_im):
        return v * p_re + pltpu.roll(v, half, axis=1) * jnp.where(lane < half, -p_im, p_im)

    sh = 1
    while sh < n_chunks:
        p_re, p_im = powers(jnp.full((1, LANES), float(chunk * sh), F32))
        prev = jnp.where(n_idx >= sh, pltpu.roll(s, sh, axis=0), 0.0)
        s = s + cmul(prev, p_re, p_im)
        sh *= 2
    s_in = jnp.where(n_idx >= 1, pltpu.roll(s, 1, axis=0), 0.0)
    y = y + _dot_t(s_in.astype(BF16), cpow[gs:gs + width].astype(BF16))
    y_ref[0] = y


def _s5(u, lam_re, lam_im, log_dt, b_re, b_im, c_re, c_im, bsz, seq):
    g_n, gs, p = SSM_GROUPS, SSM_GROUP_SIZE, SSM_STATE
    chunk = SSM_CHUNK
    n_chunks = seq // chunk
    rows = bsz * n_chunks
    width = chunk * gs
    xg = u.reshape(rows, chunk, g_n, gs).transpose(2, 0, 1, 3).reshape(g_n, rows, width)
    dt = jnp.exp(log_dt.astype(F32))[:, None]
    dup = lambda a: jnp.concatenate([a, a], axis=-1).reshape(g_n, 1, 2 * p)
    zr, zi = dup(lam_re * dt), dup(lam_im * dt)
    lr, li = dup(lam_re), dup(lam_im)
    bt_re, bt_im = b_re.transpose(0, 2, 1), b_im.transpose(0, 2, 1)
    ba = jnp.concatenate([bt_re, bt_im], axis=-1)
    bb = jnp.concatenate([-bt_im, bt_re], axis=-1)
    ca = jnp.concatenate([c_re, -c_im], axis=-1)
    cb = jnp.concatenate([-c_im, -c_re], axis=-1)
    vec = pl.BlockSpec((1, 1, 2 * p), lambda g: (g, 0, 0))
    mat = pl.BlockSpec((1, gs, 2 * p), lambda g: (g, 0, 0))
    y = pl.pallas_call(
        functools.partial(_s5_kernel, chunk=chunk, n_chunks=n_chunks),
        grid=(g_n,),
        in_specs=[pl.BlockSpec((1, rows, width), lambda g: (g, 0, 0)),
                  vec, vec, vec, vec, mat, mat, mat, mat],
        out_specs=pl.BlockSpec((1, rows, width), lambda g: (g, 0, 0)),
        out_shape=jax.ShapeDtypeStruct((g_n, rows, width), F32),
        scratch_shapes=[pltpu.VMEM((width, width), BF16)],
        compiler_params=_cparams(("arbitrary",)),
        name="s5",
    )(xg, zr, zi, lr, li, ca, cb, ba, bb)
    return y.reshape(g_n, rows, chunk, gs).transpose(1, 2, 0, 3).reshape(bsz * seq, g_n * gs)


def _moba_kernel(qt_ref, k_ref, vt_ref, km_ref, o_ref, *, group):
    blk = MOBA_BLOCK
    qb = pl.program_id(2)
    qt = qt_ref[...]
    km = km_ref[...]
    km_hi = km.astype(BF16)
    km_lo = (km - km_hi.astype(F32)).astype(BF16)
    rowi = lax.broadcasted_iota(I32, (LANES, blk), 0)
    key_i = lax.broadcasted_iota(I32, (blk, blk), 0)
    qry_i = lax.broadcasted_iota(I32, (blk, blk), 1)
    valid = rowi < qb

    q_aug = []
    for h in range(2):
        qh = jnp.where((rowi >= h * HEAD_DIM) & (rowi < (h + 1) * HEAD_DIM), qt, jnp.zeros_like(qt))
        g = jnp.where(valid, _dot(km_hi, qh) + _dot(km_lo, qh), NEG_INF)
        sel = jnp.zeros((LANES, blk), jnp.bool_)
        for _ in range(MOBA_TOPK):
            m = jnp.max(g, axis=0, keepdims=True)
            idx = jnp.min(jnp.where(g == m, rowi, LANES), axis=0, keepdims=True)
            pick = rowi == idx
            sel = sel | pick
            g = jnp.where(pick, -jnp.inf, g)
        bias = jnp.where(sel & valid, 0.0, NEG_INF).astype(BF16)
        q_aug.append(jnp.concatenate([qh, bias], axis=0))

    def softmax_pv(h, s, vt, state):
        m_i, l_i, acc = state
        m_new = jnp.maximum(m_i, jnp.max(s, axis=0, keepdims=True))
        alpha = jnp.exp2(m_i - m_new)
        p = jnp.exp2(s - m_new)
        l_new = alpha * l_i + jnp.sum(p, axis=0, keepdims=True)
        return m_new, l_new, alpha * acc + _dot(vt, p.astype(BF16))

    def body(i, carry):
        kb0 = pl.multiple_of(i * group, group)
        keys = group * blk
        k_cat = k_ref[pl.ds(kb0, group)].reshape(keys, LANES)
        blk_id = kb0 + lax.broadcasted_iota(I32, (keys, LANES), 0) // blk
        onehot = jnp.where(lax.broadcasted_iota(I32, (keys, LANES), 1) == blk_id, 1.0, 0.0).astype(BF16)
        k_aug = jnp.concatenate([k_cat, onehot], axis=1)
        s = [_dot(k_aug, q_aug[h]) for h in range(2)]
        out = []
        for h in range(2):
            vt = jnp.concatenate([vt_ref[kb0 + j, h * HEAD_DIM:(h + 1) * HEAD_DIM, :] for j in range(group)], axis=1)
            out.append(softmax_pv(h, s[h], vt, carry[h]))
        return tuple(out)

    init = (jnp.full((1, blk), -jnp.inf, F32), jnp.zeros((1, blk), F32), jnp.zeros((HEAD_DIM, blk), F32))
    st = lax.fori_loop(0, (qb + group - 1) // group, body, (init, init))
    k_own = jnp.concatenate([k_ref[qb], jnp.zeros((blk, LANES), BF16)], axis=1)
    s_own = [jnp.where(key_i <= qry_i, _dot(k_own, q_aug[h]), NEG_INF) for h in range(2)]
    for h in range(2):
        _, l_i, acc = softmax_pv(h, s_own[h], vt_ref[qb, h * HEAD_DIM:(h + 1) * HEAD_DIM, :], st[h])
        o_ref[h * HEAD_DIM:(h + 1) * HEAD_DIM, :] = (acc / l_i).astype(BF16)


def _moba(q, k, v, km, bsz, seq):
    n_tok, attn_w = q.shape
    blk = MOBA_BLOCK
    nb = seq // blk
    assert nb <= LANES
    hp = attn_w // LANES
    km_pad = jnp.zeros((bsz, LANES, attn_w), F32).at[:, :nb].set(km.reshape(bsz, nb, attn_w))
    qt = q.reshape(bsz, seq, attn_w).transpose(0, 2, 1)
    k4 = k.reshape(bsz, nb, blk, attn_w)
    vt4 = v.reshape(bsz, nb, blk, attn_w).transpose(0, 1, 3, 2)
    group = 4 if nb % 4 == 0 else (2 if nb % 2 == 0 else 1)
    ot = pl.pallas_call(
        functools.partial(_moba_kernel, group=group),
        grid=(bsz, hp, nb),
        in_specs=[pl.BlockSpec((None, LANES, blk), lambda b, p, i: (b, p, i)),
                  pl.BlockSpec((None, nb, blk, LANES), lambda b, p, i: (b, 0, 0, p)),
                  pl.BlockSpec((None, nb, LANES, blk), lambda b, p, i: (b, 0, p, 0)),
                  pl.BlockSpec((None, LANES, LANES), lambda b, p, i: (b, 0, p))],
        out_specs=pl.BlockSpec((None, LANES, blk), lambda b, p, i: (b, p, i)),
        out_shape=jax.ShapeDtypeStruct((bsz, attn_w, seq), BF16),
        compiler_params=_cparams(("arbitrary", "arbitrary", "arbitrary")),
        name="moba",
    )(qt, k4, vt4, km_pad)
    return ot.transpose(0, 2, 1).reshape(n_tok, attn_w)


def _rms(t, g):
    return t * lax.rsqrt(jnp.mean(t * t, axis=-1, keepdims=True) + NORM_EPS) * g


def _merge_kernel(ys_ref, u_ref, ya_ref, gs_ref, ga_ref, x_ref, d_ref, wglu_ref, wsb_ref, wab_ref,
                  wout_ref, postg_ref, gtm_ref, preg_ref, scf_ref, shf_ref, rw_ref, rb_ref,
                  x1_ref, h2_ref, eid_ref, tw_ref, rank_ref, cnt_ref, run_ref):
    @pl.when(pl.program_id(0) == 0)
    def _():
        run_ref[...] = jnp.zeros_like(run_ref)

    y = ys_ref[...] + d_ref[...] * u_ref[...].astype(F32)
    y = jax.nn.gelu(y)
    y = y * jax.nn.sigmoid(_dot(y.astype(BF16), wglu_ref[...]))
    bs = _dot(y.astype(BF16), wsb_ref[...])
    ba = _dot(ya_ref[...], wab_ref[...])
    merged = gs_ref[...].astype(F32) * bs + ga_ref[...].astype(F32) * ba
    mix = _dot(merged.astype(BF16), wout_ref[...])
    x1 = x_ref[...] + gtm_ref[0] * _rms(mix, postg_ref[...])
    x1_ref[...] = x1
    h2 = _rms(x1, preg_ref[...]) * (1.0 + scf_ref[0]) + shf_ref[0]
    h2_ref[...] = h2

    logits = _dot(h2, rw_ref[...], precision=HIGHEST) + rb_ref[...]
    tm = logits.shape[0]
    lane = lax.broadcasted_iota(I32, (tm, LANES), 1)
    g = logits
    vals, picks, idxs = [], [], []
    for _ in range(TOP_K):
        m = jnp.max(g, axis=1, keepdims=True)
        idx = jnp.min(jnp.where(g == m, lane, LANES), axis=1, keepdims=True)
        pick = lane == idx
        vals.append(m)
        idxs.append(idx)
        picks.append(pick)
        g = jnp.where(pick, -jnp.inf, g)
    exps = [jnp.exp(v - vals[0]) for v in vals]
    tot = exps[0] + exps[1] + exps[2] + exps[3]

    onehot = jnp.where(picks[0] | picks[1] | picks[2] | picks[3], 1.0, 0.0)
    r_i = lax.broadcasted_iota(I32, (tm, tm), 0)
    c_i = lax.broadcasted_iota(I32, (tm, tm), 1)
    tri = jnp.where(c_i < r_i, 1.0, 0.0).astype(BF16)
    rank_full = _dot(tri, onehot.astype(BF16)) + run_ref[...]
    run_ref[...] = run_ref[...] + jnp.sum(onehot, axis=0, keepdims=True)
    cnt_ref[...] = run_ref[...]

    eid = jnp.zeros((tm, LANES), I32)
    tw = jnp.zeros((tm, LANES), F32)
    rk = jnp.zeros((tm, LANES), F32)
    for r in range(TOP_K):
        eid = jnp.where(lane == r, idxs[r], eid)
        tw = jnp.where(lane == r, exps[r] / tot, tw)
        pos = jnp.sum(jnp.where(picks[r], rank_full, 0.0), axis=1, keepdims=True)
        rk = jnp.where(lane == r, pos, rk)
    eid_ref[...] = eid
    tw_ref[...] = tw
    rank_ref[...] = rk.astype(I32)


def _merge(ys, u, ya, gs, ga, x2, ssm_d, w_glu, w_sb, w_ab, w_out, post_g, gt_m, pre_g, sc_f, sh_f,
           router_w, router_b, seq, tm):
    n_tok, d = x2.shape
    sw = ys.shape[1]
    aw = ya.shape[1]
    ne = router_w.shape[1]
    per_b = seq // tm
    rw = jnp.zeros((d, LANES), F32).at[:, :ne].set(router_w)
    rb = jnp.full((1, LANES), NEG_INF, F32).at[0, :ne].set(router_b)
    tok = lambda w: pl.BlockSpec((tm, w), lambda i: (i, 0))
    full = lambda a, b: pl.BlockSpec((a, b), lambda i: (0, 0))
    bvec = pl.BlockSpec((1, 1, d), lambda i: (i // per_b, 0, 0))
    return pl.pallas_call(
        _merge_kernel,
        grid=(n_tok // tm,),
        in_specs=[tok(sw), tok(sw), tok(aw), tok(d), tok(d), tok(d),
                  full(1, sw), full(sw, sw), full(sw, d), full(aw, d), full(d, d),
                  full(1, d), bvec, full(1, d), bvec, bvec, full(d, LANES), full(1, LANES)],
        out_specs=[tok(d), tok(d), tok(LANES), tok(LANES), tok(LANES), full(1, LANES)],
        out_shape=[jax.ShapeDtypeStruct((n_tok, d), F32),
                   jax.ShapeDtypeStruct((n_tok, d), F32),
                   jax.ShapeDtypeStruct((n_tok, LANES), I32),
                   jax.ShapeDtypeStruct((n_tok, LANES), F32),
                   jax.ShapeDtypeStruct((n_tok, LANES), I32),
                   jax.ShapeDtypeStruct((1, LANES), F32)],
        scratch_shapes=[pltpu.VMEM((1, LANES), F32)],
        compiler_params=_cparams(("arbitrary",)),
        name="merge",
    )(ys, u, ya, gs, ga, x2, ssm_d.reshape(1, sw), w_glu.astype(BF16), w_sb.astype(BF16),
      w_ab.astype(BF16), w_out.astype(BF16), post_g.reshape(1, d), gt_m, pre_g.reshape(1, d),
      sc_f, sh_f, rw, rb)


def _gather_kernel(idx_ref, src_ref, o_ref, sem, *, rows):
    def issue(r, _):
        pltpu.make_async_copy(src_ref.at[idx_ref[0, 0, r]], o_ref.at[r], sem).start()
        return 0

    lax.fori_loop(0, rows, issue, 0)

    def drain(r, _):
        pltpu.make_async_copy(src_ref.at[0], o_ref.at[r], sem).wait()
        return 0

    lax.fori_loop(0, rows, drain, 0)


def _gather_rows(src3, idx, rows):
    m = idx.shape[0]
    _, s, l = src3.shape
    return pl.pallas_call(
        functools.partial(_gather_kernel, rows=rows),
        grid=(m // rows,),
        in_specs=[pl.BlockSpec((1, 1, rows), lambda i: (i, 0, 0), memory_space=pltpu.SMEM),
                  pl.BlockSpec(memory_space=pl.ANY)],
        out_specs=pl.BlockSpec((rows, s, l), lambda i: (i, 0, 0)),
        out_shape=jax.ShapeDtypeStruct((m, s, l), src3.dtype),
        scratch_shapes=[pltpu.SemaphoreType.DMA(())],
        compiler_params=_cparams(("arbitrary",)),
        name="gather",
    )(idx.reshape(m // rows, 1, rows), src3)


def _expert_kernel(be_ref, nv_ref, x_ref, rw_ref, wg_ref, bg_ref, wu_ref, bu_ref, wd_ref, bd_ref,
                   o_ref, wg_s, wu_s, wd_s):
    i = pl.program_id(0)
    prev = be_ref[jnp.maximum(i - 1, 0)]
    changed = (i == 0) | (be_ref[i] != prev)

    @pl.when(changed)
    def _():
        wg_s[...] = wg_ref[0].astype(BF16)
        wu_s[...] = wu_ref[0].astype(BF16)
        wd_s[...] = wd_ref[0].astype(BF16)

    @pl.when(i < nv_ref[0])
    def _():
        xb = x_ref[...].astype(BF16)
        g = _dot(xb, wg_s[...]) + bg_ref[0]
        u = _dot(xb, wu_s[...]) + bu_ref[0]
        g = jnp.minimum(g, SWIGLU_LIMIT)
        u = jnp.clip(u, -SWIGLU_LIMIT, SWIGLU_LIMIT)
        act = g * jax.nn.sigmoid(SWIGLU_ALPHA * g) * (u + 1.0)
        y = _dot(act.astype(BF16), wd_s[...]) + bd_ref[0]
        o_ref[...] = y * rw_ref[...]

    @pl.when(i >= nv_ref[0])
    def _():
        o_ref[...] = jnp.zeros_like(o_ref)


def _experts(xs, row_w, blk_exp, n_valid, w_gate, b_gate, w_up, b_up, w_down, b_down, bm):
    n_rows, d = xs.shape
    ne, _, f = w_gate.shape
    wspec = lambda a, b: pl.BlockSpec((1, a, b), lambda i, be, nv: (be[i], 0, 0))
    grid_spec = pltpu.PrefetchScalarGridSpec(
        num_scalar_prefetch=2,
        grid=(n_rows // bm,),
        in_specs=[pl.BlockSpec((bm, d), lambda i, be, nv: (i, 0)),
                  pl.BlockSpec((bm, 1), lambda i, be, nv: (i, 0)),
                  wspec(d, f), wspec(1, f), wspec(d, f), wspec(1, f), wspec(f, d), wspec(1, d)],
        out_specs=pl.BlockSpec((bm, d), lambda i, be, nv: (i, 0)),
        scratch_shapes=[pltpu.VMEM((d, f), BF16), pltpu.VMEM((d, f), BF16), pltpu.VMEM((f, d), BF16)],
    )
    return pl.pallas_call(
        _expert_kernel,
        grid_spec=grid_spec,
        out_shape=jax.ShapeDtypeStruct((n_rows, d), F32),
        compiler_params=_cparams(("arbitrary",)),
        name="experts",
    )(blk_exp, n_valid, xs, row_w, w_gate, b_gate.reshape(ne, 1, f), w_up, b_up.reshape(ne, 1, f),
      w_down, b_down.reshape(ne, 1, d))


def _combine_kernel(idx_ref, yb_ref, x1_ref, g_ref, gt_ref, o_ref, buf, sem, *, tm):
    n = tm * TOP_K

    def issue(r, _):
        pltpu.make_async_copy(yb_ref.at[idx_ref[0, 0, r]], buf.at[r], sem).start()
        return 0

    lax.fori_loop(0, n, issue, 0)

    def drain(r, _):
        pltpu.make_async_copy(yb_ref.at[0], buf.at[r], sem).wait()
        return 0

    lax.fori_loop(0, n, drain, 0)
    ff = buf[0:tm] + buf[tm:2 * tm] + buf[2 * tm:3 * tm] + buf[3 * tm:4 * tm]
    feat = ff.shape[1] * ff.shape[2]
    ss = jnp.sum(jnp.sum(ff * ff, axis=2, keepdims=True), axis=1, keepdims=True)
    nrm = ff * lax.rsqrt(ss / feat + NORM_EPS) * g_ref[...]
    o_ref[...] = x1_ref[...] + gt_ref[...] * nrm


def _combine(yb3, dest_t, x1_3, post_g3, gt_f3, seq, tm):
    n_tok, s, l = x1_3.shape
    per_b = seq // tm
    return pl.pallas_call(
        functools.partial(_combine_kernel, tm=tm),
        grid=(n_tok // tm,),
        in_specs=[pl.BlockSpec((1, 1, TOP_K * tm), lambda i: (i, 0, 0), memory_space=pltpu.SMEM),
                  pl.BlockSpec(memory_space=pl.ANY),
                  pl.BlockSpec((tm, s, l), lambda i: (i, 0, 0)),
                  pl.BlockSpec((1, s, l), lambda i: (0, 0, 0)),
                  pl.BlockSpec((1, s, l), lambda i: (i // per_b, 0, 0))],
        out_specs=pl.BlockSpec((tm, s, l), lambda i: (i, 0, 0)),
        out_shape=jax.ShapeDtypeStruct((n_tok, s, l), F32),
        scratch_shapes=[pltpu.VMEM((TOP_K * tm, s, l), F32), pltpu.SemaphoreType.DMA(())],
        compiler_params=_cparams(("arbitrary",)),
        name="combine",
    )(dest_t, yb3, x1_3, post_g3, gt_f3)


def kernel(x, c, positions, ada_w, ada_b, mix_pre_g, mix_post_g, ffn_pre_g, ffn_post_g, w_in, ssm_lam_re, ssm_lam_im, ssm_log_dt, ssm_b_re, ssm_b_im, ssm_c_re, ssm_c_im, ssm_d, ssm_w_glu, w_ssm_branch, w_attn_branch, w_out, router_w, router_b, w_gate, b_gate, w_up, b_up, w_down, b_down):
    bsz, seq, d = x.shape
    depth = ada_w.shape[0]
    n_tok = bsz * seq
    bm = EXPERT_ROWS
    sub = d // LANES
    xcur = x.reshape(n_tok, d)
    pos = positions.reshape(n_tok, 1).astype(I32)
    for l in range(depth):
        ada = _ada(c, ada_w[l], ada_b[l])
        sh_m, sc_m, gt_m, sh_f, sc_f, gt_f = [a.reshape(bsz, 1, d) for a in jnp.split(ada, 6, axis=-1)]

        u, q, k, v, gs, ga, km = _inproj(xcur, sc_m, sh_m, mix_pre_g[l].reshape(1, d), pos, w_in[l],
                                         seq, tm=min(512, seq))
        ys = _s5(u, ssm_lam_re[l], ssm_lam_im[l], ssm_log_dt[l], ssm_b_re[l], ssm_b_im[l],
                 ssm_c_re[l], ssm_c_im[l], bsz, seq)
        ya = _moba(q, k, v, km, bsz, seq)
        x1, h2, eid, tw, rank, cnt = _merge(ys, u, ya, gs, ga, xcur, ssm_d[l], ssm_w_glu[l], w_ssm_branch[l],
                                            w_attn_branch[l], w_out[l], mix_post_g[l], gt_m, ffn_pre_g[l],
                                            sc_f, sh_f, router_w[l], router_b[l], seq, tm=256)

        ne = router_w.shape[-1]
        counts = cnt[0, :ne].astype(I32)
        padded = (counts + bm - 1) // bm * bm
        p_ends = jnp.cumsum(padded)
        p_starts = p_ends - padded
        eid4 = eid[:, :TOP_K]
        dest = p_starts[eid4] + rank[:, :TOP_K]
        n_blocks = (n_tok * TOP_K) // bm + ne
        n_rows = n_blocks * bm
        tok_ids = jnp.broadcast_to(jnp.arange(n_tok, dtype=I32)[:, None], (n_tok, TOP_K))
        row_tok = jnp.zeros((n_rows,), I32).at[dest.reshape(-1)].set(tok_ids.reshape(-1))
        row_w = jnp.zeros((n_rows,), F32).at[dest.reshape(-1)].set(tw[:, :TOP_K].reshape(-1))
        blk_exp = jnp.minimum(jnp.searchsorted(p_ends, jnp.arange(n_blocks, dtype=I32) * bm, side='right'),
                              ne - 1).astype(I32)
        n_valid = (p_ends[-1] // bm).astype(I32).reshape(1)

        xs = _gather_rows(h2.reshape(n_tok, sub, LANES), row_tok, rows=bm).reshape(n_rows, d)
        yb = _experts(xs, row_w.reshape(n_rows, 1), blk_exp, n_valid, w_gate[l], b_gate[l], w_up[l], b_up[l],
                      w_down[l], b_down[l], bm)
        tmc = 128
        dest_t = dest.reshape(n_tok // tmc, tmc, TOP_K).transpose(0, 2, 1).reshape(n_tok // tmc, 1, TOP_K * tmc)
        out3 = _combine(yb.reshape(n_rows, sub, LANES), dest_t, x1.reshape(n_tok, sub, LANES),
                        ffn_post_g[l].reshape(1, sub, LANES), gt_f.reshape(bsz, sub, LANES), seq, tmc)
        xcur = out3.reshape(n_tok, d)
    return xcur.reshape(bsz, seq, d).astype(x.dtype)
```

```python
import functools
import math

import jax
import jax.numpy as jnp
from jax import lax
from jax.experimental import pallas as pl
from jax.experimental.pallas import tpu as pltpu

F32 = jnp.float32
BF16 = jnp.bfloat16
I32 = jnp.int32

N_HEADS = 8
HEAD_DIM = 64
ROPE_THETA = 10000.0
MOBA_BLOCK = 256
MOBA_TOPK = 3
SSM_GROUP_SIZE = 16
SSM_GROUPS = 32
SSM_STATE = 64
N_EXPERTS = 32
TOP_K = 4
SWIGLU_ALPHA = 1.702
SWIGLU_LIMIT = 7.0
NORM_EPS = 1e-6
NEG_INF = -1e30

LANES = 128
SSM_CHUNK = 64
EXPERT_ROWS = 256
VMEM_LIMIT = 56 * 1024 * 1024
HIGHEST = lax.Precision.HIGHEST
Q_SCALE = HEAD_DIM ** -0.5 * math.log2(math.e)


def _cparams(sem):
    return pltpu.CompilerParams(dimension_semantics=sem, vmem_limit_bytes=VMEM_LIMIT)


def _dot(a, b, **kw):
    return jnp.dot(a, b, preferred_element_type=F32, **kw)


def _dot_t(a, b, **kw):
    return lax.dot_general(a, b, (((1,), (1,)), ((), ())), preferred_element_type=F32, **kw)


def _ada_kernel(c_ref, w_ref, b_ref, o_ref):
    c = c_ref[...]
    cond = c * jax.nn.sigmoid(c)
    o_ref[...] = _dot(cond, w_ref[...], precision=HIGHEST) + b_ref[...]


def _ada(c, ada_w, ada_b):
    bsz, d = c.shape
    n = ada_w.shape[1]
    c8 = jnp.zeros((8, d), F32).at[:bsz].set(c)
    out = pl.pallas_call(
        _ada_kernel,
        grid=(n // d,),
        in_specs=[pl.BlockSpec((8, d), lambda j: (0, 0)),
                  pl.BlockSpec((d, d), lambda j: (0, j)),
                  pl.BlockSpec((1, d), lambda j: (0, j))],
        out_specs=pl.BlockSpec((8, d), lambda j: (0, j)),
        out_shape=jax.ShapeDtypeStruct((8, n), F32),
        compiler_params=_cparams(("arbitrary",)),
        name="ada",
    )(c8, ada_w, ada_b.reshape(1, n))
    return out[:bsz]


def _inproj_kernel(x_ref, sc_ref, sh_ref, g_ref, pos_ref, invf_ref, w_ref,
                   u_ref, q_ref, k_ref, v_ref, gs_ref, ga_ref, km_ref, *, ssm_w, attn_w, d_model):
    x = x_ref[...]
    ms = jnp.mean(x * x, axis=-1, keepdims=True)
    xn = x * lax.rsqrt(ms + NORM_EPS) * g_ref[...]
    h = (xn * (1.0 + sc_ref[0]) + sh_ref[0]).astype(BF16)

    def proj(lo, width):
        return _dot(h, w_ref[:, lo:lo + width])

    u_ref[...] = proj(0, ssm_w).astype(BF16)

    ang = pos_ref[...].astype(F32) * invf_ref[...]
    reps = attn_w // LANES
    cos = jnp.concatenate([jnp.cos(ang)] * reps, axis=1)
    sin = jnp.concatenate([jnp.sin(ang)] * reps, axis=1)
    lane = lax.broadcasted_iota(I32, (1, attn_w), 1)
    first = (lane % HEAD_DIM) < (HEAD_DIM // 2)
    sin = jnp.where(first, -sin, sin)

    def rope(t):
        rot = jnp.where(first, pltpu.roll(t, attn_w - HEAD_DIM // 2, axis=1),
                        pltpu.roll(t, HEAD_DIM // 2, axis=1))
        return t * cos + rot * sin

    q = rope(proj(ssm_w, attn_w))
    q_ref[...] = (q * Q_SCALE).astype(BF16)
    k = rope(proj(ssm_w + attn_w, attn_w))
    k_ref[...] = k.astype(BF16)
    nblk = k.shape[0] // MOBA_BLOCK
    km_ref[0] = jnp.mean(k.reshape(nblk, MOBA_BLOCK, attn_w), axis=1)
    v_ref[...] = proj(ssm_w + 2 * attn_w, attn_w).astype(BF16)
    gs_ref[...] = jax.nn.sigmoid(proj(ssm_w + 3 * attn_w, d_model)).astype(BF16)
    ga_ref[...] = jax.nn.sigmoid(proj(ssm_w + 3 * attn_w + d_model, d_model)).astype(BF16)


def _inproj(x2, sc, sh, g, pos, w_in, seq, tm):
    n_tok, d = x2.shape
    attn_w = N_HEADS * HEAD_DIM
    ssm_w = SSM_GROUPS * SSM_GROUP_SIZE
    in_w = w_in.shape[1]
    half = HEAD_DIM // 2
    inv_freq = ROPE_THETA ** (-jnp.arange(half, dtype=F32) / half)
    invf = jnp.tile(inv_freq, LANES // half).reshape(1, LANES)
    nt = n_tok // tm
    per_b = seq // tm
    nblk = tm // MOBA_BLOCK
    tok = lambda w: pl.BlockSpec((tm, w), lambda i: (i, 0))
    bvec = pl.BlockSpec((1, 1, d), lambda i: (i // per_b, 0, 0))
    outs = pl.pallas_call(
        functools.partial(_inproj_kernel, ssm_w=ssm_w, attn_w=attn_w, d_model=d),
        grid=(nt,),
        in_specs=[tok(d), bvec, bvec,
                  pl.BlockSpec((1, d), lambda i: (0, 0)),
                  tok(1),
                  pl.BlockSpec((1, LANES), lambda i: (0, 0)),
                  pl.BlockSpec((d, in_w), lambda i: (0, 0))],
        out_specs=[tok(ssm_w), tok(attn_w), tok(attn_w), tok(attn_w), tok(d), tok(d),
                   pl.BlockSpec((1, nblk, attn_w), lambda i: (i, 0, 0))],
        out_shape=[jax.ShapeDtypeStruct((n_tok, ssm_w), BF16),
                   jax.ShapeDtypeStruct((n_tok, attn_w), BF16),
                   jax.ShapeDtypeStruct((n_tok, attn_w), BF16),
                   jax.ShapeDtypeStruct((n_tok, attn_w), BF16),
                   jax.ShapeDtypeStruct((n_tok, d), BF16),
                   jax.ShapeDtypeStruct((n_tok, d), BF16),
                   jax.ShapeDtypeStruct((nt, nblk, attn_w), F32)],
        compiler_params=_cparams(("arbitrary",)),
        name="inproj",
    )(x2, sc, sh, g, pos, invf, w_in.astype(BF16))
    return outs


def _s5_kernel(x_ref, zr_ref, zi_ref, lr_ref, li_ref, ca_ref, cb_ref, ba_ref, bb_ref,
               y_ref, t_ref, *, chunk, n_chunks):
    gs = SSM_GROUP_SIZE
    zr = zr_ref[0]
    zi = zi_ref[0]

    def powers(tau):
        mag = jnp.exp(tau * zr)
        return mag * jnp.cos(tau * zi), mag * jnp.sin(tau * zi)

    tau = lax.broadcasted_iota(I32, (chunk + 8, LANES), 0).astype(F32)
    e_re, e_im = powers(tau)
    lb_re, lb_im = e_re[1:2], e_im[1:2]
    lr, li = lr_ref[0], li_ref[0]
    den = lr * lr + li * li
    a, b = lb_re - 1.0, lb_im
    cf_re = (a * lr + b * li) / den
    cf_im = (b * lr - a * li) / den
    ba, bb = ba_ref[0], bb_ref[0]
    bri = cf_re * ba + cf_im * bb
    bri_sw = cf_re * bb - cf_im * ba

    ca, cb = ca_ref[0], cb_ref[0]
    cpow = (e_re[:chunk + 1, None, :] * ca[None] + e_im[:chunk + 1, None, :] * cb[None])
    cpow = cpow.reshape((chunk + 1) * gs, LANES)
    width = chunk * gs
    r = _dot_t(bri, cpow[:width], precision=HIGHEST)
    col = lax.broadcasted_iota(I32, (gs, width), 1)
    t_ref[0:gs, :] = r.astype(BF16)
    for j in range(1, chunk):
        shifted = jnp.where(col >= gs * j, pltpu.roll(r, gs * j, axis=1), 0.0)
        t_ref[gs * j:gs * (j + 1), :] = shifted.astype(BF16)

    x = x_ref[0]
    y = _dot(x, t_ref[...])

    tau_rev = (chunk - 1) - lax.broadcasted_iota(I32, (chunk, LANES), 0)
    r_re, r_im = powers(tau_rev.astype(F32))
    bst = r_re[:, None, :] * bri[None] + r_im[:, None, :] * bri_sw[None]
    bst = bst.reshape(width, LANES).astype(BF16)
    s = _dot(x, bst)

    rows = s.shape[0]
    n_idx = lax.broadcasted_iota(I32, (rows, LANES), 0) % n_chunks
    lane = lax.broadcasted_iota(I32, (1, LANES), 1)
    half = LANES // 2

    def cmul(v, p_re, p_im):
        return v * p_re + pltpu.roll(v, half, axis=1) * jnp.where(lane < half, -p_im, p_im)

    sh = 1
    while sh < n_chunks:
        p_re, p_im = powers(jnp.full((1, LANES), float(chunk * sh), F32))
        prev = jnp.where(n_idx >= sh, pltpu.roll(s, sh, axis=0), 0.0)
        s = s + cmul(prev, p_re, p_im)
        sh *= 2
    s_in = jnp.where(n_idx >= 1, pltpu.roll(s, 1, axis=0), 0.0)
    y = y + _dot_t(s_in.astype(BF16), cpow[gs:gs + width].astype(BF16))
    y_ref[0] = y


def _s5(u, lam_re, lam_im, log_dt, b_re, b_im, c_re, c_im, bsz, seq):
    g_n, gs, p = SSM_GROUPS, SSM_GROUP_SIZE, SSM_STATE
    chunk = SSM_CHUNK
    n_chunks = seq // chunk
    rows = bsz * n_chunks
    width = chunk * gs
    xg = u.reshape(rows, chunk, g_n, gs).transpose(2, 0, 1, 3).reshape(g_n, rows, width)
    dt = jnp.exp(log_dt.astype(F32))[:, None]
    dup = lambda a: jnp.concatenate([a, a], axis=-1).reshape(g_n, 1, 2 * p)
    zr, zi = dup(lam_re * dt), dup(lam_im * dt)
    lr, li = dup(lam_re), dup(lam_im)
    bt_re, bt_im = b_re.transpose(0, 2, 1), b_im.transpose(0, 2, 1)
    ba = jnp.concatenate([bt_re, bt_im], axis=-1)
    bb = jnp.concatenate([-bt_im, bt_re], axis=-1)
    ca = jnp.concatenate([c_re, -c_im], axis=-1)
    cb = jnp.concatenate([-c_im, -c_re], axis=-1)
    vec = pl.BlockSpec((1, 1, 2 * p), lambda g: (g, 0, 0))
    mat = pl.BlockSpec((1, gs, 2 * p), lambda g: (g, 0, 0))
    y = pl.pallas_call(
        functools.partial(_s5_kernel, chunk=chunk, n_chunks=n_chunks),
        grid=(g_n,),
        in_specs=[pl.BlockSpec((1, rows, width), lambda g: (g, 0, 0)),
                  vec, vec, vec, vec, mat, mat, mat, mat],
        out_specs=pl.BlockSpec((1, rows, width), lambda g: (g, 0, 0)),
        out_shape=jax.ShapeDtypeStruct((g_n, rows, width), F32),
        scratch_shapes=[pltpu.VMEM((width, width), BF16)],
        compiler_params=_cparams(("arbitrary",)),
        name="s5",
    )(xg, zr, zi, lr, li, ca, cb, ba, bb)
    return y.reshape(g_n, rows, chunk, gs).transpose(1, 2, 0, 3).reshape(bsz * seq, g_n * gs)


def _moba_kernel(qt_ref, k_ref, vt_ref, km_ref, o_ref, *, group):
    blk = MOBA_BLOCK
    qb = pl.program_id(2)
    qt = qt_ref[...]
    km = km_ref[...]
    km_hi = km.astype(BF16)
    km_lo = (km - km_hi.astype(F32)).astype(BF16)
    rowi = lax.broadcasted_iota(I32, (LANES, blk), 0)
    key_i = lax.broadcasted_iota(I32, (blk, blk), 0)
    qry_i = lax.broadcasted_iota(I32, (blk, blk), 1)
    valid = rowi < qb

    q_aug = []
    for h in range(2):
        qh = jnp.where((rowi >= h * HEAD_DIM) & (rowi < (h + 1) * HEAD_DIM), qt, jnp.zeros_like(qt))
        g = jnp.where(valid, _dot(km_hi, qh) + _dot(km_lo, qh), NEG_INF)
        sel = jnp.zeros((LANES, blk), jnp.bool_)
        for _ in range(MOBA_TOPK):
            m = jnp.max(g, axis=0, keepdims=True)
            idx = jnp.min(jnp.where(g == m, rowi, LANES), axis=0, keepdims=True)
            pick = rowi == idx
            sel = sel | pick
            g = jnp.where(pick, -jnp.inf, g)
        bias = jnp.where(sel & valid, 0.0, NEG_INF).astype(BF16)
        q_aug.append(jnp.concatenate([qh, bias], axis=0))

    def softmax_pv(h, s, vt, state):
        m_i, l_i, acc = state
        m_new = jnp.maximum(m_i, jnp.max(s, axis=0, keepdims=True))
        alpha = jnp.exp2(m_i - m_new)
        p = jnp.exp2(s - m_new)
        l_new = alpha * l_i + jnp.sum(p, axis=0, keepdims=True)
        return m_new, l_new, alpha * acc + _dot(vt, p.astype(BF16))

    def body(i, carry):
        kb0 = pl.multiple_of(i * group, group)
        keys = group * blk
        k_cat = k_ref[pl.ds(kb0, group)].reshape(keys, LANES)
        blk_id = kb0 + lax.broadcasted_iota(I32, (keys, LANES), 0) // blk
        onehot = jnp.where(lax.broadcasted_iota(I32, (keys, LANES), 1) == blk_id, 1.0, 0.0).astype(BF16)
        k_aug = jnp.concatenate([k_cat, onehot], axis=1)
        s = [_dot(k_aug, q_aug[h]) for h in range(2)]
        out = []
        for h in range(2):
            vt = jnp.concatenate([vt_ref[kb0 + j, h * HEAD_DIM:(h + 1) * HEAD_DIM, :] for j in range(group)], axis=1)
            out.append(softmax_pv(h, s[h], vt, carry[h]))
        return tuple(out)

    init = (jnp.full((1, blk), -jnp.inf, F32), jnp.zeros((1, blk), F32), jnp.zeros((HEAD_DIM, blk), F32))
    st = lax.fori_loop(0, (qb + group - 1) // group, body, (init, init))
    k_own = jnp.concatenate([k_ref[qb], jnp.zeros((blk, LANES), BF16)], axis=1)
    s_own = [jnp.where(key_i <= qry_i, _dot(k_own, q_aug[h]), NEG_INF) for h in range(2)]
    for h in range(2):
        _, l_i, acc = softmax_pv(h, s_own[h], vt_ref[qb, h * HEAD_DIM:(h + 1) * HEAD_DIM, :], st[h])
        o_ref[h * HEAD_DIM:(h + 1) * HEAD_DIM, :] = (acc / l_i).astype(BF16)


def _moba(q, k, v, km, bsz, seq):
    n_tok, attn_w = q.shape
    blk = MOBA_BLOCK
    nb = seq // blk
    assert nb <= LANES
    hp = attn_w // LANES
    km_pad = jnp.zeros((bsz, LANES, attn_w), F32).at[:, :nb].set(km.reshape(bsz, nb, attn_w))
    qt = q.reshape(bsz, seq, attn_w).transpose(0, 2, 1)
    k4 = k.reshape(bsz, nb, blk, attn_w)
    vt4 = v.reshape(bsz, nb, blk, attn_w).transpose(0, 1, 3, 2)
    group = 4 if nb % 4 == 0 else (2 if nb % 2 == 0 else 1)
    ot = pl.pallas_call(
        functools.partial(_moba_kernel, group=group),
        grid=(bsz, hp, nb),
        in_specs=[pl.BlockSpec((None, LANES, blk), lambda b, p, i: (b, p, i)),
                  pl.BlockSpec((None, nb, blk, LANES), lambda b, p, i: (b, 0, 0, p)),
                  pl.BlockSpec((None, nb, LANES, blk), lambda b, p, i: (b, 0, p, 0)),
                  pl.BlockSpec((None, LANES, LANES), lambda b, p, i: (b, 0, p))],
        out_specs=pl.BlockSpec((None, LANES, blk), lambda b, p, i: (b, p, i)),
        out_shape=jax.ShapeDtypeStruct((bsz, attn_w, seq), BF16),
        compiler_params=_cparams(("arbitrary", "arbitrary", "arbitrary")),
        name="moba",
    )(qt, k4, vt4, km_pad)
    return ot.transpose(0, 2, 1).reshape(n_tok, attn_w)


def _rms(t, g):
    return t * lax.rsqrt(jnp.mean(t * t, axis=-1, keepdims=True) + NORM_EPS) * g


def _store_slab(ref, val, base=0):
    rows, d = val.shape
    sub = d // LANES
    for s in range(sub):
        ref[pl.ds(base + s, rows, stride=sub), :] = val[:, s * LANES:(s + 1) * LANES]


def _load_slab(ref, rows, sub, base=0):
    return jnp.concatenate([ref[pl.ds(base + s, rows, stride=sub), :] for s in range(sub)], axis=1)


def _merge_kernel(ys_ref, u_ref, ya_ref, gs_ref, ga_ref, x_ref, d_ref, wglu_ref, wsb_ref, wab_ref,
                  wout_ref, postg_ref, gtm_ref, preg_ref, scf_ref, shf_ref, rw_ref, rb_ref,
                  x1_ref, h2_ref, eid_ref, tw_ref, rank_ref, cnt_ref, run_ref):
    @pl.when(pl.program_id(0) == 0)
    def _():
        run_ref[...] = jnp.zeros_like(run_ref)

    y = ys_ref[...] + d_ref[...] * u_ref[...].astype(F32)
    y = jax.nn.gelu(y)
    y = y * jax.nn.sigmoid(_dot(y.astype(BF16), wglu_ref[...]))
    bs = _dot(y.astype(BF16), wsb_ref[...])
    ba = _dot(ya_ref[...], wab_ref[...])
    merged = gs_ref[...].astype(F32) * bs + ga_ref[...].astype(F32) * ba
    mix = _dot(merged.astype(BF16), wout_ref[...])
    x1 = x_ref[...] + gtm_ref[0] * _rms(mix, postg_ref[...])
    x1_ref[...] = x1
    h2 = _rms(x1, preg_ref[...]) * (1.0 + scf_ref[0]) + shf_ref[0]
    _store_slab(h2_ref, h2)

    logits = _dot(h2, rw_ref[...], precision=HIGHEST) + rb_ref[...]
    tm = logits.shape[0]
    lane = lax.broadcasted_iota(I32, (tm, LANES), 1)
    g = logits
    vals, picks, idxs = [], [], []
    for _ in range(TOP_K):
        m = jnp.max(g, axis=1, keepdims=True)
        idx = jnp.min(jnp.where(g == m, lane, LANES), axis=1, keepdims=True)
        pick = lane == idx
        vals.append(m)
        idxs.append(idx)
        picks.append(pick)
        g = jnp.where(pick, -jnp.inf, g)
    exps = [jnp.exp(v - vals[0]) for v in vals]
    tot = exps[0] + exps[1] + exps[2] + exps[3]

    onehot = jnp.where(picks[0] | picks[1] | picks[2] | picks[3], 1.0, 0.0)
    r_i = lax.broadcasted_iota(I32, (tm, tm), 0)
    c_i = lax.broadcasted_iota(I32, (tm, tm), 1)
    tri = jnp.where(c_i < r_i, 1.0, 0.0).astype(BF16)
    rank_full = _dot(tri, onehot.astype(BF16)) + run_ref[...]
    run_ref[...] = run_ref[...] + jnp.sum(onehot, axis=0, keepdims=True)
    cnt_ref[...] = run_ref[...]

    eid = jnp.zeros((tm, LANES), I32)
    tw = jnp.zeros((tm, LANES), F32)
    rk = jnp.zeros((tm, LANES), F32)
    for r in range(TOP_K):
        eid = jnp.where(lane == r, idxs[r], eid)
        tw = jnp.where(lane == r, exps[r] / tot, tw)
        pos = jnp.sum(jnp.where(picks[r], rank_full, 0.0), axis=1, keepdims=True)
        rk = jnp.where(lane == r, pos, rk)
    eid_ref[...] = eid
    tw_ref[...] = tw
    rank_ref[...] = rk.astype(I32)


def _merge(ys, u, ya, gs, ga, x2, ssm_d, w_glu, w_sb, w_ab, w_out, post_g, gt_m, pre_g, sc_f, sh_f,
           router_w, router_b, seq, tm):
    n_tok, d = x2.shape
    sw = ys.shape[1]
    aw = ya.shape[1]
    ne = router_w.shape[1]
    per_b = seq // tm
    rw = jnp.zeros((d, LANES), F32).at[:, :ne].set(router_w)
    rb = jnp.full((1, LANES), NEG_INF, F32).at[0, :ne].set(router_b)
    tok = lambda w: pl.BlockSpec((tm, w), lambda i: (i, 0))
    full = lambda a, b: pl.BlockSpec((a, b), lambda i: (0, 0))
    bvec = pl.BlockSpec((1, 1, d), lambda i: (i // per_b, 0, 0))
    return pl.pallas_call(
        _merge_kernel,
        grid=(n_tok // tm,),
        in_specs=[tok(sw), tok(sw), tok(aw), tok(d), tok(d), tok(d),
                  full(1, sw), full(sw, sw), full(sw, d), full(aw, d), full(d, d),
                  full(1, d), bvec, full(1, d), bvec, bvec, full(d, LANES), full(1, LANES)],
        out_specs=[tok(d), pl.BlockSpec((tm * (d // LANES), LANES), lambda i: (i, 0)),
                   tok(LANES), tok(LANES), tok(LANES), full(1, LANES)],
        out_shape=[jax.ShapeDtypeStruct((n_tok, d), F32),
                   jax.ShapeDtypeStruct((n_tok * (d // LANES), LANES), F32),
                   jax.ShapeDtypeStruct((n_tok, LANES), I32),
                   jax.ShapeDtypeStruct((n_tok, LANES), F32),
                   jax.ShapeDtypeStruct((n_tok, LANES), I32),
                   jax.ShapeDtypeStruct((1, LANES), F32)],
        scratch_shapes=[pltpu.VMEM((1, LANES), F32)],
        compiler_params=_cparams(("arbitrary",)),
        name="merge",
    )(ys, u, ya, gs, ga, x2, ssm_d.reshape(1, sw), w_glu.astype(BF16), w_sb.astype(BF16),
      w_ab.astype(BF16), w_out.astype(BF16), post_g.reshape(1, d), gt_m, pre_g.reshape(1, d),
      sc_f, sh_f, rw, rb)


def _dispatch_kernel(pad_start_ref, pad_cnt_ref, tail_ref, dest_ref, h_ref, xs_ref, zero_ref, sem, zsem,
                     *, tm, sub, bm, n_rows, ne):
    i = pl.program_id(0)
    n_copies = tm * TOP_K

    def row(ref, r, n=1):
        return ref.at[pl.ds(pl.multiple_of(r * sub, sub), n * sub)]

    def zero_copies(do):
        def per_expert(e, _):
            start, cnt = pad_start_ref[e], pad_cnt_ref[e]
            for b in range(bm.bit_length() - 1):
                n = 1 << b

                @pl.when((cnt >> b) & 1 == 1)
                def _():
                    do(pltpu.make_async_copy(row(zero_ref, 0, n), row(xs_ref, start + (cnt & (n - 1)), n), zsem))
            return 0

        lax.fori_loop(0, ne, per_expert, 0)

        def per_block(j, _):
            r = tail_ref[0] + j * bm

            @pl.when(r < n_rows)
            def _():
                do(pltpu.make_async_copy(row(zero_ref, 0, bm), row(xs_ref, r, bm), zsem))
            return 0

        lax.fori_loop(0, ne, per_block, 0)

    @pl.when(i == 0)
    def _():
        zero_ref[...] = jnp.zeros_like(zero_ref)
        zero_copies(lambda cp: cp.start())

    def issue(r, _):
        src = row(h_ref, i * tm + r // TOP_K)
        pltpu.make_async_copy(src, row(xs_ref, dest_ref[0, 0, r]), sem).start()
        return 0

    lax.fori_loop(0, n_copies, issue, 0, unroll=8)
    pltpu.make_async_copy(row(h_ref, 0, n_copies), row(xs_ref, 0, n_copies), sem).wait()

    @pl.when(i == 0)
    def _():
        zero_copies(lambda cp: cp.wait())


def _dispatch(h_slab, dest, pad_start, pad_cnt, tail, n_rows, sub, bm, tm):
    n_tok = dest.shape[0]
    ne = pad_start.shape[0]
    grid_spec = pltpu.PrefetchScalarGridSpec(
        num_scalar_prefetch=3,
        grid=(n_tok // tm,),
        in_specs=[pl.BlockSpec((1, 1, TOP_K * tm), lambda i, *_: (i, 0, 0), memory_space=pltpu.SMEM),
                  pl.BlockSpec(memory_space=pl.ANY)],
        out_specs=pl.BlockSpec(memory_space=pl.ANY),
        scratch_shapes=[pltpu.VMEM((bm * sub, LANES), F32), pltpu.SemaphoreType.DMA(()),
                        pltpu.SemaphoreType.DMA(())],
    )
    return pl.pallas_call(
        functools.partial(_dispatch_kernel, tm=tm, sub=sub, bm=bm, n_rows=n_rows, ne=ne),
        grid_spec=grid_spec,
        out_shape=jax.ShapeDtypeStruct((n_rows * sub, LANES), F32),
        compiler_params=_cparams(("arbitrary",)),
        name="dispatch",
    )(pad_start, pad_cnt, tail, dest.reshape(n_tok // tm, 1, TOP_K * tm), h_slab)


def _expert_kernel(be_ref, nv_ref, x_ref, wg_ref, bg_ref, wu_ref, bu_ref, wd_ref, bd_ref,
                   o_ref, wg_s, wu_s, wd_s, *, bm, sub):
    i = pl.program_id(0)
    prev = be_ref[jnp.maximum(i - 1, 0)]
    changed = (i == 0) | (be_ref[i] != prev)

    @pl.when(changed)
    def _():
        wg_s[...] = wg_ref[0].astype(BF16)
        wu_s[...] = wu_ref[0].astype(BF16)
        wd_s[...] = wd_ref[0].astype(BF16)

    @pl.when(i < nv_ref[0])
    def _():
        xb = _load_slab(x_ref, bm, sub).astype(BF16)
        g = _dot(xb, wg_s[...]) + bg_ref[0]
        u = _dot(xb, wu_s[...]) + bu_ref[0]
        g = jnp.minimum(g, SWIGLU_LIMIT)
        u = jnp.clip(u, -SWIGLU_LIMIT, SWIGLU_LIMIT)
        act = g * jax.nn.sigmoid(SWIGLU_ALPHA * g) * (u + 1.0)
        _store_slab(o_ref, _dot(act.astype(BF16), wd_s[...]) + bd_ref[0])

    @pl.when(i >= nv_ref[0])
    def _():
        o_ref[...] = jnp.zeros_like(o_ref)


def _experts(xs, blk_exp, n_valid, w_gate, b_gate, w_up, b_up, w_down, b_down, bm):
    ne, d, f = w_gate.shape
    sub = d // LANES
    n_blocks = xs.shape[0] // (bm * sub)
    wspec = lambda a, b: pl.BlockSpec((1, a, b), lambda i, be, nv: (be[i], 0, 0))
    grid_spec = pltpu.PrefetchScalarGridSpec(
        num_scalar_prefetch=2,
        grid=(n_blocks,),
        in_specs=[pl.BlockSpec((bm * sub, LANES), lambda i, be, nv: (jnp.minimum(i, nv[0] - 1), 0)),
                  wspec(d, f), wspec(1, f), wspec(d, f), wspec(1, f), wspec(f, d), wspec(1, d)],
        out_specs=pl.BlockSpec((bm * sub, LANES), lambda i, be, nv: (i, 0)),
        scratch_shapes=[pltpu.VMEM((d, f), BF16), pltpu.VMEM((d, f), BF16), pltpu.VMEM((f, d), BF16)],
    )
    return pl.pallas_call(
        functools.partial(_expert_kernel, bm=bm, sub=sub),
        grid_spec=grid_spec,
        out_shape=jax.ShapeDtypeStruct(xs.shape, F32),
        compiler_params=_cparams(("arbitrary",)),
        name="experts",
    )(blk_exp, n_valid, xs, w_gate, b_gate.reshape(ne, 1, f), w_up, b_up.reshape(ne, 1, f),
      w_down, b_down.reshape(ne, 1, d))


def _combine_kernel(cur_ref, nxt_ref, yb_ref, tw_ref, x1_ref, g_ref, gt_ref, o_ref, buf, sem, *, tm, sub):
    i = pl.program_id(0)
    n_copies = tm * TOP_K

    def start_tile(idx_ref, slot):
        def issue(r, _):
            src = yb_ref.at[pl.ds(pl.multiple_of(idx_ref[0, 0, r] * sub, sub), sub)]
            dst = buf.at[slot, pl.ds(pl.multiple_of(r * sub, sub), sub)]
            pltpu.make_async_copy(src, dst, sem.at[slot]).start()
            return 0

        lax.fori_loop(0, n_copies, issue, 0, unroll=8)

    slot = i % 2

    @pl.when(i == 0)
    def _():
        start_tile(cur_ref, 0)

    @pl.when(i + 1 < pl.num_programs(0))
    def _():
        start_tile(nxt_ref, 1 - slot)

    pltpu.make_async_copy(yb_ref.at[pl.ds(0, n_copies * sub)], buf.at[slot], sem.at[slot]).wait()
    tw = tw_ref[...]
    ff = jnp.zeros((tm, sub * LANES), F32)
    for k in range(TOP_K):
        rows_k = jnp.concatenate([buf[slot, pl.ds(k * tm * sub + s, tm, stride=sub), :] for s in range(sub)], axis=1)
        ff = ff + tw[:, k:k + 1] * rows_k
    o_ref[...] = x1_ref[...] + gt_ref[0] * _rms(ff, g_ref[...])


def _combine(yb, dest, tw, x1, post_g, gt_f, seq, tm):
    n_tok, d = x1.shape
    sub = d // LANES
    nt = n_tok // tm
    per_b = seq // tm
    dest_t = dest.reshape(nt, tm, TOP_K).transpose(0, 2, 1).reshape(nt, 1, TOP_K * tm)
    idx_spec = lambda f: pl.BlockSpec((1, 1, TOP_K * tm), f, memory_space=pltpu.SMEM)
    return pl.pallas_call(
        functools.partial(_combine_kernel, tm=tm, sub=sub),
        grid=(nt,),
        in_specs=[idx_spec(lambda i: (i, 0, 0)),
                  idx_spec(lambda i: (jnp.minimum(i + 1, nt - 1), 0, 0)),
                  pl.BlockSpec(memory_space=pl.ANY),
                  pl.BlockSpec((tm, LANES), lambda i: (i, 0)),
                  pl.BlockSpec((tm, d), lambda i: (i, 0)),
                  pl.BlockSpec((1, d), lambda i: (0, 0)),
                  pl.BlockSpec((1, 1, d), lambda i: (i // per_b, 0, 0))],
        out_specs=pl.BlockSpec((tm, d), lambda i: (i, 0)),
        out_shape=jax.ShapeDtypeStruct((n_tok, d), F32),
        scratch_shapes=[pltpu.VMEM((2, TOP_K * tm * sub, LANES), F32), pltpu.SemaphoreType.DMA((2,))],
        compiler_params=_cparams(("arbitrary",)),
        name="combine",
    )(dest_t, dest_t, yb, tw, x1, post_g.reshape(1, d), gt_f)


def kernel(x, c, positions, ada_w, ada_b, mix_pre_g, mix_post_g, ffn_pre_g, ffn_post_g, w_in, ssm_lam_re, ssm_lam_im, ssm_log_dt, ssm_b_re, ssm_b_im, ssm_c_re, ssm_c_im, ssm_d, ssm_w_glu, w_ssm_branch, w_attn_branch, w_out, router_w, router_b, w_gate, b_gate, w_up, b_up, w_down, b_down):
    bsz, seq, d = x.shape
    depth = ada_w.shape[0]
    n_tok = bsz * seq
    bm = EXPERT_ROWS
    sub = d // LANES
    xcur = x.reshape(n_tok, d)
    pos = positions.reshape(n_tok, 1).astype(I32)
    for l in range(depth):
        ada = _ada(c, ada_w[l], ada_b[l])
        sh_m, sc_m, gt_m, sh_f, sc_f, gt_f = [a.reshape(bsz, 1, d) for a in jnp.split(ada, 6, axis=-1)]

        u, q, k, v, gs, ga, km = _inproj(xcur, sc_m, sh_m, mix_pre_g[l].reshape(1, d), pos, w_in[l],
                                         seq, tm=min(512, seq))
        ys = _s5(u, ssm_lam_re[l], ssm_lam_im[l], ssm_log_dt[l], ssm_b_re[l], ssm_b_im[l],
                 ssm_c_re[l], ssm_c_im[l], bsz, seq)
        ya = _moba(q, k, v, km, bsz, seq)
        x1, h2, eid, tw, rank, cnt = _merge(ys, u, ya, gs, ga, xcur, ssm_d[l], ssm_w_glu[l], w_ssm_branch[l],
                                            w_attn_branch[l], w_out[l], mix_post_g[l], gt_m, ffn_pre_g[l],
                                            sc_f, sh_f, router_w[l], router_b[l], seq, tm=256)

        ne = router_w.shape[-1]
        counts = cnt[0, :ne].astype(I32)
        padded = (counts + bm - 1) // bm * bm
        p_ends = jnp.cumsum(padded)
        p_starts = p_ends - padded
        eid4 = eid[:, :TOP_K]
        start_of = jnp.sum(jnp.where(eid4[:, :, None] == jnp.arange(ne, dtype=I32), p_starts, 0), axis=-1)
        dest = start_of + rank[:, :TOP_K]
        n_blocks = (n_tok * TOP_K) // bm + ne
        n_rows = n_blocks * bm
        blk_start = jnp.arange(n_blocks, dtype=I32)[:, None] * bm
        blk_exp = jnp.minimum(jnp.sum((blk_start >= p_ends[None, :]).astype(I32), axis=1), ne - 1)
        n_valid = (p_ends[-1:] // bm).astype(I32)

        xs = _dispatch(h2, dest, p_starts + counts, padded - counts, p_ends[-1:], n_rows, sub, bm, tm=256)
        yb = _experts(xs, blk_exp, n_valid, w_gate[l], b_gate[l], w_up[l], b_up[l], w_down[l], b_down[l], bm)
        xcur = _combine(yb, dest, tw, x1, ffn_post_g[l], gt_f, seq, tm=128)
    return xcur.reshape(bsz, seq, d).astype(x.dtype)
```

```python
import functools
import math

import jax
import jax.numpy as jnp
from jax import lax
from jax.experimental import pallas as pl
from jax.experimental.pallas import tpu as pltpu

F32 = jnp.float32
BF16 = jnp.bfloat16
I32 = jnp.int32

N_HEADS = 8
HEAD_DIM = 64
ROPE_THETA = 10000.0
MOBA_BLOCK = 256
MOBA_TOPK = 3
SSM_GROUP_SIZE = 16
SSM_GROUPS = 32
SSM_STATE = 64
N_EXPERTS = 32
TOP_K = 4
SWIGLU_ALPHA = 1.702
SWIGLU_LIMIT = 7.0
NORM_EPS = 1e-6
NEG_INF = -1e30

LANES = 128
SSM_CHUNK = 64
EXPERT_ROWS = 256
VMEM_LIMIT = 56 * 1024 * 1024
HIGHEST = lax.Precision.HIGHEST
Q_SCALE = HEAD_DIM ** -0.5 * math.log2(math.e)


def _cparams(sem):
    return pltpu.CompilerParams(dimension_semantics=sem, vmem_limit_bytes=VMEM_LIMIT)


def _dot(a, b, **kw):
    return jnp.dot(a, b, preferred_element_type=F32, **kw)


def _dot_t(a, b, **kw):
    return lax.dot_general(a, b, (((1,), (1,)), ((), ())), preferred_element_type=F32, **kw)


def _ada_kernel(c_ref, w_ref, b_ref, o_ref):
    c = c_ref[...]
    cond = c * jax.nn.sigmoid(c)
    o_ref[...] = _dot(cond, w_ref[...], precision=HIGHEST) + b_ref[...]


def _ada(c, ada_w, ada_b):
    bsz, d = c.shape
    n = ada_w.shape[1]
    c8 = jnp.zeros((8, d), F32).at[:bsz].set(c)
    out = pl.pallas_call(
        _ada_kernel,
        grid=(n // d,),
        in_specs=[pl.BlockSpec((8, d), lambda j: (0, 0)),
                  pl.BlockSpec((d, d), lambda j: (0, j)),
                  pl.BlockSpec((1, d), lambda j: (0, j))],
        out_specs=pl.BlockSpec((8, d), lambda j: (0, j)),
        out_shape=jax.ShapeDtypeStruct((8, n), F32),
        compiler_params=_cparams(("arbitrary",)),
        name="ada",
    )(c8, ada_w, ada_b.reshape(1, n))
    return out[:bsz]


def _inproj_kernel(x_ref, sc_ref, sh_ref, g_ref, pos_ref, invf_ref, w_ref,
                   u_ref, q_ref, k_ref, v_ref, gs_ref, ga_ref, km_ref, *, ssm_w, attn_w, d_model):
    x = x_ref[...]
    ms = jnp.mean(x * x, axis=-1, keepdims=True)
    xn = x * lax.rsqrt(ms + NORM_EPS) * g_ref[...]
    h = (xn * (1.0 + sc_ref[0]) + sh_ref[0]).astype(BF16)

    def proj(lo, width):
        return _dot(h, w_ref[:, lo:lo + width])

    u_ref[...] = proj(0, ssm_w).astype(BF16)

    ang = pos_ref[...].astype(F32) * invf_ref[...]
    reps = attn_w // LANES
    cos = jnp.concatenate([jnp.cos(ang)] * reps, axis=1)
    sin = jnp.concatenate([jnp.sin(ang)] * reps, axis=1)
    lane = lax.broadcasted_iota(I32, (1, attn_w), 1)
    first = (lane % HEAD_DIM) < (HEAD_DIM // 2)
    sin = jnp.where(first, -sin, sin)

    def rope(t):
        rot = jnp.where(first, pltpu.roll(t, attn_w - HEAD_DIM // 2, axis=1),
                        pltpu.roll(t, HEAD_DIM // 2, axis=1))
        return t * cos + rot * sin

    q = rope(proj(ssm_w, attn_w))
    q_ref[...] = (q * Q_SCALE).astype(BF16)
    k = rope(proj(ssm_w + attn_w, attn_w))
    k_ref[...] = k.astype(BF16)
    nblk = k.shape[0] // MOBA_BLOCK
    km_ref[0] = jnp.mean(k.reshape(nblk, MOBA_BLOCK, attn_w), axis=1)
    v_ref[...] = proj(ssm_w + 2 * attn_w, attn_w).astype(BF16)
    gs_ref[...] = jax.nn.sigmoid(proj(ssm_w + 3 * attn_w, d_model)).astype(BF16)
    ga_ref[...] = jax.nn.sigmoid(proj(ssm_w + 3 * attn_w + d_model, d_model)).astype(BF16)


def _inproj(x2, sc, sh, g, pos, w_in, seq, tm):
    n_tok, d = x2.shape
    attn_w = N_HEADS * HEAD_DIM
    ssm_w = SSM_GROUPS * SSM_GROUP_SIZE
    in_w = w_in.shape[1]
    half = HEAD_DIM // 2
    inv_freq = ROPE_THETA ** (-jnp.arange(half, dtype=F32) / half)
    invf = jnp.tile(inv_freq, LANES // half).reshape(1, LANES)
    nt = n_tok // tm
    per_b = seq // tm
    nblk = tm // MOBA_BLOCK
    tok = lambda w: pl.BlockSpec((tm, w), lambda i: (i, 0))
    bvec = pl.BlockSpec((1, 1, d), lambda i: (i // per_b, 0, 0))
    outs = pl.pallas_call(
        functools.partial(_inproj_kernel, ssm_w=ssm_w, attn_w=attn_w, d_model=d),
        grid=(nt,),
        in_specs=[tok(d), bvec, bvec,
                  pl.BlockSpec((1, d), lambda i: (0, 0)),
                  tok(1),
                  pl.BlockSpec((1, LANES), lambda i: (0, 0)),
                  pl.BlockSpec((d, in_w), lambda i: (0, 0))],
        out_specs=[tok(ssm_w), tok(attn_w), tok(attn_w), tok(attn_w), tok(d), tok(d),
                   pl.BlockSpec((1, nblk, attn_w), lambda i: (i, 0, 0))],
        out_shape=[jax.ShapeDtypeStruct((n_tok, ssm_w), BF16),
                   jax.ShapeDtypeStruct((n_tok, attn_w), BF16),
                   jax.ShapeDtypeStruct((n_tok, attn_w), BF16),
                   jax.ShapeDtypeStruct((n_tok, attn_w), BF16),
                   jax.ShapeDtypeStruct((n_tok, d), BF16),
                   jax.ShapeDtypeStruct((n_tok, d), BF16),
                   jax.ShapeDtypeStruct((nt, nblk, attn_w), F32)],
        compiler_params=_cparams(("arbitrary",)),
        name="inproj",
    )(x2, sc, sh, g, pos, invf, w_in.astype(BF16))
    return outs


def _s5_kernel(x_ref, zr_ref, zi_ref, lr_ref, li_ref, ca_ref, cb_ref, ba_ref, bb_ref,
               y_ref, t_ref, *, chunk, n_chunks):
    gs = SSM_GROUP_SIZE
    zr = zr_ref[0]
    zi = zi_ref[0]

    def powers(tau):
        mag = jnp.exp(tau * zr)
        return mag * jnp.cos(tau * zi), mag * jnp.sin(tau * zi)

    tau = lax.broadcasted_iota(I32, (chunk + 8, LANES), 0).astype(F32)
    e_re, e_im = powers(tau)
    lb_re, lb_im = e_re[1:2], e_im[1:2]
    lr, li = lr_ref[0], li_ref[0]
    den = lr * lr + li * li
    a, b = lb_re - 1.0, lb_im
    cf_re = (a * lr + b * li) / den
    cf_im = (b * lr - a * li) / den
    ba, bb = ba_ref[0], bb_ref[0]
    bri = cf_re * ba + cf_im * bb
    bri_sw = cf_re * bb - cf_im * ba

    ca, cb = ca_ref[0], cb_ref[0]
    cpow = (e_re[:chunk + 1, None, :] * ca[None] + e_im[:chunk + 1, None, :] * cb[None])
    cpow = cpow.reshape((chunk + 1) * gs, LANES)
    width = chunk * gs
    r = _dot_t(bri, cpow[:width], precision=HIGHEST)
    col = lax.broadcasted_iota(I32, (gs, width), 1)
    t_ref[0:gs, :] = r.astype(BF16)
    for j in range(1, chunk):
        shifted = jnp.where(col >= gs * j, pltpu.roll(r, gs * j, axis=1), 0.0)
        t_ref[gs * j:gs * (j + 1), :] = shifted.astype(BF16)

    x = x_ref[0]
    y = _dot(x, t_ref[...])

    tau_rev = (chunk - 1) - lax.broadcasted_iota(I32, (chunk, LANES), 0)
    r_re, r_im = powers(tau_rev.astype(F32))
    bst = r_re[:, None, :] * bri[None] + r_im[:, None, :] * bri_sw[None]
    bst = bst.reshape(width, LANES).astype(BF16)
    s = _dot(x, bst)

    rows = s.shape[0]
    n_idx = lax.broadcasted_iota(I32, (rows, LANES), 0) % n_chunks
    lane = lax.broadcasted_iota(I32, (1, LANES), 1)
    half = LANES // 2

    def cmul(v, p_re, p_im):
        return v * p_re + pltpu.roll(v, half, axis=1) * jnp.where(lane < half, -p_im, p_im)

    sh = 1
    while sh < n_chunks:
        p_re, p_im = powers(jnp.full((1, LANES), float(chunk * sh), F32))
        prev = jnp.where(n_idx >= sh, pltpu.roll(s, sh, axis=0), 0.0)
        s = s + cmul(prev, p_re, p_im)
        sh *= 2
    s_in = jnp.where(n_idx >= 1, pltpu.roll(s, 1, axis=0), 0.0)
    y = y + _dot_t(s_in.astype(BF16), cpow[gs:gs + width].astype(BF16))
    y_ref[0] = y


def _s5(u, lam_re, lam_im, log_dt, b_re, b_im, c_re, c_im, bsz, seq):
    g_n, gs, p = SSM_GROUPS, SSM_GROUP_SIZE, SSM_STATE
    chunk = SSM_CHUNK
    n_chunks = seq // chunk
    rows = bsz * n_chunks
    width = chunk * gs
    xg = u.reshape(rows, chunk, g_n, gs).transpose(2, 0, 1, 3).reshape(g_n, rows, width)
    dt = jnp.exp(log_dt.astype(F32))[:, None]
    dup = lambda a: jnp.concatenate([a, a], axis=-1).reshape(g_n, 1, 2 * p)
    zr, zi = dup(lam_re * dt), dup(lam_im * dt)
    lr, li = dup(lam_re), dup(lam_im)
    bt_re, bt_im = b_re.transpose(0, 2, 1), b_im.transpose(0, 2, 1)
    ba = jnp.concatenate([bt_re, bt_im], axis=-1)
    bb = jnp.concatenate([-bt_im, bt_re], axis=-1)
    ca = jnp.concatenate([c_re, -c_im], axis=-1)
    cb = jnp.concatenate([-c_im, -c_re], axis=-1)
    vec = pl.BlockSpec((1, 1, 2 * p), lambda g: (g, 0, 0))
    mat = pl.BlockSpec((1, gs, 2 * p), lambda g: (g, 0, 0))
    y = pl.pallas_call(
        functools.partial(_s5_kernel, chunk=chunk, n_chunks=n_chunks),
        grid=(g_n,),
        in_specs=[pl.BlockSpec((1, rows, width), lambda g: (g, 0, 0)),
                  vec, vec, vec, vec, mat, mat, mat, mat],
        out_specs=pl.BlockSpec((1, rows, width), lambda g: (g, 0, 0)),
        out_shape=jax.ShapeDtypeStruct((g_n, rows, width), F32),
        scratch_shapes=[pltpu.VMEM((width, width), BF16)],
        compiler_params=_cparams(("arbitrary",)),
        name="s5",
    )(xg, zr, zi, lr, li, ca, cb, ba, bb)
    return y.reshape(g_n, rows, chunk, gs).transpose(1, 2, 0, 3).reshape(bsz * seq, g_n * gs)


def _moba_kernel(qt_ref, k_ref, vt_ref, km_ref, o_ref, *, group):
    blk = MOBA_BLOCK
    qb = pl.program_id(2)
    qt = qt_ref[...]
    km = km_ref[...]
    km_hi = km.astype(BF16)
    km_lo = (km - km_hi.astype(F32)).astype(BF16)
    rowi = lax.broadcasted_iota(I32, (LANES, blk), 0)
    key_i = lax.broadcasted_iota(I32, (blk, blk), 0)
    qry_i = lax.broadcasted_iota(I32, (blk, blk), 1)
    valid = rowi < qb

    q_aug = []
    for h in range(2):
        qh = jnp.where((rowi >= h * HEAD_DIM) & (rowi < (h + 1) * HEAD_DIM), qt, jnp.zeros_like(qt))
        g = jnp.where(valid, _dot(km_hi, qh) + _dot(km_lo, qh), NEG_INF)
        sel = jnp.zeros((LANES, blk), jnp.bool_)
        for _ in range(MOBA_TOPK):
            m = jnp.max(g, axis=0, keepdims=True)
            idx = jnp.min(jnp.where(g == m, rowi, LANES), axis=0, keepdims=True)
            pick = rowi == idx
            sel = sel | pick
            g = jnp.where(pick, -jnp.inf, g)
        bias = jnp.where(sel & valid, 0.0, NEG_INF).astype(BF16)
        q_aug.append(jnp.concatenate([qh, bias], axis=0))

    def softmax_pv(h, s, vt, state):
        m_i, l_i, acc = state
        m_new = jnp.maximum(m_i, jnp.max(s, axis=0, keepdims=True))
        alpha = jnp.exp2(m_i - m_new)
        p = jnp.exp2(s - m_new)
        l_new = alpha * l_i + jnp.sum(p, axis=0, keepdims=True)
        return m_new, l_new, alpha * acc + _dot(vt, p.astype(BF16))

    def body(i, carry):
        kb0 = pl.multiple_of(i * group, group)
        keys = group * blk
        k_cat = k_ref[pl.ds(kb0, group)].reshape(keys, LANES)
        blk_id = kb0 + lax.broadcasted_iota(I32, (keys, LANES), 0) // blk
        onehot = jnp.where(lax.broadcasted_iota(I32, (keys, LANES), 1) == blk_id, 1.0, 0.0).astype(BF16)
        k_aug = jnp.concatenate([k_cat, onehot], axis=1)
        s = [_dot(k_aug, q_aug[h]) for h in range(2)]
        out = []
        for h in range(2):
            vt = jnp.concatenate([vt_ref[kb0 + j, h * HEAD_DIM:(h + 1) * HEAD_DIM, :] for j in range(group)], axis=1)
            out.append(softmax_pv(h, s[h], vt, carry[h]))
        return tuple(out)

    init = (jnp.full((1, blk), -jnp.inf, F32), jnp.zeros((1, blk), F32), jnp.zeros((HEAD_DIM, blk), F32))
    st = lax.fori_loop(0, (qb + group - 1) // group, body, (init, init))
    k_own = jnp.concatenate([k_ref[qb], jnp.zeros((blk, LANES), BF16)], axis=1)
    s_own = [jnp.where(key_i <= qry_i, _dot(k_own, q_aug[h]), NEG_INF) for h in range(2)]
    for h in range(2):
        _, l_i, acc = softmax_pv(h, s_own[h], vt_ref[qb, h * HEAD_DIM:(h + 1) * HEAD_DIM, :], st[h])
        o_ref[h * HEAD_DIM:(h + 1) * HEAD_DIM, :] = (acc / l_i).astype(BF16)


def _moba(q, k, v, km, bsz, seq):
    n_tok, attn_w = q.shape
    blk = MOBA_BLOCK
    nb = seq // blk
    assert nb <= LANES
    hp = attn_w // LANES
    km_pad = jnp.zeros((bsz, LANES, attn_w), F32).at[:, :nb].set(km.reshape(bsz, nb, attn_w))
    qt = q.reshape(bsz, seq, attn_w).transpose(0, 2, 1)
    k4 = k.reshape(bsz, nb, blk, attn_w)
    vt4 = v.reshape(bsz, nb, blk, attn_w).transpose(0, 1, 3, 2)
    group = 4 if nb % 4 == 0 else (2 if nb % 2 == 0 else 1)
    ot = pl.pallas_call(
        functools.partial(_moba_kernel, group=group),
        grid=(bsz, hp, nb),
        in_specs=[pl.BlockSpec((None, LANES, blk), lambda b, p, i: (b, p, i)),
                  pl.BlockSpec((None, nb, blk, LANES), lambda b, p, i: (b, 0, 0, p)),
                  pl.BlockSpec((None, nb, LANES, blk), lambda b, p, i: (b, 0, p, 0)),
                  pl.BlockSpec((None, LANES, LANES), lambda b, p, i: (b, 0, p))],
        out_specs=pl.BlockSpec((None, LANES, blk), lambda b, p, i: (b, p, i)),
        out_shape=jax.ShapeDtypeStruct((bsz, attn_w, seq), BF16),
        compiler_params=_cparams(("arbitrary", "arbitrary", "arbitrary")),
        name="moba",
    )(qt, k4, vt4, km_pad)
    return ot.transpose(0, 2, 1).reshape(n_tok, attn_w)


def _rms(t, g):
    return t * lax.rsqrt(jnp.mean(t * t, axis=-1, keepdims=True) + NORM_EPS) * g


def _store_slab(ref, val, base=0):
    rows, d = val.shape
    sub = d // LANES
    for s in range(sub):
        ref[pl.ds(base + s, rows, stride=sub), :] = val[:, s * LANES:(s + 1) * LANES]


def _load_slab(ref, rows, sub, base=0):
    return jnp.concatenate([ref[pl.ds(base + s, rows, stride=sub), :] for s in range(sub)], axis=1)


def _merge_kernel(ys_ref, u_ref, ya_ref, gs_ref, ga_ref, x_ref, d_ref, wglu_ref, wsb_ref, wab_ref,
                  wout_ref, postg_ref, gtm_ref, preg_ref, scf_ref, shf_ref, rw_ref, rb_ref,
                  x1_ref, h2_ref, eid_ref, tw_ref, rank_ref, cnt_ref, run_ref):
    @pl.when(pl.program_id(0) == 0)
    def _():
        run_ref[...] = jnp.zeros_like(run_ref)

    y = ys_ref[...] + d_ref[...] * u_ref[...].astype(F32)
    y = jax.nn.gelu(y)
    y = y * jax.nn.sigmoid(_dot(y.astype(BF16), wglu_ref[...]))
    bs = _dot(y.astype(BF16), wsb_ref[...])
    ba = _dot(ya_ref[...], wab_ref[...])
    merged = gs_ref[...].astype(F32) * bs + ga_ref[...].astype(F32) * ba
    mix = _dot(merged.astype(BF16), wout_ref[...])
    x1 = x_ref[...] + gtm_ref[0] * _rms(mix, postg_ref[...])
    x1_ref[...] = x1
    h2 = _rms(x1, preg_ref[...]) * (1.0 + scf_ref[0]) + shf_ref[0]
    _store_slab(h2_ref, h2)

    logits = _dot(h2, rw_ref[...], precision=HIGHEST) + rb_ref[...]
    tm = logits.shape[0]
    lane = lax.broadcasted_iota(I32, (tm, LANES), 1)
    g = logits
    vals, picks, idxs = [], [], []
    for _ in range(TOP_K):
        m = jnp.max(g, axis=1, keepdims=True)
        idx = jnp.min(jnp.where(g == m, lane, LANES), axis=1, keepdims=True)
        pick = lane == idx
        vals.append(m)
        idxs.append(idx)
        picks.append(pick)
        g = jnp.where(pick, -jnp.inf, g)
    exps = [jnp.exp(v - vals[0]) for v in vals]
    tot = exps[0] + exps[1] + exps[2] + exps[3]

    onehot = jnp.where(picks[0] | picks[1] | picks[2] | picks[3], 1.0, 0.0)
    r_i = lax.broadcasted_iota(I32, (tm, tm), 0)
    c_i = lax.broadcasted_iota(I32, (tm, tm), 1)
    tri = jnp.where(c_i < r_i, 1.0, 0.0).astype(BF16)
    rank_full = _dot(tri, onehot.astype(BF16)) + run_ref[...]
    run_ref[...] = run_ref[...] + jnp.sum(onehot, axis=0, keepdims=True)
    cnt_ref[...] = run_ref[...]

    eid = jnp.zeros((tm, LANES), I32)
    tw = jnp.zeros((tm, LANES), F32)
    rk = jnp.zeros((tm, LANES), F32)
    for r in range(TOP_K):
        eid = jnp.where(lane == r, idxs[r], eid)
        tw = jnp.where(lane == r, exps[r] / tot, tw)
        pos = jnp.sum(jnp.where(picks[r], rank_full, 0.0), axis=1, keepdims=True)
        rk = jnp.where(lane == r, pos, rk)
    eid_ref[...] = eid
    tw_ref[...] = tw
    rank_ref[...] = rk.astype(I32)


def _merge(ys, u, ya, gs, ga, x2, ssm_d, w_glu, w_sb, w_ab, w_out, post_g, gt_m, pre_g, sc_f, sh_f,
           router_w, router_b, seq, tm):
    n_tok, d = x2.shape
    sw = ys.shape[1]
    aw = ya.shape[1]
    ne = router_w.shape[1]
    per_b = seq // tm
    rw = jnp.zeros((d, LANES), F32).at[:, :ne].set(router_w)
    rb = jnp.full((1, LANES), NEG_INF, F32).at[0, :ne].set(router_b)
    tok = lambda w: pl.BlockSpec((tm, w), lambda i: (i, 0))
    full = lambda a, b: pl.BlockSpec((a, b), lambda i: (0, 0))
    bvec = pl.BlockSpec((1, 1, d), lambda i: (i // per_b, 0, 0))
    return pl.pallas_call(
        _merge_kernel,
        grid=(n_tok // tm,),
        in_specs=[tok(sw), tok(sw), tok(aw), tok(d), tok(d), tok(d),
                  full(1, sw), full(sw, sw), full(sw, d), full(aw, d), full(d, d),
                  full(1, d), bvec, full(1, d), bvec, bvec, full(d, LANES), full(1, LANES)],
        out_specs=[tok(d), pl.BlockSpec((tm * (d // LANES), LANES), lambda i: (i, 0)),
                   tok(LANES), tok(LANES), tok(LANES), full(1, LANES)],
        out_shape=[jax.ShapeDtypeStruct((n_tok, d), F32),
                   jax.ShapeDtypeStruct((n_tok * (d // LANES), LANES), F32),
                   jax.ShapeDtypeStruct((n_tok, LANES), I32),
                   jax.ShapeDtypeStruct((n_tok, LANES), F32),
                   jax.ShapeDtypeStruct((n_tok, LANES), I32),
                   jax.ShapeDtypeStruct((1, LANES), F32)],
        scratch_shapes=[pltpu.VMEM((1, LANES), F32)],
        compiler_params=_cparams(("arbitrary",)),
        name="merge",
    )(ys, u, ya, gs, ga, x2, ssm_d.reshape(1, sw), w_glu.astype(BF16), w_sb.astype(BF16),
      w_ab.astype(BF16), w_out.astype(BF16), post_g.reshape(1, d), gt_m, pre_g.reshape(1, d),
      sc_f, sh_f, rw, rb)


def _dispatch_kernel(pad_start_ref, pad_cnt_ref, tail_ref, dest_ref, h_ref, xs_ref, zero_ref, sem, zsem,
                     *, tm, sub, bm, n_rows, ne):
    i = pl.program_id(0)

    def row(ref, r, n=1):
        return ref.at[pl.ds(pl.multiple_of(r * sub, sub), n * sub)]

    def zero_copies(do):
        def per_expert(e, _):
            start, cnt = pad_start_ref[e], pad_cnt_ref[e]
            for b in range(bm.bit_length() - 1):
                n = 1 << b

                @pl.when((cnt >> b) & 1 == 1)
                def _():
                    do(pltpu.make_async_copy(row(zero_ref, 0, n), row(xs_ref, start + (cnt & (n - 1)), n), zsem))
            return 0

        lax.fori_loop(0, ne, per_expert, 0)

        def per_block(j, _):
            r = tail_ref[0] + j * bm

            @pl.when(r < n_rows)
            def _():
                do(pltpu.make_async_copy(row(zero_ref, 0, bm), row(xs_ref, r, bm), zsem))
            return 0

        lax.fori_loop(0, ne, per_block, 0)

    @pl.when(i == 0)
    def _():
        zero_ref[...] = jnp.zeros_like(zero_ref)
        zero_copies(lambda cp: cp.start())

    def issue(t, _):
        for k in range(TOP_K):
            pltpu.make_async_copy(row(h_ref, t), row(xs_ref, dest_ref[0, 0, t * TOP_K + k]), sem).start()
        return 0

    lax.fori_loop(0, tm, issue, 0, unroll=2)
    for _ in range(TOP_K):
        pltpu.make_async_copy(h_ref, row(xs_ref, 0, tm), sem).wait()

    @pl.when(i == 0)
    def _():
        zero_copies(lambda cp: cp.wait())


def _dispatch(h_slab, dest, pad_start, pad_cnt, tail, n_rows, sub, bm, tm):
    n_tok = dest.shape[0]
    ne = pad_start.shape[0]
    grid_spec = pltpu.PrefetchScalarGridSpec(
        num_scalar_prefetch=3,
        grid=(n_tok // tm,),
        in_specs=[pl.BlockSpec((1, 1, TOP_K * tm), lambda i, *_: (i, 0, 0), memory_space=pltpu.SMEM),
                  pl.BlockSpec((tm * sub, LANES), lambda i, *_: (i, 0))],
        out_specs=pl.BlockSpec(memory_space=pl.ANY),
        scratch_shapes=[pltpu.VMEM((bm * sub, LANES), F32), pltpu.SemaphoreType.DMA(()),
                        pltpu.SemaphoreType.DMA(())],
    )
    return pl.pallas_call(
        functools.partial(_dispatch_kernel, tm=tm, sub=sub, bm=bm, n_rows=n_rows, ne=ne),
        grid_spec=grid_spec,
        out_shape=jax.ShapeDtypeStruct((n_rows * sub, LANES), F32),
        compiler_params=_cparams(("arbitrary",)),
        name="dispatch",
    )(pad_start, pad_cnt, tail, dest.reshape(n_tok // tm, 1, TOP_K * tm), h_slab)


def _expert_kernel(be_ref, nv_ref, x_ref, wg_ref, bg_ref, wu_ref, bu_ref, wd_ref, bd_ref,
                   o_ref, wg_s, wu_s, wd_s, *, bm, sub):
    i = pl.program_id(0)
    prev = be_ref[jnp.maximum(i - 1, 0)]
    changed = (i == 0) | (be_ref[i] != prev)

    @pl.when(changed)
    def _():
        wg_s[...] = wg_ref[0].astype(BF16)
        wu_s[...] = wu_ref[0].astype(BF16)
        wd_s[...] = wd_ref[0].astype(BF16)

    @pl.when(i < nv_ref[0])
    def _():
        xb = _load_slab(x_ref, bm, sub).astype(BF16)
        g = _dot(xb, wg_s[...]) + bg_ref[0]
        u = _dot(xb, wu_s[...]) + bu_ref[0]
        g = jnp.minimum(g, SWIGLU_LIMIT)
        u = jnp.clip(u, -SWIGLU_LIMIT, SWIGLU_LIMIT)
        act = g * jax.nn.sigmoid(SWIGLU_ALPHA * g) * (u + 1.0)
        _store_slab(o_ref, _dot(act.astype(BF16), wd_s[...]) + bd_ref[0])

    @pl.when(i >= nv_ref[0])
    def _():
        o_ref[...] = jnp.zeros_like(o_ref)


def _experts(xs, blk_exp, n_valid, w_gate, b_gate, w_up, b_up, w_down, b_down, bm):
    ne, d, f = w_gate.shape
    sub = d // LANES
    n_blocks = xs.shape[0] // (bm * sub)
    wspec = lambda a, b: pl.BlockSpec((1, a, b), lambda i, be, nv: (be[i], 0, 0))
    grid_spec = pltpu.PrefetchScalarGridSpec(
        num_scalar_prefetch=2,
        grid=(n_blocks,),
        in_specs=[pl.BlockSpec((bm * sub, LANES), lambda i, be, nv: (jnp.minimum(i, nv[0] - 1), 0)),
                  wspec(d, f), wspec(1, f), wspec(d, f), wspec(1, f), wspec(f, d), wspec(1, d)],
        out_specs=pl.BlockSpec((bm * sub, LANES), lambda i, be, nv: (i, 0)),
        scratch_shapes=[pltpu.VMEM((d, f), BF16), pltpu.VMEM((d, f), BF16), pltpu.VMEM((f, d), BF16)],
    )
    return pl.pallas_call(
        functools.partial(_expert_kernel, bm=bm, sub=sub),
        grid_spec=grid_spec,
        out_shape=jax.ShapeDtypeStruct(xs.shape, F32),
        compiler_params=_cparams(("arbitrary",)),
        name="experts",
    )(blk_exp, n_valid, xs, w_gate, b_gate.reshape(ne, 1, f), w_up, b_up.reshape(ne, 1, f),
      w_down, b_down.reshape(ne, 1, d))


def _combine_kernel(cur_ref, nxt_ref, yb_ref, tw_ref, x1_ref, g_ref, gt_ref, o_ref, buf, sem, *, tm, sub):
    i = pl.program_id(0)
    n_copies = tm * TOP_K

    def start_tile(idx_ref, slot):
        def issue(r, _):
            src = yb_ref.at[pl.ds(pl.multiple_of(idx_ref[0, 0, r] * sub, sub), sub)]
            dst = buf.at[slot, pl.ds(pl.multiple_of(r * sub, sub), sub)]
            pltpu.make_async_copy(src, dst, sem.at[slot]).start()
            return 0

        lax.fori_loop(0, n_copies, issue, 0, unroll=8)

    slot = i % 2

    @pl.when(i == 0)
    def _():
        start_tile(cur_ref, 0)

    @pl.when(i + 1 < pl.num_programs(0))
    def _():
        start_tile(nxt_ref, 1 - slot)

    pltpu.make_async_copy(yb_ref.at[pl.ds(0, n_copies * sub)], buf.at[slot], sem.at[slot]).wait()
    tw = tw_ref[...]
    ff = jnp.zeros((tm, sub * LANES), F32)
    for k in range(TOP_K):
        rows_k = jnp.concatenate([buf[slot, pl.ds(k * tm * sub + s, tm, stride=sub), :] for s in range(sub)], axis=1)
        ff = ff + tw[:, k:k + 1] * rows_k
    o_ref[...] = x1_ref[...] + gt_ref[0] * _rms(ff, g_ref[...])


def _combine(yb, dest, tw, x1, post_g, gt_f, seq, tm):
    n_tok, d = x1.shape
    sub = d // LANES
    nt = n_tok // tm
    per_b = seq // tm
    dest_t = dest.reshape(nt, tm, TOP_K).transpose(0, 2, 1).reshape(nt, 1, TOP_K * tm)
    idx_spec = lambda f: pl.BlockSpec((1, 1, TOP_K * tm), f, memory_space=pltpu.SMEM)
    return pl.pallas_call(
        functools.partial(_combine_kernel, tm=tm, sub=sub),
        grid=(nt,),
        in_specs=[idx_spec(lambda i: (i, 0, 0)),
                  idx_spec(lambda i: (jnp.minimum(i + 1, nt - 1), 0, 0)),
                  pl.BlockSpec(memory_space=pl.ANY),
                  pl.BlockSpec((tm, LANES), lambda i: (i, 0)),
                  pl.BlockSpec((tm, d), lambda i: (i, 0)),
                  pl.BlockSpec((1, d), lambda i: (0, 0)),
                  pl.BlockSpec((1, 1, d), lambda i: (i // per_b, 0, 0))],
        out_specs=pl.BlockSpec((tm, d), lambda i: (i, 0)),
        out_shape=jax.ShapeDtypeStruct((n_tok, d), F32),
        scratch_shapes=[pltpu.VMEM((2, TOP_K * tm * sub, LANES), F32), pltpu.SemaphoreType.DMA((2,))],
        compiler_params=_cparams(("arbitrary",)),
        name="combine",
    )(dest_t, dest_t, yb, tw, x1, post_g.reshape(1, d), gt_f)


def kernel(x, c, positions, ada_w, ada_b, mix_pre_g, mix_post_g, ffn_pre_g, ffn_post_g, w_in, ssm_lam_re, ssm_lam_im, ssm_log_dt, ssm_b_re, ssm_b_im, ssm_c_re, ssm_c_im, ssm_d, ssm_w_glu, w_ssm_branch, w_attn_branch, w_out, router_w, router_b, w_gate, b_gate, w_up, b_up, w_down, b_down):
    bsz, seq, d = x.shape
    depth = ada_w.shape[0]
    n_tok = bsz * seq
    bm = EXPERT_ROWS
    sub = d // LANES
    xcur = x.reshape(n_tok, d)
    pos = positions.reshape(n_tok, 1).astype(I32)
    for l in range(depth):
        ada = _ada(c, ada_w[l], ada_b[l])
        sh_m, sc_m, gt_m, sh_f, sc_f, gt_f = [a.reshape(bsz, 1, d) for a in jnp.split(ada, 6, axis=-1)]

        u, q, k, v, gs, ga, km = _inproj(xcur, sc_m, sh_m, mix_pre_g[l].reshape(1, d), pos, w_in[l],
                                         seq, tm=min(512, seq))
        ys = _s5(u, ssm_lam_re[l], ssm_lam_im[l], ssm_log_dt[l], ssm_b_re[l], ssm_b_im[l],
                 ssm_c_re[l], ssm_c_im[l], bsz, seq)
        ya = _moba(q, k, v, km, bsz, seq)
        x1, h2, eid, tw, rank, cnt = _merge(ys, u, ya, gs, ga, xcur, ssm_d[l], ssm_w_glu[l], w_ssm_branch[l],
                                            w_attn_branch[l], w_out[l], mix_post_g[l], gt_m, ffn_pre_g[l],
                                            sc_f, sh_f, router_w[l], router_b[l], seq, tm=256)

        ne = router_w.shape[-1]
        counts = cnt[0, :ne].astype(I32)
        padded = (counts + bm - 1) // bm * bm
        p_ends = jnp.cumsum(padded)
        p_starts = p_ends - padded
        eid4 = eid[:, :TOP_K]
        start_of = jnp.sum(jnp.where(eid4[:, :, None] == jnp.arange(ne, dtype=I32), p_starts, 0), axis=-1)
        dest = start_of + rank[:, :TOP_K]
        n_blocks = (n_tok * TOP_K) // bm + ne
        n_rows = n_blocks * bm
        blk_start = jnp.arange(n_blocks, dtype=I32)[:, None] * bm
        blk_exp = jnp.minimum(jnp.sum((blk_start >= p_ends[None, :]).astype(I32), axis=1), ne - 1)
        n_valid = (p_ends[-1:] // bm).astype(I32)

        xs = _dispatch(h2, dest, p_starts + counts, padded - counts, p_ends[-1:], n_rows, sub, bm, tm=256)
        yb = _experts(xs, blk_exp, n_valid, w_gate[l], b_gate[l], w_up[l], b_up[l], w_down[l], b_down[l], bm)
        xcur = _combine(yb, dest, tw, x1, ffn_post_g[l], gt_f, seq, tm=128)
    return xcur.reshape(bsz, seq, d).astype(x.dtype)
```

```python
import functools
import math

import jax
import jax.numpy as jnp
from jax import lax
from jax.experimental import pallas as pl
from jax.experimental.pallas import tpu as pltpu

F32 = jnp.float32
BF16 = jnp.bfloat16
I32 = jnp.int32

N_HEADS = 8
HEAD_DIM = 64
ROPE_THETA = 10000.0
MOBA_BLOCK = 256
MOBA_TOPK = 3
SSM_GROUP_SIZE = 16
SSM_GROUPS = 32
SSM_STATE = 64
N_EXPERTS = 32
TOP_K = 4
SWIGLU_ALPHA = 1.702
SWIGLU_LIMIT = 7.0
NORM_EPS = 1e-6
NEG_INF = -1e30

LANES = 128
SSM_CHUNK = 64
EXPERT_ROWS = 512
VMEM_LIMIT = 56 * 1024 * 1024
HIGHEST = lax.Precision.HIGHEST
Q_SCALE = HEAD_DIM ** -0.5 * math.log2(math.e)


def _cparams(sem):
    return pltpu.CompilerParams(dimension_semantics=sem, vmem_limit_bytes=VMEM_LIMIT)


def _dot(a, b, **kw):
    return jnp.dot(a, b, preferred_element_type=F32, **kw)


def _dot_t(a, b, **kw):
    return lax.dot_general(a, b, (((1,), (1,)), ((), ())), preferred_element_type=F32, **kw)


def _ada_kernel(c_ref, w_ref, b_ref, o_ref):
    c = c_ref[...]
    cond = c * jax.nn.sigmoid(c)
    o_ref[...] = _dot(cond, w_ref[...], precision=HIGHEST) + b_ref[...]


def _ada(c, ada_w, ada_b):
    bsz, d = c.shape
    n = ada_w.shape[1]
    c8 = jnp.zeros((8, d), F32).at[:bsz].set(c)
    out = pl.pallas_call(
        _ada_kernel,
        grid=(n // d,),
        in_specs=[pl.BlockSpec((8, d), lambda j: (0, 0)),
                  pl.BlockSpec((d, d), lambda j: (0, j)),
                  pl.BlockSpec((1, d), lambda j: (0, j))],
        out_specs=pl.BlockSpec((8, d), lambda j: (0, j)),
        out_shape=jax.ShapeDtypeStruct((8, n), F32),
        compiler_params=_cparams(("arbitrary",)),
        name="ada",
    )(c8, ada_w, ada_b.reshape(1, n))
    return out[:bsz]


def _inproj_kernel(x_ref, sc_ref, sh_ref, g_ref, pos_ref, invf_ref, w_ref,
                   u_ref, q_ref, k_ref, v_ref, gs_ref, ga_ref, km_ref, *, ssm_w, attn_w, d_model):
    x = x_ref[...]
    ms = jnp.mean(x * x, axis=-1, keepdims=True)
    xn = x * lax.rsqrt(ms + NORM_EPS) * g_ref[...]
    h = (xn * (1.0 + sc_ref[0]) + sh_ref[0]).astype(BF16)

    def proj(lo, width):
        return _dot(h, w_ref[:, lo:lo + width])

    u_ref[...] = proj(0, ssm_w).astype(BF16)

    ang = pos_ref[...].astype(F32) * invf_ref[...]
    reps = attn_w // LANES
    cos = jnp.concatenate([jnp.cos(ang)] * reps, axis=1)
    sin = jnp.concatenate([jnp.sin(ang)] * reps, axis=1)
    lane = lax.broadcasted_iota(I32, (1, attn_w), 1)
    first = (lane % HEAD_DIM) < (HEAD_DIM // 2)
    sin = jnp.where(first, -sin, sin)

    def rope(t):
        rot = jnp.where(first, pltpu.roll(t, attn_w - HEAD_DIM // 2, axis=1),
                        pltpu.roll(t, HEAD_DIM // 2, axis=1))
        return t * cos + rot * sin

    q = rope(proj(ssm_w, attn_w))
    q_ref[...] = (q * Q_SCALE).astype(BF16)
    k = rope(proj(ssm_w + attn_w, attn_w))
    k_ref[...] = k.astype(BF16)
    nblk = k.shape[0] // MOBA_BLOCK
    km_ref[0] = jnp.mean(k.reshape(nblk, MOBA_BLOCK, attn_w), axis=1)
    v_ref[...] = proj(ssm_w + 2 * attn_w, attn_w).astype(BF16)
    gs_ref[...] = jax.nn.sigmoid(proj(ssm_w + 3 * attn_w, d_model)).astype(BF16)
    ga_ref[...] = jax.nn.sigmoid(proj(ssm_w + 3 * attn_w + d_model, d_model)).astype(BF16)


def _inproj(x2, sc, sh, g, pos, w_in, seq, tm):
    n_tok, d = x2.shape
    attn_w = N_HEADS * HEAD_DIM
    ssm_w = SSM_GROUPS * SSM_GROUP_SIZE
    in_w = w_in.shape[1]
    half = HEAD_DIM // 2
    inv_freq = ROPE_THETA ** (-jnp.arange(half, dtype=F32) / half)
    invf = jnp.tile(inv_freq, LANES // half).reshape(1, LANES)
    nt = n_tok // tm
    per_b = seq // tm
    nblk = tm // MOBA_BLOCK
    tok = lambda w: pl.BlockSpec((tm, w), lambda i: (i, 0))
    bvec = pl.BlockSpec((1, 1, d), lambda i: (i // per_b, 0, 0))
    outs = pl.pallas_call(
        functools.partial(_inproj_kernel, ssm_w=ssm_w, attn_w=attn_w, d_model=d),
        grid=(nt,),
        in_specs=[tok(d), bvec, bvec,
                  pl.BlockSpec((1, d), lambda i: (0, 0)),
                  tok(1),
                  pl.BlockSpec((1, LANES), lambda i: (0, 0)),
                  pl.BlockSpec((d, in_w), lambda i: (0, 0))],
        out_specs=[tok(ssm_w), tok(attn_w), tok(attn_w), tok(attn_w), tok(d), tok(d),
                   pl.BlockSpec((1, nblk, attn_w), lambda i: (i, 0, 0))],
        out_shape=[jax.ShapeDtypeStruct((n_tok, ssm_w), BF16),
                   jax.ShapeDtypeStruct((n_tok, attn_w), BF16),
                   jax.ShapeDtypeStruct((n_tok, attn_w), BF16),
                   jax.ShapeDtypeStruct((n_tok, attn_w), BF16),
                   jax.ShapeDtypeStruct((n_tok, d), BF16),
                   jax.ShapeDtypeStruct((n_tok, d), BF16),
                   jax.ShapeDtypeStruct((nt, nblk, attn_w), F32)],
        compiler_params=_cparams(("arbitrary",)),
        name="inproj",
    )(x2, sc, sh, g, pos, invf, w_in.astype(BF16))
    return outs


def _s5_kernel(x_ref, zr_ref, zi_ref, lr_ref, li_ref, ca_ref, cb_ref, ba_ref, bb_ref,
               y_ref, t_ref, *, chunk, n_chunks):
    gs = SSM_GROUP_SIZE
    zr = zr_ref[0]
    zi = zi_ref[0]

    def powers(tau):
        mag = jnp.exp(tau * zr)
        return mag * jnp.cos(tau * zi), mag * jnp.sin(tau * zi)

    tau = lax.broadcasted_iota(I32, (chunk + 8, LANES), 0).astype(F32)
    e_re, e_im = powers(tau)
    lb_re, lb_im = e_re[1:2], e_im[1:2]
    lr, li = lr_ref[0], li_ref[0]
    den = lr * lr + li * li
    a, b = lb_re - 1.0, lb_im
    cf_re = (a * lr + b * li) / den
    cf_im = (b * lr - a * li) / den
    ba, bb = ba_ref[0], bb_ref[0]
    bri = cf_re * ba + cf_im * bb
    bri_sw = cf_re * bb - cf_im * ba

    ca, cb = ca_ref[0], cb_ref[0]
    cpow = (e_re[:chunk + 1, None, :] * ca[None] + e_im[:chunk + 1, None, :] * cb[None])
    cpow = cpow.reshape((chunk + 1) * gs, LANES)
    width = chunk * gs
    r = _dot_t(bri, cpow[:width], precision=HIGHEST)
    col = lax.broadcasted_iota(I32, (gs, width), 1)
    t_ref[0:gs, :] = r.astype(BF16)
    for j in range(1, chunk):
        shifted = jnp.where(col >= gs * j, pltpu.roll(r, gs * j, axis=1), 0.0)
        t_ref[gs * j:gs * (j + 1), :] = shifted.astype(BF16)

    x = x_ref[0]
    y = _dot(x, t_ref[...])

    tau_rev = (chunk - 1) - lax.broadcasted_iota(I32, (chunk, LANES), 0)
    r_re, r_im = powers(tau_rev.astype(F32))
    bst = r_re[:, None, :] * bri[None] + r_im[:, None, :] * bri_sw[None]
    bst = bst.reshape(width, LANES).astype(BF16)
    s = _dot(x, bst)

    rows = s.shape[0]
    n_idx = lax.broadcasted_iota(I32, (rows, LANES), 0) % n_chunks
    lane = lax.broadcasted_iota(I32, (1, LANES), 1)
    half = LANES // 2

    def cmul(v, p_re, p_im):
        return v * p_re + pltpu.roll(v, half, axis=1) * jnp.where(lane < half, -p_im, p_im)

    sh = 1
    while sh < n_chunks:
        p_re, p_im = powers(jnp.full((1, LANES), float(chunk * sh), F32))
        prev = jnp.where(n_idx >= sh, pltpu.roll(s, sh, axis=0), 0.0)
        s = s + cmul(prev, p_re, p_im)
        sh *= 2
    s_in = jnp.where(n_idx >= 1, pltpu.roll(s, 1, axis=0), 0.0)
    y = y + _dot_t(s_in.astype(BF16), cpow[gs:gs + width].astype(BF16))
    y_ref[0] = y


def _s5(u, lam_re, lam_im, log_dt, b_re, b_im, c_re, c_im, bsz, seq):
    g_n, gs, p = SSM_GROUPS, SSM_GROUP_SIZE, SSM_STATE
    chunk = SSM_CHUNK
    n_chunks = seq // chunk
    rows = bsz * n_chunks
    width = chunk * gs
    xg = u.reshape(rows, chunk, g_n, gs).transpose(2, 0, 1, 3).reshape(g_n, rows, width)
    dt = jnp.exp(log_dt.astype(F32))[:, None]
    dup = lambda a: jnp.concatenate([a, a], axis=-1).reshape(g_n, 1, 2 * p)
    zr, zi = dup(lam_re * dt), dup(lam_im * dt)
    lr, li = dup(lam_re), dup(lam_im)
    bt_re, bt_im = b_re.transpose(0, 2, 1), b_im.transpose(0, 2, 1)
    ba = jnp.concatenate([bt_re, bt_im], axis=-1)
    bb = jnp.concatenate([-bt_im, bt_re], axis=-1)
    ca = jnp.concatenate([c_re, -c_im], axis=-1)
    cb = jnp.concatenate([-c_im, -c_re], axis=-1)
    vec = pl.BlockSpec((1, 1, 2 * p), lambda g: (g, 0, 0))
    mat = pl.BlockSpec((1, gs, 2 * p), lambda g: (g, 0, 0))
    y = pl.pallas_call(
        functools.partial(_s5_kernel, chunk=chunk, n_chunks=n_chunks),
        grid=(g_n,),
        in_specs=[pl.BlockSpec((1, rows, width), lambda g: (g, 0, 0)),
                  vec, vec, vec, vec, mat, mat, mat, mat],
        out_specs=pl.BlockSpec((1, rows, width), lambda g: (g, 0, 0)),
        out_shape=jax.ShapeDtypeStruct((g_n, rows, width), F32),
        scratch_shapes=[pltpu.VMEM((width, width), BF16)],
        compiler_params=_cparams(("arbitrary",)),
        name="s5",
    )(xg, zr, zi, lr, li, ca, cb, ba, bb)
    return y.reshape(g_n, rows, chunk, gs).transpose(1, 2, 0, 3).reshape(bsz * seq, g_n * gs)


def _moba_kernel(qt_ref, k_ref, vt_ref, km_ref, o_ref, s_a, s_b, s_own, *, group, nb, nbp):
    blk = MOBA_BLOCK
    keys = group * blk
    qb = pl.program_id(2)
    qt = qt_ref[...]
    km = km_ref[...]
    km_hi = km.astype(BF16)
    km_lo = (km - km_hi.astype(F32)).astype(BF16)
    dim_i = lax.broadcasted_iota(I32, (LANES, blk), 0)
    blk_i = lax.broadcasted_iota(I32, (nbp, blk), 0)
    blk_f = blk_i.astype(F32)
    valid = blk_i < qb

    q_aug = []
    for h in range(2):
        qh = jnp.where((dim_i >= h * HEAD_DIM) & (dim_i < (h + 1) * HEAD_DIM), qt, jnp.zeros_like(qt))
        g = jnp.where(valid, _dot(km_hi, qh) + _dot(km_lo, qh), NEG_INF)
        sel = jnp.zeros((nbp, blk), jnp.bool_)
        for _ in range(MOBA_TOPK):
            m = jnp.max(g, axis=0, keepdims=True)
            idx = jnp.min(jnp.where(g == m, blk_f, float(nbp)), axis=0, keepdims=True)
            pick = blk_f == idx
            sel = sel | pick
            g = jnp.where(pick, -jnp.inf, g)
        parts = [qh, jnp.where(sel & valid, 0.0, NEG_INF).astype(BF16)]
        if nbp < LANES:
            parts.append(jnp.full((LANES - nbp, blk), NEG_INF, BF16))
        q_aug.append(jnp.concatenate(parts, axis=0))

    def scores(grp, s_ref):
        kb0 = grp * group
        k_cat = k_ref[pl.ds(jnp.minimum(kb0, nb - group), group)].reshape(keys, LANES)
        blk_id = kb0 + lax.broadcasted_iota(I32, (keys, LANES), 0) // blk
        onehot = jnp.where(lax.broadcasted_iota(I32, (keys, LANES), 1) == blk_id, 1.0, 0.0).astype(BF16)
        k_aug = jnp.concatenate([k_cat, onehot], axis=1)
        for h in range(2):
            s_ref[h] = _dot(k_aug, q_aug[h])

    def softmax_pv(s, vt, state):
        m_i, l_i, acc = state
        m_new = jnp.maximum(m_i, jnp.max(s, axis=0, keepdims=True))
        alpha = jnp.exp2(m_i - m_new)
        p = jnp.exp2(s - m_new)
        l_new = alpha * l_i + jnp.sum(p, axis=0, keepdims=True)
        return m_new, l_new, alpha * acc + _dot(vt, p.astype(BF16))

    def consume(grp, s_ref, st):
        out = []
        for h in range(2):
            vt = jnp.concatenate([vt_ref[jnp.minimum(grp * group + j, nb - 1), h * HEAD_DIM:(h + 1) * HEAD_DIM, :]
                                  for j in range(group)], axis=1)
            out.append(softmax_pv(s_ref[h], vt, st[h]))
        return tuple(out)

    k_own = jnp.concatenate([k_ref[qb], jnp.zeros((blk, LANES), BF16)], axis=1)
    for h in range(2):
        s_own[h] = _dot(k_own, q_aug[h])
    scores(0, s_a)

    def body(t, st):
        scores(2 * t + 1, s_b)
        st = consume(2 * t, s_a, st)
        scores(2 * t + 2, s_a)
        return consume(2 * t + 1, s_b, st)

    n_groups = (qb + group - 1) // group
    init = (jnp.full((1, blk), -jnp.inf, F32), jnp.zeros((1, blk), F32), jnp.zeros((HEAD_DIM, blk), F32))
    st = lax.fori_loop(0, (n_groups + 1) // 2, body, (init, init))
    key_i = lax.broadcasted_iota(I32, (blk, blk), 0)
    qry_i = lax.broadcasted_iota(I32, (blk, blk), 1)
    for h in range(2):
        s = jnp.where(key_i <= qry_i, s_own[h], NEG_INF)
        _, l_i, acc = softmax_pv(s, vt_ref[qb, h * HEAD_DIM:(h + 1) * HEAD_DIM, :], st[h])
        o_ref[h * HEAD_DIM:(h + 1) * HEAD_DIM, :] = (acc / l_i).astype(BF16)


def _moba(q, k, v, km, bsz, seq):
    n_tok, attn_w = q.shape
    blk = MOBA_BLOCK
    nb = seq // blk
    group = 2 if nb % 2 == 0 else 1
    nbp = -(-nb // 16) * 16
    assert nb + 3 * group <= LANES
    hp = attn_w // LANES
    km_pad = jnp.zeros((bsz, nbp, attn_w), F32).at[:, :nb].set(km.reshape(bsz, nb, attn_w))
    qt = q.reshape(bsz, seq, attn_w).transpose(0, 2, 1)
    k4 = k.reshape(bsz, nb, blk, attn_w)
    vt4 = v.reshape(bsz, nb, blk, attn_w).transpose(0, 1, 3, 2)
    ot = pl.pallas_call(
        functools.partial(_moba_kernel, group=group, nb=nb, nbp=nbp),
        grid=(bsz, hp, nb),
        in_specs=[pl.BlockSpec((None, LANES, blk), lambda b, p, i: (b, p, i)),
                  pl.BlockSpec((None, nb, blk, LANES), lambda b, p, i: (b, 0, 0, p)),
                  pl.BlockSpec((None, nb, LANES, blk), lambda b, p, i: (b, 0, p, 0)),
                  pl.BlockSpec((None, nbp, LANES), lambda b, p, i: (b, 0, p))],
        out_specs=pl.BlockSpec((None, LANES, blk), lambda b, p, i: (b, p, i)),
        out_shape=jax.ShapeDtypeStruct((bsz, attn_w, seq), BF16),
        scratch_shapes=[pltpu.VMEM((2, group * blk, blk), F32), pltpu.VMEM((2, group * blk, blk), F32),
                        pltpu.VMEM((2, blk, blk), F32)],
        compiler_params=_cparams(("arbitrary", "arbitrary", "arbitrary")),
        name="moba",
    )(qt, k4, vt4, km_pad)
    return ot.transpose(0, 2, 1).reshape(n_tok, attn_w)


def _rms(t, g):
    return t * lax.rsqrt(jnp.mean(t * t, axis=-1, keepdims=True) + NORM_EPS) * g


def _store_slab(ref, val, base=0):
    rows, d = val.shape
    sub = d // LANES
    for s in range(sub):
        ref[pl.ds(base + s, rows, stride=sub), :] = val[:, s * LANES:(s + 1) * LANES]


def _load_slab(ref, rows, sub, base=0):
    return jnp.concatenate([ref[pl.ds(base + s, rows, stride=sub), :] for s in range(sub)], axis=1)


def _merge_kernel(ys_ref, u_ref, ya_ref, gs_ref, ga_ref, x_ref, d_ref, wglu_ref, wsb_ref, wab_ref,
                  wout_ref, postg_ref, gtm_ref, preg_ref, scf_ref, shf_ref, rw_ref, rb_ref,
                  x1_ref, h2_ref, eid_ref, tw_ref, rank_ref, cnt_ref, run_ref):
    @pl.when(pl.program_id(0) == 0)
    def _():
        run_ref[...] = jnp.zeros_like(run_ref)

    y = ys_ref[...] + d_ref[...] * u_ref[...].astype(F32)
    y = jax.nn.gelu(y)
    y = y * jax.nn.sigmoid(_dot(y.astype(BF16), wglu_ref[...]))
    bs = _dot(y.astype(BF16), wsb_ref[...])
    ba = _dot(ya_ref[...], wab_ref[...])
    merged = gs_ref[...].astype(F32) * bs + ga_ref[...].astype(F32) * ba
    mix = _dot(merged.astype(BF16), wout_ref[...])
    x1 = x_ref[...] + gtm_ref[0] * _rms(mix, postg_ref[...])
    x1_ref[...] = x1
    h2 = _rms(x1, preg_ref[...]) * (1.0 + scf_ref[0]) + shf_ref[0]
    _store_slab(h2_ref, h2)

    h_hi = h2.astype(BF16)
    h_lo = (h2 - h_hi.astype(F32)).astype(BF16)
    logits = _dot(jnp.concatenate([h_hi, h_hi, h_lo], axis=1), rw_ref[...]) + rb_ref[...]
    tm = logits.shape[0]
    lane = lax.broadcasted_iota(I32, (tm, LANES), 1)
    lane_f = lane.astype(F32)
    g = logits
    vals, picks, idxs = [], [], []
    for _ in range(TOP_K):
        m = jnp.max(g, axis=1, keepdims=True)
        idx = jnp.min(jnp.where(g == m, lane_f, float(LANES)), axis=1, keepdims=True)
        pick = lane_f == idx
        vals.append(m)
        idxs.append(idx.astype(I32))
        picks.append(pick)
        g = jnp.where(pick, -jnp.inf, g)
    exps = [jnp.exp(v - vals[0]) for v in vals]
    tot = exps[0] + exps[1] + exps[2] + exps[3]

    onehot = jnp.where(picks[0] | picks[1] | picks[2] | picks[3], 1.0, 0.0)
    r_i = lax.broadcasted_iota(I32, (tm, tm), 0)
    c_i = lax.broadcasted_iota(I32, (tm, tm), 1)
    tri = jnp.where(c_i < r_i, 1.0, 0.0).astype(BF16)
    rank_full = _dot(tri, onehot.astype(BF16)) + run_ref[...]
    run_ref[...] = run_ref[...] + jnp.sum(onehot, axis=0, keepdims=True)
    cnt_ref[...] = run_ref[...]

    eid = jnp.zeros((tm, LANES), I32)
    tw = jnp.zeros((tm, LANES), F32)
    rk = jnp.zeros((tm, LANES), F32)
    for r in range(TOP_K):
        eid = jnp.where(lane == r, idxs[r], eid)
        tw = jnp.where(lane == r, exps[r] / tot, tw)
        pos = jnp.sum(jnp.where(picks[r], rank_full, 0.0), axis=1, keepdims=True)
        rk = jnp.where(lane == r, pos, rk)
    eid_ref[...] = eid
    tw_ref[...] = tw
    rank_ref[...] = rk.astype(I32)


def _merge(ys, u, ya, gs, ga, x2, ssm_d, w_glu, w_sb, w_ab, w_out, post_g, gt_m, pre_g, sc_f, sh_f,
           router_w, router_b, seq, tm):
    n_tok, d = x2.shape
    sw = ys.shape[1]
    aw = ya.shape[1]
    ne = router_w.shape[1]
    per_b = seq // tm
    rw = jnp.zeros((d, LANES), F32).at[:, :ne].set(router_w)
    rw_hi = rw.astype(BF16)
    rw_lo = (rw - rw_hi.astype(F32)).astype(BF16)
    rw = jnp.concatenate([rw_hi, rw_lo, rw_hi], axis=0)
    rb = jnp.full((1, LANES), NEG_INF, F32).at[0, :ne].set(router_b)
    tok = lambda w: pl.BlockSpec((tm, w), lambda i: (i, 0))
    full = lambda a, b: pl.BlockSpec((a, b), lambda i: (0, 0))
    bvec = pl.BlockSpec((1, 1, d), lambda i: (i // per_b, 0, 0))
    return pl.pallas_call(
        _merge_kernel,
        grid=(n_tok // tm,),
        in_specs=[tok(sw), tok(sw), tok(aw), tok(d), tok(d), tok(d),
                  full(1, sw), full(sw, sw), full(sw, d), full(aw, d), full(d, d),
                  full(1, d), bvec, full(1, d), bvec, bvec, full(3 * d, LANES), full(1, LANES)],
        out_specs=[tok(d), pl.BlockSpec((tm * (d // LANES), LANES), lambda i: (i, 0)),
                   tok(LANES), tok(LANES), tok(LANES), full(1, LANES)],
        out_shape=[jax.ShapeDtypeStruct((n_tok, d), F32),
                   jax.ShapeDtypeStruct((n_tok * (d // LANES), LANES), F32),
                   jax.ShapeDtypeStruct((n_tok, LANES), I32),
                   jax.ShapeDtypeStruct((n_tok, LANES), F32),
                   jax.ShapeDtypeStruct((n_tok, LANES), I32),
                   jax.ShapeDtypeStruct((1, LANES), F32)],
        scratch_shapes=[pltpu.VMEM((1, LANES), F32)],
        compiler_params=_cparams(("arbitrary",)),
        name="merge",
    )(ys, u, ya, gs, ga, x2, ssm_d.reshape(1, sw), w_glu.astype(BF16), w_sb.astype(BF16),
      w_ab.astype(BF16), w_out.astype(BF16), post_g.reshape(1, d), gt_m, pre_g.reshape(1, d),
      sc_f, sh_f, rw, rb)


def _dispatch_kernel(pad_start_ref, pad_cnt_ref, tail_ref, dest_ref, h_ref, xs_ref, zero_ref, sem, zsem,
                     *, tm, sub, bm, n_rows, ne):
    i = pl.program_id(0)

    def row(ref, r, n=1):
        return ref.at[pl.ds(pl.multiple_of(r * sub, sub), n * sub)]

    def zero_copies(do):
        def per_expert(e, _):
            start, cnt = pad_start_ref[e], pad_cnt_ref[e]
            for b in range(bm.bit_length() - 1):
                n = 1 << b

                @pl.when((cnt >> b) & 1 == 1)
                def _():
                    do(pltpu.make_async_copy(row(zero_ref, 0, n), row(xs_ref, start + (cnt & (n - 1)), n), zsem))
            return 0

        lax.fori_loop(0, ne, per_expert, 0)

        def per_block(j, _):
            r = tail_ref[0] + j * bm

            @pl.when(r < n_rows)
            def _():
                do(pltpu.make_async_copy(row(zero_ref, 0, bm), row(xs_ref, r, bm), zsem))
            return 0

        lax.fori_loop(0, ne, per_block, 0)

    @pl.when(i == 0)
    def _():
        zero_ref[...] = jnp.zeros_like(zero_ref)
        zero_copies(lambda cp: cp.start())

    def issue(t, _):
        for k in range(TOP_K):
            pltpu.make_async_copy(row(h_ref, t), row(xs_ref, dest_ref[0, 0, t * TOP_K + k]), sem).start()
        return 0

    lax.fori_loop(0, tm, issue, 0, unroll=2)
    for _ in range(TOP_K):
        pltpu.make_async_copy(h_ref, row(xs_ref, 0, tm), sem).wait()

    @pl.when(i == 0)
    def _():
        zero_copies(lambda cp: cp.wait())


def _dispatch(h_slab, dest, pad_start, pad_cnt, tail, n_rows, sub, bm, tm):
    n_tok = dest.shape[0]
    ne = pad_start.shape[0]
    grid_spec = pltpu.PrefetchScalarGridSpec(
        num_scalar_prefetch=3,
        grid=(n_tok // tm,),
        in_specs=[pl.BlockSpec((1, 1, TOP_K * tm), lambda i, *_: (i, 0, 0), memory_space=pltpu.SMEM),
                  pl.BlockSpec((tm * sub, LANES), lambda i, *_: (i, 0))],
        out_specs=pl.BlockSpec(memory_space=pl.ANY),
        scratch_shapes=[pltpu.VMEM((bm * sub, LANES), F32), pltpu.SemaphoreType.DMA(()),
                        pltpu.SemaphoreType.DMA(())],
    )
    return pl.pallas_call(
        functools.partial(_dispatch_kernel, tm=tm, sub=sub, bm=bm, n_rows=n_rows, ne=ne),
        grid_spec=grid_spec,
        out_shape=jax.ShapeDtypeStruct((n_rows * sub, LANES), F32),
        compiler_params=_cparams(("arbitrary",)),
        name="dispatch",
    )(pad_start, pad_cnt, tail, dest.reshape(n_tok // tm, 1, TOP_K * tm), h_slab)


def _expert_kernel(be_ref, nv_ref, x_ref, wg_ref, bg_ref, wu_ref, bu_ref, wd_ref, bd_ref,
                   o_ref, wg_s, wu_s, wd_s, *, bm, sub):
    i = pl.program_id(0)
    prev = be_ref[jnp.maximum(i - 1, 0)]
    changed = (i == 0) | (be_ref[i] != prev)

    @pl.when(changed)
    def _():
        wg_s[...] = wg_ref[0].astype(BF16)
        wu_s[...] = wu_ref[0].astype(BF16)
        wd_s[...] = wd_ref[0].astype(BF16)

    @pl.when(i < nv_ref[0])
    def _():
        xb = _load_slab(x_ref, bm, sub).astype(BF16)
        g = _dot(xb, wg_s[...]) + bg_ref[0]
        u = _dot(xb, wu_s[...]) + bu_ref[0]
        g = jnp.minimum(g, SWIGLU_LIMIT)
        u = jnp.clip(u, -SWIGLU_LIMIT, SWIGLU_LIMIT)
        act = g * jax.nn.sigmoid(SWIGLU_ALPHA * g) * (u + 1.0)
        _store_slab(o_ref, _dot(act.astype(BF16), wd_s[...]) + bd_ref[0])

    @pl.when(i >= nv_ref[0])
    def _():
        o_ref[...] = jnp.zeros_like(o_ref)


def _experts(xs, blk_exp, n_valid, w_gate, b_gate, w_up, b_up, w_down, b_down, bm):
    ne, d, f = w_gate.shape
    sub = d // LANES
    n_blocks = xs.shape[0] // (bm * sub)
    wspec = lambda a, b: pl.BlockSpec((1, a, b), lambda i, be, nv: (be[i], 0, 0))
    grid_spec = pltpu.PrefetchScalarGridSpec(
        num_scalar_prefetch=2,
        grid=(n_blocks,),
        in_specs=[pl.BlockSpec((bm * sub, LANES), lambda i, be, nv: (jnp.clip(nv[0] - 1, 0, i), 0)),
                  wspec(d, f), wspec(1, f), wspec(d, f), wspec(1, f), wspec(f, d), wspec(1, d)],
        out_specs=pl.BlockSpec((bm * sub, LANES), lambda i, be, nv: (i, 0)),
        scratch_shapes=[pltpu.VMEM((d, f), BF16), pltpu.VMEM((d, f), BF16), pltpu.VMEM((f, d), BF16)],
    )
    return pl.pallas_call(
        functools.partial(_expert_kernel, bm=bm, sub=sub),
        grid_spec=grid_spec,
        out_shape=jax.ShapeDtypeStruct(xs.shape, F32),
        compiler_params=_cparams(("arbitrary",)),
        name="experts",
    )(blk_exp, n_valid, xs, w_gate, b_gate.reshape(ne, 1, f), w_up, b_up.reshape(ne, 1, f),
      w_down, b_down.reshape(ne, 1, d))


def _combine_kernel(cur_ref, nxt_ref, yb_ref, tw_ref, x1_ref, g_ref, gt_ref, o_ref, buf, sem, *, tm, sub):
    i = pl.program_id(0)
    n_copies = tm * TOP_K

    def start_tile(idx_ref, slot):
        def issue(r, _):
            src = yb_ref.at[pl.ds(pl.multiple_of(idx_ref[0, 0, r] * sub, sub), sub)]
            dst = buf.at[slot, pl.ds(pl.multiple_of(r * sub, sub), sub)]
            pltpu.make_async_copy(src, dst, sem.at[slot]).start()
            return 0

        lax.fori_loop(0, n_copies, issue, 0, unroll=8)

    slot = i % 2

    @pl.when(i == 0)
    def _():
        start_tile(cur_ref, 0)

    @pl.when(i + 1 < pl.num_programs(0))
    def _():
        start_tile(nxt_ref, 1 - slot)

    pltpu.make_async_copy(yb_ref.at[pl.ds(0, n_copies * sub)], buf.at[slot], sem.at[slot]).wait()
    tw = tw_ref[...]
    ff = jnp.zeros((tm, sub * LANES), F32)
    for k in range(TOP_K):
        rows_k = jnp.concatenate([buf[slot, pl.ds(k * tm * sub + s, tm, stride=sub), :] for s in range(sub)], axis=1)
        ff = ff + tw[:, k:k + 1] * rows_k
    o_ref[...] = x1_ref[...] + gt_ref[0] * _rms(ff, g_ref[...])


def _combine(yb, dest, tw, x1, post_g, gt_f, seq, tm):
    n_tok, d = x1.shape
    sub = d // LANES
    nt = n_tok // tm
    per_b = seq // tm
    dest_t = dest.reshape(nt, tm, TOP_K).transpose(0, 2, 1).reshape(nt, 1, TOP_K * tm)
    idx_spec = lambda f: pl.BlockSpec((1, 1, TOP_K * tm), f, memory_space=pltpu.SMEM)
    return pl.pallas_call(
        functools.partial(_combine_kernel, tm=tm, sub=sub),
        grid=(nt,),
        in_specs=[idx_spec(lambda i: (i, 0, 0)),
                  idx_spec(lambda i: (jnp.minimum(i + 1, nt - 1), 0, 0)),
                  pl.BlockSpec(memory_space=pl.ANY),
                  pl.BlockSpec((tm, LANES), lambda i: (i, 0)),
                  pl.BlockSpec((tm, d), lambda i: (i, 0)),
                  pl.BlockSpec((1, d), lambda i: (0, 0)),
                  pl.BlockSpec((1, 1, d), lambda i: (i // per_b, 0, 0))],
        out_specs=pl.BlockSpec((tm, d), lambda i: (i, 0)),
        out_shape=jax.ShapeDtypeStruct((n_tok, d), F32),
        scratch_shapes=[pltpu.VMEM((2, TOP_K * tm * sub, LANES), F32), pltpu.SemaphoreType.DMA((2,))],
        compiler_params=_cparams(("arbitrary",)),
        name="combine",
    )(dest_t, dest_t, yb, tw, x1, post_g.reshape(1, d), gt_f)


def kernel(x, c, positions, ada_w, ada_b, mix_pre_g, mix_post_g, ffn_pre_g, ffn_post_g, w_in, ssm_lam_re, ssm_lam_im, ssm_log_dt, ssm_b_re, ssm_b_im, ssm_c_re, ssm_c_im, ssm_d, ssm_w_glu, w_ssm_branch, w_attn_branch, w_out, router_w, router_b, w_gate, b_gate, w_up, b_up, w_down, b_down):
    bsz, seq, d = x.shape
    depth = ada_w.shape[0]
    n_tok = bsz * seq
    bm = EXPERT_ROWS
    sub = d // LANES
    xcur = x.reshape(n_tok, d)
    pos = positions.reshape(n_tok, 1).astype(I32)
    for l in range(depth):
        ada = _ada(c, ada_w[l], ada_b[l])
        sh_m, sc_m, gt_m, sh_f, sc_f, gt_f = [a.reshape(bsz, 1, d) for a in jnp.split(ada, 6, axis=-1)]

        u, q, k, v, gs, ga, km = _inproj(xcur, sc_m, sh_m, mix_pre_g[l].reshape(1, d), pos, w_in[l],
                                         seq, tm=min(512, seq))
        ys = _s5(u, ssm_lam_re[l], ssm_lam_im[l], ssm_log_dt[l], ssm_b_re[l], ssm_b_im[l],
                 ssm_c_re[l], ssm_c_im[l], bsz, seq)
        ya = _moba(q, k, v, km, bsz, seq)
        x1, h2, eid, tw, rank, cnt = _merge(ys, u, ya, gs, ga, xcur, ssm_d[l], ssm_w_glu[l], w_ssm_branch[l],
                                            w_attn_branch[l], w_out[l], mix_post_g[l], gt_m, ffn_pre_g[l],
                                            sc_f, sh_f, router_w[l], router_b[l], seq, tm=256)

        ne = router_w.shape[-1]
        counts = cnt[0, :ne].astype(I32)
        padded = (counts + bm - 1) // bm * bm
        p_ends = jnp.cumsum(padded)
        p_starts = p_ends - padded
        eid4 = eid[:, :TOP_K]
        start_of = jnp.sum(jnp.where(eid4[:, :, None] == jnp.arange(ne, dtype=I32), p_starts, 0), axis=-1)
        dest = start_of + rank[:, :TOP_K]
        n_blocks = (n_tok * TOP_K) // bm + ne
        n_rows = n_blocks * bm
        blk_start = jnp.arange(n_blocks, dtype=I32)[:, None] * bm
        blk_exp = jnp.minimum(jnp.sum((blk_start >= p_ends[None, :]).astype(I32), axis=1), ne - 1)
        n_valid = (p_ends[-1:] // bm).astype(I32)

        xs = _dispatch(h2, dest, p_starts + counts, padded - counts, p_ends[-1:], n_rows, sub, bm, tm=256)
        yb = _experts(xs, blk_exp, n_valid, w_gate[l], b_gate[l], w_up[l], b_up[l], w_down[l], b_down[l], bm)
        xcur = _combine(yb, dest, tw, x1, ffn_post_g[l], gt_f, seq, tm=128)
    return xcur.reshape(bsz, seq, d).astype(x.dtype)
```

```python
import functools
import math

import jax
import jax.numpy as jnp
from jax import lax
from jax.experimental import pallas as pl
from jax.experimental.pallas import tpu as pltpu

F32 = jnp.float32
BF16 = jnp.bfloat16
I32 = jnp.int32

N_HEADS = 8
HEAD_DIM = 64
ROPE_THETA = 10000.0
MOBA_BLOCK = 256
MOBA_TOPK = 3
SSM_GROUP_SIZE = 16
SSM_GROUPS = 32
SSM_STATE = 64
N_EXPERTS = 32
TOP_K = 4
SWIGLU_ALPHA = 1.702
SWIGLU_LIMIT = 7.0
NORM_EPS = 1e-6
NEG_INF = -1e30

LANES = 128
SSM_CHUNK = 64
EXPERT_ROWS = 512
VMEM_LIMIT = 56 * 1024 * 1024
HIGHEST = lax.Precision.HIGHEST
Q_SCALE = HEAD_DIM ** -0.5 * math.log2(math.e)


def _cparams(sem):
    return pltpu.CompilerParams(dimension_semantics=sem, vmem_limit_bytes=VMEM_LIMIT)


def _dot(a, b, **kw):
    return jnp.dot(a, b, preferred_element_type=F32, **kw)


def _dot_t(a, b, **kw):
    return lax.dot_general(a, b, (((1,), (1,)), ((), ())), preferred_element_type=F32, **kw)


def _ada_kernel(c_ref, w_ref, b_ref, o_ref):
    c = c_ref[...]
    cond = c * jax.nn.sigmoid(c)
    o_ref[...] = _dot(cond, w_ref[...], precision=HIGHEST) + b_ref[...]


def _ada(c, ada_w, ada_b):
    bsz, d = c.shape
    n = ada_w.shape[1]
    c8 = jnp.zeros((8, d), F32).at[:bsz].set(c)
    out = pl.pallas_call(
        _ada_kernel,
        grid=(n // d,),
        in_specs=[pl.BlockSpec((8, d), lambda j: (0, 0)),
                  pl.BlockSpec((d, d), lambda j: (0, j)),
                  pl.BlockSpec((1, d), lambda j: (0, j))],
        out_specs=pl.BlockSpec((8, d), lambda j: (0, j)),
        out_shape=jax.ShapeDtypeStruct((8, n), F32),
        compiler_params=_cparams(("arbitrary",)),
        name="ada",
    )(c8, ada_w, ada_b.reshape(1, n))
    return out[:bsz]


def _inproj_kernel(x_ref, sc_ref, sh_ref, g_ref, pos_ref, invf_ref, w_ref,
                   u_ref, q_ref, k_ref, v_ref, gs_ref, ga_ref, km_ref, *, ssm_w, attn_w, d_model):
    x = x_ref[...]
    ms = jnp.mean(x * x, axis=-1, keepdims=True)
    xn = x * lax.rsqrt(ms + NORM_EPS) * g_ref[...]
    h = (xn * (1.0 + sc_ref[0]) + sh_ref[0]).astype(BF16)

    def proj(lo, width):
        return _dot(h, w_ref[:, lo:lo + width])

    u_ref[...] = proj(0, ssm_w)

    ang = pos_ref[...].astype(F32) * invf_ref[...]
    reps = attn_w // LANES
    cos = jnp.concatenate([jnp.cos(ang)] * reps, axis=1)
    sin = jnp.concatenate([jnp.sin(ang)] * reps, axis=1)
    lane = lax.broadcasted_iota(I32, (1, attn_w), 1)
    first = (lane % HEAD_DIM) < (HEAD_DIM // 2)
    sin = jnp.where(first, -sin, sin)

    def rope(t):
        rot = jnp.where(first, pltpu.roll(t, attn_w - HEAD_DIM // 2, axis=1),
                        pltpu.roll(t, HEAD_DIM // 2, axis=1))
        return t * cos + rot * sin

    q = rope(proj(ssm_w, attn_w))
    q_ref[...] = (q * Q_SCALE).astype(BF16)
    k = rope(proj(ssm_w + attn_w, attn_w))
    k_ref[...] = k.astype(BF16)
    nblk = k.shape[0] // MOBA_BLOCK
    km_ref[0] = jnp.mean(k.reshape(nblk, MOBA_BLOCK, attn_w), axis=1)
    v_ref[...] = proj(ssm_w + 2 * attn_w, attn_w).astype(BF16)
    gs_ref[...] = jax.nn.sigmoid(proj(ssm_w + 3 * attn_w, d_model)).astype(BF16)
    ga_ref[...] = jax.nn.sigmoid(proj(ssm_w + 3 * attn_w + d_model, d_model)).astype(BF16)


def _inproj(x2, sc, sh, g, pos, w_in, seq, tm):
    n_tok, d = x2.shape
    attn_w = N_HEADS * HEAD_DIM
    ssm_w = SSM_GROUPS * SSM_GROUP_SIZE
    in_w = w_in.shape[1]
    half = HEAD_DIM // 2
    inv_freq = ROPE_THETA ** (-jnp.arange(half, dtype=F32) / half)
    invf = jnp.tile(inv_freq, LANES // half).reshape(1, LANES)
    nt = n_tok // tm
    per_b = seq // tm
    nblk = tm // MOBA_BLOCK
    tok = lambda w: pl.BlockSpec((tm, w), lambda i: (i, 0))
    bvec = pl.BlockSpec((1, 1, d), lambda i: (i // per_b, 0, 0))
    outs = pl.pallas_call(
        functools.partial(_inproj_kernel, ssm_w=ssm_w, attn_w=attn_w, d_model=d),
        grid=(nt,),
        in_specs=[tok(d), bvec, bvec,
                  pl.BlockSpec((1, d), lambda i: (0, 0)),
                  tok(1),
                  pl.BlockSpec((1, LANES), lambda i: (0, 0)),
                  pl.BlockSpec((d, in_w), lambda i: (0, 0))],
        out_specs=[tok(ssm_w), tok(attn_w), tok(attn_w), tok(attn_w), tok(d), tok(d),
                   pl.BlockSpec((1, nblk, attn_w), lambda i: (i, 0, 0))],
        out_shape=[jax.ShapeDtypeStruct((n_tok, ssm_w), F32),
                   jax.ShapeDtypeStruct((n_tok, attn_w), BF16),
                   jax.ShapeDtypeStruct((n_tok, attn_w), BF16),
                   jax.ShapeDtypeStruct((n_tok, attn_w), BF16),
                   jax.ShapeDtypeStruct((n_tok, d), BF16),
                   jax.ShapeDtypeStruct((n_tok, d), BF16),
                   jax.ShapeDtypeStruct((nt, nblk, attn_w), F32)],
        compiler_params=_cparams(("arbitrary",)),
        name="inproj",
    )(x2, sc, sh, g, pos, invf, w_in.astype(BF16))
    return outs


def _s5_group(x, zr, zi, lr, li, ca, cb, ba, bb, t_ref, *, chunk, n_chunks):
    gs = SSM_GROUP_SIZE

    def powers(tau):
        mag = jnp.exp(tau * zr)
        return mag * jnp.cos(tau * zi), mag * jnp.sin(tau * zi)

    tau = lax.broadcasted_iota(I32, (chunk + 8, LANES), 0).astype(F32)
    e_re, e_im = powers(tau)
    lb_re, lb_im = e_re[1:2], e_im[1:2]
    den = lr * lr + li * li
    a, b = lb_re - 1.0, lb_im
    cf_re = (a * lr + b * li) / den
    cf_im = (b * lr - a * li) / den
    bri = cf_re * ba + cf_im * bb
    bri_sw = cf_re * bb - cf_im * ba

    cpow =(e_re[:chunk + 1, None, :] * ca[None] + e_im[:chunk + 1, None, :] * cb[None])
    cpow = cpow.reshape((chunk + 1) * gs, LANES)
    width = chunk * gs
    r = _dot_t(bri, cpow[:width], precision=HIGHEST)
    col = lax.broadcasted_iota(I32, (gs, width), 1)
    t_ref[0:gs, :] = r.astype(BF16)
    for j in range(1, chunk):
        shifted = jnp.where(col >= gs * j, pltpu.roll(r, gs * j, axis=1), 0.0)
        t_ref[gs * j:gs * (j + 1), :] = shifted.astype(BF16)

    y = _dot(x, t_ref[...])

    tau_rev = (chunk - 1) - lax.broadcasted_iota(I32, (chunk, LANES), 0)
    r_re, r_im = powers(tau_rev.astype(F32))
    bst = r_re[:, None, :] * bri[None] + r_im[:, None, :] * bri_sw[None]
    bst = bst.reshape(width, LANES).astype(BF16)
    s = _dot(x, bst)

    rows = s.shape[0]
    n_idx = lax.broadcasted_iota(I32, (rows, LANES), 0) % n_chunks
    lane = lax.broadcasted_iota(I32, (1, LANES), 1)
    half = LANES // 2

    def cmul(v, p_re, p_im):
        return v * p_re + pltpu.roll(v, half, axis=1) * jnp.where(lane < half, -p_im, p_im)

    sh = 1
    while sh < n_chunks:
        p_re, p_im = powers(jnp.full((1, LANES), float(chunk * sh), F32))
        prev = jnp.where(n_idx >= sh, pltpu.roll(s, sh, axis=0), 0.0)
        s = s + cmul(prev, p_re, p_im)
        sh *= 2
    s_in = jnp.where(n_idx >= 1, pltpu.roll(s, 1, axis=0), 0.0)
    return y + _dot_t(s_in.astype(BF16), cpow[gs:gs + width].astype(BF16))


def _s5_kernel(u_ref, sel_ref, zr_ref, zi_ref, lr_ref, li_ref, ca_ref, cb_ref, ba_ref, bb_ref,
               y_ref, x_scr, y_scr, t_ref, *, chunk, n_chunks, rows):
    gpt = LANES // SSM_GROUP_SIZE

    def plane(j):
        return pl.ds(j, rows, stride=chunk)

    for jt in range(chunk // gpt):
        planes = jnp.concatenate([u_ref[plane(jt * gpt + jj), :] for jj in range(gpt)], axis=1)
        grouped = _dot(planes.astype(BF16), sel_ref[...])
        for g in range(gpt):
            x_scr[g, :, jt * LANES:(jt + 1) * LANES] = grouped[:, g * LANES:(g + 1) * LANES].astype(BF16)

    def per_group(g, _):
        y = _s5_group(x_scr[g], zr_ref[g], zi_ref[g], lr_ref[g], li_ref[g], ca_ref[g], cb_ref[g], ba_ref[g],
                      bb_ref[g], t_ref, chunk=chunk, n_chunks=n_chunks)
        y_scr[g] = y.astype(BF16)
        return 0

    lax.fori_loop(0, gpt, per_group, 0)

    for jt in range(chunk // gpt):
        grouped = jnp.concatenate([y_scr[g, :, jt * LANES:(jt + 1) * LANES] for g in range(gpt)], axis=1)
        planes = _dot(grouped, sel_ref[...])
        for jj in range(gpt):
            y_ref[plane(jt * gpt + jj), :] = planes[:, jj * LANES:(jj + 1) * LANES]


def _s5(u, lam_re, lam_im, log_dt, b_re, b_im, c_re, c_im, bsz, seq):
    g_n, gs, p = SSM_GROUPS, SSM_GROUP_SIZE, SSM_STATE
    chunk = SSM_CHUNK
    n_chunks = seq // chunk
    rows = bsz * n_chunks
    width = chunk * gs
    gpt = LANES // gs
    assert chunk % gpt == 0 and g_n % gpt == 0 and 2 * p == LANES
    dt = jnp.exp(log_dt.astype(F32))[:, None]
    dup = lambda a: jnp.concatenate([a, a], axis=-1).reshape(g_n, 1, 2 * p)
    zr, zi = dup(lam_re * dt), dup(lam_im * dt)
    lr, li = dup(lam_re), dup(lam_im)
    bt_re, bt_im = b_re.transpose(0, 2, 1), b_im.transpose(0, 2, 1)
    ba = jnp.concatenate([bt_re, bt_im], axis=-1)
    bb = jnp.concatenate([-bt_im, bt_re], axis=-1)
    ca = jnp.concatenate([c_re, -c_im], axis=-1)
    cb = jnp.concatenate([-c_im, -c_re], axis=-1)
    idx = jnp.arange(gpt * LANES, dtype=I32)
    swapped = ((idx // gs) % gpt) * LANES + (idx // LANES) * gs + idx % gs
    sel = (swapped[:, None] == idx[None, :]).astype(BF16)
    tile = pl.BlockSpec((bsz * seq, LANES), lambda t: (0, t))
    vec = pl.BlockSpec((gpt, 1, 2 * p), lambda t: (t, 0, 0))
    mat = pl.BlockSpec((gpt, gs, 2 * p), lambda t: (t, 0, 0))
    return pl.pallas_call(
        functools.partial(_s5_kernel, chunk=chunk, n_chunks=n_chunks, rows=rows),
        grid=(g_n // gpt,),
        in_specs=[tile, pl.BlockSpec((gpt * LANES, gpt * LANES), lambda t: (0, 0)),
                  vec, vec, vec, vec, mat, mat, mat, mat],
        out_specs=tile,
        out_shape=jax.ShapeDtypeStruct((bsz * seq, g_n * gs), F32),
        scratch_shapes=[pltpu.VMEM((gpt, rows, width), BF16), pltpu.VMEM((gpt, rows, width), BF16),
                        pltpu.VMEM((width, width), BF16)],
        compiler_params=_cparams(("arbitrary",)),
        name="s5",
    )(u, sel, zr, zi, lr, li, ca, cb, ba, bb)


def _moba_kernel(qt_ref, k_ref, vt_ref, km_ref, o_ref, s_a, s_b, s_own, *, group, nb, nbp):
    blk = MOBA_BLOCK
    keys = group * blk
    qb = pl.program_id(2)
    qt = qt_ref[...]
    km = km_ref[...]
    km_hi = km.astype(BF16)
    km_lo = (km - km_hi.astype(F32)).astype(BF16)
    dim_i = lax.broadcasted_iota(I32, (LANES, blk), 0)
    blk_i = lax.broadcasted_iota(I32, (nbp, blk), 0)
    blk_f = blk_i.astype(F32)
    valid = blk_i < qb

    q_aug = []
    for h in range(2):
        qh = jnp.where((dim_i >= h * HEAD_DIM) & (dim_i < (h + 1) * HEAD_DIM), qt, jnp.zeros_like(qt))
        g = jnp.where(valid, _dot(km_hi, qh) + _dot(km_lo, qh), NEG_INF)
        sel = jnp.zeros((nbp, blk), jnp.bool_)
        for _ in range(MOBA_TOPK):
            m = jnp.max(g, axis=0, keepdims=True)
            idx = jnp.min(jnp.where(g == m, blk_f, float(nbp)), axis=0, keepdims=True)
            pick = blk_f == idx
            sel = sel | pick
            g = jnp.where(pick, -jnp.inf, g)
        parts = [qh, jnp.where(sel & valid, 0.0, NEG_INF).astype(BF16)]
        if nbp < LANES:
            parts.append(jnp.full((LANES - nbp, blk), NEG_INF, BF16))
        q_aug.append(jnp.concatenate(parts, axis=0))

    def scores(grp, s_ref):
        kb0 = grp * group
        k_cat = k_ref[pl.ds(jnp.minimum(kb0, nb - group), group)].reshape(keys, LANES)
        blk_id = kb0 + lax.broadcasted_iota(I32, (keys, LANES), 0) // blk
        onehot = jnp.where(lax.broadcasted_iota(I32, (keys, LANES), 1) == blk_id, 1.0, 0.0).astype(BF16)
        k_aug = jnp.concatenate([k_cat, onehot], axis=1)
        for h in range(2):
            s_ref[h] = _dot(k_aug, q_aug[h])

    def softmax_pv(s, vt, state):
        m_i, l_i, acc = state
        m_new = jnp.maximum(m_i, jnp.max(s, axis=0, keepdims=True))
        alpha = jnp.exp2(m_i - m_new)
        p = jnp.exp2(s - m_new).astype(BF16)
        ones = jnp.ones((16, vt.shape[1]), BF16)
        pv = _dot(jnp.concatenate([vt, ones], axis=0), p)
        return m_new, alpha * l_i + pv[HEAD_DIM:HEAD_DIM + 1], alpha * acc + pv[:HEAD_DIM]

    def consume(grp, s_ref, st):
        out = []
        for h in range(2):
            vt = jnp.concatenate([vt_ref[jnp.minimum(grp * group + j, nb - 1), h * HEAD_DIM:(h + 1) * HEAD_DIM, :]
                                  for j in range(group)], axis=1)
            out.append(softmax_pv(s_ref[h], vt, st[h]))
        return tuple(out)

    k_own = jnp.concatenate([k_ref[qb], jnp.zeros((blk, LANES), BF16)], axis=1)
    for h in range(2):
        s_own[h] = _dot(k_own, q_aug[h])
    scores(0, s_a)

    def body(t, st):
        scores(2 * t + 1, s_b)
        st = consume(2 * t, s_a, st)
        scores(2 * t + 2, s_a)
        return consume(2 * t + 1, s_b, st)

    n_groups = (qb + group - 1) // group
    init = (jnp.full((1, blk), -jnp.inf, F32), jnp.zeros((1, blk), F32), jnp.zeros((HEAD_DIM, blk), F32))
    st = lax.fori_loop(0, (n_groups + 1) // 2, body, (init, init))
    key_i = lax.broadcasted_iota(I32, (blk, blk), 0)
    qry_i = lax.broadcasted_iota(I32, (blk, blk), 1)
    for h in range(2):
        s = jnp.where(key_i <= qry_i, s_own[h], NEG_INF)
        _, l_i, acc = softmax_pv(s, vt_ref[qb, h * HEAD_DIM:(h + 1) * HEAD_DIM, :], st[h])
        o_ref[h * HEAD_DIM:(h + 1) * HEAD_DIM, :] = (acc / l_i).astype(BF16)


def _moba(q, k, v, km, bsz, seq):
    n_tok, attn_w = q.shape
    blk = MOBA_BLOCK
    nb = seq // blk
    group = 2 if nb % 2 == 0 else 1
    nbp = -(-nb // 16) * 16
    assert nb + 3 * group <= LANES
    hp = attn_w // LANES
    km_pad = jnp.zeros((bsz, nbp, attn_w), F32).at[:, :nb].set(km.reshape(bsz, nb, attn_w))
    qt = q.reshape(bsz, seq, attn_w).transpose(0, 2, 1)
    k4 = k.reshape(bsz, nb, blk, attn_w)
    vt4 = v.reshape(bsz, nb, blk, attn_w).transpose(0, 1, 3, 2)
    ot = pl.pallas_call(
        functools.partial(_moba_kernel, group=group, nb=nb, nbp=nbp),
        grid=(bsz, hp, nb),
        in_specs=[pl.BlockSpec((None, LANES, blk), lambda b, p, i: (b, p, i)),
                  pl.BlockSpec((None, nb, blk, LANES), lambda b, p, i: (b, 0, 0, p)),
                  pl.BlockSpec((None, nb, LANES, blk), lambda b, p, i: (b, 0, p, 0)),
                  pl.BlockSpec((None, nbp, LANES), lambda b, p, i: (b, 0, p))],
        out_specs=pl.BlockSpec((None, LANES, blk), lambda b, p, i: (b, p, i)),
        out_shape=jax.ShapeDtypeStruct((bsz, attn_w, seq), BF16),
        scratch_shapes=[pltpu.VMEM((2, group * blk, blk), F32), pltpu.VMEM((2, group * blk, blk), F32),
                        pltpu.VMEM((2, blk, blk), F32)],
        compiler_params=_cparams(("arbitrary", "arbitrary", "arbitrary")),
        name="moba",
    )(qt, k4, vt4, km_pad)
    return ot.transpose(0, 2, 1).reshape(n_tok, attn_w)


def _rms(t, g):
    return t * lax.rsqrt(jnp.mean(t * t, axis=-1, keepdims=True) + NORM_EPS) * g


def _store_slab(ref, val, base=0):
    rows, d = val.shape
    sub = d // LANES
    for s in range(sub):
        ref[pl.ds(base + s, rows, stride=sub), :] = val[:, s * LANES:(s + 1) * LANES]


def _load_slab(ref, rows, sub, base=0):
    return jnp.concatenate([ref[pl.ds(base + s, rows, stride=sub), :] for s in range(sub)], axis=1)


def _merge_kernel(ys_ref, u_ref, ya_ref, gs_ref, ga_ref, x_ref, d_ref, wglu_ref, wsb_ref, wab_ref,
                  wout_ref, postg_ref, gtm_ref, preg_ref, scf_ref, shf_ref, rw_ref, rb_ref,
                  x1_ref, h2_ref, eid_ref, tw_ref, rank_ref, cnt_ref, run_ref):
    @pl.when(pl.program_id(0) == 0)
    def _():
        run_ref[...] = jnp.zeros_like(run_ref)

    y = ys_ref[...] + d_ref[...] * u_ref[...].astype(F32)
    y = jax.nn.gelu(y)
    y = y * jax.nn.sigmoid(_dot(y.astype(BF16), wglu_ref[...]))
    bs = _dot(y.astype(BF16), wsb_ref[...])
    ba = _dot(ya_ref[...], wab_ref[...])
    merged = gs_ref[...].astype(F32) * bs + ga_ref[...].astype(F32) * ba
    mix = _dot(merged.astype(BF16), wout_ref[...])
    x1 = x_ref[...] + gtm_ref[0] * _rms(mix, postg_ref[...])
    x1_ref[...] = x1
    h2 = _rms(x1, preg_ref[...]) * (1.0 + scf_ref[0]) + shf_ref[0]
    _store_slab(h2_ref, h2)

    h_hi = h2.astype(BF16)
    h_lo = (h2 - h_hi.astype(F32)).astype(BF16)
    logits = _dot(jnp.concatenate([h_hi, h_hi, h_lo], axis=1), rw_ref[...]) + rb_ref[...]
    tm = logits.shape[0]
    lane = lax.broadcasted_iota(I32, (tm, LANES), 1)
    lane_f = lane.astype(F32)
    g = logits
    vals, picks, idxs = [], [], []
    for _ in range(TOP_K):
        m = jnp.max(g, axis=1, keepdims=True)
        idx = jnp.min(jnp.where(g == m, lane_f, float(LANES)), axis=1, keepdims=True)
        pick = lane_f == idx
        vals.append(m)
        idxs.append(idx.astype(I32))
        picks.append(pick)
        g = jnp.where(pick, -jnp.inf, g)
    exps = [jnp.exp(v - vals[0]) for v in vals]
    tot = exps[0] + exps[1] + exps[2] + exps[3]

    onehot = jnp.where(picks[0] | picks[1] | picks[2] | picks[3], 1.0, 0.0)
    r_i = lax.broadcasted_iota(I32, (tm, tm), 0)
    c_i = lax.broadcasted_iota(I32, (tm, tm), 1)
    tri = jnp.where(c_i < r_i, 1.0, 0.0).astype(BF16)
    rank_full = _dot(tri, onehot.astype(BF16)) + run_ref[...]
    run_ref[...] = run_ref[...] + jnp.sum(onehot, axis=0, keepdims=True)
    cnt_ref[...] = run_ref[...]

    eid = jnp.zeros((tm, LANES), I32)
    tw = jnp.zeros((tm, LANES), F32)
    rk = jnp.zeros((tm, LANES), F32)
    for r in range(TOP_K):
        eid = jnp.where(lane == r, idxs[r], eid)
        tw = jnp.where(lane == r, exps[r] / tot, tw)
        pos = jnp.sum(jnp.where(picks[r], rank_full, 0.0), axis=1, keepdims=True)
        rk = jnp.where(lane == r, pos, rk)
    eid_ref[...] = eid
    tw_ref[...] = tw
    rank_ref[...] = rk.astype(I32)


def _merge(ys, u, ya, gs, ga, x2, ssm_d, w_glu, w_sb, w_ab, w_out, post_g, gt_m, pre_g, sc_f, sh_f,
           router_w, router_b, seq, tm):
    n_tok, d = x2.shape
    sw = ys.shape[1]
    aw = ya.shape[1]
    ne = router_w.shape[1]
    per_b = seq // tm
    rw = jnp.zeros((d, LANES), F32).at[:, :ne].set(router_w)
    rw_hi = rw.astype(BF16)
    rw_lo = (rw - rw_hi.astype(F32)).astype(BF16)
    rw = jnp.concatenate([rw_hi, rw_lo, rw_hi], axis=0)
    rb = jnp.full((1, LANES), NEG_INF, F32).at[0, :ne].set(router_b)
    tok = lambda w: pl.BlockSpec((tm, w), lambda i: (i, 0))
    full = lambda a, b: pl.BlockSpec((a, b), lambda i: (0, 0))
    bvec = pl.BlockSpec((1, 1, d), lambda i: (i // per_b, 0, 0))
    return pl.pallas_call(
        _merge_kernel,
        grid=(n_tok // tm,),
        in_specs=[tok(sw), tok(sw), tok(aw), tok(d), tok(d), tok(d),
                  full(1, sw), full(sw, sw), full(sw, d), full(aw, d), full(d, d),
                  full(1, d), bvec, full(1, d), bvec, bvec, full(3 * d, LANES), full(1, LANES)],
        out_specs=[tok(d), pl.BlockSpec((tm * (d // LANES), LANES), lambda i: (i, 0)),
                   tok(LANES), tok(LANES), tok(LANES), full(1, LANES)],
        out_shape=[jax.ShapeDtypeStruct((n_tok, d), F32),
                   jax.ShapeDtypeStruct((n_tok * (d // LANES), LANES), F32),
                   jax.ShapeDtypeStruct((n_tok, LANES), I32),
                   jax.ShapeDtypeStruct((n_tok, LANES), F32),
                   jax.ShapeDtypeStruct((n_tok, LANES), I32),
                   jax.ShapeDtypeStruct((1, LANES), F32)],
        scratch_shapes=[pltpu.VMEM((1, LANES), F32)],
        compiler_params=_cparams(("arbitrary",)),
        name="merge",
    )(ys, u, ya, gs, ga, x2, ssm_d.reshape(1, sw), w_glu.astype(BF16), w_sb.astype(BF16),
      w_ab.astype(BF16), w_out.astype(BF16), post_g.reshape(1, d), gt_m, pre_g.reshape(1, d),
      sc_f, sh_f, rw, rb)


def _dispatch_kernel(pad_start_ref, pad_cnt_ref, tail_ref, dest_ref, h_ref, xs_ref, zero_ref, sem, zsem,
                     *, tm, sub, bm, n_rows, ne):
    i = pl.program_id(0)

    def row(ref, r, n=1):
        return ref.at[pl.ds(pl.multiple_of(r * sub, sub), n * sub)]

    def zero_copies(do):
        def per_expert(e, _):
            start, cnt = pad_start_ref[e], pad_cnt_ref[e]
            for b in range(bm.bit_length() - 1):
                n = 1 << b

                @pl.when((cnt >> b) & 1 == 1)
                def _():
                    do(pltpu.make_async_copy(row(zero_ref, 0, n), row(xs_ref, start + (cnt & (n - 1)), n), zsem))
            return 0

        lax.fori_loop(0, ne, per_expert, 0)

        def per_block(j, _):
            r = tail_ref[0] + j * bm

            @pl.when(r < n_rows)
            def _():
                do(pltpu.make_async_copy(row(zero_ref, 0, bm), row(xs_ref, r, bm), zsem))
            return 0

        lax.fori_loop(0, ne, per_block, 0)

    @pl.when(i == 0)
    def _():
        zero_ref[...] = jnp.zeros_like(zero_ref)
        zero_copies(lambda cp: cp.start())

    def issue(t, _):
        for k in range(TOP_K):
            pltpu.make_async_copy(row(h_ref, t), row(xs_ref, dest_ref[0, 0, t * TOP_K + k]), sem).start()
        return 0

    lax.fori_loop(0, tm, issue, 0, unroll=2)
    for _ in range(TOP_K):
        pltpu.make_async_copy(h_ref, row(xs_ref, 0, tm), sem).wait()

    @pl.when(i == 0)
    def _():
        zero_copies(lambda cp: cp.wait())


def _dispatch(h_slab, dest, pad_start, pad_cnt, tail, n_rows, sub, bm, tm):
    n_tok = dest.shape[0]
    ne = pad_start.shape[0]
    grid_spec = pltpu.PrefetchScalarGridSpec(
        num_scalar_prefetch=3,
        grid=(n_tok // tm,),
        in_specs=[pl.BlockSpec((1, 1, TOP_K * tm), lambda i, *_: (i, 0, 0), memory_space=pltpu.SMEM),
                  pl.BlockSpec((tm * sub, LANES), lambda i, *_: (i, 0))],
        out_specs=pl.BlockSpec(memory_space=pl.ANY),
        scratch_shapes=[pltpu.VMEM((bm * sub, LANES), F32), pltpu.SemaphoreType.DMA(()),
                        pltpu.SemaphoreType.DMA(())],
    )
    return pl.pallas_call(
        functools.partial(_dispatch_kernel, tm=tm, sub=sub, bm=bm, n_rows=n_rows, ne=ne),
        grid_spec=grid_spec,
        out_shape=jax.ShapeDtypeStruct((n_rows * sub, LANES), F32),
        compiler_params=_cparams(("arbitrary",)),
        name="dispatch",
    )(pad_start, pad_cnt, tail, dest.reshape(n_tok // tm, 1, TOP_K * tm), h_slab)


def _expert_kernel(be_ref, nv_ref, x_ref, wg_ref, bg_ref, wu_ref, bu_ref, wd_ref, bd_ref,
                   o_ref, wg_s, wu_s, wd_s, *, bm, sub):
    i = pl.program_id(0)
    prev = be_ref[jnp.maximum(i - 1, 0)]
    changed = (i == 0) | (be_ref[i] != prev)

    @pl.when(changed)
    def _():
        wg_s[...] = wg_ref[0].astype(BF16)
        wu_s[...] = wu_ref[0].astype(BF16)
        wd_s[...] = wd_ref[0].astype(BF16)

    @pl.when(i < nv_ref[0])
    def _():
        xb = _load_slab(x_ref, bm, sub).astype(BF16)
        g = _dot(xb, wg_s[...]) + bg_ref[0]
        u = _dot(xb, wu_s[...]) + bu_ref[0]
        g = jnp.minimum(g, SWIGLU_LIMIT)
        u = jnp.clip(u, -SWIGLU_LIMIT, SWIGLU_LIMIT)
        act = g * jax.nn.sigmoid(SWIGLU_ALPHA * g) * (u + 1.0)
        _store_slab(o_ref, _dot(act.astype(BF16), wd_s[...]) + bd_ref[0])

    @pl.when(i >= nv_ref[0])
    def _():
        o_ref[...] = jnp.zeros_like(o_ref)


def _experts(xs, blk_exp, n_valid, w_gate, b_gate, w_up, b_up, w_down, b_down, bm):
    ne, d, f = w_gate.shape
    sub = d // LANES
    n_blocks = xs.shape[0] // (bm * sub)
    wspec = lambda a, b: pl.BlockSpec((1, a, b), lambda i, be, nv: (be[i], 0, 0))
    grid_spec = pltpu.PrefetchScalarGridSpec(
        num_scalar_prefetch=2,
        grid=(n_blocks,),
        in_specs=[pl.BlockSpec((bm * sub, LANES), lambda i, be, nv: (jnp.clip(nv[0] - 1, 0, i), 0)),
                  wspec(d, f), wspec(1, f), wspec(d, f), wspec(1, f), wspec(f, d), wspec(1, d)],
        out_specs=pl.BlockSpec((bm * sub, LANES), lambda i, be, nv: (i, 0)),
        scratch_shapes=[pltpu.VMEM((d, f), BF16), pltpu.VMEM((d, f), BF16), pltpu.VMEM((f, d), BF16)],
    )
    return pl.pallas_call(
        functools.partial(_expert_kernel, bm=bm, sub=sub),
        grid_spec=grid_spec,
        out_shape=jax.ShapeDtypeStruct(xs.shape, F32),
        compiler_params=_cparams(("arbitrary",)),
        name="experts",
    )(blk_exp, n_valid, xs, w_gate, b_gate.reshape(ne, 1, f), w_up, b_up.reshape(ne, 1, f),
      w_down, b_down.reshape(ne, 1, d))


def _combine_kernel(cur_ref, nxt_ref, yb_ref, tw_ref, x1_ref, g_ref, gt_ref, o_ref, buf, sem, *, tm, sub):
    i = pl.program_id(0)
    n_copies = tm * TOP_K

    def start_tile(idx_ref, slot):
        def issue(r, _):
            src = yb_ref.at[pl.ds(pl.multiple_of(idx_ref[0, 0, r] * sub, sub), sub)]
            dst = buf.at[slot, pl.ds(pl.multiple_of(r * sub, sub), sub)]
            pltpu.make_async_copy(src, dst, sem.at[slot]).start()
            return 0

        lax.fori_loop(0, n_copies, issue, 0, unroll=8)

    slot = i % 2

    @pl.when(i == 0)
    def _():
        start_tile(cur_ref, 0)

    @pl.when(i + 1 < pl.num_programs(0))
    def _():
        start_tile(nxt_ref, 1 - slot)

    pltpu.make_async_copy(yb_ref.at[pl.ds(0, n_copies * sub)], buf.at[slot], sem.at[slot]).wait()
    tw = tw_ref[...]
    ff = jnp.zeros((tm, sub * LANES), F32)
    for k in range(TOP_K):
        rows_k = jnp.concatenate([buf[slot, pl.ds(k * tm * sub + s, tm, stride=sub), :] for s in range(sub)], axis=1)
        ff = ff + tw[:, k:k + 1] * rows_k
    o_ref[...] = x1_ref[...] + gt_ref[0] * _rms(ff, g_ref[...])


def _combine(yb, dest, tw, x1, post_g, gt_f, seq, tm):
    n_tok, d = x1.shape
    sub = d // LANES
    nt = n_tok // tm
    per_b = seq // tm
    dest_t = dest.reshape(nt, tm, TOP_K).transpose(0, 2, 1).reshape(nt, 1, TOP_K * tm)
    idx_spec = lambda f: pl.BlockSpec((1, 1, TOP_K * tm), f, memory_space=pltpu.SMEM)
    return pl.pallas_call(
        functools.partial(_combine_kernel, tm=tm, sub=sub),
        grid=(nt,),
        in_specs=[idx_spec(lambda i: (i, 0, 0)),
                  idx_spec(lambda i: (jnp.minimum(i + 1, nt - 1), 0, 0)),
                  pl.BlockSpec(memory_space=pl.ANY),
                  pl.BlockSpec((tm, LANES), lambda i: (i, 0)),
                  pl.BlockSpec((tm, d), lambda i: (i, 0)),
                  pl.BlockSpec((1, d), lambda i: (0, 0)),
                  pl.BlockSpec((1, 1, d), lambda i: (i // per_b, 0, 0))],
        out_specs=pl.BlockSpec((tm, d), lambda i: (i, 0)),
        out_shape=jax.ShapeDtypeStruct((n_tok, d), F32),
        scratch_shapes=[pltpu.VMEM((2, TOP_K * tm * sub, LANES), F32), pltpu.SemaphoreType.DMA((2,))],
        compiler_params=_cparams(("arbitrary",)),
        name="combine",
    )(dest_t, dest_t, yb, tw, x1, post_g.reshape(1, d), gt_f)


def kernel(x, c, positions, ada_w, ada_b, mix_pre_g, mix_post_g, ffn_pre_g, ffn_post_g, w_in, ssm_lam_re, ssm_lam_im, ssm_log_dt, ssm_b_re, ssm_b_im, ssm_c_re, ssm_c_im, ssm_d, ssm_w_glu, w_ssm_branch, w_attn_branch, w_out, router_w, router_b, w_gate, b_gate, w_up, b_up, w_down, b_down):
    bsz, seq, d = x.shape
    depth = ada_w.shape[0]
    n_tok = bsz * seq
    bm = EXPERT_ROWS
    sub = d // LANES
    xcur = x.reshape(n_tok, d)
    pos = positions.reshape(n_tok, 1).astype(I32)
    for l in range(depth):
        ada = _ada(c, ada_w[l], ada_b[l])
        sh_m, sc_m, gt_m, sh_f, sc_f, gt_f = [a.reshape(bsz, 1, d) for a in jnp.split(ada, 6, axis=-1)]

        u, q, k, v, gs, ga, km = _inproj(xcur, sc_m, sh_m, mix_pre_g[l].reshape(1, d), pos, w_in[l],
                                         seq, tm=min(512, seq))
        ys = _s5(u, ssm_lam_re[l], ssm_lam_im[l], ssm_log_dt[l], ssm_b_re[l], ssm_b_im[l],
                 ssm_c_re[l], ssm_c_im[l], bsz, seq)
        ya = _moba(q, k, v, km, bsz, seq)
        x1, h2, eid, tw, rank, cnt = _merge(ys, u, ya, gs, ga, xcur, ssm_d[l], ssm_w_glu[l], w_ssm_branch[l],
                                            w_attn_branch[l], w_out[l], mix_post_g[l], gt_m, ffn_pre_g[l],
                                            sc_f, sh_f, router_w[l], router_b[l], seq, tm=256)

        ne = router_w.shape[-1]
        counts = cnt[0, :ne].astype(I32)
        padded = (counts + bm - 1) // bm * bm
        p_ends = jnp.cumsum(padded)
        p_starts = p_ends - padded
        eid4 = eid[:, :TOP_K]
        start_of = jnp.sum(jnp.where(eid4[:, :, None] == jnp.arange(ne, dtype=I32), p_starts, 0), axis=-1)
        dest = start_of + rank[:, :TOP_K]
        n_blocks = (n_tok * TOP_K) // bm + ne
        n_rows = n_blocks * bm
        blk_start = jnp.arange(n_blocks, dtype=I32)[:, None] * bm
        blk_exp = jnp.minimum(jnp.sum((blk_start >= p_ends[None, :]).astype(I32), axis=1), ne - 1)
        n_valid = (p_ends[-1:] // bm).astype(I32)

        xs = _dispatch(h2, dest, p_starts + counts, padded - counts, p_ends[-1:], n_rows, sub, bm, tm=256)
        yb = _experts(xs, blk_exp, n_valid, w_gate[l], b_gate[l], w_up[l], b_up[l], w_down[l], b_down[l], bm)
        xcur = _combine(yb, dest, tw, x1, ffn_post_g[l], gt_f, seq, tm=128)
    return xcur.reshape(bsz, seq, d).astype(x.dtype)
```

```python
import functools
import math

import jax
import jax.numpy as jnp
from jax import lax
from jax.experimental import pallas as pl
from jax.experimental.pallas import tpu as pltpu

F32 = jnp.float32
BF16 = jnp.bfloat16
I32 = jnp.int32

N_HEADS = 8
HEAD_DIM = 64
ROPE_THETA = 10000.0
MOBA_BLOCK = 256
MOBA_TOPK = 3
SSM_GROUP_SIZE = 16
SSM_GROUPS = 32
SSM_STATE = 64
N_EXPERTS = 32
TOP_K = 4
SWIGLU_ALPHA = 1.702
SWIGLU_LIMIT = 7.0
NORM_EPS = 1e-6
NEG_INF = -1e30

LANES = 128
SSM_CHUNK = 64
EXPERT_ROWS = 512
VMEM_LIMIT = 56 * 1024 * 1024
HIGHEST = lax.Precision.HIGHEST
Q_SCALE = HEAD_DIM ** -0.5 * math.log2(math.e)


def _cparams(sem):
    return pltpu.CompilerParams(dimension_semantics=sem, vmem_limit_bytes=VMEM_LIMIT)


def _dot(a, b, **kw):
    return jnp.dot(a, b, preferred_element_type=F32, **kw)


def _dot_t(a, b, **kw):
    return lax.dot_general(a, b, (((1,), (1,)), ((), ())), preferred_element_type=F32, **kw)


def _ada_kernel(c_ref, w_ref, b_ref, o_ref):
    c = c_ref[...]
    cond = c * jax.nn.sigmoid(c)
    o_ref[...] = _dot(cond, w_ref[...], precision=HIGHEST) + b_ref[...]


def _ada(c, ada_w, ada_b):
    bsz, d = c.shape
    n = ada_w.shape[1]
    c8 = jnp.zeros((8, d), F32).at[:bsz].set(c)
    out = pl.pallas_call(
        _ada_kernel,
        grid=(n // d,),
        in_specs=[pl.BlockSpec((8, d), lambda j: (0, 0)),
                  pl.BlockSpec((d, d), lambda j: (0, j)),
                  pl.BlockSpec((1, d), lambda j: (0, j))],
        out_specs=pl.BlockSpec((8, d), lambda j: (0, j)),
        out_shape=jax.ShapeDtypeStruct((8, n), F32),
        compiler_params=_cparams(("arbitrary",)),
        name="ada",
    )(c8, ada_w, ada_b.reshape(1, n))
    return out[:bsz]


def _inproj_kernel(x_ref, sc_ref, sh_ref, g_ref, pos_ref, invf_ref, w_ref,
                   u_ref, q_ref, k_ref, v_ref, gs_ref, ga_ref, km_ref, *, ssm_w, attn_w, d_model):
    x = x_ref[...]
    ms = jnp.mean(x * x, axis=-1, keepdims=True)
    xn = x * lax.rsqrt(ms + NORM_EPS) * g_ref[...]
    h = (xn * (1.0 + sc_ref[0]) + sh_ref[0]).astype(BF16)

    def proj(lo, width):
        return _dot(h, w_ref[:, lo:lo + width])

    u_ref[...] = proj(0, ssm_w)

    ang = pos_ref[...].astype(F32) * invf_ref[...]
    reps = attn_w // LANES
    cos = jnp.concatenate([jnp.cos(ang)] * reps, axis=1)
    sin = jnp.concatenate([jnp.sin(ang)] * reps, axis=1)
    lane = lax.broadcasted_iota(I32, (1, attn_w), 1)
    first = (lane % HEAD_DIM) < (HEAD_DIM // 2)
    sin = jnp.where(first, -sin, sin)

    def rope(t):
        rot = jnp.where(first, pltpu.roll(t, attn_w - HEAD_DIM // 2, axis=1),
                        pltpu.roll(t, HEAD_DIM // 2, axis=1))
        return t * cos + rot * sin

    q = rope(proj(ssm_w, attn_w))
    q_ref[...] = (q * Q_SCALE).astype(BF16)
    k = rope(proj(ssm_w + attn_w, attn_w))
    k_ref[...] = k.astype(BF16)
    nblk = k.shape[0] // MOBA_BLOCK
    km_ref[0] = jnp.mean(k.reshape(nblk, MOBA_BLOCK, attn_w), axis=1)
    v_ref[...] = proj(ssm_w + 2 * attn_w, attn_w).astype(BF16)
    gs_ref[...] = jax.nn.sigmoid(proj(ssm_w + 3 * attn_w, d_model)).astype(BF16)
    ga_ref[...] = jax.nn.sigmoid(proj(ssm_w + 3 * attn_w + d_model, d_model)).astype(BF16)


def _inproj(x2, sc, sh, g, pos, w_in, seq, tm):
    n_tok, d = x2.shape
    attn_w = N_HEADS * HEAD_DIM
    ssm_w = SSM_GROUPS * SSM_GROUP_SIZE
    in_w = w_in.shape[1]
    half = HEAD_DIM // 2
    inv_freq = ROPE_THETA ** (-jnp.arange(half, dtype=F32) / half)
    invf = jnp.tile(inv_freq, LANES // half).reshape(1, LANES)
    nt = n_tok // tm
    per_b = seq // tm
    nblk = tm // MOBA_BLOCK
    tok = lambda w: pl.BlockSpec((tm, w), lambda i: (i, 0))
    bvec = pl.BlockSpec((1, 1, d), lambda i: (i // per_b, 0, 0))
    outs = pl.pallas_call(
        functools.partial(_inproj_kernel, ssm_w=ssm_w, attn_w=attn_w, d_model=d),
        grid=(nt,),
        in_specs=[tok(d), bvec, bvec,
                  pl.BlockSpec((1, d), lambda i: (0, 0)),
                  tok(1),
                  pl.BlockSpec((1, LANES), lambda i: (0, 0)),
                  pl.BlockSpec((d, in_w), lambda i: (0, 0))],
        out_specs=[tok(ssm_w), tok(attn_w), tok(attn_w), tok(attn_w), tok(d), tok(d),
                   pl.BlockSpec((1, nblk, attn_w), lambda i: (i, 0, 0))],
        out_shape=[jax.ShapeDtypeStruct((n_tok, ssm_w), F32),
                   jax.ShapeDtypeStruct((n_tok, attn_w), BF16),
                   jax.ShapeDtypeStruct((n_tok, attn_w), BF16),
                   jax.ShapeDtypeStruct((n_tok, attn_w), BF16),
                   jax.ShapeDtypeStruct((n_tok, d), BF16),
                   jax.ShapeDtypeStruct((n_tok, d), BF16),
                   jax.ShapeDtypeStruct((nt, nblk, attn_w), F32)],
        compiler_params=_cparams(("arbitrary",)),
        name="inproj",
    )(x2, sc, sh, g, pos, invf, w_in.astype(BF16))
    return outs


def _s5_group(x, zr, zi, lr, li, ca, cb, ba, bb, t_ref, *, chunk, n_chunks):
    gs = SSM_GROUP_SIZE

    def powers(tau):
        mag = jnp.exp(tau * zr)
        return mag * jnp.cos(tau * zi), mag * jnp.sin(tau * zi)

    tau = lax.broadcasted_iota(I32, (chunk + 8, LANES), 0).astype(F32)
    e_re, e_im = powers(tau)
    lb_re, lb_im = e_re[1:2], e_im[1:2]
    den = lr * lr + li * li
    a, b = lb_re - 1.0, lb_im
    cf_re = (a * lr + b * li) / den
    cf_im = (b * lr - a * li) / den
    bri = cf_re * ba + cf_im * bb
    bri_sw = cf_re * bb - cf_im * ba

    cpow =(e_re[:chunk + 1, None, :] * ca[None] + e_im[:chunk + 1, None, :] * cb[None])
    cpow = cpow.reshape((chunk + 1) * gs, LANES)
    width = chunk * gs
    r = _dot_t(bri, cpow[:width], precision=HIGHEST)
    col = lax.broadcasted_iota(I32, (gs, width), 1)
    t_ref[0:gs, :] = r.astype(BF16)
    for j in range(1, chunk):
        shifted = jnp.where(col >= gs * j, pltpu.roll(r, gs * j, axis=1), 0.0)
        t_ref[gs * j:gs * (j + 1), :] = shifted.astype(BF16)

    y = _dot(x, t_ref[...])

    tau_rev = (chunk - 1) - lax.broadcasted_iota(I32, (chunk, LANES), 0)
    r_re, r_im = powers(tau_rev.astype(F32))
    bst = r_re[:, None, :] * bri[None] + r_im[:, None, :] * bri_sw[None]
    bst = bst.reshape(width, LANES).astype(BF16)
    s = _dot(x, bst)

    rows = s.shape[0]
    n_idx = lax.broadcasted_iota(I32, (rows, LANES), 0) % n_chunks
    lane = lax.broadcasted_iota(I32, (1, LANES), 1)
    half = LANES // 2

    def cmul(v, p_re, p_im):
        return v * p_re + pltpu.roll(v, half, axis=1) * jnp.where(lane < half, -p_im, p_im)

    sh = 1
    while sh < n_chunks:
        p_re, p_im = powers(jnp.full((1, LANES), float(chunk * sh), F32))
        prev = jnp.where(n_idx >= sh, pltpu.roll(s, sh, axis=0), 0.0)
        s = s + cmul(prev, p_re, p_im)
        sh *= 2
    s_in = jnp.where(n_idx >= 1, pltpu.roll(s, 1, axis=0), 0.0)
    return y + _dot_t(s_in.astype(BF16), cpow[gs:gs + width].astype(BF16))


def _s5_kernel(u_ref, sel_ref, zr_ref, zi_ref, lr_ref, li_ref, ca_ref, cb_ref, ba_ref, bb_ref,
               y_ref, x_scr, y_scr, t_ref, *, chunk, n_chunks, rows):
    gpt = LANES // SSM_GROUP_SIZE

    def plane(j):
        return pl.ds(j, rows, stride=chunk)

    for jt in range(chunk // gpt):
        planes = jnp.concatenate([u_ref[plane(jt * gpt + jj), :] for jj in range(gpt)], axis=1)
        grouped = _dot(planes.astype(BF16), sel_ref[...])
        for g in range(gpt):
            x_scr[g, :, jt * LANES:(jt + 1) * LANES] = grouped[:, g * LANES:(g + 1) * LANES].astype(BF16)

    def per_group(g, _):
        y = _s5_group(x_scr[g], zr_ref[g], zi_ref[g], lr_ref[g], li_ref[g], ca_ref[g], cb_ref[g], ba_ref[g],
                      bb_ref[g], t_ref, chunk=chunk, n_chunks=n_chunks)
        y_scr[g] = y.astype(BF16)
        return 0

    lax.fori_loop(0, gpt, per_group, 0)

    for jt in range(chunk // gpt):
        grouped = jnp.concatenate([y_scr[g, :, jt * LANES:(jt + 1) * LANES] for g in range(gpt)], axis=1)
        planes = _dot(grouped, sel_ref[...])
        for jj in range(gpt):
            y_ref[plane(jt * gpt + jj), :] = planes[:, jj * LANES:(jj + 1) * LANES]


def _s5(u, lam_re, lam_im, log_dt, b_re, b_im, c_re, c_im, bsz, seq):
    g_n, gs, p = SSM_GROUPS, SSM_GROUP_SIZE, SSM_STATE
    chunk = SSM_CHUNK
    n_chunks = seq // chunk
    rows = bsz * n_chunks
    width = chunk * gs
    gpt = LANES // gs
    assert chunk % gpt == 0 and g_n % gpt == 0 and 2 * p == LANES
    dt = jnp.exp(log_dt.astype(F32))[:, None]
    dup = lambda a: jnp.concatenate([a, a], axis=-1).reshape(g_n, 1, 2 * p)
    zr, zi = dup(lam_re * dt), dup(lam_im * dt)
    lr, li = dup(lam_re), dup(lam_im)
    bt_re, bt_im = b_re.transpose(0, 2, 1), b_im.transpose(0, 2, 1)
    ba = jnp.concatenate([bt_re, bt_im], axis=-1)
    bb = jnp.concatenate([-bt_im, bt_re], axis=-1)
    ca = jnp.concatenate([c_re, -c_im], axis=-1)
    cb = jnp.concatenate([-c_im, -c_re], axis=-1)
    idx = jnp.arange(gpt * LANES, dtype=I32)
    swapped = ((idx // gs) % gpt) * LANES + (idx // LANES) * gs + idx % gs
    sel = (swapped[:, None] == idx[None, :]).astype(BF16)
    tile = pl.BlockSpec((bsz * seq, LANES), lambda t: (0, t))
    vec = pl.BlockSpec((gpt, 1, 2 * p), lambda t: (t, 0, 0))
    mat = pl.BlockSpec((gpt, gs, 2 * p), lambda t: (t, 0, 0))
    return pl.pallas_call(
        functools.partial(_s5_kernel, chunk=chunk, n_chunks=n_chunks, rows=rows),
        grid=(g_n // gpt,),
        in_specs=[tile, pl.BlockSpec((gpt * LANES, gpt * LANES), lambda t: (0, 0)),
                  vec, vec, vec, vec, mat, mat, mat, mat],
        out_specs=tile,
        out_shape=jax.ShapeDtypeStruct((bsz * seq, g_n * gs), F32),
        scratch_shapes=[pltpu.VMEM((gpt, rows, width), BF16), pltpu.VMEM((gpt, rows, width), BF16),
                        pltpu.VMEM((width, width), BF16)],
        compiler_params=_cparams(("arbitrary",)),
        name="s5",
    )(u, sel, zr, zi, lr, li, ca, cb, ba, bb)


def _moba_kernel(qt_ref, k_ref, vt_ref, km_ref, o_ref, s_a, s_b, s_fin, p_a, p_b, *, nb, nbp):
    blk = MOBA_BLOCK
    group = 2
    keys = nq = group * blk
    i = pl.program_id(2)
    qt = qt_ref[...]
    km = km_ref[...]
    km_hi = km.astype(BF16)
    km_lo = (km - km_hi.astype(F32)).astype(BF16)
    dim_i = lax.broadcasted_iota(I32, (LANES, nq), 0)
    blk_i = lax.broadcasted_iota(I32, (nbp, nq), 0)
    blk_f = blk_i.astype(F32)
    own = group * i + lax.broadcasted_iota(I32, (nbp, nq), 1) // blk
    valid = blk_i < own

    q_loop, q_fin = [], []
    for h in range(2):
        qh = jnp.where((dim_i >= h * HEAD_DIM) & (dim_i < (h + 1) * HEAD_DIM), qt, jnp.zeros_like(qt))
        g = jnp.where(valid, _dot(km_hi, qh) + _dot(km_lo, qh), NEG_INF)
        sel = jnp.zeros((nbp, nq), jnp.bool_)
        for _ in range(MOBA_TOPK):
            m = jnp.max(g, axis=0, keepdims=True)
            idx = jnp.min(jnp.where(g == m, blk_f, float(nbp)), axis=0, keepdims=True)
            pick = blk_f == idx
            sel = sel | pick
            g = jnp.where(pick, -jnp.inf, g)
        allowed = sel & valid
        for lst, ok in ((q_loop, allowed & (blk_i < group * i)), (q_fin, allowed | (blk_i == own))):
            parts = [qh, jnp.where(ok, 0.0, NEG_INF).astype(BF16)]
            if nbp < LANES:
                parts.append(jnp.full((LANES - nbp, nq), NEG_INF, BF16))
            lst.append(jnp.concatenate(parts, axis=0))

    def scores(grp, s_ref, q_aug, causal=False):
        kb0 = grp * group
        k_cat = k_ref[pl.ds(jnp.minimum(kb0, nb - group), group)].reshape(keys, LANES)
        blk_id = kb0 + lax.broadcasted_iota(I32, (keys, LANES), 0) // blk
        onehot = jnp.where(lax.broadcasted_iota(I32, (keys, LANES), 1) == blk_id, 1.0, 0.0).astype(BF16)
        k_aug = jnp.concatenate([k_cat, onehot], axis=1)
        mx = []
        for h in range(2):
            s = _dot(k_aug, q_aug[h])
            if causal:
                key_i = lax.broadcasted_iota(I32, (keys, nq), 0)
                s = jnp.where(key_i <= lax.broadcasted_iota(I32, (keys, nq), 1), s, NEG_INF)
            s_ref[h] = s
            mx.append(jnp.max(s, axis=0, keepdims=True))
        return tuple(mx)

    def softmax(s_ref, mx, p_ref, st):
        out = []
        for h in range(2):
            m_i, _, l_i, acc = st[h]
            m_new = jnp.maximum(m_i, mx[h])
            p_ref[h] = jnp.exp2(s_ref[h] - m_new).astype(BF16)
            out.append((m_new, jnp.exp2(m_i - m_new), l_i, acc))
        return tuple(out)

    def values(grp, p_ref, st):
        out = []
        for h in range(2):
            m_i, alpha, l_i, acc = st[h]
            blocks = [vt_ref[jnp.clip(grp * group + j, 0, nb - 1), h * HEAD_DIM:(h + 1) * HEAD_DIM, :]
                      for j in range(group)]
            ones = jnp.ones((16, keys), BF16)
            pv = _dot(jnp.concatenate([jnp.concatenate(blocks, axis=1), ones], axis=0), p_ref[h])
            out.append((m_i, alpha, alpha * l_i + pv[HEAD_DIM:HEAD_DIM + 1], alpha * acc + pv[:HEAD_DIM]))
        return tuple(out)

    mx_fin = scores(i, s_fin, q_fin, causal=True)
    mx_first = scores(0, s_a, q_loop)
    p_b[...] = jnp.zeros_like(p_b)

    def body(t, carry):
        st, mx_a = carry
        mx_b = scores(2 * t + 1, s_b, q_loop)
        st = values(2 * t - 1, p_b, st)
        st = softmax(s_a, mx_a, p_a, st)
        mx_a = scores(2 * t + 2, s_a, q_loop)
        st = values(2 * t, p_a, st)
        return softmax(s_b, mx_b, p_b, st), mx_a

    init = (jnp.full((1, nq), -jnp.inf, F32), jnp.ones((1, nq), F32), jnp.zeros((1, nq), F32),
            jnp.zeros((HEAD_DIM, nq), F32))
    trips = (i + 1) // 2
    st, _ = lax.fori_loop(0, trips, body, ((init, init), mx_first))
    st = values(2 * trips - 1, p_b, st)
    st = values(i, p_a, softmax(s_fin, mx_fin, p_a, st))
    for h in range(2):
        _, _, l_i, acc = st[h]
        o_ref[h * HEAD_DIM:(h + 1) * HEAD_DIM, :] = (acc / l_i).astype(BF16)


def _moba(q, k, v, km, bsz, seq):
    n_tok, attn_w = q.shape
    blk = MOBA_BLOCK
    nb = seq // blk
    group = 2
    nbp = -(-nb // 16) * 16
    assert nb % group == 0 and nb + 2 * group <= LANES
    hp = attn_w // LANES
    km_pad = jnp.zeros((bsz, nbp, attn_w), F32).at[:, :nb].set(km.reshape(bsz, nb, attn_w))
    qt = q.reshape(bsz, seq, attn_w).transpose(0, 2, 1)
    k4 = k.reshape(bsz, nb, blk, attn_w)
    vt4 = v.reshape(bsz, nb, blk, attn_w).transpose(0, 1, 3, 2)
    ot = pl.pallas_call(
        functools.partial(_moba_kernel, nb=nb, nbp=nbp),
        grid=(bsz, hp, nb // group),
        in_specs=[pl.BlockSpec((None, LANES, group * blk), lambda b, p, i: (b, p, i)),
                  pl.BlockSpec((None, nb, blk, LANES), lambda b, p, i: (b, 0, 0, p)),
                  pl.BlockSpec((None, nb, LANES, blk), lambda b, p, i: (b, 0, p, 0)),
                  pl.BlockSpec((None, nbp, LANES), lambda b, p, i: (b, 0, p))],
        out_specs=pl.BlockSpec((None, LANES, group * blk), lambda b, p, i: (b, p, i)),
        out_shape=jax.ShapeDtypeStruct((bsz, attn_w, seq), BF16),
        scratch_shapes=([pltpu.VMEM((2, group * blk, group * blk), F32)] * 3
                        + [pltpu.VMEM((2, group * blk, group * blk), BF16)] * 2),
        compiler_params=_cparams(("arbitrary", "arbitrary", "arbitrary")),
        name="moba",
    )(qt, k4, vt4, km_pad)
    return ot.transpose(0, 2, 1).reshape(n_tok, attn_w)


def _rms(t, g):
    return t * lax.rsqrt(jnp.mean(t * t, axis=-1, keepdims=True) + NORM_EPS) * g


def _store_slab(ref, val, base=0):
    rows, d = val.shape
    sub = d // LANES
    for s in range(sub):
        ref[pl.ds(base + s, rows, stride=sub), :] = val[:, s * LANES:(s + 1) * LANES]


def _load_slab(ref, rows, sub, base=0):
    return jnp.concatenate([ref[pl.ds(base + s, rows, stride=sub), :] for s in range(sub)], axis=1)


def _merge_kernel(ys_ref, u_ref, ya_ref, gs_ref, ga_ref, x_ref, d_ref, wglu_ref, wsb_ref, wab_ref,
                  wout_ref, postg_ref, gtm_ref, preg_ref, scf_ref, shf_ref, rw_ref, rb_ref,
                  x1_ref, h2_ref, eid_ref, tw_ref, rank_ref, cnt_ref, run_ref):
    @pl.when(pl.program_id(0) == 0)
    def _():
        run_ref[...] = jnp.zeros_like(run_ref)

    y = ys_ref[...] + d_ref[...] * u_ref[...].astype(F32)
    y = jax.nn.gelu(y)
    y = y * jax.nn.sigmoid(_dot(y.astype(BF16), wglu_ref[...]))
    bs = _dot(y.astype(BF16), wsb_ref[...])
    ba = _dot(ya_ref[...], wab_ref[...])
    merged = gs_ref[...].astype(F32) * bs + ga_ref[...].astype(F32) * ba
    mix = _dot(merged.astype(BF16), wout_ref[...])
    x1 = x_ref[...] + gtm_ref[0] * _rms(mix, postg_ref[...])
    x1_ref[...] = x1
    h2 = _rms(x1, preg_ref[...]) * (1.0 + scf_ref[0]) + shf_ref[0]
    _store_slab(h2_ref, h2)

    h_hi = h2.astype(BF16)
    h_lo = (h2 - h_hi.astype(F32)).astype(BF16)
    logits = _dot(jnp.concatenate([h_hi, h_hi, h_lo], axis=1), rw_ref[...]) + rb_ref[...]
    tm = logits.shape[0]
    lane = lax.broadcasted_iota(I32, (tm, LANES), 1)
    lane_f = lane.astype(F32)
    g = logits
    vals, picks, idxs = [], [], []
    for _ in range(TOP_K):
        m = jnp.max(g, axis=1, keepdims=True)
        idx = jnp.min(jnp.where(g == m, lane_f, float(LANES)), axis=1, keepdims=True)
        pick = lane_f == idx
        vals.append(m)
        idxs.append(idx.astype(I32))
        picks.append(pick)
        g = jnp.where(pick, -jnp.inf, g)
    exps = [jnp.exp(v - vals[0]) for v in vals]
    tot = exps[0] + exps[1] + exps[2] + exps[3]

    onehot = jnp.where(picks[0] | picks[1] | picks[2] | picks[3], 1.0, 0.0)
    r_i = lax.broadcasted_iota(I32, (tm, tm), 0)
    c_i = lax.broadcasted_iota(I32, (tm, tm), 1)
    tri = jnp.where(c_i < r_i, 1.0, 0.0).astype(BF16)
    rank_full = _dot(tri, onehot.astype(BF16)) + run_ref[...]
    run_ref[...] = run_ref[...] + jnp.sum(onehot, axis=0, keepdims=True)
    cnt_ref[...] = run_ref[...]

    eid = jnp.zeros((tm, LANES), I32)
    tw = jnp.zeros((tm, LANES), F32)
    rk = jnp.zeros((tm, LANES), F32)
    for r in range(TOP_K):
        eid = jnp.where(lane == r, idxs[r], eid)
        tw = jnp.where(lane == r, exps[r] / tot, tw)
        pos = jnp.sum(jnp.where(picks[r], rank_full, 0.0), axis=1, keepdims=True)
        rk = jnp.where(lane == r, pos, rk)
    eid_ref[...] = eid
    tw_ref[...] = tw
    rank_ref[...] = rk.astype(I32)


def _merge(ys, u, ya, gs, ga, x2, ssm_d, w_glu, w_sb, w_ab, w_out, post_g, gt_m, pre_g, sc_f, sh_f,
           router_w, router_b, seq, tm):
    n_tok, d = x2.shape
    sw = ys.shape[1]
    aw = ya.shape[1]
    ne = router_w.shape[1]
    per_b = seq // tm
    rw = jnp.zeros((d, LANES), F32).at[:, :ne].set(router_w)
    rw_hi = rw.astype(BF16)
    rw_lo = (rw - rw_hi.astype(F32)).astype(BF16)
    rw = jnp.concatenate([rw_hi, rw_lo, rw_hi], axis=0)
    rb = jnp.full((1, LANES), NEG_INF, F32).at[0, :ne].set(router_b)
    tok = lambda w: pl.BlockSpec((tm, w), lambda i: (i, 0))
    full = lambda a, b: pl.BlockSpec((a, b), lambda i: (0, 0))
    bvec = pl.BlockSpec((1, 1, d), lambda i: (i // per_b, 0, 0))
    return pl.pallas_call(
        _merge_kernel,
        grid=(n_tok // tm,),
        in_specs=[tok(sw), tok(sw), tok(aw), tok(d), tok(d), tok(d),
                  full(1, sw), full(sw, sw), full(sw, d), full(aw, d), full(d, d),
                  full(1, d), bvec, full(1, d), bvec, bvec, full(3 * d, LANES), full(1, LANES)],
        out_specs=[tok(d), pl.BlockSpec((tm * (d // LANES), LANES), lambda i: (i, 0)),
                   tok(LANES), tok(LANES), tok(LANES), full(1, LANES)],
        out_shape=[jax.ShapeDtypeStruct((n_tok, d), F32),
                   jax.ShapeDtypeStruct((n_tok * (d // LANES), LANES), F32),
                   jax.ShapeDtypeStruct((n_tok, LANES), I32),
                   jax.ShapeDtypeStruct((n_tok, LANES), F32),
                   jax.ShapeDtypeStruct((n_tok, LANES), I32),
                   jax.ShapeDtypeStruct((1, LANES), F32)],
        scratch_shapes=[pltpu.VMEM((1, LANES), F32)],
        compiler_params=_cparams(("arbitrary",)),
        name="merge",
    )(ys, u, ya, gs, ga, x2, ssm_d.reshape(1, sw), w_glu.astype(BF16), w_sb.astype(BF16),
      w_ab.astype(BF16), w_out.astype(BF16), post_g.reshape(1, d), gt_m, pre_g.reshape(1, d),
      sc_f, sh_f, rw, rb)


def _dispatch_kernel(pad_start_ref, pad_cnt_ref, tail_ref, dest_ref, h_ref, xs_ref, zero_ref, sem, zsem,
                     *, tm, sub, bm, n_rows, ne):
    i = pl.program_id(0)

    def row(ref, r, n=1):
        return ref.at[pl.ds(pl.multiple_of(r * sub, sub), n * sub)]

    def zero_copies(do):
        def per_expert(e, _):
            start, cnt = pad_start_ref[e], pad_cnt_ref[e]
            for b in range(bm.bit_length() - 1):
                n = 1 << b

                @pl.when((cnt >> b) & 1 == 1)
                def _():
                    do(pltpu.make_async_copy(row(zero_ref, 0, n), row(xs_ref, start + (cnt & (n - 1)), n), zsem))
            return 0

        lax.fori_loop(0, ne, per_expert, 0)

        def per_block(j, _):
            r = tail_ref[0] + j * bm

            @pl.when(r < n_rows)
            def _():
                do(pltpu.make_async_copy(row(zero_ref, 0, bm), row(xs_ref, r, bm), zsem))
            return 0

        lax.fori_loop(0, ne, per_block, 0)

    @pl.when(i == 0)
    def _():
        zero_ref[...] = jnp.zeros_like(zero_ref)
        zero_copies(lambda cp: cp.start())

    def issue(t, _):
        for k in range(TOP_K):
            pltpu.make_async_copy(row(h_ref, t), row(xs_ref, dest_ref[0, 0, t * TOP_K + k]), sem).start()
        return 0

    lax.fori_loop(0, tm, issue, 0, unroll=2)
    for _ in range(TOP_K):
        pltpu.make_async_copy(h_ref, row(xs_ref, 0, tm), sem).wait()

    @pl.when(i == 0)
    def _():
        zero_copies(lambda cp: cp.wait())


def _dispatch(h_slab, dest, pad_start, pad_cnt, tail, n_rows, sub, bm, tm):
    n_tok = dest.shape[0]
    ne = pad_start.shape[0]
    grid_spec = pltpu.PrefetchScalarGridSpec(
        num_scalar_prefetch=3,
        grid=(n_tok // tm,),
        in_specs=[pl.BlockSpec((1, 1, TOP_K * tm), lambda i, *_: (i, 0, 0), memory_space=pltpu.SMEM),
                  pl.BlockSpec((tm * sub, LANES), lambda i, *_: (i, 0))],
        out_specs=pl.BlockSpec(memory_space=pl.ANY),
        scratch_shapes=[pltpu.VMEM((bm * sub, LANES), F32), pltpu.SemaphoreType.DMA(()),
                        pltpu.SemaphoreType.DMA(())],
    )
    return pl.pallas_call(
        functools.partial(_dispatch_kernel, tm=tm, sub=sub, bm=bm, n_rows=n_rows, ne=ne),
        grid_spec=grid_spec,
        out_shape=jax.ShapeDtypeStruct((n_rows * sub, LANES), F32),
        compiler_params=_cparams(("arbitrary",)),
        name="dispatch",
    )(pad_start, pad_cnt, tail, dest.reshape(n_tok // tm, 1, TOP_K * tm), h_slab)


def _expert_kernel(be_ref, nv_ref, x_ref, wg_ref, bg_ref, wu_ref, bu_ref, wd_ref, bd_ref,
                   o_ref, wg_s, wu_s, wd_s, *, bm, sub):
    i = pl.program_id(0)
    prev = be_ref[jnp.maximum(i - 1, 0)]
    changed = (i == 0) | (be_ref[i] != prev)

    @pl.when(changed)
    def _():
        wg_s[...] = wg_ref[0].astype(BF16)
        wu_s[...] = wu_ref[0].astype(BF16)
        wd_s[...] = wd_ref[0].astype(BF16)

    @pl.when(i < nv_ref[0])
    def _():
        xb = _load_slab(x_ref, bm, sub).astype(BF16)
        g = _dot(xb, wg_s[...]) + bg_ref[0]
        u = _dot(xb, wu_s[...]) + bu_ref[0]
        g = jnp.minimum(g, SWIGLU_LIMIT)
        u = jnp.clip(u, -SWIGLU_LIMIT, SWIGLU_LIMIT)
        act = g * jax.nn.sigmoid(SWIGLU_ALPHA * g) * (u + 1.0)
        _store_slab(o_ref, _dot(act.astype(BF16), wd_s[...]) + bd_ref[0])

    @pl.when(i >= nv_ref[0])
    def _():
        o_ref[...] = jnp.zeros_like(o_ref)


def _experts(xs, blk_exp, n_valid, w_gate, b_gate, w_up, b_up, w_down, b_down, bm):
    ne, d, f = w_gate.shape
    sub = d // LANES
    n_blocks = xs.shape[0] // (bm * sub)
    wspec = lambda a, b: pl.BlockSpec((1, a, b), lambda i, be, nv: (be[i], 0, 0))
    grid_spec = pltpu.PrefetchScalarGridSpec(
        num_scalar_prefetch=2,
        grid=(n_blocks,),
        in_specs=[pl.BlockSpec((bm * sub, LANES), lambda i, be, nv: (jnp.clip(nv[0] - 1, 0, i), 0)),
                  wspec(d, f), wspec(1, f), wspec(d, f), wspec(1, f), wspec(f, d), wspec(1, d)],
        out_specs=pl.BlockSpec((bm * sub, LANES), lambda i, be, nv: (i, 0)),
        scratch_shapes=[pltpu.VMEM((d, f), BF16), pltpu.VMEM((d, f), BF16), pltpu.VMEM((f, d), BF16)],
    )
    return pl.pallas_call(
        functools.partial(_expert_kernel, bm=bm, sub=sub),
        grid_spec=grid_spec,
        out_shape=jax.ShapeDtypeStruct(xs.shape, F32),
        compiler_params=_cparams(("arbitrary",)),
        name="experts",
    )(blk_exp, n_valid, xs, w_gate, b_gate.reshape(ne, 1, f), w_up, b_up.reshape(ne, 1, f),
      w_down, b_down.reshape(ne, 1, d))


def _combine_kernel(cur_ref, nxt_ref, yb_ref, tw_ref, x1_ref, g_ref, gt_ref, o_ref, buf, sem, *, tm, sub):
    i = pl.program_id(0)
    n_copies = tm * TOP_K

    def start_tile(idx_ref, slot):
        def issue(r, _):
            src = yb_ref.at[pl.ds(pl.multiple_of(idx_ref[0, 0, r] * sub, sub), sub)]
            dst = buf.at[slot, pl.ds(pl.multiple_of(r * sub, sub), sub)]
            pltpu.make_async_copy(src, dst, sem.at[slot]).start()
            return 0

        lax.fori_loop(0, n_copies, issue, 0, unroll=8)

    slot = i % 2

    @pl.when(i == 0)
    def _():
        start_tile(cur_ref, 0)

    @pl.when(i + 1 < pl.num_programs(0))
    def _():
        start_tile(nxt_ref, 1 - slot)

    pltpu.make_async_copy(yb_ref.at[pl.ds(0, n_copies * sub)], buf.at[slot], sem.at[slot]).wait()
    tw = tw_ref[...]
    ff = jnp.zeros((tm, sub * LANES), F32)
    for k in range(TOP_K):
        rows_k = jnp.concatenate([buf[slot, pl.ds(k * tm * sub + s, tm, stride=sub), :] for s in range(sub)], axis=1)
        ff = ff + tw[:, k:k + 1] * rows_k
    o_ref[...] = x1_ref[...] + gt_ref[0] * _rms(ff, g_ref[...])


def _combine(yb, dest, tw, x1, post_g, gt_f, seq, tm):
    n_tok, d = x1.shape
    sub = d // LANES
    nt = n_tok // tm
    per_b = seq // tm
    dest_t = dest.reshape(nt, tm, TOP_K).transpose(0, 2, 1).reshape(nt, 1, TOP_K * tm)
    idx_spec = lambda f: pl.BlockSpec((1, 1, TOP_K * tm), f, memory_space=pltpu.SMEM)
    return pl.pallas_call(
        functools.partial(_combine_kernel, tm=tm, sub=sub),
        grid=(nt,),
        in_specs=[idx_spec(lambda i: (i, 0, 0)),
                  idx_spec(lambda i: (jnp.minimum(i + 1, nt - 1), 0, 0)),
                  pl.BlockSpec(memory_space=pl.ANY),
                  pl.BlockSpec((tm, LANES), lambda i: (i, 0)),
                  pl.BlockSpec((tm, d), lambda i: (i, 0)),
                  pl.BlockSpec((1, d), lambda i: (0, 0)),
                  pl.BlockSpec((1, 1, d), lambda i: (i // per_b, 0, 0))],
        out_specs=pl.BlockSpec((tm, d), lambda i: (i, 0)),
        out_shape=jax.ShapeDtypeStruct((n_tok, d), F32),
        scratch_shapes=[pltpu.VMEM((2, TOP_K * tm * sub, LANES), F32), pltpu.SemaphoreType.DMA((2,))],
        compiler_params=_cparams(("arbitrary",)),
        name="combine",
    )(dest_t, dest_t, yb, tw, x1, post_g.reshape(1, d), gt_f)


def kernel(x, c, positions, ada_w, ada_b, mix_pre_g, mix_post_g, ffn_pre_g, ffn_post_g, w_in, ssm_lam_re, ssm_lam_im, ssm_log_dt, ssm_b_re, ssm_b_im, ssm_c_re, ssm_c_im, ssm_d, ssm_w_glu, w_ssm_branch, w_attn_branch, w_out, router_w, router_b, w_gate, b_gate, w_up, b_up, w_down, b_down):
    bsz, seq, d = x.shape
    depth = ada_w.shape[0]
    n_tok = bsz * seq
    bm = EXPERT_ROWS
    sub = d // LANES
    xcur = x.reshape(n_tok, d)
    pos = positions.reshape(n_tok, 1).astype(I32)
    for l in range(depth):
        ada = _ada(c, ada_w[l], ada_b[l])
        sh_m, sc_m, gt_m, sh_f, sc_f, gt_f = [a.reshape(bsz, 1, d) for a in jnp.split(ada, 6, axis=-1)]

        u, q, k, v, gs, ga, km = _inproj(xcur, sc_m, sh_m, mix_pre_g[l].reshape(1, d), pos, w_in[l],
                                         seq, tm=min(512, seq))
        ys = _s5(u, ssm_lam_re[l], ssm_lam_im[l], ssm_log_dt[l], ssm_b_re[l], ssm_b_im[l],
                 ssm_c_re[l], ssm_c_im[l], bsz, seq)
        ya = _moba(q, k, v, km, bsz, seq)
        x1, h2, eid, tw, rank, cnt = _merge(ys, u, ya, gs, ga, xcur, ssm_d[l], ssm_w_glu[l], w_ssm_branch[l],
                                            w_attn_branch[l], w_out[l], mix_post_g[l], gt_m, ffn_pre_g[l],
                                            sc_f, sh_f, router_w[l], router_b[l], seq, tm=256)

        ne = router_w.shape[-1]
        counts = cnt[0, :ne].astype(I32)
        padded = (counts + bm - 1) // bm * bm
        p_ends = jnp.cumsum(padded)
        p_starts = p_ends - padded
        eid4 = eid[:, :TOP_K]
        start_of = jnp.sum(jnp.where(eid4[:, :, None] == jnp.arange(ne, dtype=I32), p_starts, 0), axis=-1)
        dest = start_of + rank[:, :TOP_K]
        n_blocks = (n_tok * TOP_K) // bm + ne
        n_rows = n_blocks * bm
        blk_start = jnp.arange(n_blocks, dtype=I32)[:, None] * bm
        blk_exp = jnp.minimum(jnp.sum((blk_start >= p_ends[None, :]).astype(I32), axis=1), ne - 1)
        n_valid = (p_ends[-1:] // bm).astype(I32)

        xs = _dispatch(h2, dest, p_starts + counts, padded - counts, p_ends[-1:], n_rows, sub, bm, tm=256)
        yb = _experts(xs, blk_exp, n_valid, w_gate[l], b_gate[l], w_up[l], b_up[l], w_down[l], b_down[l], bm)
        xcur = _combine(yb, dest, tw, x1, ffn_post_g[l], gt_f, seq, tm=128)
    return xcur.reshape(bsz, seq, d).astype(x.dtype)
```

```python
import functools
import math

import jax
import jax.numpy as jnp
from jax import lax
from jax.experimental import pallas as pl
from jax.experimental.pallas import tpu as pltpu

F32 = jnp.float32
BF16 = jnp.bfloat16
I32 = jnp.int32

N_HEADS = 8
HEAD_DIM = 64
ROPE_THETA = 10000.0
MOBA_BLOCK = 256
MOBA_TOPK = 3
SSM_GROUP_SIZE = 16
SSM_GROUPS = 32
SSM_STATE = 64
N_EXPERTS = 32
TOP_K = 4
SWIGLU_ALPHA = 1.702
SWIGLU_LIMIT = 7.0
NORM_EPS = 1e-6
NEG_INF = -1e30

LANES = 128
SSM_CHUNK = 64
EXPERT_ROWS = 512
ROUTE_TILE = 256
VMEM_LIMIT = 56 * 1024 * 1024
HIGHEST = lax.Precision.HIGHEST
Q_SCALE = HEAD_DIM ** -0.5 * math.log2(math.e)


def _cparams(sem):
    return pltpu.CompilerParams(dimension_semantics=sem, vmem_limit_bytes=VMEM_LIMIT)


def _dot(a, b, **kw):
    return jnp.dot(a, b, preferred_element_type=F32, **kw)


def _dot_t(a, b, **kw):
    return lax.dot_general(a, b, (((1,), (1,)), ((), ())), preferred_element_type=F32, **kw)


def _ada_kernel(c_ref, w_ref, b_ref, o_ref):
    c = c_ref[...]
    cond = c * jax.nn.sigmoid(c)
    o_ref[...] = _dot(cond, w_ref[...], precision=HIGHEST) + b_ref[...]


def _ada(c, ada_w, ada_b):
    bsz, d = c.shape
    n = ada_w.shape[1]
    c8 = jnp.zeros((8, d), F32).at[:bsz].set(c)
    out = pl.pallas_call(
        _ada_kernel,
        grid=(n // d,),
        in_specs=[pl.BlockSpec((8, d), lambda j: (0, 0)),
                  pl.BlockSpec((d, d), lambda j: (0, j)),
                  pl.BlockSpec((1, d), lambda j: (0, j))],
        out_specs=pl.BlockSpec((8, d), lambda j: (0, j)),
        out_shape=jax.ShapeDtypeStruct((8, n), F32),
        compiler_params=_cparams(("arbitrary",)),
        name="ada",
    )(c8, ada_w, ada_b.reshape(1, n))
    return out[:bsz]


def _inproj_kernel(x_ref, sc_ref, sh_ref, g_ref, pos_ref, invf_ref, w_ref,
                   u_ref, q_ref, k_ref, v_ref, gs_ref, ga_ref, km_ref, *, ssm_w, attn_w, d_model):
    x = x_ref[...]
    ms = jnp.mean(x * x, axis=-1, keepdims=True)
    xn = x * lax.rsqrt(ms + NORM_EPS) * g_ref[...]
    h = (xn * (1.0 + sc_ref[0]) + sh_ref[0]).astype(BF16)

    def proj(lo, width):
        return _dot(h, w_ref[:, lo:lo + width])

    u_ref[...] = proj(0, ssm_w)

    ang = pos_ref[...].astype(F32) * invf_ref[...]
    reps = attn_w // LANES
    cos = jnp.concatenate([jnp.cos(ang)] * reps, axis=1)
    sin = jnp.concatenate([jnp.sin(ang)] * reps, axis=1)
    lane = lax.broadcasted_iota(I32, (1, attn_w), 1)
    first = (lane % HEAD_DIM) < (HEAD_DIM // 2)
    sin = jnp.where(first, -sin, sin)

    def rope(t):
        rot = jnp.where(first, pltpu.roll(t, attn_w - HEAD_DIM // 2, axis=1),
                        pltpu.roll(t, HEAD_DIM // 2, axis=1))
        return t * cos + rot * sin

    q = rope(proj(ssm_w, attn_w))
    q_ref[...] = (q * Q_SCALE).astype(BF16)
    k = rope(proj(ssm_w + attn_w, attn_w))
    k_ref[...] = k.astype(BF16)
    nblk = k.shape[0] // MOBA_BLOCK
    km_ref[0] = jnp.mean(k.reshape(nblk, MOBA_BLOCK, attn_w), axis=1)
    v_ref[...] = proj(ssm_w + 2 * attn_w, attn_w).astype(BF16)
    gs_ref[...] = jax.nn.sigmoid(proj(ssm_w + 3 * attn_w, d_model)).astype(BF16)
    ga_ref[...] = jax.nn.sigmoid(proj(ssm_w + 3 * attn_w + d_model, d_model)).astype(BF16)


def _inproj(x2, sc, sh, g, pos, w_in, seq, tm):
    n_tok, d = x2.shape
    attn_w = N_HEADS * HEAD_DIM
    ssm_w = SSM_GROUPS * SSM_GROUP_SIZE
    in_w = w_in.shape[1]
    half = HEAD_DIM // 2
    inv_freq = ROPE_THETA ** (-jnp.arange(half, dtype=F32) / half)
    invf = jnp.tile(inv_freq, LANES // half).reshape(1, LANES)
    nt = n_tok // tm
    per_b = seq // tm
    nblk = tm // MOBA_BLOCK
    tok = lambda w: pl.BlockSpec((tm, w), lambda i: (i, 0))
    bvec = pl.BlockSpec((1, 1, d), lambda i: (i // per_b, 0, 0))
    outs = pl.pallas_call(
        functools.partial(_inproj_kernel, ssm_w=ssm_w, attn_w=attn_w, d_model=d),
        grid=(nt,),
        in_specs=[tok(d), bvec, bvec,
                  pl.BlockSpec((1, d), lambda i: (0, 0)),
                  tok(1),
                  pl.BlockSpec((1, LANES), lambda i: (0, 0)),
                  pl.BlockSpec((d, in_w), lambda i: (0, 0))],
        out_specs=[tok(ssm_w), tok(attn_w), tok(attn_w), tok(attn_w), tok(d), tok(d),
                   pl.BlockSpec((1, nblk, attn_w), lambda i: (i, 0, 0))],
        out_shape=[jax.ShapeDtypeStruct((n_tok, ssm_w), F32),
                   jax.ShapeDtypeStruct((n_tok, attn_w), BF16),
                   jax.ShapeDtypeStruct((n_tok, attn_w), BF16),
                   jax.ShapeDtypeStruct((n_tok, attn_w), BF16),
                   jax.ShapeDtypeStruct((n_tok, d), BF16),
                   jax.ShapeDtypeStruct((n_tok, d), BF16),
                   jax.ShapeDtypeStruct((nt, nblk, attn_w), F32)],
        compiler_params=_cparams(("arbitrary",)),
        name="inproj",
    )(x2, sc, sh, g, pos, invf, w_in.astype(BF16))
    return outs


def _s5_group(x, zr, zi, lr, li, ca, cb, ba, bb, t_ref, *, chunk, n_chunks):
    gs = SSM_GROUP_SIZE

    def powers(tau):
        mag = jnp.exp(tau * zr)
        return mag * jnp.cos(tau * zi), mag * jnp.sin(tau * zi)

    tau = lax.broadcasted_iota(I32, (chunk + 8, LANES), 0).astype(F32)
    e_re, e_im = powers(tau)
    lb_re, lb_im = e_re[1:2], e_im[1:2]
    den = lr * lr + li * li
    a, b = lb_re - 1.0, lb_im
    cf_re = (a * lr + b * li) / den
    cf_im = (b * lr - a * li) / den
    bri = cf_re * ba + cf_im * bb
    bri_sw = cf_re * bb - cf_im * ba

    cpow =(e_re[:chunk + 1, None, :] * ca[None] + e_im[:chunk + 1, None, :] * cb[None])
    cpow = cpow.reshape((chunk + 1) * gs, LANES)
    width = chunk * gs
    r = _dot_t(bri, cpow[:width], precision=HIGHEST)
    col = lax.broadcasted_iota(I32, (gs, width), 1)
    t_ref[0:gs, :] = r.astype(BF16)
    for j in range(1, chunk):
        shifted = jnp.where(col >= gs * j, pltpu.roll(r, gs * j, axis=1), 0.0)
        t_ref[gs * j:gs * (j + 1), :] = shifted.astype(BF16)

    y = _dot(x, t_ref[...])

    tau_rev = (chunk - 1) - lax.broadcasted_iota(I32, (chunk, LANES), 0)
    r_re, r_im = powers(tau_rev.astype(F32))
    bst = r_re[:, None, :] * bri[None] + r_im[:, None, :] * bri_sw[None]
    bst = bst.reshape(width, LANES).astype(BF16)
    s = _dot(x, bst)

    rows = s.shape[0]
    n_idx = lax.broadcasted_iota(I32, (rows, LANES), 0) % n_chunks
    lane = lax.broadcasted_iota(I32, (1, LANES), 1)
    half = LANES // 2

    def cmul(v, p_re, p_im):
        return v * p_re + pltpu.roll(v, half, axis=1) * jnp.where(lane < half, -p_im, p_im)

    sh = 1
    while sh < n_chunks:
        p_re, p_im = powers(jnp.full((1, LANES), float(chunk * sh), F32))
        prev = jnp.where(n_idx >= sh, pltpu.roll(s, sh, axis=0), 0.0)
        s = s + cmul(prev, p_re, p_im)
        sh *= 2
    s_in = jnp.where(n_idx >= 1, pltpu.roll(s, 1, axis=0), 0.0)
    return y + _dot_t(s_in.astype(BF16), cpow[gs:gs + width].astype(BF16))


def _s5_kernel(u_ref, sel_ref, zr_ref, zi_ref, lr_ref, li_ref, ca_ref, cb_ref, ba_ref, bb_ref,
               y_ref, x_scr, y_scr, t_ref, *, chunk, n_chunks, rows):
    gpt = LANES // SSM_GROUP_SIZE

    def plane(j):
        return pl.ds(j, rows, stride=chunk)

    for jt in range(chunk // gpt):
        planes = jnp.concatenate([u_ref[plane(jt * gpt + jj), :] for jj in range(gpt)], axis=1)
        grouped = _dot(planes.astype(BF16), sel_ref[...])
        for g in range(gpt):
            x_scr[g, :, jt * LANES:(jt + 1) * LANES] = grouped[:, g * LANES:(g + 1) * LANES].astype(BF16)

    def per_group(g, _):
        y = _s5_group(x_scr[g], zr_ref[g], zi_ref[g], lr_ref[g], li_ref[g], ca_ref[g], cb_ref[g], ba_ref[g],
                      bb_ref[g], t_ref, chunk=chunk, n_chunks=n_chunks)
        y_scr[g] = y.astype(BF16)
        return 0

    lax.fori_loop(0, gpt, per_group, 0)

    for jt in range(chunk // gpt):
        grouped = jnp.concatenate([y_scr[g, :, jt * LANES:(jt + 1) * LANES] for g in range(gpt)], axis=1)
        planes = _dot(grouped, sel_ref[...])
        for jj in range(gpt):
            y_ref[plane(jt * gpt + jj), :] = planes[:, jj * LANES:(jj + 1) * LANES]


def _s5(u, lam_re, lam_im, log_dt, b_re, b_im, c_re, c_im, bsz, seq):
    g_n, gs, p = SSM_GROUPS, SSM_GROUP_SIZE, SSM_STATE
    chunk = SSM_CHUNK
    n_chunks = seq // chunk
    rows = bsz * n_chunks
    width = chunk * gs
    gpt = LANES // gs
    assert chunk % gpt == 0 and g_n % gpt == 0 and 2 * p == LANES
    dt = jnp.exp(log_dt.astype(F32))[:, None]
    dup = lambda a: jnp.concatenate([a, a], axis=-1).reshape(g_n, 1, 2 * p)
    zr, zi = dup(lam_re * dt), dup(lam_im * dt)
    lr, li = dup(lam_re), dup(lam_im)
    bt_re, bt_im = b_re.transpose(0, 2, 1), b_im.transpose(0, 2, 1)
    ba = jnp.concatenate([bt_re, bt_im], axis=-1)
    bb = jnp.concatenate([-bt_im, bt_re], axis=-1)
    ca = jnp.concatenate([c_re, -c_im], axis=-1)
    cb = jnp.concatenate([-c_im, -c_re], axis=-1)
    idx = jnp.arange(gpt * LANES, dtype=I32)
    swapped = ((idx // gs) % gpt) * LANES + (idx // LANES) * gs + idx % gs
    sel = (swapped[:, None] == idx[None, :]).astype(BF16)
    tile = pl.BlockSpec((bsz * seq, LANES), lambda t: (0, t))
    vec = pl.BlockSpec((gpt, 1, 2 * p), lambda t: (t, 0, 0))
    mat = pl.BlockSpec((gpt, gs, 2 * p), lambda t: (t, 0, 0))
    return pl.pallas_call(
        functools.partial(_s5_kernel, chunk=chunk, n_chunks=n_chunks, rows=rows),
        grid=(g_n // gpt,),
        in_specs=[tile, pl.BlockSpec((gpt * LANES, gpt * LANES), lambda t: (0, 0)),
                  vec, vec, vec, vec, mat, mat, mat, mat],
        out_specs=tile,
        out_shape=jax.ShapeDtypeStruct((bsz * seq, g_n * gs), F32),
        scratch_shapes=[pltpu.VMEM((gpt, rows, width), BF16), pltpu.VMEM((gpt, rows, width), BF16),
                        pltpu.VMEM((width, width), BF16)],
        compiler_params=_cparams(("arbitrary",)),
        name="s5",
    )(u, sel, zr, zi, lr, li, ca, cb, ba, bb)


def _moba_kernel(qt_ref, k_ref, vt_ref, km_ref, o_ref, s_a, s_b, s_fin, p_a, p_b, *, nb, nbp):
    blk = MOBA_BLOCK
    group = 2
    keys = nq = group * blk
    i = pl.program_id(2)
    qt = qt_ref[...]
    km = km_ref[...]
    km_hi = km.astype(BF16)
    km_lo = (km - km_hi.astype(F32)).astype(BF16)
    dim_i = lax.broadcasted_iota(I32, (LANES, nq), 0)
    blk_i = lax.broadcasted_iota(I32, (nbp, nq), 0)
    blk_f = blk_i.astype(F32)
    own = group * i + lax.broadcasted_iota(I32, (nbp, nq), 1) // blk
    valid = blk_i < own

    q_loop, q_fin = [], []
    for h in range(2):
        qh = jnp.where((dim_i >= h * HEAD_DIM) & (dim_i < (h + 1) * HEAD_DIM), qt, jnp.zeros_like(qt))
        g = jnp.where(valid, _dot(km_hi, qh) + _dot(km_lo, qh), NEG_INF)
        sel = jnp.zeros((nbp, nq), jnp.bool_)
        for _ in range(MOBA_TOPK):
            m = jnp.max(g, axis=0, keepdims=True)
            idx = jnp.min(jnp.where(g == m, blk_f, float(nbp)), axis=0, keepdims=True)
            pick = blk_f == idx
            sel = sel | pick
            g = jnp.where(pick, -jnp.inf, g)
        allowed = sel & valid
        for lst, ok in ((q_loop, allowed & (blk_i < group * i)), (q_fin, allowed | (blk_i == own))):
            parts = [qh, jnp.where(ok, 0.0, NEG_INF).astype(BF16)]
            if nbp < LANES:
                parts.append(jnp.full((LANES - nbp, nq), NEG_INF, BF16))
            lst.append(jnp.concatenate(parts, axis=0))

    def scores(grp, s_ref, q_aug, causal=False):
        kb0 = grp * group
        k_cat = k_ref[pl.ds(jnp.minimum(kb0, nb - group), group)].reshape(keys, LANES)
        blk_id = kb0 + lax.broadcasted_iota(I32, (keys, LANES), 0) // blk
        onehot = jnp.where(lax.broadcasted_iota(I32, (keys, LANES), 1) == blk_id, 1.0, 0.0).astype(BF16)
        k_aug = jnp.concatenate([k_cat, onehot], axis=1)
        mx = []
        for h in range(2):
            s = _dot(k_aug, q_aug[h])
            if causal:
                key_i = lax.broadcasted_iota(I32, (keys, nq), 0)
                s = jnp.where(key_i <= lax.broadcasted_iota(I32, (keys, nq), 1), s, NEG_INF)
            s_ref[h] = s
            mx.append(jnp.max(s, axis=0, keepdims=True))
        return tuple(mx)

    def softmax(s_ref, mx, p_ref, st):
        out = []
        for h in range(2):
            m_i, _, l_i, acc = st[h]
            m_new = jnp.maximum(m_i, mx[h])
            p_ref[h] = jnp.exp2(s_ref[h] - m_new).astype(BF16)
            out.append((m_new, jnp.exp2(m_i - m_new), l_i, acc))
        return tuple(out)

    def values(grp, p_ref, st):
        out = []
        for h in range(2):
            m_i, alpha, l_i, acc = st[h]
            blocks = [vt_ref[jnp.clip(grp * group + j, 0, nb - 1), h * HEAD_DIM:(h + 1) * HEAD_DIM, :]
                      for j in range(group)]
            ones = jnp.ones((16, keys), BF16)
            pv = _dot(jnp.concatenate([jnp.concatenate(blocks, axis=1), ones], axis=0), p_ref[h])
            out.append((m_i, alpha, alpha * l_i + pv[HEAD_DIM:HEAD_DIM + 1], alpha * acc + pv[:HEAD_DIM]))
        return tuple(out)

    mx_fin = scores(i, s_fin, q_fin, causal=True)
    mx_first = scores(0, s_a, q_loop)
    p_b[...] = jnp.zeros_like(p_b)

    def body(t, carry):
        st, mx_a = carry
        mx_b = scores(2 * t + 1, s_b, q_loop)
        st = values(2 * t - 1, p_b, st)
        st = softmax(s_a, mx_a, p_a, st)
        mx_a = scores(2 * t + 2, s_a, q_loop)
        st = values(2 * t, p_a, st)
        return softmax(s_b, mx_b, p_b, st), mx_a

    init = (jnp.full((1, nq), -jnp.inf, F32), jnp.ones((1, nq), F32), jnp.zeros((1, nq), F32),
            jnp.zeros((HEAD_DIM, nq), F32))
    trips = (i + 1) // 2
    st, _ = lax.fori_loop(0, trips, body, ((init, init), mx_first))
    st = values(2 * trips - 1, p_b, st)
    st = values(i, p_a, softmax(s_fin, mx_fin, p_a, st))
    for h in range(2):
        _, _, l_i, acc = st[h]
        o_ref[h * HEAD_DIM:(h + 1) * HEAD_DIM, :] = (acc / l_i).astype(BF16)


def _moba(q, k, v, km, bsz, seq):
    n_tok, attn_w = q.shape
    blk = MOBA_BLOCK
    nb = seq // blk
    group = 2
    nbp = -(-nb // 16) * 16
    assert nb % group == 0 and nb + 2 * group <= LANES
    hp = attn_w // LANES
    km_pad = jnp.zeros((bsz, nbp, attn_w), F32).at[:, :nb].set(km.reshape(bsz, nb, attn_w))
    qt = q.reshape(bsz, seq, attn_w).transpose(0, 2, 1)
    k4 = k.reshape(bsz, nb, blk, attn_w)
    vt4 = v.reshape(bsz, nb, blk, attn_w).transpose(0, 1, 3, 2)
    ot = pl.pallas_call(
        functools.partial(_moba_kernel, nb=nb, nbp=nbp),
        grid=(bsz, hp, nb // group),
        in_specs=[pl.BlockSpec((None, LANES, group * blk), lambda b, p, i: (b, p, i)),
                  pl.BlockSpec((None, nb, blk, LANES), lambda b, p, i: (b, 0, 0, p)),
                  pl.BlockSpec((None, nb, LANES, blk), lambda b, p, i: (b, 0, p, 0)),
                  pl.BlockSpec((None, nbp, LANES), lambda b, p, i: (b, 0, p))],
        out_specs=pl.BlockSpec((None, LANES, group * blk), lambda b, p, i: (b, p, i)),
        out_shape=jax.ShapeDtypeStruct((bsz, attn_w, seq), BF16),
        scratch_shapes=([pltpu.VMEM((2, group * blk, group * blk), F32)] * 3
                        + [pltpu.VMEM((2, group * blk, group * blk), BF16)] * 2),
        compiler_params=_cparams(("arbitrary", "arbitrary", "arbitrary")),
        name="moba",
    )(qt, k4, vt4, km_pad)
    return ot.transpose(0, 2, 1).reshape(n_tok, attn_w)


def _rms(t, g):
    return t * lax.rsqrt(jnp.mean(t * t, axis=-1, keepdims=True) + NORM_EPS) * g


def _store_slab(ref, val, base=0):
    rows, d = val.shape
    sub = d // LANES
    for s in range(sub):
        ref[pl.ds(base + s, rows, stride=sub), :] = val[:, s * LANES:(s + 1) * LANES]


def _load_slab(ref, rows, sub, base=0):
    return jnp.concatenate([ref[pl.ds(base + s, rows, stride=sub), :] for s in range(sub)], axis=1)


def _merge_kernel(ys_ref, u_ref, ya_ref, gs_ref, ga_ref, x_ref, d_ref, wglu_ref, wsb_ref, wab_ref,
                  wout_ref, postg_ref, gtm_ref, preg_ref, scf_ref, shf_ref, rw_ref, rb_ref,
                  x1_ref, h2_ref, eid_ref, tw_ref, rank_ref, before_ref, cnt_ref, run_ref):
    @pl.when(pl.program_id(0) == 0)
    def _():
        run_ref[...] = jnp.zeros_like(run_ref)

    y = ys_ref[...] + d_ref[...] * u_ref[...].astype(F32)
    y = jax.nn.gelu(y)
    y = y * jax.nn.sigmoid(_dot(y.astype(BF16), wglu_ref[...]))
    bs = _dot(y.astype(BF16), wsb_ref[...])
    ba = _dot(ya_ref[...], wab_ref[...])
    merged = gs_ref[...].astype(F32) * bs + ga_ref[...].astype(F32) * ba
    mix = _dot(merged.astype(BF16), wout_ref[...])
    x1 = x_ref[...] + gtm_ref[0] * _rms(mix, postg_ref[...])
    x1_ref[...] = x1
    h2 = _rms(x1, preg_ref[...]) * (1.0 + scf_ref[0]) + shf_ref[0]
    _store_slab(h2_ref, h2)

    h_hi = h2.astype(BF16)
    h_lo = (h2 - h_hi.astype(F32)).astype(BF16)
    logits = _dot(jnp.concatenate([h_hi, h_hi, h_lo], axis=1), rw_ref[...]) + rb_ref[...]
    tm = logits.shape[0]
    lane = lax.broadcasted_iota(I32, (tm, LANES), 1)
    lane_f = lane.astype(F32)
    g = logits
    vals, picks, idxs = [], [], []
    for _ in range(TOP_K):
        m = jnp.max(g, axis=1, keepdims=True)
        idx = jnp.min(jnp.where(g == m, lane_f, float(LANES)), axis=1, keepdims=True)
        pick = lane_f == idx
        vals.append(m)
        idxs.append(idx.astype(I32))
        picks.append(pick)
        g = jnp.where(pick, -jnp.inf, g)
    exps = [jnp.exp(v - vals[0]) for v in vals]
    tot = exps[0] + exps[1] + exps[2] + exps[3]

    onehot = jnp.where(picks[0] | picks[1] | picks[2] | picks[3], 1.0, 0.0)
    r_i = lax.broadcasted_iota(I32, (tm, tm), 0)
    c_i = lax.broadcasted_iota(I32, (tm, tm), 1)
    tri = jnp.where(c_i < r_i, 1.0, 0.0).astype(BF16)
    rank_full = _dot(tri, onehot.astype(BF16))
    before_ref[0] = run_ref[...]
    run_ref[...] = run_ref[...] + jnp.sum(onehot, axis=0, keepdims=True)
    cnt_ref[...] = run_ref[...]

    eid = jnp.zeros((tm, LANES), I32)
    tw = jnp.zeros((tm, LANES), F32)
    rk = jnp.zeros((tm, LANES), F32)
    for r in range(TOP_K):
        eid = jnp.where(lane == r, idxs[r], eid)
        tw = jnp.where(lane == r, exps[r] / tot, tw)
        pos = jnp.sum(jnp.where(picks[r], rank_full, 0.0), axis=1, keepdims=True)
        rk = jnp.where(lane == r, pos, rk)
    eid_ref[...] = eid
    tw_ref[...] = tw
    rank_ref[...] = rk.astype(I32)


def _merge(ys, u, ya, gs, ga, x2, ssm_d, w_glu, w_sb, w_ab, w_out, post_g, gt_m, pre_g, sc_f, sh_f,
           router_w, router_b, seq, tm):
    n_tok, d = x2.shape
    sw = ys.shape[1]
    aw = ya.shape[1]
    ne = router_w.shape[1]
    per_b = seq // tm
    rw = jnp.zeros((d, LANES), F32).at[:, :ne].set(router_w)
    rw_hi = rw.astype(BF16)
    rw_lo = (rw - rw_hi.astype(F32)).astype(BF16)
    rw = jnp.concatenate([rw_hi, rw_lo, rw_hi], axis=0)
    rb = jnp.full((1, LANES), NEG_INF, F32).at[0, :ne].set(router_b)
    tok = lambda w: pl.BlockSpec((tm, w), lambda i: (i, 0))
    full = lambda a, b: pl.BlockSpec((a, b), lambda i: (0, 0))
    bvec = pl.BlockSpec((1, 1, d), lambda i: (i // per_b, 0, 0))
    return pl.pallas_call(
        _merge_kernel,
        grid=(n_tok // tm,),
        in_specs=[tok(sw), tok(sw), tok(aw), tok(d), tok(d), tok(d),
                  full(1, sw), full(sw, sw), full(sw, d), full(aw, d), full(d, d),
                  full(1, d), bvec, full(1, d), bvec, bvec, full(3 * d, LANES), full(1, LANES)],
        out_specs=[tok(d), pl.BlockSpec((tm * (d // LANES), LANES), lambda i: (i, 0)),
                   tok(LANES), tok(LANES), tok(LANES),
                   pl.BlockSpec((1, 1, LANES), lambda i: (i, 0, 0)), full(1, LANES)],
        out_shape=[jax.ShapeDtypeStruct((n_tok, d), F32),
                   jax.ShapeDtypeStruct((n_tok * (d // LANES), LANES), F32),
                   jax.ShapeDtypeStruct((n_tok, LANES), I32),
                   jax.ShapeDtypeStruct((n_tok, LANES), F32),
                   jax.ShapeDtypeStruct((n_tok, LANES), I32),
                   jax.ShapeDtypeStruct((n_tok // tm, 1, LANES), F32),
                   jax.ShapeDtypeStruct((1, LANES), F32)],
        scratch_shapes=[pltpu.VMEM((1, LANES), F32)],
        compiler_params=_cparams(("arbitrary",)),
        name="merge",
    )(ys, u, ya, gs, ga, x2, ssm_d.reshape(1, sw), w_glu.astype(BF16), w_sb.astype(BF16),
      w_ab.astype(BF16), w_out.astype(BF16), post_g.reshape(1, d), gt_m, pre_g.reshape(1, d),
      sc_f, sh_f, rw, rb)


def _chunk_copies(cnt_ref, start_ref, tile, ne, max_rows, make):
    def per_expert(e, off):
        c, start = cnt_ref[tile * ne + e], start_ref[tile * ne + e]
        for b in range(max_rows.bit_length()):
            n = 1 << b

            @pl.when((c >> b) & 1 == 1)
            def _():
                make(off + (c & (n - 1)), start + (c & (n - 1)), n)
        return off + c

    lax.fori_loop(0, ne, per_expert, 0)


def _dispatch_kernel(cnt_ref, start_ref, pad_start_ref, pad_cnt_ref, tail_ref, loct_ref, h_ref, xs_ref,
                     buf, zero_ref, sem, zsem, *, tm, sub, bm, n_rows, ne):
    i = pl.program_id(0)
    last = pl.num_programs(0) - 1
    slot = i % 2
    n_loc = tm * TOP_K

    def row(ref, r, n=1):
        return ref.at[pl.ds(pl.multiple_of(r * sub, sub), n * sub)]

    def wait_slot(s):
        pltpu.make_async_copy(buf.at[s], row(xs_ref, 0, n_loc), sem.at[s]).wait()

    def zero_copies(do):
        def per_expert(e, _):
            start, cnt = pad_start_ref[e], pad_cnt_ref[e]
            for b in range(bm.bit_length() - 1):
                n = 1 << b

                @pl.when((cnt >> b) & 1 == 1)
                def _():
                    do(pltpu.make_async_copy(row(zero_ref, 0, n), row(xs_ref, start + (cnt & (n - 1)), n), zsem))
            return 0

        lax.fori_loop(0, ne, per_expert, 0)

        def per_block(j, _):
            r = tail_ref[0] + j * bm

            @pl.when(r < n_rows)
            def _():
                do(pltpu.make_async_copy(row(zero_ref, 0, bm), row(xs_ref, r, bm), zsem))
            return 0

        lax.fori_loop(0, ne, per_block, 0)

    @pl.when(i == 0)
    def _():
        zero_ref[...] = jnp.zeros_like(zero_ref)
        zero_copies(lambda cp: cp.start())

    @pl.when(i >= 2)
    def _():
        wait_slot(slot)

    loct = loct_ref[0]
    row_i = lax.broadcasted_iota(I32, (n_loc, tm), 0)
    place = row_i == loct[0:1, :]
    for k in range(1, TOP_K):
        place = place | (row_i == loct[k:k + 1, :])
    grouped = _dot(jnp.where(place, 1.0, 0.0).astype(BF16), _load_slab(h_ref, tm, sub).astype(BF16))
    _store_slab(buf.at[slot], grouped)
    _chunk_copies(cnt_ref, start_ref, i, ne, tm,
                  lambda loc, glob, n: pltpu.make_async_copy(row(buf.at[slot], loc, n), row(xs_ref, glob, n),
                                                             sem.at[slot]).start())

    @pl.when(i == last)
    def _():
        wait_slot(slot)

    @pl.when((i == last) & (i >= 1))
    def _():
        wait_slot(1 - slot)

    @pl.when(i == 0)
    def _():
        zero_copies(lambda cp: cp.wait())


def _dispatch(h_slab, loc, cnt_tile, start_tile, pad_start, pad_cnt, tail, n_rows, sub, bm, tm):
    n_tok = loc.shape[0]
    nt = n_tok // tm
    ne = pad_start.shape[0]
    loct = jnp.full((nt, 8, tm), -1, I32).at[:, :TOP_K].set(loc.reshape(nt, tm, TOP_K).transpose(0, 2, 1))
    grid_spec = pltpu.PrefetchScalarGridSpec(
        num_scalar_prefetch=5,
        grid=(nt,),
        in_specs=[pl.BlockSpec((1, 8, tm), lambda i, *_: (i, 0, 0)),
                  pl.BlockSpec((tm * sub, LANES), lambda i, *_: (i, 0))],
        out_specs=pl.BlockSpec(memory_space=pl.ANY),
        scratch_shapes=[pltpu.VMEM((2, TOP_K * tm * sub, LANES), F32), pltpu.VMEM((bm * sub, LANES), F32),
                        pltpu.SemaphoreType.DMA((2,)), pltpu.SemaphoreType.DMA(())],
    )
    return pl.pallas_call(
        functools.partial(_dispatch_kernel, tm=tm, sub=sub, bm=bm, n_rows=n_rows, ne=ne),
        grid_spec=grid_spec,
        out_shape=jax.ShapeDtypeStruct((n_rows * sub, LANES), F32),
        compiler_params=_cparams(("arbitrary",)),
        name="dispatch",
    )(cnt_tile.reshape(-1), start_tile.reshape(-1), pad_start, pad_cnt, tail, loct, h_slab)


def _expert_kernel(be_ref, nv_ref, x_ref, wg_ref, bg_ref, wu_ref, bu_ref, wd_ref, bd_ref,
                   o_ref, wg_s, wu_s, wd_s, *, bm, sub):
    i = pl.program_id(0)
    prev = be_ref[jnp.maximum(i - 1, 0)]
    changed = (i == 0) | (be_ref[i] != prev)

    @pl.when(changed)
    def _():
        wg_s[...] = wg_ref[0].astype(BF16)
        wu_s[...] = wu_ref[0].astype(BF16)
        wd_s[...] = wd_ref[0].astype(BF16)

    @pl.when(i < nv_ref[0])
    def _():
        xb = _load_slab(x_ref, bm, sub).astype(BF16)
        g = _dot(xb, wg_s[...]) + bg_ref[0]
        u = _dot(xb, wu_s[...]) + bu_ref[0]
        g = jnp.minimum(g, SWIGLU_LIMIT)
        u = jnp.clip(u, -SWIGLU_LIMIT, SWIGLU_LIMIT)
        act = g * jax.nn.sigmoid(SWIGLU_ALPHA * g) * (u + 1.0)
        _store_slab(o_ref, _dot(act.astype(BF16), wd_s[...]) + bd_ref[0])

    @pl.when(i >= nv_ref[0])
    def _():
        o_ref[...] = jnp.zeros_like(o_ref)


def _experts(xs, blk_exp, n_valid, w_gate, b_gate, w_up, b_up, w_down, b_down, bm):
    ne, d, f = w_gate.shape
    sub = d // LANES
    n_blocks = xs.shape[0] // (bm * sub)
    wspec = lambda a, b: pl.BlockSpec((1, a, b), lambda i, be, nv: (be[i], 0, 0))
    grid_spec = pltpu.PrefetchScalarGridSpec(
        num_scalar_prefetch=2,
        grid=(n_blocks,),
        in_specs=[pl.BlockSpec((bm * sub, LANES), lambda i, be, nv: (jnp.clip(nv[0] - 1, 0, i), 0)),
                  wspec(d, f), wspec(1, f), wspec(d, f), wspec(1, f), wspec(f, d), wspec(1, d)],
        out_specs=pl.BlockSpec((bm * sub, LANES), lambda i, be, nv: (i, 0)),
        scratch_shapes=[pltpu.VMEM((d, f), BF16), pltpu.VMEM((d, f), BF16), pltpu.VMEM((f, d), BF16)],
    )
    return pl.pallas_call(
        functools.partial(_expert_kernel, bm=bm, sub=sub),
        grid_spec=grid_spec,
        out_shape=jax.ShapeDtypeStruct(xs.shape, F32),
        compiler_params=_cparams(("arbitrary",)),
        name="experts",
    )(blk_exp, n_valid, xs, w_gate, b_gate.reshape(ne, 1, f), w_up, b_up.reshape(ne, 1, f),
      w_down, b_down.reshape(ne, 1, d))


def _combine_kernel(cnt_ref, start_ref, yb_ref, loc_ref, tw_ref, x1_ref, g_ref, gt_ref, o_ref, buf, sem,
                    *, tm, sub, ne):
    i = pl.program_id(0)
    n_loc = tm * TOP_K

    def start_tile(tile, slot):
        def make(loc, glob, n):
            src = yb_ref.at[pl.ds(pl.multiple_of(glob * sub, sub), n * sub)]
            dst = buf.at[slot, pl.ds(pl.multiple_of(loc * sub, sub), n * sub)]
            pltpu.make_async_copy(src, dst, sem.at[slot]).start()

        _chunk_copies(cnt_ref, start_ref, tile, ne, tm, make)

    slot = i % 2

    @pl.when(i == 0)
    def _():
        start_tile(0, 0)

    @pl.when(i + 1 < pl.num_programs(0))
    def _():
        start_tile(i + 1, 1 - slot)

    pltpu.make_async_copy(yb_ref.at[pl.ds(0, n_loc * sub)], buf.at[slot], sem.at[slot]).wait()
    rows = _load_slab(buf.at[slot], n_loc, sub).astype(BF16)
    loc, tw = loc_ref[...], tw_ref[...]
    col_i = lax.broadcasted_iota(I32, (tm, n_loc), 1)
    w = jnp.zeros((tm, n_loc), F32)
    for k in range(TOP_K):
        w = w + jnp.where(col_i == loc[:, k:k + 1], tw[:, k:k + 1], 0.0)
    w_hi = w.astype(BF16)
    ff = _dot(w_hi, rows) + _dot((w - w_hi.astype(F32)).astype(BF16), rows)
    o_ref[...] = x1_ref[...] + gt_ref[0] * _rms(ff, g_ref[...])


def _combine(yb, loc, tw, cnt_tile, start_tile, x1, post_g, gt_f, seq, tm):
    n_tok, d = x1.shape
    sub = d // LANES
    nt = n_tok // tm
    ne = cnt_tile.shape[1]
    per_b = seq // tm
    loc_l = jnp.full((n_tok, LANES), -1, I32).at[:, :TOP_K].set(loc)
    grid_spec = pltpu.PrefetchScalarGridSpec(
        num_scalar_prefetch=2,
        grid=(nt,),
        in_specs=[pl.BlockSpec(memory_space=pl.ANY),
                  pl.BlockSpec((tm, LANES), lambda i, *_: (i, 0)),
                  pl.BlockSpec((tm, LANES), lambda i, *_: (i, 0)),
                  pl.BlockSpec((tm, d), lambda i, *_: (i, 0)),
                  pl.BlockSpec((1, d), lambda i, *_: (0, 0)),
                  pl.BlockSpec((1, 1, d), lambda i, *_: (i // per_b, 0, 0))],
        out_specs=pl.BlockSpec((tm, d), lambda i, *_: (i, 0)),
        scratch_shapes=[pltpu.VMEM((2, TOP_K * tm * sub, LANES), F32), pltpu.SemaphoreType.DMA((2,))],
    )
    return pl.pallas_call(
        functools.partial(_combine_kernel, tm=tm, sub=sub, ne=ne),
        grid_spec=grid_spec,
        out_shape=jax.ShapeDtypeStruct((n_tok, d), F32),
        compiler_params=_cparams(("arbitrary",)),
        name="combine",
    )(cnt_tile.reshape(-1), start_tile.reshape(-1), yb, loc_l, tw, x1, post_g.reshape(1, d), gt_f)


def kernel(x, c, positions, ada_w, ada_b, mix_pre_g, mix_post_g, ffn_pre_g, ffn_post_g, w_in, ssm_lam_re, ssm_lam_im, ssm_log_dt, ssm_b_re, ssm_b_im, ssm_c_re, ssm_c_im, ssm_d, ssm_w_glu, w_ssm_branch, w_attn_branch, w_out, router_w, router_b, w_gate, b_gate, w_up, b_up, w_down, b_down):
    bsz, seq, d = x.shape
    depth = ada_w.shape[0]
    n_tok = bsz * seq
    bm = EXPERT_ROWS
    sub = d // LANES
    xcur = x.reshape(n_tok, d)
    pos = positions.reshape(n_tok, 1).astype(I32)
    for l in range(depth):
        ada = _ada(c, ada_w[l], ada_b[l])
        sh_m, sc_m, gt_m, sh_f, sc_f, gt_f = [a.reshape(bsz, 1, d) for a in jnp.split(ada, 6, axis=-1)]

        u, q, k, v, gs, ga, km = _inproj(xcur, sc_m, sh_m, mix_pre_g[l].reshape(1, d), pos, w_in[l],
                                         seq, tm=min(512, seq))
        ys = _s5(u, ssm_lam_re[l], ssm_lam_im[l], ssm_log_dt[l], ssm_b_re[l], ssm_b_im[l],
                 ssm_c_re[l], ssm_c_im[l], bsz, seq)
        ya = _moba(q, k, v, km, bsz, seq)
        tm = ROUTE_TILE
        x1, h2, eid, tw, rank, before, cnt = _merge(
            ys, u, ya, gs, ga, xcur, ssm_d[l], ssm_w_glu[l], w_ssm_branch[l], w_attn_branch[l], w_out[l],
            mix_post_g[l], gt_m, ffn_pre_g[l], sc_f, sh_f, router_w[l], router_b[l], seq, tm=tm)

        ne = router_w.shape[-1]
        nt = n_tok // tm
        experts = jnp.arange(ne, dtype=I32)
        counts = cnt[0, :ne].astype(I32)
        padded = (counts + bm - 1) // bm * bm
        p_ends = jnp.cumsum(padded)
        p_starts = p_ends - padded
        before = before.reshape(nt, LANES)[:, :ne].astype(I32)
        cnt_tile = jnp.concatenate([before[1:], counts[None]], axis=0) - before
        start_tile = p_starts[None, :] + before
        off_tile = jnp.cumsum(cnt_tile, axis=1) - cnt_tile
        onehot = eid[:, :TOP_K].reshape(nt, tm, TOP_K, 1) == experts
        loc = (rank[:, :TOP_K] + jnp.sum(jnp.where(onehot, off_tile[:, None, None, :], 0), axis=-1).reshape(n_tok, TOP_K))
        n_blocks = (n_tok * TOP_K) // bm + ne
        n_rows = n_blocks * bm
        blk_start = jnp.arange(n_blocks, dtype=I32)[:, None] * bm
        blk_exp = jnp.minimum(jnp.sum((blk_start >= p_ends[None, :]).astype(I32), axis=1), ne - 1)
        n_valid = (p_ends[-1:] // bm).astype(I32)

        xs = _dispatch(h2, loc, cnt_tile, start_tile, p_starts + counts, padded - counts, p_ends[-1:],
                       n_rows, sub, bm, tm)
        yb = _experts(xs, blk_exp, n_valid, w_gate[l], b_gate[l], w_up[l], b_up[l], w_down[l], b_down[l], bm)
        xcur = _combine(yb, loc, tw, cnt_tile, start_tile, x1, ffn_post_g[l], gt_f, seq, tm)
    return xcur.reshape(bsz, seq, d).astype(x.dtype)
```

```python
import functools
import math

import jax
import jax.numpy as jnp
from jax import lax
from jax.experimental import pallas as pl
from jax.experimental.pallas import tpu as pltpu

F32 = jnp.float32
BF16 = jnp.bfloat16
I32 = jnp.int32

N_HEADS = 8
HEAD_DIM = 64
ROPE_THETA = 10000.0
MOBA_BLOCK = 256
MOBA_TOPK = 3
SSM_GROUP_SIZE = 16
SSM_GROUPS = 32
SSM_STATE = 64
N_EXPERTS = 32
TOP_K = 4
SWIGLU_ALPHA = 1.702
SWIGLU_LIMIT = 7.0
NORM_EPS = 1e-6
NEG_INF = -1e30

LANES = 128
SSM_CHUNK = 64
EXPERT_ROWS = 512
ROUTE_TILE = 256
VMEM_LIMIT = 56 * 1024 * 1024
HIGHEST = lax.Precision.HIGHEST
Q_SCALE = HEAD_DIM ** -0.5 * math.log2(math.e)


def _cparams(sem):
    return pltpu.CompilerParams(dimension_semantics=sem, vmem_limit_bytes=VMEM_LIMIT)


def _dot(a, b, **kw):
    return jnp.dot(a, b, preferred_element_type=F32, **kw)


def _dot_t(a, b, **kw):
    return lax.dot_general(a, b, (((1,), (1,)), ((), ())), preferred_element_type=F32, **kw)


def _ada_kernel(c_ref, w_ref, b_ref, o_ref):
    c = c_ref[...]
    cond = c * jax.nn.sigmoid(c)
    o_ref[...] = _dot(cond, w_ref[...], precision=HIGHEST) + b_ref[...]


def _ada(c, ada_w, ada_b):
    bsz, d = c.shape
    n = ada_w.shape[1]
    c8 = jnp.zeros((8, d), F32).at[:bsz].set(c)
    out = pl.pallas_call(
        _ada_kernel,
        grid=(n // d,),
        in_specs=[pl.BlockSpec((8, d), lambda j: (0, 0)),
                  pl.BlockSpec((d, d), lambda j: (0, j)),
                  pl.BlockSpec((1, d), lambda j: (0, j))],
        out_specs=pl.BlockSpec((8, d), lambda j: (0, j)),
        out_shape=jax.ShapeDtypeStruct((8, n), F32),
        compiler_params=_cparams(("arbitrary",)),
        name="ada",
    )(c8, ada_w, ada_b.reshape(1, n))
    return out[:bsz]


def _inproj_kernel(x_ref, sc_ref, sh_ref, g_ref, pos_ref, invf_ref, w_ref,
                   u_ref, q_ref, k_ref, v_ref, gs_ref, ga_ref, km_ref, *, ssm_w, attn_w, d_model):
    x = x_ref[...]
    ms = jnp.mean(x * x, axis=-1, keepdims=True)
    xn = x * lax.rsqrt(ms + NORM_EPS) * g_ref[...]
    h = (xn * (1.0 + sc_ref[0]) + sh_ref[0]).astype(BF16)

    def proj(lo, width):
        return _dot(h, w_ref[:, lo:lo + width])

    u_ref[...] = proj(0, ssm_w)

    ang = pos_ref[...].astype(F32) * invf_ref[...]
    reps = attn_w // LANES
    cos = jnp.concatenate([jnp.cos(ang)] * reps, axis=1)
    sin = jnp.concatenate([jnp.sin(ang)] * reps, axis=1)
    lane = lax.broadcasted_iota(I32, (1, attn_w), 1)
    first = (lane % HEAD_DIM) < (HEAD_DIM // 2)
    sin = jnp.where(first, -sin, sin)

    def rope(t):
        rot = jnp.where(first, pltpu.roll(t, attn_w - HEAD_DIM // 2, axis=1),
                        pltpu.roll(t, HEAD_DIM // 2, axis=1))
        return t * cos + rot * sin

    q = rope(proj(ssm_w, attn_w))
    q_ref[...] = (q * Q_SCALE).astype(BF16)
    k = rope(proj(ssm_w + attn_w, attn_w))
    k_ref[...] = k.astype(BF16)
    nblk = k.shape[0] // MOBA_BLOCK
    km_ref[0] = jnp.mean(k.reshape(nblk, MOBA_BLOCK, attn_w), axis=1)
    v_ref[...] = proj(ssm_w + 2 * attn_w, attn_w).astype(BF16)
    gs_ref[...] = jax.nn.sigmoid(proj(ssm_w + 3 * attn_w, d_model)).astype(BF16)
    ga_ref[...] = jax.nn.sigmoid(proj(ssm_w + 3 * attn_w + d_model, d_model)).astype(BF16)


def _inproj(x2, sc, sh, g, pos, w_in, seq, tm):
    n_tok, d = x2.shape
    attn_w = N_HEADS * HEAD_DIM
    ssm_w = SSM_GROUPS * SSM_GROUP_SIZE
    in_w = w_in.shape[1]
    half = HEAD_DIM // 2
    inv_freq = ROPE_THETA ** (-jnp.arange(half, dtype=F32) / half)
    invf = jnp.tile(inv_freq, LANES // half).reshape(1, LANES)
    nt = n_tok // tm
    per_b = seq // tm
    nblk = tm // MOBA_BLOCK
    tok = lambda w: pl.BlockSpec((tm, w), lambda i: (i, 0))
    bvec = pl.BlockSpec((1, 1, d), lambda i: (i // per_b, 0, 0))
    outs = pl.pallas_call(
        functools.partial(_inproj_kernel, ssm_w=ssm_w, attn_w=attn_w, d_model=d),
        grid=(nt,),
        in_specs=[tok(d), bvec, bvec,
                  pl.BlockSpec((1, d), lambda i: (0, 0)),
                  tok(1),
                  pl.BlockSpec((1, LANES), lambda i: (0, 0)),
                  pl.BlockSpec((d, in_w), lambda i: (0, 0))],
        out_specs=[tok(ssm_w), tok(attn_w), tok(attn_w), tok(attn_w), tok(d), tok(d),
                   pl.BlockSpec((1, nblk, attn_w), lambda i: (i, 0, 0))],
        out_shape=[jax.ShapeDtypeStruct((n_tok, ssm_w), F32),
                   jax.ShapeDtypeStruct((n_tok, attn_w), BF16),
                   jax.ShapeDtypeStruct((n_tok, attn_w), BF16),
                   jax.ShapeDtypeStruct((n_tok, attn_w), BF16),
                   jax.ShapeDtypeStruct((n_tok, d), BF16),
                   jax.ShapeDtypeStruct((n_tok, d), BF16),
                   jax.ShapeDtypeStruct((nt, nblk, attn_w), F32)],
        compiler_params=_cparams(("arbitrary",)),
        name="inproj",
    )(x2, sc, sh, g, pos, invf, w_in.astype(BF16))
    return outs


def _s5_group(x, zr, zi, lr, li, ca, cb, ba, bb, t_ref, *, chunk, n_chunks):
    gs = SSM_GROUP_SIZE

    def powers(tau):
        mag = jnp.exp(tau * zr)
        return mag * jnp.cos(tau * zi), mag * jnp.sin(tau * zi)

    tau = lax.broadcasted_iota(I32, (chunk + 8, LANES), 0).astype(F32)
    e_re, e_im = powers(tau)
    lb_re, lb_im = e_re[1:2], e_im[1:2]
    den = lr * lr + li * li
    a, b = lb_re - 1.0, lb_im
    cf_re = (a * lr + b * li) / den
    cf_im = (b * lr - a * li) / den
    bri = cf_re * ba + cf_im * bb
    bri_sw = cf_re * bb - cf_im * ba

    cpow =(e_re[:chunk + 1, None, :] * ca[None] + e_im[:chunk + 1, None, :] * cb[None])
    cpow = cpow.reshape((chunk + 1) * gs, LANES)
    width = chunk * gs
    r = _dot_t(bri, cpow[:width], precision=HIGHEST)
    col = lax.broadcasted_iota(I32, (gs, width), 1)
    t_ref[0:gs, :] = r.astype(BF16)
    for j in range(1, chunk):
        shifted = jnp.where(col >= gs * j, pltpu.roll(r, gs * j, axis=1), 0.0)
        t_ref[gs * j:gs * (j + 1), :] = shifted.astype(BF16)

    y = _dot(x, t_ref[...])

    tau_rev = (chunk - 1) - lax.broadcasted_iota(I32, (chunk, LANES), 0)
    r_re, r_im = powers(tau_rev.astype(F32))
    bst = r_re[:, None, :] * bri[None] + r_im[:, None, :] * bri_sw[None]
    bst = bst.reshape(width, LANES).astype(BF16)
    s = _dot(x, bst)

    rows = s.shape[0]
    n_idx = lax.broadcasted_iota(I32, (rows, LANES), 0) % n_chunks
    lane = lax.broadcasted_iota(I32, (1, LANES), 1)
    half = LANES // 2

    def cmul(v, p_re, p_im):
        return v * p_re + pltpu.roll(v, half, axis=1) * jnp.where(lane < half, -p_im, p_im)

    sh = 1
    while sh < n_chunks:
        p_re, p_im = powers(jnp.full((1, LANES), float(chunk * sh), F32))
        prev = jnp.where(n_idx >= sh, pltpu.roll(s, sh, axis=0), 0.0)
        s = s + cmul(prev, p_re, p_im)
        sh *= 2
    s_in = jnp.where(n_idx >= 1, pltpu.roll(s, 1, axis=0), 0.0)
    return y + _dot_t(s_in.astype(BF16), cpow[gs:gs + width].astype(BF16))


def _s5_kernel(u_ref, sel_ref, zr_ref, zi_ref, lr_ref, li_ref, ca_ref, cb_ref, ba_ref, bb_ref,
               y_ref, x_scr, y_scr, t_ref, *, chunk, n_chunks, rows):
    gpt = LANES // SSM_GROUP_SIZE

    def plane(j):
        return pl.ds(j, rows, stride=chunk)

    for jt in range(chunk // gpt):
        planes = jnp.concatenate([u_ref[plane(jt * gpt + jj), :] for jj in range(gpt)], axis=1)
        grouped = _dot(planes.astype(BF16), sel_ref[...])
        for g in range(gpt):
            x_scr[g, :, jt * LANES:(jt + 1) * LANES] = grouped[:, g * LANES:(g + 1) * LANES].astype(BF16)

    def per_group(g, _):
        y = _s5_group(x_scr[g], zr_ref[g], zi_ref[g], lr_ref[g], li_ref[g], ca_ref[g], cb_ref[g], ba_ref[g],
                      bb_ref[g], t_ref, chunk=chunk, n_chunks=n_chunks)
        y_scr[g] = y.astype(BF16)
        return 0

    lax.fori_loop(0, gpt, per_group, 0)

    for jt in range(chunk // gpt):
        grouped = jnp.concatenate([y_scr[g, :, jt * LANES:(jt + 1) * LANES] for g in range(gpt)], axis=1)
        planes = _dot(grouped, sel_ref[...])
        for jj in range(gpt):
            y_ref[plane(jt * gpt + jj), :] = planes[:, jj * LANES:(jj + 1) * LANES]


def _s5(u, lam_re, lam_im, log_dt, b_re, b_im, c_re, c_im, bsz, seq):
    g_n, gs, p = SSM_GROUPS, SSM_GROUP_SIZE, SSM_STATE
    chunk = SSM_CHUNK
    n_chunks = seq // chunk
    rows = bsz * n_chunks
    width = chunk * gs
    gpt = LANES // gs
    assert chunk % gpt == 0 and g_n % gpt == 0 and 2 * p == LANES
    dt = jnp.exp(log_dt.astype(F32))[:, None]
    dup = lambda a: jnp.concatenate([a, a], axis=-1).reshape(g_n, 1, 2 * p)
    zr, zi = dup(lam_re * dt), dup(lam_im * dt)
    lr, li = dup(lam_re), dup(lam_im)
    bt_re, bt_im = b_re.transpose(0, 2, 1), b_im.transpose(0, 2, 1)
    ba = jnp.concatenate([bt_re, bt_im], axis=-1)
    bb = jnp.concatenate([-bt_im, bt_re], axis=-1)
    ca = jnp.concatenate([c_re, -c_im], axis=-1)
    cb = jnp.concatenate([-c_im, -c_re], axis=-1)
    idx = jnp.arange(gpt * LANES, dtype=I32)
    swapped = ((idx // gs) % gpt) * LANES + (idx // LANES) * gs + idx % gs
    sel = (swapped[:, None] == idx[None, :]).astype(BF16)
    tile = pl.BlockSpec((bsz * seq, LANES), lambda t: (0, t))
    vec = pl.BlockSpec((gpt, 1, 2 * p), lambda t: (t, 0, 0))
    mat = pl.BlockSpec((gpt, gs, 2 * p), lambda t: (t, 0, 0))
    return pl.pallas_call(
        functools.partial(_s5_kernel, chunk=chunk, n_chunks=n_chunks, rows=rows),
        grid=(g_n // gpt,),
        in_specs=[tile, pl.BlockSpec((gpt * LANES, gpt * LANES), lambda t: (0, 0)),
                  vec, vec, vec, vec, mat, mat, mat, mat],
        out_specs=tile,
        out_shape=jax.ShapeDtypeStruct((bsz * seq, g_n * gs), F32),
        scratch_shapes=[pltpu.VMEM((gpt, rows, width), BF16), pltpu.VMEM((gpt, rows, width), BF16),
                        pltpu.VMEM((width, width), BF16)],
        compiler_params=_cparams(("arbitrary",)),
        name="s5",
    )(u, sel, zr, zi, lr, li, ca, cb, ba, bb)


def _moba_kernel(qt_ref, k_ref, vt_ref, km_ref, o_ref, s_a, s_b, s_fin, p_a, p_b, *, nb, nbp):
    blk = MOBA_BLOCK
    group = 2
    keys = nq = group * blk
    i = pl.program_id(2)
    qt = qt_ref[...]
    km = km_ref[...]
    km_hi = km.astype(BF16)
    km_lo = (km - km_hi.astype(F32)).astype(BF16)
    dim_i = lax.broadcasted_iota(I32, (LANES, nq), 0)
    blk_i = lax.broadcasted_iota(I32, (nbp, nq), 0)
    blk_f = blk_i.astype(F32)
    own = group * i + lax.broadcasted_iota(I32, (nbp, nq), 1) // blk
    valid = blk_i < own

    q_loop, q_fin = [], []
    for h in range(2):
        qh = jnp.where((dim_i >= h * HEAD_DIM) & (dim_i < (h + 1) * HEAD_DIM), qt, jnp.zeros_like(qt))
        g = jnp.where(valid, _dot(km_hi, qh) + _dot(km_lo, qh), NEG_INF)
        sel = jnp.zeros((nbp, nq), jnp.bool_)
        for _ in range(MOBA_TOPK):
            m = jnp.max(g, axis=0, keepdims=True)
            idx = jnp.min(jnp.where(g == m, blk_f, float(nbp)), axis=0, keepdims=True)
            pick = blk_f == idx
            sel = sel | pick
            g = jnp.where(pick, -jnp.inf, g)
        allowed = sel & valid
        for lst, ok in ((q_loop, allowed & (blk_i < group * i)), (q_fin, allowed | (blk_i == own))):
            parts = [qh, jnp.where(ok, 0.0, NEG_INF).astype(BF16)]
            if nbp < LANES:
                parts.append(jnp.full((LANES - nbp, nq), NEG_INF, BF16))
            lst.append(jnp.concatenate(parts, axis=0))

    def scores(grp, s_ref, q_aug, causal=False):
        kb0 = grp * group
        k_cat = k_ref[pl.ds(jnp.minimum(kb0, nb - group), group)].reshape(keys, LANES)
        blk_id = kb0 + lax.broadcasted_iota(I32, (keys, LANES), 0) // blk
        onehot = jnp.where(lax.broadcasted_iota(I32, (keys, LANES), 1) == blk_id, 1.0, 0.0).astype(BF16)
        k_aug = jnp.concatenate([k_cat, onehot], axis=1)
        mx = []
        for h in range(2):
            s = _dot(k_aug, q_aug[h])
            if causal:
                key_i = lax.broadcasted_iota(I32, (keys, nq), 0)
                s = jnp.where(key_i <= lax.broadcasted_iota(I32, (keys, nq), 1), s, NEG_INF)
            s_ref[h] = s
            mx.append(jnp.max(s, axis=0, keepdims=True))
        return tuple(mx)

    def softmax(s_ref, mx, p_ref, st):
        out = []
        for h in range(2):
            m_i, _, l_i, acc = st[h]
            m_new = jnp.maximum(m_i, mx[h])
            p_ref[h] = jnp.exp2(s_ref[h] - m_new).astype(BF16)
            out.append((m_new, jnp.exp2(m_i - m_new), l_i, acc))
        return tuple(out)

    def values(grp, p_ref, st):
        out = []
        for h in range(2):
            m_i, alpha, l_i, acc = st[h]
            blocks = [vt_ref[jnp.clip(grp * group + j, 0, nb - 1), h * HEAD_DIM:(h + 1) * HEAD_DIM, :]
                      for j in range(group)]
            ones = jnp.ones((16, keys), BF16)
            pv = _dot(jnp.concatenate([jnp.concatenate(blocks, axis=1), ones], axis=0), p_ref[h])
            out.append((m_i, alpha, alpha * l_i + pv[HEAD_DIM:HEAD_DIM + 1], alpha * acc + pv[:HEAD_DIM]))
        return tuple(out)

    mx_fin = scores(i, s_fin, q_fin, causal=True)
    mx_first = scores(0, s_a, q_loop)
    p_b[...] = jnp.zeros_like(p_b)

    def body(t, carry):
        st, mx_a = carry
        mx_b = scores(2 * t + 1, s_b, q_loop)
        st = values(2 * t - 1, p_b, st)
        st = softmax(s_a, mx_a, p_a, st)
        mx_a = scores(2 * t + 2, s_a, q_loop)
        st = values(2 * t, p_a, st)
        return softmax(s_b, mx_b, p_b, st), mx_a

    init = (jnp.full((1, nq), -jnp.inf, F32), jnp.ones((1, nq), F32), jnp.zeros((1, nq), F32),
            jnp.zeros((HEAD_DIM, nq), F32))
    trips = (i + 1) // 2
    st, _ = lax.fori_loop(0, trips, body, ((init, init), mx_first))
    st = values(2 * trips - 1, p_b, st)
    st = values(i, p_a, softmax(s_fin, mx_fin, p_a, st))
    for h in range(2):
        _, _, l_i, acc = st[h]
        o_ref[h * HEAD_DIM:(h + 1) * HEAD_DIM, :] = (acc / l_i).astype(BF16)


def _moba(q, k, v, km, bsz, seq):
    n_tok, attn_w = q.shape
    blk = MOBA_BLOCK
    nb = seq // blk
    group = 2
    nbp = -(-nb // 16) * 16
    assert nb % group == 0 and nb + 2 * group <= LANES
    hp = attn_w // LANES
    km_pad = jnp.zeros((bsz, nbp, attn_w), F32).at[:, :nb].set(km.reshape(bsz, nb, attn_w))
    qt = q.reshape(bsz, seq, attn_w).transpose(0, 2, 1)
    k4 = k.reshape(bsz, nb, blk, attn_w)
    vt4 = v.reshape(bsz, nb, blk, attn_w).transpose(0, 1, 3, 2)
    ot = pl.pallas_call(
        functools.partial(_moba_kernel, nb=nb, nbp=nbp),
        grid=(bsz, hp, nb // group),
        in_specs=[pl.BlockSpec((None, LANES, group * blk), lambda b, p, i: (b, p, i)),
                  pl.BlockSpec((None, nb, blk, LANES), lambda b, p, i: (b, 0, 0, p)),
                  pl.BlockSpec((None, nb, LANES, blk), lambda b, p, i: (b, 0, p, 0)),
                  pl.BlockSpec((None, nbp, LANES), lambda b, p, i: (b, 0, p))],
        out_specs=pl.BlockSpec((None, LANES, group * blk), lambda b, p, i: (b, p, i)),
        out_shape=jax.ShapeDtypeStruct((bsz, attn_w, seq), BF16),
        scratch_shapes=([pltpu.VMEM((2, group * blk, group * blk), F32)] * 3
                        + [pltpu.VMEM((2, group * blk, group * blk), BF16)] * 2),
        compiler_params=_cparams(("arbitrary", "arbitrary", "arbitrary")),
        name="moba",
    )(qt, k4, vt4, km_pad)
    return ot.transpose(0, 2, 1).reshape(n_tok, attn_w)


def _rms(t, g):
    return t * lax.rsqrt(jnp.mean(t * t, axis=-1, keepdims=True) + NORM_EPS) * g


def _store_slab(ref, val, base=0):
    rows, d = val.shape
    sub = d // LANES
    for s in range(sub):
        ref[pl.ds(base + s, rows, stride=sub), :] = val[:, s * LANES:(s + 1) * LANES]


def _load_slab(ref, rows, sub, base=0):
    return jnp.concatenate([ref[pl.ds(base + s, rows, stride=sub), :] for s in range(sub)], axis=1)


def _merge_kernel(ys_ref, u_ref, ya_ref, gs_ref, ga_ref, x_ref, d_ref, wglu_ref, wsb_ref, wab_ref,
                  wout_ref, postg_ref, gtm_ref, preg_ref, scf_ref, shf_ref, rw_ref, rb_ref,
                  x1_ref, h2_ref, eid_ref, tw_ref, rank_ref, before_ref, cnt_ref, run_ref):
    @pl.when(pl.program_id(0) == 0)
    def _():
        run_ref[...] = jnp.zeros_like(run_ref)

    y = ys_ref[...] + d_ref[...] * u_ref[...].astype(F32)
    y = jax.nn.gelu(y)
    y = y * jax.nn.sigmoid(_dot(y.astype(BF16), wglu_ref[...]))
    bs = _dot(y.astype(BF16), wsb_ref[...])
    ba = _dot(ya_ref[...], wab_ref[...])
    merged = gs_ref[...].astype(F32) * bs + ga_ref[...].astype(F32) * ba
    mix = _dot(merged.astype(BF16), wout_ref[...])
    x1 = x_ref[...] + gtm_ref[0] * _rms(mix, postg_ref[...])
    x1_ref[...] = x1
    h2 = _rms(x1, preg_ref[...]) * (1.0 + scf_ref[0]) + shf_ref[0]
    _store_slab(h2_ref, h2)

    h_hi = h2.astype(BF16)
    h_lo = (h2 - h_hi.astype(F32)).astype(BF16)
    logits = _dot(jnp.concatenate([h_hi, h_hi, h_lo], axis=1), rw_ref[...]) + rb_ref[...]
    tm = logits.shape[0]
    lane = lax.broadcasted_iota(I32, (tm, LANES), 1)
    lane_f = lane.astype(F32)
    g = logits
    vals, picks, idxs = [], [], []
    for _ in range(TOP_K):
        m = jnp.max(g, axis=1, keepdims=True)
        idx = jnp.min(jnp.where(g == m, lane_f, float(LANES)), axis=1, keepdims=True)
        pick = lane_f == idx
        vals.append(m)
        idxs.append(idx.astype(I32))
        picks.append(pick)
        g = jnp.where(pick, -jnp.inf, g)
    exps = [jnp.exp(v - vals[0]) for v in vals]
    tot = exps[0] + exps[1] + exps[2] + exps[3]

    onehot = jnp.where(picks[0] | picks[1] | picks[2] | picks[3], 1.0, 0.0)
    r_i = lax.broadcasted_iota(I32, (tm, tm), 0)
    c_i = lax.broadcasted_iota(I32, (tm, tm), 1)
    tri = jnp.where(c_i < r_i, 1.0, 0.0).astype(BF16)
    rank_full = _dot(tri, onehot.astype(BF16))
    before_ref[0] = run_ref[...]
    run_ref[...] = run_ref[...] + jnp.sum(onehot, axis=0, keepdims=True)
    cnt_ref[...] = run_ref[...]

    eid = jnp.zeros((tm, LANES), I32)
    tw = jnp.zeros((tm, LANES), F32)
    rk = jnp.zeros((tm, LANES), F32)
    for r in range(TOP_K):
        eid = jnp.where(lane == r, idxs[r], eid)
        tw = jnp.where(lane == r, exps[r] / tot, tw)
        pos = jnp.sum(jnp.where(picks[r], rank_full, 0.0), axis=1, keepdims=True)
        rk = jnp.where(lane == r, pos, rk)
    eid_ref[...] = eid
    tw_ref[...] = tw
    rank_ref[...] = rk.astype(I32)


def _merge(ys, u, ya, gs, ga, x2, ssm_d, w_glu, w_sb, w_ab, w_out, post_g, gt_m, pre_g, sc_f, sh_f,
           router_w, router_b, seq, tm):
    n_tok, d = x2.shape
    sw = ys.shape[1]
    aw = ya.shape[1]
    ne = router_w.shape[1]
    per_b = seq // tm
    rw = jnp.zeros((d, LANES), F32).at[:, :ne].set(router_w)
    rw_hi = rw.astype(BF16)
    rw_lo = (rw - rw_hi.astype(F32)).astype(BF16)
    rw = jnp.concatenate([rw_hi, rw_lo, rw_hi], axis=0)
    rb = jnp.full((1, LANES), NEG_INF, F32).at[0, :ne].set(router_b)
    tok = lambda w: pl.BlockSpec((tm, w), lambda i: (i, 0))
    full = lambda a, b: pl.BlockSpec((a, b), lambda i: (0, 0))
    bvec = pl.BlockSpec((1, 1, d), lambda i: (i // per_b, 0, 0))
    return pl.pallas_call(
        _merge_kernel,
        grid=(n_tok // tm,),
        in_specs=[tok(sw), tok(sw), tok(aw), tok(d), tok(d), tok(d),
                  full(1, sw), full(sw, sw), full(sw, d), full(aw, d), full(d, d),
                  full(1, d), bvec, full(1, d), bvec, bvec, full(3 * d, LANES), full(1, LANES)],
        out_specs=[tok(d), pl.BlockSpec((tm * (d // LANES), LANES), lambda i: (i, 0)),
                   tok(LANES), tok(LANES), tok(LANES),
                   pl.BlockSpec((1, 1, LANES), lambda i: (i, 0, 0)), full(1, LANES)],
        out_shape=[jax.ShapeDtypeStruct((n_tok, d), F32),
                   jax.ShapeDtypeStruct((n_tok * (d // LANES), LANES), F32),
                   jax.ShapeDtypeStruct((n_tok, LANES), I32),
                   jax.ShapeDtypeStruct((n_tok, LANES), F32),
                   jax.ShapeDtypeStruct((n_tok, LANES), I32),
                   jax.ShapeDtypeStruct((n_tok // tm, 1, LANES), F32),
                   jax.ShapeDtypeStruct((1, LANES), F32)],
        scratch_shapes=[pltpu.VMEM((1, LANES), F32)],
        compiler_params=_cparams(("arbitrary",)),
        name="merge",
    )(ys, u, ya, gs, ga, x2, ssm_d.reshape(1, sw), w_glu.astype(BF16), w_sb.astype(BF16),
      w_ab.astype(BF16), w_out.astype(BF16), post_g.reshape(1, d), gt_m, pre_g.reshape(1, d),
      sc_f, sh_f, rw, rb)


def _chunk_copies(cnt_ref, start_ref, tile, ne, max_rows, make):
    def per_expert(e, off):
        c, start = cnt_ref[tile * ne + e], start_ref[tile * ne + e]
        for b in range(max_rows.bit_length()):
            n = 1 << b

            @pl.when((c >> b) & 1 == 1)
            def _():
                make(off + (c & (n - 1)), start + (c & (n - 1)), n)
        return off + c

    lax.fori_loop(0, ne, per_expert, 0)


def _dispatch_kernel(cnt_ref, start_ref, pad_start_ref, pad_cnt_ref, tail_ref, loct_ref, h_ref, xs_ref,
                     buf, zero_ref, sem, zsem, *, tm, sub, bm, n_rows, ne):
    i = pl.program_id(0)
    last = pl.num_programs(0) - 1
    slot = i % 2
    n_loc = tm * TOP_K

    def row(ref, r, n=1):
        return ref.at[pl.ds(pl.multiple_of(r * sub, sub), n * sub)]

    def wait_slot(s):
        pltpu.make_async_copy(buf.at[s], row(xs_ref, 0, n_loc), sem.at[s]).wait()

    def zero_copies(do):
        def per_expert(e, _):
            start, cnt = pad_start_ref[e], pad_cnt_ref[e]
            for b in range(bm.bit_length() - 1):
                n = 1 << b

                @pl.when((cnt >> b) & 1 == 1)
                def _():
                    do(pltpu.make_async_copy(row(zero_ref, 0, n), row(xs_ref, start + (cnt & (n - 1)), n), zsem))
            return 0

        lax.fori_loop(0, ne, per_expert, 0)

        def per_block(j, _):
            r = tail_ref[0] + j * bm

            @pl.when(r < n_rows)
            def _():
                do(pltpu.make_async_copy(row(zero_ref, 0, bm), row(xs_ref, r, bm), zsem))
            return 0

        lax.fori_loop(0, ne, per_block, 0)

    @pl.when(i == 0)
    def _():
        zero_ref[...] = jnp.zeros_like(zero_ref)
        zero_copies(lambda cp: cp.start())

    @pl.when(i >= 2)
    def _():
        wait_slot(slot)

    loct = loct_ref[0]
    row_i = lax.broadcasted_iota(I32, (n_loc, tm), 0)
    place = row_i == loct[0:1, :]
    for k in range(1, TOP_K):
        place = place | (row_i == loct[k:k + 1, :])
    grouped = _dot(jnp.where(place, 1.0, 0.0).astype(BF16), _load_slab(h_ref, tm, sub).astype(BF16))
    _store_slab(buf.at[slot], grouped)
    _chunk_copies(cnt_ref, start_ref, i, ne, tm,
                  lambda loc, glob, n: pltpu.make_async_copy(row(buf.at[slot], loc, n), row(xs_ref, glob, n),
                                                             sem.at[slot]).start())

    @pl.when(i == last)
    def _():
        wait_slot(slot)

    @pl.when((i == last) & (i >= 1))
    def _():
        wait_slot(1 - slot)

    @pl.when(i == 0)
    def _():
        zero_copies(lambda cp: cp.wait())


def _dispatch(h_slab, loc, cnt_tile, start_tile, pad_start, pad_cnt, tail, n_rows, sub, bm, tm):
    n_tok = loc.shape[0]
    nt = n_tok // tm
    ne = pad_start.shape[0]
    loct = jnp.full((nt, 8, tm), -1, I32).at[:, :TOP_K].set(loc.reshape(nt, tm, TOP_K).transpose(0, 2, 1))
    grid_spec = pltpu.PrefetchScalarGridSpec(
        num_scalar_prefetch=5,
        grid=(nt,),
        in_specs=[pl.BlockSpec((1, 8, tm), lambda i, *_: (i, 0, 0)),
                  pl.BlockSpec((tm * sub, LANES), lambda i, *_: (i, 0))],
        out_specs=pl.BlockSpec(memory_space=pl.ANY),
        scratch_shapes=[pltpu.VMEM((2, TOP_K * tm * sub, LANES), F32), pltpu.VMEM((bm * sub, LANES), F32),
                        pltpu.SemaphoreType.DMA((2,)), pltpu.SemaphoreType.DMA(())],
    )
    return pl.pallas_call(
        functools.partial(_dispatch_kernel, tm=tm, sub=sub, bm=bm, n_rows=n_rows, ne=ne),
        grid_spec=grid_spec,
        out_shape=jax.ShapeDtypeStruct((n_rows * sub, LANES), F32),
        compiler_params=_cparams(("arbitrary",)),
        name="dispatch",
    )(cnt_tile.reshape(-1), start_tile.reshape(-1), pad_start, pad_cnt, tail, loct, h_slab)


def _expert_kernel(first_ref, nblk_ref, x_ref, wg_ref, bg_ref, wu_ref, bu_ref, wd_ref, bd_ref,
                   y_ref, wg_s, wu_s, wd_s, xbuf, ybuf, xsem, ysem, *, bm, sub, n_blocks):
    e = pl.program_id(0)
    first, n = first_ref[e], nblk_ref[e]
    rows = bm * sub

    def block(ref, b):
        return ref.at[pl.ds(pl.multiple_of(b * rows, rows), rows)]

    def x_copy(j, slot):
        return pltpu.make_async_copy(block(x_ref, first + j), xbuf.at[slot], xsem.at[slot])

    def y_copy(j, slot):
        return pltpu.make_async_copy(ybuf.at[slot], block(y_ref, first + j), ysem.at[slot])

    @pl.when(n > 0)
    def _():
        x_copy(0, 0).start()
        wg_s[...] = wg_ref[0].astype(BF16)
        wu_s[...] = wu_ref[0].astype(BF16)
        wd_s[...] = wd_ref[0].astype(BF16)

    def body(j, _):
        slot = j % 2

        @pl.when(j + 1 < n)
        def _():
            x_copy(j + 1, 1 - slot).start()

        x_copy(j, slot).wait()

        @pl.when(j >= 2)
        def _():
            y_copy(j - 2, slot).wait()

        xb = _load_slab(xbuf.at[slot], bm, sub).astype(BF16)
        g = _dot(xb, wg_s[...]) + bg_ref[0]
        u = _dot(xb, wu_s[...]) + bu_ref[0]
        g = jnp.minimum(g, SWIGLU_LIMIT)
        u = jnp.clip(u, -SWIGLU_LIMIT, SWIGLU_LIMIT)
        act = g * jax.nn.sigmoid(SWIGLU_ALPHA * g) * (u + 1.0)
        _store_slab(ybuf.at[slot], _dot(act.astype(BF16), wd_s[...]) + bd_ref[0])
        y_copy(j, slot).start()
        return 0

    lax.fori_loop(0, n, body, 0)

    @pl.when(n >= 1)
    def _():
        y_copy(n - 1, (n - 1) % 2).wait()

    @pl.when(n >= 2)
    def _():
        y_copy(n - 2, n % 2).wait()

    @pl.when(e == pl.num_programs(0) - 1)
    def _():
        used = first + n
        ybuf[0] = jnp.zeros((rows, LANES), F32)

        def fill(b, _):
            cp = pltpu.make_async_copy(ybuf.at[0], block(y_ref, b), ysem.at[0])
            cp.start()
            cp.wait()
            return 0

        lax.fori_loop(used, n_blocks, fill, 0)


def _experts(xs, first_blk, n_blk, w_gate, b_gate, w_up, b_up, w_down, b_down, bm):
    ne, d, f = w_gate.shape
    sub = d // LANES
    n_blocks = xs.shape[0] // (bm * sub)
    wspec = lambda a, b: pl.BlockSpec((1, a, b), lambda e, *_: (e, 0, 0))
    grid_spec = pltpu.PrefetchScalarGridSpec(
        num_scalar_prefetch=2,
        grid=(ne,),
        in_specs=[pl.BlockSpec(memory_space=pl.ANY),
                  wspec(d, f), wspec(1, f), wspec(d, f), wspec(1, f), wspec(f, d), wspec(1, d)],
        out_specs=pl.BlockSpec(memory_space=pl.ANY),
        scratch_shapes=[pltpu.VMEM((d, f), BF16), pltpu.VMEM((d, f), BF16), pltpu.VMEM((f, d), BF16),
                        pltpu.VMEM((2, bm * sub, LANES), F32), pltpu.VMEM((2, bm * sub, LANES), F32),
                        pltpu.SemaphoreType.DMA((2,)), pltpu.SemaphoreType.DMA((2,))],
    )
    return pl.pallas_call(
        functools.partial(_expert_kernel, bm=bm, sub=sub, n_blocks=n_blocks),
        grid_spec=grid_spec,
        out_shape=jax.ShapeDtypeStruct(xs.shape, F32),
        compiler_params=_cparams(("arbitrary",)),
        name="experts",
    )(first_blk, n_blk, xs, w_gate, b_gate.reshape(ne, 1, f), w_up, b_up.reshape(ne, 1, f),
      w_down, b_down.reshape(ne, 1, d))


def _combine_kernel(cnt_ref, start_ref, yb_ref, loc_ref, tw_ref, x1_ref, g_ref, gt_ref, o_ref, buf, sem,
                    *, tm, sub, ne):
    i = pl.program_id(0)
    n_loc = tm * TOP_K

    def start_tile(tile, slot):
        def make(loc, glob, n):
            src = yb_ref.at[pl.ds(pl.multiple_of(glob * sub, sub), n * sub)]
            dst = buf.at[slot, pl.ds(pl.multiple_of(loc * sub, sub), n * sub)]
            pltpu.make_async_copy(src, dst, sem.at[slot]).start()

        _chunk_copies(cnt_ref, start_ref, tile, ne, tm, make)

    slot = i % 2

    @pl.when(i == 0)
    def _():
        start_tile(0, 0)

    @pl.when(i + 1 < pl.num_programs(0))
    def _():
        start_tile(i + 1, 1 - slot)

    pltpu.make_async_copy(yb_ref.at[pl.ds(0, n_loc * sub)], buf.at[slot], sem.at[slot]).wait()
    rows = _load_slab(buf.at[slot], n_loc, sub).astype(BF16)
    loc, tw = loc_ref[...], tw_ref[...]
    col_i = lax.broadcasted_iota(I32, (tm, n_loc), 1)
    w = jnp.zeros((tm, n_loc), F32)
    for k in range(TOP_K):
        w = w + jnp.where(col_i == loc[:, k:k + 1], tw[:, k:k + 1], 0.0)
    w_hi = w.astype(BF16)
    ff = _dot(w_hi, rows) + _dot((w - w_hi.astype(F32)).astype(BF16), rows)
    o_ref[...] = x1_ref[...] + gt_ref[0] * _rms(ff, g_ref[...])


def _combine(yb, loc, tw, cnt_tile, start_tile, x1, post_g, gt_f, seq, tm):
    n_tok, d = x1.shape
    sub = d // LANES
    nt = n_tok // tm
    ne = cnt_tile.shape[1]
    per_b = seq // tm
    loc_l = jnp.full((n_tok, LANES), -1, I32).at[:, :TOP_K].set(loc)
    grid_spec = pltpu.PrefetchScalarGridSpec(
        num_scalar_prefetch=2,
        grid=(nt,),
        in_specs=[pl.BlockSpec(memory_space=pl.ANY),
                  pl.BlockSpec((tm, LANES), lambda i, *_: (i, 0)),
                  pl.BlockSpec((tm, LANES), lambda i, *_: (i, 0)),
                  pl.BlockSpec((tm, d), lambda i, *_: (i, 0)),
                  pl.BlockSpec((1, d), lambda i, *_: (0, 0)),
                  pl.BlockSpec((1, 1, d), lambda i, *_: (i // per_b, 0, 0))],
        out_specs=pl.BlockSpec((tm, d), lambda i, *_: (i, 0)),
        scratch_shapes=[pltpu.VMEM((2, TOP_K * tm * sub, LANES), F32), pltpu.SemaphoreType.DMA((2,))],
    )
    return pl.pallas_call(
        functools.partial(_combine_kernel, tm=tm, sub=sub, ne=ne),
        grid_spec=grid_spec,
        out_shape=jax.ShapeDtypeStruct((n_tok, d), F32),
        compiler_params=_cparams(("arbitrary",)),
        name="combine",
    )(cnt_tile.reshape(-1), start_tile.reshape(-1), yb, loc_l, tw, x1, post_g.reshape(1, d), gt_f)


def kernel(x, c, positions, ada_w, ada_b, mix_pre_g, mix_post_g, ffn_pre_g, ffn_post_g, w_in, ssm_lam_re, ssm_lam_im, ssm_log_dt, ssm_b_re, ssm_b_im, ssm_c_re, ssm_c_im, ssm_d, ssm_w_glu, w_ssm_branch, w_attn_branch, w_out, router_w, router_b, w_gate, b_gate, w_up, b_up, w_down, b_down):
    bsz, seq, d = x.shape
    depth = ada_w.shape[0]
    n_tok = bsz * seq
    bm = EXPERT_ROWS
    sub = d // LANES
    xcur = x.reshape(n_tok, d)
    pos = positions.reshape(n_tok, 1).astype(I32)
    for l in range(depth):
        ada = _ada(c, ada_w[l], ada_b[l])
        sh_m, sc_m, gt_m, sh_f, sc_f, gt_f = [a.reshape(bsz, 1, d) for a in jnp.split(ada, 6, axis=-1)]

        u, q, k, v, gs, ga, km = _inproj(xcur, sc_m, sh_m, mix_pre_g[l].reshape(1, d), pos, w_in[l],
                                         seq, tm=min(512, seq))
        ys = _s5(u, ssm_lam_re[l], ssm_lam_im[l], ssm_log_dt[l], ssm_b_re[l], ssm_b_im[l],
                 ssm_c_re[l], ssm_c_im[l], bsz, seq)
        ya = _moba(q, k, v, km, bsz, seq)
        tm = ROUTE_TILE
        x1, h2, eid, tw, rank, before, cnt = _merge(
            ys, u, ya, gs, ga, xcur, ssm_d[l], ssm_w_glu[l], w_ssm_branch[l], w_attn_branch[l], w_out[l],
            mix_post_g[l], gt_m, ffn_pre_g[l], sc_f, sh_f, router_w[l], router_b[l], seq, tm=tm)

        ne = router_w.shape[-1]
        nt = n_tok // tm
        experts = jnp.arange(ne, dtype=I32)
        counts = cnt[0, :ne].astype(I32)
        padded = (counts + bm - 1) // bm * bm
        p_ends = jnp.cumsum(padded)
        p_starts = p_ends - padded
        before = before.reshape(nt, LANES)[:, :ne].astype(I32)
        cnt_tile = jnp.concatenate([before[1:], counts[None]], axis=0) - before
        start_tile = p_starts[None, :] + before
        off_tile = jnp.cumsum(cnt_tile, axis=1) - cnt_tile
        onehot = eid[:, :TOP_K].reshape(nt, tm, TOP_K, 1) == experts
        loc = (rank[:, :TOP_K] + jnp.sum(jnp.where(onehot, off_tile[:, None, None, :], 0), axis=-1).reshape(n_tok, TOP_K))
        n_rows = n_tok * TOP_K + ne * bm

        xs = _dispatch(h2, loc, cnt_tile, start_tile, p_starts + counts, padded - counts, p_ends[-1:],
                       n_rows, sub, bm, tm)
        yb = _experts(xs, p_starts // bm, padded // bm, w_gate[l], b_gate[l], w_up[l], b_up[l], w_down[l],
                      b_down[l], bm)
        xcur = _combine(yb, loc, tw, cnt_tile, start_tile, x1, ffn_post_g[l], gt_f, seq, tm)
    return xcur.reshape(bsz, seq, d).astype(x.dtype)
```

```python
import functools
import math

import jax
import jax.numpy as jnp
from jax import lax
from jax.experimental import pallas as pl
from jax.experimental.pallas import tpu as pltpu

F32 = jnp.float32
BF16 = jnp.bfloat16
I32 = jnp.int32

N_HEADS = 8
HEAD_DIM = 64
ROPE_THETA = 10000.0
MOBA_BLOCK = 256
MOBA_TOPK = 3
SSM_GROUP_SIZE = 16
SSM_GROUPS = 32
SSM_STATE = 64
N_EXPERTS = 32
TOP_K = 4
SWIGLU_ALPHA = 1.702
SWIGLU_LIMIT = 7.0
NORM_EPS = 1e-6
NEG_INF = -1e30

LANES = 128
SSM_CHUNK = 64
EXPERT_ROWS = 512
ROUTE_TILE = 512
VMEM_LIMIT = 56 * 1024 * 1024
HIGHEST = lax.Precision.HIGHEST
Q_SCALE = HEAD_DIM ** -0.5 * math.log2(math.e)


def _cparams(sem):
    return pltpu.CompilerParams(dimension_semantics=sem, vmem_limit_bytes=VMEM_LIMIT)


def _dot(a, b, **kw):
    return jnp.dot(a, b, preferred_element_type=F32, **kw)


def _dot_t(a, b, **kw):
    return lax.dot_general(a, b, (((1,), (1,)), ((), ())), preferred_element_type=F32, **kw)


def _ada_kernel(c_ref, w_ref, b_ref, o_ref):
    c = c_ref[...]
    cond = c * jax.nn.sigmoid(c)
    o_ref[...] = _dot(cond, w_ref[...], precision=HIGHEST) + b_ref[...]


def _ada(c, ada_w, ada_b):
    bsz, d = c.shape
    n = ada_w.shape[1]
    c8 = jnp.zeros((8, d), F32).at[:bsz].set(c)
    out = pl.pallas_call(
        _ada_kernel,
        grid=(n // d,),
        in_specs=[pl.BlockSpec((8, d), lambda j: (0, 0)),
                  pl.BlockSpec((d, d), lambda j: (0, j)),
                  pl.BlockSpec((1, d), lambda j: (0, j))],
        out_specs=pl.BlockSpec((8, d), lambda j: (0, j)),
        out_shape=jax.ShapeDtypeStruct((8, n), F32),
        compiler_params=_cparams(("arbitrary",)),
        name="ada",
    )(c8, ada_w, ada_b.reshape(1, n))
    return out[:bsz]


def _inproj_kernel(x_ref, sc_ref, sh_ref, g_ref, pos_ref, invf_ref, w_ref,
                   u_ref, q_ref, k_ref, v_ref, gs_ref, ga_ref, km_ref, *, ssm_w, attn_w, d_model):
    x = x_ref[...]
    ms = jnp.mean(x * x, axis=-1, keepdims=True)
    xn = x * lax.rsqrt(ms + NORM_EPS) * g_ref[...]
    h = (xn * (1.0 + sc_ref[0]) + sh_ref[0]).astype(BF16)

    def proj(lo, width):
        return _dot(h, w_ref[:, lo:lo + width])

    u_ref[...] = proj(0, ssm_w)

    ang = pos_ref[...].astype(F32) * invf_ref[...]
    reps = attn_w // LANES
    cos = jnp.concatenate([jnp.cos(ang)] * reps, axis=1)
    sin = jnp.concatenate([jnp.sin(ang)] * reps, axis=1)
    lane = lax.broadcasted_iota(I32, (1, attn_w), 1)
    first = (lane % HEAD_DIM) < (HEAD_DIM // 2)
    sin = jnp.where(first, -sin, sin)

    def rope(t):
        rot = jnp.where(first, pltpu.roll(t, attn_w - HEAD_DIM // 2, axis=1),
                        pltpu.roll(t, HEAD_DIM // 2, axis=1))
        return t * cos + rot * sin

    q = rope(proj(ssm_w, attn_w))
    q_ref[...] = (q * Q_SCALE).astype(BF16)
    k = rope(proj(ssm_w + attn_w, attn_w))
    k_ref[...] = k.astype(BF16)
    nblk = k.shape[0] // MOBA_BLOCK
    km_ref[0] = jnp.mean(k.reshape(nblk, MOBA_BLOCK, attn_w), axis=1)
    v_ref[...] = proj(ssm_w + 2 * attn_w, attn_w).astype(BF16)
    gs_ref[...] = jax.nn.sigmoid(proj(ssm_w + 3 * attn_w, d_model)).astype(BF16)
    ga_ref[...] = jax.nn.sigmoid(proj(ssm_w + 3 * attn_w + d_model, d_model)).astype(BF16)


def _inproj(x2, sc, sh, g, pos, w_in, seq, tm):
    n_tok, d = x2.shape
    attn_w = N_HEADS * HEAD_DIM
    ssm_w = SSM_GROUPS * SSM_GROUP_SIZE
    in_w = w_in.shape[1]
    half = HEAD_DIM // 2
    inv_freq = ROPE_THETA ** (-jnp.arange(half, dtype=F32) / half)
    invf = jnp.tile(inv_freq, LANES // half).reshape(1, LANES)
    nt = n_tok // tm
    per_b = seq // tm
    nblk = tm // MOBA_BLOCK
    tok = lambda w: pl.BlockSpec((tm, w), lambda i: (i, 0))
    bvec = pl.BlockSpec((1, 1, d), lambda i: (i // per_b, 0, 0))
    outs = pl.pallas_call(
        functools.partial(_inproj_kernel, ssm_w=ssm_w, attn_w=attn_w, d_model=d),
        grid=(nt,),
        in_specs=[tok(d), bvec, bvec,
                  pl.BlockSpec((1, d), lambda i: (0, 0)),
                  tok(1),
                  pl.BlockSpec((1, LANES), lambda i: (0, 0)),
                  pl.BlockSpec((d, in_w), lambda i: (0, 0))],
        out_specs=[tok(ssm_w), tok(attn_w), tok(attn_w), tok(attn_w), tok(d), tok(d),
                   pl.BlockSpec((1, nblk, attn_w), lambda i: (i, 0, 0))],
        out_shape=[jax.ShapeDtypeStruct((n_tok, ssm_w), F32),
                   jax.ShapeDtypeStruct((n_tok, attn_w), BF16),
                   jax.ShapeDtypeStruct((n_tok, attn_w), BF16),
                   jax.ShapeDtypeStruct((n_tok, attn_w), BF16),
                   jax.ShapeDtypeStruct((n_tok, d), BF16),
                   jax.ShapeDtypeStruct((n_tok, d), BF16),
                   jax.ShapeDtypeStruct((nt, nblk, attn_w), F32)],
        compiler_params=_cparams(("arbitrary",)),
        name="inproj",
    )(x2, sc, sh, g, pos, invf, w_in.astype(BF16))
    return outs


def _s5_group(x, zr, zi, lr, li, ca, cb, ba, bb, t_ref, *, chunk, n_chunks):
    gs = SSM_GROUP_SIZE

    def powers(tau):
        mag = jnp.exp(tau * zr)
        return mag * jnp.cos(tau * zi), mag * jnp.sin(tau * zi)

    tau = lax.broadcasted_iota(I32, (chunk + 8, LANES), 0).astype(F32)
    e_re, e_im = powers(tau)
    lb_re, lb_im = e_re[1:2], e_im[1:2]
    den = lr * lr + li * li
    a, b = lb_re - 1.0, lb_im
    cf_re = (a * lr + b * li) / den
    cf_im = (b * lr - a * li) / den
    bri = cf_re * ba + cf_im * bb
    bri_sw = cf_re * bb - cf_im * ba

    cpow =(e_re[:chunk + 1, None, :] * ca[None] + e_im[:chunk + 1, None, :] * cb[None])
    cpow = cpow.reshape((chunk + 1) * gs, LANES)
    width = chunk * gs
    r = _dot_t(bri, cpow[:width], precision=HIGHEST)
    col = lax.broadcasted_iota(I32, (gs, width), 1)
    t_ref[0:gs, :] = r.astype(BF16)
    for j in range(1, chunk):
        shifted = jnp.where(col >= gs * j, pltpu.roll(r, gs * j, axis=1), 0.0)
        t_ref[gs * j:gs * (j + 1), :] = shifted.astype(BF16)

    y = _dot(x, t_ref[...])

    tau_rev = (chunk - 1) - lax.broadcasted_iota(I32, (chunk, LANES), 0)
    r_re, r_im = powers(tau_rev.astype(F32))
    bst = r_re[:, None, :] * bri[None] + r_im[:, None, :] * bri_sw[None]
    bst = bst.reshape(width, LANES).astype(BF16)
    s = _dot(x, bst)

    rows = s.shape[0]
    n_idx = lax.broadcasted_iota(I32, (rows, LANES), 0) % n_chunks
    lane = lax.broadcasted_iota(I32, (1, LANES), 1)
    half = LANES // 2

    def cmul(v, p_re, p_im):
        return v * p_re + pltpu.roll(v, half, axis=1) * jnp.where(lane < half, -p_im, p_im)

    sh = 1
    while sh < n_chunks:
        p_re, p_im = powers(jnp.full((1, LANES), float(chunk * sh), F32))
        prev = jnp.where(n_idx >= sh, pltpu.roll(s, sh, axis=0), 0.0)
        s = s + cmul(prev, p_re, p_im)
        sh *= 2
    s_in = jnp.where(n_idx >= 1, pltpu.roll(s, 1, axis=0), 0.0)
    return y + _dot_t(s_in.astype(BF16), cpow[gs:gs + width].astype(BF16))


def _s5_kernel(u_ref, sel_ref, zr_ref, zi_ref, lr_ref, li_ref, ca_ref, cb_ref, ba_ref, bb_ref,
               y_ref, x_scr, y_scr, t_ref, *, chunk, n_chunks, rows):
    gpt = LANES // SSM_GROUP_SIZE

    def plane(j):
        return pl.ds(j, rows, stride=chunk)

    for jt in range(chunk // gpt):
        planes = jnp.concatenate([u_ref[plane(jt * gpt + jj), :] for jj in range(gpt)], axis=1)
        grouped = _dot(planes.astype(BF16), sel_ref[...])
        for g in range(gpt):
            x_scr[g, :, jt * LANES:(jt + 1) * LANES] = grouped[:, g * LANES:(g + 1) * LANES].astype(BF16)

    def per_group(g, _):
        y = _s5_group(x_scr[g], zr_ref[g], zi_ref[g], lr_ref[g], li_ref[g], ca_ref[g], cb_ref[g], ba_ref[g],
                      bb_ref[g], t_ref, chunk=chunk, n_chunks=n_chunks)
        y_scr[g] = y.astype(BF16)
        return 0

    lax.fori_loop(0, gpt, per_group, 0)

    for jt in range(chunk // gpt):
        grouped = jnp.concatenate([y_scr[g, :, jt * LANES:(jt + 1) * LANES] for g in range(gpt)], axis=1)
        planes = _dot(grouped, sel_ref[...])
        for jj in range(gpt):
            y_ref[plane(jt * gpt + jj), :] = planes[:, jj * LANES:(jj + 1) * LANES]


def _s5(u, lam_re, lam_im, log_dt, b_re, b_im, c_re, c_im, bsz, seq):
    g_n, gs, p = SSM_GROUPS, SSM_GROUP_SIZE, SSM_STATE
    chunk = SSM_CHUNK
    n_chunks = seq // chunk
    rows = bsz * n_chunks
    width = chunk * gs
    gpt = LANES // gs
    assert chunk % gpt == 0 and g_n % gpt == 0 and 2 * p == LANES
    dt = jnp.exp(log_dt.astype(F32))[:, None]
    dup = lambda a: jnp.concatenate([a, a], axis=-1).reshape(g_n, 1, 2 * p)
    zr, zi = dup(lam_re * dt), dup(lam_im * dt)
    lr, li = dup(lam_re), dup(lam_im)
    bt_re, bt_im = b_re.transpose(0, 2, 1), b_im.transpose(0, 2, 1)
    ba = jnp.concatenate([bt_re, bt_im], axis=-1)
    bb = jnp.concatenate([-bt_im, bt_re], axis=-1)
    ca = jnp.concatenate([c_re, -c_im], axis=-1)
    cb = jnp.concatenate([-c_im, -c_re], axis=-1)
    idx = jnp.arange(gpt * LANES, dtype=I32)
    swapped = ((idx // gs) % gpt) * LANES + (idx // LANES) * gs + idx % gs
    sel = (swapped[:, None] == idx[None, :]).astype(BF16)
    tile = pl.BlockSpec((bsz * seq, LANES), lambda t: (0, t))
    vec = pl.BlockSpec((gpt, 1, 2 * p), lambda t: (t, 0, 0))
    mat = pl.BlockSpec((gpt, gs, 2 * p), lambda t: (t, 0, 0))
    return pl.pallas_call(
        functools.partial(_s5_kernel, chunk=chunk, n_chunks=n_chunks, rows=rows),
        grid=(g_n // gpt,),
        in_specs=[tile, pl.BlockSpec((gpt * LANES, gpt * LANES), lambda t: (0, 0)),
                  vec, vec, vec, vec, mat, mat, mat, mat],
        out_specs=tile,
        out_shape=jax.ShapeDtypeStruct((bsz * seq, g_n * gs), F32),
        scratch_shapes=[pltpu.VMEM((gpt, rows, width), BF16), pltpu.VMEM((gpt, rows, width), BF16),
                        pltpu.VMEM((width, width), BF16)],
        compiler_params=_cparams(("arbitrary",)),
        name="s5",
    )(u, sel, zr, zi, lr, li, ca, cb, ba, bb)


def _moba_kernel(qt_ref, k_ref, vt_ref, km_ref, o_ref, s_a, s_b, s_fin, p_a, p_b, *, nb, nbp):
    blk = MOBA_BLOCK
    group = 2
    keys = nq = group * blk
    i = pl.program_id(2)
    qt = qt_ref[...]
    km = km_ref[...]
    km_hi = km.astype(BF16)
    km_lo = (km - km_hi.astype(F32)).astype(BF16)
    dim_i = lax.broadcasted_iota(I32, (LANES, nq), 0)
    blk_i = lax.broadcasted_iota(I32, (nbp, nq), 0)
    blk_f = blk_i.astype(F32)
    own = group * i + lax.broadcasted_iota(I32, (nbp, nq), 1) // blk
    valid = blk_i < own

    q_loop, q_fin = [], []
    for h in range(2):
        qh = jnp.where((dim_i >= h * HEAD_DIM) & (dim_i < (h + 1) * HEAD_DIM), qt, jnp.zeros_like(qt))
        g = jnp.where(valid, _dot(km_hi, qh) + _dot(km_lo, qh), NEG_INF)
        sel = jnp.zeros((nbp, nq), jnp.bool_)
        for _ in range(MOBA_TOPK):
            m = jnp.max(g, axis=0, keepdims=True)
            idx = jnp.min(jnp.where(g == m, blk_f, float(nbp)), axis=0, keepdims=True)
            pick = blk_f == idx
            sel = sel | pick
            g = jnp.where(pick, -jnp.inf, g)
        allowed = sel & valid
        for lst, ok in ((q_loop, allowed & (blk_i < group * i)), (q_fin, allowed | (blk_i == own))):
            parts = [qh, jnp.where(ok, 0.0, NEG_INF).astype(BF16)]
            if nbp < LANES:
                parts.append(jnp.full((LANES - nbp, nq), NEG_INF, BF16))
            lst.append(jnp.concatenate(parts, axis=0))

    def scores(grp, s_ref, q_aug, causal=False):
        kb0 = grp * group
        k_cat = k_ref[pl.ds(jnp.minimum(kb0, nb - group), group)].reshape(keys, LANES)
        blk_id = kb0 + lax.broadcasted_iota(I32, (keys, LANES), 0) // blk
        onehot = jnp.where(lax.broadcasted_iota(I32, (keys, LANES), 1) == blk_id, 1.0, 0.0).astype(BF16)
        k_aug = jnp.concatenate([k_cat, onehot], axis=1)
        mx = []
        for h in range(2):
            s = _dot(k_aug, q_aug[h])
            if causal:
                key_i = lax.broadcasted_iota(I32, (keys, nq), 0)
                s = jnp.where(key_i <= lax.broadcasted_iota(I32, (keys, nq), 1), s, NEG_INF)
            s_ref[h] = s
            mx.append(jnp.max(s, axis=0, keepdims=True))
        return tuple(mx)

    def softmax(s_ref, mx, p_ref, st):
        out = []
        for h in range(2):
            m_i, _, l_i, acc = st[h]
            m_new = jnp.maximum(m_i, mx[h])
            p_ref[h] = jnp.exp2(s_ref[h] - m_new).astype(BF16)
            out.append((m_new, jnp.exp2(m_i - m_new), l_i, acc))
        return tuple(out)

    def values(grp, p_ref, st):
        out = []
        for h in range(2):
            m_i, alpha, l_i, acc = st[h]
            blocks = [vt_ref[jnp.clip(grp * group + j, 0, nb - 1), h * HEAD_DIM:(h + 1) * HEAD_DIM, :]
                      for j in range(group)]
            ones = jnp.ones((16, keys), BF16)
            pv = _dot(jnp.concatenate([jnp.concatenate(blocks, axis=1), ones], axis=0), p_ref[h])
            out.append((m_i, alpha, alpha * l_i + pv[HEAD_DIM:HEAD_DIM + 1], alpha * acc + pv[:HEAD_DIM]))
        return tuple(out)

    mx_fin = scores(i, s_fin, q_fin, causal=True)
    mx_first = scores(0, s_a, q_loop)
    p_b[...] = jnp.zeros_like(p_b)

    def body(t, carry):
        st, mx_a = carry
        mx_b = scores(2 * t + 1, s_b, q_loop)
        st = values(2 * t - 1, p_b, st)
        st = softmax(s_a, mx_a, p_a, st)
        mx_a = scores(2 * t + 2, s_a, q_loop)
        st = values(2 * t, p_a, st)
        return softmax(s_b, mx_b, p_b, st), mx_a

    init = (jnp.full((1, nq), -jnp.inf, F32), jnp.ones((1, nq), F32), jnp.zeros((1, nq), F32),
            jnp.zeros((HEAD_DIM, nq), F32))
    trips = (i + 1) // 2
    st, _ = lax.fori_loop(0, trips, body, ((init, init), mx_first))
    st = values(2 * trips - 1, p_b, st)
    st = values(i, p_a, softmax(s_fin, mx_fin, p_a, st))
    for h in range(2):
        _, _, l_i, acc = st[h]
        o_ref[h * HEAD_DIM:(h + 1) * HEAD_DIM, :] = (acc / l_i).astype(BF16)


def _moba(q, k, v, km, bsz, seq):
    n_tok, attn_w = q.shape
    blk = MOBA_BLOCK
    nb = seq // blk
    group = 2
    nbp = -(-nb // 16) * 16
    assert nb % group == 0 and nb + 2 * group <= LANES
    hp = attn_w // LANES
    km_pad = jnp.zeros((bsz, nbp, attn_w), F32).at[:, :nb].set(km.reshape(bsz, nb, attn_w))
    qt = q.reshape(bsz, seq, attn_w).transpose(0, 2, 1)
    k4 = k.reshape(bsz, nb, blk, attn_w)
    vt4 = v.reshape(bsz, nb, blk, attn_w).transpose(0, 1, 3, 2)
    ot = pl.pallas_call(
        functools.partial(_moba_kernel, nb=nb, nbp=nbp),
        grid=(bsz, hp, nb // group),
        in_specs=[pl.BlockSpec((None, LANES, group * blk), lambda b, p, i: (b, p, i)),
                  pl.BlockSpec((None, nb, blk, LANES), lambda b, p, i: (b, 0, 0, p)),
                  pl.BlockSpec((None, nb, LANES, blk), lambda b, p, i: (b, 0, p, 0)),
                  pl.BlockSpec((None, nbp, LANES), lambda b, p, i: (b, 0, p))],
        out_specs=pl.BlockSpec((None, LANES, group * blk), lambda b, p, i: (b, p, i)),
        out_shape=jax.ShapeDtypeStruct((bsz, attn_w, seq), BF16),
        scratch_shapes=([pltpu.VMEM((2, group * blk, group * blk), F32)] * 3
                        + [pltpu.VMEM((2, group * blk, group * blk), BF16)] * 2),
        compiler_params=_cparams(("arbitrary", "arbitrary", "arbitrary")),
        name="moba",
    )(qt, k4, vt4, km_pad)
    return ot.transpose(0, 2, 1).reshape(n_tok, attn_w)


def _rms(t, g):
    return t * lax.rsqrt(jnp.mean(t * t, axis=-1, keepdims=True) + NORM_EPS) * g


def _store_slab(ref, val, base=0):
    rows, d = val.shape
    sub = d // LANES
    for s in range(sub):
        ref[pl.ds(base + s, rows, stride=sub), :] = val[:, s * LANES:(s + 1) * LANES]


def _load_slab(ref, rows, sub, base=0):
    return jnp.concatenate([ref[pl.ds(base + s, rows, stride=sub), :] for s in range(sub)], axis=1)


def _merge_kernel(ys_ref, u_ref, ya_ref, gs_ref, ga_ref, x_ref, d_ref, wglu_ref, wsb_ref, wab_ref,
                  wout_ref, postg_ref, gtm_ref, preg_ref, scf_ref, shf_ref, rw_ref, rb_ref,
                  x1_ref, h2_ref, eid_ref, tw_ref, rank_ref, before_ref, cnt_ref, run_ref):
    @pl.when(pl.program_id(0) == 0)
    def _():
        run_ref[...] = jnp.zeros_like(run_ref)

    y = ys_ref[...] + d_ref[...] * u_ref[...].astype(F32)
    y = jax.nn.gelu(y)
    y = y * jax.nn.sigmoid(_dot(y.astype(BF16), wglu_ref[...]))
    bs = _dot(y.astype(BF16), wsb_ref[...])
    ba = _dot(ya_ref[...], wab_ref[...])
    merged = gs_ref[...].astype(F32) * bs + ga_ref[...].astype(F32) * ba
    mix = _dot(merged.astype(BF16), wout_ref[...])
    x1 = x_ref[...] + gtm_ref[0] * _rms(mix, postg_ref[...])
    x1_ref[...] = x1
    h2 = _rms(x1, preg_ref[...]) * (1.0 + scf_ref[0]) + shf_ref[0]
    _store_slab(h2_ref, h2)

    h_hi = h2.astype(BF16)
    h_lo = (h2 - h_hi.astype(F32)).astype(BF16)
    logits = _dot(jnp.concatenate([h_hi, h_hi, h_lo], axis=1), rw_ref[...]) + rb_ref[...]
    tm = logits.shape[0]
    lane = lax.broadcasted_iota(I32, (tm, LANES), 1)
    lane_f = lane.astype(F32)
    g = logits
    vals, picks, idxs = [], [], []
    for _ in range(TOP_K):
        m = jnp.max(g, axis=1, keepdims=True)
        idx = jnp.min(jnp.where(g == m, lane_f, float(LANES)), axis=1, keepdims=True)
        pick = lane_f == idx
        vals.append(m)
        idxs.append(idx.astype(I32))
        picks.append(pick)
        g = jnp.where(pick, -jnp.inf, g)
    exps = [jnp.exp(v - vals[0]) for v in vals]
    tot = exps[0] + exps[1] + exps[2] + exps[3]

    onehot = jnp.where(picks[0] | picks[1] | picks[2] | picks[3], 1.0, 0.0)
    r_i = lax.broadcasted_iota(I32, (tm, tm), 0)
    c_i = lax.broadcasted_iota(I32, (tm, tm), 1)
    tri = jnp.where(c_i < r_i, 1.0, 0.0).astype(BF16)
    rank_full = _dot(tri, onehot.astype(BF16))
    before_ref[0] = run_ref[...]
    run_ref[...] = run_ref[...] + jnp.sum(onehot, axis=0, keepdims=True)
    cnt_ref[...] = run_ref[...]

    eid = jnp.zeros((tm, LANES), I32)
    tw = jnp.zeros((tm, LANES), F32)
    rk = jnp.zeros((tm, LANES), F32)
    for r in range(TOP_K):
        eid = jnp.where(lane == r, idxs[r], eid)
        tw = jnp.where(lane == r, exps[r] / tot, tw)
        pos = jnp.sum(jnp.where(picks[r], rank_full, 0.0), axis=1, keepdims=True)
        rk = jnp.where(lane == r, pos, rk)
    eid_ref[...] = eid
    tw_ref[...] = tw
    rank_ref[...] = rk.astype(I32)


def _merge(ys, u, ya, gs, ga, x2, ssm_d, w_glu, w_sb, w_ab, w_out, post_g, gt_m, pre_g, sc_f, sh_f,
           router_w, router_b, seq, tm):
    n_tok, d = x2.shape
    sw = ys.shape[1]
    aw = ya.shape[1]
    ne = router_w.shape[1]
    per_b = seq // tm
    rw = jnp.zeros((d, LANES), F32).at[:, :ne].set(router_w)
    rw_hi = rw.astype(BF16)
    rw_lo = (rw - rw_hi.astype(F32)).astype(BF16)
    rw = jnp.concatenate([rw_hi, rw_lo, rw_hi], axis=0)
    rb = jnp.full((1, LANES), NEG_INF, F32).at[0, :ne].set(router_b)
    tok = lambda w: pl.BlockSpec((tm, w), lambda i: (i, 0))
    full = lambda a, b: pl.BlockSpec((a, b), lambda i: (0, 0))
    bvec = pl.BlockSpec((1, 1, d), lambda i: (i // per_b, 0, 0))
    return pl.pallas_call(
        _merge_kernel,
        grid=(n_tok // tm,),
        in_specs=[tok(sw), tok(sw), tok(aw), tok(d), tok(d), tok(d),
                  full(1, sw), full(sw, sw), full(sw, d), full(aw, d), full(d, d),
                  full(1, d), bvec, full(1, d), bvec, bvec, full(3 * d, LANES), full(1, LANES)],
        out_specs=[tok(d), pl.BlockSpec((tm * (d // LANES), LANES), lambda i: (i, 0)),
                   tok(LANES), tok(LANES), tok(LANES),
                   pl.BlockSpec((1, 1, LANES), lambda i: (i, 0, 0)), full(1, LANES)],
        out_shape=[jax.ShapeDtypeStruct((n_tok, d), F32),
                   jax.ShapeDtypeStruct((n_tok * (d // LANES), LANES), F32),
                   jax.ShapeDtypeStruct((n_tok, LANES), I32),
                   jax.ShapeDtypeStruct((n_tok, LANES), F32),
                   jax.ShapeDtypeStruct((n_tok, LANES), I32),
                   jax.ShapeDtypeStruct((n_tok // tm, 1, LANES), F32),
                   jax.ShapeDtypeStruct((1, LANES), F32)],
        scratch_shapes=[pltpu.VMEM((1, LANES), F32)],
        compiler_params=_cparams(("arbitrary",)),
        name="merge",
    )(ys, u, ya, gs, ga, x2, ssm_d.reshape(1, sw), w_glu.astype(BF16), w_sb.astype(BF16),
      w_ab.astype(BF16), w_out.astype(BF16), post_g.reshape(1, d), gt_m, pre_g.reshape(1, d),
      sc_f, sh_f, rw, rb)


def _chunk_copies(cnt_ref, start_ref, tile, ne, max_rows, make):
    def per_expert(e, off):
        c, start = cnt_ref[tile * ne + e], start_ref[tile * ne + e]
        for b in range(max_rows.bit_length()):
            n = 1 << b

            @pl.when((c >> b) & 1 == 1)
            def _():
                make(off + (c & (n - 1)), start + (c & (n - 1)), n)
        return off + c

    lax.fori_loop(0, ne, per_expert, 0)


def _dispatch_kernel(cnt_ref, start_ref, pad_start_ref, pad_cnt_ref, tail_ref, loct_ref, h_ref, xs_ref,
                     buf, zero_ref, sem, zsem, *, tm, sub, bm, n_rows, ne):
    i = pl.program_id(0)
    last = pl.num_programs(0) - 1
    slot = i % 2
    n_loc = tm * TOP_K

    def row(ref, r, n=1):
        return ref.at[pl.ds(pl.multiple_of(r * sub, sub), n * sub)]

    def wait_slot(s):
        pltpu.make_async_copy(buf.at[s], row(xs_ref, 0, n_loc), sem.at[s]).wait()

    def zero_copies(do):
        def per_expert(e, _):
            start, cnt = pad_start_ref[e], pad_cnt_ref[e]
            for b in range(bm.bit_length() - 1):
                n = 1 << b

                @pl.when((cnt >> b) & 1 == 1)
                def _():
                    do(pltpu.make_async_copy(row(zero_ref, 0, n), row(xs_ref, start + (cnt & (n - 1)), n), zsem))
            return 0

        lax.fori_loop(0, ne, per_expert, 0)

        def per_block(j, _):
            r = tail_ref[0] + j * bm

            @pl.when(r < n_rows)
            def _():
                do(pltpu.make_async_copy(row(zero_ref, 0, bm), row(xs_ref, r, bm), zsem))
            return 0

        lax.fori_loop(0, ne, per_block, 0)

    @pl.when(i == 0)
    def _():
        zero_ref[...] = jnp.zeros_like(zero_ref)
        zero_copies(lambda cp: cp.start())

    @pl.when(i >= 2)
    def _():
        wait_slot(slot)

    loct = loct_ref[0]
    row_i = lax.broadcasted_iota(I32, (n_loc, tm), 0)
    place = row_i == loct[0:1, :]
    for k in range(1, TOP_K):
        place = place | (row_i == loct[k:k + 1, :])
    grouped = _dot(jnp.where(place, 1.0, 0.0).astype(BF16), _load_slab(h_ref, tm, sub).astype(BF16))
    _store_slab(buf.at[slot], grouped)
    _chunk_copies(cnt_ref, start_ref, i, ne, tm,
                  lambda loc, glob, n: pltpu.make_async_copy(row(buf.at[slot], loc, n), row(xs_ref, glob, n),
                                                             sem.at[slot]).start())

    @pl.when(i == last)
    def _():
        wait_slot(slot)

    @pl.when((i == last) & (i >= 1))
    def _():
        wait_slot(1 - slot)

    @pl.when(i == 0)
    def _():
        zero_copies(lambda cp: cp.wait())


def _dispatch(h_slab, loc, cnt_tile, start_tile, pad_start, pad_cnt, tail, n_rows, sub, bm, tm):
    n_tok = loc.shape[0]
    nt = n_tok // tm
    ne = pad_start.shape[0]
    loct = jnp.full((nt, 8, tm), -1, I32).at[:, :TOP_K].set(loc.reshape(nt, tm, TOP_K).transpose(0, 2, 1))
    grid_spec = pltpu.PrefetchScalarGridSpec(
        num_scalar_prefetch=5,
        grid=(nt,),
        in_specs=[pl.BlockSpec((1, 8, tm), lambda i, *_: (i, 0, 0)),
                  pl.BlockSpec((tm * sub, LANES), lambda i, *_: (i, 0))],
        out_specs=pl.BlockSpec(memory_space=pl.ANY),
        scratch_shapes=[pltpu.VMEM((2, TOP_K * tm * sub, LANES), F32), pltpu.VMEM((bm * sub, LANES), F32),
                        pltpu.SemaphoreType.DMA((2,)), pltpu.SemaphoreType.DMA(())],
    )
    return pl.pallas_call(
        functools.partial(_dispatch_kernel, tm=tm, sub=sub, bm=bm, n_rows=n_rows, ne=ne),
        grid_spec=grid_spec,
        out_shape=jax.ShapeDtypeStruct((n_rows * sub, LANES), F32),
        compiler_params=_cparams(("arbitrary",)),
        name="dispatch",
    )(cnt_tile.reshape(-1), start_tile.reshape(-1), pad_start, pad_cnt, tail, loct, h_slab)


def _expert_kernel(be_ref, nv_ref, x_ref, wg_ref, bg_ref, wu_ref, bu_ref, wd_ref, bd_ref,
                   o_ref, wg_s, wu_s, wd_s, *, bm, sub):
    i = pl.program_id(0)
    prev = be_ref[jnp.maximum(i - 1, 0)]
    changed = (i == 0) | (be_ref[i] != prev)

    @pl.when(changed)
    def _():
        wg_s[...] = wg_ref[0].astype(BF16)
        wu_s[...] = wu_ref[0].astype(BF16)
        wd_s[...] = wd_ref[0].astype(BF16)

    @pl.when(i < nv_ref[0])
    def _():
        xb = _load_slab(x_ref, bm, sub).astype(BF16)
        g = _dot(xb, wg_s[...]) + bg_ref[0]
        u = _dot(xb, wu_s[...]) + bu_ref[0]
        g = jnp.minimum(g, SWIGLU_LIMIT)
        u = jnp.clip(u, -SWIGLU_LIMIT, SWIGLU_LIMIT)
        act = g * jax.nn.sigmoid(SWIGLU_ALPHA * g) * (u + 1.0)
        _store_slab(o_ref, _dot(act.astype(BF16), wd_s[...]) + bd_ref[0])

    @pl.when(i >= nv_ref[0])
    def _():
        o_ref[...] = jnp.zeros_like(o_ref)


def _experts(xs, blk_exp, n_valid, w_gate, b_gate, w_up, b_up, w_down, b_down, bm):
    ne, d, f = w_gate.shape
    sub = d // LANES
    n_blocks = xs.shape[0] // (bm * sub)
    wspec = lambda a, b: pl.BlockSpec((1, a, b), lambda i, be, nv: (be[i], 0, 0))
    grid_spec = pltpu.PrefetchScalarGridSpec(
        num_scalar_prefetch=2,
        grid=(n_blocks,),
        in_specs=[pl.BlockSpec((bm * sub, LANES), lambda i, be, nv: (jnp.clip(nv[0] - 1, 0, i), 0)),
                  wspec(d, f), wspec(1, f), wspec(d, f), wspec(1, f), wspec(f, d), wspec(1, d)],
        out_specs=pl.BlockSpec((bm * sub, LANES), lambda i, be, nv: (i, 0)),
        scratch_shapes=[pltpu.VMEM((d, f), BF16), pltpu.VMEM((d, f), BF16), pltpu.VMEM((f, d), BF16)],
    )
    return pl.pallas_call(
        functools.partial(_expert_kernel, bm=bm, sub=sub),
        grid_spec=grid_spec,
        out_shape=jax.ShapeDtypeStruct(xs.shape, F32),
        compiler_params=_cparams(("arbitrary",)),
        name="experts",
    )(blk_exp, n_valid, xs, w_gate, b_gate.reshape(ne, 1, f), w_up, b_up.reshape(ne, 1, f),
      w_down, b_down.reshape(ne, 1, d))


def _combine_kernel(cnt_ref, start_ref, yb_ref, loc_ref, tw_ref, x1_ref, g_ref, gt_ref, o_ref, buf, sem,
                    *, tm, sub, ne):
    i = pl.program_id(0)
    n_loc = tm * TOP_K

    def start_tile(tile, slot):
        def make(loc, glob, n):
            src = yb_ref.at[pl.ds(pl.multiple_of(glob * sub, sub), n * sub)]
            dst = buf.at[slot, pl.ds(pl.multiple_of(loc * sub, sub), n * sub)]
            pltpu.make_async_copy(src, dst, sem.at[slot]).start()

        _chunk_copies(cnt_ref, start_ref, tile, ne, tm, make)

    slot = i % 2

    @pl.when(i == 0)
    def _():
        start_tile(0, 0)

    @pl.when(i + 1 < pl.num_programs(0))
    def _():
        start_tile(i + 1, 1 - slot)

    pltpu.make_async_copy(yb_ref.at[pl.ds(0, n_loc * sub)], buf.at[slot], sem.at[slot]).wait()
    rows = _load_slab(buf.at[slot], n_loc, sub).astype(BF16)
    loc, tw = loc_ref[...], tw_ref[...]
    col_i = lax.broadcasted_iota(I32, (tm, n_loc), 1)
    w = jnp.zeros((tm, n_loc), F32)
    for k in range(TOP_K):
        w = jnp.where(col_i == loc[:, k:k + 1], tw[:, k:k + 1], w)
    ff = _dot(w.astype(BF16), rows)
    o_ref[...] = x1_ref[...] + gt_ref[0] * _rms(ff, g_ref[...])


def _combine(yb, loc, tw, cnt_tile, start_tile, x1, post_g, gt_f, seq, tm):
    n_tok, d = x1.shape
    sub = d // LANES
    nt = n_tok // tm
    ne = cnt_tile.shape[1]
    per_b = seq // tm
    loc_l = jnp.full((n_tok, LANES), -1, I32).at[:, :TOP_K].set(loc)
    grid_spec = pltpu.PrefetchScalarGridSpec(
        num_scalar_prefetch=2,
        grid=(nt,),
        in_specs=[pl.BlockSpec(memory_space=pl.ANY),
                  pl.BlockSpec((tm, LANES), lambda i, *_: (i, 0)),
                  pl.BlockSpec((tm, LANES), lambda i, *_: (i, 0)),
                  pl.BlockSpec((tm, d), lambda i, *_: (i, 0)),
                  pl.BlockSpec((1, d), lambda i, *_: (0, 0)),
                  pl.BlockSpec((1, 1, d), lambda i, *_: (i // per_b, 0, 0))],
        out_specs=pl.BlockSpec((tm, d), lambda i, *_: (i, 0)),
        scratch_shapes=[pltpu.VMEM((2, TOP_K * tm * sub, LANES), F32), pltpu.SemaphoreType.DMA((2,))],
    )
    return pl.pallas_call(
        functools.partial(_combine_kernel, tm=tm, sub=sub, ne=ne),
        grid_spec=grid_spec,
        out_shape=jax.ShapeDtypeStruct((n_tok, d), F32),
        compiler_params=_cparams(("arbitrary",)),
        name="combine",
    )(cnt_tile.reshape(-1), start_tile.reshape(-1), yb, loc_l, tw, x1, post_g.reshape(1, d), gt_f)


def kernel(x, c, positions, ada_w, ada_b, mix_pre_g, mix_post_g, ffn_pre_g, ffn_post_g, w_in, ssm_lam_re, ssm_lam_im, ssm_log_dt, ssm_b_re, ssm_b_im, ssm_c_re, ssm_c_im, ssm_d, ssm_w_glu, w_ssm_branch, w_attn_branch, w_out, router_w, router_b, w_gate, b_gate, w_up, b_up, w_down, b_down):
    bsz, seq, d = x.shape
    depth = ada_w.shape[0]
    n_tok = bsz * seq
    bm = EXPERT_ROWS
    sub = d // LANES
    xcur = x.reshape(n_tok, d)
    pos = positions.reshape(n_tok, 1).astype(I32)
    for l in range(depth):
        ada = _ada(c, ada_w[l], ada_b[l])
        sh_m, sc_m, gt_m, sh_f, sc_f, gt_f = [a.reshape(bsz, 1, d) for a in jnp.split(ada, 6, axis=-1)]

        u, q, k, v, gs, ga, km = _inproj(xcur, sc_m, sh_m, mix_pre_g[l].reshape(1, d), pos, w_in[l],
                                         seq, tm=min(512, seq))
        ys = _s5(u, ssm_lam_re[l], ssm_lam_im[l], ssm_log_dt[l], ssm_b_re[l], ssm_b_im[l],
                 ssm_c_re[l], ssm_c_im[l], bsz, seq)
        ya = _moba(q, k, v, km, bsz, seq)
        tm = ROUTE_TILE
        x1, h2, eid, tw, rank, before, cnt = _merge(
            ys, u, ya, gs, ga, xcur, ssm_d[l], ssm_w_glu[l], w_ssm_branch[l], w_attn_branch[l], w_out[l],
            mix_post_g[l], gt_m, ffn_pre_g[l], sc_f, sh_f, router_w[l], router_b[l], seq, tm=tm)

        ne = router_w.shape[-1]
        nt = n_tok // tm
        experts = jnp.arange(ne, dtype=I32)
        counts = cnt[0, :ne].astype(I32)
        padded = (counts + bm - 1) // bm * bm
        p_ends = jnp.cumsum(padded)
        p_starts = p_ends - padded
        before = before.reshape(nt, LANES)[:, :ne].astype(I32)
        cnt_tile = jnp.concatenate([before[1:], counts[None]], axis=0) - before
        start_tile = p_starts[None, :] + before
        off_tile = jnp.cumsum(cnt_tile, axis=1) - cnt_tile
        onehot = eid[:, :TOP_K].reshape(nt, tm, TOP_K, 1) == experts
        loc = (rank[:, :TOP_K] + jnp.sum(jnp.where(onehot, off_tile[:, None, None, :], 0), axis=-1).reshape(n_tok, TOP_K))
        n_blocks = (n_tok * TOP_K) // bm + ne
        n_rows = n_blocks * bm
        blk_start = jnp.arange(n_blocks, dtype=I32)[:, None] * bm
        blk_exp = jnp.minimum(jnp.sum((blk_start >= p_ends[None, :]).astype(I32), axis=1), ne - 1)
        n_valid = (p_ends[-1:] // bm).astype(I32)

        xs = _dispatch(h2, loc, cnt_tile, start_tile, p_starts + counts, padded - counts, p_ends[-1:],
                       n_rows, sub, bm, tm)
        yb = _experts(xs, blk_exp, n_valid, w_gate[l], b_gate[l], w_up[l], b_up[l], w_down[l], b_down[l], bm)
        xcur = _combine(yb, loc, tw, cnt_tile, start_tile, x1, ffn_post_g[l], gt_f, seq, tm)
    return xcur.reshape(bsz, seq, d).astype(x.dtype)
```

```python
import functools
import math

import jax
import jax.numpy as jnp
from jax import lax
from jax.experimental import pallas as pl
from jax.experimental.pallas import tpu as pltpu

F32 = jnp.float32
BF16 = jnp.bfloat16
I32 = jnp.int32

N_HEADS = 8
HEAD_DIM = 64
ROPE_THETA = 10000.0
MOBA_BLOCK = 256
MOBA_TOPK = 3
SSM_GROUP_SIZE = 16
SSM_GROUPS = 32
SSM_STATE = 64
N_EXPERTS = 32
TOP_K = 4
SWIGLU_ALPHA = 1.702
SWIGLU_LIMIT = 7.0
NORM_EPS = 1e-6
NEG_INF = -1e30

LANES = 128
SSM_CHUNK = 64
EXPERT_ROWS = 512
ROUTE_TILE = 512
VMEM_LIMIT = 56 * 1024 * 1024
HIGHEST = lax.Precision.HIGHEST
Q_SCALE = HEAD_DIM ** -0.5 * math.log2(math.e)


def _cparams(sem):
    return pltpu.CompilerParams(dimension_semantics=sem, vmem_limit_bytes=VMEM_LIMIT)


def _dot(a, b, **kw):
    return jnp.dot(a, b, preferred_element_type=F32, **kw)


def _dot_t(a, b, **kw):
    return lax.dot_general(a, b, (((1,), (1,)), ((), ())), preferred_element_type=F32, **kw)


def _ada_kernel(c_ref, w_ref, b_ref, o_ref):
    c = c_ref[...]
    cond = c * jax.nn.sigmoid(c)
    o_ref[...] = _dot(cond, w_ref[...], precision=HIGHEST) + b_ref[...]


def _ada(c, ada_w, ada_b):
    bsz, d = c.shape
    n = ada_w.shape[1]
    c8 = jnp.zeros((8, d), F32).at[:bsz].set(c)
    out = pl.pallas_call(
        _ada_kernel,
        grid=(n // d,),
        in_specs=[pl.BlockSpec((8, d), lambda j: (0, 0)),
                  pl.BlockSpec((d, d), lambda j: (0, j)),
                  pl.BlockSpec((1, d), lambda j: (0, j))],
        out_specs=pl.BlockSpec((8, d), lambda j: (0, j)),
        out_shape=jax.ShapeDtypeStruct((8, n), F32),
        compiler_params=_cparams(("arbitrary",)),
        name="ada",
    )(c8, ada_w, ada_b.reshape(1, n))
    return out[:bsz]


def _inproj_kernel(x_ref, sc_ref, sh_ref, g_ref, pos_ref, invf_ref, post_ref, invft_ref, w_ref, wt_ref,
                   u_ref, qt_ref, k_ref, vt_ref, gs_ref, ga_ref, km_ref, *, ssm_w, attn_w, d_model):
    x = x_ref[...]
    ms = jnp.mean(x * x, axis=-1, keepdims=True)
    xn = x * lax.rsqrt(ms + NORM_EPS) * g_ref[...]
    h = (xn * (1.0 + sc_ref[0]) + sh_ref[0]).astype(BF16)

    def proj(lo, width):
        return _dot(h, w_ref[:, lo:lo + width])

    u_ref[...] = proj(0, ssm_w)

    ang = pos_ref[...].astype(F32) * invf_ref[...]
    reps = attn_w // LANES
    cos = jnp.concatenate([jnp.cos(ang)] * reps, axis=1)
    sin = jnp.concatenate([jnp.sin(ang)] * reps, axis=1)
    lane = lax.broadcasted_iota(I32, (1, attn_w), 1)
    first = (lane % HEAD_DIM) < (HEAD_DIM // 2)
    sin = jnp.where(first, -sin, sin)

    def rope(t):
        rot = jnp.where(first, pltpu.roll(t, attn_w - HEAD_DIM // 2, axis=1),
                        pltpu.roll(t, HEAD_DIM // 2, axis=1))
        return t * cos + rot * sin

    k = rope(proj(ssm_w + attn_w, attn_w))
    k_ref[...] = k.astype(BF16)
    tm = k.shape[0]
    nblk = tm // MOBA_BLOCK
    km_ref[0] = jnp.mean(k.reshape(nblk, MOBA_BLOCK, attn_w), axis=1)

    half = HEAD_DIM // 2
    ang_t = invft_ref[...] * post_ref[0].astype(F32)
    cos_t, sin_t = jnp.cos(ang_t), jnp.sin(ang_t)
    qt = _dot_t(wt_ref[0:attn_w, :], h).reshape(N_HEADS, 2, half, tm)
    t1, t2 = qt[:, 0], qt[:, 1]
    qt = jnp.stack([t1 * cos_t - t2 * sin_t, t2 * cos_t + t1 * sin_t], axis=1).reshape(attn_w, tm)
    qt_ref[...] = (qt * Q_SCALE).astype(BF16)
    vt = _dot_t(wt_ref[attn_w:2 * attn_w, :], h).astype(BF16)
    for j in range(nblk):
        vt_ref[j] = vt[:, j * MOBA_BLOCK:(j + 1) * MOBA_BLOCK]
    gs_ref[...] = jax.nn.sigmoid(proj(ssm_w + 3 * attn_w, d_model)).astype(BF16)
    ga_ref[...] = jax.nn.sigmoid(proj(ssm_w + 3 * attn_w + d_model, d_model)).astype(BF16)


def _inproj(x2, sc, sh, g, pos, w_in, seq, tm):
    n_tok, d = x2.shape
    attn_w = N_HEADS * HEAD_DIM
    ssm_w = SSM_GROUPS * SSM_GROUP_SIZE
    in_w = w_in.shape[1]
    half = HEAD_DIM // 2
    inv_freq = ROPE_THETA ** (-jnp.arange(half, dtype=F32) / half)
    invf = jnp.tile(inv_freq, LANES // half).reshape(1, LANES)
    nt = n_tok // tm
    per_b = seq // tm
    nblk = tm // MOBA_BLOCK
    bsz = n_tok // seq
    blk = MOBA_BLOCK
    w_bf = w_in.astype(BF16)
    q_lo, v_lo = ssm_w, ssm_w + 2 * attn_w
    wt = jnp.concatenate([w_bf[:, q_lo:q_lo + attn_w], w_bf[:, v_lo:v_lo + attn_w]], axis=1).T
    tok = lambda w: pl.BlockSpec((tm, w), lambda i: (i, 0))
    full = lambda a, b: pl.BlockSpec((a, b), lambda i: (0, 0))
    bvec = pl.BlockSpec((1, 1, d), lambda i: (i // per_b, 0, 0))
    outs = pl.pallas_call(
        functools.partial(_inproj_kernel, ssm_w=ssm_w, attn_w=attn_w, d_model=d),
        grid=(nt,),
        in_specs=[tok(d), bvec, bvec, full(1, d), tok(1), full(1, LANES),
                  pl.BlockSpec((1, 1, tm), lambda i: (i, 0, 0)), full(half, 1),
                  full(d, in_w), full(2 * attn_w, d)],
        out_specs=[tok(ssm_w),
                   pl.BlockSpec((None, attn_w, tm), lambda i: (i // per_b, 0, i % per_b)),
                   tok(attn_w),
                   pl.BlockSpec((None, nblk, attn_w, blk), lambda i: (i // per_b, i % per_b, 0, 0)),
                   tok(d), tok(d),
                   pl.BlockSpec((1, nblk, attn_w), lambda i: (i, 0, 0))],
        out_shape=[jax.ShapeDtypeStruct((n_tok, ssm_w), F32),
                   jax.ShapeDtypeStruct((bsz, attn_w, seq), BF16),
                   jax.ShapeDtypeStruct((n_tok, attn_w), BF16),
                   jax.ShapeDtypeStruct((bsz, seq // blk, attn_w, blk), BF16),
                   jax.ShapeDtypeStruct((n_tok, d), BF16),
                   jax.ShapeDtypeStruct((n_tok, d), BF16),
                   jax.ShapeDtypeStruct((nt, nblk, attn_w), F32)],
        compiler_params=_cparams(("arbitrary",)),
        name="inproj",
    )(x2, sc, sh, g, pos, invf, pos.reshape(nt, 1, tm), inv_freq.reshape(half, 1), w_bf, wt)
    return outs


def _s5_group(x, zr, zi, lr, li, ca, cb, ba, bb, t_ref, *, chunk, n_chunks):
    gs = SSM_GROUP_SIZE

    def powers(tau):
        mag = jnp.exp(tau * zr)
        return mag * jnp.cos(tau * zi), mag * jnp.sin(tau * zi)

    tau = lax.broadcasted_iota(I32, (chunk + 8, LANES), 0).astype(F32)
    e_re, e_im = powers(tau)
    lb_re, lb_im = e_re[1:2], e_im[1:2]
    den = lr * lr + li * li
    a, b = lb_re - 1.0, lb_im
    cf_re = (a * lr + b * li) / den
    cf_im = (b * lr - a * li) / den
    bri = cf_re * ba + cf_im * bb
    bri_sw = cf_re * bb - cf_im * ba

    cpow =(e_re[:chunk + 1, None, :] * ca[None] + e_im[:chunk + 1, None, :] * cb[None])
    cpow = cpow.reshape((chunk + 1) * gs, LANES)
    width = chunk * gs
    r = _dot_t(bri, cpow[:width], precision=HIGHEST)
    col = lax.broadcasted_iota(I32, (gs, width), 1)
    t_ref[0:gs, :] = r.astype(BF16)
    for j in range(1, chunk):
        shifted = jnp.where(col >= gs * j, pltpu.roll(r, gs * j, axis=1), 0.0)
        t_ref[gs * j:gs * (j + 1), :] = shifted.astype(BF16)

    y = _dot(x, t_ref[...])

    tau_rev = (chunk - 1) - lax.broadcasted_iota(I32, (chunk, LANES), 0)
    r_re, r_im = powers(tau_rev.astype(F32))
    bst = r_re[:, None, :] * bri[None] + r_im[:, None, :] * bri_sw[None]
    bst = bst.reshape(width, LANES).astype(BF16)
    s = _dot(x, bst)

    rows = s.shape[0]
    n_idx = lax.broadcasted_iota(I32, (rows, LANES), 0) % n_chunks
    lane = lax.broadcasted_iota(I32, (1, LANES), 1)
    half = LANES // 2

    def cmul(v, p_re, p_im):
        return v * p_re + pltpu.roll(v, half, axis=1) * jnp.where(lane < half, -p_im, p_im)

    sh = 1
    while sh < n_chunks:
        p_re, p_im = powers(jnp.full((1, LANES), float(chunk * sh), F32))
        prev = jnp.where(n_idx >= sh, pltpu.roll(s, sh, axis=0), 0.0)
        s = s + cmul(prev, p_re, p_im)
        sh *= 2
    s_in = jnp.where(n_idx >= 1, pltpu.roll(s, 1, axis=0), 0.0)
    return y + _dot_t(s_in.astype(BF16), cpow[gs:gs + width].astype(BF16))


def _s5_kernel(u_ref, sel_ref, zr_ref, zi_ref, lr_ref, li_ref, ca_ref, cb_ref, ba_ref, bb_ref,
               y_ref, x_scr, y_scr, t_ref, *, chunk, n_chunks, rows):
    gpt = LANES // SSM_GROUP_SIZE

    def plane(j):
        return pl.ds(j, rows, stride=chunk)

    for jt in range(chunk // gpt):
        planes = jnp.concatenate([u_ref[plane(jt * gpt + jj), :] for jj in range(gpt)], axis=1)
        grouped = _dot(planes.astype(BF16), sel_ref[...])
        for g in range(gpt):
            x_scr[g, :, jt * LANES:(jt + 1) * LANES] = grouped[:, g * LANES:(g + 1) * LANES].astype(BF16)

    def per_group(g, _):
        y = _s5_group(x_scr[g], zr_ref[g], zi_ref[g], lr_ref[g], li_ref[g], ca_ref[g], cb_ref[g], ba_ref[g],
                      bb_ref[g], t_ref, chunk=chunk, n_chunks=n_chunks)
        y_scr[g] = y.astype(BF16)
        return 0

    lax.fori_loop(0, gpt, per_group, 0)

    for jt in range(chunk // gpt):
        grouped = jnp.concatenate([y_scr[g, :, jt * LANES:(jt + 1) * LANES] for g in range(gpt)], axis=1)
        planes = _dot(grouped, sel_ref[...])
        for jj in range(gpt):
            y_ref[plane(jt * gpt + jj), :] = planes[:, jj * LANES:(jj + 1) * LANES]


def _s5(u, lam_re, lam_im, log_dt, b_re, b_im, c_re, c_im, bsz, seq):
    g_n, gs, p = SSM_GROUPS, SSM_GROUP_SIZE, SSM_STATE
    chunk = SSM_CHUNK
    n_chunks = seq // chunk
    rows = bsz * n_chunks
    width = chunk * gs
    gpt = LANES // gs
    assert chunk % gpt == 0 and g_n % gpt == 0 and 2 * p == LANES
    dt = jnp.exp(log_dt.astype(F32))[:, None]
    dup = lambda a: jnp.concatenate([a, a], axis=-1).reshape(g_n, 1, 2 * p)
    zr, zi = dup(lam_re * dt), dup(lam_im * dt)
    lr, li = dup(lam_re), dup(lam_im)
    bt_re, bt_im = b_re.transpose(0, 2, 1), b_im.transpose(0, 2, 1)
    ba = jnp.concatenate([bt_re, bt_im], axis=-1)
    bb = jnp.concatenate([-bt_im, bt_re], axis=-1)
    ca = jnp.concatenate([c_re, -c_im], axis=-1)
    cb = jnp.concatenate([-c_im, -c_re], axis=-1)
    idx = jnp.arange(gpt * LANES, dtype=I32)
    swapped = ((idx // gs) % gpt) * LANES + (idx // LANES) * gs + idx % gs
    sel = (swapped[:, None] == idx[None, :]).astype(BF16)
    tile = pl.BlockSpec((bsz * seq, LANES), lambda t: (0, t))
    vec = pl.BlockSpec((gpt, 1, 2 * p), lambda t: (t, 0, 0))
    mat = pl.BlockSpec((gpt, gs, 2 * p), lambda t: (t, 0, 0))
    return pl.pallas_call(
        functools.partial(_s5_kernel, chunk=chunk, n_chunks=n_chunks, rows=rows),
        grid=(g_n // gpt,),
        in_specs=[tile, pl.BlockSpec((gpt * LANES, gpt * LANES), lambda t: (0, 0)),
                  vec, vec, vec, vec, mat, mat, mat, mat],
        out_specs=tile,
        out_shape=jax.ShapeDtypeStruct((bsz * seq, g_n * gs), F32),
        scratch_shapes=[pltpu.VMEM((gpt, rows, width), BF16), pltpu.VMEM((gpt, rows, width), BF16),
                        pltpu.VMEM((width, width), BF16)],
        compiler_params=_cparams(("arbitrary",)),
        name="s5",
    )(u, sel, zr, zi, lr, li, ca, cb, ba, bb)


def _moba_kernel(qt_ref, k_ref, vt_ref, km_ref, o_ref, s_a, s_b, s_fin, p_a, p_b, *, nb, nbp):
    blk = MOBA_BLOCK
    group = 2
    keys = nq = group * blk
    i = pl.program_id(2)
    qt = qt_ref[...]
    km = km_ref[...]
    km_hi = km.astype(BF16)
    km_lo = (km - km_hi.astype(F32)).astype(BF16)
    dim_i = lax.broadcasted_iota(I32, (LANES, nq), 0)
    blk_i = lax.broadcasted_iota(I32, (nbp, nq), 0)
    blk_f = blk_i.astype(F32)
    own = group * i + lax.broadcasted_iota(I32, (nbp, nq), 1) // blk
    valid = blk_i < own

    q_loop, q_fin = [], []
    for h in range(2):
        qh = jnp.where((dim_i >= h * HEAD_DIM) & (dim_i < (h + 1) * HEAD_DIM), qt, jnp.zeros_like(qt))
        g = jnp.where(valid, _dot(km_hi, qh) + _dot(km_lo, qh), NEG_INF)
        sel = jnp.zeros((nbp, nq), jnp.bool_)
        for _ in range(MOBA_TOPK):
            m = jnp.max(g, axis=0, keepdims=True)
            idx = jnp.min(jnp.where(g == m, blk_f, float(nbp)), axis=0, keepdims=True)
            pick = blk_f == idx
            sel = sel | pick
            g = jnp.where(pick, -jnp.inf, g)
        allowed = sel & valid
        for lst, ok in ((q_loop, allowed & (blk_i < group * i)), (q_fin, allowed | (blk_i == own))):
            parts = [qh, jnp.where(ok, 0.0, NEG_INF).astype(BF16)]
            if nbp < LANES:
                parts.append(jnp.full((LANES - nbp, nq), NEG_INF, BF16))
            lst.append(jnp.concatenate(parts, axis=0))

    def scores(grp, s_ref, q_aug, causal=False):
        kb0 = grp * group
        k_cat = k_ref[pl.ds(jnp.minimum(kb0, nb - group), group)].reshape(keys, LANES)
        blk_id = kb0 + lax.broadcasted_iota(I32, (keys, LANES), 0) // blk
        onehot = jnp.where(lax.broadcasted_iota(I32, (keys, LANES), 1) == blk_id, 1.0, 0.0).astype(BF16)
        k_aug = jnp.concatenate([k_cat, onehot], axis=1)
        mx = []
        for h in range(2):
            s = _dot(k_aug, q_aug[h])
            if causal:
                key_i = lax.broadcasted_iota(I32, (keys, nq), 0)
                s = jnp.where(key_i <= lax.broadcasted_iota(I32, (keys, nq), 1), s, NEG_INF)
            s_ref[h] = s
            mx.append(jnp.max(s, axis=0, keepdims=True))
        return tuple(mx)

    def softmax(s_ref, mx, p_ref, st):
        out = []
        for h in range(2):
            m_i, _, l_i, acc = st[h]
            m_new = jnp.maximum(m_i, mx[h])
            p_ref[h] = jnp.exp2(s_ref[h] - m_new).astype(BF16)
            out.append((m_new, jnp.exp2(m_i - m_new), l_i, acc))
        return tuple(out)

    def values(grp, p_ref, st):
        out = []
        for h in range(2):
            m_i, alpha, l_i, acc = st[h]
            blocks = [vt_ref[jnp.clip(grp * group + j, 0, nb - 1), h * HEAD_DIM:(h + 1) * HEAD_DIM, :]
                      for j in range(group)]
            ones = jnp.ones((16, keys), BF16)
            pv = _dot(jnp.concatenate([jnp.concatenate(blocks, axis=1), ones], axis=0), p_ref[h])
            out.append((m_i, alpha, alpha * l_i + pv[HEAD_DIM:HEAD_DIM + 1], alpha * acc + pv[:HEAD_DIM]))
        return tuple(out)

    mx_fin = scores(i, s_fin, q_fin, causal=True)
    mx_first = scores(0, s_a, q_loop)
    p_b[...] = jnp.zeros_like(p_b)

    def body(t, carry):
        st, mx_a = carry
        mx_b = scores(2 * t + 1, s_b, q_loop)
        st = values(2 * t - 1, p_b, st)
        st = softmax(s_a, mx_a, p_a, st)
        mx_a = scores(2 * t + 2, s_a, q_loop)
        st = values(2 * t, p_a, st)
        return softmax(s_b, mx_b, p_b, st), mx_a

    init = (jnp.full((1, nq), -jnp.inf, F32), jnp.ones((1, nq), F32), jnp.zeros((1, nq), F32),
            jnp.zeros((HEAD_DIM, nq), F32))
    trips = (i + 1) // 2
    st, _ = lax.fori_loop(0, trips, body, ((init, init), mx_first))
    st = values(2 * trips - 1, p_b, st)
    st = values(i, p_a, softmax(s_fin, mx_fin, p_a, st))
    both = jnp.concatenate([st[h][3] / st[h][2] for h in range(2)], axis=0)
    o_ref[...] = both.T.astype(BF16)


def _moba(qt, k, vt4, km, bsz, seq):
    n_tok, attn_w = k.shape
    blk = MOBA_BLOCK
    nb = seq // blk
    group = 2
    nbp = -(-nb // 16) * 16
    assert nb % group == 0 and nb + 2 * group <= LANES
    hp = attn_w // LANES
    km_pad = jnp.zeros((bsz, nbp, attn_w), F32).at[:, :nb].set(km.reshape(bsz, nb, attn_w))
    k4 = k.reshape(bsz, nb, blk, attn_w)
    steps = nb // group
    return pl.pallas_call(
        functools.partial(_moba_kernel, nb=nb, nbp=nbp),
        grid=(bsz, hp, nb // group),
        in_specs=[pl.BlockSpec((None, LANES, group * blk), lambda b, p, i: (b, p, i)),
                  pl.BlockSpec((None, nb, blk, LANES), lambda b, p, i: (b, 0, 0, p)),
                  pl.BlockSpec((None, nb, LANES, blk), lambda b, p, i: (b, 0, p, 0)),
                  pl.BlockSpec((None, nbp, LANES), lambda b, p, i: (b, 0, p))],
        out_specs=pl.BlockSpec((group * blk, LANES), lambda b, p, i: (b * steps + i, p)),
        out_shape=jax.ShapeDtypeStruct((n_tok, attn_w), BF16),
        scratch_shapes=([pltpu.VMEM((2, group * blk, group * blk), F32)] * 3
                        + [pltpu.VMEM((2, group * blk, group * blk), BF16)] * 2),
        compiler_params=_cparams(("arbitrary", "arbitrary", "arbitrary")),
        name="moba",
    )(qt, k4, vt4, km_pad)


def _rms(t, g):
    return t * lax.rsqrt(jnp.mean(t * t, axis=-1, keepdims=True) + NORM_EPS) * g


def _store_slab(ref, val, base=0):
    rows, d = val.shape
    sub = d // LANES
    for s in range(sub):
        ref[pl.ds(base + s, rows, stride=sub), :] = val[:, s * LANES:(s + 1) * LANES]


def _load_slab(ref, rows, sub, base=0):
    return jnp.concatenate([ref[pl.ds(base + s, rows, stride=sub), :] for s in range(sub)], axis=1)


def _merge_kernel(ys_ref, u_ref, ya_ref, gs_ref, ga_ref, x_ref, d_ref, wglu_ref, wsb_ref, wab_ref,
                  wout_ref, postg_ref, gtm_ref, preg_ref, scf_ref, shf_ref, rw_ref, rb_ref,
                  x1_ref, h2_ref, eid_ref, tw_ref, rank_ref, before_ref, cnt_ref, run_ref):
    @pl.when(pl.program_id(0) == 0)
    def _():
        run_ref[...] = jnp.zeros_like(run_ref)

    y = ys_ref[...] + d_ref[...] * u_ref[...].astype(F32)
    y = jax.nn.gelu(y)
    y = y * jax.nn.sigmoid(_dot(y.astype(BF16), wglu_ref[...]))
    bs = _dot(y.astype(BF16), wsb_ref[...])
    ba = _dot(ya_ref[...], wab_ref[...])
    merged = gs_ref[...].astype(F32) * bs + ga_ref[...].astype(F32) * ba
    mix = _dot(merged.astype(BF16), wout_ref[...])
    x1 = x_ref[...] + gtm_ref[0] * _rms(mix, postg_ref[...])
    x1_ref[...] = x1
    h2 = _rms(x1, preg_ref[...]) * (1.0 + scf_ref[0]) + shf_ref[0]
    _store_slab(h2_ref, h2)

    h_hi = h2.astype(BF16)
    h_lo = (h2 - h_hi.astype(F32)).astype(BF16)
    logits = _dot(jnp.concatenate([h_hi, h_hi, h_lo], axis=1), rw_ref[...]) + rb_ref[...]
    tm = logits.shape[0]
    lane = lax.broadcasted_iota(I32, (tm, LANES), 1)
    lane_f = lane.astype(F32)
    g = logits
    vals, picks, idxs = [], [], []
    for _ in range(TOP_K):
        m = jnp.max(g, axis=1, keepdims=True)
        idx = jnp.min(jnp.where(g == m, lane_f, float(LANES)), axis=1, keepdims=True)
        pick = lane_f == idx
        vals.append(m)
        idxs.append(idx.astype(I32))
        picks.append(pick)
        g = jnp.where(pick, -jnp.inf, g)
    exps = [jnp.exp(v - vals[0]) for v in vals]
    tot = exps[0] + exps[1] + exps[2] + exps[3]

    onehot = jnp.where(picks[0] | picks[1] | picks[2] | picks[3], 1.0, 0.0)
    r_i = lax.broadcasted_iota(I32, (tm, tm), 0)
    c_i = lax.broadcasted_iota(I32, (tm, tm), 1)
    tri = jnp.where(c_i < r_i, 1.0, 0.0).astype(BF16)
    rank_full = _dot(tri, onehot.astype(BF16))
    before_ref[0] = run_ref[...]
    run_ref[...] = run_ref[...] + jnp.sum(onehot, axis=0, keepdims=True)
    cnt_ref[...] = run_ref[...]

    eid = jnp.zeros((tm, LANES), I32)
    tw = jnp.zeros((tm, LANES), F32)
    rk = jnp.zeros((tm, LANES), F32)
    for r in range(TOP_K):
        eid = jnp.where(lane == r, idxs[r], eid)
        tw = jnp.where(lane == r, exps[r] / tot, tw)
        pos = jnp.sum(jnp.where(picks[r], rank_full, 0.0), axis=1, keepdims=True)
        rk = jnp.where(lane == r, pos, rk)
    eid_ref[...] = eid
    tw_ref[...] = tw
    rank_ref[...] = rk.astype(I32)


def _merge(ys, u, ya, gs, ga, x2, ssm_d, w_glu, w_sb, w_ab, w_out, post_g, gt_m, pre_g, sc_f, sh_f,
           router_w, router_b, seq, tm):
    n_tok, d = x2.shape
    sw = ys.shape[1]
    aw = ya.shape[1]
    ne = router_w.shape[1]
    per_b = seq // tm
    rw = jnp.zeros((d, LANES), F32).at[:, :ne].set(router_w)
    rw_hi = rw.astype(BF16)
    rw_lo = (rw - rw_hi.astype(F32)).astype(BF16)
    rw = jnp.concatenate([rw_hi, rw_lo, rw_hi], axis=0)
    rb = jnp.full((1, LANES), NEG_INF, F32).at[0, :ne].set(router_b)
    tok = lambda w: pl.BlockSpec((tm, w), lambda i: (i, 0))
    full = lambda a, b: pl.BlockSpec((a, b), lambda i: (0, 0))
    bvec = pl.BlockSpec((1, 1, d), lambda i: (i // per_b, 0, 0))
    return pl.pallas_call(
        _merge_kernel,
        grid=(n_tok // tm,),
        in_specs=[tok(sw), tok(sw), tok(aw), tok(d), tok(d), tok(d),
                  full(1, sw), full(sw, sw), full(sw, d), full(aw, d), full(d, d),
                  full(1, d), bvec, full(1, d), bvec, bvec, full(3 * d, LANES), full(1, LANES)],
        out_specs=[tok(d), pl.BlockSpec((tm * (d // LANES), LANES), lambda i: (i, 0)),
                   tok(LANES), tok(LANES), tok(LANES),
                   pl.BlockSpec((1, 1, LANES), lambda i: (i, 0, 0)), full(1, LANES)],
        out_shape=[jax.ShapeDtypeStruct((n_tok, d), F32),
                   jax.ShapeDtypeStruct((n_tok * (d // LANES), LANES), F32),
                   jax.ShapeDtypeStruct((n_tok, LANES), I32),
                   jax.ShapeDtypeStruct((n_tok, LANES), F32),
                   jax.ShapeDtypeStruct((n_tok, LANES), I32),
                   jax.ShapeDtypeStruct((n_tok // tm, 1, LANES), F32),
                   jax.ShapeDtypeStruct((1, LANES), F32)],
        scratch_shapes=[pltpu.VMEM((1, LANES), F32)],
        compiler_params=_cparams(("arbitrary",)),
        name="merge",
    )(ys, u, ya, gs, ga, x2, ssm_d.reshape(1, sw), w_glu.astype(BF16), w_sb.astype(BF16),
      w_ab.astype(BF16), w_out.astype(BF16), post_g.reshape(1, d), gt_m, pre_g.reshape(1, d),
      sc_f, sh_f, rw, rb)


def _chunk_copies(cnt_ref, start_ref, tile, ne, max_rows, make):
    def per_expert(e, off):
        c, start = cnt_ref[tile * ne + e], start_ref[tile * ne + e]
        for b in range(max_rows.bit_length()):
            n = 1 << b

            @pl.when((c >> b) & 1 == 1)
            def _():
                make(off + (c & (n - 1)), start + (c & (n - 1)), n)
        return off + c

    lax.fori_loop(0, ne, per_expert, 0)


def _dispatch_kernel(cnt_ref, start_ref, pad_start_ref, pad_cnt_ref, tail_ref, loct_ref, h_ref, xs_ref,
                     buf, zero_ref, sem, zsem, *, tm, sub, bm, n_rows, ne):
    i = pl.program_id(0)
    last = pl.num_programs(0) - 1
    slot = i % 2
    n_loc = tm * TOP_K

    def row(ref, r, n=1):
        return ref.at[pl.ds(pl.multiple_of(r * sub, sub), n * sub)]

    def wait_slot(s):
        pltpu.make_async_copy(buf.at[s], row(xs_ref, 0, n_loc), sem.at[s]).wait()

    def zero_copies(do):
        def per_expert(e, _):
            start, cnt = pad_start_ref[e], pad_cnt_ref[e]
            for b in range(bm.bit_length() - 1):
                n = 1 << b

                @pl.when((cnt >> b) & 1 == 1)
                def _():
                    do(pltpu.make_async_copy(row(zero_ref, 0, n), row(xs_ref, start + (cnt & (n - 1)), n), zsem))
            return 0

        lax.fori_loop(0, ne, per_expert, 0)

        def per_block(j, _):
            r = tail_ref[0] + j * bm

            @pl.when(r < n_rows)
            def _():
                do(pltpu.make_async_copy(row(zero_ref, 0, bm), row(xs_ref, r, bm), zsem))
            return 0

        lax.fori_loop(0, ne, per_block, 0)

    @pl.when(i == 0)
    def _():
        zero_ref[...] = jnp.zeros_like(zero_ref)
        zero_copies(lambda cp: cp.start())

    @pl.when(i >= 2)
    def _():
        wait_slot(slot)

    loct = loct_ref[0]
    row_i = lax.broadcasted_iota(I32, (n_loc, tm), 0)
    place = row_i == loct[0:1, :]
    for k in range(1, TOP_K):
        place = place | (row_i == loct[k:k + 1, :])
    grouped = _dot(jnp.where(place, 1.0, 0.0).astype(BF16), _load_slab(h_ref, tm, sub).astype(BF16))
    _store_slab(buf.at[slot], grouped)
    _chunk_copies(cnt_ref, start_ref, i, ne, tm,
                  lambda loc, glob, n: pltpu.make_async_copy(row(buf.at[slot], loc, n), row(xs_ref, glob, n),
                                                             sem.at[slot]).start())

    @pl.when(i == last)
    def _():
        wait_slot(slot)

    @pl.when((i == last) & (i >= 1))
    def _():
        wait_slot(1 - slot)

    @pl.when(i == 0)
    def _():
        zero_copies(lambda cp: cp.wait())


def _dispatch(h_slab, loc, cnt_tile, start_tile, pad_start, pad_cnt, tail, n_rows, sub, bm, tm):
    n_tok = loc.shape[0]
    nt = n_tok // tm
    ne = pad_start.shape[0]
    loct = jnp.full((nt, 8, tm), -1, I32).at[:, :TOP_K].set(loc.reshape(nt, tm, TOP_K).transpose(0, 2, 1))
    grid_spec = pltpu.PrefetchScalarGridSpec(
        num_scalar_prefetch=5,
        grid=(nt,),
        in_specs=[pl.BlockSpec((1, 8, tm), lambda i, *_: (i, 0, 0)),
                  pl.BlockSpec((tm * sub, LANES), lambda i, *_: (i, 0))],
        out_specs=pl.BlockSpec(memory_space=pl.ANY),
        scratch_shapes=[pltpu.VMEM((2, TOP_K * tm * sub, LANES), F32), pltpu.VMEM((bm * sub, LANES), F32),
                        pltpu.SemaphoreType.DMA((2,)), pltpu.SemaphoreType.DMA(())],
    )
    return pl.pallas_call(
        functools.partial(_dispatch_kernel, tm=tm, sub=sub, bm=bm, n_rows=n_rows, ne=ne),
        grid_spec=grid_spec,
        out_shape=jax.ShapeDtypeStruct((n_rows * sub, LANES), F32),
        compiler_params=_cparams(("arbitrary",)),
        name="dispatch",
    )(cnt_tile.reshape(-1), start_tile.reshape(-1), pad_start, pad_cnt, tail, loct, h_slab)


def _expert_kernel(be_ref, nv_ref, x_ref, wg_ref, bg_ref, wu_ref, bu_ref, wd_ref, bd_ref,
                   o_ref, wg_s, wu_s, wd_s, *, bm, sub):
    i = pl.program_id(0)
    prev = be_ref[jnp.maximum(i - 1, 0)]
    changed = (i == 0) | (be_ref[i] != prev)

    @pl.when(changed)
    def _():
        wg_s[...] = wg_ref[0].astype(BF16)
        wu_s[...] = wu_ref[0].astype(BF16)
        wd_s[...] = wd_ref[0].astype(BF16)

    @pl.when(i < nv_ref[0])
    def _():
        xb = _load_slab(x_ref, bm, sub).astype(BF16)
        g = _dot(xb, wg_s[...]) + bg_ref[0]
        u = _dot(xb, wu_s[...]) + bu_ref[0]
        g = jnp.minimum(g, SWIGLU_LIMIT)
        u = jnp.clip(u, -SWIGLU_LIMIT, SWIGLU_LIMIT)
        act = g * jax.nn.sigmoid(SWIGLU_ALPHA * g) * (u + 1.0)
        _store_slab(o_ref, _dot(act.astype(BF16), wd_s[...]) + bd_ref[0])

    @pl.when(i >= nv_ref[0])
    def _():
        o_ref[...] = jnp.zeros_like(o_ref)


def _experts(xs, blk_exp, n_valid, w_gate, b_gate, w_up, b_up, w_down, b_down, bm):
    ne, d, f = w_gate.shape
    sub = d // LANES
    n_blocks = xs.shape[0] // (bm * sub)
    wspec = lambda a, b: pl.BlockSpec((1, a, b), lambda i, be, nv: (be[i], 0, 0))
    grid_spec = pltpu.PrefetchScalarGridSpec(
        num_scalar_prefetch=2,
        grid=(n_blocks,),
        in_specs=[pl.BlockSpec((bm * sub, LANES), lambda i, be, nv: (jnp.clip(nv[0] - 1, 0, i), 0)),
                  wspec(d, f), wspec(1, f), wspec(d, f), wspec(1, f), wspec(f, d), wspec(1, d)],
        out_specs=pl.BlockSpec((bm * sub, LANES), lambda i, be, nv: (i, 0)),
        scratch_shapes=[pltpu.VMEM((d, f), BF16), pltpu.VMEM((d, f), BF16), pltpu.VMEM((f, d), BF16)],
    )
    return pl.pallas_call(
        functools.partial(_expert_kernel, bm=bm, sub=sub),
        grid_spec=grid_spec,
        out_shape=jax.ShapeDtypeStruct(xs.shape, F32),
        compiler_params=_cparams(("arbitrary",)),
        name="experts",
    )(blk_exp, n_valid, xs, w_gate, b_gate.reshape(ne, 1, f), w_up, b_up.reshape(ne, 1, f),
      w_down, b_down.reshape(ne, 1, d))


def _combine_kernel(cnt_ref, start_ref, yb_ref, loc_ref, tw_ref, x1_ref, g_ref, gt_ref, o_ref, buf, sem,
                    *, tm, sub, ne):
    i = pl.program_id(0)
    n_loc = tm * TOP_K

    def start_tile(tile, slot):
        def make(loc, glob, n):
            src = yb_ref.at[pl.ds(pl.multiple_of(glob * sub, sub), n * sub)]
            dst = buf.at[slot, pl.ds(pl.multiple_of(loc * sub, sub), n * sub)]
            pltpu.make_async_copy(src, dst, sem.at[slot]).start()

        _chunk_copies(cnt_ref, start_ref, tile, ne, tm, make)

    slot = i % 2

    @pl.when(i == 0)
    def _():
        start_tile(0, 0)

    @pl.when(i + 1 < pl.num_programs(0))
    def _():
        start_tile(i + 1, 1 - slot)

    pltpu.make_async_copy(yb_ref.at[pl.ds(0, n_loc * sub)], buf.at[slot], sem.at[slot]).wait()
    rows = _load_slab(buf.at[slot], n_loc, sub).astype(BF16)
    loc, tw = loc_ref[...], tw_ref[...]
    col_i = lax.broadcasted_iota(I32, (tm, n_loc), 1)
    w = jnp.zeros((tm, n_loc), F32)
    for k in range(TOP_K):
        w = jnp.where(col_i == loc[:, k:k + 1], tw[:, k:k + 1], w)
    ff = _dot(w.astype(BF16), rows)
    o_ref[...] = x1_ref[...] + gt_ref[0] * _rms(ff, g_ref[...])


def _combine(yb, loc, tw, cnt_tile, start_tile, x1, post_g, gt_f, seq, tm):
    n_tok, d = x1.shape
    sub = d // LANES
    nt = n_tok // tm
    ne = cnt_tile.shape[1]
    per_b = seq // tm
    loc_l = jnp.full((n_tok, LANES), -1, I32).at[:, :TOP_K].set(loc)
    grid_spec = pltpu.PrefetchScalarGridSpec(
        num_scalar_prefetch=2,
        grid=(nt,),
        in_specs=[pl.BlockSpec(memory_space=pl.ANY),
                  pl.BlockSpec((tm, LANES), lambda i, *_: (i, 0)),
                  pl.BlockSpec((tm, LANES), lambda i, *_: (i, 0)),
                  pl.BlockSpec((tm, d), lambda i, *_: (i, 0)),
                  pl.BlockSpec((1, d), lambda i, *_: (0, 0)),
                  pl.BlockSpec((1, 1, d), lambda i, *_: (i // per_b, 0, 0))],
        out_specs=pl.BlockSpec((tm, d), lambda i, *_: (i, 0)),
        scratch_shapes=[pltpu.VMEM((2, TOP_K * tm * sub, LANES), F32), pltpu.SemaphoreType.DMA((2,))],
    )
    return pl.pallas_call(
        functools.partial(_combine_kernel, tm=tm, sub=sub, ne=ne),
        grid_spec=grid_spec,
        out_shape=jax.ShapeDtypeStruct((n_tok, d), F32),
        compiler_params=_cparams(("arbitrary",)),
        name="combine",
    )(cnt_tile.reshape(-1), start_tile.reshape(-1), yb, loc_l, tw, x1, post_g.reshape(1, d), gt_f)


def kernel(x, c, positions, ada_w, ada_b, mix_pre_g, mix_post_g, ffn_pre_g, ffn_post_g, w_in, ssm_lam_re, ssm_lam_im, ssm_log_dt, ssm_b_re, ssm_b_im, ssm_c_re, ssm_c_im, ssm_d, ssm_w_glu, w_ssm_branch, w_attn_branch, w_out, router_w, router_b, w_gate, b_gate, w_up, b_up, w_down, b_down):
    bsz, seq, d = x.shape
    depth = ada_w.shape[0]
    n_tok = bsz * seq
    bm = EXPERT_ROWS
    sub = d // LANES
    xcur = x.reshape(n_tok, d)
    pos = positions.reshape(n_tok, 1).astype(I32)
    for l in range(depth):
        ada = _ada(c, ada_w[l], ada_b[l])
        sh_m, sc_m, gt_m, sh_f, sc_f, gt_f = [a.reshape(bsz, 1, d) for a in jnp.split(ada, 6, axis=-1)]

        u, q, k, v, gs, ga, km = _inproj(xcur, sc_m, sh_m, mix_pre_g[l].reshape(1, d), pos, w_in[l],
                                         seq, tm=min(512, seq))
        ys = _s5(u, ssm_lam_re[l], ssm_lam_im[l], ssm_log_dt[l], ssm_b_re[l], ssm_b_im[l],
                 ssm_c_re[l], ssm_c_im[l], bsz, seq)
        ya = _moba(q, k, v, km, bsz, seq)
        tm = ROUTE_TILE
        x1, h2, eid, tw, rank, before, cnt = _merge(
            ys, u, ya, gs, ga, xcur, ssm_d[l], ssm_w_glu[l], w_ssm_branch[l], w_attn_branch[l], w_out[l],
            mix_post_g[l], gt_m, ffn_pre_g[l], sc_f, sh_f, router_w[l], router_b[l], seq, tm=tm)

        ne = router_w.shape[-1]
        nt = n_tok // tm
        experts = jnp.arange(ne, dtype=I32)
        counts = cnt[0, :ne].astype(I32)
        padded = (counts + bm - 1) // bm * bm
        p_ends = jnp.cumsum(padded)
        p_starts = p_ends - padded
        before = before.reshape(nt, LANES)[:, :ne].astype(I32)
        cnt_tile = jnp.concatenate([before[1:], counts[None]], axis=0) - before
        start_tile = p_starts[None, :] + before
        off_tile = jnp.cumsum(cnt_tile, axis=1) - cnt_tile
        onehot = eid[:, :TOP_K].reshape(nt, tm, TOP_K, 1) == experts
        loc = (rank[:, :TOP_K] + jnp.sum(jnp.where(onehot, off_tile[:, None, None, :], 0), axis=-1).reshape(n_tok, TOP_K))
        n_blocks = (n_tok * TOP_K) // bm + ne
        n_rows = n_blocks * bm
        blk_start = jnp.arange(n_blocks, dtype=I32)[:, None] * bm
        blk_exp = jnp.minimum(jnp.sum((blk_start >= p_ends[None, :]).astype(I32), axis=1), ne - 1)
        n_valid = (p_ends[-1:] // bm).astype(I32)

        xs = _dispatch(h2, loc, cnt_tile, start_tile, p_starts + counts, padded - counts, p_ends[-1:],
                       n_rows, sub, bm, tm)
        yb = _experts(xs, blk_exp, n_valid, w_gate[l], b_gate[l], w_up[l], b_up[l], w_down[l], b_down[l], bm)
        xcur = _combine(yb, loc, tw, cnt_tile, start_tile, x1, ffn_post_g[l], gt_f, seq, tm)
    return xcur.reshape(bsz, seq, d).astype(x.dtype)
```

```python
import functools
import math

import jax
import jax.numpy as jnp
from jax import lax
from jax.experimental import pallas as pl
from jax.experimental.pallas import tpu as pltpu

F32 = jnp.float32
BF16 = jnp.bfloat16
I32 = jnp.int32

N_HEADS = 8
HEAD_DIM = 64
ROPE_THETA = 10000.0
MOBA_BLOCK = 256
MOBA_TOPK = 3
SSM_GROUP_SIZE = 16
SSM_GROUPS = 32
SSM_STATE = 64
N_EXPERTS = 32
TOP_K = 4
SWIGLU_ALPHA = 1.702
SWIGLU_LIMIT = 7.0
NORM_EPS = 1e-6
NEG_INF = -1e30

LANES = 128
SSM_CHUNK = 64
EXPERT_ROWS = 512
ROUTE_TILE = 512
VMEM_LIMIT = 56 * 1024 * 1024
HIGHEST = lax.Precision.HIGHEST
Q_SCALE = HEAD_DIM ** -0.5 * math.log2(math.e)


def _cparams(sem):
    return pltpu.CompilerParams(dimension_semantics=sem, vmem_limit_bytes=VMEM_LIMIT)


def _dot(a, b, **kw):
    return jnp.dot(a, b, preferred_element_type=F32, **kw)


def _dot_t(a, b, **kw):
    return lax.dot_general(a, b, (((1,), (1,)), ((), ())), preferred_element_type=F32, **kw)


def _ada_kernel(c_ref, w_ref, b_ref, o_ref):
    c = c_ref[...]
    cond = c * jax.nn.sigmoid(c)
    o_ref[...] = _dot(cond, w_ref[...], precision=HIGHEST) + b_ref[...]


def _ada(c, ada_w, ada_b):
    bsz, d = c.shape
    n = ada_w.shape[1]
    c8 = jnp.zeros((8, d), F32).at[:bsz].set(c)
    out = pl.pallas_call(
        _ada_kernel,
        grid=(n // d,),
        in_specs=[pl.BlockSpec((8, d), lambda j: (0, 0)),
                  pl.BlockSpec((d, d), lambda j: (0, j)),
                  pl.BlockSpec((1, d), lambda j: (0, j))],
        out_specs=pl.BlockSpec((8, d), lambda j: (0, j)),
        out_shape=jax.ShapeDtypeStruct((8, n), F32),
        compiler_params=_cparams(("arbitrary",)),
        name="ada",
    )(c8, ada_w, ada_b.reshape(1, n))
    return out[:bsz]


def _inproj_kernel(x_ref, sc_ref, sh_ref, g_ref, pos_ref, invf_ref, post_ref, invft_ref, w_ref, wt_ref,
                   u_ref, qt_ref, k_ref, vt_ref, gs_ref, ga_ref, km_ref, *, ssm_w, attn_w, d_model):
    x = x_ref[...]
    ms = jnp.mean(x * x, axis=-1, keepdims=True)
    xn = x * lax.rsqrt(ms + NORM_EPS) * g_ref[...]
    h = (xn * (1.0 + sc_ref[0]) + sh_ref[0]).astype(BF16)

    def proj(lo, width):
        return _dot(h, w_ref[:, lo:lo + width])

    u_ref[...] = proj(0, ssm_w)

    ang = pos_ref[...].astype(F32) * invf_ref[...]
    reps = attn_w // LANES
    cos = jnp.concatenate([jnp.cos(ang)] * reps, axis=1)
    sin = jnp.concatenate([jnp.sin(ang)] * reps, axis=1)
    lane = lax.broadcasted_iota(I32, (1, attn_w), 1)
    first = (lane % HEAD_DIM) < (HEAD_DIM // 2)
    sin = jnp.where(first, -sin, sin)

    def rope(t):
        rot = jnp.where(first, pltpu.roll(t, attn_w - HEAD_DIM // 2, axis=1),
                        pltpu.roll(t, HEAD_DIM // 2, axis=1))
        return t * cos + rot * sin

    k = rope(proj(ssm_w + attn_w, attn_w))
    k_ref[...] = k.astype(BF16)
    tm = k.shape[0]
    nblk = tm // MOBA_BLOCK
    km_ref[0] = jnp.mean(k.reshape(nblk, MOBA_BLOCK, attn_w), axis=1)

    half = HEAD_DIM // 2
    ang_t = invft_ref[...] * post_ref[0].astype(F32)
    cos_t, sin_t = jnp.cos(ang_t), jnp.sin(ang_t)
    qt = _dot_t(wt_ref[0:attn_w, :], h).reshape(N_HEADS, 2, half, tm)
    t1, t2 = qt[:, 0], qt[:, 1]
    qt = jnp.stack([t1 * cos_t - t2 * sin_t, t2 * cos_t + t1 * sin_t], axis=1).reshape(attn_w, tm)
    qt_ref[...] = (qt * Q_SCALE).astype(BF16)
    vt = _dot_t(wt_ref[attn_w:2 * attn_w, :], h).astype(BF16)
    for j in range(nblk):
        vt_ref[j] = vt[:, j * MOBA_BLOCK:(j + 1) * MOBA_BLOCK]
    gs_ref[...] = jax.nn.sigmoid(proj(ssm_w + 3 * attn_w, d_model)).astype(BF16)
    ga_ref[...] = jax.nn.sigmoid(proj(ssm_w + 3 * attn_w + d_model, d_model)).astype(BF16)


def _inproj(x2, sc, sh, g, pos, w_in, seq, tm):
    n_tok, d = x2.shape
    attn_w = N_HEADS * HEAD_DIM
    ssm_w = SSM_GROUPS * SSM_GROUP_SIZE
    in_w = w_in.shape[1]
    half = HEAD_DIM // 2
    inv_freq = ROPE_THETA ** (-jnp.arange(half, dtype=F32) / half)
    invf = jnp.tile(inv_freq, LANES // half).reshape(1, LANES)
    nt = n_tok // tm
    per_b = seq // tm
    nblk = tm // MOBA_BLOCK
    bsz = n_tok // seq
    blk = MOBA_BLOCK
    w_bf = w_in.astype(BF16)
    q_lo, v_lo = ssm_w, ssm_w + 2 * attn_w
    wt = jnp.concatenate([w_in[:, q_lo:q_lo + attn_w].T, w_in[:, v_lo:v_lo + attn_w].T], axis=0).astype(BF16)
    tok = lambda w: pl.BlockSpec((tm, w), lambda i: (i, 0))
    full = lambda a, b: pl.BlockSpec((a, b), lambda i: (0, 0))
    bvec = pl.BlockSpec((1, 1, d), lambda i: (i // per_b, 0, 0))
    outs = pl.pallas_call(
        functools.partial(_inproj_kernel, ssm_w=ssm_w, attn_w=attn_w, d_model=d),
        grid=(nt,),
        in_specs=[tok(d), bvec, bvec, full(1, d), tok(1), full(1, LANES),
                  pl.BlockSpec((1, 1, tm), lambda i: (i, 0, 0)), full(half, 1),
                  full(d, in_w), full(2 * attn_w, d)],
        out_specs=[tok(ssm_w),
                   pl.BlockSpec((None, attn_w, tm), lambda i: (i // per_b, 0, i % per_b)),
                   tok(attn_w),
                   pl.BlockSpec((None, nblk, attn_w, blk), lambda i: (i // per_b, i % per_b, 0, 0)),
                   tok(d), tok(d),
                   pl.BlockSpec((1, nblk, attn_w), lambda i: (i, 0, 0))],
        out_shape=[jax.ShapeDtypeStruct((n_tok, ssm_w), F32),
                   jax.ShapeDtypeStruct((bsz, attn_w, seq), BF16),
                   jax.ShapeDtypeStruct((n_tok, attn_w), BF16),
                   jax.ShapeDtypeStruct((bsz, seq // blk, attn_w, blk), BF16),
                   jax.ShapeDtypeStruct((n_tok, d), BF16),
                   jax.ShapeDtypeStruct((n_tok, d), BF16),
                   jax.ShapeDtypeStruct((nt, nblk, attn_w), F32)],
        compiler_params=_cparams(("arbitrary",)),
        name="inproj",
    )(x2, sc, sh, g, pos, invf, pos.reshape(nt, 1, tm), inv_freq.reshape(half, 1), w_bf, wt)
    return outs


def _s5_group(x, zr, zi, lr, li, ca, cb, ba, bb, t_ref, *, chunk, n_chunks):
    gs = SSM_GROUP_SIZE

    def powers(tau):
        mag = jnp.exp(tau * zr)
        return mag * jnp.cos(tau * zi), mag * jnp.sin(tau * zi)

    tau = lax.broadcasted_iota(I32, (chunk + 8, LANES), 0).astype(F32)
    e_re, e_im = powers(tau)
    lb_re, lb_im = e_re[1:2], e_im[1:2]
    den = lr * lr + li * li
    a, b = lb_re - 1.0, lb_im
    cf_re = (a * lr + b * li) / den
    cf_im = (b * lr - a * li) / den
    bri = cf_re * ba + cf_im * bb
    bri_sw = cf_re * bb - cf_im * ba

    cpow =(e_re[:chunk + 1, None, :] * ca[None] + e_im[:chunk + 1, None, :] * cb[None])
    cpow = cpow.reshape((chunk + 1) * gs, LANES)
    width = chunk * gs
    r = _dot_t(bri, cpow[:width], precision=HIGHEST)
    col = lax.broadcasted_iota(I32, (gs, width), 1)
    t_ref[0:gs, :] = r.astype(BF16)
    for j in range(1, chunk):
        shifted = jnp.where(col >= gs * j, pltpu.roll(r, gs * j, axis=1), 0.0)
        t_ref[gs * j:gs * (j + 1), :] = shifted.astype(BF16)

    y = _dot(x, t_ref[...])

    tau_rev = (chunk - 1) - lax.broadcasted_iota(I32, (chunk, LANES), 0)
    r_re, r_im = powers(tau_rev.astype(F32))
    bst = r_re[:, None, :] * bri[None] + r_im[:, None, :] * bri_sw[None]
    bst = bst.reshape(width, LANES).astype(BF16)
    s = _dot(x, bst)

    rows = s.shape[0]
    n_idx = lax.broadcasted_iota(I32, (rows, LANES), 0) % n_chunks
    lane = lax.broadcasted_iota(I32, (1, LANES), 1)
    half = LANES // 2

    def cmul(v, p_re, p_im):
        return v * p_re + pltpu.roll(v, half, axis=1) * jnp.where(lane < half, -p_im, p_im)

    sh = 1
    while sh < n_chunks:
        p_re, p_im = powers(jnp.full((1, LANES), float(chunk * sh), F32))
        prev = jnp.where(n_idx >= sh, pltpu.roll(s, sh, axis=0), 0.0)
        s = s + cmul(prev, p_re, p_im)
        sh *= 2
    s_in = jnp.where(n_idx >= 1, pltpu.roll(s, 1, axis=0), 0.0)
    return y + _dot_t(s_in.astype(BF16), cpow[gs:gs + width].astype(BF16))


def _s5_kernel(u_ref, sel_ref, zr_ref, zi_ref, lr_ref, li_ref, ca_ref, cb_ref, ba_ref, bb_ref,
               y_ref, x_scr, y_scr, t_ref, *, chunk, n_chunks, rows):
    gpt = LANES // SSM_GROUP_SIZE

    def plane(j):
        return pl.ds(j, rows, stride=chunk)

    for jt in range(chunk // gpt):
        planes = jnp.concatenate([u_ref[plane(jt * gpt + jj), :] for jj in range(gpt)], axis=1)
        grouped = _dot(planes.astype(BF16), sel_ref[...])
        for g in range(gpt):
            x_scr[g, :, jt * LANES:(jt + 1) * LANES] = grouped[:, g * LANES:(g + 1) * LANES].astype(BF16)

    def per_group(g, _):
        y = _s5_group(x_scr[g], zr_ref[g], zi_ref[g], lr_ref[g], li_ref[g], ca_ref[g], cb_ref[g], ba_ref[g],
                      bb_ref[g], t_ref, chunk=chunk, n_chunks=n_chunks)
        y_scr[g] = y.astype(BF16)
        return 0

    lax.fori_loop(0, gpt, per_group, 0)

    for jt in range(chunk // gpt):
        grouped = jnp.concatenate([y_scr[g, :, jt * LANES:(jt + 1) * LANES] for g in range(gpt)], axis=1)
        planes = _dot(grouped, sel_ref[...])
        for jj in range(gpt):
            y_ref[plane(jt * gpt + jj), :] = planes[:, jj * LANES:(jj + 1) * LANES]


def _s5(u, lam_re, lam_im, log_dt, b_re, b_im, c_re, c_im, bsz, seq):
    g_n, gs, p = SSM_GROUPS, SSM_GROUP_SIZE, SSM_STATE
    chunk = SSM_CHUNK
    n_chunks = seq // chunk
    rows = bsz * n_chunks
    width = chunk * gs
    gpt = LANES // gs
    assert chunk % gpt == 0 and g_n % gpt == 0 and 2 * p == LANES
    dt = jnp.exp(log_dt.astype(F32))[:, None]
    dup = lambda a: jnp.concatenate([a, a], axis=-1).reshape(g_n, 1, 2 * p)
    zr, zi = dup(lam_re * dt), dup(lam_im * dt)
    lr, li = dup(lam_re), dup(lam_im)
    bt_re, bt_im = b_re.transpose(0, 2, 1), b_im.transpose(0, 2, 1)
    ba = jnp.concatenate([bt_re, bt_im], axis=-1)
    bb = jnp.concatenate([-bt_im, bt_re], axis=-1)
    ca = jnp.concatenate([c_re, -c_im], axis=-1)
    cb = jnp.concatenate([-c_im, -c_re], axis=-1)
    idx = jnp.arange(gpt * LANES, dtype=I32)
    swapped = ((idx // gs) % gpt) * LANES + (idx // LANES) * gs + idx % gs
    sel = (swapped[:, None] == idx[None, :]).astype(BF16)
    tile = pl.BlockSpec((bsz * seq, LANES), lambda t: (0, t))
    vec = pl.BlockSpec((gpt, 1, 2 * p), lambda t: (t, 0, 0))
    mat = pl.BlockSpec((gpt, gs, 2 * p), lambda t: (t, 0, 0))
    return pl.pallas_call(
        functools.partial(_s5_kernel, chunk=chunk, n_chunks=n_chunks, rows=rows),
        grid=(g_n // gpt,),
        in_specs=[tile, pl.BlockSpec((gpt * LANES, gpt * LANES), lambda t: (0, 0)),
                  vec, vec, vec, vec, mat, mat, mat, mat],
        out_specs=tile,
        out_shape=jax.ShapeDtypeStruct((bsz * seq, g_n * gs), F32),
        scratch_shapes=[pltpu.VMEM((gpt, rows, width), BF16), pltpu.VMEM((gpt, rows, width), BF16),
                        pltpu.VMEM((width, width), BF16)],
        compiler_params=_cparams(("arbitrary",)),
        name="s5",
    )(u, sel, zr, zi, lr, li, ca, cb, ba, bb)


def _moba_kernel(qt_ref, k_ref, vt_ref, km_ref, o_ref, s_a, s_b, s_fin, p_a, p_b, *, nb, nbp):
    blk = MOBA_BLOCK
    group = 2
    keys = nq = group * blk
    i = pl.program_id(2)
    qt = qt_ref[...]
    km = km_ref[...]
    km_hi = km.astype(BF16)
    km_lo = (km - km_hi.astype(F32)).astype(BF16)
    dim_i = lax.broadcasted_iota(I32, (LANES, nq), 0)
    blk_i = lax.broadcasted_iota(I32, (nbp, nq), 0)
    blk_f = blk_i.astype(F32)
    own = group * i + lax.broadcasted_iota(I32, (nbp, nq), 1) // blk
    valid = blk_i < own

    q_loop, q_fin = [], []
    for h in range(2):
        qh = jnp.where((dim_i >= h * HEAD_DIM) & (dim_i < (h + 1) * HEAD_DIM), qt, jnp.zeros_like(qt))
        g = jnp.where(valid, _dot(km_hi, qh) + _dot(km_lo, qh), NEG_INF)
        sel = jnp.zeros((nbp, nq), jnp.bool_)
        for _ in range(MOBA_TOPK):
            m = jnp.max(g, axis=0, keepdims=True)
            idx = jnp.min(jnp.where(g == m, blk_f, float(nbp)), axis=0, keepdims=True)
            pick = blk_f == idx
            sel = sel | pick
            g = jnp.where(pick, -jnp.inf, g)
        allowed = sel & valid
        for lst, ok in ((q_loop, allowed & (blk_i < group * i)), (q_fin, allowed | (blk_i == own))):
            parts = [qh, jnp.where(ok, 0.0, NEG_INF).astype(BF16)]
            if nbp < LANES:
                parts.append(jnp.full((LANES - nbp, nq), NEG_INF, BF16))
            lst.append(jnp.concatenate(parts, axis=0))

    def scores(grp, s_ref, q_aug, causal=False):
        kb0 = grp * group
        k_cat = k_ref[pl.ds(jnp.minimum(kb0, nb - group), group)].reshape(keys, LANES)
        blk_id = kb0 + lax.broadcasted_iota(I32, (keys, LANES), 0) // blk
        onehot = jnp.where(lax.broadcasted_iota(I32, (keys, LANES), 1) == blk_id, 1.0, 0.0).astype(BF16)
        k_aug = jnp.concatenate([k_cat, onehot], axis=1)
        mx = []
        for h in range(2):
            s = _dot(k_aug, q_aug[h])
            if causal:
                key_i = lax.broadcasted_iota(I32, (keys, nq), 0)
                s = jnp.where(key_i <= lax.broadcasted_iota(I32, (keys, nq), 1), s, NEG_INF)
            s_ref[h] = s
            mx.append(jnp.max(s, axis=0, keepdims=True))
        return tuple(mx)

    def softmax(s_ref, mx, p_ref, st):
        out = []
        for h in range(2):
            m_i, _, l_i, acc = st[h]
            m_new = jnp.maximum(m_i, mx[h])
            p_ref[h] = jnp.exp2(s_ref[h] - m_new).astype(BF16)
            out.append((m_new, jnp.exp2(m_i - m_new), l_i, acc))
        return tuple(out)

    def values(grp, p_ref, st):
        out = []
        for h in range(2):
            m_i, alpha, l_i, acc = st[h]
            blocks = [vt_ref[jnp.clip(grp * group + j, 0, nb - 1), h * HEAD_DIM:(h + 1) * HEAD_DIM, :]
                      for j in range(group)]
            ones = jnp.ones((16, keys), BF16)
            pv = _dot(jnp.concatenate([jnp.concatenate(blocks, axis=1), ones], axis=0), p_ref[h])
            out.append((m_i, alpha, alpha * l_i + pv[HEAD_DIM:HEAD_DIM + 1], alpha * acc + pv[:HEAD_DIM]))
        return tuple(out)

    mx_first = scores(0, s_a, q_loop)
    p_b[...] = jnp.zeros_like(p_b)

    def body(t, carry):
        st, mx_a = carry
        mx_b = scores(2 * t + 1, s_b, q_loop)
        st = values(2 * t - 1, p_b, st)
        st = softmax(s_a, mx_a, p_a, st)
        mx_a = scores(2 * t + 2, s_a, q_loop)
        st = values(2 * t, p_a, st)
        return softmax(s_b, mx_b, p_b, st), mx_a

    init = (jnp.full((1, nq), -jnp.inf, F32), jnp.ones((1, nq), F32), jnp.zeros((1, nq), F32),
            jnp.zeros((HEAD_DIM, nq), F32))
    trips = (i + 1) // 2
    st, _ = lax.fori_loop(0, trips, body, ((init, init), mx_first))
    mx_fin = scores(i, s_fin, q_fin, causal=True)
    st = values(2 * trips - 1, p_b, st)
    st = values(i, p_a, softmax(s_fin, mx_fin, p_a, st))
    both = jnp.concatenate([st[h][3] / st[h][2] for h in range(2)], axis=0)
    o_ref[...] = both.T.astype(BF16)


def _moba(qt, k, vt4, km, bsz, seq):
    n_tok, attn_w = k.shape
    blk = MOBA_BLOCK
    nb = seq // blk
    group = 2
    nbp = -(-nb // 16) * 16
    assert nb % group == 0 and nb + 2 * group <= LANES
    hp = attn_w // LANES
    km_pad = jnp.zeros((bsz, nbp, attn_w), F32).at[:, :nb].set(km.reshape(bsz, nb, attn_w))
    k4 = k.reshape(bsz, nb, blk, attn_w)
    steps = nb // group
    return pl.pallas_call(
        functools.partial(_moba_kernel, nb=nb, nbp=nbp),
        grid=(bsz, hp, nb // group),
        in_specs=[pl.BlockSpec((None, LANES, group * blk), lambda b, p, i: (b, p, i)),
                  pl.BlockSpec((None, nb, blk, LANES), lambda b, p, i: (b, 0, 0, p)),
                  pl.BlockSpec((None, nb, LANES, blk), lambda b, p, i: (b, 0, p, 0)),
                  pl.BlockSpec((None, nbp, LANES), lambda b, p, i: (b, 0, p))],
        out_specs=pl.BlockSpec((group * blk, LANES), lambda b, p, i: (b * steps + i, p)),
        out_shape=jax.ShapeDtypeStruct((n_tok, attn_w), BF16),
        scratch_shapes=([pltpu.VMEM((2, group * blk, group * blk), F32)] * 3
                        + [pltpu.VMEM((2, group * blk, group * blk), BF16)] * 2),
        compiler_params=_cparams(("arbitrary", "arbitrary", "arbitrary")),
        name="moba",
    )(qt, k4, vt4, km_pad)


def _rms(t, g):
    return t * lax.rsqrt(jnp.mean(t * t, axis=-1, keepdims=True) + NORM_EPS) * g


def _store_slab(ref, val, base=0):
    rows, d = val.shape
    sub = d // LANES
    for s in range(sub):
        ref[pl.ds(base + s, rows, stride=sub), :] = val[:, s * LANES:(s + 1) * LANES]


def _load_slab(ref, rows, sub, base=0):
    return jnp.concatenate([ref[pl.ds(base + s, rows, stride=sub), :] for s in range(sub)], axis=1)


def _merge_kernel(ys_ref, u_ref, ya_ref, gs_ref, ga_ref, x_ref, d_ref, wglu_ref, wsb_ref, wab_ref,
                  wout_ref, postg_ref, gtm_ref, preg_ref, scf_ref, shf_ref, rw_ref, rb_ref,
                  x1_ref, h2_ref, eid_ref, tw_ref, rank_ref, before_ref, cnt_ref, run_ref):
    @pl.when(pl.program_id(0) == 0)
    def _():
        run_ref[...] = jnp.zeros_like(run_ref)

    y = ys_ref[...] + d_ref[...] * u_ref[...].astype(F32)
    y = jax.nn.gelu(y)
    y = y * jax.nn.sigmoid(_dot(y.astype(BF16), wglu_ref[...]))
    bs = _dot(y.astype(BF16), wsb_ref[...])
    ba = _dot(ya_ref[...], wab_ref[...])
    merged = gs_ref[...].astype(F32) * bs + ga_ref[...].astype(F32) * ba
    mix = _dot(merged.astype(BF16), wout_ref[...])
    x1 = x_ref[...] + gtm_ref[0] * _rms(mix, postg_ref[...])
    x1_ref[...] = x1
    h2 = _rms(x1, preg_ref[...]) * (1.0 + scf_ref[0]) + shf_ref[0]
    _store_slab(h2_ref, h2)

    h_hi = h2.astype(BF16)
    h_lo = (h2 - h_hi.astype(F32)).astype(BF16)
    logits = _dot(jnp.concatenate([h_hi, h_hi, h_lo], axis=1), rw_ref[...]) + rb_ref[...]
    tm = logits.shape[0]
    lane = lax.broadcasted_iota(I32, (tm, LANES), 1)
    lane_f = lane.astype(F32)
    g = logits
    vals, picks, idxs = [], [], []
    for _ in range(TOP_K):
        m = jnp.max(g, axis=1, keepdims=True)
        idx = jnp.min(jnp.where(g == m, lane_f, float(LANES)), axis=1, keepdims=True)
        pick = lane_f == idx
        vals.append(m)
        idxs.append(idx.astype(I32))
        picks.append(pick)
        g = jnp.where(pick, -jnp.inf, g)
    exps = [jnp.exp(v - vals[0]) for v in vals]
    tot = exps[0] + exps[1] + exps[2] + exps[3]

    onehot = jnp.where(picks[0] | picks[1] | picks[2] | picks[3], 1.0, 0.0)
    r_i = lax.broadcasted_iota(I32, (tm, tm), 0)
    c_i = lax.broadcasted_iota(I32, (tm, tm), 1)
    tri = jnp.where(c_i < r_i, 1.0, 0.0).astype(BF16)
    rank_full = _dot(tri, onehot.astype(BF16))
    before_ref[0] = run_ref[...]
    run_ref[...] = run_ref[...] + jnp.sum(onehot, axis=0, keepdims=True)
    cnt_ref[...] = run_ref[...]

    eid = jnp.zeros((tm, LANES), I32)
    tw = jnp.zeros((tm, LANES), F32)
    rk = jnp.zeros((tm, LANES), F32)
    for r in range(TOP_K):
        eid = jnp.where(lane == r, idxs[r], eid)
        tw = jnp.where(lane == r, exps[r] / tot, tw)
        pos = jnp.sum(jnp.where(picks[r], rank_full, 0.0), axis=1, keepdims=True)
        rk = jnp.where(lane == r, pos, rk)
    eid_ref[...] = eid
    tw_ref[...] = tw
    rank_ref[...] = rk.astype(I32)


def _merge(ys, u, ya, gs, ga, x2, ssm_d, w_glu, w_sb, w_ab, w_out, post_g, gt_m, pre_g, sc_f, sh_f,
           router_w, router_b, seq, tm):
    n_tok, d = x2.shape
    sw = ys.shape[1]
    aw = ya.shape[1]
    ne = router_w.shape[1]
    per_b = seq // tm
    rw = jnp.zeros((d, LANES), F32).at[:, :ne].set(router_w)
    rw_hi = rw.astype(BF16)
    rw_lo = (rw - rw_hi.astype(F32)).astype(BF16)
    rw = jnp.concatenate([rw_hi, rw_lo, rw_hi], axis=0)
    rb = jnp.full((1, LANES), NEG_INF, F32).at[0, :ne].set(router_b)
    tok = lambda w: pl.BlockSpec((tm, w), lambda i: (i, 0))
    full = lambda a, b: pl.BlockSpec((a, b), lambda i: (0, 0))
    bvec = pl.BlockSpec((1, 1, d), lambda i: (i // per_b, 0, 0))
    return pl.pallas_call(
        _merge_kernel,
        grid=(n_tok // tm,),
        in_specs=[tok(sw), tok(sw), tok(aw), tok(d), tok(d), tok(d),
                  full(1, sw), full(sw, sw), full(sw, d), full(aw, d), full(d, d),
                  full(1, d), bvec, full(1, d), bvec, bvec, full(3 * d, LANES), full(1, LANES)],
        out_specs=[tok(d), pl.BlockSpec((tm * (d // LANES), LANES), lambda i: (i, 0)),
                   tok(LANES), tok(LANES), tok(LANES),
                   pl.BlockSpec((1, 1, LANES), lambda i: (i, 0, 0)), full(1, LANES)],
        out_shape=[jax.ShapeDtypeStruct((n_tok, d), F32),
                   jax.ShapeDtypeStruct((n_tok * (d // LANES), LANES), F32),
                   jax.ShapeDtypeStruct((n_tok, LANES), I32),
                   jax.ShapeDtypeStruct((n_tok, LANES), F32),
                   jax.ShapeDtypeStruct((n_tok, LANES), I32),
                   jax.ShapeDtypeStruct((n_tok // tm, 1, LANES), F32),
                   jax.ShapeDtypeStruct((1, LANES), F32)],
        scratch_shapes=[pltpu.VMEM((1, LANES), F32)],
        compiler_params=_cparams(("arbitrary",)),
        name="merge",
    )(ys, u, ya, gs, ga, x2, ssm_d.reshape(1, sw), w_glu.astype(BF16), w_sb.astype(BF16),
      w_ab.astype(BF16), w_out.astype(BF16), post_g.reshape(1, d), gt_m, pre_g.reshape(1, d),
      sc_f, sh_f, rw, rb)


def _chunk_copies(cnt_ref, start_ref, tile, ne, max_rows, make):
    def per_expert(e, off):
        c, start = cnt_ref[tile * ne + e], start_ref[tile * ne + e]
        for b in range(max_rows.bit_length()):
            n = 1 << b

            @pl.when((c >> b) & 1 == 1)
            def _():
                make(off + (c & (n - 1)), start + (c & (n - 1)), n)
        return off + c

    lax.fori_loop(0, ne, per_expert, 0)


def _dispatch_kernel(cnt_ref, start_ref, pad_start_ref, pad_cnt_ref, tail_ref, loct_ref, h_ref, xs_ref,
                     buf, zero_ref, sem, zsem, *, tm, sub, bm, n_rows, ne):
    i = pl.program_id(0)
    last = pl.num_programs(0) - 1
    slot = i % 2
    n_loc = tm * TOP_K

    def row(ref, r, n=1):
        return ref.at[pl.ds(pl.multiple_of(r * sub, sub), n * sub)]

    def wait_slot(s):
        pltpu.make_async_copy(buf.at[s], row(xs_ref, 0, n_loc), sem.at[s]).wait()

    def zero_copies(do):
        def per_expert(e, _):
            start, cnt = pad_start_ref[e], pad_cnt_ref[e]
            for b in range(bm.bit_length() - 1):
                n = 1 << b

                @pl.when((cnt >> b) & 1 == 1)
                def _():
                    do(pltpu.make_async_copy(row(zero_ref, 0, n), row(xs_ref, start + (cnt & (n - 1)), n), zsem))
            return 0

        lax.fori_loop(0, ne, per_expert, 0)

        def per_block(j, _):
            r = tail_ref[0] + j * bm

            @pl.when(r < n_rows)
            def _():
                do(pltpu.make_async_copy(row(zero_ref, 0, bm), row(xs_ref, r, bm), zsem))
            return 0

        lax.fori_loop(0, ne, per_block, 0)

    @pl.when(i == 0)
    def _():
        zero_ref[...] = jnp.zeros_like(zero_ref)
        zero_copies(lambda cp: cp.start())

    @pl.when(i >= 2)
    def _():
        wait_slot(slot)

    loct = loct_ref[0]
    row_i = lax.broadcasted_iota(I32, (n_loc, tm), 0)
    place = row_i == loct[0:1, :]
    for k in range(1, TOP_K):
        place = place | (row_i == loct[k:k + 1, :])
    grouped = _dot(jnp.where(place, 1.0, 0.0).astype(BF16), _load_slab(h_ref, tm, sub).astype(BF16))
    _store_slab(buf.at[slot], grouped)
    _chunk_copies(cnt_ref, start_ref, i, ne, tm,
                  lambda loc, glob, n: pltpu.make_async_copy(row(buf.at[slot], loc, n), row(xs_ref, glob, n),
                                                             sem.at[slot]).start())

    @pl.when(i == last)
    def _():
        wait_slot(slot)

    @pl.when((i == last) & (i >= 1))
    def _():
        wait_slot(1 - slot)

    @pl.when(i == 0)
    def _():
        zero_copies(lambda cp: cp.wait())


def _dispatch(h_slab, loc, cnt_tile, start_tile, pad_start, pad_cnt, tail, n_rows, sub, bm, tm):
    n_tok = loc.shape[0]
    nt = n_tok // tm
    ne = pad_start.shape[0]
    loct = jnp.full((nt, 8, tm), -1, I32).at[:, :TOP_K].set(loc.reshape(nt, tm, TOP_K).transpose(0, 2, 1))
    grid_spec = pltpu.PrefetchScalarGridSpec(
        num_scalar_prefetch=5,
        grid=(nt,),
        in_specs=[pl.BlockSpec((1, 8, tm), lambda i, *_: (i, 0, 0)),
                  pl.BlockSpec((tm * sub, LANES), lambda i, *_: (i, 0))],
        out_specs=pl.BlockSpec(memory_space=pl.ANY),
        scratch_shapes=[pltpu.VMEM((2, TOP_K * tm * sub, LANES), F32), pltpu.VMEM((bm * sub, LANES), F32),
                        pltpu.SemaphoreType.DMA((2,)), pltpu.SemaphoreType.DMA(())],
    )
    return pl.pallas_call(
        functools.partial(_dispatch_kernel, tm=tm, sub=sub, bm=bm, n_rows=n_rows, ne=ne),
        grid_spec=grid_spec,
        out_shape=jax.ShapeDtypeStruct((n_rows * sub, LANES), F32),
        compiler_params=_cparams(("arbitrary",)),
        name="dispatch",
    )(cnt_tile.reshape(-1), start_tile.reshape(-1), pad_start, pad_cnt, tail, loct, h_slab)


def _expert_kernel(be_ref, nv_ref, x_ref, wg_ref, bg_ref, wu_ref, bu_ref, wd_ref, bd_ref,
                   o_ref, wg_s, wu_s, wd_s, *, bm, sub):
    i = pl.program_id(0)
    prev = be_ref[jnp.maximum(i - 1, 0)]
    changed = (i == 0) | (be_ref[i] != prev)

    @pl.when(changed)
    def _():
        wg_s[...] = wg_ref[0].astype(BF16)
        wu_s[...] = wu_ref[0].astype(BF16)
        wd_s[...] = wd_ref[0].astype(BF16)

    @pl.when(i < nv_ref[0])
    def _():
        xb = _load_slab(x_ref, bm, sub).astype(BF16)
        g = _dot(xb, wg_s[...]) + bg_ref[0]
        u = _dot(xb, wu_s[...]) + bu_ref[0]
        g = jnp.minimum(g, SWIGLU_LIMIT)
        u = jnp.clip(u, -SWIGLU_LIMIT, SWIGLU_LIMIT)
        act = g * jax.nn.sigmoid(SWIGLU_ALPHA * g) * (u + 1.0)
        _store_slab(o_ref, _dot(act.astype(BF16), wd_s[...]) + bd_ref[0])

    @pl.when(i >= nv_ref[0])
    def _():
        o_ref[...] = jnp.zeros_like(o_ref)


def _experts(xs, blk_exp, n_valid, w_gate, b_gate, w_up, b_up, w_down, b_down, bm):
    ne, d, f = w_gate.shape
    sub = d // LANES
    n_blocks = xs.shape[0] // (bm * sub)
    wspec = lambda a, b: pl.BlockSpec((1, a, b), lambda i, be, nv: (be[i], 0, 0))
    grid_spec = pltpu.PrefetchScalarGridSpec(
        num_scalar_prefetch=2,
        grid=(n_blocks,),
        in_specs=[pl.BlockSpec((bm * sub, LANES), lambda i, be, nv: (jnp.clip(nv[0] - 1, 0, i), 0)),
                  wspec(d, f), wspec(1, f), wspec(d, f), wspec(1, f), wspec(f, d), wspec(1, d)],
        out_specs=pl.BlockSpec((bm * sub, LANES), lambda i, be, nv: (i, 0)),
        scratch_shapes=[pltpu.VMEM((d, f), BF16), pltpu.VMEM((d, f), BF16), pltpu.VMEM((f, d), BF16)],
    )
    return pl.pallas_call(
        functools.partial(_expert_kernel, bm=bm, sub=sub),
        grid_spec=grid_spec,
        out_shape=jax.ShapeDtypeStruct(xs.shape, F32),
        compiler_params=_cparams(("arbitrary",)),
        name="experts",
    )(blk_exp, n_valid, xs, w_gate, b_gate.reshape(ne, 1, f), w_up, b_up.reshape(ne, 1, f),
      w_down, b_down.reshape(ne, 1, d))


def _combine_kernel(cnt_ref, start_ref, yb_ref, loc_ref, tw_ref, x1_ref, g_ref, gt_ref, o_ref, buf, sem,
                    *, tm, sub, ne):
    i = pl.program_id(0)
    n_loc = tm * TOP_K

    def start_tile(tile, slot):
        def make(loc, glob, n):
            src = yb_ref.at[pl.ds(pl.multiple_of(glob * sub, sub), n * sub)]
            dst = buf.at[slot, pl.ds(pl.multiple_of(loc * sub, sub), n * sub)]
            pltpu.make_async_copy(src, dst, sem.at[slot]).start()

        _chunk_copies(cnt_ref, start_ref, tile, ne, tm, make)

    slot = i % 2

    @pl.when(i == 0)
    def _():
        start_tile(0, 0)

    @pl.when(i + 1 < pl.num_programs(0))
    def _():
        start_tile(i + 1, 1 - slot)

    pltpu.make_async_copy(yb_ref.at[pl.ds(0, n_loc * sub)], buf.at[slot], sem.at[slot]).wait()
    rows = _load_slab(buf.at[slot], n_loc, sub).astype(BF16)
    loc, tw = loc_ref[...], tw_ref[...]
    col_i = lax.broadcasted_iota(I32, (tm, n_loc), 1)
    w = jnp.zeros((tm, n_loc), F32)
    for k in range(TOP_K):
        w = jnp.where(col_i == loc[:, k:k + 1], tw[:, k:k + 1], w)
    ff = _dot(w.astype(BF16), rows)
    o_ref[...] = x1_ref[...] + gt_ref[0] * _rms(ff, g_ref[...])


def _combine(yb, loc, tw, cnt_tile, start_tile, x1, post_g, gt_f, seq, tm):
    n_tok, d = x1.shape
    sub = d // LANES
    nt = n_tok // tm
    ne = cnt_tile.shape[1]
    per_b = seq // tm
    loc_l = jnp.full((n_tok, LANES), -1, I32).at[:, :TOP_K].set(loc)
    grid_spec = pltpu.PrefetchScalarGridSpec(
        num_scalar_prefetch=2,
        grid=(nt,),
        in_specs=[pl.BlockSpec(memory_space=pl.ANY),
                  pl.BlockSpec((tm, LANES), lambda i, *_: (i, 0)),
                  pl.BlockSpec((tm, LANES), lambda i, *_: (i, 0)),
                  pl.BlockSpec((tm, d), lambda i, *_: (i, 0)),
                  pl.BlockSpec((1, d), lambda i, *_: (0, 0)),
                  pl.BlockSpec((1, 1, d), lambda i, *_: (i // per_b, 0, 0))],
        out_specs=pl.BlockSpec((tm, d), lambda i, *_: (i, 0)),
        scratch_shapes=[pltpu.VMEM((2, TOP_K * tm * sub, LANES), F32), pltpu.SemaphoreType.DMA((2,))],
    )
    return pl.pallas_call(
        functools.partial(_combine_kernel, tm=tm, sub=sub, ne=ne),
        grid_spec=grid_spec,
        out_shape=jax.ShapeDtypeStruct((n_tok, d), F32),
        compiler_params=_cparams(("arbitrary",)),
        name="combine",
    )(cnt_tile.reshape(-1), start_tile.reshape(-1), yb, loc_l, tw, x1, post_g.reshape(1, d), gt_f)


def kernel(x, c, positions, ada_w, ada_b, mix_pre_g, mix_post_g, ffn_pre_g, ffn_post_g, w_in, ssm_lam_re, ssm_lam_im, ssm_log_dt, ssm_b_re, ssm_b_im, ssm_c_re, ssm_c_im, ssm_d, ssm_w_glu, w_ssm_branch, w_attn_branch, w_out, router_w, router_b, w_gate, b_gate, w_up, b_up, w_down, b_down):
    bsz, seq, d = x.shape
    depth = ada_w.shape[0]
    n_tok = bsz * seq
    bm = EXPERT_ROWS
    sub = d // LANES
    xcur = x.reshape(n_tok, d)
    pos = positions.reshape(n_tok, 1).astype(I32)
    for l in range(depth):
        ada = _ada(c, ada_w[l], ada_b[l])
        sh_m, sc_m, gt_m, sh_f, sc_f, gt_f = [a.reshape(bsz, 1, d) for a in jnp.split(ada, 6, axis=-1)]

        u, q, k, v, gs, ga, km = _inproj(xcur, sc_m, sh_m, mix_pre_g[l].reshape(1, d), pos, w_in[l],
                                         seq, tm=min(512, seq))
        ys = _s5(u, ssm_lam_re[l], ssm_lam_im[l], ssm_log_dt[l], ssm_b_re[l], ssm_b_im[l],
                 ssm_c_re[l], ssm_c_im[l], bsz, seq)
        ya = _moba(q, k, v, km, bsz, seq)
        tm = ROUTE_TILE
        x1, h2, eid, tw, rank, before, cnt = _merge(
            ys, u, ya, gs, ga, xcur, ssm_d[l], ssm_w_glu[l], w_ssm_branch[l], w_attn_branch[l], w_out[l],
            mix_post_g[l], gt_m, ffn_pre_g[l], sc_f, sh_f, router_w[l], router_b[l], seq, tm=tm)

        ne = router_w.shape[-1]
        nt = n_tok // tm
        experts = jnp.arange(ne, dtype=I32)
        counts = cnt[0, :ne].astype(I32)
        padded = (counts + bm - 1) // bm * bm
        p_ends = jnp.cumsum(padded)
        p_starts = p_ends - padded
        before = before.reshape(nt, LANES)[:, :ne].astype(I32)
        cnt_tile = jnp.concatenate([before[1:], counts[None]], axis=0) - before
        start_tile = p_starts[None, :] + before
        off_tile = jnp.cumsum(cnt_tile, axis=1) - cnt_tile
        onehot = eid[:, :TOP_K].reshape(nt, tm, TOP_K, 1) == experts
        loc = (rank[:, :TOP_K] + jnp.sum(jnp.where(onehot, off_tile[:, None, None, :], 0), axis=-1).reshape(n_tok, TOP_K))
        n_blocks = (n_tok * TOP_K) // bm + ne
        n_rows = n_blocks * bm
        blk_start = jnp.arange(n_blocks, dtype=I32)[:, None] * bm
        blk_exp = jnp.minimum(jnp.sum((blk_start >= p_ends[None, :]).astype(I32), axis=1), ne - 1)
        n_valid = (p_ends[-1:] // bm).astype(I32)

        xs = _dispatch(h2, loc, cnt_tile, start_tile, p_starts + counts, padded - counts, p_ends[-1:],
                       n_rows, sub, bm, tm)
        yb = _experts(xs, blk_exp, n_valid, w_gate[l], b_gate[l], w_up[l], b_up[l], w_down[l], b_down[l], bm)
        xcur = _combine(yb, loc, tw, cnt_tile, start_tile, x1, ffn_post_g[l], gt_f, seq, tm)
    return xcur.reshape(bsz, seq, d).astype(x.dtype)
```

```python
import functools
import math

import jax
import jax.numpy as jnp
from jax import lax
from jax.experimental import pallas as pl
from jax.experimental.pallas import tpu as pltpu

F32 = jnp.float32
BF16 = jnp.bfloat16
I32 = jnp.int32

N_HEADS = 8
HEAD_DIM = 64
ROPE_THETA = 10000.0
MOBA_BLOCK = 256
MOBA_TOPK = 3
SSM_GROUP_SIZE = 16
SSM_GROUPS = 32
SSM_STATE = 64
N_EXPERTS = 32
TOP_K = 4
SWIGLU_ALPHA = 1.702
SWIGLU_LIMIT = 7.0
NORM_EPS = 1e-6
NEG_INF = -1e30

LANES = 128
SSM_CHUNK = 64
EXPERT_ROWS = 512
ROUTE_TILE = 512
VMEM_LIMIT = 56 * 1024 * 1024
HIGHEST = lax.Precision.HIGHEST
Q_SCALE = HEAD_DIM ** -0.5 * math.log2(math.e)


def _cparams(sem):
    return pltpu.CompilerParams(dimension_semantics=sem, vmem_limit_bytes=VMEM_LIMIT)


def _dot(a, b, **kw):
    return jnp.dot(a, b, preferred_element_type=F32, **kw)


def _dot_t(a, b, **kw):
    return lax.dot_general(a, b, (((1,), (1,)), ((), ())), preferred_element_type=F32, **kw)


def _ada_kernel(c_ref, w_ref, b_ref, o_ref):
    c = c_ref[...]
    cond = c * jax.nn.sigmoid(c)
    o_ref[...] = _dot(cond, w_ref[...], precision=HIGHEST) + b_ref[...]


def _ada(c, ada_w, ada_b):
    bsz, d = c.shape
    n = ada_w.shape[1]
    c8 = jnp.zeros((8, d), F32).at[:bsz].set(c)
    out = pl.pallas_call(
        _ada_kernel,
        grid=(n // d,),
        in_specs=[pl.BlockSpec((8, d), lambda j: (0, 0)),
                  pl.BlockSpec((d, d), lambda j: (0, j)),
                  pl.BlockSpec((1, d), lambda j: (0, j))],
        out_specs=pl.BlockSpec((8, d), lambda j: (0, j)),
        out_shape=jax.ShapeDtypeStruct((8, n), F32),
        compiler_params=_cparams(("arbitrary",)),
        name="ada",
    )(c8, ada_w, ada_b.reshape(1, n))
    return out[:bsz]


def _transpose_kernel(w_ref, o_ref):
    o_ref[...] = w_ref[...].T.astype(BF16)


def _inproj_kernel(x_ref, sc_ref, sh_ref, g_ref, pos_ref, invf_ref, post_ref, invft_ref, w_ref, wt_ref,
                   u_ref, qt_ref, k_ref, vt_ref, gs_ref, ga_ref, km_ref, *, ssm_w, attn_w, d_model):
    x = x_ref[...]
    ms = jnp.mean(x * x, axis=-1, keepdims=True)
    xn = x * lax.rsqrt(ms + NORM_EPS) * g_ref[...]
    h = (xn * (1.0 + sc_ref[0]) + sh_ref[0]).astype(BF16)

    def proj(lo, width):
        return _dot(h, w_ref[:, lo:lo + width])

    u_ref[...] = proj(0, ssm_w)

    ang = pos_ref[...].astype(F32) * invf_ref[...]
    reps = attn_w // LANES
    cos = jnp.concatenate([jnp.cos(ang)] * reps, axis=1)
    sin = jnp.concatenate([jnp.sin(ang)] * reps, axis=1)
    lane = lax.broadcasted_iota(I32, (1, attn_w), 1)
    first = (lane % HEAD_DIM) < (HEAD_DIM // 2)
    sin = jnp.where(first, -sin, sin)

    def rope(t):
        rot = jnp.where(first, pltpu.roll(t, attn_w - HEAD_DIM // 2, axis=1),
                        pltpu.roll(t, HEAD_DIM // 2, axis=1))
        return t * cos + rot * sin

    k = rope(proj(ssm_w + attn_w, attn_w))
    k_ref[...] = k.astype(BF16)
    tm = k.shape[0]
    nblk = tm // MOBA_BLOCK
    km_ref[0] = jnp.mean(k.reshape(nblk, MOBA_BLOCK, attn_w), axis=1)

    half = HEAD_DIM // 2
    ang_t = invft_ref[...] * post_ref[0].astype(F32)
    cos_t, sin_t = jnp.cos(ang_t), jnp.sin(ang_t)
    qt = _dot_t(wt_ref[0:attn_w, :], h).reshape(N_HEADS, 2, half, tm)
    t1, t2 = qt[:, 0], qt[:, 1]
    qt = jnp.stack([t1 * cos_t - t2 * sin_t, t2 * cos_t + t1 * sin_t], axis=1).reshape(attn_w, tm)
    qt_ref[...] = (qt * Q_SCALE).astype(BF16)
    vt = _dot_t(wt_ref[attn_w:2 * attn_w, :], h).astype(BF16)
    for j in range(nblk):
        vt_ref[j] = vt[:, j * MOBA_BLOCK:(j + 1) * MOBA_BLOCK]
    gs_ref[...] = jax.nn.sigmoid(proj(ssm_w + 3 * attn_w, d_model)).astype(BF16)
    ga_ref[...] = jax.nn.sigmoid(proj(ssm_w + 3 * attn_w + d_model, d_model)).astype(BF16)


def _inproj(x2, sc, sh, g, pos, w_in, seq, tm):
    n_tok, d = x2.shape
    attn_w = N_HEADS * HEAD_DIM
    ssm_w = SSM_GROUPS * SSM_GROUP_SIZE
    in_w = w_in.shape[1]
    half = HEAD_DIM // 2
    inv_freq = ROPE_THETA ** (-jnp.arange(half, dtype=F32) / half)
    invf = jnp.tile(inv_freq, LANES // half).reshape(1, LANES)
    nt = n_tok // tm
    per_b = seq // tm
    nblk = tm // MOBA_BLOCK
    bsz = n_tok // seq
    blk = MOBA_BLOCK
    assert ssm_w % attn_w == 0
    q_blk, v_blk = ssm_w // attn_w, ssm_w // attn_w + 2
    wt = pl.pallas_call(
        _transpose_kernel,
        grid=(2,),
        in_specs=[pl.BlockSpec((d, attn_w), lambda j: (0, q_blk + j * (v_blk - q_blk)))],
        out_specs=pl.BlockSpec((attn_w, d), lambda j: (j, 0)),
        out_shape=jax.ShapeDtypeStruct((2 * attn_w, d), BF16),
        compiler_params=_cparams(("arbitrary",)),
        name="wt",
    )(w_in)
    tok = lambda w: pl.BlockSpec((tm, w), lambda i: (i, 0))
    full = lambda a, b: pl.BlockSpec((a, b), lambda i: (0, 0))
    bvec = pl.BlockSpec((1, 1, d), lambda i: (i // per_b, 0, 0))
    outs = pl.pallas_call(
        functools.partial(_inproj_kernel, ssm_w=ssm_w, attn_w=attn_w, d_model=d),
        grid=(nt,),
        in_specs=[tok(d), bvec, bvec, full(1, d), tok(1), full(1, LANES),
                  pl.BlockSpec((1, 1, tm), lambda i: (i, 0, 0)), full(half, 1),
                  full(d, in_w), full(2 * attn_w, d)],
        out_specs=[tok(ssm_w),
                   pl.BlockSpec((None, attn_w, tm), lambda i: (i // per_b, 0, i % per_b)),
                   tok(attn_w),
                   pl.BlockSpec((None, nblk, attn_w, blk), lambda i: (i // per_b, i % per_b, 0, 0)),
                   tok(d), tok(d),
                   pl.BlockSpec((1, nblk, attn_w), lambda i: (i, 0, 0))],
        out_shape=[jax.ShapeDtypeStruct((n_tok, ssm_w), F32),
                   jax.ShapeDtypeStruct((bsz, attn_w, seq), BF16),
                   jax.ShapeDtypeStruct((n_tok, attn_w), BF16),
                   jax.ShapeDtypeStruct((bsz, seq // blk, attn_w, blk), BF16),
                   jax.ShapeDtypeStruct((n_tok, d), BF16),
                   jax.ShapeDtypeStruct((n_tok, d), BF16),
                   jax.ShapeDtypeStruct((nt, nblk, attn_w), F32)],
        compiler_params=_cparams(("arbitrary",)),
        name="inproj",
    )(x2, sc, sh, g, pos, invf, pos.reshape(nt, 1, tm), inv_freq.reshape(half, 1), w_in.astype(BF16), wt)
    return outs


def _s5_group(x, zr, zi, lr, li, ca, cb, ba, bb, t_ref, *, chunk, n_chunks):
    gs = SSM_GROUP_SIZE

    def powers(tau):
        mag = jnp.exp(tau * zr)
        return mag * jnp.cos(tau * zi), mag * jnp.sin(tau * zi)

    tau = lax.broadcasted_iota(I32, (chunk + 8, LANES), 0).astype(F32)
    e_re, e_im = powers(tau)
    lb_re, lb_im = e_re[1:2], e_im[1:2]
    den = lr * lr + li * li
    a, b = lb_re - 1.0, lb_im
    cf_re = (a * lr + b * li) / den
    cf_im = (b * lr - a * li) / den
    bri = cf_re * ba + cf_im * bb
    bri_sw = cf_re * bb - cf_im * ba

    cpow =(e_re[:chunk + 1, None, :] * ca[None] + e_im[:chunk + 1, None, :] * cb[None])
    cpow = cpow.reshape((chunk + 1) * gs, LANES)
    width = chunk * gs
    r = _dot_t(bri, cpow[:width], precision=HIGHEST)
    col = lax.broadcasted_iota(I32, (gs, width), 1)
    t_ref[0:gs, :] = r.astype(BF16)
    for j in range(1, chunk):
        shifted = jnp.where(col >= gs * j, pltpu.roll(r, gs * j, axis=1), 0.0)
        t_ref[gs * j:gs * (j + 1), :] = shifted.astype(BF16)

    y = _dot(x, t_ref[...])

    tau_rev = (chunk - 1) - lax.broadcasted_iota(I32, (chunk, LANES), 0)
    r_re, r_im = powers(tau_rev.astype(F32))
    bst = r_re[:, None, :] * bri[None] + r_im[:, None, :] * bri_sw[None]
    bst = bst.reshape(width, LANES).astype(BF16)
    s = _dot(x, bst)

    rows = s.shape[0]
    n_idx = lax.broadcasted_iota(I32, (rows, LANES), 0) % n_chunks
    lane = lax.broadcasted_iota(I32, (1, LANES), 1)
    half = LANES // 2

    def cmul(v, p_re, p_im):
        return v * p_re + pltpu.roll(v, half, axis=1) * jnp.where(lane < half, -p_im, p_im)

    sh = 1
    while sh < n_chunks:
        p_re, p_im = powers(jnp.full((1, LANES), float(chunk * sh), F32))
        prev = jnp.where(n_idx >= sh, pltpu.roll(s, sh, axis=0), 0.0)
        s = s + cmul(prev, p_re, p_im)
        sh *= 2
    s_in = jnp.where(n_idx >= 1, pltpu.roll(s, 1, axis=0), 0.0)
    return y + _dot_t(s_in.astype(BF16), cpow[gs:gs + width].astype(BF16))


def _s5_kernel(u_ref, sel_ref, zr_ref, zi_ref, lr_ref, li_ref, ca_ref, cb_ref, ba_ref, bb_ref,
               y_ref, x_scr, y_scr, t_ref, *, chunk, n_chunks, rows):
    gpt = LANES // SSM_GROUP_SIZE

    def plane(j):
        return pl.ds(j, rows, stride=chunk)

    for jt in range(chunk // gpt):
        planes = jnp.concatenate([u_ref[plane(jt * gpt + jj), :] for jj in range(gpt)], axis=1)
        grouped = _dot(planes.astype(BF16), sel_ref[...])
        for g in range(gpt):
            x_scr[g, :, jt * LANES:(jt + 1) * LANES] = grouped[:, g * LANES:(g + 1) * LANES].astype(BF16)

    def per_group(g, _):
        y = _s5_group(x_scr[g], zr_ref[g], zi_ref[g], lr_ref[g], li_ref[g], ca_ref[g], cb_ref[g], ba_ref[g],
                      bb_ref[g], t_ref, chunk=chunk, n_chunks=n_chunks)
        y_scr[g] = y.astype(BF16)
        return 0

    lax.fori_loop(0, gpt, per_group, 0)

    for jt in range(chunk // gpt):
        grouped = jnp.concatenate([y_scr[g, :, jt * LANES:(jt + 1) * LANES] for g in range(gpt)], axis=1)
        planes = _dot(grouped, sel_ref[...])
        for jj in range(gpt):
            y_ref[plane(jt * gpt + jj), :] = planes[:, jj * LANES:(jj + 1) * LANES]


def _s5(u, lam_re, lam_im, log_dt, b_re, b_im, c_re, c_im, bsz, seq):
    g_n, gs, p = SSM_GROUPS, SSM_GROUP_SIZE, SSM_STATE
    chunk = SSM_CHUNK
    n_chunks = seq // chunk
    rows = bsz * n_chunks
    width = chunk * gs
    gpt = LANES // gs
    assert chunk % gpt == 0 and g_n % gpt == 0 and 2 * p == LANES
    dt = jnp.exp(log_dt.astype(F32))[:, None]
    dup = lambda a: jnp.concatenate([a, a], axis=-1).reshape(g_n, 1, 2 * p)
    zr, zi = dup(lam_re * dt), dup(lam_im * dt)
    lr, li = dup(lam_re), dup(lam_im)
    bt_re, bt_im = b_re.transpose(0, 2, 1), b_im.transpose(0, 2, 1)
    ba = jnp.concatenate([bt_re, bt_im], axis=-1)
    bb = jnp.concatenate([-bt_im, bt_re], axis=-1)
    ca = jnp.concatenate([c_re, -c_im], axis=-1)
    cb = jnp.concatenate([-c_im, -c_re], axis=-1)
    idx = jnp.arange(gpt * LANES, dtype=I32)
    swapped = ((idx // gs) % gpt) * LANES + (idx // LANES) * gs + idx % gs
    sel = (swapped[:, None] == idx[None, :]).astype(BF16)
    tile = pl.BlockSpec((bsz * seq, LANES), lambda t: (0, t))
    vec = pl.BlockSpec((gpt, 1, 2 * p), lambda t: (t, 0, 0))
    mat = pl.BlockSpec((gpt, gs, 2 * p), lambda t: (t, 0, 0))
    return pl.pallas_call(
        functools.partial(_s5_kernel, chunk=chunk, n_chunks=n_chunks, rows=rows),
        grid=(g_n // gpt,),
        in_specs=[tile, pl.BlockSpec((gpt * LANES, gpt * LANES), lambda t: (0, 0)),
                  vec, vec, vec, vec, mat, mat, mat, mat],
        out_specs=tile,
        out_shape=jax.ShapeDtypeStruct((bsz * seq, g_n * gs), F32),
        scratch_shapes=[pltpu.VMEM((gpt, rows, width), BF16), pltpu.VMEM((gpt, rows, width), BF16),
                        pltpu.VMEM((width, width), BF16)],
        compiler_params=_cparams(("arbitrary",)),
        name="s5",
    )(u, sel, zr, zi, lr, li, ca, cb, ba, bb)


def _moba_kernel(qt_ref, k_ref, vt_ref, km_ref, o_ref, s_a, s_b, s_fin, p_a, p_b, *, nb, nbp):
    blk = MOBA_BLOCK
    group = 2
    keys = nq = group * blk
    i = pl.program_id(2)
    qt = qt_ref[...]
    km = km_ref[...]
    km_hi = km.astype(BF16)
    km_lo = (km - km_hi.astype(F32)).astype(BF16)
    dim_i = lax.broadcasted_iota(I32, (LANES, nq), 0)
    blk_i = lax.broadcasted_iota(I32, (nbp, nq), 0)
    blk_f = blk_i.astype(F32)
    own = group * i + lax.broadcasted_iota(I32, (nbp, nq), 1) // blk
    valid = blk_i < own

    q_loop, q_fin = [], []
    for h in range(2):
        qh = jnp.where((dim_i >= h * HEAD_DIM) & (dim_i < (h + 1) * HEAD_DIM), qt, jnp.zeros_like(qt))
        g = jnp.where(valid, _dot(km_hi, qh) + _dot(km_lo, qh), NEG_INF)
        sel = jnp.zeros((nbp, nq), jnp.bool_)
        for _ in range(MOBA_TOPK):
            m = jnp.max(g, axis=0, keepdims=True)
            idx = jnp.min(jnp.where(g == m, blk_f, float(nbp)), axis=0, keepdims=True)
            pick = blk_f == idx
            sel = sel | pick
            g = jnp.where(pick, -jnp.inf, g)
        allowed = sel & valid
        for lst, ok in ((q_loop, allowed & (blk_i < group * i)), (q_fin, allowed | (blk_i == own))):
            parts = [qh, jnp.where(ok, 0.0, NEG_INF).astype(BF16)]
            if nbp < LANES:
                parts.append(jnp.full((LANES - nbp, nq), NEG_INF, BF16))
            lst.append(jnp.concatenate(parts, axis=0))

    def scores(grp, s_ref, q_aug, causal=False):
        kb0 = grp * group
        k_cat = k_ref[pl.ds(jnp.minimum(kb0, nb - group), group)].reshape(keys, LANES)
        blk_id = kb0 + lax.broadcasted_iota(I32, (keys, LANES), 0) // blk
        onehot = jnp.where(lax.broadcasted_iota(I32, (keys, LANES), 1) == blk_id, 1.0, 0.0).astype(BF16)
        k_aug = jnp.concatenate([k_cat, onehot], axis=1)
        mx = []
        for h in range(2):
            s = _dot(k_aug, q_aug[h])
            if causal:
                key_i = lax.broadcasted_iota(I32, (keys, nq), 0)
                s = jnp.where(key_i <= lax.broadcasted_iota(I32, (keys, nq), 1), s, NEG_INF)
            s_ref[h] = s
            mx.append(jnp.max(s, axis=0, keepdims=True))
        return tuple(mx)

    def softmax(s_ref, mx, p_ref, st):
        out = []
        for h in range(2):
            m_i, _, l_i, acc = st[h]
            m_new = jnp.maximum(m_i, mx[h])
            p_ref[h] = jnp.exp2(s_ref[h] - m_new).astype(BF16)
            out.append((m_new, jnp.exp2(m_i - m_new), l_i, acc))
        return tuple(out)

    def values(grp, p_ref, st):
        out = []
        for h in range(2):
            m_i, alpha, l_i, acc = st[h]
            blocks = [vt_ref[jnp.clip(grp * group + j, 0, nb - 1), h * HEAD_DIM:(h + 1) * HEAD_DIM, :]
                      for j in range(group)]
            ones = jnp.ones((16, keys), BF16)
            pv = _dot(jnp.concatenate([jnp.concatenate(blocks, axis=1), ones], axis=0), p_ref[h])
            out.append((m_i, alpha, alpha * l_i + pv[HEAD_DIM:HEAD_DIM + 1], alpha * acc + pv[:HEAD_DIM]))
        return tuple(out)

    mx_first = scores(0, s_a, q_loop)
    p_b[...] = jnp.zeros_like(p_b)

    def body(t, carry):
        st, mx_a = carry
        mx_b = scores(2 * t + 1, s_b, q_loop)
        st = values(2 * t - 1, p_b, st)
        st = softmax(s_a, mx_a, p_a, st)
        mx_a = scores(2 * t + 2, s_a, q_loop)
        st = values(2 * t, p_a, st)
        return softmax(s_b, mx_b, p_b, st), mx_a

    init = (jnp.full((1, nq), -jnp.inf, F32), jnp.ones((1, nq), F32), jnp.zeros((1, nq), F32),
            jnp.zeros((HEAD_DIM, nq), F32))
    trips = (i + 1) // 2
    st, _ = lax.fori_loop(0, trips, body, ((init, init), mx_first))
    mx_fin = scores(i, s_fin, q_fin, causal=True)
    st = values(2 * trips - 1, p_b, st)
    st = values(i, p_a, softmax(s_fin, mx_fin, p_a, st))
    both = jnp.concatenate([st[h][3] / st[h][2] for h in range(2)], axis=0)
    o_ref[...] = both.T.astype(BF16)


def _moba(qt, k, vt4, km, bsz, seq):
    n_tok, attn_w = k.shape
    blk = MOBA_BLOCK
    nb = seq // blk
    group = 2
    nbp = -(-nb // 16) * 16
    assert nb % group == 0 and nb + 2 * group <= LANES
    hp = attn_w // LANES
    km_pad = jnp.zeros((bsz, nbp, attn_w), F32).at[:, :nb].set(km.reshape(bsz, nb, attn_w))
    k4 = k.reshape(bsz, nb, blk, attn_w)
    steps = nb // group
    return pl.pallas_call(
        functools.partial(_moba_kernel, nb=nb, nbp=nbp),
        grid=(bsz, hp, nb // group),
        in_specs=[pl.BlockSpec((None, LANES, group * blk), lambda b, p, i: (b, p, i)),
                  pl.BlockSpec((None, nb, blk, LANES), lambda b, p, i: (b, 0, 0, p)),
                  pl.BlockSpec((None, nb, LANES, blk), lambda b, p, i: (b, 0, p, 0)),
                  pl.BlockSpec((None, nbp, LANES), lambda b, p, i: (b, 0, p))],
        out_specs=pl.BlockSpec((group * blk, LANES), lambda b, p, i: (b * steps + i, p)),
        out_shape=jax.ShapeDtypeStruct((n_tok, attn_w), BF16),
        scratch_shapes=([pltpu.VMEM((2, group * blk, group * blk), F32)] * 3
                        + [pltpu.VMEM((2, group * blk, group * blk), BF16)] * 2),
        compiler_params=_cparams(("arbitrary", "arbitrary", "arbitrary")),
        name="moba",
    )(qt, k4, vt4, km_pad)


def _rms(t, g):
    return t * lax.rsqrt(jnp.mean(t * t, axis=-1, keepdims=True) + NORM_EPS) * g


def _store_slab(ref, val, base=0):
    rows, d = val.shape
    sub = d // LANES
    for s in range(sub):
        ref[pl.ds(base + s, rows, stride=sub), :] = val[:, s * LANES:(s + 1) * LANES]


def _load_slab(ref, rows, sub, base=0):
    return jnp.concatenate([ref[pl.ds(base + s, rows, stride=sub), :] for s in range(sub)], axis=1)


def _merge_kernel(ys_ref, u_ref, ya_ref, gs_ref, ga_ref, x_ref, d_ref, wglu_ref, wsb_ref, wab_ref,
                  wout_ref, postg_ref, gtm_ref, preg_ref, scf_ref, shf_ref, rw_ref, rb_ref,
                  x1_ref, h2_ref, eid_ref, tw_ref, rank_ref, before_ref, cnt_ref, run_ref):
    @pl.when(pl.program_id(0) == 0)
    def _():
        run_ref[...] = jnp.zeros_like(run_ref)

    y = ys_ref[...] + d_ref[...] * u_ref[...].astype(F32)
    y = jax.nn.gelu(y)
    y = y * jax.nn.sigmoid(_dot(y.astype(BF16), wglu_ref[...]))
    bs = _dot(y.astype(BF16), wsb_ref[...])
    ba = _dot(ya_ref[...], wab_ref[...])
    merged = gs_ref[...].astype(F32) * bs + ga_ref[...].astype(F32) * ba
    mix = _dot(merged.astype(BF16), wout_ref[...])
    x1 = x_ref[...] + gtm_ref[0] * _rms(mix, postg_ref[...])
    x1_ref[...] = x1
    h2 = _rms(x1, preg_ref[...]) * (1.0 + scf_ref[0]) + shf_ref[0]
    _store_slab(h2_ref, h2)

    h_hi = h2.astype(BF16)
    h_lo = (h2 - h_hi.astype(F32)).astype(BF16)
    logits = _dot(jnp.concatenate([h_hi, h_hi, h_lo], axis=1), rw_ref[...]) + rb_ref[...]
    tm = logits.shape[0]
    lane = lax.broadcasted_iota(I32, (tm, LANES), 1)
    lane_f = lane.astype(F32)
    g = logits
    vals, picks, idxs = [], [], []
    for _ in range(TOP_K):
        m = jnp.max(g, axis=1, keepdims=True)
        idx = jnp.min(jnp.where(g == m, lane_f, float(LANES)), axis=1, keepdims=True)
        pick = lane_f == idx
        vals.append(m)
        idxs.append(idx.astype(I32))
        picks.append(pick)
        g = jnp.where(pick, -jnp.inf, g)
    exps = [jnp.exp(v - vals[0]) for v in vals]
    tot = exps[0] + exps[1] + exps[2] + exps[3]

    onehot = jnp.where(picks[0] | picks[1] | picks[2] | picks[3], 1.0, 0.0)
    r_i = lax.broadcasted_iota(I32, (tm, tm), 0)
    c_i = lax.broadcasted_iota(I32, (tm, tm), 1)
    tri = jnp.where(c_i < r_i, 1.0, 0.0).astype(BF16)
    rank_full = _dot(tri, onehot.astype(BF16))
    before_ref[0] = run_ref[...]
    run_ref[...] = run_ref[...] + jnp.sum(onehot, axis=0, keepdims=True)
    cnt_ref[...] = run_ref[...]

    eid = jnp.zeros((tm, LANES), I32)
    tw = jnp.zeros((tm, LANES), F32)
    rk = jnp.zeros((tm, LANES), F32)
    for r in range(TOP_K):
        eid = jnp.where(lane == r, idxs[r], eid)
        tw = jnp.where(lane == r, exps[r] / tot, tw)
        pos = jnp.sum(jnp.where(picks[r], rank_full, 0.0), axis=1, keepdims=True)
        rk = jnp.where(lane == r, pos, rk)
    eid_ref[...] = eid
    tw_ref[...] = tw
    rank_ref[...] = rk.astype(I32)


def _merge(ys, u, ya, gs, ga, x2, ssm_d, w_glu, w_sb, w_ab, w_out, post_g, gt_m, pre_g, sc_f, sh_f,
           router_w, router_b, seq, tm):
    n_tok, d = x2.shape
    sw = ys.shape[1]
    aw = ya.shape[1]
    ne = router_w.shape[1]
    per_b = seq // tm
    rw = jnp.zeros((d, LANES), F32).at[:, :ne].set(router_w)
    rw_hi = rw.astype(BF16)
    rw_lo = (rw - rw_hi.astype(F32)).astype(BF16)
    rw = jnp.concatenate([rw_hi, rw_lo, rw_hi], axis=0)
    rb = jnp.full((1, LANES), NEG_INF, F32).at[0, :ne].set(router_b)
    tok = lambda w: pl.BlockSpec((tm, w), lambda i: (i, 0))
    full = lambda a, b: pl.BlockSpec((a, b), lambda i: (0, 0))
    bvec = pl.BlockSpec((1, 1, d), lambda i: (i // per_b, 0, 0))
    return pl.pallas_call(
        _merge_kernel,
        grid=(n_tok // tm,),
        in_specs=[tok(sw), tok(sw), tok(aw), tok(d), tok(d), tok(d),
                  full(1, sw), full(sw, sw), full(sw, d), full(aw, d), full(d, d),
                  full(1, d), bvec, full(1, d), bvec, bvec, full(3 * d, LANES), full(1, LANES)],
        out_specs=[tok(d), pl.BlockSpec((tm * (d // LANES), LANES), lambda i: (i, 0)),
                   tok(LANES), tok(LANES), tok(LANES),
                   pl.BlockSpec((1, 1, LANES), lambda i: (i, 0, 0)), full(1, LANES)],
        out_shape=[jax.ShapeDtypeStruct((n_tok, d), F32),
                   jax.ShapeDtypeStruct((n_tok * (d // LANES), LANES), F32),
                   jax.ShapeDtypeStruct((n_tok, LANES), I32),
                   jax.ShapeDtypeStruct((n_tok, LANES), F32),
                   jax.ShapeDtypeStruct((n_tok, LANES), I32),
                   jax.ShapeDtypeStruct((n_tok // tm, 1, LANES), F32),
                   jax.ShapeDtypeStruct((1, LANES), F32)],
        scratch_shapes=[pltpu.VMEM((1, LANES), F32)],
        compiler_params=_cparams(("arbitrary",)),
        name="merge",
    )(ys, u, ya, gs, ga, x2, ssm_d.reshape(1, sw), w_glu.astype(BF16), w_sb.astype(BF16),
      w_ab.astype(BF16), w_out.astype(BF16), post_g.reshape(1, d), gt_m, pre_g.reshape(1, d),
      sc_f, sh_f, rw, rb)


def _chunk_copies(cnt_ref, start_ref, tile, ne, max_rows, make):
    def per_expert(e, off):
        c, start = cnt_ref[tile * ne + e], start_ref[tile * ne + e]
        for b in range(max_rows.bit_length()):
            n = 1 << b

            @pl.when((c >> b) & 1 == 1)
            def _():
                make(off + (c & (n - 1)), start + (c & (n - 1)), n)
        return off + c

    lax.fori_loop(0, ne, per_expert, 0)


def _dispatch_kernel(cnt_ref, start_ref, pad_start_ref, pad_cnt_ref, tail_ref, loct_ref, h_ref, xs_ref,
                     buf, zero_ref, sem, zsem, *, tm, sub, bm, n_rows, ne):
    i = pl.program_id(0)
    last = pl.num_programs(0) - 1
    slot = i % 2
    n_loc = tm * TOP_K

    def row(ref, r, n=1):
        return ref.at[pl.ds(pl.multiple_of(r * sub, sub), n * sub)]

    def wait_slot(s):
        pltpu.make_async_copy(buf.at[s], row(xs_ref, 0, n_loc), sem.at[s]).wait()

    def zero_copies(do):
        def per_expert(e, _):
            start, cnt = pad_start_ref[e], pad_cnt_ref[e]
            for b in range(bm.bit_length() - 1):
                n = 1 << b

                @pl.when((cnt >> b) & 1 == 1)
                def _():
                    do(pltpu.make_async_copy(row(zero_ref, 0, n), row(xs_ref, start + (cnt & (n - 1)), n), zsem))
            return 0

        lax.fori_loop(0, ne, per_expert, 0)

        def per_block(j, _):
            r = tail_ref[0] + j * bm

            @pl.when(r < n_rows)
            def _():
                do(pltpu.make_async_copy(row(zero_ref, 0, bm), row(xs_ref, r, bm), zsem))
            return 0

        lax.fori_loop(0, ne, per_block, 0)

    @pl.when(i == 0)
    def _():
        zero_ref[...] = jnp.zeros_like(zero_ref)
        zero_copies(lambda cp: cp.start())

    @pl.when(i >= 2)
    def _():
        wait_slot(slot)

    loct = loct_ref[0]
    row_i = lax.broadcasted_iota(I32, (n_loc, tm), 0)
    place = row_i == loct[0:1, :]
    for k in range(1, TOP_K):
        place = place | (row_i == loct[k:k + 1, :])
    grouped = _dot(jnp.where(place, 1.0, 0.0).astype(BF16), _load_slab(h_ref, tm, sub).astype(BF16))
    _store_slab(buf.at[slot], grouped)
    _chunk_copies(cnt_ref, start_ref, i, ne, tm,
                  lambda loc, glob, n: pltpu.make_async_copy(row(buf.at[slot], loc, n), row(xs_ref, glob, n),
                                                             sem.at[slot]).start())

    @pl.when(i == last)
    def _():
        wait_slot(slot)

    @pl.when((i == last) & (i >= 1))
    def _():
        wait_slot(1 - slot)

    @pl.when(i == 0)
    def _():
        zero_copies(lambda cp: cp.wait())


def _dispatch(h_slab, loc, cnt_tile, start_tile, pad_start, pad_cnt, tail, n_rows, sub, bm, tm):
    n_tok = loc.shape[0]
    nt = n_tok // tm
    ne = pad_start.shape[0]
    loct = jnp.full((nt, 8, tm), -1, I32).at[:, :TOP_K].set(loc.reshape(nt, tm, TOP_K).transpose(0, 2, 1))
    grid_spec = pltpu.PrefetchScalarGridSpec(
        num_scalar_prefetch=5,
        grid=(nt,),
        in_specs=[pl.BlockSpec((1, 8, tm), lambda i, *_: (i, 0, 0)),
                  pl.BlockSpec((tm * sub, LANES), lambda i, *_: (i, 0))],
        out_specs=pl.BlockSpec(memory_space=pl.ANY),
        scratch_shapes=[pltpu.VMEM((2, TOP_K * tm * sub, LANES), F32), pltpu.VMEM((bm * sub, LANES), F32),
                        pltpu.SemaphoreType.DMA((2,)), pltpu.SemaphoreType.DMA(())],
    )
    return pl.pallas_call(
        functools.partial(_dispatch_kernel, tm=tm, sub=sub, bm=bm, n_rows=n_rows, ne=ne),
        grid_spec=grid_spec,
        out_shape=jax.ShapeDtypeStruct((n_rows * sub, LANES), F32),
        compiler_params=_cparams(("arbitrary",)),
        name="dispatch",
    )(cnt_tile.reshape(-1), start_tile.reshape(-1), pad_start, pad_cnt, tail, loct, h_slab)


def _expert_kernel(be_ref, nv_ref, x_ref, wg_ref, bg_ref, wu_ref, bu_ref, wd_ref, bd_ref,
                   o_ref, wg_s, wu_s, wd_s, *, bm, sub):
    i = pl.program_id(0)
    prev = be_ref[jnp.maximum(i - 1, 0)]
    changed = (i == 0) | (be_ref[i] != prev)

    @pl.when(changed)
    def _():
        wg_s[...] = wg_ref[0].astype(BF16)
        wu_s[...] = wu_ref[0].astype(BF16)
        wd_s[...] = wd_ref[0].astype(BF16)

    @pl.when(i < nv_ref[0])
    def _():
        xb = _load_slab(x_ref, bm, sub).astype(BF16)
        g = _dot(xb, wg_s[...]) + bg_ref[0]
        u = _dot(xb, wu_s[...]) + bu_ref[0]
        g = jnp.minimum(g, SWIGLU_LIMIT)
        u = jnp.clip(u, -SWIGLU_LIMIT, SWIGLU_LIMIT)
        act = g * jax.nn.sigmoid(SWIGLU_ALPHA * g) * (u + 1.0)
        _store_slab(o_ref, _dot(act.astype(BF16), wd_s[...]) + bd_ref[0])

    @pl.when(i >= nv_ref[0])
    def _():
        o_ref[...] = jnp.zeros_like(o_ref)


def _experts(xs, blk_exp, n_valid, w_gate, b_gate, w_up, b_up, w_down, b_down, bm):
    ne, d, f = w_gate.shape
    sub = d // LANES
    n_blocks = xs.shape[0] // (bm * sub)
    wspec = lambda a, b: pl.BlockSpec((1, a, b), lambda i, be, nv: (be[i], 0, 0))
    grid_spec = pltpu.PrefetchScalarGridSpec(
        num_scalar_prefetch=2,
        grid=(n_blocks,),
        in_specs=[pl.BlockSpec((bm * sub, LANES), lambda i, be, nv: (jnp.clip(nv[0] - 1, 0, i), 0)),
                  wspec(d, f), wspec(1, f), wspec(d, f), wspec(1, f), wspec(f, d), wspec(1, d)],
        out_specs=pl.BlockSpec((bm * sub, LANES), lambda i, be, nv: (i, 0)),
        scratch_shapes=[pltpu.VMEM((d, f), BF16), pltpu.VMEM((d, f), BF16), pltpu.VMEM((f, d), BF16)],
    )
    return pl.pallas_call(
        functools.partial(_expert_kernel, bm=bm, sub=sub),
        grid_spec=grid_spec,
        out_shape=jax.ShapeDtypeStruct(xs.shape, F32),
        compiler_params=_cparams(("arbitrary",)),
        name="experts",
    )(blk_exp, n_valid, xs, w_gate, b_gate.reshape(ne, 1, f), w_up, b_up.reshape(ne, 1, f),
      w_down, b_down.reshape(ne, 1, d))


def _combine_kernel(cnt_ref, start_ref, yb_ref, loc_ref, tw_ref, x1_ref, g_ref, gt_ref, o_ref, buf, sem,
                    *, tm, sub, ne):
    i = pl.program_id(0)
    n_loc = tm * TOP_K

    def start_tile(tile, slot):
        def make(loc, glob, n):
            src = yb_ref.at[pl.ds(pl.multiple_of(glob * sub, sub), n * sub)]
            dst = buf.at[slot, pl.ds(pl.multiple_of(loc * sub, sub), n * sub)]
            pltpu.make_async_copy(src, dst, sem.at[slot]).start()

        _chunk_copies(cnt_ref, start_ref, tile, ne, tm, make)

    slot = i % 2

    @pl.when(i == 0)
    def _():
        start_tile(0, 0)

    @pl.when(i + 1 < pl.num_programs(0))
    def _():
        start_tile(i + 1, 1 - slot)

    pltpu.make_async_copy(yb_ref.at[pl.ds(0, n_loc * sub)], buf.at[slot], sem.at[slot]).wait()
    rows = _load_slab(buf.at[slot], n_loc, sub).astype(BF16)
    loc, tw = loc_ref[...], tw_ref[...]
    col_i = lax.broadcasted_iota(I32, (tm, n_loc), 1)
    w = jnp.zeros((tm, n_loc), F32)
    for k in range(TOP_K):
        w = jnp.where(col_i == loc[:, k:k + 1], tw[:, k:k + 1], w)
    ff = _dot(w.astype(BF16), rows)
    o_ref[...] = x1_ref[...] + gt_ref[0] * _rms(ff, g_ref[...])


def _combine(yb, loc, tw, cnt_tile, start_tile, x1, post_g, gt_f, seq, tm):
    n_tok, d = x1.shape
    sub = d // LANES
    nt = n_tok // tm
    ne = cnt_tile.shape[1]
    per_b = seq // tm
    loc_l = jnp.full((n_tok, LANES), -1, I32).at[:, :TOP_K].set(loc)
    grid_spec = pltpu.PrefetchScalarGridSpec(
        num_scalar_prefetch=2,
        grid=(nt,),
        in_specs=[pl.BlockSpec(memory_space=pl.ANY),
                  pl.BlockSpec((tm, LANES), lambda i, *_: (i, 0)),
                  pl.BlockSpec((tm, LANES), lambda i, *_: (i, 0)),
                  pl.BlockSpec((tm, d), lambda i, *_: (i, 0)),
                  pl.BlockSpec((1, d), lambda i, *_: (0, 0)),
                  pl.BlockSpec((1, 1, d), lambda i, *_: (i // per_b, 0, 0))],
        out_specs=pl.BlockSpec((tm, d), lambda i, *_: (i, 0)),
        scratch_shapes=[pltpu.VMEM((2, TOP_K * tm * sub, LANES), F32), pltpu.SemaphoreType.DMA((2,))],
    )
    return pl.pallas_call(
        functools.partial(_combine_kernel, tm=tm, sub=sub, ne=ne),
        grid_spec=grid_spec,
        out_shape=jax.ShapeDtypeStruct((n_tok, d), F32),
        compiler_params=_cparams(("arbitrary",)),
        name="combine",
    )(cnt_tile.reshape(-1), start_tile.reshape(-1), yb, loc_l, tw, x1, post_g.reshape(1, d), gt_f)


def kernel(x, c, positions, ada_w, ada_b, mix_pre_g, mix_post_g, ffn_pre_g, ffn_post_g, w_in, ssm_lam_re, ssm_lam_im, ssm_log_dt, ssm_b_re, ssm_b_im, ssm_c_re, ssm_c_im, ssm_d, ssm_w_glu, w_ssm_branch, w_attn_branch, w_out, router_w, router_b, w_gate, b_gate, w_up, b_up, w_down, b_down):
    bsz, seq, d = x.shape
    depth = ada_w.shape[0]
    n_tok = bsz * seq
    bm = EXPERT_ROWS
    sub = d // LANES
    xcur = x.reshape(n_tok, d)
    pos = positions.reshape(n_tok, 1).astype(I32)
    for l in range(depth):
        ada = _ada(c, ada_w[l], ada_b[l])
        sh_m, sc_m, gt_m, sh_f, sc_f, gt_f = [a.reshape(bsz, 1, d) for a in jnp.split(ada, 6, axis=-1)]

        u, q, k, v, gs, ga, km = _inproj(xcur, sc_m, sh_m, mix_pre_g[l].reshape(1, d), pos, w_in[l],
                                         seq, tm=min(512, seq))
        ys = _s5(u, ssm_lam_re[l], ssm_lam_im[l], ssm_log_dt[l], ssm_b_re[l], ssm_b_im[l],
                 ssm_c_re[l], ssm_c_im[l], bsz, seq)
        ya = _moba(q, k, v, km, bsz, seq)
        tm = ROUTE_TILE
        x1, h2, eid, tw, rank, before, cnt = _merge(
            ys, u, ya, gs, ga, xcur, ssm_d[l], ssm_w_glu[l], w_ssm_branch[l], w_attn_branch[l], w_out[l],
            mix_post_g[l], gt_m, ffn_pre_g[l], sc_f, sh_f, router_w[l], router_b[l], seq, tm=tm)

        ne = router_w.shape[-1]
        nt = n_tok // tm
        experts = jnp.arange(ne, dtype=I32)
        counts = cnt[0, :ne].astype(I32)
        padded = (counts + bm - 1) // bm * bm
        p_ends = jnp.cumsum(padded)
        p_starts = p_ends - padded
        before = before.reshape(nt, LANES)[:, :ne].astype(I32)
        cnt_tile = jnp.concatenate([before[1:], counts[None]], axis=0) - before
        start_tile = p_starts[None, :] + before
        off_tile = jnp.cumsum(cnt_tile, axis=1) - cnt_tile
        onehot = eid[:, :TOP_K].reshape(nt, tm, TOP_K, 1) == experts
        loc = (rank[:, :TOP_K] + jnp.sum(jnp.where(onehot, off_tile[:, None, None, :], 0), axis=-1).reshape(n_tok, TOP_K))
        n_blocks = (n_tok * TOP_K) // bm + ne
        n_rows = n_blocks * bm
        blk_start = jnp.arange(n_blocks, dtype=I32)[:, None] * bm
        blk_exp = jnp.minimum(jnp.sum((blk_start >= p_ends[None, :]).astype(I32), axis=1), ne - 1)
        n_valid = (p_ends[-1:] // bm).astype(I32)

        xs = _dispatch(h2, loc, cnt_tile, start_tile, p_starts + counts, padded - counts, p_ends[-1:],
                       n_rows, sub, bm, tm)
        yb = _experts(xs, blk_exp, n_valid, w_gate[l], b_gate[l], w_up[l], b_up[l], w_down[l], b_down[l], bm)
        xcur = _combine(yb, loc, tw, cnt_tile, start_tile, x1, ffn_post_g[l], gt_f, seq, tm)
    return xcur.reshape(bsz, seq, d).astype(x.dtype)
```

```python
import functools
import math

import jax
import jax.numpy as jnp
from jax import lax
from jax.experimental import pallas as pl
from jax.experimental.pallas import tpu as pltpu

F32 = jnp.float32
BF16 = jnp.bfloat16
I32 = jnp.int32

N_HEADS = 8
HEAD_DIM = 64
ROPE_THETA = 10000.0
MOBA_BLOCK = 256
MOBA_TOPK = 3
SSM_GROUP_SIZE = 16
SSM_GROUPS = 32
SSM_STATE = 64
N_EXPERTS = 32
TOP_K = 4
SWIGLU_ALPHA = 1.702
SWIGLU_LIMIT = 7.0
NORM_EPS = 1e-6
NEG_INF = -1e30

LANES = 128
SSM_CHUNK = 64
EXPERT_ROWS = 512
ROUTE_TILE = 512
VMEM_LIMIT = 56 * 1024 * 1024
HIGHEST = lax.Precision.HIGHEST
Q_SCALE = HEAD_DIM ** -0.5 * math.log2(math.e)


def _cparams(sem):
    return pltpu.CompilerParams(dimension_semantics=sem, vmem_limit_bytes=VMEM_LIMIT)


def _dot(a, b, **kw):
    return jnp.dot(a, b, preferred_element_type=F32, **kw)


def _dot_t(a, b, **kw):
    return lax.dot_general(a, b, (((1,), (1,)), ((), ())), preferred_element_type=F32, **kw)


def _ada_kernel(c_ref, w_ref, b_ref, o_ref):
    c = c_ref[...]
    cond = c * jax.nn.sigmoid(c)
    o_ref[...] = _dot(cond, w_ref[...], precision=HIGHEST) + b_ref[...]


def _ada(c, ada_w, ada_b):
    bsz, d = c.shape
    n = ada_w.shape[1]
    c8 = jnp.zeros((8, d), F32).at[:bsz].set(c)
    out = pl.pallas_call(
        _ada_kernel,
        grid=(n // d,),
        in_specs=[pl.BlockSpec((8, d), lambda j: (0, 0)),
                  pl.BlockSpec((d, d), lambda j: (0, j)),
                  pl.BlockSpec((1, d), lambda j: (0, j))],
        out_specs=pl.BlockSpec((8, d), lambda j: (0, j)),
        out_shape=jax.ShapeDtypeStruct((8, n), F32),
        compiler_params=_cparams(("arbitrary",)),
        name="ada",
    )(c8, ada_w, ada_b.reshape(1, n))
    return out[:bsz]


def _transpose_kernel(w_ref, o_ref):
    o_ref[...] = w_ref[...].T.astype(BF16)


def _inproj_kernel(x_ref, sc_ref, sh_ref, g_ref, pos_ref, invf_ref, post_ref, invft_ref, w_ref, wt_ref,
                   u_ref, qt_ref, k_ref, vt_ref, gs_ref, ga_ref, km_ref, *, ssm_w, attn_w, d_model):
    x = x_ref[...]
    ms = jnp.mean(x * x, axis=-1, keepdims=True)
    xn = x * lax.rsqrt(ms + NORM_EPS) * g_ref[...]
    h = (xn * (1.0 + sc_ref[0]) + sh_ref[0]).astype(BF16)

    def proj(lo, width):
        return _dot(h, w_ref[:, lo:lo + width])

    u_ref[...] = proj(0, ssm_w)

    ang = pos_ref[...].astype(F32) * invf_ref[...]
    reps = attn_w // LANES
    cos = jnp.concatenate([jnp.cos(ang)] * reps, axis=1)
    sin = jnp.concatenate([jnp.sin(ang)] * reps, axis=1)
    lane = lax.broadcasted_iota(I32, (1, attn_w), 1)
    first = (lane % HEAD_DIM) < (HEAD_DIM // 2)
    sin = jnp.where(first, -sin, sin)

    def rope(t):
        rot = jnp.where(first, pltpu.roll(t, attn_w - HEAD_DIM // 2, axis=1),
                        pltpu.roll(t, HEAD_DIM // 2, axis=1))
        return t * cos + rot * sin

    k = rope(proj(ssm_w + attn_w, attn_w))
    k_ref[...] = k.astype(BF16)
    tm = k.shape[0]
    nblk = tm // MOBA_BLOCK
    km_ref[0] = jnp.mean(k.reshape(nblk, MOBA_BLOCK, attn_w), axis=1)

    half = HEAD_DIM // 2
    ang_t = invft_ref[...] * post_ref[0].astype(F32)
    cos_t, sin_t = jnp.cos(ang_t), jnp.sin(ang_t)
    qt = _dot_t(wt_ref[0:attn_w, :], h).reshape(N_HEADS, 2, half, tm)
    t1, t2 = qt[:, 0], qt[:, 1]
    qt = jnp.stack([t1 * cos_t - t2 * sin_t, t2 * cos_t + t1 * sin_t], axis=1).reshape(attn_w, tm)
    qt_ref[...] = (qt * Q_SCALE).astype(BF16)
    vt = _dot_t(wt_ref[attn_w:2 * attn_w, :], h).astype(BF16)
    for j in range(nblk):
        vt_ref[j] = vt[:, j * MOBA_BLOCK:(j + 1) * MOBA_BLOCK]
    gs_ref[...] = jax.nn.sigmoid(proj(ssm_w + 3 * attn_w, d_model)).astype(BF16)
    ga_ref[...] = jax.nn.sigmoid(proj(ssm_w + 3 * attn_w + d_model, d_model)).astype(BF16)


def _inproj(x2, sc, sh, g, pos, w_in, seq, tm):
    n_tok, d = x2.shape
    attn_w = N_HEADS * HEAD_DIM
    ssm_w = SSM_GROUPS * SSM_GROUP_SIZE
    in_w = w_in.shape[1]
    half = HEAD_DIM // 2
    inv_freq = ROPE_THETA ** (-jnp.arange(half, dtype=F32) / half)
    invf = jnp.tile(inv_freq, LANES // half).reshape(1, LANES)
    nt = n_tok // tm
    per_b = seq // tm
    nblk = tm // MOBA_BLOCK
    bsz = n_tok // seq
    blk = MOBA_BLOCK
    assert ssm_w % attn_w == 0
    q_blk, v_blk = ssm_w // attn_w, ssm_w // attn_w + 2
    wt = pl.pallas_call(
        _transpose_kernel,
        grid=(2,),
        in_specs=[pl.BlockSpec((d, attn_w), lambda j: (0, q_blk + j * (v_blk - q_blk)))],
        out_specs=pl.BlockSpec((attn_w, d), lambda j: (j, 0)),
        out_shape=jax.ShapeDtypeStruct((2 * attn_w, d), BF16),
        compiler_params=_cparams(("arbitrary",)),
        name="wt",
    )(w_in)
    tok = lambda w: pl.BlockSpec((tm, w), lambda i: (i, 0))
    full = lambda a, b: pl.BlockSpec((a, b), lambda i: (0, 0))
    bvec = pl.BlockSpec((1, 1, d), lambda i: (i // per_b, 0, 0))
    outs = pl.pallas_call(
        functools.partial(_inproj_kernel, ssm_w=ssm_w, attn_w=attn_w, d_model=d),
        grid=(nt,),
        in_specs=[tok(d), bvec, bvec, full(1, d), tok(1), full(1, LANES),
                  pl.BlockSpec((1, 1, tm), lambda i: (i, 0, 0)), full(half, 1),
                  full(d, in_w), full(2 * attn_w, d)],
        out_specs=[tok(ssm_w),
                   pl.BlockSpec((None, attn_w, tm), lambda i: (i // per_b, 0, i % per_b)),
                   tok(attn_w),
                   pl.BlockSpec((None, nblk, attn_w, blk), lambda i: (i // per_b, i % per_b, 0, 0)),
                   tok(d), tok(d),
                   pl.BlockSpec((1, nblk, attn_w), lambda i: (i, 0, 0))],
        out_shape=[jax.ShapeDtypeStruct((n_tok, ssm_w), F32),
                   jax.ShapeDtypeStruct((bsz, attn_w, seq), BF16),
                   jax.ShapeDtypeStruct((n_tok, attn_w), BF16),
                   jax.ShapeDtypeStruct((bsz, seq // blk, attn_w, blk), BF16),
                   jax.ShapeDtypeStruct((n_tok, d), BF16),
                   jax.ShapeDtypeStruct((n_tok, d), BF16),
                   jax.ShapeDtypeStruct((nt, nblk, attn_w), F32)],
        compiler_params=_cparams(("arbitrary",)),
        name="inproj",
    )(x2, sc, sh, g, pos, invf, pos.reshape(nt, 1, tm), inv_freq.reshape(half, 1), w_in.astype(BF16), wt)
    return outs


def _s5_group(x, zr, zi, lr, li, ca, cb, ba, bb, t_ref, *, chunk, n_chunks):
    gs = SSM_GROUP_SIZE

    def powers(tau):
        mag = jnp.exp(tau * zr)
        return mag * jnp.cos(tau * zi), mag * jnp.sin(tau * zi)

    tau = lax.broadcasted_iota(I32, (chunk + 8, LANES), 0).astype(F32)
    e_re, e_im = powers(tau)
    lb_re, lb_im = e_re[1:2], e_im[1:2]
    den = lr * lr + li * li
    a, b = lb_re - 1.0, lb_im
    cf_re = (a * lr + b * li) / den
    cf_im = (b * lr - a * li) / den
    bri = cf_re * ba + cf_im * bb
    bri_sw = cf_re * bb - cf_im * ba

    cpow =(e_re[:chunk + 1, None, :] * ca[None] + e_im[:chunk + 1, None, :] * cb[None])
    cpow = cpow.reshape((chunk + 1) * gs, LANES)
    width = chunk * gs
    r = _dot_t(bri, cpow[:width], precision=HIGHEST)
    col = lax.broadcasted_iota(I32, (gs, width), 1)
    t_ref[0:gs, :] = r.astype(BF16)
    for j in range(1, chunk):
        shifted = jnp.where(col >= gs * j, pltpu.roll(r, gs * j, axis=1), 0.0)
        t_ref[gs * j:gs * (j + 1), :] = shifted.astype(BF16)

    y = _dot(x, t_ref[...])

    tau_rev = (chunk - 1) - lax.broadcasted_iota(I32, (chunk, LANES), 0)
    r_re, r_im = powers(tau_rev.astype(F32))
    bst = r_re[:, None, :] * bri[None] + r_im[:, None, :] * bri_sw[None]
    bst = bst.reshape(width, LANES).astype(BF16)
    s = _dot(x, bst)

    rows = s.shape[0]
    n_idx = lax.broadcasted_iota(I32, (rows, LANES), 0) % n_chunks
    lane = lax.broadcasted_iota(I32, (1, LANES), 1)
    half = LANES // 2

    def cmul(v, p_re, p_im):
        return v * p_re + pltpu.roll(v, half, axis=1) * jnp.where(lane < half, -p_im, p_im)

    sh = 1
    while sh < n_chunks:
        p_re, p_im = powers(jnp.full((1, LANES), float(chunk * sh), F32))
        prev = jnp.where(n_idx >= sh, pltpu.roll(s, sh, axis=0), 0.0)
        s = s + cmul(prev, p_re, p_im)
        sh *= 2
    s_in = jnp.where(n_idx >= 1, pltpu.roll(s, 1, axis=0), 0.0)
    return y + _dot_t(s_in.astype(BF16), cpow[gs:gs + width].astype(BF16))


def _s5_kernel(u_ref, sel_ref, zr_ref, zi_ref, lr_ref, li_ref, ca_ref, cb_ref, ba_ref, bb_ref,
               y_ref, x_scr, y_scr, t_ref, *, chunk, n_chunks, rows):
    gpt = LANES // SSM_GROUP_SIZE

    def plane(j):
        return pl.ds(j, rows, stride=chunk)

    for jt in range(chunk // gpt):
        planes = jnp.concatenate([u_ref[plane(jt * gpt + jj), :] for jj in range(gpt)], axis=1)
        grouped = _dot(planes.astype(BF16), sel_ref[...])
        for g in range(gpt):
            x_scr[g, :, jt * LANES:(jt + 1) * LANES] = grouped[:, g * LANES:(g + 1) * LANES].astype(BF16)

    def per_group(g, _):
        y = _s5_group(x_scr[g], zr_ref[g], zi_ref[g], lr_ref[g], li_ref[g], ca_ref[g], cb_ref[g], ba_ref[g],
                      bb_ref[g], t_ref, chunk=chunk, n_chunks=n_chunks)
        y_scr[g] = y.astype(BF16)
        return 0

    lax.fori_loop(0, gpt, per_group, 0)

    for jt in range(chunk // gpt):
        grouped = jnp.concatenate([y_scr[g, :, jt * LANES:(jt + 1) * LANES] for g in range(gpt)], axis=1)
        planes = _dot(grouped, sel_ref[...])
        for jj in range(gpt):
            y_ref[plane(jt * gpt + jj), :] = planes[:, jj * LANES:(jj + 1) * LANES]


def _s5(u, lam_re, lam_im, log_dt, b_re, b_im, c_re, c_im, bsz, seq):
    g_n, gs, p = SSM_GROUPS, SSM_GROUP_SIZE, SSM_STATE
    chunk = SSM_CHUNK
    n_chunks = seq // chunk
    rows = bsz * n_chunks
    width = chunk * gs
    gpt = LANES // gs
    assert chunk % gpt == 0 and g_n % gpt == 0 and 2 * p == LANES
    dt = jnp.exp(log_dt.astype(F32))[:, None]
    dup = lambda a: jnp.concatenate([a, a], axis=-1).reshape(g_n, 1, 2 * p)
    zr, zi = dup(lam_re * dt), dup(lam_im * dt)
    lr, li = dup(lam_re), dup(lam_im)
    bt_re, bt_im = b_re.transpose(0, 2, 1), b_im.transpose(0, 2, 1)
    ba = jnp.concatenate([bt_re, bt_im], axis=-1)
    bb = jnp.concatenate([-bt_im, bt_re], axis=-1)
    ca = jnp.concatenate([c_re, -c_im], axis=-1)
    cb = jnp.concatenate([-c_im, -c_re], axis=-1)
    idx = jnp.arange(gpt * LANES, dtype=I32)
    swapped = ((idx // gs) % gpt) * LANES + (idx // LANES) * gs + idx % gs
    sel = (swapped[:, None] == idx[None, :]).astype(BF16)
    tile = pl.BlockSpec((bsz * seq, LANES), lambda t: (0, t))
    vec = pl.BlockSpec((gpt, 1, 2 * p), lambda t: (t, 0, 0))
    mat = pl.BlockSpec((gpt, gs, 2 * p), lambda t: (t, 0, 0))
    return pl.pallas_call(
        functools.partial(_s5_kernel, chunk=chunk, n_chunks=n_chunks, rows=rows),
        grid=(g_n // gpt,),
        in_specs=[tile, pl.BlockSpec((gpt * LANES, gpt * LANES), lambda t: (0, 0)),
                  vec, vec, vec, vec, mat, mat, mat, mat],
        out_specs=tile,
        out_shape=jax.ShapeDtypeStruct((bsz * seq, g_n * gs), F32),
        scratch_shapes=[pltpu.VMEM((gpt, rows, width), BF16), pltpu.VMEM((gpt, rows, width), BF16),
                        pltpu.VMEM((width, width), BF16)],
        compiler_params=_cparams(("arbitrary",)),
        name="s5",
    )(u, sel, zr, zi, lr, li, ca, cb, ba, bb)


def _moba_kernel(qt_ref, k_ref, vt_ref, km_ref, o_ref, s_a, s_b, p_a, p_b, *, nb, nbp, qblocks):
    blk = MOBA_BLOCK
    group = 2
    keys = group * blk
    nq = qblocks * blk
    first = qblocks * pl.program_id(2)
    qt = qt_ref[...]
    km = km_ref[...]
    km_hi = km.astype(BF16)
    km_lo = (km - km_hi.astype(F32)).astype(BF16)
    dim_i = lax.broadcasted_iota(I32, (LANES, nq), 0)
    blk_i = lax.broadcasted_iota(I32, (nbp, nq), 0)
    blk_f = blk_i.astype(F32)
    own = first + lax.broadcasted_iota(I32, (nbp, nq), 1) // blk
    valid = blk_i < own

    q_aug = []
    for h in range(2):
        qh = jnp.where((dim_i >= h * HEAD_DIM) & (dim_i < (h + 1) * HEAD_DIM), qt, jnp.zeros_like(qt))
        g = jnp.where(valid, _dot(km_hi, qh) + _dot(km_lo, qh), NEG_INF)
        sel = jnp.zeros((nbp, nq), jnp.bool_)
        for _ in range(MOBA_TOPK):
            m = jnp.max(g, axis=0, keepdims=True)
            idx = jnp.min(jnp.where(g == m, blk_f, float(nbp)), axis=0, keepdims=True)
            pick = blk_f == idx
            sel = sel | pick
            g = jnp.where(pick, -jnp.inf, g)
        ok = (sel & valid) | (blk_i == own)
        parts = [qh, jnp.where(ok, 0.0, NEG_INF).astype(BF16)]
        if nbp < LANES:
            parts.append(jnp.full((LANES - nbp, nq), NEG_INF, BF16))
        q_aug.append(jnp.concatenate(parts, axis=0))

    def scores(grp, s_ref, causal=False):
        kb0 = grp * group
        k_cat = k_ref[pl.ds(jnp.minimum(kb0, nb - group), group)].reshape(keys, LANES)
        blk_id = kb0 + lax.broadcasted_iota(I32, (keys, LANES), 0) // blk
        onehot = jnp.where(lax.broadcasted_iota(I32, (keys, LANES), 1) == blk_id, 1.0, 0.0).astype(BF16)
        k_aug = jnp.concatenate([k_cat, onehot], axis=1)
        mx = []
        for h in range(2):
            s = _dot(k_aug, q_aug[h])
            if causal:
                key_pos = (kb0 - first) * blk + lax.broadcasted_iota(I32, (keys, nq), 0)
                s = jnp.where(key_pos <= lax.broadcasted_iota(I32, (keys, nq), 1), s, NEG_INF)
            s_ref[h] = s
            mx.append(jnp.max(s, axis=0, keepdims=True))
        return tuple(mx)

    def softmax(s_ref, mx, p_ref, st):
        out = []
        for h in range(2):
            m_i, _, l_i, acc = st[h]
            m_new = jnp.maximum(m_i, mx[h])
            p_ref[h] = jnp.exp2(s_ref[h] - m_new).astype(BF16)
            out.append((m_new, jnp.exp2(m_i - m_new), l_i, acc))
        return tuple(out)

    def values(grp, p_ref, st):
        out = []
        for h in range(2):
            m_i, alpha, l_i, acc = st[h]
            blocks = [vt_ref[jnp.clip(grp * group + j, 0, nb - 1), h * HEAD_DIM:(h + 1) * HEAD_DIM, :]
                      for j in range(group)]
            ones = jnp.ones((16, keys), BF16)
            pv = _dot(jnp.concatenate([jnp.concatenate(blocks, axis=1), ones], axis=0), p_ref[h])
            out.append((m_i, alpha, alpha * l_i + pv[HEAD_DIM:HEAD_DIM + 1], alpha * acc + pv[:HEAD_DIM]))
        return tuple(out)

    mx_first = scores(0, s_a)
    p_b[...] = jnp.zeros_like(p_b)

    def body(t, carry):
        st, mx_a = carry
        mx_b = scores(2 * t + 1, s_b)
        st = values(2 * t - 1, p_b, st)
        st = softmax(s_a, mx_a, p_a, st)
        mx_a = scores(2 * t + 2, s_a)
        st = values(2 * t, p_a, st)
        return softmax(s_b, mx_b, p_b, st), mx_a

    init = (jnp.full((1, nq), -jnp.inf, F32), jnp.ones((1, nq), F32), jnp.zeros((1, nq), F32),
            jnp.zeros((HEAD_DIM, nq), F32))
    g0 = first // group
    st, _ = lax.fori_loop(0, g0 // 2, body, ((init, init), mx_first))
    for pair in range(qblocks // (2 * group)):
        ga = g0 + 2 * pair
        mx_a = scores(ga, s_a, causal=True)
        st = values(ga - 1, p_b, st)
        mx_b = scores(ga + 1, s_b, causal=True)
        st = softmax(s_a, mx_a, p_a, st)
        st = values(ga, p_a, st)
        st = softmax(s_b, mx_b, p_b, st)
    st = values(g0 + qblocks // group - 1, p_b, st)
    both = jnp.concatenate([st[h][3] / st[h][2] for h in range(2)], axis=0)
    o_ref[...] = both.T.astype(BF16)


def _moba(qt, k, vt4, km, bsz, seq):
    n_tok, attn_w = k.shape
    blk = MOBA_BLOCK
    nb = seq // blk
    group, qblocks = 2, 4
    nbp = -(-nb // 16) * 16
    assert nb % qblocks == 0 and qblocks % (2 * group) == 0 and nbp <= LANES
    hp = attn_w // LANES
    km_pad = jnp.zeros((bsz, nbp, attn_w), F32).at[:, :nb].set(km.reshape(bsz, nb, attn_w))
    k4 = k.reshape(bsz, nb, blk, attn_w)
    steps = nb // qblocks
    return pl.pallas_call(
        functools.partial(_moba_kernel, nb=nb, nbp=nbp, qblocks=qblocks),
        grid=(bsz, hp, steps),
        in_specs=[pl.BlockSpec((None, LANES, qblocks * blk), lambda b, p, i: (b, p, i)),
                  pl.BlockSpec((None, nb, blk, LANES), lambda b, p, i: (b, 0, 0, p)),
                  pl.BlockSpec((None, nb, LANES, blk), lambda b, p, i: (b, 0, p, 0)),
                  pl.BlockSpec((None, nbp, LANES), lambda b, p, i: (b, 0, p))],
        out_specs=pl.BlockSpec((qblocks * blk, LANES), lambda b, p, i: (b * steps + i, p)),
        out_shape=jax.ShapeDtypeStruct((n_tok, attn_w), BF16),
        scratch_shapes=([pltpu.VMEM((2, group * blk, qblocks * blk), F32)] * 2
                        + [pltpu.VMEM((2, group * blk, qblocks * blk), BF16)] * 2),
        compiler_params=_cparams(("arbitrary", "arbitrary", "arbitrary")),
        name="moba",
    )(qt, k4, vt4, km_pad)


def _rms(t, g):
    return t * lax.rsqrt(jnp.mean(t * t, axis=-1, keepdims=True) + NORM_EPS) * g


def _store_slab(ref, val, base=0):
    rows, d = val.shape
    sub = d // LANES
    for s in range(sub):
        ref[pl.ds(base + s, rows, stride=sub), :] = val[:, s * LANES:(s + 1) * LANES]


def _load_slab(ref, rows, sub, base=0):
    return jnp.concatenate([ref[pl.ds(base + s, rows, stride=sub), :] for s in range(sub)], axis=1)


def _merge_kernel(ys_ref, u_ref, ya_ref, gs_ref, ga_ref, x_ref, d_ref, wglu_ref, wsb_ref, wab_ref,
                  wout_ref, postg_ref, gtm_ref, preg_ref, scf_ref, shf_ref, rw_ref, rb_ref,
                  x1_ref, h2_ref, eid_ref, tw_ref, rank_ref, before_ref, cnt_ref, run_ref):
    @pl.when(pl.program_id(0) == 0)
    def _():
        run_ref[...] = jnp.zeros_like(run_ref)

    y = ys_ref[...] + d_ref[...] * u_ref[...].astype(F32)
    y = jax.nn.gelu(y)
    y = y * jax.nn.sigmoid(_dot(y.astype(BF16), wglu_ref[...]))
    bs = _dot(y.astype(BF16), wsb_ref[...])
    ba = _dot(ya_ref[...], wab_ref[...])
    merged = gs_ref[...].astype(F32) * bs + ga_ref[...].astype(F32) * ba
    mix = _dot(merged.astype(BF16), wout_ref[...])
    x1 = x_ref[...] + gtm_ref[0] * _rms(mix, postg_ref[...])
    x1_ref[...] = x1
    h2 = _rms(x1, preg_ref[...]) * (1.0 + scf_ref[0]) + shf_ref[0]
    _store_slab(h2_ref, h2)

    h_hi = h2.astype(BF16)
    h_lo = (h2 - h_hi.astype(F32)).astype(BF16)
    logits = _dot(jnp.concatenate([h_hi, h_hi, h_lo], axis=1), rw_ref[...]) + rb_ref[...]
    tm = logits.shape[0]
    lane = lax.broadcasted_iota(I32, (tm, LANES), 1)
    lane_f = lane.astype(F32)
    g = logits
    vals, picks, idxs = [], [], []
    for _ in range(TOP_K):
        m = jnp.max(g, axis=1, keepdims=True)
        idx = jnp.min(jnp.where(g == m, lane_f, float(LANES)), axis=1, keepdims=True)
        pick = lane_f == idx
        vals.append(m)
        idxs.append(idx.astype(I32))
        picks.append(pick)
        g = jnp.where(pick, -jnp.inf, g)
    exps = [jnp.exp(v - vals[0]) for v in vals]
    tot = exps[0] + exps[1] + exps[2] + exps[3]

    onehot = jnp.where(picks[0] | picks[1] | picks[2] | picks[3], 1.0, 0.0)
    r_i = lax.broadcasted_iota(I32, (tm, tm), 0)
    c_i = lax.broadcasted_iota(I32, (tm, tm), 1)
    tri = jnp.where(c_i < r_i, 1.0, 0.0).astype(BF16)
    rank_full = _dot(tri, onehot.astype(BF16))
    before_ref[0] = run_ref[...]
    run_ref[...] = run_ref[...] + jnp.sum(onehot, axis=0, keepdims=True)
    cnt_ref[...] = run_ref[...]

    eid = jnp.zeros((tm, LANES), I32)
    tw = jnp.zeros((tm, LANES), F32)
    rk = jnp.zeros((tm, LANES), F32)
    for r in range(TOP_K):
        eid = jnp.where(lane == r, idxs[r], eid)
        tw = jnp.where(lane == r, exps[r] / tot, tw)
        pos = jnp.sum(jnp.where(picks[r], rank_full, 0.0), axis=1, keepdims=True)
        rk = jnp.where(lane == r, pos, rk)
    eid_ref[...] = eid
    tw_ref[...] = tw
    rank_ref[...] = rk.astype(I32)


def _merge(ys, u, ya, gs, ga, x2, ssm_d, w_glu, w_sb, w_ab, w_out, post_g, gt_m, pre_g, sc_f, sh_f,
           router_w, router_b, seq, tm):
    n_tok, d = x2.shape
    sw = ys.shape[1]
    aw = ya.shape[1]
    ne = router_w.shape[1]
    per_b = seq // tm
    rw = jnp.zeros((d, LANES), F32).at[:, :ne].set(router_w)
    rw_hi = rw.astype(BF16)
    rw_lo = (rw - rw_hi.astype(F32)).astype(BF16)
    rw = jnp.concatenate([rw_hi, rw_lo, rw_hi], axis=0)
    rb = jnp.full((1, LANES), NEG_INF, F32).at[0, :ne].set(router_b)
    tok = lambda w: pl.BlockSpec((tm, w), lambda i: (i, 0))
    full = lambda a, b: pl.BlockSpec((a, b), lambda i: (0, 0))
    bvec = pl.BlockSpec((1, 1, d), lambda i: (i // per_b, 0, 0))
    return pl.pallas_call(
        _merge_kernel,
        grid=(n_tok // tm,),
        in_specs=[tok(sw), tok(sw), tok(aw), tok(d), tok(d), tok(d),
                  full(1, sw), full(sw, sw), full(sw, d), full(aw, d), full(d, d),
                  full(1, d), bvec, full(1, d), bvec, bvec, full(3 * d, LANES), full(1, LANES)],
        out_specs=[tok(d), pl.BlockSpec((tm * (d // LANES), LANES), lambda i: (i, 0)),
                   tok(LANES), tok(LANES), tok(LANES),
                   pl.BlockSpec((1, 1, LANES), lambda i: (i, 0, 0)), full(1, LANES)],
        out_shape=[jax.ShapeDtypeStruct((n_tok, d), F32),
                   jax.ShapeDtypeStruct((n_tok * (d // LANES), LANES), F32),
                   jax.ShapeDtypeStruct((n_tok, LANES), I32),
                   jax.ShapeDtypeStruct((n_tok, LANES), F32),
                   jax.ShapeDtypeStruct((n_tok, LANES), I32),
                   jax.ShapeDtypeStruct((n_tok // tm, 1, LANES), F32),
                   jax.ShapeDtypeStruct((1, LANES), F32)],
        scratch_shapes=[pltpu.VMEM((1, LANES), F32)],
        compiler_params=_cparams(("arbitrary",)),
        name="merge",
    )(ys, u, ya, gs, ga, x2, ssm_d.reshape(1, sw), w_glu.astype(BF16), w_sb.astype(BF16),
      w_ab.astype(BF16), w_out.astype(BF16), post_g.reshape(1, d), gt_m, pre_g.reshape(1, d),
      sc_f, sh_f, rw, rb)


def _chunk_copies(cnt_ref, start_ref, tile, ne, max_rows, make):
    def per_expert(e, off):
        c, start = cnt_ref[tile * ne + e], start_ref[tile * ne + e]
        for b in range(max_rows.bit_length()):
            n = 1 << b

            @pl.when((c >> b) & 1 == 1)
            def _():
                make(off + (c & (n - 1)), start + (c & (n - 1)), n)
        return off + c

    lax.fori_loop(0, ne, per_expert, 0)


def _dispatch_kernel(cnt_ref, start_ref, pad_start_ref, pad_cnt_ref, tail_ref, loct_ref, h_ref, xs_ref,
                     buf, zero_ref, sem, zsem, *, tm, sub, bm, n_rows, ne):
    i = pl.program_id(0)
    last = pl.num_programs(0) - 1
    slot = i % 2
    n_loc = tm * TOP_K

    def row(ref, r, n=1):
        return ref.at[pl.ds(pl.multiple_of(r * sub, sub), n * sub)]

    def wait_slot(s):
        pltpu.make_async_copy(buf.at[s], row(xs_ref, 0, n_loc), sem.at[s]).wait()

    def zero_copies(do):
        def per_expert(e, _):
            start, cnt = pad_start_ref[e], pad_cnt_ref[e]
            for b in range(bm.bit_length() - 1):
                n = 1 << b

                @pl.when((cnt >> b) & 1 == 1)
                def _():
                    do(pltpu.make_async_copy(row(zero_ref, 0, n), row(xs_ref, start + (cnt & (n - 1)), n), zsem))
            return 0

        lax.fori_loop(0, ne, per_expert, 0)

        def per_block(j, _):
            r = tail_ref[0] + j * bm

            @pl.when(r < n_rows)
            def _():
                do(pltpu.make_async_copy(row(zero_ref, 0, bm), row(xs_ref, r, bm), zsem))
            return 0

        lax.fori_loop(0, ne, per_block, 0)

    @pl.when(i == 0)
    def _():
        zero_ref[...] = jnp.zeros_like(zero_ref)
        zero_copies(lambda cp: cp.start())

    @pl.when(i >= 2)
    def _():
        wait_slot(slot)

    loct = loct_ref[0]
    row_i = lax.broadcasted_iota(I32, (n_loc, tm), 0)
    place = row_i == loct[0:1, :]
    for k in range(1, TOP_K):
        place = place | (row_i == loct[k:k + 1, :])
    grouped = _dot(jnp.where(place, 1.0, 0.0).astype(BF16), _load_slab(h_ref, tm, sub).astype(BF16))
    _store_slab(buf.at[slot], grouped)
    _chunk_copies(cnt_ref, start_ref, i, ne, tm,
                  lambda loc, glob, n: pltpu.make_async_copy(row(buf.at[slot], loc, n), row(xs_ref, glob, n),
                                                             sem.at[slot]).start())

    @pl.when(i == last)
    def _():
        wait_slot(slot)

    @pl.when((i == last) & (i >= 1))
    def _():
        wait_slot(1 - slot)

    @pl.when(i == 0)
    def _():
        zero_copies(lambda cp: cp.wait())


def _dispatch(h_slab, loc, cnt_tile, start_tile, pad_start, pad_cnt, tail, n_rows, sub, bm, tm):
    n_tok = loc.shape[0]
    nt = n_tok // tm
    ne = pad_start.shape[0]
    loct = jnp.full((nt, 8, tm), -1, I32).at[:, :TOP_K].set(loc.reshape(nt, tm, TOP_K).transpose(0, 2, 1))
    grid_spec = pltpu.PrefetchScalarGridSpec(
        num_scalar_prefetch=5,
        grid=(nt,),
        in_specs=[pl.BlockSpec((1, 8, tm), lambda i, *_: (i, 0, 0)),
                  pl.BlockSpec((tm * sub, LANES), lambda i, *_: (i, 0))],
        out_specs=pl.BlockSpec(memory_space=pl.ANY),
        scratch_shapes=[pltpu.VMEM((2, TOP_K * tm * sub, LANES), F32), pltpu.VMEM((bm * sub, LANES), F32),
                        pltpu.SemaphoreType.DMA((2,)), pltpu.SemaphoreType.DMA(())],
    )
    return pl.pallas_call(
        functools.partial(_dispatch_kernel, tm=tm, sub=sub, bm=bm, n_rows=n_rows, ne=ne),
        grid_spec=grid_spec,
        out_shape=jax.ShapeDtypeStruct((n_rows * sub, LANES), F32),
        compiler_params=_cparams(("arbitrary",)),
        name="dispatch",
    )(cnt_tile.reshape(-1), start_tile.reshape(-1), pad_start, pad_cnt, tail, loct, h_slab)


def _expert_kernel(be_ref, nv_ref, x_ref, wg_ref, bg_ref, wu_ref, bu_ref, wd_ref, bd_ref,
                   o_ref, wg_s, wu_s, wd_s, *, bm, sub):
    i = pl.program_id(0)
    prev = be_ref[jnp.maximum(i - 1, 0)]
    changed = (i == 0) | (be_ref[i] != prev)

    @pl.when(changed)
    def _():
        wg_s[...] = wg_ref[0].astype(BF16)
        wu_s[...] = wu_ref[0].astype(BF16)
        wd_s[...] = wd_ref[0].astype(BF16)

    @pl.when(i < nv_ref[0])
    def _():
        xb = _load_slab(x_ref, bm, sub).astype(BF16)
        g = _dot(xb, wg_s[...]) + bg_ref[0]
        u = _dot(xb, wu_s[...]) + bu_ref[0]
        g = jnp.minimum(g, SWIGLU_LIMIT)
        u = jnp.clip(u, -SWIGLU_LIMIT, SWIGLU_LIMIT)
        act = g * jax.nn.sigmoid(SWIGLU_ALPHA * g) * (u + 1.0)
        _store_slab(o_ref, _dot(act.astype(BF16), wd_s[...]) + bd_ref[0])

    @pl.when(i >= nv_ref[0])
    def _():
        o_ref[...] = jnp.zeros_like(o_ref)


def _experts(xs, blk_exp, n_valid, w_gate, b_gate, w_up, b_up, w_down, b_down, bm):
    ne, d, f = w_gate.shape
    sub = d // LANES
    n_blocks = xs.shape[0] // (bm * sub)
    wspec = lambda a, b: pl.BlockSpec((1, a, b), lambda i, be, nv: (be[i], 0, 0))
    grid_spec = pltpu.PrefetchScalarGridSpec(
        num_scalar_prefetch=2,
        grid=(n_blocks,),
        in_specs=[pl.BlockSpec((bm * sub, LANES), lambda i, be, nv: (jnp.clip(nv[0] - 1, 0, i), 0)),
                  wspec(d, f), wspec(1, f), wspec(d, f), wspec(1, f), wspec(f, d), wspec(1, d)],
        out_specs=pl.BlockSpec((bm * sub, LANES), lambda i, be, nv: (i, 0)),
        scratch_shapes=[pltpu.VMEM((d, f), BF16), pltpu.VMEM((d, f), BF16), pltpu.VMEM((f, d), BF16)],
    )
    return pl.pallas_call(
        functools.partial(_expert_kernel, bm=bm, sub=sub),
        grid_spec=grid_spec,
        out_shape=jax.ShapeDtypeStruct(xs.shape, F32),
        compiler_params=_cparams(("arbitrary",)),
        name="experts",
    )(blk_exp, n_valid, xs, w_gate, b_gate.reshape(ne, 1, f), w_up, b_up.reshape(ne, 1, f),
      w_down, b_down.reshape(ne, 1, d))


def _combine_kernel(cnt_ref, start_ref, yb_ref, loc_ref, tw_ref, x1_ref, g_ref, gt_ref, o_ref, buf, sem,
                    *, tm, sub, ne):
    i = pl.program_id(0)
    n_loc = tm * TOP_K

    def start_tile(tile, slot):
        def make(loc, glob, n):
            src = yb_ref.at[pl.ds(pl.multiple_of(glob * sub, sub), n * sub)]
            dst = buf.at[slot, pl.ds(pl.multiple_of(loc * sub, sub), n * sub)]
            pltpu.make_async_copy(src, dst, sem.at[slot]).start()

        _chunk_copies(cnt_ref, start_ref, tile, ne, tm, make)

    slot = i % 2

    @pl.when(i == 0)
    def _():
        start_tile(0, 0)

    @pl.when(i + 1 < pl.num_programs(0))
    def _():
        start_tile(i + 1, 1 - slot)

    pltpu.make_async_copy(yb_ref.at[pl.ds(0, n_loc * sub)], buf.at[slot], sem.at[slot]).wait()
    rows = _load_slab(buf.at[slot], n_loc, sub).astype(BF16)
    loc, tw = loc_ref[...], tw_ref[...]
    col_i = lax.broadcasted_iota(I32, (tm, n_loc), 1)
    w = jnp.zeros((tm, n_loc), F32)
    for k in range(TOP_K):
        w = jnp.where(col_i == loc[:, k:k + 1], tw[:, k:k + 1], w)
    ff = _dot(w.astype(BF16), rows)
    o_ref[...] = x1_ref[...] + gt_ref[0] * _rms(ff, g_ref[...])


def _combine(yb, loc, tw, cnt_tile, start_tile, x1, post_g, gt_f, seq, tm):
    n_tok, d = x1.shape
    sub = d // LANES
    nt = n_tok // tm
    ne = cnt_tile.shape[1]
    per_b = seq // tm
    loc_l = jnp.full((n_tok, LANES), -1, I32).at[:, :TOP_K].set(loc)
    grid_spec = pltpu.PrefetchScalarGridSpec(
        num_scalar_prefetch=2,
        grid=(nt,),
        in_specs=[pl.BlockSpec(memory_space=pl.ANY),
                  pl.BlockSpec((tm, LANES), lambda i, *_: (i, 0)),
                  pl.BlockSpec((tm, LANES), lambda i, *_: (i, 0)),
                  pl.BlockSpec((tm, d), lambda i, *_: (i, 0)),
                  pl.BlockSpec((1, d), lambda i, *_: (0, 0)),
                  pl.BlockSpec((1, 1, d), lambda i, *_: (i // per_b, 0, 0))],
        out_specs=pl.BlockSpec((tm, d), lambda i, *_: (i, 0)),
        scratch_shapes=[pltpu.VMEM((2, TOP_K * tm * sub, LANES), F32), pltpu.SemaphoreType.DMA((2,))],
    )
    return pl.pallas_call(
        functools.partial(_combine_kernel, tm=tm, sub=sub, ne=ne),
        grid_spec=grid_spec,
        out_shape=jax.ShapeDtypeStruct((n_tok, d), F32),
        compiler_params=_cparams(("arbitrary",)),
        name="combine",
    )(cnt_tile.reshape(-1), start_tile.reshape(-1), yb, loc_l, tw, x1, post_g.reshape(1, d), gt_f)


def kernel(x, c, positions, ada_w, ada_b, mix_pre_g, mix_post_g, ffn_pre_g, ffn_post_g, w_in, ssm_lam_re, ssm_lam_im, ssm_log_dt, ssm_b_re, ssm_b_im, ssm_c_re, ssm_c_im, ssm_d, ssm_w_glu, w_ssm_branch, w_attn_branch, w_out, router_w, router_b, w_gate, b_gate, w_up, b_up, w_down, b_down):
    bsz, seq, d = x.shape
    depth = ada_w.shape[0]
    n_tok = bsz * seq
    bm = EXPERT_ROWS
    sub = d // LANES
    xcur = x.reshape(n_tok, d)
    pos = positions.reshape(n_tok, 1).astype(I32)
    for l in range(depth):
        ada = _ada(c, ada_w[l], ada_b[l])
        sh_m, sc_m, gt_m, sh_f, sc_f, gt_f = [a.reshape(bsz, 1, d) for a in jnp.split(ada, 6, axis=-1)]

        u, q, k, v, gs, ga, km = _inproj(xcur, sc_m, sh_m, mix_pre_g[l].reshape(1, d), pos, w_in[l],
                                         seq, tm=min(512, seq))
        ys = _s5(u, ssm_lam_re[l], ssm_lam_im[l], ssm_log_dt[l], ssm_b_re[l], ssm_b_im[l],
                 ssm_c_re[l], ssm_c_im[l], bsz, seq)
        ya = _moba(q, k, v, km, bsz, seq)
        tm = ROUTE_TILE
        x1, h2, eid, tw, rank, before, cnt = _merge(
            ys, u, ya, gs, ga, xcur, ssm_d[l], ssm_w_glu[l], w_ssm_branch[l], w_attn_branch[l], w_out[l],
            mix_post_g[l], gt_m, ffn_pre_g[l], sc_f, sh_f, router_w[l], router_b[l], seq, tm=tm)

        ne = router_w.shape[-1]
        nt = n_tok // tm
        experts = jnp.arange(ne, dtype=I32)
        counts = cnt[0, :ne].astype(I32)
        padded = (counts + bm - 1) // bm * bm
        p_ends = jnp.cumsum(padded)
        p_starts = p_ends - padded
        before = before.reshape(nt, LANES)[:, :ne].astype(I32)
        cnt_tile = jnp.concatenate([before[1:], counts[None]], axis=0) - before
        start_tile = p_starts[None, :] + before
        off_tile = jnp.cumsum(cnt_tile, axis=1) - cnt_tile
        onehot = eid[:, :TOP_K].reshape(nt, tm, TOP_K, 1) == experts
        loc = (rank[:, :TOP_K] + jnp.sum(jnp.where(onehot, off_tile[:, None, None, :], 0), axis=-1).reshape(n_tok, TOP_K))
        n_blocks = (n_tok * TOP_K) // bm + ne
        n_rows = n_blocks * bm
        blk_start = jnp.arange(n_blocks, dtype=I32)[:, None] * bm
        blk_exp = jnp.minimum(jnp.sum((blk_start >= p_ends[None, :]).astype(I32), axis=1), ne - 1)
        n_valid = (p_ends[-1:] // bm).astype(I32)

        xs = _dispatch(h2, loc, cnt_tile, start_tile, p_starts + counts, padded - counts, p_ends[-1:],
                       n_rows, sub, bm, tm)
        yb = _experts(xs, blk_exp, n_valid, w_gate[l], b_gate[l], w_up[l], b_up[l], w_down[l], b_down[l], bm)
        xcur = _combine(yb, loc, tw, cnt_tile, start_tile, x1, ffn_post_g[l], gt_f, seq, tm)
    return xcur.reshape(bsz, seq, d).astype(x.dtype)
```

```python
import functools
import math

import jax
import jax.numpy as jnp
from jax import lax
from jax.experimental import pallas as pl
from jax.experimental.pallas import tpu as pltpu

F32 = jnp.float32
BF16 = jnp.bfloat16
I32 = jnp.int32

N_HEADS = 8
HEAD_DIM = 64
ROPE_THETA = 10000.0
MOBA_BLOCK = 256
MOBA_TOPK = 3
SSM_GROUP_SIZE = 16
SSM_GROUPS = 32
SSM_STATE = 64
N_EXPERTS = 32
TOP_K = 4
SWIGLU_ALPHA = 1.702
SWIGLU_LIMIT = 7.0
NORM_EPS = 1e-6
NEG_INF = -1e30

LANES = 128
SSM_CHUNK = 64
EXPERT_ROWS = 512
ROUTE_TILE = 512
VMEM_LIMIT = 56 * 1024 * 1024
HIGHEST = lax.Precision.HIGHEST
Q_SCALE = HEAD_DIM ** -0.5 * math.log2(math.e)


def _cparams(sem):
    return pltpu.CompilerParams(dimension_semantics=sem, vmem_limit_bytes=VMEM_LIMIT)


def _dot(a, b, **kw):
    return jnp.dot(a, b, preferred_element_type=F32, **kw)


def _dot_t(a, b, **kw):
    return lax.dot_general(a, b, (((1,), (1,)), ((), ())), preferred_element_type=F32, **kw)


def _ada_kernel(c_ref, w_ref, b_ref, o_ref):
    c = c_ref[...]
    cond = c * jax.nn.sigmoid(c)
    o_ref[...] = _dot(cond, w_ref[...], precision=HIGHEST) + b_ref[...]


def _ada(c, ada_w, ada_b):
    bsz, d = c.shape
    n = ada_w.shape[1]
    c8 = jnp.zeros((8, d), F32).at[:bsz].set(c)
    out = pl.pallas_call(
        _ada_kernel,
        grid=(n // d,),
        in_specs=[pl.BlockSpec((8, d), lambda j: (0, 0)),
                  pl.BlockSpec((d, d), lambda j: (0, j)),
                  pl.BlockSpec((1, d), lambda j: (0, j))],
        out_specs=pl.BlockSpec((8, d), lambda j: (0, j)),
        out_shape=jax.ShapeDtypeStruct((8, n), F32),
        compiler_params=_cparams(("arbitrary",)),
        name="ada",
    )(c8, ada_w, ada_b.reshape(1, n))
    return out[:bsz]


def _transpose_kernel(w_ref, o_ref):
    o_ref[...] = w_ref[...].T.astype(BF16)


def _inproj_kernel(x_ref, sc_ref, sh_ref, g_ref, pos_ref, invf_ref, post_ref, invft_ref, w_ref, wt_ref,
                   u_ref, qt_ref, k_ref, vt_ref, gs_ref, ga_ref, km_ref, *, ssm_w, attn_w, d_model):
    x = x_ref[...]
    ms = jnp.mean(x * x, axis=-1, keepdims=True)
    xn = x * lax.rsqrt(ms + NORM_EPS) * g_ref[...]
    h = (xn * (1.0 + sc_ref[0]) + sh_ref[0]).astype(BF16)

    def proj(lo, width):
        return _dot(h, w_ref[:, lo:lo + width])

    u_ref[...] = proj(0, ssm_w)

    ang = pos_ref[...].astype(F32) * invf_ref[...]
    reps = attn_w // LANES
    cos = jnp.concatenate([jnp.cos(ang)] * reps, axis=1)
    sin = jnp.concatenate([jnp.sin(ang)] * reps, axis=1)
    lane = lax.broadcasted_iota(I32, (1, attn_w), 1)
    first = (lane % HEAD_DIM) < (HEAD_DIM // 2)
    sin = jnp.where(first, -sin, sin)

    def rope(t):
        rot = jnp.where(first, pltpu.roll(t, attn_w - HEAD_DIM // 2, axis=1),
                        pltpu.roll(t, HEAD_DIM // 2, axis=1))
        return t * cos + rot * sin

    k = rope(proj(ssm_w + attn_w, attn_w))
    k_ref[...] = k.astype(BF16)
    tm = k.shape[0]
    nblk = tm // MOBA_BLOCK
    km_ref[0] = jnp.mean(k.reshape(nblk, MOBA_BLOCK, attn_w), axis=1)

    half = HEAD_DIM // 2
    ang_t = invft_ref[...] * post_ref[0].astype(F32)
    cos_t, sin_t = jnp.cos(ang_t), jnp.sin(ang_t)
    qt = _dot_t(wt_ref[0:attn_w, :], h).reshape(N_HEADS, 2, half, tm)
    t1, t2 = qt[:, 0], qt[:, 1]
    qt = jnp.stack([t1 * cos_t - t2 * sin_t, t2 * cos_t + t1 * sin_t], axis=1).reshape(attn_w, tm)
    qt_ref[...] = (qt * Q_SCALE).astype(BF16)
    vt = _dot_t(wt_ref[attn_w:2 * attn_w, :], h).astype(BF16)
    for j in range(nblk):
        vt_ref[j] = vt[:, j * MOBA_BLOCK:(j + 1) * MOBA_BLOCK]
    gs_ref[...] = jax.nn.sigmoid(proj(ssm_w + 3 * attn_w, d_model)).astype(BF16)
    ga_ref[...] = jax.nn.sigmoid(proj(ssm_w + 3 * attn_w + d_model, d_model)).astype(BF16)


def _inproj(x2, sc, sh, g, pos, w_in, seq, tm):
    n_tok, d = x2.shape
    attn_w = N_HEADS * HEAD_DIM
    ssm_w = SSM_GROUPS * SSM_GROUP_SIZE
    in_w = w_in.shape[1]
    half = HEAD_DIM // 2
    inv_freq = ROPE_THETA ** (-jnp.arange(half, dtype=F32) / half)
    invf = jnp.tile(inv_freq, LANES // half).reshape(1, LANES)
    nt = n_tok // tm
    per_b = seq // tm
    nblk = tm // MOBA_BLOCK
    bsz = n_tok // seq
    blk = MOBA_BLOCK
    assert ssm_w % attn_w == 0
    q_blk, v_blk = ssm_w // attn_w, ssm_w // attn_w + 2
    wt = pl.pallas_call(
        _transpose_kernel,
        grid=(2,),
        in_specs=[pl.BlockSpec((d, attn_w), lambda j: (0, q_blk + j * (v_blk - q_blk)))],
        out_specs=pl.BlockSpec((attn_w, d), lambda j: (j, 0)),
        out_shape=jax.ShapeDtypeStruct((2 * attn_w, d), BF16),
        compiler_params=_cparams(("arbitrary",)),
        name="wt",
    )(w_in)
    tok = lambda w: pl.BlockSpec((tm, w), lambda i: (i, 0))
    full = lambda a, b: pl.BlockSpec((a, b), lambda i: (0, 0))
    bvec = pl.BlockSpec((1, 1, d), lambda i: (i // per_b, 0, 0))
    outs = pl.pallas_call(
        functools.partial(_inproj_kernel, ssm_w=ssm_w, attn_w=attn_w, d_model=d),
        grid=(nt,),
        in_specs=[tok(d), bvec, bvec, full(1, d), tok(1), full(1, LANES),
                  pl.BlockSpec((1, 1, tm), lambda i: (i, 0, 0)), full(half, 1),
                  full(d, in_w), full(2 * attn_w, d)],
        out_specs=[tok(ssm_w),
                   pl.BlockSpec((None, attn_w, tm), lambda i: (i // per_b, 0, i % per_b)),
                   tok(attn_w),
                   pl.BlockSpec((None, nblk, attn_w, blk), lambda i: (i // per_b, i % per_b, 0, 0)),
                   tok(d), tok(d),
                   pl.BlockSpec((1, nblk, attn_w), lambda i: (i, 0, 0))],
        out_shape=[jax.ShapeDtypeStruct((n_tok, ssm_w), F32),
                   jax.ShapeDtypeStruct((bsz, attn_w, seq), BF16),
                   jax.ShapeDtypeStruct((n_tok, attn_w), BF16),
                   jax.ShapeDtypeStruct((bsz, seq // blk, attn_w, blk), BF16),
                   jax.ShapeDtypeStruct((n_tok, d), BF16),
                   jax.ShapeDtypeStruct((n_tok, d), BF16),
                   jax.ShapeDtypeStruct((nt, nblk, attn_w), F32)],
        compiler_params=_cparams(("arbitrary",)),
        name="inproj",
    )(x2, sc, sh, g, pos, invf, pos.reshape(nt, 1, tm), inv_freq.reshape(half, 1), w_in.astype(BF16), wt)
    return outs


def _s5_group(x, zr, zi, lr, li, ca, cb, ba, bb, t_ref, *, chunk, n_chunks):
    gs = SSM_GROUP_SIZE

    def powers(tau):
        mag = jnp.exp(tau * zr)
        return mag * jnp.cos(tau * zi), mag * jnp.sin(tau * zi)

    tau = lax.broadcasted_iota(I32, (chunk + 8, LANES), 0).astype(F32)
    e_re, e_im = powers(tau)
    lb_re, lb_im = e_re[1:2], e_im[1:2]
    den = lr * lr + li * li
    a, b = lb_re - 1.0, lb_im
    cf_re = (a * lr + b * li) / den
    cf_im = (b * lr - a * li) / den
    bri = cf_re * ba + cf_im * bb
    bri_sw = cf_re * bb - cf_im * ba

    cpow =(e_re[:chunk + 1, None, :] * ca[None] + e_im[:chunk + 1, None, :] * cb[None])
    cpow = cpow.reshape((chunk + 1) * gs, LANES)
    width = chunk * gs
    r = _dot_t(bri, cpow[:width], precision=HIGHEST)
    col = lax.broadcasted_iota(I32, (gs, width), 1)
    t_ref[0:gs, :] = r.astype(BF16)
    for j in range(1, chunk):
        shifted = jnp.where(col >= gs * j, pltpu.roll(r, gs * j, axis=1), 0.0)
        t_ref[gs * j:gs * (j + 1), :] = shifted.astype(BF16)

    y = _dot(x, t_ref[...])

    tau_rev = (chunk - 1) - lax.broadcasted_iota(I32, (chunk, LANES), 0)
    r_re, r_im = powers(tau_rev.astype(F32))
    bst = r_re[:, None, :] * bri[None] + r_im[:, None, :] * bri_sw[None]
    bst = bst.reshape(width, LANES).astype(BF16)
    s = _dot(x, bst)

    rows = s.shape[0]
    n_idx = lax.broadcasted_iota(I32, (rows, LANES), 0) % n_chunks
    lane = lax.broadcasted_iota(I32, (1, LANES), 1)
    half = LANES // 2

    def cmul(v, p_re, p_im):
        return v * p_re + pltpu.roll(v, half, axis=1) * jnp.where(lane < half, -p_im, p_im)

    sh = 1
    while sh < n_chunks:
        p_re, p_im = powers(jnp.full((1, LANES), float(chunk * sh), F32))
        prev = jnp.where(n_idx >= sh, pltpu.roll(s, sh, axis=0), 0.0)
        s = s + cmul(prev, p_re, p_im)
        sh *= 2
    s_in = jnp.where(n_idx >= 1, pltpu.roll(s, 1, axis=0), 0.0)
    return y + _dot_t(s_in.astype(BF16), cpow[gs:gs + width].astype(BF16))


def _s5_kernel(u_ref, sel_ref, zr_ref, zi_ref, lr_ref, li_ref, ca_ref, cb_ref, ba_ref, bb_ref,
               y_ref, x_scr, y_scr, t_ref, *, chunk, n_chunks, rows):
    gpt = LANES // SSM_GROUP_SIZE

    def plane(j):
        return pl.ds(j, rows, stride=chunk)

    for jt in range(chunk // gpt):
        planes = jnp.concatenate([u_ref[plane(jt * gpt + jj), :] for jj in range(gpt)], axis=1)
        grouped = _dot(planes.astype(BF16), sel_ref[...])
        for g in range(gpt):
            x_scr[g, :, jt * LANES:(jt + 1) * LANES] = grouped[:, g * LANES:(g + 1) * LANES].astype(BF16)

    def per_group(g, _):
        y = _s5_group(x_scr[g], zr_ref[g], zi_ref[g], lr_ref[g], li_ref[g], ca_ref[g], cb_ref[g], ba_ref[g],
                      bb_ref[g], t_ref, chunk=chunk, n_chunks=n_chunks)
        y_scr[g] = y.astype(BF16)
        return 0

    lax.fori_loop(0, gpt, per_group, 0)

    for jt in range(chunk // gpt):
        grouped = jnp.concatenate([y_scr[g, :, jt * LANES:(jt + 1) * LANES] for g in range(gpt)], axis=1)
        planes = _dot(grouped, sel_ref[...])
        for jj in range(gpt):
            y_ref[plane(jt * gpt + jj), :] = planes[:, jj * LANES:(jj + 1) * LANES]


def _s5(u, lam_re, lam_im, log_dt, b_re, b_im, c_re, c_im, bsz, seq):
    g_n, gs, p = SSM_GROUPS, SSM_GROUP_SIZE, SSM_STATE
    chunk = SSM_CHUNK
    n_chunks = seq // chunk
    rows = bsz * n_chunks
    width = chunk * gs
    gpt = LANES // gs
    assert chunk % gpt == 0 and g_n % gpt == 0 and 2 * p == LANES
    dt = jnp.exp(log_dt.astype(F32))[:, None]
    dup = lambda a: jnp.concatenate([a, a], axis=-1).reshape(g_n, 1, 2 * p)
    zr, zi = dup(lam_re * dt), dup(lam_im * dt)
    lr, li = dup(lam_re), dup(lam_im)
    bt_re, bt_im = b_re.transpose(0, 2, 1), b_im.transpose(0, 2, 1)
    ba = jnp.concatenate([bt_re, bt_im], axis=-1)
    bb = jnp.concatenate([-bt_im, bt_re], axis=-1)
    ca = jnp.concatenate([c_re, -c_im], axis=-1)
    cb = jnp.concatenate([-c_im, -c_re], axis=-1)
    idx = jnp.arange(gpt * LANES, dtype=I32)
    swapped = ((idx // gs) % gpt) * LANES + (idx // LANES) * gs + idx % gs
    sel = (swapped[:, None] == idx[None, :]).astype(BF16)
    tile = pl.BlockSpec((bsz * seq, LANES), lambda t: (0, t))
    vec = pl.BlockSpec((gpt, 1, 2 * p), lambda t: (t, 0, 0))
    mat = pl.BlockSpec((gpt, gs, 2 * p), lambda t: (t, 0, 0))
    return pl.pallas_call(
        functools.partial(_s5_kernel, chunk=chunk, n_chunks=n_chunks, rows=rows),
        grid=(g_n // gpt,),
        in_specs=[tile, pl.BlockSpec((gpt * LANES, gpt * LANES), lambda t: (0, 0)),
                  vec, vec, vec, vec, mat, mat, mat, mat],
        out_specs=tile,
        out_shape=jax.ShapeDtypeStruct((bsz * seq, g_n * gs), F32),
        scratch_shapes=[pltpu.VMEM((gpt, rows, width), BF16), pltpu.VMEM((gpt, rows, width), BF16),
                        pltpu.VMEM((width, width), BF16)],
        compiler_params=_cparams(("arbitrary",)),
        name="s5",
    )(u, sel, zr, zi, lr, li, ca, cb, ba, bb)


def _moba_kernel(qt_ref, k_ref, vt_ref, km_ref, o_ref, s_a, s_b, p_a, p_b, *, nb, nbp, qblocks):
    blk = MOBA_BLOCK
    group = 2
    keys = group * blk
    nq = qblocks * blk
    first = qblocks * pl.program_id(2)
    qt = qt_ref[...]
    km = km_ref[...]
    km_hi = km.astype(BF16)
    km_lo = (km - km_hi.astype(F32)).astype(BF16)
    dim_i = lax.broadcasted_iota(I32, (LANES, nq), 0)
    blk_i = lax.broadcasted_iota(I32, (nbp, nq), 0)
    blk_f = blk_i.astype(F32)
    own = first + lax.broadcasted_iota(I32, (nbp, nq), 1) // blk
    valid = blk_i < own

    q_aug = []
    for h in range(2):
        qh = jnp.where((dim_i >= h * HEAD_DIM) & (dim_i < (h + 1) * HEAD_DIM), qt, jnp.zeros_like(qt))
        g = jnp.where(valid, _dot(km_hi, qh) + _dot(km_lo, qh), NEG_INF)
        sel = jnp.zeros((nbp, nq), jnp.bool_)
        for _ in range(MOBA_TOPK):
            m = jnp.max(g, axis=0, keepdims=True)
            idx = jnp.min(jnp.where(g == m, blk_f, float(nbp)), axis=0, keepdims=True)
            pick = blk_f == idx
            sel = sel | pick
            g = jnp.where(pick, -jnp.inf, g)
        ok = (sel & valid) | (blk_i == own)
        parts = [qh, jnp.where(ok, 0.0, NEG_INF).astype(BF16)]
        if nbp < LANES:
            parts.append(jnp.full((LANES - nbp, nq), NEG_INF, BF16))
        q_aug.append(jnp.concatenate(parts, axis=0))

    def scores(grp, s_ref, causal=False):
        kb0 = grp * group
        k_cat = k_ref[pl.ds(jnp.minimum(kb0, nb - group), group)].reshape(keys, LANES)
        blk_id = kb0 + lax.broadcasted_iota(I32, (keys, LANES), 0) // blk
        onehot = jnp.where(lax.broadcasted_iota(I32, (keys, LANES), 1) == blk_id, 1.0, 0.0).astype(BF16)
        k_aug = jnp.concatenate([k_cat, onehot], axis=1)
        mx = []
        for h in range(2):
            s = _dot(k_aug, q_aug[h])
            if causal:
                key_pos = (kb0 - first) * blk + lax.broadcasted_iota(I32, (keys, nq), 0)
                s = jnp.where(key_pos <= lax.broadcasted_iota(I32, (keys, nq), 1), s, NEG_INF)
            s_ref[h] = s
            mx.append(jnp.max(s, axis=0, keepdims=True))
        return tuple(mx)

    def softmax(s_ref, mx, p_ref, st):
        out = []
        for h in range(2):
            m_i, _, l_i, acc = st[h]
            m_new = jnp.maximum(m_i, mx[h])
            p_ref[h] = jnp.exp2(s_ref[h] - m_new).astype(BF16)
            out.append((m_new, jnp.exp2(m_i - m_new), l_i, acc))
        return tuple(out)

    def values(grp, p_ref, st):
        out = []
        for h in range(2):
            m_i, alpha, l_i, acc = st[h]
            blocks = [vt_ref[jnp.clip(grp * group + j, 0, nb - 1), h * HEAD_DIM:(h + 1) * HEAD_DIM, :]
                      for j in range(group)]
            ones = jnp.ones((16, keys), BF16)
            pv = _dot(jnp.concatenate([jnp.concatenate(blocks, axis=1), ones], axis=0), p_ref[h])
            out.append((m_i, alpha, alpha * l_i + pv[HEAD_DIM:HEAD_DIM + 1], alpha * acc + pv[:HEAD_DIM]))
        return tuple(out)

    mx_first = scores(0, s_a)
    p_b[...] = jnp.zeros_like(p_b)

    def body(t, carry):
        st, mx_a = carry
        mx_b = scores(2 * t + 1, s_b)
        st = values(2 * t - 1, p_b, st)
        st = softmax(s_a, mx_a, p_a, st)
        mx_a = scores(2 * t + 2, s_a)
        st = values(2 * t, p_a, st)
        return softmax(s_b, mx_b, p_b, st), mx_a

    init = (jnp.full((1, nq), -jnp.inf, F32), jnp.ones((1, nq), F32), jnp.zeros((1, nq), F32),
            jnp.zeros((HEAD_DIM, nq), F32))
    g0 = first // group
    st, _ = lax.fori_loop(0, g0 // 2, body, ((init, init), mx_first))
    for pair in range(qblocks // (2 * group)):
        ga = g0 + 2 * pair
        mx_a = scores(ga, s_a, causal=True)
        st = values(ga - 1, p_b, st)
        mx_b = scores(ga + 1, s_b, causal=True)
        st = softmax(s_a, mx_a, p_a, st)
        st = values(ga, p_a, st)
        st = softmax(s_b, mx_b, p_b, st)
    st = values(g0 + qblocks // group - 1, p_b, st)
    both = jnp.concatenate([st[h][3] / st[h][2] for h in range(2)], axis=0)
    o_ref[...] = both.T.astype(BF16)


def _moba(qt, k, vt4, km, bsz, seq):
    n_tok, attn_w = k.shape
    blk = MOBA_BLOCK
    nb = seq // blk
    group, qblocks = 2, 4
    nbp = -(-nb // 16) * 16
    assert nb % qblocks == 0 and qblocks % (2 * group) == 0 and nbp <= LANES
    hp = attn_w // LANES
    km_pad = jnp.zeros((bsz, nbp, attn_w), F32).at[:, :nb].set(km.reshape(bsz, nb, attn_w))
    k4 = k.reshape(bsz, nb, blk, attn_w)
    steps = nb // qblocks
    return pl.pallas_call(
        functools.partial(_moba_kernel, nb=nb, nbp=nbp, qblocks=qblocks),
        grid=(bsz, hp, steps),
        in_specs=[pl.BlockSpec((None, LANES, qblocks * blk), lambda b, p, i: (b, p, i)),
                  pl.BlockSpec((None, nb, blk, LANES), lambda b, p, i: (b, 0, 0, p)),
                  pl.BlockSpec((None, nb, LANES, blk), lambda b, p, i: (b, 0, p, 0)),
                  pl.BlockSpec((None, nbp, LANES), lambda b, p, i: (b, 0, p))],
        out_specs=pl.BlockSpec((qblocks * blk, LANES), lambda b, p, i: (b * steps + i, p)),
        out_shape=jax.ShapeDtypeStruct((n_tok, attn_w), BF16),
        scratch_shapes=([pltpu.VMEM((2, group * blk, qblocks * blk), F32)] * 2
                        + [pltpu.VMEM((2, group * blk, qblocks * blk), BF16)] * 2),
        compiler_params=_cparams(("arbitrary", "arbitrary", "arbitrary")),
        name="moba",
    )(qt, k4, vt4, km_pad)


def _rms(t, g):
    return t * lax.rsqrt(jnp.mean(t * t, axis=-1, keepdims=True) + NORM_EPS) * g


def _store_slab(ref, val, base=0):
    rows, d = val.shape
    sub = d // LANES
    for s in range(sub):
        ref[pl.ds(base + s, rows, stride=sub), :] = val[:, s * LANES:(s + 1) * LANES]


def _load_slab(ref, rows, sub, base=0):
    return jnp.concatenate([ref[pl.ds(base + s, rows, stride=sub), :] for s in range(sub)], axis=1)


def _merge_kernel(ys_ref, u_ref, ya_ref, gs_ref, ga_ref, x_ref, d_ref, wglu_ref, wsb_ref, wab_ref,
                  wout_ref, postg_ref, gtm_ref, preg_ref, scf_ref, shf_ref, rw_ref, rb_ref,
                  x1_ref, h2_ref, eid_ref, tw_ref, rank_ref, before_ref, cnt_ref, run_ref):
    @pl.when(pl.program_id(0) == 0)
    def _():
        run_ref[...] = jnp.zeros_like(run_ref)

    tm = x_ref.shape[0]
    hm = tm // 2
    halves = [pl.ds(0, hm), pl.ds(hm, hm)]
    sub = x_ref.shape[1] // LANES

    ys = [jax.nn.gelu(ys_ref[r, :] + d_ref[...] * u_ref[r, :].astype(F32)) for r in halves]
    h2s = []
    for r, y in zip(halves, ys):
        y = y * jax.nn.sigmoid(_dot(y.astype(BF16), wglu_ref[...]))
        bs = _dot(y.astype(BF16), wsb_ref[...])
        ba = _dot(ya_ref[r, :], wab_ref[...])
        merged = gs_ref[r, :].astype(F32) * bs + ga_ref[r, :].astype(F32) * ba
        mix = _dot(merged.astype(BF16), wout_ref[...])
        x1 = x_ref[r, :] + gtm_ref[0] * _rms(mix, postg_ref[...])
        x1_ref[r, :] = x1
        h2s.append(_rms(x1, preg_ref[...]) * (1.0 + scf_ref[0]) + shf_ref[0])
    for j, h2 in enumerate(h2s):
        _store_slab(h2_ref, h2, base=j * hm * sub)

    lane = lax.broadcasted_iota(I32, (hm, LANES), 1)
    lane_f = lane.astype(F32)
    r_i = lax.broadcasted_iota(I32, (hm, hm), 0)
    c_i = lax.broadcasted_iota(I32, (hm, hm), 1)
    tri = jnp.where(c_i < r_i, 1.0, 0.0).astype(BF16)
    before_ref[0] = run_ref[...]
    gs = []
    for h2 in h2s:
        h_hi = h2.astype(BF16)
        h_lo = (h2 - h_hi.astype(F32)).astype(BF16)
        gs.append(_dot(jnp.concatenate([h_hi, h_hi, h_lo], axis=1), rw_ref[...]) + rb_ref[...])
    topk = [([], [], []) for _ in halves]
    for _ in range(TOP_K):
        for j in range(2):
            m = jnp.max(gs[j], axis=1, keepdims=True)
            idx = jnp.min(jnp.where(gs[j] == m, lane_f, float(LANES)), axis=1, keepdims=True)
            pick = lane_f == idx
            topk[j][0].append(m)
            topk[j][1].append(pick)
            topk[j][2].append(idx.astype(I32))
            gs[j] = jnp.where(pick, -jnp.inf, gs[j])
    for r, (vals, picks, idxs) in zip(halves, topk):
        exps = [jnp.exp(v - vals[0]) for v in vals]
        tot = exps[0] + exps[1] + exps[2] + exps[3]

        onehot = jnp.where(picks[0] | picks[1] | picks[2] | picks[3], 1.0, 0.0)
        rank_full = _dot(tri, onehot.astype(BF16)) + (run_ref[...] - before_ref[0])
        run_ref[...] = run_ref[...] + jnp.sum(onehot, axis=0, keepdims=True)

        eid = jnp.zeros((hm, LANES), I32)
        tw = jnp.zeros((hm, LANES), F32)
        rk = jnp.zeros((hm, LANES), F32)
        for k in range(TOP_K):
            eid = jnp.where(lane == k, idxs[k], eid)
            tw = jnp.where(lane == k, exps[k] / tot, tw)
            pos = jnp.sum(jnp.where(picks[k], rank_full, 0.0), axis=1, keepdims=True)
            rk = jnp.where(lane == k, pos, rk)
        eid_ref[r, :] = eid
        tw_ref[r, :] = tw
        rank_ref[r, :] = rk.astype(I32)
    cnt_ref[...] = run_ref[...]


def _merge(ys, u, ya, gs, ga, x2, ssm_d, w_glu, w_sb, w_ab, w_out, post_g, gt_m, pre_g, sc_f, sh_f,
           router_w, router_b, seq, tm):
    n_tok, d = x2.shape
    sw = ys.shape[1]
    aw = ya.shape[1]
    ne = router_w.shape[1]
    per_b = seq // tm
    rw = jnp.zeros((d, LANES), F32).at[:, :ne].set(router_w)
    rw_hi = rw.astype(BF16)
    rw_lo = (rw - rw_hi.astype(F32)).astype(BF16)
    rw = jnp.concatenate([rw_hi, rw_lo, rw_hi], axis=0)
    rb = jnp.full((1, LANES), NEG_INF, F32).at[0, :ne].set(router_b)
    tok = lambda w: pl.BlockSpec((tm, w), lambda i: (i, 0))
    full = lambda a, b: pl.BlockSpec((a, b), lambda i: (0, 0))
    bvec = pl.BlockSpec((1, 1, d), lambda i: (i // per_b, 0, 0))
    return pl.pallas_call(
        _merge_kernel,
        grid=(n_tok // tm,),
        in_specs=[tok(sw), tok(sw), tok(aw), tok(d), tok(d), tok(d),
                  full(1, sw), full(sw, sw), full(sw, d), full(aw, d), full(d, d),
                  full(1, d), bvec, full(1, d), bvec, bvec, full(3 * d, LANES), full(1, LANES)],
        out_specs=[tok(d), pl.BlockSpec((tm * (d // LANES), LANES), lambda i: (i, 0)),
                   tok(LANES), tok(LANES), tok(LANES),
                   pl.BlockSpec((1, 1, LANES), lambda i: (i, 0, 0)), full(1, LANES)],
        out_shape=[jax.ShapeDtypeStruct((n_tok, d), F32),
                   jax.ShapeDtypeStruct((n_tok * (d // LANES), LANES), F32),
                   jax.ShapeDtypeStruct((n_tok, LANES), I32),
                   jax.ShapeDtypeStruct((n_tok, LANES), F32),
                   jax.ShapeDtypeStruct((n_tok, LANES), I32),
                   jax.ShapeDtypeStruct((n_tok // tm, 1, LANES), F32),
                   jax.ShapeDtypeStruct((1, LANES), F32)],
        scratch_shapes=[pltpu.VMEM((1, LANES), F32)],
        compiler_params=_cparams(("arbitrary",)),
        name="merge",
    )(ys, u, ya, gs, ga, x2, ssm_d.reshape(1, sw), w_glu.astype(BF16), w_sb.astype(BF16),
      w_ab.astype(BF16), w_out.astype(BF16), post_g.reshape(1, d), gt_m, pre_g.reshape(1, d),
      sc_f, sh_f, rw, rb)


def _chunk_copies(cnt_ref, start_ref, tile, ne, max_rows, make):
    def per_expert(e, off):
        c, start = cnt_ref[tile * ne + e], start_ref[tile * ne + e]
        for b in range(max_rows.bit_length()):
            n = 1 << b

            @pl.when((c >> b) & 1 == 1)
            def _():
                make(off + (c & (n - 1)), start + (c & (n - 1)), n)
        return off + c

    lax.fori_loop(0, ne, per_expert, 0)


def _dispatch_kernel(cnt_ref, start_ref, pad_start_ref, pad_cnt_ref, tail_ref, loct_ref, h_ref, xs_ref,
                     buf, zero_ref, sem, zsem, *, tm, sub, bm, n_rows, ne):
    i = pl.program_id(0)
    last = pl.num_programs(0) - 1
    slot = i % 2
    n_loc = tm * TOP_K

    def row(ref, r, n=1):
        return ref.at[pl.ds(pl.multiple_of(r * sub, sub), n * sub)]

    def wait_slot(s):
        pltpu.make_async_copy(buf.at[s], row(xs_ref, 0, n_loc), sem.at[s]).wait()

    def zero_copies(do):
        def per_expert(e, _):
            start, cnt = pad_start_ref[e], pad_cnt_ref[e]
            for b in range(bm.bit_length() - 1):
                n = 1 << b

                @pl.when((cnt >> b) & 1 == 1)
                def _():
                    do(pltpu.make_async_copy(row(zero_ref, 0, n), row(xs_ref, start + (cnt & (n - 1)), n), zsem))
            return 0

        lax.fori_loop(0, ne, per_expert, 0)

        def per_block(j, _):
            r = tail_ref[0] + j * bm

            @pl.when(r < n_rows)
            def _():
                do(pltpu.make_async_copy(row(zero_ref, 0, bm), row(xs_ref, r, bm), zsem))
            return 0

        lax.fori_loop(0, ne, per_block, 0)

    @pl.when(i == 0)
    def _():
        zero_ref[...] = jnp.zeros_like(zero_ref)
        zero_copies(lambda cp: cp.start())

    @pl.when(i >= 2)
    def _():
        wait_slot(slot)

    loct = loct_ref[0]
    row_i = lax.broadcasted_iota(I32, (n_loc, tm), 0)
    place = row_i == loct[0:1, :]
    for k in range(1, TOP_K):
        place = place | (row_i == loct[k:k + 1, :])
    grouped = _dot(jnp.where(place, 1.0, 0.0).astype(BF16), _load_slab(h_ref, tm, sub).astype(BF16))
    _store_slab(buf.at[slot], grouped)
    _chunk_copies(cnt_ref, start_ref, i, ne, tm,
                  lambda loc, glob, n: pltpu.make_async_copy(row(buf.at[slot], loc, n), row(xs_ref, glob, n),
                                                             sem.at[slot]).start())

    @pl.when(i == last)
    def _():
        wait_slot(slot)

    @pl.when((i == last) & (i >= 1))
    def _():
        wait_slot(1 - slot)

    @pl.when(i == 0)
    def _():
        zero_copies(lambda cp: cp.wait())


def _dispatch(h_slab, loc, cnt_tile, start_tile, pad_start, pad_cnt, tail, n_rows, sub, bm, tm):
    n_tok = loc.shape[0]
    nt = n_tok // tm
    ne = pad_start.shape[0]
    loct = jnp.full((nt, 8, tm), -1, I32).at[:, :TOP_K].set(loc.reshape(nt, tm, TOP_K).transpose(0, 2, 1))
    grid_spec = pltpu.PrefetchScalarGridSpec(
        num_scalar_prefetch=5,
        grid=(nt,),
        in_specs=[pl.BlockSpec((1, 8, tm), lambda i, *_: (i, 0, 0)),
                  pl.BlockSpec((tm * sub, LANES), lambda i, *_: (i, 0))],
        out_specs=pl.BlockSpec(memory_space=pl.ANY),
        scratch_shapes=[pltpu.VMEM((2, TOP_K * tm * sub, LANES), F32), pltpu.VMEM((bm * sub, LANES), F32),
                        pltpu.SemaphoreType.DMA((2,)), pltpu.SemaphoreType.DMA(())],
    )
    return pl.pallas_call(
        functools.partial(_dispatch_kernel, tm=tm, sub=sub, bm=bm, n_rows=n_rows, ne=ne),
        grid_spec=grid_spec,
        out_shape=jax.ShapeDtypeStruct((n_rows * sub, LANES), F32),
        compiler_params=_cparams(("arbitrary",)),
        name="dispatch",
    )(cnt_tile.reshape(-1), start_tile.reshape(-1), pad_start, pad_cnt, tail, loct, h_slab)


def _expert_kernel(be_ref, nv_ref, x_ref, wg_ref, bg_ref, wu_ref, bu_ref, wd_ref, bd_ref,
                   o_ref, wg_s, wu_s, wd_s, *, bm, sub):
    i = pl.program_id(0)
    prev = be_ref[jnp.maximum(i - 1, 0)]
    changed = (i == 0) | (be_ref[i] != prev)

    @pl.when(changed)
    def _():
        wg_s[...] = wg_ref[0].astype(BF16)
        wu_s[...] = wu_ref[0].astype(BF16)
        wd_s[...] = wd_ref[0].astype(BF16)

    @pl.when(i < nv_ref[0])
    def _():
        xb = _load_slab(x_ref, bm, sub).astype(BF16)
        g = _dot(xb, wg_s[...]) + bg_ref[0]
        u = _dot(xb, wu_s[...]) + bu_ref[0]
        g = jnp.minimum(g, SWIGLU_LIMIT)
        u = jnp.clip(u, -SWIGLU_LIMIT, SWIGLU_LIMIT)
        act = g * jax.nn.sigmoid(SWIGLU_ALPHA * g) * (u + 1.0)
        _store_slab(o_ref, _dot(act.astype(BF16), wd_s[...]) + bd_ref[0])

    @pl.when(i >= nv_ref[0])
    def _():
        o_ref[...] = jnp.zeros_like(o_ref)


def _experts(xs, blk_exp, n_valid, w_gate, b_gate, w_up, b_up, w_down, b_down, bm):
    ne, d, f = w_gate.shape
    sub = d // LANES
    n_blocks = xs.shape[0] // (bm * sub)
    wspec = lambda a, b: pl.BlockSpec((1, a, b), lambda i, be, nv: (be[i], 0, 0))
    grid_spec = pltpu.PrefetchScalarGridSpec(
        num_scalar_prefetch=2,
        grid=(n_blocks,),
        in_specs=[pl.BlockSpec((bm * sub, LANES), lambda i, be, nv: (jnp.clip(nv[0] - 1, 0, i), 0)),
                  wspec(d, f), wspec(1, f), wspec(d, f), wspec(1, f), wspec(f, d), wspec(1, d)],
        out_specs=pl.BlockSpec((bm * sub, LANES), lambda i, be, nv: (i, 0)),
        scratch_shapes=[pltpu.VMEM((d, f), BF16), pltpu.VMEM((d, f), BF16), pltpu.VMEM((f, d), BF16)],
    )
    return pl.pallas_call(
        functools.partial(_expert_kernel, bm=bm, sub=sub),
        grid_spec=grid_spec,
        out_shape=jax.ShapeDtypeStruct(xs.shape, F32),
        compiler_params=_cparams(("arbitrary",)),
        name="experts",
    )(blk_exp, n_valid, xs, w_gate, b_gate.reshape(ne, 1, f), w_up, b_up.reshape(ne, 1, f),
      w_down, b_down.reshape(ne, 1, d))


def _combine_kernel(cnt_ref, start_ref, yb_ref, loc_ref, tw_ref, x1_ref, g_ref, gt_ref, o_ref, buf, sem,
                    *, tm, sub, ne):
    i = pl.program_id(0)
    n_loc = tm * TOP_K

    def start_tile(tile, slot):
        def make(loc, glob, n):
            src = yb_ref.at[pl.ds(pl.multiple_of(glob * sub, sub), n * sub)]
            dst = buf.at[slot, pl.ds(pl.multiple_of(loc * sub, sub), n * sub)]
            pltpu.make_async_copy(src, dst, sem.at[slot]).start()

        _chunk_copies(cnt_ref, start_ref, tile, ne, tm, make)

    slot = i % 2

    @pl.when(i == 0)
    def _():
        start_tile(0, 0)

    @pl.when(i + 1 < pl.num_programs(0))
    def _():
        start_tile(i + 1, 1 - slot)

    pltpu.make_async_copy(yb_ref.at[pl.ds(0, n_loc * sub)], buf.at[slot], sem.at[slot]).wait()
    rows = _load_slab(buf.at[slot], n_loc, sub).astype(BF16)
    loc, tw = loc_ref[...], tw_ref[...]
    col_i = lax.broadcasted_iota(I32, (tm, n_loc), 1)
    w = jnp.zeros((tm, n_loc), F32)
    for k in range(TOP_K):
        w = jnp.where(col_i == loc[:, k:k + 1], tw[:, k:k + 1], w)
    ff = _dot(w.astype(BF16), rows)
    o_ref[...] = x1_ref[...] + gt_ref[0] * _rms(ff, g_ref[...])


def _combine(yb, loc, tw, cnt_tile, start_tile, x1, post_g, gt_f, seq, tm):
    n_tok, d = x1.shape
    sub = d // LANES
    nt = n_tok // tm
    ne = cnt_tile.shape[1]
    per_b = seq // tm
    loc_l = jnp.full((n_tok, LANES), -1, I32).at[:, :TOP_K].set(loc)
    grid_spec = pltpu.PrefetchScalarGridSpec(
        num_scalar_prefetch=2,
        grid=(nt,),
        in_specs=[pl.BlockSpec(memory_space=pl.ANY),
                  pl.BlockSpec((tm, LANES), lambda i, *_: (i, 0)),
                  pl.BlockSpec((tm, LANES), lambda i, *_: (i, 0)),
                  pl.BlockSpec((tm, d), lambda i, *_: (i, 0)),
                  pl.BlockSpec((1, d), lambda i, *_: (0, 0)),
                  pl.BlockSpec((1, 1, d), lambda i, *_: (i // per_b, 0, 0))],
        out_specs=pl.BlockSpec((tm, d), lambda i, *_: (i, 0)),
        scratch_shapes=[pltpu.VMEM((2, TOP_K * tm * sub, LANES), F32), pltpu.SemaphoreType.DMA((2,))],
    )
    return pl.pallas_call(
        functools.partial(_combine_kernel, tm=tm, sub=sub, ne=ne),
        grid_spec=grid_spec,
        out_shape=jax.ShapeDtypeStruct((n_tok, d), F32),
        compiler_params=_cparams(("arbitrary",)),
        name="combine",
    )(cnt_tile.reshape(-1), start_tile.reshape(-1), yb, loc_l, tw, x1, post_g.reshape(1, d), gt_f)


def kernel(x, c, positions, ada_w, ada_b, mix_pre_g, mix_post_g, ffn_pre_g, ffn_post_g, w_in, ssm_lam_re, ssm_lam_im, ssm_log_dt, ssm_b_re, ssm_b_im, ssm_c_re, ssm_c_im, ssm_d, ssm_w_glu, w_ssm_branch, w_attn_branch, w_out, router_w, router_b, w_gate, b_gate, w_up, b_up, w_down, b_down):
    bsz, seq, d = x.shape
    depth = ada_w.shape[0]
    n_tok = bsz * seq
    bm = EXPERT_ROWS
    sub = d // LANES
    xcur = x.reshape(n_tok, d)
    pos = positions.reshape(n_tok, 1).astype(I32)
    for l in range(depth):
        ada = _ada(c, ada_w[l], ada_b[l])
        sh_m, sc_m, gt_m, sh_f, sc_f, gt_f = [a.reshape(bsz, 1, d) for a in jnp.split(ada, 6, axis=-1)]

        u, q, k, v, gs, ga, km = _inproj(xcur, sc_m, sh_m, mix_pre_g[l].reshape(1, d), pos, w_in[l],
                                         seq, tm=min(512, seq))
        ys = _s5(u, ssm_lam_re[l], ssm_lam_im[l], ssm_log_dt[l], ssm_b_re[l], ssm_b_im[l],
                 ssm_c_re[l], ssm_c_im[l], bsz, seq)
        ya = _moba(q, k, v, km, bsz, seq)
        tm = ROUTE_TILE
        x1, h2, eid, tw, rank, before, cnt = _merge(
            ys, u, ya, gs, ga, xcur, ssm_d[l], ssm_w_glu[l], w_ssm_branch[l], w_attn_branch[l], w_out[l],
            mix_post_g[l], gt_m, ffn_pre_g[l], sc_f, sh_f, router_w[l], router_b[l], seq, tm=tm)

        ne = router_w.shape[-1]
        nt = n_tok // tm
        experts = jnp.arange(ne, dtype=I32)
        counts = cnt[0, :ne].astype(I32)
        padded = (counts + bm - 1) // bm * bm
        p_ends = jnp.cumsum(padded)
        p_starts = p_ends - padded
        before = before.reshape(nt, LANES)[:, :ne].astype(I32)
        cnt_tile = jnp.concatenate([before[1:], counts[None]], axis=0) - before
        start_tile = p_starts[None, :] + before
        off_tile = jnp.cumsum(cnt_tile, axis=1) - cnt_tile
        onehot = eid[:, :TOP_K].reshape(nt, tm, TOP_K, 1) == experts
        loc = (rank[:, :TOP_K] + jnp.sum(jnp.where(onehot, off_tile[:, None, None, :], 0), axis=-1).reshape(n_tok, TOP_K))
        n_blocks = (n_tok * TOP_K) // bm + ne
        n_rows = n_blocks * bm
        blk_start = jnp.arange(n_blocks, dtype=I32)[:, None] * bm
        blk_exp = jnp.minimum(jnp.sum((blk_start >= p_ends[None, :]).astype(I32), axis=1), ne - 1)
        n_valid = (p_ends[-1:] // bm).astype(I32)

        xs = _dispatch(h2, loc, cnt_tile, start_tile, p_starts + counts, padded - counts, p_ends[-1:],
                       n_rows, sub, bm, tm)
        yb = _experts(xs, blk_exp, n_valid, w_gate[l], b_gate[l], w_up[l], b_up[l], w_down[l], b_down[l], bm)
        xcur = _combine(yb, loc, tw, cnt_tile, start_tile, x1, ffn_post_g[l], gt_f, seq, tm)
    return xcur.reshape(bsz, seq, d).astype(x.dtype)
```

```python
import functools
import math

import jax
import jax.numpy as jnp
from jax import lax
from jax.experimental import pallas as pl
from jax.experimental.pallas import tpu as pltpu

F32 = jnp.float32
BF16 = jnp.bfloat16
I32 = jnp.int32

N_HEADS = 8
HEAD_DIM = 64
ROPE_THETA = 10000.0
MOBA_BLOCK = 256
MOBA_TOPK = 3
SSM_GROUP_SIZE = 16
SSM_GROUPS = 32
SSM_STATE = 64
N_EXPERTS = 32
TOP_K = 4
SWIGLU_ALPHA = 1.702
SWIGLU_LIMIT = 7.0
NORM_EPS = 1e-6
NEG_INF = -1e30

LANES = 128
SSM_CHUNK = 64
EXPERT_ROWS = 512
ROUTE_TILE = 512
VMEM_LIMIT = 56 * 1024 * 1024
HIGHEST = lax.Precision.HIGHEST
Q_SCALE = HEAD_DIM ** -0.5 * math.log2(math.e)


def _cparams(sem):
    return pltpu.CompilerParams(dimension_semantics=sem, vmem_limit_bytes=VMEM_LIMIT)


def _dot(a, b, **kw):
    return jnp.dot(a, b, preferred_element_type=F32, **kw)


def _dot_t(a, b, **kw):
    return lax.dot_general(a, b, (((1,), (1,)), ((), ())), preferred_element_type=F32, **kw)


def _ada_kernel(c_ref, w_ref, b_ref, o_ref):
    c = c_ref[...]
    cond = c * jax.nn.sigmoid(c)
    o_ref[...] = _dot(cond, w_ref[...], precision=HIGHEST) + b_ref[...]


def _ada(c, ada_w, ada_b):
    bsz, d = c.shape
    n = ada_w.shape[1]
    c8 = jnp.zeros((8, d), F32).at[:bsz].set(c)
    out = pl.pallas_call(
        _ada_kernel,
        grid=(n // d,),
        in_specs=[pl.BlockSpec((8, d), lambda j: (0, 0)),
                  pl.BlockSpec((d, d), lambda j: (0, j)),
                  pl.BlockSpec((1, d), lambda j: (0, j))],
        out_specs=pl.BlockSpec((8, d), lambda j: (0, j)),
        out_shape=jax.ShapeDtypeStruct((8, n), F32),
        compiler_params=_cparams(("arbitrary",)),
        name="ada",
    )(c8, ada_w, ada_b.reshape(1, n))
    return out[:bsz]


def _transpose_kernel(w_ref, o_ref):
    o_ref[...] = w_ref[...].T.astype(BF16)


def _inproj_kernel(x_ref, sc_ref, sh_ref, g_ref, pos_ref, invf_ref, post_ref, invft_ref, w_ref, wt_ref,
                   u_ref, qt_ref, k_ref, vt_ref, gs_ref, ga_ref, km_ref, *, ssm_w, attn_w, d_model):
    x = x_ref[...]
    ms = jnp.mean(x * x, axis=-1, keepdims=True)
    xn = x * lax.rsqrt(ms + NORM_EPS) * g_ref[...]
    h = (xn * (1.0 + sc_ref[0]) + sh_ref[0]).astype(BF16)

    def proj(lo, width):
        return _dot(h, w_ref[:, lo:lo + width])

    tm = x.shape[0]
    ang = pos_ref[...].astype(F32) * invf_ref[...]
    reps = attn_w // LANES
    cos = jnp.concatenate([jnp.cos(ang)] * reps, axis=1)
    sin = jnp.concatenate([jnp.sin(ang)] * reps, axis=1)
    lane = lax.broadcasted_iota(I32, (1, attn_w), 1)
    first = (lane % HEAD_DIM) < (HEAD_DIM // 2)
    sin = jnp.where(first, -sin, sin)
    ang_t = invft_ref[...] * post_ref[0].astype(F32)
    cos_t, sin_t = jnp.cos(ang_t), jnp.sin(ang_t)

    u_ref[...] = proj(0, ssm_w)
    gs_ref[...] = jax.nn.sigmoid(proj(ssm_w + 3 * attn_w, d_model)).astype(BF16)
    ga_ref[...] = jax.nn.sigmoid(proj(ssm_w + 3 * attn_w + d_model, d_model)).astype(BF16)
    vt = _dot_t(wt_ref[attn_w:2 * attn_w, :], h).astype(BF16)
    nblk = tm // MOBA_BLOCK
    for j in range(nblk):
        vt_ref[j] = vt[:, j * MOBA_BLOCK:(j + 1) * MOBA_BLOCK]

    def rope(t):
        rot = jnp.where(first, pltpu.roll(t, attn_w - HEAD_DIM // 2, axis=1),
                        pltpu.roll(t, HEAD_DIM // 2, axis=1))
        return t * cos + rot * sin

    k = rope(proj(ssm_w + attn_w, attn_w))
    k_ref[...] = k.astype(BF16)
    km_ref[0] = jnp.mean(k.reshape(nblk, MOBA_BLOCK, attn_w), axis=1)

    half = HEAD_DIM // 2
    qt = _dot_t(wt_ref[0:attn_w, :], h).reshape(N_HEADS, 2, half, tm)
    t1, t2 = qt[:, 0], qt[:, 1]
    qt = jnp.stack([t1 * cos_t - t2 * sin_t, t2 * cos_t + t1 * sin_t], axis=1).reshape(attn_w, tm)
    qt_ref[...] = (qt * Q_SCALE).astype(BF16)


def _inproj(x2, sc, sh, g, pos, w_in, seq, tm):
    n_tok, d = x2.shape
    attn_w = N_HEADS * HEAD_DIM
    ssm_w = SSM_GROUPS * SSM_GROUP_SIZE
    in_w = w_in.shape[1]
    half = HEAD_DIM // 2
    inv_freq = ROPE_THETA ** (-jnp.arange(half, dtype=F32) / half)
    invf = jnp.tile(inv_freq, LANES // half).reshape(1, LANES)
    nt = n_tok // tm
    per_b = seq // tm
    nblk = tm // MOBA_BLOCK
    bsz = n_tok // seq
    blk = MOBA_BLOCK
    assert ssm_w % attn_w == 0
    q_blk, v_blk = ssm_w // attn_w, ssm_w // attn_w + 2
    wt = pl.pallas_call(
        _transpose_kernel,
        grid=(2,),
        in_specs=[pl.BlockSpec((d, attn_w), lambda j: (0, q_blk + j * (v_blk - q_blk)))],
        out_specs=pl.BlockSpec((attn_w, d), lambda j: (j, 0)),
        out_shape=jax.ShapeDtypeStruct((2 * attn_w, d), BF16),
        compiler_params=_cparams(("arbitrary",)),
        name="wt",
    )(w_in)
    tok = lambda w: pl.BlockSpec((tm, w), lambda i: (i, 0))
    full = lambda a, b: pl.BlockSpec((a, b), lambda i: (0, 0))
    bvec = pl.BlockSpec((1, 1, d), lambda i: (i // per_b, 0, 0))
    outs = pl.pallas_call(
        functools.partial(_inproj_kernel, ssm_w=ssm_w, attn_w=attn_w, d_model=d),
        grid=(nt,),
        in_specs=[tok(d), bvec, bvec, full(1, d), tok(1), full(1, LANES),
                  pl.BlockSpec((1, 1, tm), lambda i: (i, 0, 0)), full(half, 1),
                  full(d, in_w), full(2 * attn_w, d)],
        out_specs=[tok(ssm_w),
                   pl.BlockSpec((None, attn_w, tm), lambda i: (i // per_b, 0, i % per_b)),
                   tok(attn_w),
                   pl.BlockSpec((None, nblk, attn_w, blk), lambda i: (i // per_b, i % per_b, 0, 0)),
                   tok(d), tok(d),
                   pl.BlockSpec((1, nblk, attn_w), lambda i: (i, 0, 0))],
        out_shape=[jax.ShapeDtypeStruct((n_tok, ssm_w), F32),
                   jax.ShapeDtypeStruct((bsz, attn_w, seq), BF16),
                   jax.ShapeDtypeStruct((n_tok, attn_w), BF16),
                   jax.ShapeDtypeStruct((bsz, seq // blk, attn_w, blk), BF16),
                   jax.ShapeDtypeStruct((n_tok, d), BF16),
                   jax.ShapeDtypeStruct((n_tok, d), BF16),
                   jax.ShapeDtypeStruct((nt, nblk, attn_w), F32)],
        compiler_params=_cparams(("arbitrary",)),
        name="inproj",
    )(x2, sc, sh, g, pos, invf, pos.reshape(nt, 1, tm), inv_freq.reshape(half, 1), w_in.astype(BF16), wt)
    return outs


def _s5_group(x, zr, zi, lr, li, ca, cb, ba, bb, t_ref, *, chunk, n_chunks):
    gs = SSM_GROUP_SIZE

    def powers(tau):
        mag = jnp.exp(tau * zr)
        return mag * jnp.cos(tau * zi), mag * jnp.sin(tau * zi)

    tau = lax.broadcasted_iota(I32, (chunk + 8, LANES), 0).astype(F32)
    e_re, e_im = powers(tau)
    lb_re, lb_im = e_re[1:2], e_im[1:2]
    den = lr * lr + li * li
    a, b = lb_re - 1.0, lb_im
    cf_re = (a * lr + b * li) / den
    cf_im = (b * lr - a * li) / den
    bri = cf_re * ba + cf_im * bb
    bri_sw = cf_re * bb - cf_im * ba

    cpow =(e_re[:chunk + 1, None, :] * ca[None] + e_im[:chunk + 1, None, :] * cb[None])
    cpow = cpow.reshape((chunk + 1) * gs, LANES)
    width = chunk * gs
    r = _dot_t(bri, cpow[:width], precision=HIGHEST)
    col = lax.broadcasted_iota(I32, (gs, width), 1)
    t_ref[0:gs, :] = r.astype(BF16)
    for j in range(1, chunk):
        shifted = jnp.where(col >= gs * j, pltpu.roll(r, gs * j, axis=1), 0.0)
        t_ref[gs * j:gs * (j + 1), :] = shifted.astype(BF16)

    y = _dot(x, t_ref[...])

    tau_rev = (chunk - 1) - lax.broadcasted_iota(I32, (chunk, LANES), 0)
    r_re, r_im = powers(tau_rev.astype(F32))
    bst = r_re[:, None, :] * bri[None] + r_im[:, None, :] * bri_sw[None]
    bst = bst.reshape(width, LANES).astype(BF16)
    s = _dot(x, bst)

    rows = s.shape[0]
    n_idx = lax.broadcasted_iota(I32, (rows, LANES), 0) % n_chunks
    lane = lax.broadcasted_iota(I32, (1, LANES), 1)
    half = LANES // 2

    def cmul(v, p_re, p_im):
        return v * p_re + pltpu.roll(v, half, axis=1) * jnp.where(lane < half, -p_im, p_im)

    sh = 1
    while sh < n_chunks:
        p_re, p_im = powers(jnp.full((1, LANES), float(chunk * sh), F32))
        prev = jnp.where(n_idx >= sh, pltpu.roll(s, sh, axis=0), 0.0)
        s = s + cmul(prev, p_re, p_im)
        sh *= 2
    s_in = jnp.where(n_idx >= 1, pltpu.roll(s, 1, axis=0), 0.0)
    return y + _dot_t(s_in.astype(BF16), cpow[gs:gs + width].astype(BF16))


def _s5_kernel(u_ref, sel_ref, zr_ref, zi_ref, lr_ref, li_ref, ca_ref, cb_ref, ba_ref, bb_ref,
               y_ref, x_scr, y_scr, t_ref, *, chunk, n_chunks, rows):
    gpt = LANES // SSM_GROUP_SIZE

    def plane(j):
        return pl.ds(j, rows, stride=chunk)

    for jt in range(chunk // gpt):
        planes = jnp.concatenate([u_ref[plane(jt * gpt + jj), :] for jj in range(gpt)], axis=1)
        grouped = _dot(planes.astype(BF16), sel_ref[...])
        for g in range(gpt):
            x_scr[g, :, jt * LANES:(jt + 1) * LANES] = grouped[:, g * LANES:(g + 1) * LANES].astype(BF16)

    def per_group(g, _):
        y = _s5_group(x_scr[g], zr_ref[g], zi_ref[g], lr_ref[g], li_ref[g], ca_ref[g], cb_ref[g], ba_ref[g],
                      bb_ref[g], t_ref, chunk=chunk, n_chunks=n_chunks)
        y_scr[g] = y.astype(BF16)
        return 0

    lax.fori_loop(0, gpt, per_group, 0)

    for jt in range(chunk // gpt):
        grouped = jnp.concatenate([y_scr[g, :, jt * LANES:(jt + 1) * LANES] for g in range(gpt)], axis=1)
        planes = _dot(grouped, sel_ref[...])
        for jj in range(gpt):
            y_ref[plane(jt * gpt + jj), :] = planes[:, jj * LANES:(jj + 1) * LANES]


def _s5(u, lam_re, lam_im, log_dt, b_re, b_im, c_re, c_im, bsz, seq):
    g_n, gs, p = SSM_GROUPS, SSM_GROUP_SIZE, SSM_STATE
    chunk = SSM_CHUNK
    n_chunks = seq // chunk
    rows = bsz * n_chunks
    width = chunk * gs
    gpt = LANES // gs
    assert chunk % gpt == 0 and g_n % gpt == 0 and 2 * p == LANES
    dt = jnp.exp(log_dt.astype(F32))[:, None]
    dup = lambda a: jnp.concatenate([a, a], axis=-1).reshape(g_n, 1, 2 * p)
    zr, zi = dup(lam_re * dt), dup(lam_im * dt)
    lr, li = dup(lam_re), dup(lam_im)
    bt_re, bt_im = b_re.transpose(0, 2, 1), b_im.transpose(0, 2, 1)
    ba = jnp.concatenate([bt_re, bt_im], axis=-1)
    bb = jnp.concatenate([-bt_im, bt_re], axis=-1)
    ca = jnp.concatenate([c_re, -c_im], axis=-1)
    cb = jnp.concatenate([-c_im, -c_re], axis=-1)
    idx = jnp.arange(gpt * LANES, dtype=I32)
    swapped = ((idx // gs) % gpt) * LANES + (idx // LANES) * gs + idx % gs
    sel = (swapped[:, None] == idx[None, :]).astype(BF16)
    tile = pl.BlockSpec((bsz * seq, LANES), lambda t: (0, t))
    vec = pl.BlockSpec((gpt, 1, 2 * p), lambda t: (t, 0, 0))
    mat = pl.BlockSpec((gpt, gs, 2 * p), lambda t: (t, 0, 0))
    return pl.pallas_call(
        functools.partial(_s5_kernel, chunk=chunk, n_chunks=n_chunks, rows=rows),
        grid=(g_n // gpt,),
        in_specs=[tile, pl.BlockSpec((gpt * LANES, gpt * LANES), lambda t: (0, 0)),
                  vec, vec, vec, vec, mat, mat, mat, mat],
        out_specs=tile,
        out_shape=jax.ShapeDtypeStruct((bsz * seq, g_n * gs), F32),
        scratch_shapes=[pltpu.VMEM((gpt, rows, width), BF16), pltpu.VMEM((gpt, rows, width), BF16),
                        pltpu.VMEM((width, width), BF16)],
        compiler_params=_cparams(("arbitrary",)),
        name="s5",
    )(u, sel, zr, zi, lr, li, ca, cb, ba, bb)


def _moba_kernel(qt_ref, k_ref, vt_ref, km_ref, o_ref, s_a, s_b, p_a, p_b, *, nb, nbp, qblocks):
    blk = MOBA_BLOCK
    group = 2
    keys = group * blk
    nq = qblocks * blk
    first = qblocks * pl.program_id(2)
    qt = qt_ref[...]
    km = km_ref[...]
    km_hi = km.astype(BF16)
    km_lo = (km - km_hi.astype(F32)).astype(BF16)
    dim_i = lax.broadcasted_iota(I32, (LANES, nq), 0)
    blk_i = lax.broadcasted_iota(I32, (nbp, nq), 0)
    blk_f = blk_i.astype(F32)
    own = first + lax.broadcasted_iota(I32, (nbp, nq), 1) // blk
    valid = blk_i < own

    q_aug = []
    for h in range(2):
        qh = jnp.where((dim_i >= h * HEAD_DIM) & (dim_i < (h + 1) * HEAD_DIM), qt, jnp.zeros_like(qt))
        g = jnp.where(valid, _dot(km_hi, qh) + _dot(km_lo, qh), NEG_INF)
        sel = jnp.zeros((nbp, nq), jnp.bool_)
        for _ in range(MOBA_TOPK):
            m = jnp.max(g, axis=0, keepdims=True)
            idx = jnp.min(jnp.where(g == m, blk_f, float(nbp)), axis=0, keepdims=True)
            pick = blk_f == idx
            sel = sel | pick
            g = jnp.where(pick, -jnp.inf, g)
        ok = (sel & valid) | (blk_i == own)
        parts = [qh, jnp.where(ok, 0.0, NEG_INF).astype(BF16)]
        if nbp < LANES:
            parts.append(jnp.full((LANES - nbp, nq), NEG_INF, BF16))
        q_aug.append(jnp.concatenate(parts, axis=0))

    def scores(grp, s_ref, causal=False, lo=0):
        kb0 = grp * group
        k_cat = k_ref[pl.ds(jnp.minimum(kb0, nb - group), group)].reshape(keys, LANES)
        blk_id = kb0 + lax.broadcasted_iota(I32, (keys, LANES), 0) // blk
        onehot = jnp.where(lax.broadcasted_iota(I32, (keys, LANES), 1) == blk_id, 1.0, 0.0).astype(BF16)
        k_aug = jnp.concatenate([k_cat, onehot], axis=1)
        mx = []
        for h in range(2):
            s = _dot(k_aug, q_aug[h][:, lo:])
            if causal:
                key_pos = (kb0 - first) * blk + lax.broadcasted_iota(I32, (keys, nq - lo), 0)
                s = jnp.where(key_pos <= lo + lax.broadcasted_iota(I32, (keys, nq - lo), 1), s, NEG_INF)
            s_ref[h, :, lo:] = s
            mx.append(jnp.max(s, axis=0, keepdims=True))
        return tuple(mx)

    def softmax(s_ref, mx, p_ref, st, lo=0):
        out = []
        for h in range(2):
            m_i, _, l_i, acc = st[h]
            m_new = jnp.maximum(m_i[:, lo:], mx[h])
            p_ref[h, :, lo:] = jnp.exp2(s_ref[h, :, lo:] - m_new).astype(BF16)
            alpha = jnp.exp2(m_i[:, lo:] - m_new)
            if lo:
                m_new = jnp.concatenate([m_i[:, :lo], m_new], axis=1)
                alpha = jnp.concatenate([jnp.ones((1, lo), F32), alpha], axis=1)
            out.append((m_new, alpha, l_i, acc))
        return tuple(out)

    def values(grp, p_ref, st, lo=0):
        out = []
        for h in range(2):
            m_i, alpha, l_i, acc = st[h]
            blocks = [vt_ref[jnp.clip(grp * group + j, 0, nb - 1), h * HEAD_DIM:(h + 1) * HEAD_DIM, :]
                      for j in range(group)]
            ones = jnp.ones((16, keys), BF16)
            pv = _dot(jnp.concatenate([jnp.concatenate(blocks, axis=1), ones], axis=0), p_ref[h, :, lo:])
            if lo:
                pv = jnp.concatenate([jnp.zeros((pv.shape[0], lo), F32), pv], axis=1)
            out.append((m_i, alpha, alpha * l_i + pv[HEAD_DIM:HEAD_DIM + 1], alpha * acc + pv[:HEAD_DIM]))
        return tuple(out)

    mx_first = scores(0, s_a)
    p_b[...] = jnp.zeros_like(p_b)

    def body(t, carry):
        st, mx_a = carry
        mx_b = scores(2 * t + 1, s_b)
        st = values(2 * t - 1, p_b, st)
        st = softmax(s_a, mx_a, p_a, st)
        mx_a = scores(2 * t + 2, s_a)
        st = values(2 * t, p_a, st)
        return softmax(s_b, mx_b, p_b, st), mx_a

    init = (jnp.full((1, nq), -jnp.inf, F32), jnp.ones((1, nq), F32), jnp.zeros((1, nq), F32),
            jnp.zeros((HEAD_DIM, nq), F32))
    g0 = first // group
    st, _ = lax.fori_loop(0, g0 // 2, body, ((init, init), mx_first))
    lo_b = 0
    for pair in range(qblocks // (2 * group)):
        ga = g0 + 2 * pair
        lo = 2 * pair * keys
        mx_a = scores(ga, s_a, causal=True, lo=lo)
        st = values(ga - 1, p_b, st, lo=lo_b)
        lo_b = lo + keys
        mx_b = scores(ga + 1, s_b, causal=True, lo=lo_b)
        st = softmax(s_a, mx_a, p_a, st, lo=lo)
        st = values(ga, p_a, st, lo=lo)
        st = softmax(s_b, mx_b, p_b, st, lo=lo_b)
    st = values(g0 + qblocks // group - 1, p_b, st, lo=lo_b)
    both = jnp.concatenate([st[h][3] / st[h][2] for h in range(2)], axis=0)
    o_ref[...] = both.T.astype(BF16)


def _moba(qt, k, vt4, km, bsz, seq):
    n_tok, attn_w = k.shape
    blk = MOBA_BLOCK
    nb = seq // blk
    group, qblocks = 2, 4
    nbp = -(-nb // 16) * 16
    assert nb % qblocks == 0 and qblocks % (2 * group) == 0 and nbp <= LANES
    hp = attn_w // LANES
    km_pad = jnp.zeros((bsz, nbp, attn_w), F32).at[:, :nb].set(km.reshape(bsz, nb, attn_w))
    k4 = k.reshape(bsz, nb, blk, attn_w)
    steps = nb // qblocks
    return pl.pallas_call(
        functools.partial(_moba_kernel, nb=nb, nbp=nbp, qblocks=qblocks),
        grid=(bsz, hp, steps),
        in_specs=[pl.BlockSpec((None, LANES, qblocks * blk), lambda b, p, i: (b, p, i)),
                  pl.BlockSpec((None, nb, blk, LANES), lambda b, p, i: (b, 0, 0, p)),
                  pl.BlockSpec((None, nb, LANES, blk), lambda b, p, i: (b, 0, p, 0)),
                  pl.BlockSpec((None, nbp, LANES), lambda b, p, i: (b, 0, p))],
        out_specs=pl.BlockSpec((qblocks * blk, LANES), lambda b, p, i: (b * steps + i, p)),
        out_shape=jax.ShapeDtypeStruct((n_tok, attn_w), BF16),
        scratch_shapes=([pltpu.VMEM((2, group * blk, qblocks * blk), F32)] * 2
                        + [pltpu.VMEM((2, group * blk, qblocks * blk), BF16)] * 2),
        compiler_params=_cparams(("arbitrary", "arbitrary", "arbitrary")),
        name="moba",
    )(qt, k4, vt4, km_pad)


def _rms(t, g):
    return t * lax.rsqrt(jnp.mean(t * t, axis=-1, keepdims=True) + NORM_EPS) * g


def _store_slab(ref, val, base=0):
    rows, d = val.shape
    sub = d // LANES
    for s in range(sub):
        ref[pl.ds(base + s, rows, stride=sub), :] = val[:, s * LANES:(s + 1) * LANES]


def _load_slab(ref, rows, sub, base=0):
    return jnp.concatenate([ref[pl.ds(base + s, rows, stride=sub), :] for s in range(sub)], axis=1)


def _merge_kernel(ys_ref, u_ref, ya_ref, gs_ref, ga_ref, x_ref, d_ref, wglu_ref, wsb_ref, wab_ref,
                  wout_ref, postg_ref, gtm_ref, preg_ref, scf_ref, shf_ref, rw_ref, rb_ref,
                  x1_ref, h2_ref, eid_ref, tw_ref, rank_ref, before_ref, cnt_ref, run_ref):
    @pl.when(pl.program_id(0) == 0)
    def _():
        run_ref[...] = jnp.zeros_like(run_ref)

    tm = x_ref.shape[0]
    hm = tm // 2
    halves = [pl.ds(0, hm), pl.ds(hm, hm)]
    sub = x_ref.shape[1] // LANES

    ys = [jax.nn.gelu(ys_ref[r, :] + d_ref[...] * u_ref[r, :].astype(F32)) for r in halves]
    h2s = []
    for r, y in zip(halves, ys):
        y = y * jax.nn.sigmoid(_dot(y.astype(BF16), wglu_ref[...]))
        bs = _dot(y.astype(BF16), wsb_ref[...])
        ba = _dot(ya_ref[r, :], wab_ref[...])
        merged = gs_ref[r, :].astype(F32) * bs + ga_ref[r, :].astype(F32) * ba
        mix = _dot(merged.astype(BF16), wout_ref[...])
        x1 = x_ref[r, :] + gtm_ref[0] * _rms(mix, postg_ref[...])
        x1_ref[r, :] = x1
        h2s.append(_rms(x1, preg_ref[...]) * (1.0 + scf_ref[0]) + shf_ref[0])
    for j, h2 in enumerate(h2s):
        _store_slab(h2_ref, h2, base=j * hm * sub)

    lane = lax.broadcasted_iota(I32, (hm, LANES), 1)
    lane_f = lane.astype(F32)
    r_i = lax.broadcasted_iota(I32, (hm, hm), 0)
    c_i = lax.broadcasted_iota(I32, (hm, hm), 1)
    tri = jnp.where(c_i < r_i, 1.0, 0.0).astype(BF16)
    before_ref[0] = run_ref[...]
    gs = []
    for h2 in h2s:
        h_hi = h2.astype(BF16)
        h_lo = (h2 - h_hi.astype(F32)).astype(BF16)
        gs.append(_dot(jnp.concatenate([h_hi, h_hi, h_lo], axis=1), rw_ref[...]) + rb_ref[...])
    topk = [([], [], []) for _ in halves]
    for _ in range(TOP_K):
        for j in range(2):
            m = jnp.max(gs[j], axis=1, keepdims=True)
            idx = jnp.min(jnp.where(gs[j] == m, lane_f, float(LANES)), axis=1, keepdims=True)
            pick = lane_f == idx
            topk[j][0].append(m)
            topk[j][1].append(pick)
            topk[j][2].append(idx.astype(I32))
            gs[j] = jnp.where(pick, -jnp.inf, gs[j])
    for r, (vals, picks, idxs) in zip(halves, topk):
        exps = [jnp.exp(v - vals[0]) for v in vals]
        tot = exps[0] + exps[1] + exps[2] + exps[3]

        onehot = jnp.where(picks[0] | picks[1] | picks[2] | picks[3], 1.0, 0.0)
        rank_full = _dot(tri, onehot.astype(BF16)) + (run_ref[...] - before_ref[0])
        run_ref[...] = run_ref[...] + jnp.sum(onehot, axis=0, keepdims=True)

        eid = jnp.zeros((hm, LANES), I32)
        tw = jnp.zeros((hm, LANES), F32)
        rk = jnp.zeros((hm, LANES), F32)
        for k in range(TOP_K):
            eid = jnp.where(lane == k, idxs[k], eid)
            tw = jnp.where(lane == k, exps[k] / tot, tw)
            pos = jnp.sum(jnp.where(picks[k], rank_full, 0.0), axis=1, keepdims=True)
            rk = jnp.where(lane == k, pos, rk)
        eid_ref[r, :] = eid
        tw_ref[r, :] = tw
        rank_ref[r, :] = rk.astype(I32)
    cnt_ref[...] = run_ref[...]


def _merge(ys, u, ya, gs, ga, x2, ssm_d, w_glu, w_sb, w_ab, w_out, post_g, gt_m, pre_g, sc_f, sh_f,
           router_w, router_b, seq, tm):
    n_tok, d = x2.shape
    sw = ys.shape[1]
    aw = ya.shape[1]
    ne = router_w.shape[1]
    per_b = seq // tm
    rw = jnp.zeros((d, LANES), F32).at[:, :ne].set(router_w)
    rw_hi = rw.astype(BF16)
    rw_lo = (rw - rw_hi.astype(F32)).astype(BF16)
    rw = jnp.concatenate([rw_hi, rw_lo, rw_hi], axis=0)
    rb = jnp.full((1, LANES), NEG_INF, F32).at[0, :ne].set(router_b)
    tok = lambda w: pl.BlockSpec((tm, w), lambda i: (i, 0))
    full = lambda a, b: pl.BlockSpec((a, b), lambda i: (0, 0))
    bvec = pl.BlockSpec((1, 1, d), lambda i: (i // per_b, 0, 0))
    return pl.pallas_call(
        _merge_kernel,
        grid=(n_tok // tm,),
        in_specs=[tok(sw), tok(sw), tok(aw), tok(d), tok(d), tok(d),
                  full(1, sw), full(sw, sw), full(sw, d), full(aw, d), full(d, d),
                  full(1, d), bvec, full(1, d), bvec, bvec, full(3 * d, LANES), full(1, LANES)],
        out_specs=[tok(d), pl.BlockSpec((tm * (d // LANES), LANES), lambda i: (i, 0)),
                   tok(LANES), tok(LANES), tok(LANES),
                   pl.BlockSpec((1, 1, LANES), lambda i: (i, 0, 0)), full(1, LANES)],
        out_shape=[jax.ShapeDtypeStruct((n_tok, d), F32),
                   jax.ShapeDtypeStruct((n_tok * (d // LANES), LANES), F32),
                   jax.ShapeDtypeStruct((n_tok, LANES), I32),
                   jax.ShapeDtypeStruct((n_tok, LANES), F32),
                   jax.ShapeDtypeStruct((n_tok, LANES), I32),
                   jax.ShapeDtypeStruct((n_tok // tm, 1, LANES), F32),
                   jax.ShapeDtypeStruct((1, LANES), F32)],
        scratch_shapes=[pltpu.VMEM((1, LANES), F32)],
        compiler_params=_cparams(("arbitrary",)),
        name="merge",
    )(ys, u, ya, gs, ga, x2, ssm_d.reshape(1, sw), w_glu.astype(BF16), w_sb.astype(BF16),
      w_ab.astype(BF16), w_out.astype(BF16), post_g.reshape(1, d), gt_m, pre_g.reshape(1, d),
      sc_f, sh_f, rw, rb)


def _chunk_copies(cnt_ref, start_ref, tile, ne, max_rows, make):
    def per_expert(e, off):
        c, start = cnt_ref[tile * ne + e], start_ref[tile * ne + e]
        for b in range(max_rows.bit_length()):
            n = 1 << b

            @pl.when((c >> b) & 1 == 1)
            def _():
                make(off + (c & (n - 1)), start + (c & (n - 1)), n)
        return off + c

    lax.fori_loop(0, ne, per_expert, 0)


def _dispatch_kernel(cnt_ref, start_ref, pad_start_ref, pad_cnt_ref, tail_ref, loct_ref, h_ref, xs_ref,
                     buf, zero_ref, sem, zsem, *, tm, sub, bm, n_rows, ne):
    i = pl.program_id(0)
    last = pl.num_programs(0) - 1
    slot = i % 2
    n_loc = tm * TOP_K

    def row(ref, r, n=1):
        return ref.at[pl.ds(pl.multiple_of(r * sub, sub), n * sub)]

    def wait_slot(s):
        pltpu.make_async_copy(buf.at[s], row(xs_ref, 0, n_loc), sem.at[s]).wait()

    def zero_copies(do):
        def per_expert(e, _):
            start, cnt = pad_start_ref[e], pad_cnt_ref[e]
            for b in range(bm.bit_length() - 1):
                n = 1 << b

                @pl.when((cnt >> b) & 1 == 1)
                def _():
                    do(pltpu.make_async_copy(row(zero_ref, 0, n), row(xs_ref, start + (cnt & (n - 1)), n), zsem))
            return 0

        lax.fori_loop(0, ne, per_expert, 0)

        def per_block(j, _):
            r = tail_ref[0] + j * bm

            @pl.when(r < n_rows)
            def _():
                do(pltpu.make_async_copy(row(zero_ref, 0, bm), row(xs_ref, r, bm), zsem))
            return 0

        lax.fori_loop(0, ne, per_block, 0)

    @pl.when(i == 0)
    def _():
        zero_ref[...] = jnp.zeros_like(zero_ref)
        zero_copies(lambda cp: cp.start())

    @pl.when(i >= 2)
    def _():
        wait_slot(slot)

    loct = loct_ref[0]
    row_i = lax.broadcasted_iota(I32, (n_loc, tm), 0)
    place = row_i == loct[0:1, :]
    for k in range(1, TOP_K):
        place = place | (row_i == loct[k:k + 1, :])
    grouped = _dot(jnp.where(place, 1.0, 0.0).astype(BF16), _load_slab(h_ref, tm, sub).astype(BF16))
    _store_slab(buf.at[slot], grouped)
    _chunk_copies(cnt_ref, start_ref, i, ne, tm,
                  lambda loc, glob, n: pltpu.make_async_copy(row(buf.at[slot], loc, n), row(xs_ref, glob, n),
                                                             sem.at[slot]).start())

    @pl.when(i == last)
    def _():
        wait_slot(slot)

    @pl.when((i == last) & (i >= 1))
    def _():
        wait_slot(1 - slot)

    @pl.when(i == 0)
    def _():
        zero_copies(lambda cp: cp.wait())


def _dispatch(h_slab, loc, cnt_tile, start_tile, pad_start, pad_cnt, tail, n_rows, sub, bm, tm):
    n_tok = loc.shape[0]
    nt = n_tok // tm
    ne = pad_start.shape[0]
    loct = jnp.full((nt, 8, tm), -1, I32).at[:, :TOP_K].set(loc.reshape(nt, tm, TOP_K).transpose(0, 2, 1))
    grid_spec = pltpu.PrefetchScalarGridSpec(
        num_scalar_prefetch=5,
        grid=(nt,),
        in_specs=[pl.BlockSpec((1, 8, tm), lambda i, *_: (i, 0, 0)),
                  pl.BlockSpec((tm * sub, LANES), lambda i, *_: (i, 0))],
        out_specs=pl.BlockSpec(memory_space=pl.ANY),
        scratch_shapes=[pltpu.VMEM((2, TOP_K * tm * sub, LANES), F32), pltpu.VMEM((bm * sub, LANES), F32),
                        pltpu.SemaphoreType.DMA((2,)), pltpu.SemaphoreType.DMA(())],
    )
    return pl.pallas_call(
        functools.partial(_dispatch_kernel, tm=tm, sub=sub, bm=bm, n_rows=n_rows, ne=ne),
        grid_spec=grid_spec,
        out_shape=jax.ShapeDtypeStruct((n_rows * sub, LANES), F32),
        compiler_params=_cparams(("arbitrary",)),
        name="dispatch",
    )(cnt_tile.reshape(-1), start_tile.reshape(-1), pad_start, pad_cnt, tail, loct, h_slab)


def _expert_kernel(be_ref, nv_ref, x_ref, wg_ref, bg_ref, wu_ref, bu_ref, wd_ref, bd_ref,
                   o_ref, wg_s, wu_s, wd_s, *, bm, sub):
    i = pl.program_id(0)
    prev = be_ref[jnp.maximum(i - 1, 0)]
    changed = (i == 0) | (be_ref[i] != prev)

    @pl.when(changed)
    def _():
        wg_s[...] = wg_ref[0].astype(BF16)
        wu_s[...] = wu_ref[0].astype(BF16)
        wd_s[...] = wd_ref[0].astype(BF16)

    @pl.when(i < nv_ref[0])
    def _():
        xb = _load_slab(x_ref, bm, sub).astype(BF16)
        g = _dot(xb, wg_s[...]) + bg_ref[0]
        u = _dot(xb, wu_s[...]) + bu_ref[0]
        g = jnp.minimum(g, SWIGLU_LIMIT)
        u = jnp.clip(u, -SWIGLU_LIMIT, SWIGLU_LIMIT)
        act = g * jax.nn.sigmoid(SWIGLU_ALPHA * g) * (u + 1.0)
        _store_slab(o_ref, _dot(act.astype(BF16), wd_s[...]) + bd_ref[0])

    @pl.when(i >= nv_ref[0])
    def _():
        o_ref[...] = jnp.zeros_like(o_ref)


def _experts(xs, blk_exp, n_valid, w_gate, b_gate, w_up, b_up, w_down, b_down, bm):
    ne, d, f = w_gate.shape
    sub = d // LANES
    n_blocks = xs.shape[0] // (bm * sub)
    wspec = lambda a, b: pl.BlockSpec((1, a, b), lambda i, be, nv: (be[i], 0, 0))
    grid_spec = pltpu.PrefetchScalarGridSpec(
        num_scalar_prefetch=2,
        grid=(n_blocks,),
        in_specs=[pl.BlockSpec((bm * sub, LANES), lambda i, be, nv: (jnp.clip(nv[0] - 1, 0, i), 0)),
                  wspec(d, f), wspec(1, f), wspec(d, f), wspec(1, f), wspec(f, d), wspec(1, d)],
        out_specs=pl.BlockSpec((bm * sub, LANES), lambda i, be, nv: (i, 0)),
        scratch_shapes=[pltpu.VMEM((d, f), BF16), pltpu.VMEM((d, f), BF16), pltpu.VMEM((f, d), BF16)],
    )
    return pl.pallas_call(
        functools.partial(_expert_kernel, bm=bm, sub=sub),
        grid_spec=grid_spec,
        out_shape=jax.ShapeDtypeStruct(xs.shape, F32),
        compiler_params=_cparams(("arbitrary",)),
        name="experts",
    )(blk_exp, n_valid, xs, w_gate, b_gate.reshape(ne, 1, f), w_up, b_up.reshape(ne, 1, f),
      w_down, b_down.reshape(ne, 1, d))


def _combine_kernel(cnt_ref, start_ref, yb_ref, loc_ref, tw_ref, x1_ref, g_ref, gt_ref, o_ref, buf, sem,
                    *, tm, sub, ne):
    i = pl.program_id(0)
    n_loc = tm * TOP_K

    def start_tile(tile, slot):
        def make(loc, glob, n):
            src = yb_ref.at[pl.ds(pl.multiple_of(glob * sub, sub), n * sub)]
            dst = buf.at[slot, pl.ds(pl.multiple_of(loc * sub, sub), n * sub)]
            pltpu.make_async_copy(src, dst, sem.at[slot]).start()

        _chunk_copies(cnt_ref, start_ref, tile, ne, tm, make)

    slot = i % 2

    @pl.when(i == 0)
    def _():
        start_tile(0, 0)

    @pl.when(i + 1 < pl.num_programs(0))
    def _():
        start_tile(i + 1, 1 - slot)

    pltpu.make_async_copy(yb_ref.at[pl.ds(0, n_loc * sub)], buf.at[slot], sem.at[slot]).wait()
    rows = _load_slab(buf.at[slot], n_loc, sub).astype(BF16)
    loc, tw = loc_ref[...], tw_ref[...]
    col_i = lax.broadcasted_iota(I32, (tm, n_loc), 1)
    w = jnp.zeros((tm, n_loc), F32)
    for k in range(TOP_K):
        w = jnp.where(col_i == loc[:, k:k + 1], tw[:, k:k + 1], w)
    ff = _dot(w.astype(BF16), rows)
    o_ref[...] = x1_ref[...] + gt_ref[0] * _rms(ff, g_ref[...])


def _combine(yb, loc, tw, cnt_tile, start_tile, x1, post_g, gt_f, seq, tm):
    n_tok, d = x1.shape
    sub = d // LANES
    nt = n_tok // tm
    ne = cnt_tile.shape[1]
    per_b = seq // tm
    loc_l = jnp.full((n_tok, LANES), -1, I32).at[:, :TOP_K].set(loc)
    grid_spec = pltpu.PrefetchScalarGridSpec(
        num_scalar_prefetch=2,
        grid=(nt,),
        in_specs=[pl.BlockSpec(memory_space=pl.ANY),
                  pl.BlockSpec((tm, LANES), lambda i, *_: (i, 0)),
                  pl.BlockSpec((tm, LANES), lambda i, *_: (i, 0)),
                  pl.BlockSpec((tm, d), lambda i, *_: (i, 0)),
                  pl.BlockSpec((1, d), lambda i, *_: (0, 0)),
                  pl.BlockSpec((1, 1, d), lambda i, *_: (i // per_b, 0, 0))],
        out_specs=pl.BlockSpec((tm, d), lambda i, *_: (i, 0)),
        scratch_shapes=[pltpu.VMEM((2, TOP_K * tm * sub, LANES), F32), pltpu.SemaphoreType.DMA((2,))],
    )
    return pl.pallas_call(
        functools.partial(_combine_kernel, tm=tm, sub=sub, ne=ne),
        grid_spec=grid_spec,
        out_shape=jax.ShapeDtypeStruct((n_tok, d), F32),
        compiler_params=_cparams(("arbitrary",)),
        name="combine",
    )(cnt_tile.reshape(-1), start_tile.reshape(-1), yb, loc_l, tw, x1, post_g.reshape(1, d), gt_f)


def kernel(x, c, positions, ada_w, ada_b, mix_pre_g, mix_post_g, ffn_pre_g, ffn_post_g, w_in, ssm_lam_re, ssm_lam_im, ssm_log_dt, ssm_b_re, ssm_b_im, ssm_c_re, ssm_c_im, ssm_d, ssm_w_glu, w_ssm_branch, w_attn_branch, w_out, router_w, router_b, w_gate, b_gate, w_up, b_up, w_down, b_down):
    bsz, seq, d = x.shape
    depth = ada_w.shape[0]
    n_tok = bsz * seq
    bm = EXPERT_ROWS
    sub = d // LANES
    xcur = x.reshape(n_tok, d)
    pos = positions.reshape(n_tok, 1).astype(I32)
    for l in range(depth):
        ada = _ada(c, ada_w[l], ada_b[l])
        sh_m, sc_m, gt_m, sh_f, sc_f, gt_f = [a.reshape(bsz, 1, d) for a in jnp.split(ada, 6, axis=-1)]

        u, q, k, v, gs, ga, km = _inproj(xcur, sc_m, sh_m, mix_pre_g[l].reshape(1, d), pos, w_in[l],
                                         seq, tm=min(512, seq))
        ys = _s5(u, ssm_lam_re[l], ssm_lam_im[l], ssm_log_dt[l], ssm_b_re[l], ssm_b_im[l],
                 ssm_c_re[l], ssm_c_im[l], bsz, seq)
        ya = _moba(q, k, v, km, bsz, seq)
        tm = ROUTE_TILE
        x1, h2, eid, tw, rank, before, cnt = _merge(
            ys, u, ya, gs, ga, xcur, ssm_d[l], ssm_w_glu[l], w_ssm_branch[l], w_attn_branch[l], w_out[l],
            mix_post_g[l], gt_m, ffn_pre_g[l], sc_f, sh_f, router_w[l], router_b[l], seq, tm=tm)

        ne = router_w.shape[-1]
        nt = n_tok // tm
        experts = jnp.arange(ne, dtype=I32)
        counts = cnt[0, :ne].astype(I32)
        padded = (counts + bm - 1) // bm * bm
        p_ends = jnp.cumsum(padded)
        p_starts = p_ends - padded
        before = before.reshape(nt, LANES)[:, :ne].astype(I32)
        cnt_tile = jnp.concatenate([before[1:], counts[None]], axis=0) - before
        start_tile = p_starts[None, :] + before
        off_tile = jnp.cumsum(cnt_tile, axis=1) - cnt_tile
        onehot = eid[:, :TOP_K].reshape(nt, tm, TOP_K, 1) == experts
        loc = (rank[:, :TOP_K] + jnp.sum(jnp.where(onehot, off_tile[:, None, None, :], 0), axis=-1).reshape(n_tok, TOP_K))
        n_blocks = (n_tok * TOP_K) // bm + ne
        n_rows = n_blocks * bm
        blk_start = jnp.arange(n_blocks, dtype=I32)[:, None] * bm
        blk_exp = jnp.minimum(jnp.sum((blk_start >= p_ends[None, :]).astype(I32), axis=1), ne - 1)
        n_valid = (p_ends[-1:] // bm).astype(I32)

        xs = _dispatch(h2, loc, cnt_tile, start_tile, p_starts + counts, padded - counts, p_ends[-1:],
                       n_rows, sub, bm, tm)
        yb = _experts(xs, blk_exp, n_valid, w_gate[l], b_gate[l], w_up[l], b_up[l], w_down[l], b_down[l], bm)
        xcur = _combine(yb, loc, tw, cnt_tile, start_tile, x1, ffn_post_g[l], gt_f, seq, tm)
    return xcur.reshape(bsz, seq, d).astype(x.dtype)
```

```python
import functools
import math

import jax
import jax.numpy as jnp
from jax import lax
from jax.experimental import pallas as pl
from jax.experimental.pallas import tpu as pltpu

F32 = jnp.float32
BF16 = jnp.bfloat16
I32 = jnp.int32

N_HEADS = 8
HEAD_DIM = 64
ROPE_THETA = 10000.0
MOBA_BLOCK = 256
MOBA_TOPK = 3
SSM_GROUP_SIZE = 16
SSM_GROUPS = 32
SSM_STATE = 64
N_EXPERTS = 32
TOP_K = 4
SWIGLU_ALPHA = 1.702
SWIGLU_LIMIT = 7.0
NORM_EPS = 1e-6
NEG_INF = -1e30

LANES = 128
SSM_CHUNK = 64
EXPERT_ROWS = 512
ROUTE_TILE = 512
VMEM_LIMIT = 56 * 1024 * 1024
HIGHEST = lax.Precision.HIGHEST
Q_SCALE = HEAD_DIM ** -0.5 * math.log2(math.e)


def _cparams(sem):
    return pltpu.CompilerParams(dimension_semantics=sem, vmem_limit_bytes=VMEM_LIMIT)


def _dot(a, b, **kw):
    return jnp.dot(a, b, preferred_element_type=F32, **kw)


def _dot_t(a, b, **kw):
    return lax.dot_general(a, b, (((1,), (1,)), ((), ())), preferred_element_type=F32, **kw)


def _ada_kernel(c_ref, w_ref, b_ref, o_ref):
    c = c_ref[...]
    cond = c * jax.nn.sigmoid(c)
    o_ref[...] = _dot(cond, w_ref[...], precision=HIGHEST) + b_ref[...]


def _ada(c, ada_w, ada_b):
    bsz, d = c.shape
    n = ada_w.shape[1]
    c8 = jnp.zeros((8, d), F32).at[:bsz].set(c)
    out = pl.pallas_call(
        _ada_kernel,
        grid=(n // d,),
        in_specs=[pl.BlockSpec((8, d), lambda j: (0, 0)),
                  pl.BlockSpec((d, d), lambda j: (0, j)),
                  pl.BlockSpec((1, d), lambda j: (0, j))],
        out_specs=pl.BlockSpec((8, d), lambda j: (0, j)),
        out_shape=jax.ShapeDtypeStruct((8, n), F32),
        compiler_params=_cparams(("arbitrary",)),
        name="ada",
    )(c8, ada_w, ada_b.reshape(1, n))
    return out[:bsz]


def _transpose_kernel(w_ref, o_ref):
    o_ref[...] = w_ref[...].T.astype(BF16)


def _inproj_kernel(x_ref, sc_ref, sh_ref, g_ref, pos_ref, invf_ref, post_ref, invft_ref, w_ref, wt_ref,
                   u_ref, qt_ref, k_ref, vt_ref, gs_ref, ga_ref, km_ref, *, ssm_w, attn_w, d_model):
    x = x_ref[...]
    ms = jnp.mean(x * x, axis=-1, keepdims=True)
    xn = x * lax.rsqrt(ms + NORM_EPS) * g_ref[...]
    h = (xn * (1.0 + sc_ref[0]) + sh_ref[0]).astype(BF16)

    def proj(lo, width):
        return _dot(h, w_ref[:, lo:lo + width])

    tm = x.shape[0]
    ang = pos_ref[...].astype(F32) * invf_ref[...]
    reps = attn_w // LANES
    cos = jnp.concatenate([jnp.cos(ang)] * reps, axis=1)
    sin = jnp.concatenate([jnp.sin(ang)] * reps, axis=1)
    lane = lax.broadcasted_iota(I32, (1, attn_w), 1)
    first = (lane % HEAD_DIM) < (HEAD_DIM // 2)
    sin = jnp.where(first, -sin, sin)
    ang_t = invft_ref[...] * post_ref[0].astype(F32)
    cos_t, sin_t = jnp.cos(ang_t), jnp.sin(ang_t)

    u_ref[...] = proj(0, ssm_w)
    gs_ref[...] = jax.nn.sigmoid(proj(ssm_w + 3 * attn_w, d_model)).astype(BF16)
    ga_ref[...] = jax.nn.sigmoid(proj(ssm_w + 3 * attn_w + d_model, d_model)).astype(BF16)
    vt = _dot_t(wt_ref[attn_w:2 * attn_w, :], h).astype(BF16)
    nblk = tm // MOBA_BLOCK
    for j in range(nblk):
        vt_ref[j] = vt[:, j * MOBA_BLOCK:(j + 1) * MOBA_BLOCK]

    def rope(t):
        rot = jnp.where(first, pltpu.roll(t, attn_w - HEAD_DIM // 2, axis=1),
                        pltpu.roll(t, HEAD_DIM // 2, axis=1))
        return t * cos + rot * sin

    k = rope(proj(ssm_w + attn_w, attn_w))
    k_ref[...] = k.astype(BF16)
    km_ref[0] = jnp.mean(k.reshape(nblk, MOBA_BLOCK, attn_w), axis=1)

    half = HEAD_DIM // 2
    qt = _dot_t(wt_ref[0:attn_w, :], h).reshape(N_HEADS, 2, half, tm)
    t1, t2 = qt[:, 0], qt[:, 1]
    qt = jnp.stack([t1 * cos_t - t2 * sin_t, t2 * cos_t + t1 * sin_t], axis=1).reshape(attn_w, tm)
    qt_ref[...] = (qt * Q_SCALE).astype(BF16)


def _inproj(x2, sc, sh, g, pos, w_in, seq, tm):
    n_tok, d = x2.shape
    attn_w = N_HEADS * HEAD_DIM
    ssm_w = SSM_GROUPS * SSM_GROUP_SIZE
    in_w = w_in.shape[1]
    half = HEAD_DIM // 2
    inv_freq = ROPE_THETA ** (-jnp.arange(half, dtype=F32) / half)
    invf = jnp.tile(inv_freq, LANES // half).reshape(1, LANES)
    nt = n_tok // tm
    per_b = seq // tm
    nblk = tm // MOBA_BLOCK
    bsz = n_tok // seq
    blk = MOBA_BLOCK
    assert ssm_w % attn_w == 0
    q_blk, v_blk = ssm_w // attn_w, ssm_w // attn_w + 2
    wt = pl.pallas_call(
        _transpose_kernel,
        grid=(2,),
        in_specs=[pl.BlockSpec((d, attn_w), lambda j: (0, q_blk + j * (v_blk - q_blk)))],
        out_specs=pl.BlockSpec((attn_w, d), lambda j: (j, 0)),
        out_shape=jax.ShapeDtypeStruct((2 * attn_w, d), BF16),
        compiler_params=_cparams(("arbitrary",)),
        name="wt",
    )(w_in)
    tok = lambda w: pl.BlockSpec((tm, w), lambda i: (i, 0))
    full = lambda a, b: pl.BlockSpec((a, b), lambda i: (0, 0))
    bvec = pl.BlockSpec((1, 1, d), lambda i: (i // per_b, 0, 0))
    outs = pl.pallas_call(
        functools.partial(_inproj_kernel, ssm_w=ssm_w, attn_w=attn_w, d_model=d),
        grid=(nt,),
        in_specs=[tok(d), bvec, bvec, full(1, d), tok(1), full(1, LANES),
                  pl.BlockSpec((1, 1, tm), lambda i: (i, 0, 0)), full(half, 1),
                  full(d, in_w), full(2 * attn_w, d)],
        out_specs=[tok(ssm_w),
                   pl.BlockSpec((None, attn_w, tm), lambda i: (i // per_b, 0, i % per_b)),
                   tok(attn_w),
                   pl.BlockSpec((None, nblk, attn_w, blk), lambda i: (i // per_b, i % per_b, 0, 0)),
                   tok(d), tok(d),
                   pl.BlockSpec((1, nblk, attn_w), lambda i: (i, 0, 0))],
        out_shape=[jax.ShapeDtypeStruct((n_tok, ssm_w), F32),
                   jax.ShapeDtypeStruct((bsz, attn_w, seq), BF16),
                   jax.ShapeDtypeStruct((n_tok, attn_w), BF16),
                   jax.ShapeDtypeStruct((bsz, seq // blk, attn_w, blk), BF16),
                   jax.ShapeDtypeStruct((n_tok, d), BF16),
                   jax.ShapeDtypeStruct((n_tok, d), BF16),
                   jax.ShapeDtypeStruct((nt, nblk, attn_w), F32)],
        compiler_params=_cparams(("arbitrary",)),
        name="inproj",
    )(x2, sc, sh, g, pos, invf, pos.reshape(nt, 1, tm), inv_freq.reshape(half, 1), w_in.astype(BF16), wt)
    return outs


def _s5_group(x, zr, zi, lr, li, ca, cb, ba, bb, t_ref, *, chunk, n_chunks):
    gs = SSM_GROUP_SIZE

    def powers(tau):
        mag = jnp.exp(tau * zr)
        return mag * jnp.cos(tau * zi), mag * jnp.sin(tau * zi)

    tau = lax.broadcasted_iota(I32, (chunk + 8, LANES), 0).astype(F32)
    e_re, e_im = powers(tau)
    lb_re, lb_im = e_re[1:2], e_im[1:2]
    den = lr * lr + li * li
    a, b = lb_re - 1.0, lb_im
    cf_re = (a * lr + b * li) / den
    cf_im = (b * lr - a * li) / den
    bri = cf_re * ba + cf_im * bb
    bri_sw = cf_re * bb - cf_im * ba

    cpow =(e_re[:chunk + 1, None, :] * ca[None] + e_im[:chunk + 1, None, :] * cb[None])
    cpow = cpow.reshape((chunk + 1) * gs, LANES)
    width = chunk * gs
    r = _dot_t(bri, cpow[:width], precision=HIGHEST)
    col = lax.broadcasted_iota(I32, (gs, width), 1)
    t_ref[0:gs, :] = r.astype(BF16)
    for j in range(1, chunk):
        shifted = jnp.where(col >= gs * j, pltpu.roll(r, gs * j, axis=1), 0.0)
        t_ref[gs * j:gs * (j + 1), :] = shifted.astype(BF16)

    y = _dot(x, t_ref[...])

    tau_rev = (chunk - 1) - lax.broadcasted_iota(I32, (chunk, LANES), 0)
    r_re, r_im = powers(tau_rev.astype(F32))
    bst = r_re[:, None, :] * bri[None] + r_im[:, None, :] * bri_sw[None]
    bst = bst.reshape(width, LANES).astype(BF16)
    s = _dot(x, bst)

    rows = s.shape[0]
    n_idx = lax.broadcasted_iota(I32, (rows, LANES), 0) % n_chunks
    lane = lax.broadcasted_iota(I32, (1, LANES), 1)
    half = LANES // 2

    def cmul(v, p_re, p_im):
        return v * p_re + pltpu.roll(v, half, axis=1) * jnp.where(lane < half, -p_im, p_im)

    sh = 1
    while sh < n_chunks:
        p_re, p_im = powers(jnp.full((1, LANES), float(chunk * sh), F32))
        prev = jnp.where(n_idx >= sh, pltpu.roll(s, sh, axis=0), 0.0)
        s = s + cmul(prev, p_re, p_im)
        sh *= 2
    s_in = jnp.where(n_idx >= 1, pltpu.roll(s, 1, axis=0), 0.0)
    return y + _dot_t(s_in.astype(BF16), cpow[gs:gs + width].astype(BF16))


def _s5_kernel(u_ref, sel_ref, zr_ref, zi_ref, lr_ref, li_ref, ca_ref, cb_ref, ba_ref, bb_ref,
               y_ref, x_scr, y_scr, t_ref, *, chunk, n_chunks, rows):
    gpt = LANES // SSM_GROUP_SIZE

    def plane(j):
        return pl.ds(j, rows, stride=chunk)

    for jt in range(chunk // gpt):
        planes = jnp.concatenate([u_ref[plane(jt * gpt + jj), :] for jj in range(gpt)], axis=1)
        grouped = _dot(planes.astype(BF16), sel_ref[...])
        for g in range(gpt):
            x_scr[g, :, jt * LANES:(jt + 1) * LANES] = grouped[:, g * LANES:(g + 1) * LANES].astype(BF16)

    def per_group(g, _):
        y = _s5_group(x_scr[g], zr_ref[g], zi_ref[g], lr_ref[g], li_ref[g], ca_ref[g], cb_ref[g], ba_ref[g],
                      bb_ref[g], t_ref, chunk=chunk, n_chunks=n_chunks)
        y_scr[g] = y.astype(BF16)
        return 0

    lax.fori_loop(0, gpt, per_group, 0)

    for jt in range(chunk // gpt):
        grouped = jnp.concatenate([y_scr[g, :, jt * LANES:(jt + 1) * LANES] for g in range(gpt)], axis=1)
        planes = _dot(grouped, sel_ref[...])
        for jj in range(gpt):
            y_ref[plane(jt * gpt + jj), :] = planes[:, jj * LANES:(jj + 1) * LANES]


def _s5(u, lam_re, lam_im, log_dt, b_re, b_im, c_re, c_im, bsz, seq):
    g_n, gs, p = SSM_GROUPS, SSM_GROUP_SIZE, SSM_STATE
    chunk = SSM_CHUNK
    n_chunks = seq // chunk
    rows = bsz * n_chunks
    width = chunk * gs
    gpt = LANES // gs
    assert chunk % gpt == 0 and g_n % gpt == 0 and 2 * p == LANES
    dt = jnp.exp(log_dt.astype(F32))[:, None]
    dup = lambda a: jnp.concatenate([a, a], axis=-1).reshape(g_n, 1, 2 * p)
    zr, zi = dup(lam_re * dt), dup(lam_im * dt)
    lr, li = dup(lam_re), dup(lam_im)
    bt_re, bt_im = b_re.transpose(0, 2, 1), b_im.transpose(0, 2, 1)
    ba = jnp.concatenate([bt_re, bt_im], axis=-1)
    bb = jnp.concatenate([-bt_im, bt_re], axis=-1)
    ca = jnp.concatenate([c_re, -c_im], axis=-1)
    cb = jnp.concatenate([-c_im, -c_re], axis=-1)
    idx = jnp.arange(gpt * LANES, dtype=I32)
    swapped = ((idx // gs) % gpt) * LANES + (idx // LANES) * gs + idx % gs
    sel = (swapped[:, None] == idx[None, :]).astype(BF16)
    tile = pl.BlockSpec((bsz * seq, LANES), lambda t: (0, t))
    vec = pl.BlockSpec((gpt, 1, 2 * p), lambda t: (t, 0, 0))
    mat = pl.BlockSpec((gpt, gs, 2 * p), lambda t: (t, 0, 0))
    return pl.pallas_call(
        functools.partial(_s5_kernel, chunk=chunk, n_chunks=n_chunks, rows=rows),
        grid=(g_n // gpt,),
        in_specs=[tile, pl.BlockSpec((gpt * LANES, gpt * LANES), lambda t: (0, 0)),
                  vec, vec, vec, vec, mat, mat, mat, mat],
        out_specs=tile,
        out_shape=jax.ShapeDtypeStruct((bsz * seq, g_n * gs), F32),
        scratch_shapes=[pltpu.VMEM((gpt, rows, width), BF16), pltpu.VMEM((gpt, rows, width), BF16),
                        pltpu.VMEM((width, width), BF16)],
        compiler_params=_cparams(("arbitrary",)),
        name="s5",
    )(u, sel, zr, zi, lr, li, ca, cb, ba, bb)


def _moba_kernel(qt_ref, k_ref, vt_ref, km_ref, o_ref, s_a, s_b, p_a, p_b, *, nb, nbp, qblocks):
    blk = MOBA_BLOCK
    group = 2
    keys = group * blk
    nq = qblocks * blk
    first = qblocks * pl.program_id(2)
    qt = qt_ref[...]
    km = km_ref[...]
    km_hi = km.astype(BF16)
    km_lo = (km - km_hi.astype(F32)).astype(BF16)
    dim_i = lax.broadcasted_iota(I32, (LANES, nq), 0)
    blk_i = lax.broadcasted_iota(I32, (nbp, nq), 0)
    blk_f = blk_i.astype(F32)
    own = first + lax.broadcasted_iota(I32, (nbp, nq), 1) // blk
    valid = blk_i < own

    q_aug = []
    for h in range(2):
        qh = jnp.where((dim_i >= h * HEAD_DIM) & (dim_i < (h + 1) * HEAD_DIM), qt, jnp.zeros_like(qt))
        g = jnp.where(valid, _dot(km_hi, qh) + _dot(km_lo, qh), NEG_INF)
        sel = jnp.zeros((nbp, nq), jnp.bool_)
        for _ in range(MOBA_TOPK):
            m = jnp.max(g, axis=0, keepdims=True)
            idx = jnp.min(jnp.where(g == m, blk_f, float(nbp)), axis=0, keepdims=True)
            pick = blk_f == idx
            sel = sel | pick
            g = jnp.where(pick, -jnp.inf, g)
        ok = (sel & valid) | (blk_i == own)
        parts = [qh, jnp.where(ok, 0.0, NEG_INF).astype(BF16)]
        if nbp < LANES:
            parts.append(jnp.full((LANES - nbp, nq), NEG_INF, BF16))
        q_aug.append(jnp.concatenate(parts, axis=0))

    def scores(grp, s_ref, causal=False, lo=0):
        kb0 = grp * group
        k_cat = k_ref[pl.ds(jnp.minimum(kb0, nb - group), group)].reshape(keys, LANES)
        blk_id = kb0 + lax.broadcasted_iota(I32, (keys, LANES), 0) // blk
        onehot = jnp.where(lax.broadcasted_iota(I32, (keys, LANES), 1) == blk_id, 1.0, 0.0).astype(BF16)
        k_aug = jnp.concatenate([k_cat, onehot], axis=1)
        mx = []
        for h in range(2):
            s = _dot(k_aug, q_aug[h][:, lo:])
            if causal:
                key_pos = (kb0 - first) * blk + lax.broadcasted_iota(I32, (keys, nq - lo), 0)
                s = jnp.where(key_pos <= lo + lax.broadcasted_iota(I32, (keys, nq - lo), 1), s, NEG_INF)
            s_ref[h, :, lo:] = s
            mx.append(jnp.max(s, axis=0, keepdims=True))
        return tuple(mx)

    def softmax(s_ref, mx, p_ref, st, lo=0):
        out = []
        for h in range(2):
            m_i, _, l_i, acc = st[h]
            m_new = jnp.maximum(m_i[:, lo:], mx[h])
            p_ref[h, :, lo:] = jnp.exp2(s_ref[h, :, lo:] - m_new).astype(BF16)
            alpha = jnp.exp2(m_i[:, lo:] - m_new)
            if lo:
                m_new = jnp.concatenate([m_i[:, :lo], m_new], axis=1)
                alpha = jnp.concatenate([jnp.ones((1, lo), F32), alpha], axis=1)
            out.append((m_new, alpha, l_i, acc))
        return tuple(out)

    def values(grp, p_ref, st, lo=0):
        out = []
        for h in range(2):
            m_i, alpha, l_i, acc = st[h]
            blocks = [vt_ref[jnp.clip(grp * group + j, 0, nb - 1), h * HEAD_DIM:(h + 1) * HEAD_DIM, :]
                      for j in range(group)]
            ones = jnp.ones((16, keys), BF16)
            pv = _dot(jnp.concatenate([jnp.concatenate(blocks, axis=1), ones], axis=0), p_ref[h, :, lo:])
            if lo:
                pv = jnp.concatenate([jnp.zeros((pv.shape[0], lo), F32), pv], axis=1)
            out.append((m_i, alpha, alpha * l_i + pv[HEAD_DIM:HEAD_DIM + 1], alpha * acc + pv[:HEAD_DIM]))
        return tuple(out)

    mx_first = scores(0, s_a)
    p_b[...] = jnp.zeros_like(p_b)

    def body(t, carry):
        st, mx_a = carry
        mx_b = scores(2 * t + 1, s_b)
        st = values(2 * t - 1, p_b, st)
        st = softmax(s_a, mx_a, p_a, st)
        mx_a = scores(2 * t + 2, s_a)
        st = values(2 * t, p_a, st)
        return softmax(s_b, mx_b, p_b, st), mx_a

    init = (jnp.full((1, nq), -jnp.inf, F32), jnp.ones((1, nq), F32), jnp.zeros((1, nq), F32),
            jnp.zeros((HEAD_DIM, nq), F32))
    g0 = first // group
    st, _ = lax.fori_loop(0, g0 // 2, body, ((init, init), mx_first))
    lo_b = 0
    for pair in range(qblocks // (2 * group)):
        ga = g0 + 2 * pair
        lo = 2 * pair * keys
        mx_a = scores(ga, s_a, causal=True, lo=lo)
        st = values(ga - 1, p_b, st, lo=lo_b)
        lo_b = lo + keys
        mx_b = scores(ga + 1, s_b, causal=True, lo=lo_b)
        st = softmax(s_a, mx_a, p_a, st, lo=lo)
        st = values(ga, p_a, st, lo=lo)
        st = softmax(s_b, mx_b, p_b, st, lo=lo_b)
    st = values(g0 + qblocks // group - 1, p_b, st, lo=lo_b)
    both = jnp.concatenate([st[h][3] / st[h][2] for h in range(2)], axis=0)
    o_ref[...] = both.T.astype(BF16)


def _moba(qt, k, vt4, km, bsz, seq):
    n_tok, attn_w = k.shape
    blk = MOBA_BLOCK
    nb = seq // blk
    group, qblocks = 2, 4
    nbp = -(-nb // 16) * 16
    assert nb % qblocks == 0 and qblocks % (2 * group) == 0 and nbp <= LANES
    hp = attn_w // LANES
    km_pad = jnp.zeros((bsz, nbp, attn_w), F32).at[:, :nb].set(km.reshape(bsz, nb, attn_w))
    k4 = k.reshape(bsz, nb, blk, attn_w)
    steps = nb // qblocks
    return pl.pallas_call(
        functools.partial(_moba_kernel, nb=nb, nbp=nbp, qblocks=qblocks),
        grid=(bsz, hp, steps),
        in_specs=[pl.BlockSpec((None, LANES, qblocks * blk), lambda b, p, i: (b, p, i)),
                  pl.BlockSpec((None, nb, blk, LANES), lambda b, p, i: (b, 0, 0, p)),
                  pl.BlockSpec((None, nb, LANES, blk), lambda b, p, i: (b, 0, p, 0)),
                  pl.BlockSpec((None, nbp, LANES), lambda b, p, i: (b, 0, p))],
        out_specs=pl.BlockSpec((qblocks * blk, LANES), lambda b, p, i: (b * steps + i, p)),
        out_shape=jax.ShapeDtypeStruct((n_tok, attn_w), BF16),
        scratch_shapes=([pltpu.VMEM((2, group * blk, qblocks * blk), F32)] * 2
                        + [pltpu.VMEM((2, group * blk, qblocks * blk), BF16)] * 2),
        compiler_params=_cparams(("arbitrary", "arbitrary", "arbitrary")),
        name="moba",
    )(qt, k4, vt4, km_pad)


def _rms(t, g):
    return t * lax.rsqrt(jnp.mean(t * t, axis=-1, keepdims=True) + NORM_EPS) * g


def _store_slab(ref, val, base=0):
    rows, d = val.shape
    sub = d // LANES
    for s in range(sub):
        ref[pl.ds(base + s, rows, stride=sub), :] = val[:, s * LANES:(s + 1) * LANES]


def _load_slab(ref, rows, sub, base=0):
    return jnp.concatenate([ref[pl.ds(base + s, rows, stride=sub), :] for s in range(sub)], axis=1)


def _merge_kernel(ys_ref, u_ref, ya_ref, gs_ref, ga_ref, x_ref, d_ref, wglu_ref, wsb_ref, wab_ref,
                  wout_ref, postg_ref, gtm_ref, preg_ref, scf_ref, shf_ref, rw_ref, rb_ref,
                  x1_ref, h2_ref, eid_ref, tw_ref, rank_ref, before_ref, cnt_ref, run_ref):
    @pl.when(pl.program_id(0) == 0)
    def _():
        run_ref[...] = jnp.zeros_like(run_ref)

    tm = x_ref.shape[0]
    hm = tm // 2
    halves = [pl.ds(0, hm), pl.ds(hm, hm)]
    sub = x_ref.shape[1] // LANES

    ys = [jax.nn.gelu(ys_ref[r, :] + d_ref[...] * u_ref[r, :].astype(F32)) for r in halves]
    h2s = []
    for r, y in zip(halves, ys):
        y = y * jax.nn.sigmoid(_dot(y.astype(BF16), wglu_ref[...]))
        bs = _dot(y.astype(BF16), wsb_ref[...])
        ba = _dot(ya_ref[r, :], wab_ref[...])
        merged = gs_ref[r, :].astype(F32) * bs + ga_ref[r, :].astype(F32) * ba
        mix = _dot(merged.astype(BF16), wout_ref[...])
        x1 = x_ref[r, :] + gtm_ref[0] * _rms(mix, postg_ref[...])
        x1_ref[r, :] = x1
        h2s.append(_rms(x1, preg_ref[...]) * (1.0 + scf_ref[0]) + shf_ref[0])
    for j, h2 in enumerate(h2s):
        _store_slab(h2_ref, h2, base=j * hm * sub)

    lane = lax.broadcasted_iota(I32, (hm, LANES), 1)
    lane_f = lane.astype(F32)
    r_i = lax.broadcasted_iota(I32, (hm, hm), 0)
    c_i = lax.broadcasted_iota(I32, (hm, hm), 1)
    tri = jnp.where(c_i < r_i, 1.0, 0.0).astype(BF16)
    before_ref[0] = run_ref[...]
    gs = []
    for h2 in h2s:
        h_hi = h2.astype(BF16)
        h_lo = (h2 - h_hi.astype(F32)).astype(BF16)
        gs.append(_dot(jnp.concatenate([h_hi, h_hi, h_lo], axis=1), rw_ref[...]) + rb_ref[...])
    topk = [([], [], []) for _ in halves]
    for _ in range(TOP_K):
        for j in range(2):
            m = jnp.max(gs[j], axis=1, keepdims=True)
            idx = jnp.min(jnp.where(gs[j] == m, lane_f, float(LANES)), axis=1, keepdims=True)
            pick = lane_f == idx
            topk[j][0].append(m)
            topk[j][1].append(pick)
            topk[j][2].append(idx.astype(I32))
            gs[j] = jnp.where(pick, -jnp.inf, gs[j])
    for r, (vals, picks, idxs) in zip(halves, topk):
        exps = [jnp.exp(v - vals[0]) for v in vals]
        tot = exps[0] + exps[1] + exps[2] + exps[3]

        onehot = jnp.where(picks[0] | picks[1] | picks[2] | picks[3], 1.0, 0.0)
        rank_full = _dot(tri, onehot.astype(BF16)) + (run_ref[...] - before_ref[0])
        run_ref[...] = run_ref[...] + jnp.sum(onehot, axis=0, keepdims=True)

        eid = jnp.zeros((hm, LANES), I32)
        tw = jnp.zeros((hm, LANES), F32)
        rk = jnp.zeros((hm, LANES), F32)
        for k in range(TOP_K):
            eid = jnp.where(lane == k, idxs[k], eid)
            tw = jnp.where(lane == k, exps[k] / tot, tw)
            pos = jnp.sum(jnp.where(picks[k], rank_full, 0.0), axis=1, keepdims=True)
            rk = jnp.where(lane == k, pos, rk)
        eid_ref[r, :] = eid
        tw_ref[r, :] = tw
        rank_ref[r, :] = rk.astype(I32)
    cnt_ref[...] = run_ref[...]


def _merge(ys, u, ya, gs, ga, x2, ssm_d, w_glu, w_sb, w_ab, w_out, post_g, gt_m, pre_g, sc_f, sh_f,
           router_w, router_b, seq, tm):
    n_tok, d = x2.shape
    sw = ys.shape[1]
    aw = ya.shape[1]
    ne = router_w.shape[1]
    per_b = seq // tm
    rw = jnp.zeros((d, LANES), F32).at[:, :ne].set(router_w)
    rw_hi = rw.astype(BF16)
    rw_lo = (rw - rw_hi.astype(F32)).astype(BF16)
    rw = jnp.concatenate([rw_hi, rw_lo, rw_hi], axis=0)
    rb = jnp.full((1, LANES), NEG_INF, F32).at[0, :ne].set(router_b)
    tok = lambda w: pl.BlockSpec((tm, w), lambda i: (i, 0))
    full = lambda a, b: pl.BlockSpec((a, b), lambda i: (0, 0))
    bvec = pl.BlockSpec((1, 1, d), lambda i: (i // per_b, 0, 0))
    return pl.pallas_call(
        _merge_kernel,
        grid=(n_tok // tm,),
        in_specs=[tok(sw), tok(sw), tok(aw), tok(d), tok(d), tok(d),
                  full(1, sw), full(sw, sw), full(sw, d), full(aw, d), full(d, d),
                  full(1, d), bvec, full(1, d), bvec, bvec, full(3 * d, LANES), full(1, LANES)],
        out_specs=[tok(d), pl.BlockSpec((tm * (d // LANES), LANES), lambda i: (i, 0)),
                   tok(LANES), tok(LANES), tok(LANES),
                   pl.BlockSpec((1, 1, LANES), lambda i: (i, 0, 0)), full(1, LANES)],
        out_shape=[jax.ShapeDtypeStruct((n_tok, d), F32),
                   jax.ShapeDtypeStruct((n_tok * (d // LANES), LANES), F32),
                   jax.ShapeDtypeStruct((n_tok, LANES), I32),
                   jax.ShapeDtypeStruct((n_tok, LANES), F32),
                   jax.ShapeDtypeStruct((n_tok, LANES), I32),
                   jax.ShapeDtypeStruct((n_tok // tm, 1, LANES), F32),
                   jax.ShapeDtypeStruct((1, LANES), F32)],
        scratch_shapes=[pltpu.VMEM((1, LANES), F32)],
        compiler_params=_cparams(("arbitrary",)),
        name="merge",
    )(ys, u, ya, gs, ga, x2, ssm_d.reshape(1, sw), w_glu.astype(BF16), w_sb.astype(BF16),
      w_ab.astype(BF16), w_out.astype(BF16), post_g.reshape(1, d), gt_m, pre_g.reshape(1, d),
      sc_f, sh_f, rw, rb)


def _chunk_copies(cnt_ref, start_ref, tile, ne, max_rows, make):
    def per_expert(e, off):
        c, start = cnt_ref[tile * ne + e], start_ref[tile * ne + e]
        for b in range(max_rows.bit_length()):
            n = 1 << b

            @pl.when((c >> b) & 1 == 1)
            def _():
                make(off + (c & (n - 1)), start + (c & (n - 1)), n)
        return off + c

    lax.fori_loop(0, ne, per_expert, 0)


def _dispatch_kernel(cnt_ref, start_ref, pad_start_ref, pad_cnt_ref, tail_ref, loct_ref, h_ref, xs_ref,
                     buf, zero_ref, sem, zsem, *, tm, sub, bm, n_rows, ne):
    i = pl.program_id(0)
    last = pl.num_programs(0) - 1
    slot = i % 2
    n_loc = tm * TOP_K

    def row(ref, r, n=1):
        return ref.at[pl.ds(pl.multiple_of(r * sub, sub), n * sub)]

    def wait_slot(s):
        pltpu.make_async_copy(buf.at[s], row(xs_ref, 0, n_loc), sem.at[s]).wait()

    def zero_copies(do):
        def per_expert(e, _):
            start, cnt = pad_start_ref[e], pad_cnt_ref[e]
            for b in range(bm.bit_length() - 1):
                n = 1 << b

                @pl.when((cnt >> b) & 1 == 1)
                def _():
                    do(pltpu.make_async_copy(row(zero_ref, 0, n), row(xs_ref, start + (cnt & (n - 1)), n), zsem))
            return 0

        lax.fori_loop(0, ne, per_expert, 0)

        def per_block(j, _):
            r = tail_ref[0] + j * bm

            @pl.when(r < n_rows)
            def _():
                do(pltpu.make_async_copy(row(zero_ref, 0, bm), row(xs_ref, r, bm), zsem))
            return 0

        lax.fori_loop(0, ne, per_block, 0)

    @pl.when(i == 0)
    def _():
        zero_ref[...] = jnp.zeros_like(zero_ref)
        zero_copies(lambda cp: cp.start())

    @pl.when(i >= 2)
    def _():
        wait_slot(slot)

    loct = loct_ref[0]
    row_i = lax.broadcasted_iota(I32, (n_loc, tm), 0)
    place = row_i == loct[0:1, :]
    for k in range(1, TOP_K):
        place = place | (row_i == loct[k:k + 1, :])
    grouped = _dot(jnp.where(place, 1.0, 0.0).astype(BF16), _load_slab(h_ref, tm, sub).astype(BF16))
    _store_slab(buf.at[slot], grouped)
    _chunk_copies(cnt_ref, start_ref, i, ne, tm,
                  lambda loc, glob, n: pltpu.make_async_copy(row(buf.at[slot], loc, n), row(xs_ref, glob, n),
                                                             sem.at[slot]).start())

    @pl.when(i == last)
    def _():
        wait_slot(slot)

    @pl.when((i == last) & (i >= 1))
    def _():
        wait_slot(1 - slot)

    @pl.when(i == 0)
    def _():
        zero_copies(lambda cp: cp.wait())


def _dispatch(h_slab, loc, cnt_tile, start_tile, pad_start, pad_cnt, tail, n_rows, sub, bm, tm):
    n_tok = loc.shape[0]
    nt = n_tok // tm
    ne = pad_start.shape[0]
    loct = jnp.full((nt, 8, tm), -1, I32).at[:, :TOP_K].set(loc.reshape(nt, tm, TOP_K).transpose(0, 2, 1))
    grid_spec = pltpu.PrefetchScalarGridSpec(
        num_scalar_prefetch=5,
        grid=(nt,),
        in_specs=[pl.BlockSpec((1, 8, tm), lambda i, *_: (i, 0, 0)),
                  pl.BlockSpec((tm * sub, LANES), lambda i, *_: (i, 0))],
        out_specs=pl.BlockSpec(memory_space=pl.ANY),
        scratch_shapes=[pltpu.VMEM((2, TOP_K * tm * sub, LANES), F32), pltpu.VMEM((bm * sub, LANES), F32),
                        pltpu.SemaphoreType.DMA((2,)), pltpu.SemaphoreType.DMA(())],
    )
    return pl.pallas_call(
        functools.partial(_dispatch_kernel, tm=tm, sub=sub, bm=bm, n_rows=n_rows, ne=ne),
        grid_spec=grid_spec,
        out_shape=jax.ShapeDtypeStruct((n_rows * sub, LANES), F32),
        compiler_params=_cparams(("arbitrary",)),
        name="dispatch",
    )(cnt_tile.reshape(-1), start_tile.reshape(-1), pad_start, pad_cnt, tail, loct, h_slab)


def _expert_kernel(be_ref, nv_ref, rows_ref, x_ref, wg_ref, bg_ref, wu_ref, bu_ref, wd_ref, bd_ref,
                   o_ref, wg_s, wu_s, wd_s, *, bm, sub):
    i = pl.program_id(0)
    prev = be_ref[jnp.maximum(i - 1, 0)]
    changed = (i == 0) | (be_ref[i] != prev)

    @pl.when(changed)
    def _():
        wg_s[...] = wg_ref[0].astype(BF16)
        wu_s[...] = wu_ref[0].astype(BF16)
        wd_s[...] = wd_ref[0].astype(BF16)

    def ffn(n):
        xb = _load_slab(x_ref, n, sub).astype(BF16)
        g = _dot(xb, wg_s[...]) + bg_ref[0]
        u = _dot(xb, wu_s[...]) + bu_ref[0]
        g = jnp.minimum(g, SWIGLU_LIMIT)
        u = jnp.clip(u, -SWIGLU_LIMIT, SWIGLU_LIMIT)
        act = g * jax.nn.sigmoid(SWIGLU_ALPHA * g) * (u + 1.0)
        _store_slab(o_ref, _dot(act.astype(BF16), wd_s[...]) + bd_ref[0])

    used = rows_ref[i]
    hb = bm // 2

    @pl.when(used > hb)
    def _():
        ffn(bm)

    @pl.when((used > 0) & (used <= hb))
    def _():
        ffn(hb)
        o_ref[pl.ds(hb * sub, hb * sub), :] = jnp.zeros((hb * sub, LANES), F32)

    @pl.when(used == 0)
    def _():
        o_ref[...] = jnp.zeros_like(o_ref)


def _experts(xs, blk_exp, n_valid, blk_rows, w_gate, b_gate, w_up, b_up, w_down, b_down, bm):
    ne, d, f = w_gate.shape
    sub = d // LANES
    n_blocks = xs.shape[0] // (bm * sub)
    wspec = lambda a, b: pl.BlockSpec((1, a, b), lambda i, be, *_: (be[i], 0, 0))
    grid_spec = pltpu.PrefetchScalarGridSpec(
        num_scalar_prefetch=3,
        grid=(n_blocks,),
        in_specs=[pl.BlockSpec((bm * sub, LANES), lambda i, be, nv, *_: (jnp.clip(nv[0] - 1, 0, i), 0)),
                  wspec(d, f), wspec(1, f), wspec(d, f), wspec(1, f), wspec(f, d), wspec(1, d)],
        out_specs=pl.BlockSpec((bm * sub, LANES), lambda i, *_: (i, 0)),
        scratch_shapes=[pltpu.VMEM((d, f), BF16), pltpu.VMEM((d, f), BF16), pltpu.VMEM((f, d), BF16)],
    )
    return pl.pallas_call(
        functools.partial(_expert_kernel, bm=bm, sub=sub),
        grid_spec=grid_spec,
        out_shape=jax.ShapeDtypeStruct(xs.shape, F32),
        compiler_params=_cparams(("arbitrary",)),
        name="experts",
    )(blk_exp, n_valid, blk_rows, xs, w_gate, b_gate.reshape(ne, 1, f), w_up, b_up.reshape(ne, 1, f),
      w_down, b_down.reshape(ne, 1, d))


def _combine_kernel(cnt_ref, start_ref, yb_ref, loc_ref, tw_ref, x1_ref, g_ref, gt_ref, o_ref, buf, sem,
                    *, tm, sub, ne):
    i = pl.program_id(0)
    n_loc = tm * TOP_K

    def start_tile(tile, slot):
        def make(loc, glob, n):
            src = yb_ref.at[pl.ds(pl.multiple_of(glob * sub, sub), n * sub)]
            dst = buf.at[slot, pl.ds(pl.multiple_of(loc * sub, sub), n * sub)]
            pltpu.make_async_copy(src, dst, sem.at[slot]).start()

        _chunk_copies(cnt_ref, start_ref, tile, ne, tm, make)

    slot = i % 2

    @pl.when(i == 0)
    def _():
        start_tile(0, 0)

    @pl.when(i + 1 < pl.num_programs(0))
    def _():
        start_tile(i + 1, 1 - slot)

    pltpu.make_async_copy(yb_ref.at[pl.ds(0, n_loc * sub)], buf.at[slot], sem.at[slot]).wait()
    rows = _load_slab(buf.at[slot], n_loc, sub).astype(BF16)
    loc, tw = loc_ref[...], tw_ref[...]
    col_i = lax.broadcasted_iota(I32, (tm, n_loc), 1)
    w = jnp.zeros((tm, n_loc), F32)
    for k in range(TOP_K):
        w = jnp.where(col_i == loc[:, k:k + 1], tw[:, k:k + 1], w)
    ff = _dot(w.astype(BF16), rows)
    o_ref[...] = x1_ref[...] + gt_ref[0] * _rms(ff, g_ref[...])


def _combine(yb, loc, tw, cnt_tile, start_tile, x1, post_g, gt_f, seq, tm):
    n_tok, d = x1.shape
    sub = d // LANES
    nt = n_tok // tm
    ne = cnt_tile.shape[1]
    per_b = seq // tm
    loc_l = jnp.full((n_tok, LANES), -1, I32).at[:, :TOP_K].set(loc)
    grid_spec = pltpu.PrefetchScalarGridSpec(
        num_scalar_prefetch=2,
        grid=(nt,),
        in_specs=[pl.BlockSpec(memory_space=pl.ANY),
                  pl.BlockSpec((tm, LANES), lambda i, *_: (i, 0)),
                  pl.BlockSpec((tm, LANES), lambda i, *_: (i, 0)),
                  pl.BlockSpec((tm, d), lambda i, *_: (i, 0)),
                  pl.BlockSpec((1, d), lambda i, *_: (0, 0)),
                  pl.BlockSpec((1, 1, d), lambda i, *_: (i // per_b, 0, 0))],
        out_specs=pl.BlockSpec((tm, d), lambda i, *_: (i, 0)),
        scratch_shapes=[pltpu.VMEM((2, TOP_K * tm * sub, LANES), F32), pltpu.SemaphoreType.DMA((2,))],
    )
    return pl.pallas_call(
        functools.partial(_combine_kernel, tm=tm, sub=sub, ne=ne),
        grid_spec=grid_spec,
        out_shape=jax.ShapeDtypeStruct((n_tok, d), F32),
        compiler_params=_cparams(("arbitrary",)),
        name="combine",
    )(cnt_tile.reshape(-1), start_tile.reshape(-1), yb, loc_l, tw, x1, post_g.reshape(1, d), gt_f)


def kernel(x, c, positions, ada_w, ada_b, mix_pre_g, mix_post_g, ffn_pre_g, ffn_post_g, w_in, ssm_lam_re, ssm_lam_im, ssm_log_dt, ssm_b_re, ssm_b_im, ssm_c_re, ssm_c_im, ssm_d, ssm_w_glu, w_ssm_branch, w_attn_branch, w_out, router_w, router_b, w_gate, b_gate, w_up, b_up, w_down, b_down):
    bsz, seq, d = x.shape
    depth = ada_w.shape[0]
    n_tok = bsz * seq
    bm = EXPERT_ROWS
    sub = d // LANES
    xcur = x.reshape(n_tok, d)
    pos = positions.reshape(n_tok, 1).astype(I32)
    for l in range(depth):
        ada = _ada(c, ada_w[l], ada_b[l])
        sh_m, sc_m, gt_m, sh_f, sc_f, gt_f = [a.reshape(bsz, 1, d) for a in jnp.split(ada, 6, axis=-1)]

        u, q, k, v, gs, ga, km = _inproj(xcur, sc_m, sh_m, mix_pre_g[l].reshape(1, d), pos, w_in[l],
                                         seq, tm=min(512, seq))
        ys = _s5(u, ssm_lam_re[l], ssm_lam_im[l], ssm_log_dt[l], ssm_b_re[l], ssm_b_im[l],
                 ssm_c_re[l], ssm_c_im[l], bsz, seq)
        ya = _moba(q, k, v, km, bsz, seq)
        tm = ROUTE_TILE
        x1, h2, eid, tw, rank, before, cnt = _merge(
            ys, u, ya, gs, ga, xcur, ssm_d[l], ssm_w_glu[l], w_ssm_branch[l], w_attn_branch[l], w_out[l],
            mix_post_g[l], gt_m, ffn_pre_g[l], sc_f, sh_f, router_w[l], router_b[l], seq, tm=tm)

        ne = router_w.shape[-1]
        nt = n_tok // tm
        experts = jnp.arange(ne, dtype=I32)
        counts = cnt[0, :ne].astype(I32)
        padded = (counts + bm - 1) // bm * bm
        p_ends = jnp.cumsum(padded)
        p_starts = p_ends - padded
        before = before.reshape(nt, LANES)[:, :ne].astype(I32)
        cnt_tile = jnp.concatenate([before[1:], counts[None]], axis=0) - before
        start_tile = p_starts[None, :] + before
        off_tile = jnp.cumsum(cnt_tile, axis=1) - cnt_tile
        onehot = eid[:, :TOP_K].reshape(nt, tm, TOP_K, 1) == experts
        loc = (rank[:, :TOP_K] + jnp.sum(jnp.where(onehot, off_tile[:, None, None, :], 0), axis=-1).reshape(n_tok, TOP_K))
        n_blocks = (n_tok * TOP_K) // bm + ne
        n_rows = n_blocks * bm
        blk_start = jnp.arange(n_blocks, dtype=I32)[:, None] * bm
        blk_exp = jnp.minimum(jnp.sum((blk_start >= p_ends[None, :]).astype(I32), axis=1), ne - 1)
        n_valid = (p_ends[-1:] // bm).astype(I32)
        blk_rows = jnp.clip((p_starts + counts)[blk_exp] - blk_start[:, 0], 0, bm)
        blk_rows = jnp.where(blk_start[:, 0] < p_ends[-1], blk_rows, 0)

        xs = _dispatch(h2, loc, cnt_tile, start_tile, p_starts + counts, padded - counts, p_ends[-1:],
                       n_rows, sub, bm, tm)
        yb = _experts(xs, blk_exp, n_valid, blk_rows, w_gate[l], b_gate[l], w_up[l], b_up[l], w_down[l], b_down[l], bm)
        xcur = _combine(yb, loc, tw, cnt_tile, start_tile, x1, ffn_post_g[l], gt_f, seq, tm)
    return xcur.reshape(bsz, seq, d).astype(x.dtype)
```

```python
import functools
import math

import jax
import jax.numpy as jnp
from jax import lax
from jax.experimental import pallas as pl
from jax.experimental.pallas import tpu as pltpu

F32 = jnp.float32
BF16 = jnp.bfloat16
I32 = jnp.int32

N_HEADS = 8
HEAD_DIM = 64
ROPE_THETA = 10000.0
MOBA_BLOCK = 256
MOBA_TOPK = 3
SSM_GROUP_SIZE = 16
SSM_GROUPS = 32
SSM_STATE = 64
N_EXPERTS = 32
TOP_K = 4
SWIGLU_ALPHA = 1.702
SWIGLU_LIMIT = 7.0
NORM_EPS = 1e-6
NEG_INF = -1e30

LANES = 128
SSM_CHUNK = 64
EXPERT_ROWS = 512
ROUTE_TILE = 512
VMEM_LIMIT = 56 * 1024 * 1024
HIGHEST = lax.Precision.HIGHEST
Q_SCALE = HEAD_DIM ** -0.5 * math.log2(math.e)


def _cparams(sem):
    return pltpu.CompilerParams(dimension_semantics=sem, vmem_limit_bytes=VMEM_LIMIT)


def _dot(a, b, **kw):
    return jnp.dot(a, b, preferred_element_type=F32, **kw)


def _dot_t(a, b, **kw):
    return lax.dot_general(a, b, (((1,), (1,)), ((), ())), preferred_element_type=F32, **kw)


def _ada_kernel(c_ref, w_ref, b_ref, o_ref):
    c = c_ref[...]
    cond = c * jax.nn.sigmoid(c)
    o_ref[...] = _dot(cond, w_ref[...], precision=HIGHEST) + b_ref[...]


def _ada(c, ada_w, ada_b):
    bsz, d = c.shape
    n = ada_w.shape[1]
    c8 = jnp.zeros((8, d), F32).at[:bsz].set(c)
    out = pl.pallas_call(
        _ada_kernel,
        grid=(n // d,),
        in_specs=[pl.BlockSpec((8, d), lambda j: (0, 0)),
                  pl.BlockSpec((d, d), lambda j: (0, j)),
                  pl.BlockSpec((1, d), lambda j: (0, j))],
        out_specs=pl.BlockSpec((8, d), lambda j: (0, j)),
        out_shape=jax.ShapeDtypeStruct((8, n), F32),
        compiler_params=_cparams(("arbitrary",)),
        name="ada",
    )(c8, ada_w, ada_b.reshape(1, n))
    return out[:bsz]


def _transpose_kernel(w_ref, o_ref):
    o_ref[...] = w_ref[...].T.astype(BF16)


def _inproj_kernel(x_ref, sc_ref, sh_ref, g_ref, pos_ref, invf_ref, post_ref, invft_ref, w_ref, wt_ref,
                   u_ref, qt_ref, k_ref, vt_ref, gs_ref, ga_ref, km_ref, *, ssm_w, attn_w, d_model):
    x = x_ref[...]
    ms = jnp.mean(x * x, axis=-1, keepdims=True)
    xn = x * lax.rsqrt(ms + NORM_EPS) * g_ref[...]
    h = (xn * (1.0 + sc_ref[0]) + sh_ref[0]).astype(BF16)

    def proj(lo, width):
        return _dot(h, w_ref[:, lo:lo + width])

    tm = x.shape[0]
    ang = pos_ref[...].astype(F32) * invf_ref[...]
    reps = attn_w // LANES
    cos = jnp.concatenate([jnp.cos(ang)] * reps, axis=1)
    sin = jnp.concatenate([jnp.sin(ang)] * reps, axis=1)
    lane = lax.broadcasted_iota(I32, (1, attn_w), 1)
    first = (lane % HEAD_DIM) < (HEAD_DIM // 2)
    sin = jnp.where(first, -sin, sin)
    ang_t = invft_ref[...] * post_ref[0].astype(F32)
    cos_t, sin_t = jnp.cos(ang_t), jnp.sin(ang_t)

    u_ref[...] = proj(0, ssm_w)
    gs_ref[...] = jax.nn.sigmoid(proj(ssm_w + 3 * attn_w, d_model)).astype(BF16)
    ga_ref[...] = jax.nn.sigmoid(proj(ssm_w + 3 * attn_w + d_model, d_model)).astype(BF16)
    vt = _dot_t(wt_ref[attn_w:2 * attn_w, :], h).astype(BF16)
    nblk = tm // MOBA_BLOCK
    for j in range(nblk):
        vt_ref[j] = vt[:, j * MOBA_BLOCK:(j + 1) * MOBA_BLOCK]

    def rope(t):
        rot = jnp.where(first, pltpu.roll(t, attn_w - HEAD_DIM // 2, axis=1),
                        pltpu.roll(t, HEAD_DIM // 2, axis=1))
        return t * cos + rot * sin

    k = rope(proj(ssm_w + attn_w, attn_w))
    k_ref[...] = k.astype(BF16)
    km_ref[0] = jnp.mean(k.reshape(nblk, MOBA_BLOCK, attn_w), axis=1)

    half = HEAD_DIM // 2
    qt = _dot_t(wt_ref[0:attn_w, :], h).reshape(N_HEADS, 2, half, tm)
    t1, t2 = qt[:, 0], qt[:, 1]
    qt = jnp.stack([t1 * cos_t - t2 * sin_t, t2 * cos_t + t1 * sin_t], axis=1).reshape(attn_w, tm)
    qt_ref[...] = (qt * Q_SCALE).astype(BF16)


def _inproj(x2, sc, sh, g, pos, w_in, seq, tm):
    n_tok, d = x2.shape
    attn_w = N_HEADS * HEAD_DIM
    ssm_w = SSM_GROUPS * SSM_GROUP_SIZE
    in_w = w_in.shape[1]
    half = HEAD_DIM // 2
    inv_freq = ROPE_THETA ** (-jnp.arange(half, dtype=F32) / half)
    invf = jnp.tile(inv_freq, LANES // half).reshape(1, LANES)
    nt = n_tok // tm
    per_b = seq // tm
    nblk = tm // MOBA_BLOCK
    bsz = n_tok // seq
    blk = MOBA_BLOCK
    assert ssm_w % attn_w == 0
    q_blk, v_blk = ssm_w // attn_w, ssm_w // attn_w + 2
    wt = pl.pallas_call(
        _transpose_kernel,
        grid=(2,),
        in_specs=[pl.BlockSpec((d, attn_w), lambda j: (0, q_blk + j * (v_blk - q_blk)))],
        out_specs=pl.BlockSpec((attn_w, d), lambda j: (j, 0)),
        out_shape=jax.ShapeDtypeStruct((2 * attn_w, d), BF16),
        compiler_params=_cparams(("arbitrary",)),
        name="wt",
    )(w_in)
    tok = lambda w: pl.BlockSpec((tm, w), lambda i: (i, 0))
    full = lambda a, b: pl.BlockSpec((a, b), lambda i: (0, 0))
    bvec = pl.BlockSpec((1, 1, d), lambda i: (i // per_b, 0, 0))
    outs = pl.pallas_call(
        functools.partial(_inproj_kernel, ssm_w=ssm_w, attn_w=attn_w, d_model=d),
        grid=(nt,),
        in_specs=[tok(d), bvec, bvec, full(1, d), tok(1), full(1, LANES),
                  pl.BlockSpec((1, 1, tm), lambda i: (i, 0, 0)), full(half, 1),
                  full(d, in_w), full(2 * attn_w, d)],
        out_specs=[tok(ssm_w),
                   pl.BlockSpec((None, attn_w, tm), lambda i: (i // per_b, 0, i % per_b)),
                   tok(attn_w),
                   pl.BlockSpec((None, nblk, attn_w, blk), lambda i: (i // per_b, i % per_b, 0, 0)),
                   tok(d), tok(d),
                   pl.BlockSpec((1, nblk, attn_w), lambda i: (i, 0, 0))],
        out_shape=[jax.ShapeDtypeStruct((n_tok, ssm_w), F32),
                   jax.ShapeDtypeStruct((bsz, attn_w, seq), BF16),
                   jax.ShapeDtypeStruct((n_tok, attn_w), BF16),
                   jax.ShapeDtypeStruct((bsz, seq // blk, attn_w, blk), BF16),
                   jax.ShapeDtypeStruct((n_tok, d), BF16),
                   jax.ShapeDtypeStruct((n_tok, d), BF16),
                   jax.ShapeDtypeStruct((nt, nblk, attn_w), F32)],
        compiler_params=_cparams(("arbitrary",)),
        name="inproj",
    )(x2, sc, sh, g, pos, invf, pos.reshape(nt, 1, tm), inv_freq.reshape(half, 1), w_in.astype(BF16), wt)
    return outs


def _s5_group(x, zr, zi, lr, li, ca, cb, ba, bb, t_ref, *, chunk, n_chunks):
    gs = SSM_GROUP_SIZE

    def powers(tau):
        mag = jnp.exp(tau * zr)
        return mag * jnp.cos(tau * zi), mag * jnp.sin(tau * zi)

    tau = lax.broadcasted_iota(I32, (chunk + 8, LANES), 0).astype(F32)
    e_re, e_im = powers(tau)
    lb_re, lb_im = e_re[1:2], e_im[1:2]
    den = lr * lr + li * li
    a, b = lb_re - 1.0, lb_im
    cf_re = (a * lr + b * li) / den
    cf_im = (b * lr - a * li) / den
    bri = cf_re * ba + cf_im * bb
    bri_sw = cf_re * bb - cf_im * ba

    cpow =(e_re[:chunk + 1, None, :] * ca[None] + e_im[:chunk + 1, None, :] * cb[None])
    cpow = cpow.reshape((chunk + 1) * gs, LANES)
    width = chunk * gs
    r = _dot_t(bri, cpow[:width], precision=HIGHEST)
    col = lax.broadcasted_iota(I32, (gs, width), 1)
    t_ref[0:gs, :] = r.astype(BF16)
    for j in range(1, chunk):
        shifted = jnp.where(col >= gs * j, pltpu.roll(r, gs * j, axis=1), 0.0)
        t_ref[gs * j:gs * (j + 1), :] = shifted.astype(BF16)

    y = _dot(x, t_ref[...])

    tau_rev = (chunk - 1) - lax.broadcasted_iota(I32, (chunk, LANES), 0)
    r_re, r_im = powers(tau_rev.astype(F32))
    bst = r_re[:, None, :] * bri[None] + r_im[:, None, :] * bri_sw[None]
    bst = bst.reshape(width, LANES).astype(BF16)
    s = _dot(x, bst)

    rows = s.shape[0]
    n_idx = lax.broadcasted_iota(I32, (rows, LANES), 0) % n_chunks
    lane = lax.broadcasted_iota(I32, (1, LANES), 1)
    half = LANES // 2

    def cmul(v, p_re, p_im):
        return v * p_re + pltpu.roll(v, half, axis=1) * jnp.where(lane < half, -p_im, p_im)

    sh = 1
    while sh < n_chunks:
        p_re, p_im = powers(jnp.full((1, LANES), float(chunk * sh), F32))
        prev = jnp.where(n_idx >= sh, pltpu.roll(s, sh, axis=0), 0.0)
        s = s + cmul(prev, p_re, p_im)
        sh *= 2
    s_in = jnp.where(n_idx >= 1, pltpu.roll(s, 1, axis=0), 0.0)
    return y + _dot_t(s_in.astype(BF16), cpow[gs:gs + width].astype(BF16))


def _s5_kernel(u_ref, sel_ref, zr_ref, zi_ref, lr_ref, li_ref, ca_ref, cb_ref, ba_ref, bb_ref,
               y_ref, x_scr, y_scr, t_ref, *, chunk, n_chunks, rows):
    gpt = LANES // SSM_GROUP_SIZE

    def plane(j):
        return pl.ds(j, rows, stride=chunk)

    for jt in range(chunk // gpt):
        planes = jnp.concatenate([u_ref[plane(jt * gpt + jj), :] for jj in range(gpt)], axis=1)
        grouped = _dot(planes.astype(BF16), sel_ref[...])
        for g in range(gpt):
            x_scr[g, :, jt * LANES:(jt + 1) * LANES] = grouped[:, g * LANES:(g + 1) * LANES].astype(BF16)

    def per_group(g, _):
        y = _s5_group(x_scr[g], zr_ref[g], zi_ref[g], lr_ref[g], li_ref[g], ca_ref[g], cb_ref[g], ba_ref[g],
                      bb_ref[g], t_ref, chunk=chunk, n_chunks=n_chunks)
        y_scr[g] = y.astype(BF16)
        return 0

    lax.fori_loop(0, gpt, per_group, 0)

    for jt in range(chunk // gpt):
        grouped = jnp.concatenate([y_scr[g, :, jt * LANES:(jt + 1) * LANES] for g in range(gpt)], axis=1)
        planes = _dot(grouped, sel_ref[...])
        for jj in range(gpt):
            y_ref[plane(jt * gpt + jj), :] = planes[:, jj * LANES:(jj + 1) * LANES]


def _s5(u, lam_re, lam_im, log_dt, b_re, b_im, c_re, c_im, bsz, seq):
    g_n, gs, p = SSM_GROUPS, SSM_GROUP_SIZE, SSM_STATE
    chunk = SSM_CHUNK
    n_chunks = seq // chunk
    rows = bsz * n_chunks
    width = chunk * gs
    gpt = LANES // gs
    assert chunk % gpt == 0 and g_n % gpt == 0 and 2 * p == LANES
    dt = jnp.exp(log_dt.astype(F32))[:, None]
    dup = lambda a: jnp.concatenate([a, a], axis=-1).reshape(g_n, 1, 2 * p)
    zr, zi = dup(lam_re * dt), dup(lam_im * dt)
    lr, li = dup(lam_re), dup(lam_im)
    bt_re, bt_im = b_re.transpose(0, 2, 1), b_im.transpose(0, 2, 1)
    ba = jnp.concatenate([bt_re, bt_im], axis=-1)
    bb = jnp.concatenate([-bt_im, bt_re], axis=-1)
    ca = jnp.concatenate([c_re, -c_im], axis=-1)
    cb = jnp.concatenate([-c_im, -c_re], axis=-1)
    idx = jnp.arange(gpt * LANES, dtype=I32)
    swapped = ((idx // gs) % gpt) * LANES + (idx // LANES) * gs + idx % gs
    sel = (swapped[:, None] == idx[None, :]).astype(BF16)
    tile = pl.BlockSpec((bsz * seq, LANES), lambda t: (0, t))
    vec = pl.BlockSpec((gpt, 1, 2 * p), lambda t: (t, 0, 0))
    mat = pl.BlockSpec((gpt, gs, 2 * p), lambda t: (t, 0, 0))
    return pl.pallas_call(
        functools.partial(_s5_kernel, chunk=chunk, n_chunks=n_chunks, rows=rows),
        grid=(g_n // gpt,),
        in_specs=[tile, pl.BlockSpec((gpt * LANES, gpt * LANES), lambda t: (0, 0)),
                  vec, vec, vec, vec, mat, mat, mat, mat],
        out_specs=tile,
        out_shape=jax.ShapeDtypeStruct((bsz * seq, g_n * gs), F32),
        scratch_shapes=[pltpu.VMEM((gpt, rows, width), BF16), pltpu.VMEM((gpt, rows, width), BF16),
                        pltpu.VMEM((width, width), BF16)],
        compiler_params=_cparams(("arbitrary",)),
        name="s5",
    )(u, sel, zr, zi, lr, li, ca, cb, ba, bb)


def _moba_kernel(qt_ref, k_ref, vt_ref, km_ref, o_ref, s_a, s_b, p_a, p_b, *, nb, nbp, qblocks):
    blk = MOBA_BLOCK
    group = 2
    keys = group * blk
    nq = qblocks * blk
    first = qblocks * pl.program_id(2)
    qt = qt_ref[...]
    km = km_ref[...]
    km_hi = km.astype(BF16)
    km_lo = (km - km_hi.astype(F32)).astype(BF16)
    dim_i = lax.broadcasted_iota(I32, (LANES, nq), 0)
    blk_i = lax.broadcasted_iota(I32, (nbp, nq), 0)
    blk_f = blk_i.astype(F32)
    own = first + lax.broadcasted_iota(I32, (nbp, nq), 1) // blk
    valid = blk_i < own

    q_aug = []
    for h in range(2):
        qh = jnp.where((dim_i >= h * HEAD_DIM) & (dim_i < (h + 1) * HEAD_DIM), qt, jnp.zeros_like(qt))
        g = jnp.where(valid, _dot(km_hi, qh) + _dot(km_lo, qh), NEG_INF)
        sel = jnp.zeros((nbp, nq), jnp.bool_)
        for _ in range(MOBA_TOPK):
            m = jnp.max(g, axis=0, keepdims=True)
            idx = jnp.min(jnp.where(g == m, blk_f, float(nbp)), axis=0, keepdims=True)
            pick = blk_f == idx
            sel = sel | pick
            g = jnp.where(pick, -jnp.inf, g)
        ok = (sel & valid) | (blk_i == own)
        parts = [qh, jnp.where(ok, 0.0, NEG_INF).astype(BF16)]
        if nbp < LANES:
            parts.append(jnp.full((LANES - nbp, nq), NEG_INF, BF16))
        q_aug.append(jnp.concatenate(parts, axis=0))

    def scores(grp, s_ref, causal=False, lo=0):
        kb0 = grp * group
        k_cat = k_ref[pl.ds(jnp.minimum(kb0, nb - group), group)].reshape(keys, LANES)
        blk_id = kb0 + lax.broadcasted_iota(I32, (keys, LANES), 0) // blk
        onehot = jnp.where(lax.broadcasted_iota(I32, (keys, LANES), 1) == blk_id, 1.0, 0.0).astype(BF16)
        k_aug = jnp.concatenate([k_cat, onehot], axis=1)
        mx = []
        for h in range(2):
            s = _dot(k_aug, q_aug[h][:, lo:])
            if causal:
                key_pos = (kb0 - first) * blk + lax.broadcasted_iota(I32, (keys, nq - lo), 0)
                s = jnp.where(key_pos <= lo + lax.broadcasted_iota(I32, (keys, nq - lo), 1), s, NEG_INF)
            s_ref[h, :, lo:] = s
            mx.append(jnp.max(s, axis=0, keepdims=True))
        return tuple(mx)

    def softmax(s_ref, mx, p_ref, st, lo=0):
        out = []
        for h in range(2):
            m_i, _, l_i, acc = st[h]
            m_new = jnp.maximum(m_i[:, lo:], mx[h])
            p_ref[h, :, lo:] = jnp.exp2(s_ref[h, :, lo:] - m_new).astype(BF16)
            alpha = jnp.exp2(m_i[:, lo:] - m_new)
            if lo:
                m_new = jnp.concatenate([m_i[:, :lo], m_new], axis=1)
                alpha = jnp.concatenate([jnp.ones((1, lo), F32), alpha], axis=1)
            out.append((m_new, alpha, l_i, acc))
        return tuple(out)

    def values(grp, p_ref, st, lo=0):
        out = []
        for h in range(2):
            m_i, alpha, l_i, acc = st[h]
            blocks = [vt_ref[jnp.clip(grp * group + j, 0, nb - 1), h * HEAD_DIM:(h + 1) * HEAD_DIM, :]
                      for j in range(group)]
            ones = jnp.ones((16, keys), BF16)
            pv = _dot(jnp.concatenate([jnp.concatenate(blocks, axis=1), ones], axis=0), p_ref[h, :, lo:])
            if lo:
                pv = jnp.concatenate([jnp.zeros((pv.shape[0], lo), F32), pv], axis=1)
            out.append((m_i, alpha, alpha * l_i + pv[HEAD_DIM:HEAD_DIM + 1], alpha * acc + pv[:HEAD_DIM]))
        return tuple(out)

    mx_first = scores(0, s_a)
    p_b[...] = jnp.zeros_like(p_b)

    def body(t, carry):
        st, mx_a = carry
        mx_b = scores(2 * t + 1, s_b)
        st = values(2 * t - 1, p_b, st)
        st = softmax(s_a, mx_a, p_a, st)
        mx_a = scores(2 * t + 2, s_a)
        st = values(2 * t, p_a, st)
        return softmax(s_b, mx_b, p_b, st), mx_a

    init = (jnp.full((1, nq), -jnp.inf, F32), jnp.ones((1, nq), F32), jnp.zeros((1, nq), F32),
            jnp.zeros((HEAD_DIM, nq), F32))
    g0 = first // group
    st, _ = lax.fori_loop(0, g0 // 2, body, ((init, init), mx_first))
    lo_b = 0
    for pair in range(qblocks // (2 * group)):
        ga = g0 + 2 * pair
        lo = 2 * pair * keys
        mx_a = scores(ga, s_a, causal=True, lo=lo)
        st = values(ga - 1, p_b, st, lo=lo_b)
        lo_b = lo + keys
        mx_b = scores(ga + 1, s_b, causal=True, lo=lo_b)
        st = softmax(s_a, mx_a, p_a, st, lo=lo)
        st = values(ga, p_a, st, lo=lo)
        st = softmax(s_b, mx_b, p_b, st, lo=lo_b)
    st = values(g0 + qblocks // group - 1, p_b, st, lo=lo_b)
    both = jnp.concatenate([st[h][3] / st[h][2] for h in range(2)], axis=0)
    o_ref[...] = both.T.astype(BF16)


def _moba(qt, k, vt4, km, bsz, seq):
    n_tok, attn_w = k.shape
    blk = MOBA_BLOCK
    nb = seq // blk
    group, qblocks = 2, 4
    nbp = -(-nb // 16) * 16
    assert nb % qblocks == 0 and qblocks % (2 * group) == 0 and nbp <= LANES
    hp = attn_w // LANES
    km_pad = jnp.zeros((bsz, nbp, attn_w), F32).at[:, :nb].set(km.reshape(bsz, nb, attn_w))
    k4 = k.reshape(bsz, nb, blk, attn_w)
    steps = nb // qblocks
    return pl.pallas_call(
        functools.partial(_moba_kernel, nb=nb, nbp=nbp, qblocks=qblocks),
        grid=(bsz, hp, steps),
        in_specs=[pl.BlockSpec((None, LANES, qblocks * blk), lambda b, p, i: (b, p, i)),
                  pl.BlockSpec((None, nb, blk, LANES), lambda b, p, i: (b, 0, 0, p)),
                  pl.BlockSpec((None, nb, LANES, blk), lambda b, p, i: (b, 0, p, 0)),
                  pl.BlockSpec((None, nbp, LANES), lambda b, p, i: (b, 0, p))],
        out_specs=pl.BlockSpec((qblocks * blk, LANES), lambda b, p, i: (b * steps + i, p)),
        out_shape=jax.ShapeDtypeStruct((n_tok, attn_w), BF16),
        scratch_shapes=([pltpu.VMEM((2, group * blk, qblocks * blk), F32)] * 2
                        + [pltpu.VMEM((2, group * blk, qblocks * blk), BF16)] * 2),
        compiler_params=_cparams(("arbitrary", "arbitrary", "arbitrary")),
        name="moba",
    )(qt, k4, vt4, km_pad)


def _rms(t, g):
    return t * lax.rsqrt(jnp.mean(t * t, axis=-1, keepdims=True) + NORM_EPS) * g


def _store_slab(ref, val, base=0):
    rows, d = val.shape
    sub = d // LANES
    for s in range(sub):
        ref[pl.ds(base + s, rows, stride=sub), :] = val[:, s * LANES:(s + 1) * LANES]


def _load_slab(ref, rows, sub, base=0):
    return jnp.concatenate([ref[pl.ds(base + s, rows, stride=sub), :] for s in range(sub)], axis=1)


def _merge_kernel(ys_ref, u_ref, ya_ref, gs_ref, ga_ref, x_ref, d_ref, wglu_ref, wsb_ref, wab_ref,
                  wout_ref, postg_ref, gtm_ref, preg_ref, scf_ref, shf_ref, rw_ref, rb_ref,
                  x1_ref, h2_ref, tw_ref, loc_ref, before_ref, cnt_ref, run_ref):
    @pl.when(pl.program_id(0) == 0)
    def _():
        run_ref[...] = jnp.zeros_like(run_ref)

    tm = x_ref.shape[0]
    hm = tm // 2
    halves = [pl.ds(0, hm), pl.ds(hm, hm)]
    sub = x_ref.shape[1] // LANES

    ys = [jax.nn.gelu(ys_ref[r, :] + d_ref[...] * u_ref[r, :].astype(F32)) for r in halves]
    h2s = []
    for r, y in zip(halves, ys):
        y = y * jax.nn.sigmoid(_dot(y.astype(BF16), wglu_ref[...]))
        bs = _dot(y.astype(BF16), wsb_ref[...])
        ba = _dot(ya_ref[r, :], wab_ref[...])
        merged = gs_ref[r, :].astype(F32) * bs + ga_ref[r, :].astype(F32) * ba
        mix = _dot(merged.astype(BF16), wout_ref[...])
        x1 = x_ref[r, :] + gtm_ref[0] * _rms(mix, postg_ref[...])
        x1_ref[r, :] = x1
        h2s.append(_rms(x1, preg_ref[...]) * (1.0 + scf_ref[0]) + shf_ref[0])
    for j, h2 in enumerate(h2s):
        _store_slab(h2_ref, h2, base=j * hm * sub)

    lane = lax.broadcasted_iota(I32, (hm, LANES), 1)
    lane_f = lane.astype(F32)
    r_i = lax.broadcasted_iota(I32, (hm, hm), 0)
    c_i = lax.broadcasted_iota(I32, (hm, hm), 1)
    tri = jnp.where(c_i < r_i, 1.0, 0.0).astype(BF16)
    before_ref[0] = run_ref[...]
    gs = []
    for h2 in h2s:
        h_hi = h2.astype(BF16)
        h_lo = (h2 - h_hi.astype(F32)).astype(BF16)
        gs.append(_dot(jnp.concatenate([h_hi, h_hi, h_lo], axis=1), rw_ref[...]) + rb_ref[...])
    topk = [([], []) for _ in halves]
    for _ in range(TOP_K):
        for j in range(2):
            m = jnp.max(gs[j], axis=1, keepdims=True)
            idx = jnp.min(jnp.where(gs[j] == m, lane_f, float(LANES)), axis=1, keepdims=True)
            pick = lane_f == idx
            topk[j][0].append(m)
            topk[j][1].append(pick)
            gs[j] = jnp.where(pick, -jnp.inf, gs[j])
    onehots = [jnp.where(p[0] | p[1] | p[2] | p[3], 1.0, 0.0) for _, p in topk]
    cnt_tile = sum(jnp.sum(o, axis=0, keepdims=True) for o in onehots)
    cnt_hi = jnp.floor(cnt_tile * (1.0 / 256.0))
    parts = jnp.concatenate([cnt_hi, cnt_tile - 256.0 * cnt_hi, jnp.zeros((6, LANES), F32)], axis=0)
    before_e = jnp.where(lax.broadcasted_iota(I32, (LANES, LANES), 0) < lax.broadcasted_iota(I32, (LANES, LANES), 1),
                         1.0, 0.0).astype(BF16)
    pref = _dot(parts.astype(BF16), before_e)
    off_tile = 256.0 * pref[0:1] + pref[1:2]
    for r, (vals, picks), onehot in zip(halves, topk, onehots):
        exps = [jnp.exp(v - vals[0]) for v in vals]
        tot = exps[0] + exps[1] + exps[2] + exps[3]

        rank_full = _dot(tri, onehot.astype(BF16)) + (run_ref[...] - before_ref[0]) + off_tile
        run_ref[...] = run_ref[...] + jnp.sum(onehot, axis=0, keepdims=True)

        tw = jnp.zeros((hm, LANES), F32)
        rk = jnp.zeros((hm, LANES), F32)
        for k in range(TOP_K):
            tw = jnp.where(lane == k, exps[k] / tot, tw)
            pos = jnp.sum(jnp.where(picks[k], rank_full, 0.0), axis=1, keepdims=True)
            rk = jnp.where(lane == k, pos, rk)
        tw_ref[r, :] = tw
        loc_ref[r, :] = rk.astype(I32)
    cnt_ref[...] = run_ref[...]


def _merge(ys, u, ya, gs, ga, x2, ssm_d, w_glu, w_sb, w_ab, w_out, post_g, gt_m, pre_g, sc_f, sh_f,
           router_w, router_b, seq, tm):
    n_tok, d = x2.shape
    sw = ys.shape[1]
    aw = ya.shape[1]
    ne = router_w.shape[1]
    per_b = seq // tm
    rw = jnp.zeros((d, LANES), F32).at[:, :ne].set(router_w)
    rw_hi = rw.astype(BF16)
    rw_lo = (rw - rw_hi.astype(F32)).astype(BF16)
    rw = jnp.concatenate([rw_hi, rw_lo, rw_hi], axis=0)
    rb = jnp.full((1, LANES), NEG_INF, F32).at[0, :ne].set(router_b)
    tok = lambda w: pl.BlockSpec((tm, w), lambda i: (i, 0))
    full = lambda a, b: pl.BlockSpec((a, b), lambda i: (0, 0))
    bvec = pl.BlockSpec((1, 1, d), lambda i: (i // per_b, 0, 0))
    return pl.pallas_call(
        _merge_kernel,
        grid=(n_tok // tm,),
        in_specs=[tok(sw), tok(sw), tok(aw), tok(d), tok(d), tok(d),
                  full(1, sw), full(sw, sw), full(sw, d), full(aw, d), full(d, d),
                  full(1, d), bvec, full(1, d), bvec, bvec, full(3 * d, LANES), full(1, LANES)],
        out_specs=[tok(d), pl.BlockSpec((tm * (d // LANES), LANES), lambda i: (i, 0)),
                   tok(LANES), tok(LANES),
                   pl.BlockSpec((1, 1, LANES), lambda i: (i, 0, 0)), full(1, LANES)],
        out_shape=[jax.ShapeDtypeStruct((n_tok, d), F32),
                   jax.ShapeDtypeStruct((n_tok * (d // LANES), LANES), F32),
                   jax.ShapeDtypeStruct((n_tok, LANES), F32),
                   jax.ShapeDtypeStruct((n_tok, LANES), I32),
                   jax.ShapeDtypeStruct((n_tok // tm, 1, LANES), F32),
                   jax.ShapeDtypeStruct((1, LANES), F32)],
        scratch_shapes=[pltpu.VMEM((1, LANES), F32)],
        compiler_params=_cparams(("arbitrary",)),
        name="merge",
    )(ys, u, ya, gs, ga, x2, ssm_d.reshape(1, sw), w_glu.astype(BF16), w_sb.astype(BF16),
      w_ab.astype(BF16), w_out.astype(BF16), post_g.reshape(1, d), gt_m, pre_g.reshape(1, d),
      sc_f, sh_f, rw, rb)


def _chunk_copies(cnt_ref, start_ref, tile, ne, max_rows, make):
    def per_expert(e, off):
        c, start = cnt_ref[tile * ne + e], start_ref[tile * ne + e]
        for b in range(max_rows.bit_length()):
            n = 1 << b

            @pl.when((c >> b) & 1 == 1)
            def _():
                make(off + (c & (n - 1)), start + (c & (n - 1)), n)
        return off + c

    lax.fori_loop(0, ne, per_expert, 0)


def _dispatch_kernel(cnt_ref, start_ref, pad_start_ref, pad_cnt_ref, tail_ref, loct_ref, h_ref, xs_ref,
                     buf, zero_ref, sem, zsem, *, tm, sub, bm, n_rows, ne):
    i = pl.program_id(0)
    last = pl.num_programs(0) - 1
    slot = i % 2
    n_loc = tm * TOP_K

    def row(ref, r, n=1):
        return ref.at[pl.ds(pl.multiple_of(r * sub, sub), n * sub)]

    def wait_slot(s):
        pltpu.make_async_copy(buf.at[s], row(xs_ref, 0, n_loc), sem.at[s]).wait()

    def zero_copies(do):
        def per_expert(e, _):
            start, cnt = pad_start_ref[e], pad_cnt_ref[e]
            for b in range(bm.bit_length() - 1):
                n = 1 << b

                @pl.when((cnt >> b) & 1 == 1)
                def _():
                    do(pltpu.make_async_copy(row(zero_ref, 0, n), row(xs_ref, start + (cnt & (n - 1)), n), zsem))
            return 0

        lax.fori_loop(0, ne, per_expert, 0)

        def per_block(j, _):
            r = tail_ref[0] + j * bm

            @pl.when(r < n_rows)
            def _():
                do(pltpu.make_async_copy(row(zero_ref, 0, bm), row(xs_ref, r, bm), zsem))
            return 0

        lax.fori_loop(0, ne, per_block, 0)

    @pl.when(i == 0)
    def _():
        zero_ref[...] = jnp.zeros_like(zero_ref)
        zero_copies(lambda cp: cp.start())

    @pl.when(i >= 2)
    def _():
        wait_slot(slot)

    loct = loct_ref[0]
    row_i = lax.broadcasted_iota(I32, (n_loc, tm), 0)
    place = row_i == loct[0:1, :]
    for k in range(1, TOP_K):
        place = place | (row_i == loct[k:k + 1, :])
    grouped = _dot(jnp.where(place, 1.0, 0.0).astype(BF16), _load_slab(h_ref, tm, sub).astype(BF16))
    _store_slab(buf.at[slot], grouped)
    _chunk_copies(cnt_ref, start_ref, i, ne, tm,
                  lambda loc, glob, n: pltpu.make_async_copy(row(buf.at[slot], loc, n), row(xs_ref, glob, n),
                                                             sem.at[slot]).start())

    @pl.when(i == last)
    def _():
        wait_slot(slot)

    @pl.when((i == last) & (i >= 1))
    def _():
        wait_slot(1 - slot)

    @pl.when(i == 0)
    def _():
        zero_copies(lambda cp: cp.wait())


def _dispatch(h_slab, loc, cnt_tile, start_tile, pad_start, pad_cnt, tail, n_rows, sub, bm, tm):
    n_tok = loc.shape[0]
    nt = n_tok // tm
    ne = pad_start.shape[0]
    loct = jnp.full((nt, 8, tm), -1, I32).at[:, :TOP_K].set(loc.reshape(nt, tm, TOP_K).transpose(0, 2, 1))
    grid_spec = pltpu.PrefetchScalarGridSpec(
        num_scalar_prefetch=5,
        grid=(nt,),
        in_specs=[pl.BlockSpec((1, 8, tm), lambda i, *_: (i, 0, 0)),
                  pl.BlockSpec((tm * sub, LANES), lambda i, *_: (i, 0))],
        out_specs=pl.BlockSpec(memory_space=pl.ANY),
        scratch_shapes=[pltpu.VMEM((2, TOP_K * tm * sub, LANES), F32), pltpu.VMEM((bm * sub, LANES), F32),
                        pltpu.SemaphoreType.DMA((2,)), pltpu.SemaphoreType.DMA(())],
    )
    return pl.pallas_call(
        functools.partial(_dispatch_kernel, tm=tm, sub=sub, bm=bm, n_rows=n_rows, ne=ne),
        grid_spec=grid_spec,
        out_shape=jax.ShapeDtypeStruct((n_rows * sub, LANES), F32),
        compiler_params=_cparams(("arbitrary",)),
        name="dispatch",
    )(cnt_tile.reshape(-1), start_tile.reshape(-1), pad_start, pad_cnt, tail, loct, h_slab)


def _expert_kernel(be_ref, nv_ref, x_ref, wg_ref, bg_ref, wu_ref, bu_ref, wd_ref, bd_ref,
                   o_ref, wg_s, wu_s, wd_s, *, bm, sub):
    i = pl.program_id(0)
    prev = be_ref[jnp.maximum(i - 1, 0)]
    changed = (i == 0) | (be_ref[i] != prev)

    @pl.when(changed)
    def _():
        wg_s[...] = wg_ref[0].astype(BF16)
        wu_s[...] = wu_ref[0].astype(BF16)
        wd_s[...] = wd_ref[0].astype(BF16)

    @pl.when(i < nv_ref[0])
    def _():
        xb = _load_slab(x_ref, bm, sub).astype(BF16)
        g = _dot(xb, wg_s[...]) + bg_ref[0]
        u = _dot(xb, wu_s[...]) + bu_ref[0]
        g = jnp.minimum(g, SWIGLU_LIMIT)
        u = jnp.clip(u, -SWIGLU_LIMIT, SWIGLU_LIMIT)
        act = g * jax.nn.sigmoid(SWIGLU_ALPHA * g) * (u + 1.0)
        _store_slab(o_ref, _dot(act.astype(BF16), wd_s[...]) + bd_ref[0])

    @pl.when(i >= nv_ref[0])
    def _():
        o_ref[...] = jnp.zeros_like(o_ref)


def _experts(xs, blk_exp, n_valid, w_gate, b_gate, w_up, b_up, w_down, b_down, bm):
    ne, d, f = w_gate.shape
    sub = d // LANES
    n_blocks = xs.shape[0] // (bm * sub)
    wspec = lambda a, b: pl.BlockSpec((1, a, b), lambda i, be, nv: (be[i], 0, 0))
    grid_spec = pltpu.PrefetchScalarGridSpec(
        num_scalar_prefetch=2,
        grid=(n_blocks,),
        in_specs=[pl.BlockSpec((bm * sub, LANES), lambda i, be, nv: (jnp.clip(nv[0] - 1, 0, i), 0)),
                  wspec(d, f), wspec(1, f), wspec(d, f), wspec(1, f), wspec(f, d), wspec(1, d)],
        out_specs=pl.BlockSpec((bm * sub, LANES), lambda i, be, nv: (i, 0)),
        scratch_shapes=[pltpu.VMEM((d, f), BF16), pltpu.VMEM((d, f), BF16), pltpu.VMEM((f, d), BF16)],
    )
    return pl.pallas_call(
        functools.partial(_expert_kernel, bm=bm, sub=sub),
        grid_spec=grid_spec,
        out_shape=jax.ShapeDtypeStruct(xs.shape, F32),
        compiler_params=_cparams(("arbitrary",)),
        name="experts",
    )(blk_exp, n_valid, xs, w_gate, b_gate.reshape(ne, 1, f), w_up, b_up.reshape(ne, 1, f),
      w_down, b_down.reshape(ne, 1, d))


def _combine_kernel(cnt_ref, start_ref, yb_ref, loc_ref, tw_ref, x1_ref, g_ref, gt_ref, o_ref, buf, sem,
                    *, tm, sub, ne):
    i = pl.program_id(0)
    n_loc = tm * TOP_K

    def start_tile(tile, slot):
        def make(loc, glob, n):
            src = yb_ref.at[pl.ds(pl.multiple_of(glob * sub, sub), n * sub)]
            dst = buf.at[slot, pl.ds(pl.multiple_of(loc * sub, sub), n * sub)]
            pltpu.make_async_copy(src, dst, sem.at[slot]).start()

        _chunk_copies(cnt_ref, start_ref, tile, ne, tm, make)

    slot = i % 2

    @pl.when(i == 0)
    def _():
        start_tile(0, 0)

    @pl.when(i + 1 < pl.num_programs(0))
    def _():
        start_tile(i + 1, 1 - slot)

    pltpu.make_async_copy(yb_ref.at[pl.ds(0, n_loc * sub)], buf.at[slot], sem.at[slot]).wait()
    rows = _load_slab(buf.at[slot], n_loc, sub).astype(BF16)
    loc, tw = loc_ref[...], tw_ref[...]
    col_i = lax.broadcasted_iota(I32, (tm, n_loc), 1)
    w = jnp.zeros((tm, n_loc), F32)
    for k in range(TOP_K):
        w = jnp.where(col_i == loc[:, k:k + 1], tw[:, k:k + 1], w)
    ff = _dot(w.astype(BF16), rows)
    o_ref[...] = x1_ref[...] + gt_ref[0] * _rms(ff, g_ref[...])


def _combine(yb, loc, tw, cnt_tile, start_tile, x1, post_g, gt_f, seq, tm):
    n_tok, d = x1.shape
    sub = d // LANES
    nt = n_tok // tm
    ne = cnt_tile.shape[1]
    per_b = seq // tm
    assert loc.shape == (n_tok, LANES)
    grid_spec = pltpu.PrefetchScalarGridSpec(
        num_scalar_prefetch=2,
        grid=(nt,),
        in_specs=[pl.BlockSpec(memory_space=pl.ANY),
                  pl.BlockSpec((tm, LANES), lambda i, *_: (i, 0)),
                  pl.BlockSpec((tm, LANES), lambda i, *_: (i, 0)),
                  pl.BlockSpec((tm, d), lambda i, *_: (i, 0)),
                  pl.BlockSpec((1, d), lambda i, *_: (0, 0)),
                  pl.BlockSpec((1, 1, d), lambda i, *_: (i // per_b, 0, 0))],
        out_specs=pl.BlockSpec((tm, d), lambda i, *_: (i, 0)),
        scratch_shapes=[pltpu.VMEM((2, TOP_K * tm * sub, LANES), F32), pltpu.SemaphoreType.DMA((2,))],
    )
    return pl.pallas_call(
        functools.partial(_combine_kernel, tm=tm, sub=sub, ne=ne),
        grid_spec=grid_spec,
        out_shape=jax.ShapeDtypeStruct((n_tok, d), F32),
        compiler_params=_cparams(("arbitrary",)),
        name="combine",
    )(cnt_tile.reshape(-1), start_tile.reshape(-1), yb, loc, tw, x1, post_g.reshape(1, d), gt_f)


def kernel(x, c, positions, ada_w, ada_b, mix_pre_g, mix_post_g, ffn_pre_g, ffn_post_g, w_in, ssm_lam_re, ssm_lam_im, ssm_log_dt, ssm_b_re, ssm_b_im, ssm_c_re, ssm_c_im, ssm_d, ssm_w_glu, w_ssm_branch, w_attn_branch, w_out, router_w, router_b, w_gate, b_gate, w_up, b_up, w_down, b_down):
    bsz, seq, d = x.shape
    depth = ada_w.shape[0]
    n_tok = bsz * seq
    bm = EXPERT_ROWS
    sub = d // LANES
    xcur = x.reshape(n_tok, d)
    pos = positions.reshape(n_tok, 1).astype(I32)
    for l in range(depth):
        ada = _ada(c, ada_w[l], ada_b[l])
        sh_m, sc_m, gt_m, sh_f, sc_f, gt_f = [a.reshape(bsz, 1, d) for a in jnp.split(ada, 6, axis=-1)]

        u, q, k, v, gs, ga, km = _inproj(xcur, sc_m, sh_m, mix_pre_g[l].reshape(1, d), pos, w_in[l],
                                         seq, tm=min(512, seq))
        ys = _s5(u, ssm_lam_re[l], ssm_lam_im[l], ssm_log_dt[l], ssm_b_re[l], ssm_b_im[l],
                 ssm_c_re[l], ssm_c_im[l], bsz, seq)
        ya = _moba(q, k, v, km, bsz, seq)
        tm = ROUTE_TILE
        x1, h2, tw, loc, before, cnt = _merge(
            ys, u, ya, gs, ga, xcur, ssm_d[l], ssm_w_glu[l], w_ssm_branch[l], w_attn_branch[l], w_out[l],
            mix_post_g[l], gt_m, ffn_pre_g[l], sc_f, sh_f, router_w[l], router_b[l], seq, tm=tm)

        ne = router_w.shape[-1]
        nt = n_tok // tm
        counts = cnt[0, :ne].astype(I32)
        padded = (counts + bm - 1) // bm * bm
        p_ends = jnp.cumsum(padded)
        p_starts = p_ends - padded
        before = before.reshape(nt, LANES)[:, :ne].astype(I32)
        cnt_tile = jnp.concatenate([before[1:], counts[None]], axis=0) - before
        start_tile = p_starts[None, :] + before
        n_blocks = (n_tok * TOP_K) // bm + ne
        n_rows = n_blocks * bm
        blk_start = jnp.arange(n_blocks, dtype=I32)[:, None] * bm
        blk_exp = jnp.minimum(jnp.sum((blk_start >= p_ends[None, :]).astype(I32), axis=1), ne - 1)
        n_valid = (p_ends[-1:] // bm).astype(I32)

        xs = _dispatch(h2, loc[:, :TOP_K], cnt_tile, start_tile, p_starts + counts, padded - counts, p_ends[-1:],
                       n_rows, sub, bm, tm)
        yb = _experts(xs, blk_exp, n_valid, w_gate[l], b_gate[l], w_up[l], b_up[l], w_down[l], b_down[l], bm)
        xcur = _combine(yb, loc, tw, cnt_tile, start_tile, x1, ffn_post_g[l], gt_f, seq, tm)
    return xcur.reshape(bsz, seq, d).astype(x.dtype)
```

```python
import functools
import math

import jax
import jax.numpy as jnp
from jax import lax
from jax.experimental import pallas as pl
from jax.experimental.pallas import tpu as pltpu

F32 = jnp.float32
BF16 = jnp.bfloat16
I32 = jnp.int32

N_HEADS = 8
HEAD_DIM = 64
ROPE_THETA = 10000.0
MOBA_BLOCK = 256
MOBA_TOPK = 3
SSM_GROUP_SIZE = 16
SSM_GROUPS = 32
SSM_STATE = 64
N_EXPERTS = 32
TOP_K = 4
SWIGLU_ALPHA = 1.702
SWIGLU_LIMIT = 7.0
NORM_EPS = 1e-6
NEG_INF = -1e30

LANES = 128
SSM_CHUNK = 64
EXPERT_ROWS = 512
ROUTE_TILE = 512
VMEM_LIMIT = 56 * 1024 * 1024
HIGHEST = lax.Precision.HIGHEST
Q_SCALE = HEAD_DIM ** -0.5 * math.log2(math.e)


def _cparams(sem):
    return pltpu.CompilerParams(dimension_semantics=sem, vmem_limit_bytes=VMEM_LIMIT)


def _dot(a, b, **kw):
    return jnp.dot(a, b, preferred_element_type=F32, **kw)


def _dot_t(a, b, **kw):
    return lax.dot_general(a, b, (((1,), (1,)), ((), ())), preferred_element_type=F32, **kw)


def _ada_kernel(c_ref, w_ref, b_ref, o_ref):
    c = c_ref[...]
    cond = c * jax.nn.sigmoid(c)
    o_ref[...] = _dot(cond, w_ref[...], precision=HIGHEST) + b_ref[...]


def _ada(c, ada_w, ada_b):
    bsz, d = c.shape
    n = ada_w.shape[1]
    c8 = jnp.zeros((8, d), F32).at[:bsz].set(c)
    out = pl.pallas_call(
        _ada_kernel,
        grid=(n // d,),
        in_specs=[pl.BlockSpec((8, d), lambda j: (0, 0)),
                  pl.BlockSpec((d, d), lambda j: (0, j)),
                  pl.BlockSpec((1, d), lambda j: (0, j))],
        out_specs=pl.BlockSpec((8, d), lambda j: (0, j)),
        out_shape=jax.ShapeDtypeStruct((8, n), F32),
        compiler_params=_cparams(("arbitrary",)),
        name="ada",
    )(c8, ada_w, ada_b.reshape(1, n))
    return out[:bsz]


def _transpose_kernel(w_ref, o_ref):
    o_ref[...] = w_ref[...].T.astype(BF16)


def _inproj_kernel(x_ref, sc_ref, sh_ref, g_ref, pos_ref, invf_ref, post_ref, invft_ref, w_ref, wt_ref,
                   u_ref, qt_ref, k_ref, vt_ref, gs_ref, ga_ref, km_ref, *, ssm_w, attn_w, d_model):
    x = x_ref[...]
    ms = jnp.mean(x * x, axis=-1, keepdims=True)
    xn = x * lax.rsqrt(ms + NORM_EPS) * g_ref[...]
    h = (xn * (1.0 + sc_ref[0]) + sh_ref[0]).astype(BF16)

    def proj(lo, width):
        return _dot(h, w_ref[:, lo:lo + width])

    tm = x.shape[0]
    ang = pos_ref[...].astype(F32) * invf_ref[...]
    reps = attn_w // LANES
    cos = jnp.concatenate([jnp.cos(ang)] * reps, axis=1)
    sin = jnp.concatenate([jnp.sin(ang)] * reps, axis=1)
    lane = lax.broadcasted_iota(I32, (1, attn_w), 1)
    first = (lane % HEAD_DIM) < (HEAD_DIM // 2)
    sin = jnp.where(first, -sin, sin)
    ang_t = invft_ref[...] * post_ref[0].astype(F32)
    cos_t, sin_t = jnp.cos(ang_t), jnp.sin(ang_t)

    u_ref[...] = proj(0, ssm_w)
    gs_ref[...] = jax.nn.sigmoid(proj(ssm_w + 3 * attn_w, d_model)).astype(BF16)
    ga_ref[...] = jax.nn.sigmoid(proj(ssm_w + 3 * attn_w + d_model, d_model)).astype(BF16)
    vt = _dot_t(wt_ref[attn_w:2 * attn_w, :], h).astype(BF16)
    nblk = tm // MOBA_BLOCK
    for j in range(nblk):
        vt_ref[j] = vt[:, j * MOBA_BLOCK:(j + 1) * MOBA_BLOCK]

    def rope(t):
        rot = jnp.where(first, pltpu.roll(t, attn_w - HEAD_DIM // 2, axis=1),
                        pltpu.roll(t, HEAD_DIM // 2, axis=1))
        return t * cos + rot * sin

    k = rope(proj(ssm_w + attn_w, attn_w))
    k_ref[...] = k.astype(BF16)
    km_ref[0] = jnp.mean(k.reshape(nblk, MOBA_BLOCK, attn_w), axis=1)

    half = HEAD_DIM // 2
    qt = _dot_t(wt_ref[0:attn_w, :], h).reshape(N_HEADS, 2, half, tm)
    t1, t2 = qt[:, 0], qt[:, 1]
    qt = jnp.stack([t1 * cos_t - t2 * sin_t, t2 * cos_t + t1 * sin_t], axis=1).reshape(attn_w, tm)
    qt_ref[...] = (qt * Q_SCALE).astype(BF16)


def _inproj(x2, sc, sh, g, pos, w_in, seq, tm):
    n_tok, d = x2.shape
    attn_w = N_HEADS * HEAD_DIM
    ssm_w = SSM_GROUPS * SSM_GROUP_SIZE
    in_w = w_in.shape[1]
    half = HEAD_DIM // 2
    inv_freq = ROPE_THETA ** (-jnp.arange(half, dtype=F32) / half)
    invf = jnp.tile(inv_freq, LANES // half).reshape(1, LANES)
    nt = n_tok // tm
    per_b = seq // tm
    nblk = tm // MOBA_BLOCK
    bsz = n_tok // seq
    blk = MOBA_BLOCK
    assert ssm_w % attn_w == 0
    q_blk, v_blk = ssm_w // attn_w, ssm_w // attn_w + 2
    wt = pl.pallas_call(
        _transpose_kernel,
        grid=(2,),
        in_specs=[pl.BlockSpec((d, attn_w), lambda j: (0, q_blk + j * (v_blk - q_blk)))],
        out_specs=pl.BlockSpec((attn_w, d), lambda j: (j, 0)),
        out_shape=jax.ShapeDtypeStruct((2 * attn_w, d), BF16),
        compiler_params=_cparams(("arbitrary",)),
        name="wt",
    )(w_in)
    tok = lambda w: pl.BlockSpec((tm, w), lambda i: (i, 0))
    full = lambda a, b: pl.BlockSpec((a, b), lambda i: (0, 0))
    bvec = pl.BlockSpec((1, 1, d), lambda i: (i // per_b, 0, 0))
    outs = pl.pallas_call(
        functools.partial(_inproj_kernel, ssm_w=ssm_w, attn_w=attn_w, d_model=d),
        grid=(nt,),
        in_specs=[tok(d), bvec, bvec, full(1, d), tok(1), full(1, LANES),
                  pl.BlockSpec((1, 1, tm), lambda i: (i, 0, 0)), full(half, 1),
                  full(d, in_w), full(2 * attn_w, d)],
        out_specs=[tok(ssm_w),
                   pl.BlockSpec((None, attn_w, tm), lambda i: (i // per_b, 0, i % per_b)),
                   tok(attn_w),
                   pl.BlockSpec((None, nblk, attn_w, blk), lambda i: (i // per_b, i % per_b, 0, 0)),
                   tok(d), tok(d),
                   pl.BlockSpec((1, nblk, attn_w), lambda i: (i, 0, 0))],
        out_shape=[jax.ShapeDtypeStruct((n_tok, ssm_w), F32),
                   jax.ShapeDtypeStruct((bsz, attn_w, seq), BF16),
                   jax.ShapeDtypeStruct((n_tok, attn_w), BF16),
                   jax.ShapeDtypeStruct((bsz, seq // blk, attn_w, blk), BF16),
                   jax.ShapeDtypeStruct((n_tok, d), BF16),
                   jax.ShapeDtypeStruct((n_tok, d), BF16),
                   jax.ShapeDtypeStruct((nt, nblk, attn_w), F32)],
        compiler_params=_cparams(("arbitrary",)),
        name="inproj",
    )(x2, sc, sh, g, pos, invf, pos.reshape(nt, 1, tm), inv_freq.reshape(half, 1), w_in.astype(BF16), wt)
    return outs


def _s5_group(x, zr, zi, lr, li, ca, cb, ba, bb, t_ref, *, chunk, n_chunks):
    gs = SSM_GROUP_SIZE

    def powers(tau):
        mag = jnp.exp(tau * zr)
        return mag * jnp.cos(tau * zi), mag * jnp.sin(tau * zi)

    tau = lax.broadcasted_iota(I32, (chunk + 8, LANES), 0).astype(F32)
    e_re, e_im = powers(tau)
    lb_re, lb_im = e_re[1:2], e_im[1:2]
    den = lr * lr + li * li
    a, b = lb_re - 1.0, lb_im
    cf_re = (a * lr + b * li) / den
    cf_im = (b * lr - a * li) / den
    bri = cf_re * ba + cf_im * bb
    bri_sw = cf_re * bb - cf_im * ba

    cpow =(e_re[:chunk + 1, None, :] * ca[None] + e_im[:chunk + 1, None, :] * cb[None])
    cpow = cpow.reshape((chunk + 1) * gs, LANES)
    width = chunk * gs
    r = _dot_t(bri, cpow[:width], precision=HIGHEST)
    col = lax.broadcasted_iota(I32, (gs, width), 1)
    t_ref[0:gs, :] = r.astype(BF16)
    for j in range(1, chunk):
        shifted = jnp.where(col >= gs * j, pltpu.roll(r, gs * j, axis=1), 0.0)
        t_ref[gs * j:gs * (j + 1), :] = shifted.astype(BF16)

    y = _dot(x, t_ref[...])

    tau_rev = (chunk - 1) - lax.broadcasted_iota(I32, (chunk, LANES), 0)
    r_re, r_im = powers(tau_rev.astype(F32))
    bst = r_re[:, None, :] * bri[None] + r_im[:, None, :] * bri_sw[None]
    bst = bst.reshape(width, LANES).astype(BF16)
    s = _dot(x, bst)

    rows = s.shape[0]
    n_idx = lax.broadcasted_iota(I32, (rows, LANES), 0) % n_chunks
    lane = lax.broadcasted_iota(I32, (1, LANES), 1)
    half = LANES // 2

    def cmul(v, p_re, p_im):
        return v * p_re + pltpu.roll(v, half, axis=1) * jnp.where(lane < half, -p_im, p_im)

    sh = 1
    while sh < n_chunks:
        p_re, p_im = powers(jnp.full((1, LANES), float(chunk * sh), F32))
        prev = jnp.where(n_idx >= sh, pltpu.roll(s, sh, axis=0), 0.0)
        s = s + cmul(prev, p_re, p_im)
        sh *= 2
    s_in = jnp.where(n_idx >= 1, pltpu.roll(s, 1, axis=0), 0.0)
    return y + _dot_t(s_in.astype(BF16), cpow[gs:gs + width].astype(BF16))


def _s5_kernel(u_ref, sel_ref, zr_ref, zi_ref, lr_ref, li_ref, ca_ref, cb_ref, ba_ref, bb_ref,
               y_ref, x_scr, y_scr, t_ref, *, chunk, n_chunks, rows):
    gpt = LANES // SSM_GROUP_SIZE

    def plane(j):
        return pl.ds(j, rows, stride=chunk)

    for jt in range(chunk // gpt):
        planes = jnp.concatenate([u_ref[plane(jt * gpt + jj), :] for jj in range(gpt)], axis=1)
        grouped = _dot(planes.astype(BF16), sel_ref[...])
        for g in range(gpt):
            x_scr[g, :, jt * LANES:(jt + 1) * LANES] = grouped[:, g * LANES:(g + 1) * LANES].astype(BF16)

    def per_group(g, _):
        y = _s5_group(x_scr[g], zr_ref[g], zi_ref[g], lr_ref[g], li_ref[g], ca_ref[g], cb_ref[g], ba_ref[g],
                      bb_ref[g], t_ref, chunk=chunk, n_chunks=n_chunks)
        y_scr[g] = y.astype(BF16)
        return 0

    lax.fori_loop(0, gpt, per_group, 0)

    for jt in range(chunk // gpt):
        grouped = jnp.concatenate([y_scr[g, :, jt * LANES:(jt + 1) * LANES] for g in range(gpt)], axis=1)
        planes = _dot(grouped, sel_ref[...])
        for jj in range(gpt):
            y_ref[plane(jt * gpt + jj), :] = planes[:, jj * LANES:(jj + 1) * LANES]


def _s5(u, lam_re, lam_im, log_dt, b_re, b_im, c_re, c_im, bsz, seq):
    g_n, gs, p = SSM_GROUPS, SSM_GROUP_SIZE, SSM_STATE
    chunk = SSM_CHUNK
    n_chunks = seq // chunk
    rows = bsz * n_chunks
    width = chunk * gs
    gpt = LANES // gs
    assert chunk % gpt == 0 and g_n % gpt == 0 and 2 * p == LANES
    dt = jnp.exp(log_dt.astype(F32))[:, None]
    dup = lambda a: jnp.concatenate([a, a], axis=-1).reshape(g_n, 1, 2 * p)
    zr, zi = dup(lam_re * dt), dup(lam_im * dt)
    lr, li = dup(lam_re), dup(lam_im)
    bt_re, bt_im = b_re.transpose(0, 2, 1), b_im.transpose(0, 2, 1)
    ba = jnp.concatenate([bt_re, bt_im], axis=-1)
    bb = jnp.concatenate([-bt_im, bt_re], axis=-1)
    ca = jnp.concatenate([c_re, -c_im], axis=-1)
    cb = jnp.concatenate([-c_im, -c_re], axis=-1)
    idx = jnp.arange(gpt * LANES, dtype=I32)
    swapped = ((idx // gs) % gpt) * LANES + (idx // LANES) * gs + idx % gs
    sel = (swapped[:, None] == idx[None, :]).astype(BF16)
    tile = pl.BlockSpec((bsz * seq, LANES), lambda t: (0, t))
    vec = pl.BlockSpec((gpt, 1, 2 * p), lambda t: (t, 0, 0))
    mat = pl.BlockSpec((gpt, gs, 2 * p), lambda t: (t, 0, 0))
    return pl.pallas_call(
        functools.partial(_s5_kernel, chunk=chunk, n_chunks=n_chunks, rows=rows),
        grid=(g_n // gpt,),
        in_specs=[tile, pl.BlockSpec((gpt * LANES, gpt * LANES), lambda t: (0, 0)),
                  vec, vec, vec, vec, mat, mat, mat, mat],
        out_specs=tile,
        out_shape=jax.ShapeDtypeStruct((bsz * seq, g_n * gs), F32),
        scratch_shapes=[pltpu.VMEM((gpt, rows, width), BF16), pltpu.VMEM((gpt, rows, width), BF16),
                        pltpu.VMEM((width, width), BF16)],
        compiler_params=_cparams(("arbitrary",)),
        name="s5",
    )(u, sel, zr, zi, lr, li, ca, cb, ba, bb)


def _moba_kernel(qt_ref, k_ref, vt_ref, km_ref, o_ref, s_a, s_b, p_a, p_b, *, nb, nbp, qblocks):
    blk = MOBA_BLOCK
    group = 2
    keys = group * blk
    nq = qblocks * blk
    first = qblocks * pl.program_id(2)
    qt = qt_ref[...]
    km = km_ref[...]
    km_hi = km.astype(BF16)
    km_lo = (km - km_hi.astype(F32)).astype(BF16)
    dim_i = lax.broadcasted_iota(I32, (LANES, nq), 0)
    blk_i = lax.broadcasted_iota(I32, (nbp, nq), 0)
    blk_f = blk_i.astype(F32)
    own = first + lax.broadcasted_iota(I32, (nbp, nq), 1) // blk
    valid = blk_i < own

    q_aug = []
    for h in range(2):
        qh = jnp.where((dim_i >= h * HEAD_DIM) & (dim_i < (h + 1) * HEAD_DIM), qt, jnp.zeros_like(qt))
        g = jnp.where(valid, _dot(km_hi, qh) + _dot(km_lo, qh), NEG_INF)
        sel = jnp.zeros((nbp, nq), jnp.bool_)
        for _ in range(MOBA_TOPK):
            m = jnp.max(g, axis=0, keepdims=True)
            idx = jnp.min(jnp.where(g == m, blk_f, float(nbp)), axis=0, keepdims=True)
            pick = blk_f == idx
            sel = sel | pick
            g = jnp.where(pick, -jnp.inf, g)
        ok = (sel & valid) | (blk_i == own)
        parts = [qh, jnp.where(ok, 0.0, NEG_INF).astype(BF16)]
        if nbp < LANES:
            parts.append(jnp.full((LANES - nbp, nq), NEG_INF, BF16))
        q_aug.append(jnp.concatenate(parts, axis=0))

    def scores(grp, s_ref, causal=False, lo=0):
        kb0 = grp * group
        k_cat = k_ref[pl.ds(jnp.minimum(kb0, nb - group), group)].reshape(keys, LANES)
        blk_id = kb0 + lax.broadcasted_iota(I32, (keys, LANES), 0) // blk
        onehot = jnp.where(lax.broadcasted_iota(I32, (keys, LANES), 1) == blk_id, 1.0, 0.0).astype(BF16)
        k_aug = jnp.concatenate([k_cat, onehot], axis=1)
        mx = []
        for h in range(2):
            s = _dot(k_aug, q_aug[h][:, lo:])
            if causal:
                key_pos = (kb0 - first) * blk + lax.broadcasted_iota(I32, (keys, nq - lo), 0)
                s = jnp.where(key_pos <= lo + lax.broadcasted_iota(I32, (keys, nq - lo), 1), s, NEG_INF)
            s_ref[h, :, lo:] = s
            mx.append(jnp.max(s, axis=0, keepdims=True))
        return tuple(mx)

    def softmax(s_ref, mx, p_ref, st, lo=0):
        out = []
        for h in range(2):
            m_i, _, l_i, acc = st[h]
            m_new = jnp.maximum(m_i[:, lo:], mx[h])
            p_ref[h, :, lo:] = jnp.exp2(s_ref[h, :, lo:] - m_new).astype(BF16)
            alpha = jnp.exp2(m_i[:, lo:] - m_new)
            if lo:
                m_new = jnp.concatenate([m_i[:, :lo], m_new], axis=1)
                alpha = jnp.concatenate([jnp.ones((1, lo), F32), alpha], axis=1)
            out.append((m_new, alpha, l_i, acc))
        return tuple(out)

    def values(grp, p_ref, st, lo=0):
        out = []
        for h in range(2):
            m_i, alpha, l_i, acc = st[h]
            blocks = [vt_ref[jnp.clip(grp * group + j, 0, nb - 1), h * HEAD_DIM:(h + 1) * HEAD_DIM, :]
                      for j in range(group)]
            ones = jnp.ones((16, keys), BF16)
            pv = _dot(jnp.concatenate([jnp.concatenate(blocks, axis=1), ones], axis=0), p_ref[h, :, lo:])
            if lo:
                pv = jnp.concatenate([jnp.zeros((pv.shape[0], lo), F32), pv], axis=1)
            out.append((m_i, alpha, alpha * l_i + pv[HEAD_DIM:HEAD_DIM + 1], alpha * acc + pv[:HEAD_DIM]))
        return tuple(out)

    mx_first = scores(0, s_a)
    p_b[...] = jnp.zeros_like(p_b)

    def body(t, carry):
        st, mx_a = carry
        mx_b = scores(2 * t + 1, s_b)
        st = values(2 * t - 1, p_b, st)
        st = softmax(s_a, mx_a, p_a, st)
        mx_a = scores(2 * t + 2, s_a)
        st = values(2 * t, p_a, st)
        return softmax(s_b, mx_b, p_b, st), mx_a

    init = (jnp.full((1, nq), -jnp.inf, F32), jnp.ones((1, nq), F32), jnp.zeros((1, nq), F32),
            jnp.zeros((HEAD_DIM, nq), F32))
    g0 = first // group
    st, _ = lax.fori_loop(0, g0 // 2, body, ((init, init), mx_first))
    def mask_in_place(grp, s_ref):
        mx = []
        for h in range(2):
            key_pos = (grp * group - first) * blk + lax.broadcasted_iota(I32, (keys, nq), 0)
            s = jnp.where(key_pos <= lax.broadcasted_iota(I32, (keys, nq), 1), s_ref[h], NEG_INF)
            s_ref[h] = s
            mx.append(jnp.max(s, axis=0, keepdims=True))
        return tuple(mx)

    lo_b = 0
    for pair in range(qblocks // (2 * group)):
        ga = g0 + 2 * pair
        lo = 2 * pair * keys
        mx_a = mask_in_place(ga, s_a) if pair == 0 else scores(ga, s_a, causal=True, lo=lo)
        st = values(ga - 1, p_b, st, lo=lo_b)
        lo_b = lo + keys
        mx_b = scores(ga + 1, s_b, causal=True, lo=lo_b)
        st = softmax(s_a, mx_a, p_a, st, lo=lo)
        st = values(ga, p_a, st, lo=lo)
        st = softmax(s_b, mx_b, p_b, st, lo=lo_b)
    st = values(g0 + qblocks // group - 1, p_b, st, lo=lo_b)
    both = jnp.concatenate([st[h][3] / st[h][2] for h in range(2)], axis=0)
    o_ref[...] = both.T.astype(BF16)


def _moba(qt, k, vt4, km, bsz, seq):
    n_tok, attn_w = k.shape
    blk = MOBA_BLOCK
    nb = seq // blk
    group, qblocks = 2, 4
    nbp = -(-nb // 16) * 16
    assert nb % qblocks == 0 and qblocks % (2 * group) == 0 and nbp <= LANES
    hp = attn_w // LANES
    km_pad = jnp.zeros((bsz, nbp, attn_w), F32).at[:, :nb].set(km.reshape(bsz, nb, attn_w))
    k4 = k.reshape(bsz, nb, blk, attn_w)
    steps = nb // qblocks
    return pl.pallas_call(
        functools.partial(_moba_kernel, nb=nb, nbp=nbp, qblocks=qblocks),
        grid=(bsz, hp, steps),
        in_specs=[pl.BlockSpec((None, LANES, qblocks * blk), lambda b, p, i: (b, p, i)),
                  pl.BlockSpec((None, nb, blk, LANES), lambda b, p, i: (b, 0, 0, p)),
                  pl.BlockSpec((None, nb, LANES, blk), lambda b, p, i: (b, 0, p, 0)),
                  pl.BlockSpec((None, nbp, LANES), lambda b, p, i: (b, 0, p))],
        out_specs=pl.BlockSpec((qblocks * blk, LANES), lambda b, p, i: (b * steps + i, p)),
        out_shape=jax.ShapeDtypeStruct((n_tok, attn_w), BF16),
        scratch_shapes=([pltpu.VMEM((2, group * blk, qblocks * blk), F32)] * 2
                        + [pltpu.VMEM((2, group * blk, qblocks * blk), BF16)] * 2),
        compiler_params=_cparams(("arbitrary", "arbitrary", "arbitrary")),
        name="moba",
    )(qt, k4, vt4, km_pad)


def _rms(t, g):
    return t * lax.rsqrt(jnp.mean(t * t, axis=-1, keepdims=True) + NORM_EPS) * g


def _store_slab(ref, val, base=0):
    rows, d = val.shape
    sub = d // LANES
    for s in range(sub):
        ref[pl.ds(base + s, rows, stride=sub), :] = val[:, s * LANES:(s + 1) * LANES]


def _load_slab(ref, rows, sub, base=0):
    return jnp.concatenate([ref[pl.ds(base + s, rows, stride=sub), :] for s in range(sub)], axis=1)


def _merge_kernel(ys_ref, u_ref, ya_ref, gs_ref, ga_ref, x_ref, d_ref, wglu_ref, wsb_ref, wab_ref,
                  wout_ref, postg_ref, gtm_ref, preg_ref, scf_ref, shf_ref, rw_ref, rb_ref,
                  x1_ref, h2_ref, tw_ref, loc_ref, before_ref, cnt_ref, run_ref):
    @pl.when(pl.program_id(0) == 0)
    def _():
        run_ref[...] = jnp.zeros_like(run_ref)

    tm = x_ref.shape[0]
    hm = tm // 2
    halves = [pl.ds(0, hm), pl.ds(hm, hm)]
    sub = x_ref.shape[1] // LANES

    ys = [jax.nn.gelu(ys_ref[r, :] + d_ref[...] * u_ref[r, :].astype(F32)) for r in halves]
    h2s = []
    for r, y in zip(halves, ys):
        y = y * jax.nn.sigmoid(_dot(y.astype(BF16), wglu_ref[...]))
        bs = _dot(y.astype(BF16), wsb_ref[...])
        ba = _dot(ya_ref[r, :], wab_ref[...])
        merged = gs_ref[r, :].astype(F32) * bs + ga_ref[r, :].astype(F32) * ba
        mix = _dot(merged.astype(BF16), wout_ref[...])
        x1 = x_ref[r, :] + gtm_ref[0] * _rms(mix, postg_ref[...])
        x1_ref[r, :] = x1
        h2s.append(_rms(x1, preg_ref[...]) * (1.0 + scf_ref[0]) + shf_ref[0])
    for j, h2 in enumerate(h2s):
        _store_slab(h2_ref, h2, base=j * hm * sub)

    lane = lax.broadcasted_iota(I32, (hm, LANES), 1)
    lane_f = lane.astype(F32)
    r_i = lax.broadcasted_iota(I32, (hm, hm), 0)
    c_i = lax.broadcasted_iota(I32, (hm, hm), 1)
    tri = jnp.where(c_i < r_i, 1.0, 0.0).astype(BF16)
    before_ref[0] = run_ref[...]
    gs = []
    for h2 in h2s:
        h_hi = h2.astype(BF16)
        h_lo = (h2 - h_hi.astype(F32)).astype(BF16)
        gs.append(_dot(jnp.concatenate([h_hi, h_hi, h_lo], axis=1), rw_ref[...]) + rb_ref[...])
    topk = [([], []) for _ in halves]
    for _ in range(TOP_K):
        for j in range(2):
            m = jnp.max(gs[j], axis=1, keepdims=True)
            idx = jnp.min(jnp.where(gs[j] == m, lane_f, float(LANES)), axis=1, keepdims=True)
            pick = lane_f == idx
            topk[j][0].append(m)
            topk[j][1].append(pick)
            gs[j] = jnp.where(pick, -jnp.inf, gs[j])
    onehots = [jnp.where(p[0] | p[1] | p[2] | p[3], 1.0, 0.0) for _, p in topk]
    cnt_tile = sum(jnp.sum(o, axis=0, keepdims=True) for o in onehots)
    cnt_hi = jnp.floor(cnt_tile * (1.0 / 256.0))
    parts = jnp.concatenate([cnt_hi, cnt_tile - 256.0 * cnt_hi, jnp.zeros((6, LANES), F32)], axis=0)
    before_e = jnp.where(lax.broadcasted_iota(I32, (LANES, LANES), 0) < lax.broadcasted_iota(I32, (LANES, LANES), 1),
                         1.0, 0.0).astype(BF16)
    pref = _dot(parts.astype(BF16), before_e)
    off_tile = 256.0 * pref[0:1] + pref[1:2]
    for r, (vals, picks), onehot in zip(halves, topk, onehots):
        exps = [jnp.exp(v - vals[0]) for v in vals]
        tot = exps[0] + exps[1] + exps[2] + exps[3]

        rank_full = _dot(tri, onehot.astype(BF16)) + (run_ref[...] - before_ref[0]) + off_tile
        run_ref[...] = run_ref[...] + jnp.sum(onehot, axis=0, keepdims=True)

        tw = jnp.zeros((hm, LANES), F32)
        rk = jnp.zeros((hm, LANES), F32)
        for k in range(TOP_K):
            tw = jnp.where(lane == k, exps[k] / tot, tw)
            pos = jnp.sum(jnp.where(picks[k], rank_full, 0.0), axis=1, keepdims=True)
            rk = jnp.where(lane == k, pos, rk)
        tw_ref[r, :] = tw
        loc_ref[r, :] = rk.astype(I32)
    cnt_ref[...] = run_ref[...]


def _merge(ys, u, ya, gs, ga, x2, ssm_d, w_glu, w_sb, w_ab, w_out, post_g, gt_m, pre_g, sc_f, sh_f,
           router_w, router_b, seq, tm):
    n_tok, d = x2.shape
    sw = ys.shape[1]
    aw = ya.shape[1]
    ne = router_w.shape[1]
    per_b = seq // tm
    rw = jnp.zeros((d, LANES), F32).at[:, :ne].set(router_w)
    rw_hi = rw.astype(BF16)
    rw_lo = (rw - rw_hi.astype(F32)).astype(BF16)
    rw = jnp.concatenate([rw_hi, rw_lo, rw_hi], axis=0)
    rb = jnp.full((1, LANES), NEG_INF, F32).at[0, :ne].set(router_b)
    tok = lambda w: pl.BlockSpec((tm, w), lambda i: (i, 0))
    full = lambda a, b: pl.BlockSpec((a, b), lambda i: (0, 0))
    bvec = pl.BlockSpec((1, 1, d), lambda i: (i // per_b, 0, 0))
    return pl.pallas_call(
        _merge_kernel,
        grid=(n_tok // tm,),
        in_specs=[tok(sw), tok(sw), tok(aw), tok(d), tok(d), tok(d),
                  full(1, sw), full(sw, sw), full(sw, d), full(aw, d), full(d, d),
                  full(1, d), bvec, full(1, d), bvec, bvec, full(3 * d, LANES), full(1, LANES)],
        out_specs=[tok(d), pl.BlockSpec((tm * (d // LANES), LANES), lambda i: (i, 0)),
                   tok(LANES), tok(LANES),
                   pl.BlockSpec((1, 1, LANES), lambda i: (i, 0, 0)), full(1, LANES)],
        out_shape=[jax.ShapeDtypeStruct((n_tok, d), F32),
                   jax.ShapeDtypeStruct((n_tok * (d // LANES), LANES), F32),
                   jax.ShapeDtypeStruct((n_tok, LANES), F32),
                   jax.ShapeDtypeStruct((n_tok, LANES), I32),
                   jax.ShapeDtypeStruct((n_tok // tm, 1, LANES), F32),
                   jax.ShapeDtypeStruct((1, LANES), F32)],
        scratch_shapes=[pltpu.VMEM((1, LANES), F32)],
        compiler_params=_cparams(("arbitrary",)),
        name="merge",
    )(ys, u, ya, gs, ga, x2, ssm_d.reshape(1, sw), w_glu.astype(BF16), w_sb.astype(BF16),
      w_ab.astype(BF16), w_out.astype(BF16), post_g.reshape(1, d), gt_m, pre_g.reshape(1, d),
      sc_f, sh_f, rw, rb)


def _chunk_copies(cnt_ref, start_ref, tile, ne, max_rows, make):
    def per_expert(e, off):
        c, start = cnt_ref[tile * ne + e], start_ref[tile * ne + e]
        for b in range(max_rows.bit_length()):
            n = 1 << b

            @pl.when((c >> b) & 1 == 1)
            def _():
                make(off + (c & (n - 1)), start + (c & (n - 1)), n)
        return off + c

    lax.fori_loop(0, ne, per_expert, 0)


def _dispatch_kernel(cnt_ref, start_ref, pad_start_ref, pad_cnt_ref, tail_ref, loct_ref, h_ref, xs_ref,
                     buf, zero_ref, sem, zsem, *, tm, sub, bm, n_rows, ne):
    i = pl.program_id(0)
    last = pl.num_programs(0) - 1
    slot = i % 2
    n_loc = tm * TOP_K

    def row(ref, r, n=1):
        return ref.at[pl.ds(pl.multiple_of(r * sub, sub), n * sub)]

    def wait_slot(s):
        pltpu.make_async_copy(buf.at[s], row(xs_ref, 0, n_loc), sem.at[s]).wait()

    def zero_copies(do):
        def per_expert(e, _):
            start, cnt = pad_start_ref[e], pad_cnt_ref[e]
            for b in range(bm.bit_length() - 1):
                n = 1 << b

                @pl.when((cnt >> b) & 1 == 1)
                def _():
                    do(pltpu.make_async_copy(row(zero_ref, 0, n), row(xs_ref, start + (cnt & (n - 1)), n), zsem))
            return 0

        lax.fori_loop(0, ne, per_expert, 0)

        def per_block(j, _):
            r = tail_ref[0] + j * bm

            @pl.when(r < n_rows)
            def _():
                do(pltpu.make_async_copy(row(zero_ref, 0, bm), row(xs_ref, r, bm), zsem))
            return 0

        lax.fori_loop(0, ne, per_block, 0)

    @pl.when(i == 0)
    def _():
        zero_ref[...] = jnp.zeros_like(zero_ref)
        zero_copies(lambda cp: cp.start())

    @pl.when(i >= 2)
    def _():
        wait_slot(slot)

    loct = loct_ref[0]
    row_i = lax.broadcasted_iota(I32, (n_loc, tm), 0)
    place = row_i == loct[0:1, :]
    for k in range(1, TOP_K):
        place = place | (row_i == loct[k:k + 1, :])
    grouped = _dot(jnp.where(place, 1.0, 0.0).astype(BF16), _load_slab(h_ref, tm, sub).astype(BF16))
    _store_slab(buf.at[slot], grouped)
    _chunk_copies(cnt_ref, start_ref, i, ne, tm,
                  lambda loc, glob, n: pltpu.make_async_copy(row(buf.at[slot], loc, n), row(xs_ref, glob, n),
                                                             sem.at[slot]).start())

    @pl.when(i == last)
    def _():
        wait_slot(slot)

    @pl.when((i == last) & (i >= 1))
    def _():
        wait_slot(1 - slot)

    @pl.when(i == 0)
    def _():
        zero_copies(lambda cp: cp.wait())


def _dispatch(h_slab, loc, cnt_tile, start_tile, pad_start, pad_cnt, tail, n_rows, sub, bm, tm):
    n_tok = loc.shape[0]
    nt = n_tok // tm
    ne = pad_start.shape[0]
    loct = jnp.full((nt, 8, tm), -1, I32).at[:, :TOP_K].set(loc.reshape(nt, tm, TOP_K).transpose(0, 2, 1))
    grid_spec = pltpu.PrefetchScalarGridSpec(
        num_scalar_prefetch=5,
        grid=(nt,),
        in_specs=[pl.BlockSpec((1, 8, tm), lambda i, *_: (i, 0, 0)),
                  pl.BlockSpec((tm * sub, LANES), lambda i, *_: (i, 0))],
        out_specs=pl.BlockSpec(memory_space=pl.ANY),
        scratch_shapes=[pltpu.VMEM((2, TOP_K * tm * sub, LANES), F32), pltpu.VMEM((bm * sub, LANES), F32),
                        pltpu.SemaphoreType.DMA((2,)), pltpu.SemaphoreType.DMA(())],
    )
    return pl.pallas_call(
        functools.partial(_dispatch_kernel, tm=tm, sub=sub, bm=bm, n_rows=n_rows, ne=ne),
        grid_spec=grid_spec,
        out_shape=jax.ShapeDtypeStruct((n_rows * sub, LANES), F32),
        compiler_params=_cparams(("arbitrary",)),
        name="dispatch",
    )(cnt_tile.reshape(-1), start_tile.reshape(-1), pad_start, pad_cnt, tail, loct, h_slab)


def _expert_kernel(be_ref, nv_ref, x_ref, wg_ref, bg_ref, wu_ref, bu_ref, wd_ref, bd_ref,
                   o_ref, wg_s, wu_s, wd_s, *, bm, sub):
    i = pl.program_id(0)
    prev = be_ref[jnp.maximum(i - 1, 0)]
    changed = (i == 0) | (be_ref[i] != prev)

    @pl.when(changed)
    def _():
        wg_s[...] = wg_ref[0].astype(BF16)
        wu_s[...] = wu_ref[0].astype(BF16)
        wd_s[...] = wd_ref[0].astype(BF16)

    @pl.when(i < nv_ref[0])
    def _():
        xb = _load_slab(x_ref, bm, sub).astype(BF16)
        g = _dot(xb, wg_s[...]) + bg_ref[0]
        u = _dot(xb, wu_s[...]) + bu_ref[0]
        g = jnp.minimum(g, SWIGLU_LIMIT)
        u = jnp.clip(u, -SWIGLU_LIMIT, SWIGLU_LIMIT)
        act = g * jax.nn.sigmoid(SWIGLU_ALPHA * g) * (u + 1.0)
        _store_slab(o_ref, _dot(act.astype(BF16), wd_s[...]) + bd_ref[0])

    @pl.when(i >= nv_ref[0])
    def _():
        o_ref[...] = jnp.zeros_like(o_ref)


def _experts(xs, blk_exp, n_valid, w_gate, b_gate, w_up, b_up, w_down, b_down, bm):
    ne, d, f = w_gate.shape
    sub = d // LANES
    n_blocks = xs.shape[0] // (bm * sub)
    wspec = lambda a, b: pl.BlockSpec((1, a, b), lambda i, be, nv: (be[i], 0, 0))
    grid_spec = pltpu.PrefetchScalarGridSpec(
        num_scalar_prefetch=2,
        grid=(n_blocks,),
        in_specs=[pl.BlockSpec((bm * sub, LANES), lambda i, be, nv: (jnp.clip(nv[0] - 1, 0, i), 0)),
                  wspec(d, f), wspec(1, f), wspec(d, f), wspec(1, f), wspec(f, d), wspec(1, d)],
        out_specs=pl.BlockSpec((bm * sub, LANES), lambda i, be, nv: (i, 0)),
        scratch_shapes=[pltpu.VMEM((d, f), BF16), pltpu.VMEM((d, f), BF16), pltpu.VMEM((f, d), BF16)],
    )
    return pl.pallas_call(
        functools.partial(_expert_kernel, bm=bm, sub=sub),
        grid_spec=grid_spec,
        out_shape=jax.ShapeDtypeStruct(xs.shape, F32),
        compiler_params=_cparams(("arbitrary",)),
        name="experts",
    )(blk_exp, n_valid, xs, w_gate, b_gate.reshape(ne, 1, f), w_up, b_up.reshape(ne, 1, f),
      w_down, b_down.reshape(ne, 1, d))


def _combine_kernel(cnt_ref, start_ref, yb_ref, loc_ref, tw_ref, x1_ref, g_ref, gt_ref, o_ref, buf, sem,
                    *, tm, sub, ne):
    i = pl.program_id(0)
    n_loc = tm * TOP_K

    def start_tile(tile, slot):
        def make(loc, glob, n):
            src = yb_ref.at[pl.ds(pl.multiple_of(glob * sub, sub), n * sub)]
            dst = buf.at[slot, pl.ds(pl.multiple_of(loc * sub, sub), n * sub)]
            pltpu.make_async_copy(src, dst, sem.at[slot]).start()

        _chunk_copies(cnt_ref, start_ref, tile, ne, tm, make)

    slot = i % 2

    @pl.when(i == 0)
    def _():
        start_tile(0, 0)

    @pl.when(i + 1 < pl.num_programs(0))
    def _():
        start_tile(i + 1, 1 - slot)

    pltpu.make_async_copy(yb_ref.at[pl.ds(0, n_loc * sub)], buf.at[slot], sem.at[slot]).wait()
    rows = _load_slab(buf.at[slot], n_loc, sub).astype(BF16)
    loc, tw = loc_ref[...], tw_ref[...]
    col_i = lax.broadcasted_iota(I32, (tm, n_loc), 1)
    w = jnp.zeros((tm, n_loc), F32)
    for k in range(TOP_K):
        w = jnp.where(col_i == loc[:, k:k + 1], tw[:, k:k + 1], w)
    ff = _dot(w.astype(BF16), rows)
    o_ref[...] = x1_ref[...] + gt_ref[0] * _rms(ff, g_ref[...])


def _combine(yb, loc, tw, cnt_tile, start_tile, x1, post_g, gt_f, seq, tm):
    n_tok, d = x1.shape
    sub = d // LANES
    nt = n_tok // tm
    ne = cnt_tile.shape[1]
    per_b = seq // tm
    assert loc.shape == (n_tok, LANES)
    grid_spec = pltpu.PrefetchScalarGridSpec(
        num_scalar_prefetch=2,
        grid=(nt,),
        in_specs=[pl.BlockSpec(memory_space=pl.ANY),
                  pl.BlockSpec((tm, LANES), lambda i, *_: (i, 0)),
                  pl.BlockSpec((tm, LANES), lambda i, *_: (i, 0)),
                  pl.BlockSpec((tm, d), lambda i, *_: (i, 0)),
                  pl.BlockSpec((1, d), lambda i, *_: (0, 0)),
                  pl.BlockSpec((1, 1, d), lambda i, *_: (i // per_b, 0, 0))],
        out_specs=pl.BlockSpec((tm, d), lambda i, *_: (i, 0)),
        scratch_shapes=[pltpu.VMEM((2, TOP_K * tm * sub, LANES), F32), pltpu.SemaphoreType.DMA((2,))],
    )
    return pl.pallas_call(
        functools.partial(_combine_kernel, tm=tm, sub=sub, ne=ne),
        grid_spec=grid_spec,
        out_shape=jax.ShapeDtypeStruct((n_tok, d), F32),
        compiler_params=_cparams(("arbitrary",)),
        name="combine",
    )(cnt_tile.reshape(-1), start_tile.reshape(-1), yb, loc, tw, x1, post_g.reshape(1, d), gt_f)


def kernel(x, c, positions, ada_w, ada_b, mix_pre_g, mix_post_g, ffn_pre_g, ffn_post_g, w_in, ssm_lam_re, ssm_lam_im, ssm_log_dt, ssm_b_re, ssm_b_im, ssm_c_re, ssm_c_im, ssm_d, ssm_w_glu, w_ssm_branch, w_attn_branch, w_out, router_w, router_b, w_gate, b_gate, w_up, b_up, w_down, b_down):
    bsz, seq, d = x.shape
    depth = ada_w.shape[0]
    n_tok = bsz * seq
    bm = EXPERT_ROWS
    sub = d // LANES
    xcur = x.reshape(n_tok, d)
    pos = positions.reshape(n_tok, 1).astype(I32)
    for l in range(depth):
        ada = _ada(c, ada_w[l], ada_b[l])
        sh_m, sc_m, gt_m, sh_f, sc_f, gt_f = [a.reshape(bsz, 1, d) for a in jnp.split(ada, 6, axis=-1)]

        u, q, k, v, gs, ga, km = _inproj(xcur, sc_m, sh_m, mix_pre_g[l].reshape(1, d), pos, w_in[l],
                                         seq, tm=min(512, seq))
        ys = _s5(u, ssm_lam_re[l], ssm_lam_im[l], ssm_log_dt[l], ssm_b_re[l], ssm_b_im[l],
                 ssm_c_re[l], ssm_c_im[l], bsz, seq)
        ya = _moba(q, k, v, km, bsz, seq)
        tm = ROUTE_TILE
        x1, h2, tw, loc, before, cnt = _merge(
            ys, u, ya, gs, ga, xcur, ssm_d[l], ssm_w_glu[l], w_ssm_branch[l], w_attn_branch[l], w_out[l],
            mix_post_g[l], gt_m, ffn_pre_g[l], sc_f, sh_f, router_w[l], router_b[l], seq, tm=tm)

        ne = router_w.shape[-1]
        nt = n_tok // tm
        counts = cnt[0, :ne].astype(I32)
        padded = (counts + bm - 1) // bm * bm
        p_ends = jnp.cumsum(padded)
        p_starts = p_ends - padded
        before = before.reshape(nt, LANES)[:, :ne].astype(I32)
        cnt_tile = jnp.concatenate([before[1:], counts[None]], axis=0) - before
        start_tile = p_starts[None, :] + before
        n_blocks = (n_tok * TOP_K) // bm + ne
        n_rows = n_blocks * bm
        blk_start = jnp.arange(n_blocks, dtype=I32)[:, None] * bm
        blk_exp = jnp.minimum(jnp.sum((blk_start >= p_ends[None, :]).astype(I32), axis=1), ne - 1)
        n_valid = (p_ends[-1:] // bm).astype(I32)

        xs = _dispatch(h2, loc[:, :TOP_K], cnt_tile, start_tile, p_starts + counts, padded - counts, p_ends[-1:],
                       n_rows, sub, bm, tm)
        yb = _experts(xs, blk_exp, n_valid, w_gate[l], b_gate[l], w_up[l], b_up[l], w_down[l], b_down[l], bm)
        xcur = _combine(yb, loc, tw, cnt_tile, start_tile, x1, ffn_post_g[l], gt_f, seq, tm)
    return xcur.reshape(bsz, seq, d).astype(x.dtype)
```

```python
import functools
import math

import jax
import jax.numpy as jnp
from jax import lax
from jax.experimental import pallas as pl
from jax.experimental.pallas import tpu as pltpu

F32 = jnp.float32
BF16 = jnp.bfloat16
I32 = jnp.int32

N_HEADS = 8
HEAD_DIM = 64
ROPE_THETA = 10000.0
MOBA_BLOCK = 256
MOBA_TOPK = 3
SSM_GROUP_SIZE = 16
SSM_GROUPS = 32
SSM_STATE = 64
N_EXPERTS = 32
TOP_K = 4
SWIGLU_ALPHA = 1.702
SWIGLU_LIMIT = 7.0
NORM_EPS = 1e-6
NEG_INF = -1e30

LANES = 128
SSM_CHUNK = 64
EXPERT_ROWS = 512
ROUTE_TILE = 512
VMEM_LIMIT = 56 * 1024 * 1024
HIGHEST = lax.Precision.HIGHEST
Q_SCALE = HEAD_DIM ** -0.5 * math.log2(math.e)


def _cparams(sem):
    return pltpu.CompilerParams(dimension_semantics=sem, vmem_limit_bytes=VMEM_LIMIT)


def _dot(a, b, **kw):
    return jnp.dot(a, b, preferred_element_type=F32, **kw)


def _dot_t(a, b, **kw):
    return lax.dot_general(a, b, (((1,), (1,)), ((), ())), preferred_element_type=F32, **kw)


def _ada_kernel(c_ref, w_ref, b_ref, o_ref):
    c = c_ref[...]
    cond = c * jax.nn.sigmoid(c)
    o_ref[...] = _dot(cond, w_ref[...], precision=HIGHEST) + b_ref[...]


def _ada(c, ada_w, ada_b):
    bsz, d = c.shape
    n = ada_w.shape[1]
    c8 = jnp.zeros((8, d), F32).at[:bsz].set(c)
    out = pl.pallas_call(
        _ada_kernel,
        grid=(n // d,),
        in_specs=[pl.BlockSpec((8, d), lambda j: (0, 0)),
                  pl.BlockSpec((d, d), lambda j: (0, j)),
                  pl.BlockSpec((1, d), lambda j: (0, j))],
        out_specs=pl.BlockSpec((8, d), lambda j: (0, j)),
        out_shape=jax.ShapeDtypeStruct((8, n), F32),
        compiler_params=_cparams(("arbitrary",)),
        name="ada",
    )(c8, ada_w, ada_b.reshape(1, n))
    return out[:bsz]


def _transpose_kernel(w_ref, o_ref):
    o_ref[...] = w_ref[...].T.astype(BF16)


def _inproj_kernel(x_ref, sc_ref, sh_ref, g_ref, pos_ref, invf_ref, post_ref, invft_ref, w_ref, wt_ref,
                   u_ref, qt_ref, k_ref, vt_ref, gs_ref, ga_ref, km_ref, *, ssm_w, attn_w, d_model):
    x = x_ref[...]
    ms = jnp.mean(x * x, axis=-1, keepdims=True)
    xn = x * lax.rsqrt(ms + NORM_EPS) * g_ref[...]
    h = (xn * (1.0 + sc_ref[0]) + sh_ref[0]).astype(BF16)

    def proj(lo, width):
        return _dot(h, w_ref[:, lo:lo + width])

    tm = x.shape[0]
    ang = pos_ref[...].astype(F32) * invf_ref[...]
    reps = attn_w // LANES
    cos = jnp.concatenate([jnp.cos(ang)] * reps, axis=1)
    sin = jnp.concatenate([jnp.sin(ang)] * reps, axis=1)
    lane = lax.broadcasted_iota(I32, (1, attn_w), 1)
    first = (lane % HEAD_DIM) < (HEAD_DIM // 2)
    sin = jnp.where(first, -sin, sin)
    ang_t = invft_ref[...] * post_ref[0].astype(F32)
    cos_t, sin_t = jnp.cos(ang_t), jnp.sin(ang_t)

    u_ref[...] = proj(0, ssm_w)
    gs_ref[...] = jax.nn.sigmoid(proj(ssm_w + 3 * attn_w, d_model)).astype(BF16)
    ga_ref[...] = jax.nn.sigmoid(proj(ssm_w + 3 * attn_w + d_model, d_model)).astype(BF16)
    vt = _dot_t(wt_ref[attn_w:2 * attn_w, :], h).astype(BF16)
    nblk = tm // MOBA_BLOCK
    for j in range(nblk):
        vt_ref[j] = vt[:, j * MOBA_BLOCK:(j + 1) * MOBA_BLOCK]

    def rope(t):
        rot = jnp.where(first, pltpu.roll(t, attn_w - HEAD_DIM // 2, axis=1),
                        pltpu.roll(t, HEAD_DIM // 2, axis=1))
        return t * cos + rot * sin

    k = rope(proj(ssm_w + attn_w, attn_w))
    k_ref[...] = k.astype(BF16)
    km_ref[0] = jnp.mean(k.reshape(nblk, MOBA_BLOCK, attn_w), axis=1)

    half = HEAD_DIM // 2
    qt = _dot_t(wt_ref[0:attn_w, :], h).reshape(N_HEADS, 2, half, tm)
    t1, t2 = qt[:, 0], qt[:, 1]
    qt = jnp.stack([t1 * cos_t - t2 * sin_t, t2 * cos_t + t1 * sin_t], axis=1).reshape(attn_w, tm)
    qt_ref[...] = (qt * Q_SCALE).astype(BF16)


def _inproj(x2, sc, sh, g, pos, w_in, seq, tm):
    n_tok, d = x2.shape
    attn_w = N_HEADS * HEAD_DIM
    ssm_w = SSM_GROUPS * SSM_GROUP_SIZE
    in_w = w_in.shape[1]
    half = HEAD_DIM // 2
    inv_freq = ROPE_THETA ** (-jnp.arange(half, dtype=F32) / half)
    invf = jnp.tile(inv_freq, LANES // half).reshape(1, LANES)
    nt = n_tok // tm
    per_b = seq // tm
    nblk = tm // MOBA_BLOCK
    bsz = n_tok // seq
    blk = MOBA_BLOCK
    assert ssm_w % attn_w == 0
    q_blk, v_blk = ssm_w // attn_w, ssm_w // attn_w + 2
    wt = pl.pallas_call(
        _transpose_kernel,
        grid=(2,),
        in_specs=[pl.BlockSpec((d, attn_w), lambda j: (0, q_blk + j * (v_blk - q_blk)))],
        out_specs=pl.BlockSpec((attn_w, d), lambda j: (j, 0)),
        out_shape=jax.ShapeDtypeStruct((2 * attn_w, d), BF16),
        compiler_params=_cparams(("arbitrary",)),
        name="wt",
    )(w_in)
    tok = lambda w: pl.BlockSpec((tm, w), lambda i: (i, 0))
    full = lambda a, b: pl.BlockSpec((a, b), lambda i: (0, 0))
    bvec = pl.BlockSpec((1, 1, d), lambda i: (i // per_b, 0, 0))
    outs = pl.pallas_call(
        functools.partial(_inproj_kernel, ssm_w=ssm_w, attn_w=attn_w, d_model=d),
        grid=(nt,),
        in_specs=[tok(d), bvec, bvec, full(1, d), tok(1), full(1, LANES),
                  pl.BlockSpec((1, 1, tm), lambda i: (i, 0, 0)), full(half, 1),
                  full(d, in_w), full(2 * attn_w, d)],
        out_specs=[tok(ssm_w),
                   pl.BlockSpec((None, attn_w, tm), lambda i: (i // per_b, 0, i % per_b)),
                   tok(attn_w),
                   pl.BlockSpec((None, nblk, attn_w, blk), lambda i: (i // per_b, i % per_b, 0, 0)),
                   tok(d), tok(d),
                   pl.BlockSpec((1, nblk, attn_w), lambda i: (i, 0, 0))],
        out_shape=[jax.ShapeDtypeStruct((n_tok, ssm_w), F32),
                   jax.ShapeDtypeStruct((bsz, attn_w, seq), BF16),
                   jax.ShapeDtypeStruct((n_tok, attn_w), BF16),
                   jax.ShapeDtypeStruct((bsz, seq // blk, attn_w, blk), BF16),
                   jax.ShapeDtypeStruct((n_tok, d), BF16),
                   jax.ShapeDtypeStruct((n_tok, d), BF16),
                   jax.ShapeDtypeStruct((nt, nblk, attn_w), F32)],
        compiler_params=_cparams(("arbitrary",)),
        name="inproj",
    )(x2, sc, sh, g, pos, invf, pos.reshape(nt, 1, tm), inv_freq.reshape(half, 1), w_in.astype(BF16), wt)
    return outs


def _s5_group(x, zr, zi, lr, li, ca, cb, ba, bb, t_ref, *, chunk, n_chunks):
    gs = SSM_GROUP_SIZE

    def powers(tau):
        mag = jnp.exp(tau * zr)
        return mag * jnp.cos(tau * zi), mag * jnp.sin(tau * zi)

    tau = lax.broadcasted_iota(I32, (chunk + 8, LANES), 0).astype(F32)
    e_re, e_im = powers(tau)
    lb_re, lb_im = e_re[1:2], e_im[1:2]
    den = lr * lr + li * li
    a, b = lb_re - 1.0, lb_im
    cf_re = (a * lr + b * li) / den
    cf_im = (b * lr - a * li) / den
    bri = cf_re * ba + cf_im * bb
    bri_sw = cf_re * bb - cf_im * ba

    cpow =(e_re[:chunk + 1, None, :] * ca[None] + e_im[:chunk + 1, None, :] * cb[None])
    cpow = cpow.reshape((chunk + 1) * gs, LANES)
    width = chunk * gs
    r = _dot_t(bri, cpow[:width], precision=HIGHEST)
    col = lax.broadcasted_iota(I32, (gs, width), 1)
    t_ref[0:gs, :] = r.astype(BF16)
    for j in range(1, chunk):
        shifted = jnp.where(col >= gs * j, pltpu.roll(r, gs * j, axis=1), 0.0)
        t_ref[gs * j:gs * (j + 1), :] = shifted.astype(BF16)

    y = _dot(x, t_ref[...])

    tau_rev = (chunk - 1) - lax.broadcasted_iota(I32, (chunk, LANES), 0)
    r_re, r_im = powers(tau_rev.astype(F32))
    bst = r_re[:, None, :] * bri[None] + r_im[:, None, :] * bri_sw[None]
    bst = bst.reshape(width, LANES).astype(BF16)
    s = _dot(x, bst)

    rows = s.shape[0]
    n_idx = lax.broadcasted_iota(I32, (rows, LANES), 0) % n_chunks
    lane = lax.broadcasted_iota(I32, (1, LANES), 1)
    half = LANES // 2

    def cmul(v, p_re, p_im):
        return v * p_re + pltpu.roll(v, half, axis=1) * jnp.where(lane < half, -p_im, p_im)

    sh = 1
    while sh < n_chunks:
        p_re, p_im = powers(jnp.full((1, LANES), float(chunk * sh), F32))
        prev = jnp.where(n_idx >= sh, pltpu.roll(s, sh, axis=0), 0.0)
        s = s + cmul(prev, p_re, p_im)
        sh *= 2
    s_in = jnp.where(n_idx >= 1, pltpu.roll(s, 1, axis=0), 0.0)
    return y + _dot_t(s_in.astype(BF16), cpow[gs:gs + width].astype(BF16))


def _s5_kernel(u_ref, sel_ref, zr_ref, zi_ref, lr_ref, li_ref, ca_ref, cb_ref, ba_ref, bb_ref,
               y_ref, x_scr, y_scr, t_ref, *, chunk, n_chunks, rows):
    gpt = LANES // SSM_GROUP_SIZE

    def plane(j):
        return pl.ds(j, rows, stride=chunk)

    for jt in range(chunk // gpt):
        planes = jnp.concatenate([u_ref[plane(jt * gpt + jj), :] for jj in range(gpt)], axis=1)
        grouped = _dot(planes.astype(BF16), sel_ref[...])
        for g in range(gpt):
            x_scr[g, :, jt * LANES:(jt + 1) * LANES] = grouped[:, g * LANES:(g + 1) * LANES].astype(BF16)

    def per_group(g, _):
        y = _s5_group(x_scr[g], zr_ref[g], zi_ref[g], lr_ref[g], li_ref[g], ca_ref[g], cb_ref[g], ba_ref[g],
                      bb_ref[g], t_ref, chunk=chunk, n_chunks=n_chunks)
        y_scr[g] = y.astype(BF16)
        return 0

    lax.fori_loop(0, gpt, per_group, 0)

    for jt in range(chunk // gpt):
        grouped = jnp.concatenate([y_scr[g, :, jt * LANES:(jt + 1) * LANES] for g in range(gpt)], axis=1)
        planes = _dot(grouped, sel_ref[...])
        for jj in range(gpt):
            y_ref[plane(jt * gpt + jj), :] = planes[:, jj * LANES:(jj + 1) * LANES]


def _s5(u, lam_re, lam_im, log_dt, b_re, b_im, c_re, c_im, bsz, seq):
    g_n, gs, p = SSM_GROUPS, SSM_GROUP_SIZE, SSM_STATE
    chunk = SSM_CHUNK
    n_chunks = seq // chunk
    rows = bsz * n_chunks
    width = chunk * gs
    gpt = LANES // gs
    assert chunk % gpt == 0 and g_n % gpt == 0 and 2 * p == LANES
    dt = jnp.exp(log_dt.astype(F32))[:, None]
    dup = lambda a: jnp.concatenate([a, a], axis=-1).reshape(g_n, 1, 2 * p)
    zr, zi = dup(lam_re * dt), dup(lam_im * dt)
    lr, li = dup(lam_re), dup(lam_im)
    bt_re, bt_im = b_re.transpose(0, 2, 1), b_im.transpose(0, 2, 1)
    ba = jnp.concatenate([bt_re, bt_im], axis=-1)
    bb = jnp.concatenate([-bt_im, bt_re], axis=-1)
    ca = jnp.concatenate([c_re, -c_im], axis=-1)
    cb = jnp.concatenate([-c_im, -c_re], axis=-1)
    idx = jnp.arange(gpt * LANES, dtype=I32)
    swapped = ((idx // gs) % gpt) * LANES + (idx // LANES) * gs + idx % gs
    sel = (swapped[:, None] == idx[None, :]).astype(BF16)
    tile = pl.BlockSpec((bsz * seq, LANES), lambda t: (0, t))
    vec = pl.BlockSpec((gpt, 1, 2 * p), lambda t: (t, 0, 0))
    mat = pl.BlockSpec((gpt, gs, 2 * p), lambda t: (t, 0, 0))
    return pl.pallas_call(
        functools.partial(_s5_kernel, chunk=chunk, n_chunks=n_chunks, rows=rows),
        grid=(g_n // gpt,),
        in_specs=[tile, pl.BlockSpec((gpt * LANES, gpt * LANES), lambda t: (0, 0)),
                  vec, vec, vec, vec, mat, mat, mat, mat],
        out_specs=tile,
        out_shape=jax.ShapeDtypeStruct((bsz * seq, g_n * gs), F32),
        scratch_shapes=[pltpu.VMEM((gpt, rows, width), BF16), pltpu.VMEM((gpt, rows, width), BF16),
                        pltpu.VMEM((width, width), BF16)],
        compiler_params=_cparams(("arbitrary",)),
        name="s5",
    )(u, sel, zr, zi, lr, li, ca, cb, ba, bb)


def _moba_kernel(qt_ref, k_ref, vt_ref, km_ref, o_ref, s_a, s_b, p_a, p_b, *, nb, nbp, qblocks):
    blk = MOBA_BLOCK
    group = 2
    keys = group * blk
    nq = qblocks * blk
    first = qblocks * pl.program_id(2)
    qt = qt_ref[...]
    km = km_ref[...]
    km_hi = km.astype(BF16)
    km_lo = (km - km_hi.astype(F32)).astype(BF16)
    dim_i = lax.broadcasted_iota(I32, (LANES, nq), 0)
    blk_i = lax.broadcasted_iota(I32, (nbp, nq), 0)
    blk_f = blk_i.astype(F32)
    own = first + lax.broadcasted_iota(I32, (nbp, nq), 1) // blk
    valid = blk_i < own

    q_aug = []
    for h in range(2):
        qh = jnp.where((dim_i >= h * HEAD_DIM) & (dim_i < (h + 1) * HEAD_DIM), qt, jnp.zeros_like(qt))
        g = jnp.where(valid, _dot(km_hi, qh) + _dot(km_lo, qh), NEG_INF)
        sel = jnp.zeros((nbp, nq), jnp.bool_)
        for _ in range(MOBA_TOPK):
            m = jnp.max(g, axis=0, keepdims=True)
            idx = jnp.min(jnp.where(g == m, blk_f, float(nbp)), axis=0, keepdims=True)
            pick = blk_f == idx
            sel = sel | pick
            g = jnp.where(pick, -jnp.inf, g)
        ok = (sel & valid) | (blk_i == own)
        parts = [qh, jnp.where(ok, 0.0, NEG_INF).astype(BF16)]
        if nbp < LANES:
            parts.append(jnp.full((LANES - nbp, nq), NEG_INF, BF16))
        q_aug.append(jnp.concatenate(parts, axis=0))

    def scores(grp, s_ref, causal=False, lo=0):
        kb0 = grp * group
        k_cat = k_ref[pl.ds(jnp.minimum(kb0, nb - group), group)].reshape(keys, LANES)
        blk_id = kb0 + lax.broadcasted_iota(I32, (keys, LANES), 0) // blk
        onehot = jnp.where(lax.broadcasted_iota(I32, (keys, LANES), 1) == blk_id, 1.0, 0.0).astype(BF16)
        k_aug = jnp.concatenate([k_cat, onehot], axis=1)
        mx = []
        for h in range(2):
            s = _dot(k_aug, q_aug[h][:, lo:])
            if causal:
                key_pos = (kb0 - first) * blk + lax.broadcasted_iota(I32, (keys, nq - lo), 0)
                s = jnp.where(key_pos <= lo + lax.broadcasted_iota(I32, (keys, nq - lo), 1), s, NEG_INF)
            s_ref[h, :, lo:] = s
            mx.append(jnp.max(s, axis=0, keepdims=True))
        return tuple(mx)

    def softmax(s_ref, mx, p_ref, st, lo=0):
        out = []
        for h in range(2):
            m_i, _, l_i, acc = st[h]
            m_new = jnp.maximum(m_i[:, lo:], mx[h])
            p_ref[h, :, lo:] = jnp.exp2(s_ref[h, :, lo:] - m_new).astype(BF16)
            alpha = jnp.exp2(m_i[:, lo:] - m_new)
            if lo:
                m_new = jnp.concatenate([m_i[:, :lo], m_new], axis=1)
                alpha = jnp.concatenate([jnp.ones((1, lo), F32), alpha], axis=1)
            out.append((m_new, alpha, l_i, acc))
        return tuple(out)

    def values(grp, p_ref, st, lo=0, live=None):
        out = []
        for h in range(2):
            m_i, alpha, l_i, acc = st[h]
            blocks = [vt_ref[jnp.clip(grp * group + j, 0, nb - 1), h * HEAD_DIM:(h + 1) * HEAD_DIM, :]
                      for j in range(group)]
            ones = jnp.ones((16, keys), BF16)
            pv = _dot(jnp.concatenate([jnp.concatenate(blocks, axis=1), ones], axis=0), p_ref[h, :, lo:])
            if live is not None:
                pv = pv * live
            if lo:
                pv = jnp.concatenate([jnp.zeros((pv.shape[0], lo), F32), pv], axis=1)
            out.append((m_i, alpha, alpha * l_i + pv[HEAD_DIM:HEAD_DIM + 1], alpha * acc + pv[:HEAD_DIM]))
        return tuple(out)

    mx_first = scores(0, s_a)

    @pl.when((pl.program_id(0) == 0) & (pl.program_id(1) == 0) & (pl.program_id(2) == 0))
    def _():
        p_b[...] = jnp.zeros_like(p_b)

    def body(t, carry):
        st, mx_a = carry
        mx_b = scores(2 * t + 1, s_b)
        st = values(2 * t - 1, p_b, st, live=jnp.where(t > 0, 1.0, 0.0))
        st = softmax(s_a, mx_a, p_a, st)
        mx_a = scores(2 * t + 2, s_a)
        st = values(2 * t, p_a, st)
        return softmax(s_b, mx_b, p_b, st), mx_a

    init = (jnp.full((1, nq), -jnp.inf, F32), jnp.ones((1, nq), F32), jnp.zeros((1, nq), F32),
            jnp.zeros((HEAD_DIM, nq), F32))
    g0 = first // group
    st, _ = lax.fori_loop(0, g0 // 2, body, ((init, init), mx_first))
    def mask_in_place(grp, s_ref):
        mx = []
        for h in range(2):
            key_pos = (grp * group - first) * blk + lax.broadcasted_iota(I32, (keys, nq), 0)
            s = jnp.where(key_pos <= lax.broadcasted_iota(I32, (keys, nq), 1), s_ref[h], NEG_INF)
            s_ref[h] = s
            mx.append(jnp.max(s, axis=0, keepdims=True))
        return tuple(mx)

    lo_b = 0
    for pair in range(qblocks // (2 * group)):
        ga = g0 + 2 * pair
        lo = 2 * pair * keys
        mx_a = mask_in_place(ga, s_a) if pair == 0 else scores(ga, s_a, causal=True, lo=lo)
        st = values(ga - 1, p_b, st, lo=lo_b, live=jnp.where(ga > 0, 1.0, 0.0))
        lo_b = lo + keys
        mx_b = scores(ga + 1, s_b, causal=True, lo=lo_b)
        st = softmax(s_a, mx_a, p_a, st, lo=lo)
        st = values(ga, p_a, st, lo=lo)
        st = softmax(s_b, mx_b, p_b, st, lo=lo_b)
    st = values(g0 + qblocks // group - 1, p_b, st, lo=lo_b)
    both = jnp.concatenate([st[h][3] / st[h][2] for h in range(2)], axis=0)
    o_ref[...] = both.T.astype(BF16)


def _moba(qt, k, vt4, km, bsz, seq):
    n_tok, attn_w = k.shape
    blk = MOBA_BLOCK
    nb = seq // blk
    group, qblocks = 2, 4
    nbp = -(-nb // 16) * 16
    assert nb % qblocks == 0 and qblocks % (2 * group) == 0 and nbp <= LANES
    hp = attn_w // LANES
    km_pad = jnp.zeros((bsz, nbp, attn_w), F32).at[:, :nb].set(km.reshape(bsz, nb, attn_w))
    k4 = k.reshape(bsz, nb, blk, attn_w)
    steps = nb // qblocks
    return pl.pallas_call(
        functools.partial(_moba_kernel, nb=nb, nbp=nbp, qblocks=qblocks),
        grid=(bsz, hp, steps),
        in_specs=[pl.BlockSpec((None, LANES, qblocks * blk), lambda b, p, i: (b, p, i)),
                  pl.BlockSpec((None, nb, blk, LANES), lambda b, p, i: (b, 0, 0, p)),
                  pl.BlockSpec((None, nb, LANES, blk), lambda b, p, i: (b, 0, p, 0)),
                  pl.BlockSpec((None, nbp, LANES), lambda b, p, i: (b, 0, p))],
        out_specs=pl.BlockSpec((qblocks * blk, LANES), lambda b, p, i: (b * steps + i, p)),
        out_shape=jax.ShapeDtypeStruct((n_tok, attn_w), BF16),
        scratch_shapes=([pltpu.VMEM((2, group * blk, qblocks * blk), F32)] * 2
                        + [pltpu.VMEM((2, group * blk, qblocks * blk), BF16)] * 2),
        compiler_params=_cparams(("arbitrary", "arbitrary", "arbitrary")),
        name="moba",
    )(qt, k4, vt4, km_pad)


def _rms(t, g):
    return t * lax.rsqrt(jnp.mean(t * t, axis=-1, keepdims=True) + NORM_EPS) * g


def _store_slab(ref, val, base=0):
    rows, d = val.shape
    sub = d // LANES
    for s in range(sub):
        ref[pl.ds(base + s, rows, stride=sub), :] = val[:, s * LANES:(s + 1) * LANES]


def _load_slab(ref, rows, sub, base=0):
    return jnp.concatenate([ref[pl.ds(base + s, rows, stride=sub), :] for s in range(sub)], axis=1)


def _merge_kernel(ys_ref, u_ref, ya_ref, gs_ref, ga_ref, x_ref, d_ref, wglu_ref, wsb_ref, wab_ref,
                  wout_ref, postg_ref, gtm_ref, preg_ref, scf_ref, shf_ref, rw_ref, rb_ref,
                  x1_ref, h2_ref, tw_ref, loc_ref, before_ref, cnt_ref, run_ref):
    @pl.when(pl.program_id(0) == 0)
    def _():
        run_ref[...] = jnp.zeros_like(run_ref)

    tm = x_ref.shape[0]
    hm = tm // 2
    halves = [pl.ds(0, hm), pl.ds(hm, hm)]
    sub = x_ref.shape[1] // LANES

    ys = [jax.nn.gelu(ys_ref[r, :] + d_ref[...] * u_ref[r, :].astype(F32)) for r in halves]
    h2s = []
    for r, y in zip(halves, ys):
        y = y * jax.nn.sigmoid(_dot(y.astype(BF16), wglu_ref[...]))
        bs = _dot(y.astype(BF16), wsb_ref[...])
        ba = _dot(ya_ref[r, :], wab_ref[...])
        merged = gs_ref[r, :].astype(F32) * bs + ga_ref[r, :].astype(F32) * ba
        mix = _dot(merged.astype(BF16), wout_ref[...])
        x1 = x_ref[r, :] + gtm_ref[0] * _rms(mix, postg_ref[...])
        x1_ref[r, :] = x1
        h2s.append(_rms(x1, preg_ref[...]) * (1.0 + scf_ref[0]) + shf_ref[0])
    for j, h2 in enumerate(h2s):
        _store_slab(h2_ref, h2, base=j * hm * sub)

    lane = lax.broadcasted_iota(I32, (hm, LANES), 1)
    lane_f = lane.astype(F32)
    r_i = lax.broadcasted_iota(I32, (hm, hm), 0)
    c_i = lax.broadcasted_iota(I32, (hm, hm), 1)
    tri = jnp.where(c_i < r_i, 1.0, 0.0).astype(BF16)
    before_ref[0] = run_ref[...]
    gs = []
    for h2 in h2s:
        h_hi = h2.astype(BF16)
        h_lo = (h2 - h_hi.astype(F32)).astype(BF16)
        gs.append(_dot(jnp.concatenate([h_hi, h_hi, h_lo], axis=1), rw_ref[...]) + rb_ref[...])
    topk = [([], []) for _ in halves]
    for _ in range(TOP_K):
        for j in range(2):
            m = jnp.max(gs[j], axis=1, keepdims=True)
            idx = jnp.min(jnp.where(gs[j] == m, lane_f, float(LANES)), axis=1, keepdims=True)
            pick = lane_f == idx
            topk[j][0].append(m)
            topk[j][1].append(pick)
            gs[j] = jnp.where(pick, -jnp.inf, gs[j])
    onehots = [jnp.where(p[0] | p[1] | p[2] | p[3], 1.0, 0.0) for _, p in topk]
    cnt_tile = sum(jnp.sum(o, axis=0, keepdims=True) for o in onehots)
    cnt_hi = jnp.floor(cnt_tile * (1.0 / 256.0))
    parts = jnp.concatenate([cnt_hi, cnt_tile - 256.0 * cnt_hi, jnp.zeros((6, LANES), F32)], axis=0)
    before_e = jnp.where(lax.broadcasted_iota(I32, (LANES, LANES), 0) < lax.broadcasted_iota(I32, (LANES, LANES), 1),
                         1.0, 0.0).astype(BF16)
    pref = _dot(parts.astype(BF16), before_e)
    off_tile = 256.0 * pref[0:1] + pref[1:2]
    for r, (vals, picks), onehot in zip(halves, topk, onehots):
        exps = [jnp.exp(v - vals[0]) for v in vals]
        tot = exps[0] + exps[1] + exps[2] + exps[3]

        rank_full = _dot(tri, onehot.astype(BF16)) + (run_ref[...] - before_ref[0]) + off_tile
        run_ref[...] = run_ref[...] + jnp.sum(onehot, axis=0, keepdims=True)

        tw = jnp.zeros((hm, LANES), F32)
        rk = jnp.zeros((hm, LANES), F32)
        for k in range(TOP_K):
            tw = jnp.where(lane == k, exps[k] / tot, tw)
            pos = jnp.sum(jnp.where(picks[k], rank_full, 0.0), axis=1, keepdims=True)
            rk = jnp.where(lane == k, pos, rk)
        tw_ref[r, :] = tw
        loc_ref[r, :] = rk.astype(I32)
    cnt_ref[...] = run_ref[...]


def _merge(ys, u, ya, gs, ga, x2, ssm_d, w_glu, w_sb, w_ab, w_out, post_g, gt_m, pre_g, sc_f, sh_f,
           router_w, router_b, seq, tm):
    n_tok, d = x2.shape
    sw = ys.shape[1]
    aw = ya.shape[1]
    ne = router_w.shape[1]
    per_b = seq // tm
    rw = jnp.zeros((d, LANES), F32).at[:, :ne].set(router_w)
    rw_hi = rw.astype(BF16)
    rw_lo = (rw - rw_hi.astype(F32)).astype(BF16)
    rw = jnp.concatenate([rw_hi, rw_lo, rw_hi], axis=0)
    rb = jnp.full((1, LANES), NEG_INF, F32).at[0, :ne].set(router_b)
    tok = lambda w: pl.BlockSpec((tm, w), lambda i: (i, 0))
    full = lambda a, b: pl.BlockSpec((a, b), lambda i: (0, 0))
    bvec = pl.BlockSpec((1, 1, d), lambda i: (i // per_b, 0, 0))
    return pl.pallas_call(
        _merge_kernel,
        grid=(n_tok // tm,),
        in_specs=[tok(sw), tok(sw), tok(aw), tok(d), tok(d), tok(d),
                  full(1, sw), full(sw, sw), full(sw, d), full(aw, d), full(d, d),
                  full(1, d), bvec, full(1, d), bvec, bvec, full(3 * d, LANES), full(1, LANES)],
        out_specs=[tok(d), pl.BlockSpec((tm * (d // LANES), LANES), lambda i: (i, 0)),
                   tok(LANES), tok(LANES),
                   pl.BlockSpec((1, 1, LANES), lambda i: (i, 0, 0)), full(1, LANES)],
        out_shape=[jax.ShapeDtypeStruct((n_tok, d), F32),
                   jax.ShapeDtypeStruct((n_tok * (d // LANES), LANES), F32),
                   jax.ShapeDtypeStruct((n_tok, LANES), F32),
                   jax.ShapeDtypeStruct((n_tok, LANES), I32),
                   jax.ShapeDtypeStruct((n_tok // tm, 1, LANES), F32),
                   jax.ShapeDtypeStruct((1, LANES), F32)],
        scratch_shapes=[pltpu.VMEM((1, LANES), F32)],
        compiler_params=_cparams(("arbitrary",)),
        name="merge",
    )(ys, u, ya, gs, ga, x2, ssm_d.reshape(1, sw), w_glu.astype(BF16), w_sb.astype(BF16),
      w_ab.astype(BF16), w_out.astype(BF16), post_g.reshape(1, d), gt_m, pre_g.reshape(1, d),
      sc_f, sh_f, rw, rb)


def _chunk_copies(cnt_ref, start_ref, tile, ne, max_rows, make):
    def per_expert(e, off):
        c, start = cnt_ref[tile * ne + e], start_ref[tile * ne + e]
        for b in range(max_rows.bit_length()):
            n = 1 << b

            @pl.when((c >> b) & 1 == 1)
            def _():
                make(off + (c & (n - 1)), start + (c & (n - 1)), n)
        return off + c

    lax.fori_loop(0, ne, per_expert, 0)


def _dispatch_kernel(cnt_ref, start_ref, pad_start_ref, pad_cnt_ref, tail_ref, loct_ref, h_ref, xs_ref,
                     buf, zero_ref, sem, zsem, *, tm, sub, bm, n_rows, ne):
    i = pl.program_id(0)
    last = pl.num_programs(0) - 1
    slot = i % 2
    n_loc = tm * TOP_K

    def row(ref, r, n=1):
        return ref.at[pl.ds(pl.multiple_of(r * sub, sub), n * sub)]

    def wait_slot(s):
        pltpu.make_async_copy(buf.at[s], row(xs_ref, 0, n_loc), sem.at[s]).wait()

    def zero_copies(do):
        def per_expert(e, _):
            start, cnt = pad_start_ref[e], pad_cnt_ref[e]
            for b in range(bm.bit_length() - 1):
                n = 1 << b

                @pl.when((cnt >> b) & 1 == 1)
                def _():
                    do(pltpu.make_async_copy(row(zero_ref, 0, n), row(xs_ref, start + (cnt & (n - 1)), n), zsem))
            return 0

        lax.fori_loop(0, ne, per_expert, 0)

        def per_block(j, _):
            r = tail_ref[0] + j * bm

            @pl.when(r < n_rows)
            def _():
                do(pltpu.make_async_copy(row(zero_ref, 0, bm), row(xs_ref, r, bm), zsem))
            return 0

        lax.fori_loop(0, ne, per_block, 0)

    @pl.when(i == 0)
    def _():
        zero_ref[...] = jnp.zeros_like(zero_ref)
        zero_copies(lambda cp: cp.start())

    @pl.when(i >= 2)
    def _():
        wait_slot(slot)

    loct = loct_ref[0]
    row_i = lax.broadcasted_iota(I32, (n_loc, tm), 0)
    place = row_i == loct[0:1, :]
    for k in range(1, TOP_K):
        place = place | (row_i == loct[k:k + 1, :])
    grouped = _dot(jnp.where(place, 1.0, 0.0).astype(BF16), _load_slab(h_ref, tm, sub).astype(BF16))
    _store_slab(buf.at[slot], grouped)
    _chunk_copies(cnt_ref, start_ref, i, ne, tm,
                  lambda loc, glob, n: pltpu.make_async_copy(row(buf.at[slot], loc, n), row(xs_ref, glob, n),
                                                             sem.at[slot]).start())

    @pl.when(i == last)
    def _():
        wait_slot(slot)

    @pl.when((i == last) & (i >= 1))
    def _():
        wait_slot(1 - slot)

    @pl.when(i == 0)
    def _():
        zero_copies(lambda cp: cp.wait())


def _dispatch(h_slab, loc, cnt_tile, start_tile, pad_start, pad_cnt, tail, n_rows, sub, bm, tm):
    n_tok = loc.shape[0]
    nt = n_tok // tm
    ne = pad_start.shape[0]
    loct = jnp.full((nt, 8, tm), -1, I32).at[:, :TOP_K].set(loc.reshape(nt, tm, TOP_K).transpose(0, 2, 1))
    grid_spec = pltpu.PrefetchScalarGridSpec(
        num_scalar_prefetch=5,
        grid=(nt,),
        in_specs=[pl.BlockSpec((1, 8, tm), lambda i, *_: (i, 0, 0)),
                  pl.BlockSpec((tm * sub, LANES), lambda i, *_: (i, 0))],
        out_specs=pl.BlockSpec(memory_space=pl.ANY),
        scratch_shapes=[pltpu.VMEM((2, TOP_K * tm * sub, LANES), F32), pltpu.VMEM((bm * sub, LANES), F32),
                        pltpu.SemaphoreType.DMA((2,)), pltpu.SemaphoreType.DMA(())],
    )
    return pl.pallas_call(
        functools.partial(_dispatch_kernel, tm=tm, sub=sub, bm=bm, n_rows=n_rows, ne=ne),
        grid_spec=grid_spec,
        out_shape=jax.ShapeDtypeStruct((n_rows * sub, LANES), F32),
        compiler_params=_cparams(("arbitrary",)),
        name="dispatch",
    )(cnt_tile.reshape(-1), start_tile.reshape(-1), pad_start, pad_cnt, tail, loct, h_slab)


def _expert_kernel(be_ref, nv_ref, x_ref, wg_ref, bg_ref, wu_ref, bu_ref, wd_ref, bd_ref,
                   o_ref, wg_s, wu_s, wd_s, *, bm, sub):
    i = pl.program_id(0)
    prev = be_ref[jnp.maximum(i - 1, 0)]
    changed = (i == 0) | (be_ref[i] != prev)

    @pl.when(changed)
    def _():
        wg_s[...] = wg_ref[0].astype(BF16)
        wu_s[...] = wu_ref[0].astype(BF16)
        wd_s[...] = wd_ref[0].astype(BF16)

    @pl.when(i < nv_ref[0])
    def _():
        xb = _load_slab(x_ref, bm, sub).astype(BF16)
        g = _dot(xb, wg_s[...]) + bg_ref[0]
        u = _dot(xb, wu_s[...]) + bu_ref[0]
        g = jnp.minimum(g, SWIGLU_LIMIT)
        u = jnp.clip(u, -SWIGLU_LIMIT, SWIGLU_LIMIT)
        act = g * jax.nn.sigmoid(SWIGLU_ALPHA * g) * (u + 1.0)
        _store_slab(o_ref, _dot(act.astype(BF16), wd_s[...]) + bd_ref[0])

    @pl.when(i >= nv_ref[0])
    def _():
        o_ref[...] = jnp.zeros_like(o_ref)


def _experts(xs, blk_exp, n_valid, w_gate, b_gate, w_up, b_up, w_down, b_down, bm):
    ne, d, f = w_gate.shape
    sub = d // LANES
    n_blocks = xs.shape[0] // (bm * sub)
    wspec = lambda a, b: pl.BlockSpec((1, a, b), lambda i, be, nv: (be[i], 0, 0))
    grid_spec = pltpu.PrefetchScalarGridSpec(
        num_scalar_prefetch=2,
        grid=(n_blocks,),
        in_specs=[pl.BlockSpec((bm * sub, LANES), lambda i, be, nv: (jnp.clip(nv[0] - 1, 0, i), 0)),
                  wspec(d, f), wspec(1, f), wspec(d, f), wspec(1, f), wspec(f, d), wspec(1, d)],
        out_specs=pl.BlockSpec((bm * sub, LANES), lambda i, be, nv: (i, 0)),
        scratch_shapes=[pltpu.VMEM((d, f), BF16), pltpu.VMEM((d, f), BF16), pltpu.VMEM((f, d), BF16)],
    )
    return pl.pallas_call(
        functools.partial(_expert_kernel, bm=bm, sub=sub),
        grid_spec=grid_spec,
        out_shape=jax.ShapeDtypeStruct(xs.shape, F32),
        compiler_params=_cparams(("arbitrary",)),
        name="experts",
    )(blk_exp, n_valid, xs, w_gate, b_gate.reshape(ne, 1, f), w_up, b_up.reshape(ne, 1, f),
      w_down, b_down.reshape(ne, 1, d))


def _combine_kernel(cnt_ref, start_ref, yb_ref, loc_ref, tw_ref, x1_ref, g_ref, gt_ref, o_ref, buf, sem,
                    *, tm, sub, ne):
    i = pl.program_id(0)
    n_loc = tm * TOP_K

    def start_tile(tile, slot):
        def make(loc, glob, n):
            src = yb_ref.at[pl.ds(pl.multiple_of(glob * sub, sub), n * sub)]
            dst = buf.at[slot, pl.ds(pl.multiple_of(loc * sub, sub), n * sub)]
            pltpu.make_async_copy(src, dst, sem.at[slot]).start()

        _chunk_copies(cnt_ref, start_ref, tile, ne, tm, make)

    slot = i % 2

    @pl.when(i == 0)
    def _():
        start_tile(0, 0)

    @pl.when(i + 1 < pl.num_programs(0))
    def _():
        start_tile(i + 1, 1 - slot)

    pltpu.make_async_copy(yb_ref.at[pl.ds(0, n_loc * sub)], buf.at[slot], sem.at[slot]).wait()
    rows = _load_slab(buf.at[slot], n_loc, sub).astype(BF16)
    loc, tw = loc_ref[...], tw_ref[...]
    col_i = lax.broadcasted_iota(I32, (tm, n_loc), 1)
    w = jnp.zeros((tm, n_loc), F32)
    for k in range(TOP_K):
        w = jnp.where(col_i == loc[:, k:k + 1], tw[:, k:k + 1], w)
    ff = _dot(w.astype(BF16), rows)
    o_ref[...] = x1_ref[...] + gt_ref[0] * _rms(ff, g_ref[...])


def _combine(yb, loc, tw, cnt_tile, start_tile, x1, post_g, gt_f, seq, tm):
    n_tok, d = x1.shape
    sub = d // LANES
    nt = n_tok // tm
    ne = cnt_tile.shape[1]
    per_b = seq // tm
    assert loc.shape == (n_tok, LANES)
    grid_spec = pltpu.PrefetchScalarGridSpec(
        num_scalar_prefetch=2,
        grid=(nt,),
        in_specs=[pl.BlockSpec(memory_space=pl.ANY),
                  pl.BlockSpec((tm, LANES), lambda i, *_: (i, 0)),
                  pl.BlockSpec((tm, LANES), lambda i, *_: (i, 0)),
                  pl.BlockSpec((tm, d), lambda i, *_: (i, 0)),
                  pl.BlockSpec((1, d), lambda i, *_: (0, 0)),
                  pl.BlockSpec((1, 1, d), lambda i, *_: (i // per_b, 0, 0))],
        out_specs=pl.BlockSpec((tm, d), lambda i, *_: (i, 0)),
        scratch_shapes=[pltpu.VMEM((2, TOP_K * tm * sub, LANES), F32), pltpu.SemaphoreType.DMA((2,))],
    )
    return pl.pallas_call(
        functools.partial(_combine_kernel, tm=tm, sub=sub, ne=ne),
        grid_spec=grid_spec,
        out_shape=jax.ShapeDtypeStruct((n_tok, d), F32),
        compiler_params=_cparams(("arbitrary",)),
        name="combine",
    )(cnt_tile.reshape(-1), start_tile.reshape(-1), yb, loc, tw, x1, post_g.reshape(1, d), gt_f)


def kernel(x, c, positions, ada_w, ada_b, mix_pre_g, mix_post_g, ffn_pre_g, ffn_post_g, w_in, ssm_lam_re, ssm_lam_im, ssm_log_dt, ssm_b_re, ssm_b_im, ssm_c_re, ssm_c_im, ssm_d, ssm_w_glu, w_ssm_branch, w_attn_branch, w_out, router_w, router_b, w_gate, b_gate, w_up, b_up, w_down, b_down):
    bsz, seq, d = x.shape
    depth = ada_w.shape[0]
    n_tok = bsz * seq
    bm = EXPERT_ROWS
    sub = d // LANES
    xcur = x.reshape(n_tok, d)
    pos = positions.reshape(n_tok, 1).astype(I32)
    for l in range(depth):
        ada = _ada(c, ada_w[l], ada_b[l])
        sh_m, sc_m, gt_m, sh_f, sc_f, gt_f = [a.reshape(bsz, 1, d) for a in jnp.split(ada, 6, axis=-1)]

        u, q, k, v, gs, ga, km = _inproj(xcur, sc_m, sh_m, mix_pre_g[l].reshape(1, d), pos, w_in[l],
                                         seq, tm=min(512, seq))
        ys = _s5(u, ssm_lam_re[l], ssm_lam_im[l], ssm_log_dt[l], ssm_b_re[l], ssm_b_im[l],
                 ssm_c_re[l], ssm_c_im[l], bsz, seq)
        ya = _moba(q, k, v, km, bsz, seq)
        tm = ROUTE_TILE
        x1, h2, tw, loc, before, cnt = _merge(
            ys, u, ya, gs, ga, xcur, ssm_d[l], ssm_w_glu[l], w_ssm_branch[l], w_attn_branch[l], w_out[l],
            mix_post_g[l], gt_m, ffn_pre_g[l], sc_f, sh_f, router_w[l], router_b[l], seq, tm=tm)

        ne = router_w.shape[-1]
        nt = n_tok // tm
        counts = cnt[0, :ne].astype(I32)
        padded = (counts + bm - 1) // bm * bm
        p_ends = jnp.cumsum(padded)
        p_starts = p_ends - padded
        before = before.reshape(nt, LANES)[:, :ne].astype(I32)
        cnt_tile = jnp.concatenate([before[1:], counts[None]], axis=0) - before
        start_tile = p_starts[None, :] + before
        n_blocks = (n_tok * TOP_K) // bm + ne
        n_rows = n_blocks * bm
        blk_start = jnp.arange(n_blocks, dtype=I32)[:, None] * bm
        blk_exp = jnp.minimum(jnp.sum((blk_start >= p_ends[None, :]).astype(I32), axis=1), ne - 1)
        n_valid = (p_ends[-1:] // bm).astype(I32)

        xs = _dispatch(h2, loc[:, :TOP_K], cnt_tile, start_tile, p_starts + counts, padded - counts, p_ends[-1:],
                       n_rows, sub, bm, tm)
        yb = _experts(xs, blk_exp, n_valid, w_gate[l], b_gate[l], w_up[l], b_up[l], w_down[l], b_down[l], bm)
        xcur = _combine(yb, loc, tw, cnt_tile, start_tile, x1, ffn_post_g[l], gt_f, seq, tm)
    return xcur.reshape(bsz, seq, d).astype(x.dtype)
```

```python
import functools
import math

import jax
import jax.numpy as jnp
from jax import lax
from jax.experimental import pallas as pl
from jax.experimental.pallas import tpu as pltpu

F32 = jnp.float32
BF16 = jnp.bfloat16
I32 = jnp.int32

N_HEADS = 8
HEAD_DIM = 64
ROPE_THETA = 10000.0
MOBA_BLOCK = 256
MOBA_TOPK = 3
SSM_GROUP_SIZE = 16
SSM_GROUPS = 32
SSM_STATE = 64
TOP_K = 4
SWIGLU_ALPHA = 1.702
SWIGLU_LIMIT = 7.0
NORM_EPS = 1e-6
NEG_INF = -1e30

LANES = 128
SSM_CHUNK = 64
EXPERT_ROWS = 512
ROUTE_TILE = 512
VMEM_LIMIT = 56 * 1024 * 1024
HIGHEST = lax.Precision.HIGHEST
Q_SCALE = HEAD_DIM ** -0.5 * math.log2(math.e)


def _cparams(sem):
    return pltpu.CompilerParams(dimension_semantics=sem, vmem_limit_bytes=VMEM_LIMIT)


def _dot(a, b, **kw):
    return jnp.dot(a, b, preferred_element_type=F32, **kw)


def _dot_t(a, b, **kw):
    return lax.dot_general(a, b, (((1,), (1,)), ((), ())), preferred_element_type=F32, **kw)


def _ada_kernel(c_ref, w_ref, b_ref, o_ref):
    c = c_ref[...]
    cond = c * jax.nn.sigmoid(c)
    o_ref[...] = _dot(cond, w_ref[...], precision=HIGHEST) + b_ref[...]


def _ada(c, ada_w, ada_b):
    bsz, d = c.shape
    n = ada_w.shape[1]
    c8 = jnp.zeros((8, d), F32).at[:bsz].set(c)
    out = pl.pallas_call(
        _ada_kernel,
        grid=(n // d,),
        in_specs=[pl.BlockSpec((8, d), lambda j: (0, 0)),
                  pl.BlockSpec((d, d), lambda j: (0, j)),
                  pl.BlockSpec((1, d), lambda j: (0, j))],
        out_specs=pl.BlockSpec((8, d), lambda j: (0, j)),
        out_shape=jax.ShapeDtypeStruct((8, n), F32),
        compiler_params=_cparams(("arbitrary",)),
        name="ada",
    )(c8, ada_w, ada_b.reshape(1, n))
    return out[:bsz]


def _transpose_kernel(w_ref, o_ref):
    o_ref[...] = w_ref[...].T.astype(BF16)


def _inproj_kernel(x_ref, sc_ref, sh_ref, g_ref, post_ref, invft_ref, w_ref, wt_ref,
                   u_ref, qt_ref, k_ref, vt_ref, gs_ref, ga_ref, km_ref, *, ssm_w, attn_w, d_model):
    x = x_ref[...]
    ms = jnp.mean(x * x, axis=-1, keepdims=True)
    xn = x * lax.rsqrt(ms + NORM_EPS) * g_ref[...]
    h = (xn * (1.0 + sc_ref[0]) + sh_ref[0]).astype(BF16)

    def proj(lo, width):
        return _dot(h, w_ref[:, lo:lo + width])

    tm = x.shape[0]
    ang_t = invft_ref[...] * post_ref[0].astype(F32)
    cos_t, sin_t = jnp.cos(ang_t), jnp.sin(ang_t)
    fold = LANES // (HEAD_DIM // 2)
    cos128 = jnp.concatenate([cos_t] * fold, axis=0).T
    sin128 = jnp.concatenate([sin_t] * fold, axis=0).T
    reps = attn_w // LANES
    cos = jnp.concatenate([cos128] * reps, axis=1)
    sin = jnp.concatenate([sin128] * reps, axis=1)
    lane = lax.broadcasted_iota(I32, (1, attn_w), 1)
    first = (lane % HEAD_DIM) < (HEAD_DIM // 2)
    sin = jnp.where(first, -sin, sin)

    u_ref[...] = proj(0, ssm_w)
    gs_ref[...] = jax.nn.sigmoid(proj(ssm_w + 3 * attn_w, d_model)).astype(BF16)
    ga_ref[...] = jax.nn.sigmoid(proj(ssm_w + 3 * attn_w + d_model, d_model)).astype(BF16)
    vt = _dot_t(wt_ref[attn_w:2 * attn_w, :], h).astype(BF16)
    nblk = tm // MOBA_BLOCK
    for j in range(nblk):
        vt_ref[j] = vt[:, j * MOBA_BLOCK:(j + 1) * MOBA_BLOCK]

    def rope(t):
        rot = jnp.where(first, pltpu.roll(t, attn_w - HEAD_DIM // 2, axis=1),
                        pltpu.roll(t, HEAD_DIM // 2, axis=1))
        return t * cos + rot * sin

    k = rope(proj(ssm_w + attn_w, attn_w))
    k_ref[...] = k.astype(BF16)
    km_ref[0] = jnp.mean(k.reshape(nblk, MOBA_BLOCK, attn_w), axis=1)

    half = HEAD_DIM // 2
    qt = _dot_t(wt_ref[0:attn_w, :], h).reshape(N_HEADS, 2, half, tm)
    t1, t2 = qt[:, 0], qt[:, 1]
    qt = jnp.stack([t1 * cos_t - t2 * sin_t, t2 * cos_t + t1 * sin_t], axis=1).reshape(attn_w, tm)
    qt_ref[...] = (qt * Q_SCALE).astype(BF16)


def _inproj(x2, sc, sh, g, pos, w_in, seq, tm):
    n_tok, d = x2.shape
    attn_w = N_HEADS * HEAD_DIM
    ssm_w = SSM_GROUPS * SSM_GROUP_SIZE
    in_w = w_in.shape[1]
    half = HEAD_DIM // 2
    inv_freq = ROPE_THETA ** (-jnp.arange(half, dtype=F32) / half)
    nt = n_tok // tm
    per_b = seq // tm
    nblk = tm // MOBA_BLOCK
    bsz = n_tok // seq
    blk = MOBA_BLOCK
    assert ssm_w % attn_w == 0
    q_blk, v_blk = ssm_w // attn_w, ssm_w // attn_w + 2
    wt = pl.pallas_call(
        _transpose_kernel,
        grid=(2,),
        in_specs=[pl.BlockSpec((d, attn_w), lambda j: (0, q_blk + j * (v_blk - q_blk)))],
        out_specs=pl.BlockSpec((attn_w, d), lambda j: (j, 0)),
        out_shape=jax.ShapeDtypeStruct((2 * attn_w, d), BF16),
        compiler_params=_cparams(("arbitrary",)),
        name="wt",
    )(w_in)
    tok = lambda w: pl.BlockSpec((tm, w), lambda i: (i, 0))
    full = lambda a, b: pl.BlockSpec((a, b), lambda i: (0, 0))
    bvec = pl.BlockSpec((1, 1, d), lambda i: (i // per_b, 0, 0))
    outs = pl.pallas_call(
        functools.partial(_inproj_kernel, ssm_w=ssm_w, attn_w=attn_w, d_model=d),
        grid=(nt,),
        in_specs=[tok(d), bvec, bvec, full(1, d),
                  pl.BlockSpec((1, 1, tm), lambda i: (i, 0, 0)), full(half, 1),
                  full(d, in_w), full(2 * attn_w, d)],
        out_specs=[tok(ssm_w),
                   pl.BlockSpec((None, attn_w, tm), lambda i: (i // per_b, 0, i % per_b)),
                   tok(attn_w),
                   pl.BlockSpec((None, nblk, attn_w, blk), lambda i: (i // per_b, i % per_b, 0, 0)),
                   tok(d), tok(d),
                   pl.BlockSpec((1, nblk, attn_w), lambda i: (i, 0, 0))],
        out_shape=[jax.ShapeDtypeStruct((n_tok, ssm_w), F32),
                   jax.ShapeDtypeStruct((bsz, attn_w, seq), BF16),
                   jax.ShapeDtypeStruct((n_tok, attn_w), BF16),
                   jax.ShapeDtypeStruct((bsz, seq // blk, attn_w, blk), BF16),
                   jax.ShapeDtypeStruct((n_tok, d), BF16),
                   jax.ShapeDtypeStruct((n_tok, d), BF16),
                   jax.ShapeDtypeStruct((nt, nblk, attn_w), F32)],
        compiler_params=_cparams(("arbitrary",)),
        name="inproj",
    )(x2, sc, sh, g, pos.reshape(nt, 1, tm), inv_freq.reshape(half, 1), w_in.astype(BF16), wt)
    return outs


def _s5_group(x, zr, zi, lr, li, ca, cb, ba, bb, t_ref, *, chunk, n_chunks):
    gs = SSM_GROUP_SIZE

    def powers(tau):
        mag = jnp.exp(tau * zr)
        return mag * jnp.cos(tau * zi), mag * jnp.sin(tau * zi)

    tau = lax.broadcasted_iota(I32, (chunk + 8, LANES), 0).astype(F32)
    e_re, e_im = powers(tau)
    lb_re, lb_im = e_re[1:2], e_im[1:2]
    den = lr * lr + li * li
    a, b = lb_re - 1.0, lb_im
    cf_re = (a * lr + b * li) / den
    cf_im = (b * lr - a * li) / den
    bri = cf_re * ba + cf_im * bb
    bri_sw = cf_re * bb - cf_im * ba

    cpow =(e_re[:chunk + 1, None, :] * ca[None] + e_im[:chunk + 1, None, :] * cb[None])
    cpow = cpow.reshape((chunk + 1) * gs, LANES)
    width = chunk * gs
    r = _dot_t(bri, cpow[:width], precision=HIGHEST)
    col = lax.broadcasted_iota(I32, (gs, width), 1)
    t_ref[0:gs, :] = r.astype(BF16)
    for j in range(1, chunk):
        shifted = jnp.where(col >= gs * j, pltpu.roll(r, gs * j, axis=1), 0.0)
        t_ref[gs * j:gs * (j + 1), :] = shifted.astype(BF16)

    y = _dot(x, t_ref[...])

    tau_rev = (chunk - 1) - lax.broadcasted_iota(I32, (chunk, LANES), 0)
    r_re, r_im = powers(tau_rev.astype(F32))
    bst = r_re[:, None, :] * bri[None] + r_im[:, None, :] * bri_sw[None]
    bst = bst.reshape(width, LANES).astype(BF16)
    s = _dot(x, bst)

    rows = s.shape[0]
    n_idx = lax.broadcasted_iota(I32, (rows, LANES), 0) % n_chunks
    lane = lax.broadcasted_iota(I32, (1, LANES), 1)
    half = LANES // 2

    def cmul(v, p_re, p_im):
        return v * p_re + pltpu.roll(v, half, axis=1) * jnp.where(lane < half, -p_im, p_im)

    sh = 1
    while sh < n_chunks:
        p_re, p_im = powers(jnp.full((1, LANES), float(chunk * sh), F32))
        prev = jnp.where(n_idx >= sh, pltpu.roll(s, sh, axis=0), 0.0)
        s = s + cmul(prev, p_re, p_im)
        sh *= 2
    s_in = jnp.where(n_idx >= 1, pltpu.roll(s, 1, axis=0), 0.0)
    return y + _dot_t(s_in.astype(BF16), cpow[gs:gs + width].astype(BF16))


def _s5_kernel(u_ref, sel_ref, zr_ref, zi_ref, lr_ref, li_ref, ca_ref, cb_ref, ba_ref, bb_ref,
               y_ref, x_scr, y_scr, t_ref, *, chunk, n_chunks, rows):
    gpt = LANES // SSM_GROUP_SIZE

    def plane(j):
        return pl.ds(j, rows, stride=chunk)

    for jt in range(chunk // gpt):
        planes = jnp.concatenate([u_ref[plane(jt * gpt + jj), :] for jj in range(gpt)], axis=1)
        grouped = _dot(planes.astype(BF16), sel_ref[...])
        for g in range(gpt):
            x_scr[g, :, jt * LANES:(jt + 1) * LANES] = grouped[:, g * LANES:(g + 1) * LANES].astype(BF16)

    def per_group(g, _):
        y = _s5_group(x_scr[g], zr_ref[g], zi_ref[g], lr_ref[g], li_ref[g], ca_ref[g], cb_ref[g], ba_ref[g],
                      bb_ref[g], t_ref, chunk=chunk, n_chunks=n_chunks)
        y_scr[g] = y.astype(BF16)
        return 0

    lax.fori_loop(0, gpt, per_group, 0)

    for jt in range(chunk // gpt):
        grouped = jnp.concatenate([y_scr[g, :, jt * LANES:(jt + 1) * LANES] for g in range(gpt)], axis=1)
        planes = _dot(grouped, sel_ref[...])
        for jj in range(gpt):
            y_ref[plane(jt * gpt + jj), :] = planes[:, jj * LANES:(jj + 1) * LANES]


def _s5(u, lam_re, lam_im, log_dt, b_re, b_im, c_re, c_im, bsz, seq):
    g_n, gs, p = SSM_GROUPS, SSM_GROUP_SIZE, SSM_STATE
    chunk = SSM_CHUNK
    n_chunks = seq // chunk
    rows = bsz * n_chunks
    width = chunk * gs
    gpt = LANES // gs
    assert chunk % gpt == 0 and g_n % gpt == 0 and 2 * p == LANES
    dt = jnp.exp(log_dt.astype(F32))[:, None]
    dup = lambda a: jnp.concatenate([a, a], axis=-1).reshape(g_n, 1, 2 * p)
    zr, zi = dup(lam_re * dt), dup(lam_im * dt)
    lr, li = dup(lam_re), dup(lam_im)
    bt_re, bt_im = b_re.transpose(0, 2, 1), b_im.transpose(0, 2, 1)
    ba = jnp.concatenate([bt_re, bt_im], axis=-1)
    bb = jnp.concatenate([-bt_im, bt_re], axis=-1)
    ca = jnp.concatenate([c_re, -c_im], axis=-1)
    cb = jnp.concatenate([-c_im, -c_re], axis=-1)
    idx = jnp.arange(gpt * LANES, dtype=I32)
    swapped = ((idx // gs) % gpt) * LANES + (idx // LANES) * gs + idx % gs
    sel = (swapped[:, None] == idx[None, :]).astype(BF16)
    tile = pl.BlockSpec((bsz * seq, LANES), lambda t: (0, t))
    vec = pl.BlockSpec((gpt, 1, 2 * p), lambda t: (t, 0, 0))
    mat = pl.BlockSpec((gpt, gs, 2 * p), lambda t: (t, 0, 0))
    return pl.pallas_call(
        functools.partial(_s5_kernel, chunk=chunk, n_chunks=n_chunks, rows=rows),
        grid=(g_n // gpt,),
        in_specs=[tile, pl.BlockSpec((gpt * LANES, gpt * LANES), lambda t: (0, 0)),
                  vec, vec, vec, vec, mat, mat, mat, mat],
        out_specs=tile,
        out_shape=jax.ShapeDtypeStruct((bsz * seq, g_n * gs), F32),
        scratch_shapes=[pltpu.VMEM((gpt, rows, width), BF16), pltpu.VMEM((gpt, rows, width), BF16),
                        pltpu.VMEM((width, width), BF16)],
        compiler_params=_cparams(("arbitrary",)),
        name="s5",
    )(u, sel, zr, zi, lr, li, ca, cb, ba, bb)


def _moba_kernel(qt_ref, k_ref, vt_ref, km_ref, o_ref, s_a, s_b, p_a, p_b, *, nb, nbp, qblocks):
    blk = MOBA_BLOCK
    group = 2
    keys = group * blk
    nq = qblocks * blk
    first = qblocks * pl.program_id(2)
    qt = qt_ref[...]
    km = km_ref[...]
    km_hi = km.astype(BF16)
    km_lo = (km - km_hi.astype(F32)).astype(BF16)
    dim_i = lax.broadcasted_iota(I32, (LANES, nq), 0)
    blk_i = lax.broadcasted_iota(I32, (nbp, nq), 0)
    blk_f = blk_i.astype(F32)
    own = first + lax.broadcasted_iota(I32, (nbp, nq), 1) // blk
    valid = blk_i < own

    q_aug = []
    for h in range(2):
        qh = jnp.where((dim_i >= h * HEAD_DIM) & (dim_i < (h + 1) * HEAD_DIM), qt, jnp.zeros_like(qt))
        g = jnp.where(valid, _dot(km_hi, qh) + _dot(km_lo, qh), NEG_INF)
        sel = jnp.zeros((nbp, nq), jnp.bool_)
        for _ in range(MOBA_TOPK):
            m = jnp.max(g, axis=0, keepdims=True)
            idx = jnp.min(jnp.where(g == m, blk_f, float(nbp)), axis=0, keepdims=True)
            pick = blk_f == idx
            sel = sel | pick
            g = jnp.where(pick, -jnp.inf, g)
        ok = (sel & valid) | (blk_i == own)
        parts = [qh, jnp.where(ok, 0.0, NEG_INF).astype(BF16)]
        if nbp < LANES:
            parts.append(jnp.full((LANES - nbp, nq), NEG_INF, BF16))
        q_aug.append(jnp.concatenate(parts, axis=0))

    def scores(grp, s_ref, causal=False, lo=0):
        kb0 = grp * group
        k_cat = k_ref[pl.ds(jnp.minimum(kb0, nb - group), group)].reshape(keys, LANES)
        blk_id = kb0 + lax.broadcasted_iota(I32, (keys, LANES), 0) // blk
        onehot = jnp.where(lax.broadcasted_iota(I32, (keys, LANES), 1) == blk_id, 1.0, 0.0).astype(BF16)
        k_aug = jnp.concatenate([k_cat, onehot], axis=1)
        mx = []
        for h in range(2):
            s = _dot(k_aug, q_aug[h][:, lo:])
            if causal:
                key_pos = (kb0 - first) * blk + lax.broadcasted_iota(I32, (keys, nq - lo), 0)
                s = jnp.where(key_pos <= lo + lax.broadcasted_iota(I32, (keys, nq - lo), 1), s, NEG_INF)
            s_ref[h, :, lo:] = s
            mx.append(jnp.max(s, axis=0, keepdims=True))
        return tuple(mx)

    def softmax(s_ref, mx, p_ref, st, lo=0):
        out = []
        for h in range(2):
            m_i, _, l_i, acc = st[h]
            m_new = jnp.maximum(m_i[:, lo:], mx[h])
            p_ref[h, :, lo:] = jnp.exp2(s_ref[h, :, lo:] - m_new).astype(BF16)
            alpha = jnp.exp2(m_i[:, lo:] - m_new)
            if lo:
                m_new = jnp.concatenate([m_i[:, :lo], m_new], axis=1)
                alpha = jnp.concatenate([jnp.ones((1, lo), F32), alpha], axis=1)
            out.append((m_new, alpha, l_i, acc))
        return tuple(out)

    def values(grp, p_ref, st, lo=0, live=None):
        out = []
        for h in range(2):
            m_i, alpha, l_i, acc = st[h]
            blocks = [vt_ref[jnp.clip(grp * group + j, 0, nb - 1), h * HEAD_DIM:(h + 1) * HEAD_DIM, :]
                      for j in range(group)]
            ones = jnp.ones((16, keys), BF16)
            pv = _dot(jnp.concatenate([jnp.concatenate(blocks, axis=1), ones], axis=0), p_ref[h, :, lo:])
            if live is not None:
                pv = pv * live
            if lo:
                pv = jnp.concatenate([jnp.zeros((pv.shape[0], lo), F32), pv], axis=1)
            out.append((m_i, alpha, alpha * l_i + pv[HEAD_DIM:HEAD_DIM + 1], alpha * acc + pv[:HEAD_DIM]))
        return tuple(out)

    mx_first = scores(0, s_a)

    @pl.when((pl.program_id(0) == 0) & (pl.program_id(1) == 0) & (pl.program_id(2) == 0))
    def _():
        p_b[...] = jnp.zeros_like(p_b)

    def body(t, carry):
        st, mx_a = carry
        mx_b = scores(2 * t + 1, s_b)
        st = values(2 * t - 1, p_b, st, live=jnp.where(t > 0, 1.0, 0.0))
        st = softmax(s_a, mx_a, p_a, st)
        mx_a = scores(2 * t + 2, s_a)
        st = values(2 * t, p_a, st)
        return softmax(s_b, mx_b, p_b, st), mx_a

    init = (jnp.full((1, nq), -jnp.inf, F32), jnp.ones((1, nq), F32), jnp.zeros((1, nq), F32),
            jnp.zeros((HEAD_DIM, nq), F32))
    g0 = first // group
    st, _ = lax.fori_loop(0, g0 // 2, body, ((init, init), mx_first))
    def mask_in_place(grp, s_ref):
        mx = []
        for h in range(2):
            key_pos = (grp * group - first) * blk + lax.broadcasted_iota(I32, (keys, nq), 0)
            s = jnp.where(key_pos <= lax.broadcasted_iota(I32, (keys, nq), 1), s_ref[h], NEG_INF)
            s_ref[h] = s
            mx.append(jnp.max(s, axis=0, keepdims=True))
        return tuple(mx)

    lo_b = 0
    for pair in range(qblocks // (2 * group)):
        ga = g0 + 2 * pair
        lo = 2 * pair * keys
        mx_a = mask_in_place(ga, s_a) if pair == 0 else scores(ga, s_a, causal=True, lo=lo)
        st = values(ga - 1, p_b, st, lo=lo_b, live=jnp.where(ga > 0, 1.0, 0.0))
        lo_b = lo + keys
        mx_b = scores(ga + 1, s_b, causal=True, lo=lo_b)
        st = softmax(s_a, mx_a, p_a, st, lo=lo)
        st = values(ga, p_a, st, lo=lo)
        st = softmax(s_b, mx_b, p_b, st, lo=lo_b)
    st = values(g0 + qblocks // group - 1, p_b, st, lo=lo_b)
    both = jnp.concatenate([st[h][3] / st[h][2] for h in range(2)], axis=0)
    o_ref[...] = both.T.astype(BF16)


def _moba(qt, k, vt4, km, bsz, seq):
    n_tok, attn_w = k.shape
    blk = MOBA_BLOCK
    nb = seq // blk
    group, qblocks = 2, 4
    nbp = -(-nb // 16) * 16
    assert nb % qblocks == 0 and qblocks % (2 * group) == 0 and nbp <= LANES
    hp = attn_w // LANES
    km_pad = jnp.zeros((bsz, nbp, attn_w), F32).at[:, :nb].set(km.reshape(bsz, nb, attn_w))
    k4 = k.reshape(bsz, nb, blk, attn_w)
    steps = nb // qblocks
    return pl.pallas_call(
        functools.partial(_moba_kernel, nb=nb, nbp=nbp, qblocks=qblocks),
        grid=(bsz, hp, steps),
        in_specs=[pl.BlockSpec((None, LANES, qblocks * blk), lambda b, p, i: (b, p, i)),
                  pl.BlockSpec((None, nb, blk, LANES), lambda b, p, i: (b, 0, 0, p)),
                  pl.BlockSpec((None, nb, LANES, blk), lambda b, p, i: (b, 0, p, 0)),
                  pl.BlockSpec((None, nbp, LANES), lambda b, p, i: (b, 0, p))],
        out_specs=pl.BlockSpec((qblocks * blk, LANES), lambda b, p, i: (b * steps + i, p)),
        out_shape=jax.ShapeDtypeStruct((n_tok, attn_w), BF16),
        scratch_shapes=([pltpu.VMEM((2, group * blk, qblocks * blk), F32)] * 2
                        + [pltpu.VMEM((2, group * blk, qblocks * blk), BF16)] * 2),
        compiler_params=_cparams(("arbitrary", "arbitrary", "arbitrary")),
        name="moba",
    )(qt, k4, vt4, km_pad)


def _rms(t, g):
    return t * lax.rsqrt(jnp.mean(t * t, axis=-1, keepdims=True) + NORM_EPS) * g


def _store_slab(ref, val, base=0):
    rows, d = val.shape
    sub = d // LANES
    for s in range(sub):
        ref[pl.ds(base + s, rows, stride=sub), :] = val[:, s * LANES:(s + 1) * LANES]


def _load_slab(ref, rows, sub, base=0):
    return jnp.concatenate([ref[pl.ds(base + s, rows, stride=sub), :] for s in range(sub)], axis=1)


def _merge_kernel(ys_ref, u_ref, ya_ref, gs_ref, ga_ref, x_ref, d_ref, wglu_ref, wsb_ref, wab_ref,
                  wout_ref, postg_ref, gtm_ref, preg_ref, scf_ref, shf_ref, rw_ref, rb_ref,
                  x1_ref, h2_ref, tw_ref, loc_ref, before_ref, cnt_ref, run_ref):
    @pl.when(pl.program_id(0) == 0)
    def _():
        run_ref[...] = jnp.zeros_like(run_ref)

    tm = x_ref.shape[0]
    hm = tm // 2
    halves = [pl.ds(0, hm), pl.ds(hm, hm)]
    sub = x_ref.shape[1] // LANES

    ys = [jax.nn.gelu(ys_ref[r, :] + d_ref[...] * u_ref[r, :].astype(F32)) for r in halves]
    h2s = []
    for r, y in zip(halves, ys):
        y = y * jax.nn.sigmoid(_dot(y.astype(BF16), wglu_ref[...]))
        bs = _dot(y.astype(BF16), wsb_ref[...])
        ba = _dot(ya_ref[r, :], wab_ref[...])
        merged = gs_ref[r, :].astype(F32) * bs + ga_ref[r, :].astype(F32) * ba
        mix = _dot(merged.astype(BF16), wout_ref[...])
        x1 = x_ref[r, :] + gtm_ref[0] * _rms(mix, postg_ref[...])
        x1_ref[r, :] = x1
        h2s.append(_rms(x1, preg_ref[...]) * (1.0 + scf_ref[0]) + shf_ref[0])
    for j, h2 in enumerate(h2s):
        _store_slab(h2_ref, h2, base=j * hm * sub)

    lane = lax.broadcasted_iota(I32, (hm, LANES), 1)
    lane_f = lane.astype(F32)
    r_i = lax.broadcasted_iota(I32, (hm, hm), 0)
    c_i = lax.broadcasted_iota(I32, (hm, hm), 1)
    tri = jnp.where(c_i < r_i, 1.0, 0.0).astype(BF16)
    before_ref[0] = run_ref[...]
    gs = []
    for h2 in h2s:
        h_hi = h2.astype(BF16)
        h_lo = (h2 - h_hi.astype(F32)).astype(BF16)
        gs.append(_dot(jnp.concatenate([h_hi, h_hi, h_lo], axis=1), rw_ref[...]) + rb_ref[...])
    topk = [([], []) for _ in halves]
    for _ in range(TOP_K):
        for j in range(2):
            m = jnp.max(gs[j], axis=1, keepdims=True)
            idx = jnp.min(jnp.where(gs[j] == m, lane_f, float(LANES)), axis=1, keepdims=True)
            pick = lane_f == idx
            topk[j][0].append(m)
            topk[j][1].append(pick)
            gs[j] = jnp.where(pick, -jnp.inf, gs[j])
    onehots = [jnp.where(p[0] | p[1] | p[2] | p[3], 1.0, 0.0) for _, p in topk]
    cnt_tile = sum(jnp.sum(o, axis=0, keepdims=True) for o in onehots)
    cnt_hi = jnp.floor(cnt_tile * (1.0 / 256.0))
    parts = jnp.concatenate([cnt_hi, cnt_tile - 256.0 * cnt_hi, jnp.zeros((6, LANES), F32)], axis=0)
    before_e = jnp.where(lax.broadcasted_iota(I32, (LANES, LANES), 0) < lax.broadcasted_iota(I32, (LANES, LANES), 1),
                         1.0, 0.0).astype(BF16)
    pref = _dot(parts.astype(BF16), before_e)
    off_tile = 256.0 * pref[0:1] + pref[1:2]
    for r, (vals, picks), onehot in zip(halves, topk, onehots):
        exps = [jnp.exp(v - vals[0]) for v in vals]
        tot = exps[0] + exps[1] + exps[2] + exps[3]

        rank_full = _dot(tri, onehot.astype(BF16)) + (run_ref[...] - before_ref[0]) + off_tile
        run_ref[...] = run_ref[...] + jnp.sum(onehot, axis=0, keepdims=True)

        tw = jnp.zeros((hm, LANES), F32)
        rk = jnp.zeros((hm, LANES), F32)
        for k in range(TOP_K):
            tw = jnp.where(lane == k, exps[k] / tot, tw)
            pos = jnp.sum(jnp.where(picks[k], rank_full, 0.0), axis=1, keepdims=True)
            rk = jnp.where(lane == k, pos, rk)
        tw_ref[r, :] = tw
        loc_ref[r, :] = rk.astype(I32)
    cnt_ref[...] = run_ref[...]


def _merge(ys, u, ya, gs, ga, x2, ssm_d, w_glu, w_sb, w_ab, w_out, post_g, gt_m, pre_g, sc_f, sh_f,
           router_w, router_b, seq, tm):
    n_tok, d = x2.shape
    sw = ys.shape[1]
    aw = ya.shape[1]
    ne = router_w.shape[1]
    per_b = seq // tm
    rw = jnp.zeros((d, LANES), F32).at[:, :ne].set(router_w)
    rw_hi = rw.astype(BF16)
    rw_lo = (rw - rw_hi.astype(F32)).astype(BF16)
    rw = jnp.concatenate([rw_hi, rw_lo, rw_hi], axis=0)
    rb = jnp.full((1, LANES), NEG_INF, F32).at[0, :ne].set(router_b)
    tok = lambda w: pl.BlockSpec((tm, w), lambda i: (i, 0))
    full = lambda a, b: pl.BlockSpec((a, b), lambda i: (0, 0))
    bvec = pl.BlockSpec((1, 1, d), lambda i: (i // per_b, 0, 0))
    return pl.pallas_call(
        _merge_kernel,
        grid=(n_tok // tm,),
        in_specs=[tok(sw), tok(sw), tok(aw), tok(d), tok(d), tok(d),
                  full(1, sw), full(sw, sw), full(sw, d), full(aw, d), full(d, d),
                  full(1, d), bvec, full(1, d), bvec, bvec, full(3 * d, LANES), full(1, LANES)],
        out_specs=[tok(d), pl.BlockSpec((tm * (d // LANES), LANES), lambda i: (i, 0)),
                   tok(LANES), tok(LANES),
                   pl.BlockSpec((1, 1, LANES), lambda i: (i, 0, 0)), full(1, LANES)],
        out_shape=[jax.ShapeDtypeStruct((n_tok, d), F32),
                   jax.ShapeDtypeStruct((n_tok * (d // LANES), LANES), F32),
                   jax.ShapeDtypeStruct((n_tok, LANES), F32),
                   jax.ShapeDtypeStruct((n_tok, LANES), I32),
                   jax.ShapeDtypeStruct((n_tok // tm, 1, LANES), F32),
                   jax.ShapeDtypeStruct((1, LANES), F32)],
        scratch_shapes=[pltpu.VMEM((1, LANES), F32)],
        compiler_params=_cparams(("arbitrary",)),
        name="merge",
    )(ys, u, ya, gs, ga, x2, ssm_d.reshape(1, sw), w_glu.astype(BF16), w_sb.astype(BF16),
      w_ab.astype(BF16), w_out.astype(BF16), post_g.reshape(1, d), gt_m, pre_g.reshape(1, d),
      sc_f, sh_f, rw, rb)


def _chunk_copies(cnt_ref, start_ref, tile, ne, max_rows, make):
    def per_expert(e, off):
        c, start = cnt_ref[tile * ne + e], start_ref[tile * ne + e]
        for b in range(max_rows.bit_length()):
            n = 1 << b

            @pl.when((c >> b) & 1 == 1)
            def _():
                make(off + (c & (n - 1)), start + (c & (n - 1)), n)
        return off + c

    lax.fori_loop(0, ne, per_expert, 0)


def _dispatch_kernel(cnt_ref, start_ref, pad_start_ref, pad_cnt_ref, tail_ref, loct_ref, h_ref, xs_ref,
                     buf, zero_ref, sem, zsem, *, tm, sub, bm, n_rows, ne):
    i = pl.program_id(0)
    last = pl.num_programs(0) - 1
    slot = i % 2
    n_loc = tm * TOP_K

    def row(ref, r, n=1):
        return ref.at[pl.ds(pl.multiple_of(r * sub, sub), n * sub)]

    def wait_slot(s):
        pltpu.make_async_copy(buf.at[s], row(xs_ref, 0, n_loc), sem.at[s]).wait()

    def zero_copies(do):
        def per_expert(e, _):
            start, cnt = pad_start_ref[e], pad_cnt_ref[e]
            for b in range(bm.bit_length() - 1):
                n = 1 << b

                @pl.when((cnt >> b) & 1 == 1)
                def _():
                    do(pltpu.make_async_copy(row(zero_ref, 0, n), row(xs_ref, start + (cnt & (n - 1)), n), zsem))
            return 0

        lax.fori_loop(0, ne, per_expert, 0)

        def per_block(j, _):
            r = tail_ref[0] + j * bm

            @pl.when(r < n_rows)
            def _():
                do(pltpu.make_async_copy(row(zero_ref, 0, bm), row(xs_ref, r, bm), zsem))
            return 0

        lax.fori_loop(0, ne, per_block, 0)

    @pl.when(i == 0)
    def _():
        zero_ref[...] = jnp.zeros_like(zero_ref)
        zero_copies(lambda cp: cp.start())

    @pl.when(i >= 2)
    def _():
        wait_slot(slot)

    loct = loct_ref[0]
    row_i = lax.broadcasted_iota(I32, (n_loc, tm), 0)
    place = row_i == loct[0:1, :]
    for k in range(1, TOP_K):
        place = place | (row_i == loct[k:k + 1, :])
    grouped = _dot(jnp.where(place, 1.0, 0.0).astype(BF16), _load_slab(h_ref, tm, sub).astype(BF16))
    _store_slab(buf.at[slot], grouped)
    _chunk_copies(cnt_ref, start_ref, i, ne, tm,
                  lambda loc, glob, n: pltpu.make_async_copy(row(buf.at[slot], loc, n), row(xs_ref, glob, n),
                                                             sem.at[slot]).start())

    @pl.when(i == last)
    def _():
        wait_slot(slot)

    @pl.when((i == last) & (i >= 1))
    def _():
        wait_slot(1 - slot)

    @pl.when(i == 0)
    def _():
        zero_copies(lambda cp: cp.wait())


def _dispatch(h_slab, loc, cnt_tile, start_tile, pad_start, pad_cnt, tail, n_rows, sub, bm, tm):
    n_tok = loc.shape[0]
    nt = n_tok // tm
    ne = pad_start.shape[0]
    loct = jnp.full((nt, 8, tm), -1, I32).at[:, :TOP_K].set(loc.reshape(nt, tm, TOP_K).transpose(0, 2, 1))
    grid_spec = pltpu.PrefetchScalarGridSpec(
        num_scalar_prefetch=5,
        grid=(nt,),
        in_specs=[pl.BlockSpec((1, 8, tm), lambda i, *_: (i, 0, 0)),
                  pl.BlockSpec((tm * sub, LANES), lambda i, *_: (i, 0))],
        out_specs=pl.BlockSpec(memory_space=pl.ANY),
        scratch_shapes=[pltpu.VMEM((2, TOP_K * tm * sub, LANES), F32), pltpu.VMEM((bm * sub, LANES), F32),
                        pltpu.SemaphoreType.DMA((2,)), pltpu.SemaphoreType.DMA(())],
    )
    return pl.pallas_call(
        functools.partial(_dispatch_kernel, tm=tm, sub=sub, bm=bm, n_rows=n_rows, ne=ne),
        grid_spec=grid_spec,
        out_shape=jax.ShapeDtypeStruct((n_rows * sub, LANES), F32),
        compiler_params=_cparams(("arbitrary",)),
        name="dispatch",
    )(cnt_tile.reshape(-1), start_tile.reshape(-1), pad_start, pad_cnt, tail, loct, h_slab)


def _expert_kernel(be_ref, nv_ref, x_ref, wg_ref, bg_ref, wu_ref, bu_ref, wd_ref, bd_ref,
                   o_ref, wg_s, wu_s, wd_s, *, bm, sub):
    i = pl.program_id(0)
    prev = be_ref[jnp.maximum(i - 1, 0)]
    changed = (i == 0) | (be_ref[i] != prev)

    @pl.when(changed)
    def _():
        wg_s[...] = wg_ref[0].astype(BF16)
        wu_s[...] = wu_ref[0].astype(BF16)
        wd_s[...] = wd_ref[0].astype(BF16)

    @pl.when(i < nv_ref[0])
    def _():
        xb = _load_slab(x_ref, bm, sub).astype(BF16)
        g = _dot(xb, wg_s[...]) + bg_ref[0]
        u = _dot(xb, wu_s[...]) + bu_ref[0]
        g = jnp.minimum(g, SWIGLU_LIMIT)
        u = jnp.clip(u, -SWIGLU_LIMIT, SWIGLU_LIMIT)
        act = g * jax.nn.sigmoid(SWIGLU_ALPHA * g) * (u + 1.0)
        _store_slab(o_ref, _dot(act.astype(BF16), wd_s[...]) + bd_ref[0])

    @pl.when(i >= nv_ref[0])
    def _():
        o_ref[...] = jnp.zeros_like(o_ref)


def _experts(xs, blk_exp, n_valid, w_gate, b_gate, w_up, b_up, w_down, b_down, bm):
    ne, d, f = w_gate.shape
    sub = d // LANES
    n_blocks = xs.shape[0] // (bm * sub)
    wspec = lambda a, b: pl.BlockSpec((1, a, b), lambda i, be, nv: (be[i], 0, 0))
    grid_spec = pltpu.PrefetchScalarGridSpec(
        num_scalar_prefetch=2,
        grid=(n_blocks,),
        in_specs=[pl.BlockSpec((bm * sub, LANES), lambda i, be, nv: (jnp.clip(nv[0] - 1, 0, i), 0)),
                  wspec(d, f), wspec(1, f), wspec(d, f), wspec(1, f), wspec(f, d), wspec(1, d)],
        out_specs=pl.BlockSpec((bm * sub, LANES), lambda i, be, nv: (i, 0)),
        scratch_shapes=[pltpu.VMEM((d, f), BF16), pltpu.VMEM((d, f), BF16), pltpu.VMEM((f, d), BF16)],
    )
    return pl.pallas_call(
        functools.partial(_expert_kernel, bm=bm, sub=sub),
        grid_spec=grid_spec,
        out_shape=jax.ShapeDtypeStruct(xs.shape, F32),
        compiler_params=_cparams(("arbitrary",)),
        name="experts",
    )(blk_exp, n_valid, xs, w_gate, b_gate.reshape(ne, 1, f), w_up, b_up.reshape(ne, 1, f),
      w_down, b_down.reshape(ne, 1, d))


def _combine_kernel(cnt_ref, start_ref, yb_ref, loc_ref, tw_ref, x1_ref, g_ref, gt_ref, o_ref, buf, sem,
                    *, tm, sub, ne):
    i = pl.program_id(0)
    n_loc = tm * TOP_K

    def start_tile(tile, slot):
        def make(loc, glob, n):
            src = yb_ref.at[pl.ds(pl.multiple_of(glob * sub, sub), n * sub)]
            dst = buf.at[slot, pl.ds(pl.multiple_of(loc * sub, sub), n * sub)]
            pltpu.make_async_copy(src, dst, sem.at[slot]).start()

        _chunk_copies(cnt_ref, start_ref, tile, ne, tm, make)

    slot = i % 2

    @pl.when(i == 0)
    def _():
        start_tile(0, 0)

    @pl.when(i + 1 < pl.num_programs(0))
    def _():
        start_tile(i + 1, 1 - slot)

    pltpu.make_async_copy(yb_ref.at[pl.ds(0, n_loc * sub)], buf.at[slot], sem.at[slot]).wait()
    rows = _load_slab(buf.at[slot], n_loc, sub).astype(BF16)
    loc, tw = loc_ref[...], tw_ref[...]
    col_i = lax.broadcasted_iota(I32, (tm, n_loc), 1)
    w = jnp.zeros((tm, n_loc), F32)
    for k in range(TOP_K):
        w = jnp.where(col_i == loc[:, k:k + 1], tw[:, k:k + 1], w)
    ff = _dot(w.astype(BF16), rows)
    o_ref[...] = x1_ref[...] + gt_ref[0] * _rms(ff, g_ref[...])


def _combine(yb, loc, tw, cnt_tile, start_tile, x1, post_g, gt_f, seq, tm):
    n_tok, d = x1.shape
    sub = d // LANES
    nt = n_tok // tm
    ne = cnt_tile.shape[1]
    per_b = seq // tm
    assert loc.shape == (n_tok, LANES)
    grid_spec = pltpu.PrefetchScalarGridSpec(
        num_scalar_prefetch=2,
        grid=(nt,),
        in_specs=[pl.BlockSpec(memory_space=pl.ANY),
                  pl.BlockSpec((tm, LANES), lambda i, *_: (i, 0)),
                  pl.BlockSpec((tm, LANES), lambda i, *_: (i, 0)),
                  pl.BlockSpec((tm, d), lambda i, *_: (i, 0)),
                  pl.BlockSpec((1, d), lambda i, *_: (0, 0)),
                  pl.BlockSpec((1, 1, d), lambda i, *_: (i // per_b, 0, 0))],
        out_specs=pl.BlockSpec((tm, d), lambda i, *_: (i, 0)),
        scratch_shapes=[pltpu.VMEM((2, TOP_K * tm * sub, LANES), F32), pltpu.SemaphoreType.DMA((2,))],
    )
    return pl.pallas_call(
        functools.partial(_combine_kernel, tm=tm, sub=sub, ne=ne),
        grid_spec=grid_spec,
        out_shape=jax.ShapeDtypeStruct((n_tok, d), F32),
        compiler_params=_cparams(("arbitrary",)),
        name="combine",
    )(cnt_tile.reshape(-1), start_tile.reshape(-1), yb, loc, tw, x1, post_g.reshape(1, d), gt_f)


def kernel(x, c, positions, ada_w, ada_b, mix_pre_g, mix_post_g, ffn_pre_g, ffn_post_g, w_in, ssm_lam_re, ssm_lam_im, ssm_log_dt, ssm_b_re, ssm_b_im, ssm_c_re, ssm_c_im, ssm_d, ssm_w_glu, w_ssm_branch, w_attn_branch, w_out, router_w, router_b, w_gate, b_gate, w_up, b_up, w_down, b_down):
    bsz, seq, d = x.shape
    depth = ada_w.shape[0]
    n_tok = bsz * seq
    bm = EXPERT_ROWS
    sub = d // LANES
    xcur = x.reshape(n_tok, d)
    pos = positions.reshape(n_tok, 1).astype(I32)
    for l in range(depth):
        ada = _ada(c, ada_w[l], ada_b[l])
        sh_m, sc_m, gt_m, sh_f, sc_f, gt_f = [a.reshape(bsz, 1, d) for a in jnp.split(ada, 6, axis=-1)]

        u, q, k, v, gs, ga, km = _inproj(xcur, sc_m, sh_m, mix_pre_g[l].reshape(1, d), pos, w_in[l],
                                         seq, tm=min(512, seq))
        ys = _s5(u, ssm_lam_re[l], ssm_lam_im[l], ssm_log_dt[l], ssm_b_re[l], ssm_b_im[l],
                 ssm_c_re[l], ssm_c_im[l], bsz, seq)
        ya = _moba(q, k, v, km, bsz, seq)
        tm = ROUTE_TILE
        x1, h2, tw, loc, before, cnt = _merge(
            ys, u, ya, gs, ga, xcur, ssm_d[l], ssm_w_glu[l], w_ssm_branch[l], w_attn_branch[l], w_out[l],
            mix_post_g[l], gt_m, ffn_pre_g[l], sc_f, sh_f, router_w[l], router_b[l], seq, tm=tm)

        ne = router_w.shape[-1]
        nt = n_tok // tm
        counts = cnt[0, :ne].astype(I32)
        padded = (counts + bm - 1) // bm * bm
        p_ends = jnp.cumsum(padded)
        p_starts = p_ends - padded
        before = before.reshape(nt, LANES)[:, :ne].astype(I32)
        cnt_tile = jnp.concatenate([before[1:], counts[None]], axis=0) - before
        start_tile = p_starts[None, :] + before
        n_blocks = (n_tok * TOP_K) // bm + ne
        n_rows = n_blocks * bm
        blk_start = jnp.arange(n_blocks, dtype=I32)[:, None] * bm
        blk_exp = jnp.minimum(jnp.sum((blk_start >= p_ends[None, :]).astype(I32), axis=1), ne - 1)
        n_valid = (p_ends[-1:] // bm).astype(I32)

        xs = _dispatch(h2, loc[:, :TOP_K], cnt_tile, start_tile, p_starts + counts, padded - counts, p_ends[-1:],
                       n_rows, sub, bm, tm)
        yb = _experts(xs, blk_exp, n_valid, w_gate[l], b_gate[l], w_up[l], b_up[l], w_down[l], b_down[l], bm)
        xcur = _combine(yb, loc, tw, cnt_tile, start_tile, x1, ffn_post_g[l], gt_f, seq, tm)
    return xcur.reshape(bsz, seq, d).astype(x.dtype)
```

```python
import functools
import math

import jax
import jax.numpy as jnp
from jax import lax
from jax.experimental import pallas as pl
from jax.experimental.pallas import tpu as pltpu

F32 = jnp.float32
BF16 = jnp.bfloat16
I32 = jnp.int32

N_HEADS = 8
HEAD_DIM = 64
ROPE_THETA = 10000.0
MOBA_BLOCK = 256
MOBA_TOPK = 3
SSM_GROUP_SIZE = 16
SSM_GROUPS = 32
SSM_STATE = 64
TOP_K = 4
SWIGLU_ALPHA = 1.702
SWIGLU_LIMIT = 7.0
NORM_EPS = 1e-6
NEG_INF = -1e30

LANES = 128
SSM_CHUNK = 64
EXPERT_ROWS = 512
ROUTE_TILE = 512
VMEM_LIMIT = 56 * 1024 * 1024
HIGHEST = lax.Precision.HIGHEST
Q_SCALE = HEAD_DIM ** -0.5 * math.log2(math.e)


def _cparams(sem):
    return pltpu.CompilerParams(dimension_semantics=sem, vmem_limit_bytes=VMEM_LIMIT)


def _dot(a, b, **kw):
    return jnp.dot(a, b, preferred_element_type=F32, **kw)


def _dot_t(a, b, **kw):
    return lax.dot_general(a, b, (((1,), (1,)), ((), ())), preferred_element_type=F32, **kw)


def _ada_kernel(c_ref, w_ref, b_ref, o_ref):
    c = c_ref[...]
    cond = c * jax.nn.sigmoid(c)
    o_ref[...] = _dot(cond, w_ref[...], precision=HIGHEST) + b_ref[...]


def _ada(c, ada_w, ada_b):
    bsz, d = c.shape
    n = ada_w.shape[1]
    c8 = jnp.zeros((8, d), F32).at[:bsz].set(c)
    out = pl.pallas_call(
        _ada_kernel,
        grid=(n // d,),
        in_specs=[pl.BlockSpec((8, d), lambda j: (0, 0)),
                  pl.BlockSpec((d, d), lambda j: (0, j)),
                  pl.BlockSpec((1, d), lambda j: (0, j))],
        out_specs=pl.BlockSpec((8, d), lambda j: (0, j)),
        out_shape=jax.ShapeDtypeStruct((8, n), F32),
        compiler_params=_cparams(("arbitrary",)),
        name="ada",
    )(c8, ada_w, ada_b.reshape(1, n))
    return out[:bsz]


def _transpose_kernel(w_ref, o_ref):
    o_ref[...] = w_ref[...].T.astype(BF16)


def _inproj_kernel(x_ref, sc_ref, sh_ref, g_ref, post_ref, invft_ref, w_ref, wt_ref,
                   u_ref, qt_ref, k_ref, vt_ref, gs_ref, ga_ref, km_ref, *, ssm_w, attn_w, d_model):
    x = x_ref[...]
    ms = jnp.mean(x * x, axis=-1, keepdims=True)
    xn = x * lax.rsqrt(ms + NORM_EPS) * g_ref[...]
    h = (xn * (1.0 + sc_ref[0]) + sh_ref[0]).astype(BF16)

    def proj(lo, width):
        return _dot(h, w_ref[:, lo:lo + width])

    tm = x.shape[0]
    ang_t = invft_ref[...] * post_ref[0].astype(F32)
    cos_t, sin_t = jnp.cos(ang_t), jnp.sin(ang_t)
    fold = LANES // (HEAD_DIM // 2)
    cos128 = jnp.concatenate([cos_t] * fold, axis=0).T
    sin128 = jnp.concatenate([sin_t] * fold, axis=0).T
    reps = attn_w // LANES
    cos = jnp.concatenate([cos128] * reps, axis=1)
    sin = jnp.concatenate([sin128] * reps, axis=1)
    lane = lax.broadcasted_iota(I32, (1, attn_w), 1)
    first = (lane % HEAD_DIM) < (HEAD_DIM // 2)
    sin = jnp.where(first, -sin, sin)

    u_ref[...] = proj(0, ssm_w)
    gs_ref[...] = jax.nn.sigmoid(proj(ssm_w + 3 * attn_w, d_model)).astype(BF16)
    ga_ref[...] = jax.nn.sigmoid(proj(ssm_w + 3 * attn_w + d_model, d_model)).astype(BF16)
    vt = _dot_t(wt_ref[attn_w:2 * attn_w, :], h).astype(BF16)
    nblk = tm // MOBA_BLOCK
    for j in range(nblk):
        vt_ref[j] = vt[:, j * MOBA_BLOCK:(j + 1) * MOBA_BLOCK]

    def rope(t):
        rot = jnp.where(first, pltpu.roll(t, attn_w - HEAD_DIM // 2, axis=1),
                        pltpu.roll(t, HEAD_DIM // 2, axis=1))
        return t * cos + rot * sin

    k = rope(proj(ssm_w + attn_w, attn_w))
    k_ref[...] = k.astype(BF16)
    km_ref[0] = jnp.mean(k.reshape(nblk, MOBA_BLOCK, attn_w), axis=1)

    half = HEAD_DIM // 2
    qt = _dot_t(wt_ref[0:attn_w, :], h).reshape(N_HEADS, 2, half, tm)
    t1, t2 = qt[:, 0], qt[:, 1]
    qt = jnp.stack([t1 * cos_t - t2 * sin_t, t2 * cos_t + t1 * sin_t], axis=1).reshape(attn_w, tm)
    qt_ref[...] = (qt * Q_SCALE).astype(BF16)


def _inproj(x2, sc, sh, g, pos, w_in, seq, tm):
    n_tok, d = x2.shape
    attn_w = N_HEADS * HEAD_DIM
    ssm_w = SSM_GROUPS * SSM_GROUP_SIZE
    in_w = w_in.shape[1]
    half = HEAD_DIM // 2
    inv_freq = ROPE_THETA ** (-jnp.arange(half, dtype=F32) / half)
    nt = n_tok // tm
    per_b = seq // tm
    nblk = tm // MOBA_BLOCK
    bsz = n_tok // seq
    blk = MOBA_BLOCK
    assert ssm_w % attn_w == 0
    q_blk, v_blk = ssm_w // attn_w, ssm_w // attn_w + 2
    wt = pl.pallas_call(
        _transpose_kernel,
        grid=(2,),
        in_specs=[pl.BlockSpec((d, attn_w), lambda j: (0, q_blk + j * (v_blk - q_blk)))],
        out_specs=pl.BlockSpec((attn_w, d), lambda j: (j, 0)),
        out_shape=jax.ShapeDtypeStruct((2 * attn_w, d), BF16),
        compiler_params=_cparams(("arbitrary",)),
        name="wt",
    )(w_in)
    tok = lambda w: pl.BlockSpec((tm, w), lambda i: (i, 0))
    full = lambda a, b: pl.BlockSpec((a, b), lambda i: (0, 0))
    bvec = pl.BlockSpec((1, 1, d), lambda i: (i // per_b, 0, 0))
    outs = pl.pallas_call(
        functools.partial(_inproj_kernel, ssm_w=ssm_w, attn_w=attn_w, d_model=d),
        grid=(nt,),
        in_specs=[tok(d), bvec, bvec, full(1, d),
                  pl.BlockSpec((1, 1, tm), lambda i: (i, 0, 0)), full(half, 1),
                  full(d, in_w), full(2 * attn_w, d)],
        out_specs=[tok(ssm_w),
                   pl.BlockSpec((None, attn_w, tm), lambda i: (i // per_b, 0, i % per_b)),
                   tok(attn_w),
                   pl.BlockSpec((None, nblk, attn_w, blk), lambda i: (i // per_b, i % per_b, 0, 0)),
                   tok(d), tok(d),
                   pl.BlockSpec((1, nblk, attn_w), lambda i: (i, 0, 0))],
        out_shape=[jax.ShapeDtypeStruct((n_tok, ssm_w), F32),
                   jax.ShapeDtypeStruct((bsz, attn_w, seq), BF16),
                   jax.ShapeDtypeStruct((n_tok, attn_w), BF16),
                   jax.ShapeDtypeStruct((bsz, seq // blk, attn_w, blk), BF16),
                   jax.ShapeDtypeStruct((n_tok, d), BF16),
                   jax.ShapeDtypeStruct((n_tok, d), BF16),
                   jax.ShapeDtypeStruct((nt, nblk, attn_w), F32)],
        compiler_params=_cparams(("arbitrary",)),
        name="inproj",
    )(x2, sc, sh, g, pos.reshape(nt, 1, tm), inv_freq.reshape(half, 1), w_in.astype(BF16), wt)
    return outs


def _s5_group(x, zr, zi, lr, li, ca, cb, ba, bb, t_ref, *, chunk, n_chunks):
    gs = SSM_GROUP_SIZE

    def powers(tau):
        mag = jnp.exp(tau * zr)
        return mag * jnp.cos(tau * zi), mag * jnp.sin(tau * zi)

    tau = lax.broadcasted_iota(I32, (chunk + 8, LANES), 0).astype(F32)
    e_re, e_im = powers(tau)
    lb_re, lb_im = e_re[1:2], e_im[1:2]
    den = lr * lr + li * li
    a, b = lb_re - 1.0, lb_im
    cf_re = (a * lr + b * li) / den
    cf_im = (b * lr - a * li) / den
    bri = cf_re * ba + cf_im * bb
    bri_sw = cf_re * bb - cf_im * ba

    cpow =(e_re[:chunk + 1, None, :] * ca[None] + e_im[:chunk + 1, None, :] * cb[None])
    cpow = cpow.reshape((chunk + 1) * gs, LANES)
    width = chunk * gs
    r = _dot_t(bri, cpow[:width], precision=HIGHEST)
    col = lax.broadcasted_iota(I32, (gs, width), 1)
    t_ref[0:gs, :] = r.astype(BF16)
    for j in range(1, chunk):
        shifted = jnp.where(col >= gs * j, pltpu.roll(r, gs * j, axis=1), 0.0)
        t_ref[gs * j:gs * (j + 1), :] = shifted.astype(BF16)

    y = _dot(x, t_ref[...])

    tau_rev = (chunk - 1) - lax.broadcasted_iota(I32, (chunk, LANES), 0)
    r_re, r_im = powers(tau_rev.astype(F32))
    bst = r_re[:, None, :] * bri[None] + r_im[:, None, :] * bri_sw[None]
    bst = bst.reshape(width, LANES).astype(BF16)
    s = _dot(x, bst)

    rows = s.shape[0]
    n_idx = lax.broadcasted_iota(I32, (rows, LANES), 0) % n_chunks
    lane = lax.broadcasted_iota(I32, (1, LANES), 1)
    half = LANES // 2

    def cmul(v, p_re, p_im):
        return v * p_re + pltpu.roll(v, half, axis=1) * jnp.where(lane < half, -p_im, p_im)

    sh = 1
    while sh < n_chunks:
        p_re, p_im = powers(jnp.full((1, LANES), float(chunk * sh), F32))
        prev = jnp.where(n_idx >= sh, pltpu.roll(s, sh, axis=0), 0.0)
        s = s + cmul(prev, p_re, p_im)
        sh *= 2
    s_in = jnp.where(n_idx >= 1, pltpu.roll(s, 1, axis=0), 0.0)
    return y + _dot_t(s_in.astype(BF16), cpow[gs:gs + width].astype(BF16))


def _s5_kernel(u_ref, sel_ref, zr_ref, zi_ref, lr_ref, li_ref, ca_ref, cb_ref, ba_ref, bb_ref,
               y_ref, x_scr, y_scr, t_ref, *, chunk, n_chunks, rows):
    gpt = LANES // SSM_GROUP_SIZE

    def plane(j):
        return pl.ds(j, rows, stride=chunk)

    for jt in range(chunk // gpt):
        planes = jnp.concatenate([u_ref[plane(jt * gpt + jj), :] for jj in range(gpt)], axis=1)
        grouped = _dot(planes.astype(BF16), sel_ref[...])
        for g in range(gpt):
            x_scr[g, :, jt * LANES:(jt + 1) * LANES] = grouped[:, g * LANES:(g + 1) * LANES].astype(BF16)

    def per_group(g, _):
        y = _s5_group(x_scr[g], zr_ref[g], zi_ref[g], lr_ref[g], li_ref[g], ca_ref[g], cb_ref[g], ba_ref[g],
                      bb_ref[g], t_ref, chunk=chunk, n_chunks=n_chunks)
        y_scr[g] = y.astype(BF16)
        return 0

    lax.fori_loop(0, gpt, per_group, 0)

    for jt in range(chunk // gpt):
        grouped = jnp.concatenate([y_scr[g, :, jt * LANES:(jt + 1) * LANES] for g in range(gpt)], axis=1)
        planes = _dot(grouped, sel_ref[...])
        for jj in range(gpt):
            y_ref[plane(jt * gpt + jj), :] = planes[:, jj * LANES:(jj + 1) * LANES]


def _s5(u, lam_re, lam_im, log_dt, b_re, b_im, c_re, c_im, bsz, seq):
    g_n, gs, p = SSM_GROUPS, SSM_GROUP_SIZE, SSM_STATE
    chunk = SSM_CHUNK
    n_chunks = seq // chunk
    rows = bsz * n_chunks
    width = chunk * gs
    gpt = LANES // gs
    assert chunk % gpt == 0 and g_n % gpt == 0 and 2 * p == LANES
    dt = jnp.exp(log_dt.astype(F32))[:, None]
    dup = lambda a: jnp.concatenate([a, a], axis=-1).reshape(g_n, 1, 2 * p)
    zr, zi = dup(lam_re * dt), dup(lam_im * dt)
    lr, li = dup(lam_re), dup(lam_im)
    bt_re, bt_im = b_re.transpose(0, 2, 1), b_im.transpose(0, 2, 1)
    ba = jnp.concatenate([bt_re, bt_im], axis=-1)
    bb = jnp.concatenate([-bt_im, bt_re], axis=-1)
    ca = jnp.concatenate([c_re, -c_im], axis=-1)
    cb = jnp.concatenate([-c_im, -c_re], axis=-1)
    idx = jnp.arange(gpt * LANES, dtype=I32)
    swapped = ((idx // gs) % gpt) * LANES + (idx // LANES) * gs + idx % gs
    sel = (swapped[:, None] == idx[None, :]).astype(BF16)
    tile = pl.BlockSpec((bsz * seq, LANES), lambda t: (0, t))
    vec = pl.BlockSpec((gpt, 1, 2 * p), lambda t: (t, 0, 0))
    mat = pl.BlockSpec((gpt, gs, 2 * p), lambda t: (t, 0, 0))
    return pl.pallas_call(
        functools.partial(_s5_kernel, chunk=chunk, n_chunks=n_chunks, rows=rows),
        grid=(g_n // gpt,),
        in_specs=[tile, pl.BlockSpec((gpt * LANES, gpt * LANES), lambda t: (0, 0)),
                  vec, vec, vec, vec, mat, mat, mat, mat],
        out_specs=tile,
        out_shape=jax.ShapeDtypeStruct((bsz * seq, g_n * gs), F32),
        scratch_shapes=[pltpu.VMEM((gpt, rows, width), BF16), pltpu.VMEM((gpt, rows, width), BF16),
                        pltpu.VMEM((width, width), BF16)],
        compiler_params=_cparams(("arbitrary",)),
        name="s5",
    )(u, sel, zr, zi, lr, li, ca, cb, ba, bb)


def _moba_kernel(qt_ref, k_ref, vt_ref, km_ref, o_ref, s_a, s_b, p_a, p_b, *, nb, nbp, qblocks):
    blk = MOBA_BLOCK
    group = 2
    keys = group * blk
    nq = qblocks * blk
    first = qblocks * pl.program_id(2)
    qt = qt_ref[...]
    km = km_ref[...]
    km_hi = km.astype(BF16)
    km_lo = (km - km_hi.astype(F32)).astype(BF16)
    dim_i = lax.broadcasted_iota(I32, (LANES, nq), 0)
    blk_i = lax.broadcasted_iota(I32, (nbp, nq), 0)
    blk_f = blk_i.astype(F32)
    own = first + lax.broadcasted_iota(I32, (nbp, nq), 1) // blk
    valid = blk_i < own

    q_aug = []
    for h in range(2):
        qh = jnp.where((dim_i >= h * HEAD_DIM) & (dim_i < (h + 1) * HEAD_DIM), qt, jnp.zeros_like(qt))
        g = jnp.where(valid, _dot(km_hi, qh) + _dot(km_lo, qh), NEG_INF)
        sel = jnp.zeros((nbp, nq), jnp.bool_)
        for _ in range(MOBA_TOPK):
            m = jnp.max(g, axis=0, keepdims=True)
            idx = jnp.min(jnp.where(g == m, blk_f, float(nbp)), axis=0, keepdims=True)
            pick = blk_f == idx
            sel = sel | pick
            g = jnp.where(pick, -jnp.inf, g)
        ok = (sel & valid) | (blk_i == own)
        parts = [qh, jnp.where(ok, 0.0, NEG_INF).astype(BF16)]
        if nbp < LANES:
            parts.append(jnp.full((LANES - nbp, nq), NEG_INF, BF16))
        q_aug.append(jnp.concatenate(parts, axis=0))

    def scores(grp, s_ref, causal=False, lo=0):
        kb0 = grp * group
        k_cat = k_ref[pl.ds(jnp.minimum(kb0, nb - group), group)].reshape(keys, LANES)
        blk_id = kb0 + lax.broadcasted_iota(I32, (keys, LANES), 0) // blk
        onehot = jnp.where(lax.broadcasted_iota(I32, (keys, LANES), 1) == blk_id, 1.0, 0.0).astype(BF16)
        k_aug = jnp.concatenate([k_cat, onehot], axis=1)
        mx = []
        for h in range(2):
            s = _dot(k_aug, q_aug[h][:, lo:])
            if causal:
                key_pos = (kb0 - first) * blk + lax.broadcasted_iota(I32, (keys, nq - lo), 0)
                s = jnp.where(key_pos <= lo + lax.broadcasted_iota(I32, (keys, nq - lo), 1), s, NEG_INF)
            s_ref[h, :, lo:] = s
            mx.append(jnp.max(s, axis=0, keepdims=True))
        return tuple(mx)

    def softmax(s_ref, mx, p_ref, st, lo=0):
        out = []
        for h in range(2):
            m_i, _, l_i, acc = st[h]
            m_new = jnp.maximum(m_i[:, lo:], mx[h])
            p_ref[h, :, lo:] = jnp.exp2(s_ref[h, :, lo:] - m_new).astype(BF16)
            alpha = jnp.exp2(m_i[:, lo:] - m_new)
            if lo:
                m_new = jnp.concatenate([m_i[:, :lo], m_new], axis=1)
                alpha = jnp.concatenate([jnp.ones((1, lo), F32), alpha], axis=1)
            out.append((m_new, alpha, l_i, acc))
        return tuple(out)

    def values(grp, p_ref, st, lo=0, live=None):
        out = []
        for h in range(2):
            m_i, alpha, l_i, acc = st[h]
            blocks = [vt_ref[jnp.clip(grp * group + j, 0, nb - 1), h * HEAD_DIM:(h + 1) * HEAD_DIM, :]
                      for j in range(group)]
            ones = jnp.ones((16, keys), BF16)
            pv = _dot(jnp.concatenate([jnp.concatenate(blocks, axis=1), ones], axis=0), p_ref[h, :, lo:])
            if live is not None:
                pv = pv * live
            if lo:
                pv = jnp.concatenate([jnp.zeros((pv.shape[0], lo), F32), pv], axis=1)
            out.append((m_i, alpha, alpha * l_i + pv[HEAD_DIM:HEAD_DIM + 1], alpha * acc + pv[:HEAD_DIM]))
        return tuple(out)

    mx_first = scores(0, s_a)

    @pl.when((pl.program_id(0) == 0) & (pl.program_id(1) == 0) & (pl.program_id(2) == 0))
    def _():
        p_b[...] = jnp.zeros_like(p_b)

    def body(t, carry):
        st, mx_a = carry
        mx_b = scores(2 * t + 1, s_b)
        st = values(2 * t - 1, p_b, st, live=jnp.where(t > 0, 1.0, 0.0))
        st = softmax(s_a, mx_a, p_a, st)
        mx_a = scores(2 * t + 2, s_a)
        st = values(2 * t, p_a, st)
        return softmax(s_b, mx_b, p_b, st), mx_a

    init = (jnp.full((1, nq), -jnp.inf, F32), jnp.ones((1, nq), F32), jnp.zeros((1, nq), F32),
            jnp.zeros((HEAD_DIM, nq), F32))
    g0 = first // group
    st, _ = lax.fori_loop(0, g0 // 2, body, ((init, init), mx_first))
    def mask_in_place(grp, s_ref):
        mx = []
        for h in range(2):
            key_pos = (grp * group - first) * blk + lax.broadcasted_iota(I32, (keys, nq), 0)
            s = jnp.where(key_pos <= lax.broadcasted_iota(I32, (keys, nq), 1), s_ref[h], NEG_INF)
            s_ref[h] = s
            mx.append(jnp.max(s, axis=0, keepdims=True))
        return tuple(mx)

    lo_b = 0
    for pair in range(qblocks // (2 * group)):
        ga = g0 + 2 * pair
        lo = 2 * pair * keys
        mx_a = mask_in_place(ga, s_a) if pair == 0 else scores(ga, s_a, causal=True, lo=lo)
        st = values(ga - 1, p_b, st, lo=lo_b, live=jnp.where(ga > 0, 1.0, 0.0))
        lo_b = lo + keys
        mx_b = scores(ga + 1, s_b, causal=True, lo=lo_b)
        st = softmax(s_a, mx_a, p_a, st, lo=lo)
        st = values(ga, p_a, st, lo=lo)
        st = softmax(s_b, mx_b, p_b, st, lo=lo_b)
    st = values(g0 + qblocks // group - 1, p_b, st, lo=lo_b)
    both = jnp.concatenate([st[h][3] / st[h][2] for h in range(2)], axis=0)
    o_ref[...] = both.T.astype(BF16)


def _moba(qt, k, vt4, km, bsz, seq):
    n_tok, attn_w = k.shape
    blk = MOBA_BLOCK
    nb = seq // blk
    group, qblocks = 2, 4
    nbp = -(-nb // 16) * 16
    assert nb % qblocks == 0 and qblocks % (2 * group) == 0 and nbp <= LANES
    hp = attn_w // LANES
    km_pad = jnp.zeros((bsz, nbp, attn_w), F32).at[:, :nb].set(km.reshape(bsz, nb, attn_w))
    k4 = k.reshape(bsz, nb, blk, attn_w)
    steps = nb // qblocks
    return pl.pallas_call(
        functools.partial(_moba_kernel, nb=nb, nbp=nbp, qblocks=qblocks),
        grid=(bsz, hp, steps),
        in_specs=[pl.BlockSpec((None, LANES, qblocks * blk), lambda b, p, i: (b, p, i)),
                  pl.BlockSpec((None, nb, blk, LANES), lambda b, p, i: (b, 0, 0, p)),
                  pl.BlockSpec((None, nb, LANES, blk), lambda b, p, i: (b, 0, p, 0)),
                  pl.BlockSpec((None, nbp, LANES), lambda b, p, i: (b, 0, p))],
        out_specs=pl.BlockSpec((qblocks * blk, LANES), lambda b, p, i: (b * steps + i, p)),
        out_shape=jax.ShapeDtypeStruct((n_tok, attn_w), BF16),
        scratch_shapes=([pltpu.VMEM((2, group * blk, qblocks * blk), F32)] * 2
                        + [pltpu.VMEM((2, group * blk, qblocks * blk), BF16)] * 2),
        compiler_params=_cparams(("arbitrary", "arbitrary", "arbitrary")),
        name="moba",
    )(qt, k4, vt4, km_pad)


def _rms(t, g):
    return t * lax.rsqrt(jnp.mean(t * t, axis=-1, keepdims=True) + NORM_EPS) * g


def _store_slab(ref, val, base=0):
    rows, d = val.shape
    sub = d // LANES
    for s in range(sub):
        ref[pl.ds(base + s, rows, stride=sub), :] = val[:, s * LANES:(s + 1) * LANES]


def _load_slab(ref, rows, sub, base=0):
    return jnp.concatenate([ref[pl.ds(base + s, rows, stride=sub), :] for s in range(sub)], axis=1)


def _merge_kernel(ys_ref, u_ref, ya_ref, gs_ref, ga_ref, x_ref, d_ref, wglu_ref, wsb_ref, wab_ref,
                  wout_ref, postg_ref, gtm_ref, preg_ref, scf_ref, shf_ref, rw_ref, rb_ref,
                  x1_ref, h2_ref, tw_ref, loc_ref, before_ref, cnt_ref, run_ref):
    @pl.when(pl.program_id(0) == 0)
    def _():
        run_ref[...] = jnp.zeros_like(run_ref)

    tm = x_ref.shape[0]
    hm = tm // 2
    halves = [pl.ds(0, hm), pl.ds(hm, hm)]
    sub = x_ref.shape[1] // LANES

    ys = [jax.nn.gelu(ys_ref[r, :] + d_ref[...] * u_ref[r, :].astype(F32)) for r in halves]
    h2s = []
    for r, y in zip(halves, ys):
        y = y * jax.nn.sigmoid(_dot(y.astype(BF16), wglu_ref[...]))
        bs = _dot(y.astype(BF16), wsb_ref[...])
        ba = _dot(ya_ref[r, :], wab_ref[...])
        merged = gs_ref[r, :].astype(F32) * bs + ga_ref[r, :].astype(F32) * ba
        mix = _dot(merged.astype(BF16), wout_ref[...])
        x1 = x_ref[r, :] + gtm_ref[0] * _rms(mix, postg_ref[...])
        x1_ref[r, :] = x1
        h2s.append(_rms(x1, preg_ref[...]) * (1.0 + scf_ref[0]) + shf_ref[0])
    for j, h2 in enumerate(h2s):
        _store_slab(h2_ref, h2, base=j * hm * sub)

    lane = lax.broadcasted_iota(I32, (hm, LANES), 1)
    lane_f = lane.astype(F32)
    r_i = lax.broadcasted_iota(I32, (hm, hm), 0)
    c_i = lax.broadcasted_iota(I32, (hm, hm), 1)
    tri = jnp.where(c_i < r_i, 1.0, 0.0).astype(BF16)
    before_ref[0] = run_ref[...]
    gs = []
    for h2 in h2s:
        h_hi = h2.astype(BF16)
        h_lo = (h2 - h_hi.astype(F32)).astype(BF16)
        gs.append(_dot(jnp.concatenate([h_hi, h_hi, h_lo], axis=1), rw_ref[...]) + rb_ref[...])
    topk = [([], []) for _ in halves]
    for _ in range(TOP_K):
        for j in range(2):
            m = jnp.max(gs[j], axis=1, keepdims=True)
            idx = jnp.min(jnp.where(gs[j] == m, lane_f, float(LANES)), axis=1, keepdims=True)
            pick = lane_f == idx
            topk[j][0].append(m)
            topk[j][1].append(pick)
            gs[j] = jnp.where(pick, -jnp.inf, gs[j])
    onehots = [jnp.where(p[0] | p[1] | p[2] | p[3], 1.0, 0.0) for _, p in topk]
    cnt_tile = sum(jnp.sum(o, axis=0, keepdims=True) for o in onehots)
    cnt_hi = jnp.floor(cnt_tile * (1.0 / 256.0))
    parts = jnp.concatenate([cnt_hi, cnt_tile - 256.0 * cnt_hi, jnp.zeros((6, LANES), F32)], axis=0)
    before_e = jnp.where(lax.broadcasted_iota(I32, (LANES, LANES), 0) < lax.broadcasted_iota(I32, (LANES, LANES), 1),
                         1.0, 0.0).astype(BF16)
    pref = _dot(parts.astype(BF16), before_e)
    off_tile = 256.0 * pref[0:1] + pref[1:2]
    for r, (vals, picks), onehot in zip(halves, topk, onehots):
        exps = [jnp.exp(v - vals[0]) for v in vals]
        tot = exps[0] + exps[1] + exps[2] + exps[3]

        rank_full = _dot(tri, onehot.astype(BF16)) + (run_ref[...] - before_ref[0]) + off_tile
        run_ref[...] = run_ref[...] + jnp.sum(onehot, axis=0, keepdims=True)

        tw = jnp.zeros((hm, LANES), F32)
        rk = jnp.zeros((hm, LANES), F32)
        for k in range(TOP_K):
            tw = jnp.where(lane == k, exps[k] / tot, tw)
            pos = jnp.sum(jnp.where(picks[k], rank_full, 0.0), axis=1, keepdims=True)
            rk = jnp.where(lane == k, pos, rk)
        tw_ref[r, :] = tw
        loc_ref[r, :] = rk.astype(I32)
    cnt_ref[...] = run_ref[...]


def _merge(ys, u, ya, gs, ga, x2, ssm_d, w_glu, w_sb, w_ab, w_out, post_g, gt_m, pre_g, sc_f, sh_f,
           router_w, router_b, seq, tm):
    n_tok, d = x2.shape
    sw = ys.shape[1]
    aw = ya.shape[1]
    ne = router_w.shape[1]
    per_b = seq // tm
    rw = jnp.zeros((d, LANES), F32).at[:, :ne].set(router_w)
    rw_hi = rw.astype(BF16)
    rw_lo = (rw - rw_hi.astype(F32)).astype(BF16)
    rw = jnp.concatenate([rw_hi, rw_lo, rw_hi], axis=0)
    rb = jnp.full((1, LANES), NEG_INF, F32).at[0, :ne].set(router_b)
    tok = lambda w: pl.BlockSpec((tm, w), lambda i: (i, 0))
    full = lambda a, b: pl.BlockSpec((a, b), lambda i: (0, 0))
    bvec = pl.BlockSpec((1, 1, d), lambda i: (i // per_b, 0, 0))
    return pl.pallas_call(
        _merge_kernel,
        grid=(n_tok // tm,),
        in_specs=[tok(sw), tok(sw), tok(aw), tok(d), tok(d), tok(d),
                  full(1, sw), full(sw, sw), full(sw, d), full(aw, d), full(d, d),
                  full(1, d), bvec, full(1, d), bvec, bvec, full(3 * d, LANES), full(1, LANES)],
        out_specs=[tok(d), pl.BlockSpec((tm * (d // LANES), LANES), lambda i: (i, 0)),
                   tok(LANES), tok(LANES),
                   pl.BlockSpec((1, 1, LANES), lambda i: (i, 0, 0)), full(1, LANES)],
        out_shape=[jax.ShapeDtypeStruct((n_tok, d), F32),
                   jax.ShapeDtypeStruct((n_tok * (d // LANES), LANES), F32),
                   jax.ShapeDtypeStruct((n_tok, LANES), F32),
                   jax.ShapeDtypeStruct((n_tok, LANES), I32),
                   jax.ShapeDtypeStruct((n_tok // tm, 1, LANES), F32),
                   jax.ShapeDtypeStruct((1, LANES), F32)],
        scratch_shapes=[pltpu.VMEM((1, LANES), F32)],
        compiler_params=_cparams(("arbitrary",)),
        name="merge",
    )(ys, u, ya, gs, ga, x2, ssm_d.reshape(1, sw), w_glu.astype(BF16), w_sb.astype(BF16),
      w_ab.astype(BF16), w_out.astype(BF16), post_g.reshape(1, d), gt_m, pre_g.reshape(1, d),
      sc_f, sh_f, rw, rb)


def _chunk_copies(cnt_ref, start_ref, tile, ne, max_rows, make):
    def per_expert(e, off):
        c, start = cnt_ref[tile * ne + e], start_ref[tile * ne + e]
        for b in range(max_rows.bit_length()):
            n = 1 << b

            @pl.when((c >> b) & 1 == 1)
            def _():
                make(off + (c & (n - 1)), start + (c & (n - 1)), n, b % 2)
        return off + c

    lax.fori_loop(0, ne, per_expert, 0)


def _dispatch_kernel(cnt_ref, start_ref, pad_start_ref, pad_cnt_ref, tail_ref, loct_ref, h_ref, xs_ref,
                     buf, zero_ref, sem, zsem, *, tm, sub, bm, n_rows, ne):
    i = pl.program_id(0)
    last = pl.num_programs(0) - 1
    slot = i % 2
    n_loc = tm * TOP_K

    def row(ref, r, n=1):
        return ref.at[pl.ds(pl.multiple_of(r * sub, sub), n * sub)]

    def wait_slot(s):
        pltpu.make_async_copy(buf.at[s], row(xs_ref, 0, n_loc), sem.at[s]).wait()

    def zero_copies(do):
        def per_expert(e, _):
            start, cnt = pad_start_ref[e], pad_cnt_ref[e]
            for b in range(bm.bit_length() - 1):
                n = 1 << b

                @pl.when((cnt >> b) & 1 == 1)
                def _():
                    do(pltpu.make_async_copy(row(zero_ref, 0, n), row(xs_ref, start + (cnt & (n - 1)), n), zsem))
            return 0

        lax.fori_loop(0, ne, per_expert, 0)

        def per_block(j, _):
            r = tail_ref[0] + j * bm

            @pl.when(r < n_rows)
            def _():
                do(pltpu.make_async_copy(row(zero_ref, 0, bm), row(xs_ref, r, bm), zsem))
            return 0

        lax.fori_loop(0, ne, per_block, 0)

    @pl.when(i == 0)
    def _():
        zero_ref[...] = jnp.zeros_like(zero_ref)
        zero_copies(lambda cp: cp.start())

    @pl.when(i >= 2)
    def _():
        wait_slot(slot)

    loct = loct_ref[0]
    row_i = lax.broadcasted_iota(I32, (n_loc, tm), 0)
    place = row_i == loct[0:1, :]
    for k in range(1, TOP_K):
        place = place | (row_i == loct[k:k + 1, :])
    grouped = _dot(jnp.where(place, 1.0, 0.0).astype(BF16), _load_slab(h_ref, tm, sub).astype(BF16))
    _store_slab(buf.at[slot], grouped)
    _chunk_copies(cnt_ref, start_ref, i, ne, tm,
                  lambda loc, glob, n, queue: pltpu.make_async_copy(
                      row(buf.at[slot], loc, n), row(xs_ref, glob, n), sem.at[slot]).start(priority=queue))

    @pl.when(i == last)
    def _():
        wait_slot(slot)

    @pl.when((i == last) & (i >= 1))
    def _():
        wait_slot(1 - slot)

    @pl.when(i == 0)
    def _():
        zero_copies(lambda cp: cp.wait())


def _dispatch(h_slab, loc, cnt_tile, start_tile, pad_start, pad_cnt, tail, n_rows, sub, bm, tm):
    n_tok = loc.shape[0]
    nt = n_tok // tm
    ne = pad_start.shape[0]
    loct = jnp.full((nt, 8, tm), -1, I32).at[:, :TOP_K].set(loc.reshape(nt, tm, TOP_K).transpose(0, 2, 1))
    grid_spec = pltpu.PrefetchScalarGridSpec(
        num_scalar_prefetch=5,
        grid=(nt,),
        in_specs=[pl.BlockSpec((1, 8, tm), lambda i, *_: (i, 0, 0)),
                  pl.BlockSpec((tm * sub, LANES), lambda i, *_: (i, 0))],
        out_specs=pl.BlockSpec(memory_space=pl.ANY),
        scratch_shapes=[pltpu.VMEM((2, TOP_K * tm * sub, LANES), F32), pltpu.VMEM((bm * sub, LANES), F32),
                        pltpu.SemaphoreType.DMA((2,)), pltpu.SemaphoreType.DMA(())],
    )
    return pl.pallas_call(
        functools.partial(_dispatch_kernel, tm=tm, sub=sub, bm=bm, n_rows=n_rows, ne=ne),
        grid_spec=grid_spec,
        out_shape=jax.ShapeDtypeStruct((n_rows * sub, LANES), F32),
        compiler_params=_cparams(("arbitrary",)),
        name="dispatch",
    )(cnt_tile.reshape(-1), start_tile.reshape(-1), pad_start, pad_cnt, tail, loct, h_slab)


def _expert_kernel(be_ref, nv_ref, x_ref, wg_ref, bg_ref, wu_ref, bu_ref, wd_ref, bd_ref,
                   o_ref, wg_s, wu_s, wd_s, *, bm, sub):
    i = pl.program_id(0)
    prev = be_ref[jnp.maximum(i - 1, 0)]
    changed = (i == 0) | (be_ref[i] != prev)

    @pl.when(changed)
    def _():
        wg_s[...] = wg_ref[0].astype(BF16)
        wu_s[...] = wu_ref[0].astype(BF16)
        wd_s[...] = wd_ref[0].astype(BF16)

    @pl.when(i < nv_ref[0])
    def _():
        xb = _load_slab(x_ref, bm, sub).astype(BF16)
        g = _dot(xb, wg_s[...]) + bg_ref[0]
        u = _dot(xb, wu_s[...]) + bu_ref[0]
        g = jnp.minimum(g, SWIGLU_LIMIT)
        u = jnp.clip(u, -SWIGLU_LIMIT, SWIGLU_LIMIT)
        act = g * jax.nn.sigmoid(SWIGLU_ALPHA * g) * (u + 1.0)
        _store_slab(o_ref, _dot(act.astype(BF16), wd_s[...]) + bd_ref[0])

    @pl.when(i >= nv_ref[0])
    def _():
        o_ref[...] = jnp.zeros_like(o_ref)


def _experts(xs, blk_exp, n_valid, w_gate, b_gate, w_up, b_up, w_down, b_down, bm):
    ne, d, f = w_gate.shape
    sub = d // LANES
    n_blocks = xs.shape[0] // (bm * sub)
    wspec = lambda a, b: pl.BlockSpec((1, a, b), lambda i, be, nv: (be[i], 0, 0))
    grid_spec = pltpu.PrefetchScalarGridSpec(
        num_scalar_prefetch=2,
        grid=(n_blocks,),
        in_specs=[pl.BlockSpec((bm * sub, LANES), lambda i, be, nv: (jnp.clip(nv[0] - 1, 0, i), 0)),
                  wspec(d, f), wspec(1, f), wspec(d, f), wspec(1, f), wspec(f, d), wspec(1, d)],
        out_specs=pl.BlockSpec((bm * sub, LANES), lambda i, be, nv: (i, 0)),
        scratch_shapes=[pltpu.VMEM((d, f), BF16), pltpu.VMEM((d, f), BF16), pltpu.VMEM((f, d), BF16)],
    )
    return pl.pallas_call(
        functools.partial(_expert_kernel, bm=bm, sub=sub),
        grid_spec=grid_spec,
        out_shape=jax.ShapeDtypeStruct(xs.shape, F32),
        compiler_params=_cparams(("arbitrary",)),
        name="experts",
    )(blk_exp, n_valid, xs, w_gate, b_gate.reshape(ne, 1, f), w_up, b_up.reshape(ne, 1, f),
      w_down, b_down.reshape(ne, 1, d))


def _combine_kernel(cnt_ref, start_ref, yb_ref, loc_ref, tw_ref, x1_ref, g_ref, gt_ref, o_ref, buf, sem,
                    *, tm, sub, ne):
    i = pl.program_id(0)
    n_loc = tm * TOP_K

    def start_tile(tile, slot):
        def make(loc, glob, n, queue):
            src = yb_ref.at[pl.ds(pl.multiple_of(glob * sub, sub), n * sub)]
            dst = buf.at[slot, pl.ds(pl.multiple_of(loc * sub, sub), n * sub)]
            pltpu.make_async_copy(src, dst, sem.at[slot]).start(priority=queue)

        _chunk_copies(cnt_ref, start_ref, tile, ne, tm, make)

    slot = i % 2

    @pl.when(i == 0)
    def _():
        start_tile(0, 0)

    @pl.when(i + 1 < pl.num_programs(0))
    def _():
        start_tile(i + 1, 1 - slot)

    pltpu.make_async_copy(yb_ref.at[pl.ds(0, n_loc * sub)], buf.at[slot], sem.at[slot]).wait()
    rows = _load_slab(buf.at[slot], n_loc, sub).astype(BF16)
    loc, tw = loc_ref[...], tw_ref[...]
    col_i = lax.broadcasted_iota(I32, (tm, n_loc), 1)
    w = jnp.zeros((tm, n_loc), F32)
    for k in range(TOP_K):
        w = jnp.where(col_i == loc[:, k:k + 1], tw[:, k:k + 1], w)
    ff = _dot(w.astype(BF16), rows)
    o_ref[...] = x1_ref[...] + gt_ref[0] * _rms(ff, g_ref[...])


def _combine(yb, loc, tw, cnt_tile, start_tile, x1, post_g, gt_f, seq, tm):
    n_tok, d = x1.shape
    sub = d // LANES
    nt = n_tok // tm
    ne = cnt_tile.shape[1]
    per_b = seq // tm
    assert loc.shape == (n_tok, LANES)
    grid_spec = pltpu.PrefetchScalarGridSpec(
        num_scalar_prefetch=2,
        grid=(nt,),
        in_specs=[pl.BlockSpec(memory_space=pl.ANY),
                  pl.BlockSpec((tm, LANES), lambda i, *_: (i, 0)),
                  pl.BlockSpec((tm, LANES), lambda i, *_: (i, 0)),
                  pl.BlockSpec((tm, d), lambda i, *_: (i, 0)),
                  pl.BlockSpec((1, d), lambda i, *_: (0, 0)),
                  pl.BlockSpec((1, 1, d), lambda i, *_: (i // per_b, 0, 0))],
        out_specs=pl.BlockSpec((tm, d), lambda i, *_: (i, 0)),
        scratch_shapes=[pltpu.VMEM((2, TOP_K * tm * sub, LANES), F32), pltpu.SemaphoreType.DMA((2,))],
    )
    return pl.pallas_call(
        functools.partial(_combine_kernel, tm=tm, sub=sub, ne=ne),
        grid_spec=grid_spec,
        out_shape=jax.ShapeDtypeStruct((n_tok, d), F32),
        compiler_params=_cparams(("arbitrary",)),
        name="combine",
    )(cnt_tile.reshape(-1), start_tile.reshape(-1), yb, loc, tw, x1, post_g.reshape(1, d), gt_f)


def kernel(x, c, positions, ada_w, ada_b, mix_pre_g, mix_post_g, ffn_pre_g, ffn_post_g, w_in, ssm_lam_re, ssm_lam_im, ssm_log_dt, ssm_b_re, ssm_b_im, ssm_c_re, ssm_c_im, ssm_d, ssm_w_glu, w_ssm_branch, w_attn_branch, w_out, router_w, router_b, w_gate, b_gate, w_up, b_up, w_down, b_down):
    bsz, seq, d = x.shape
    depth = ada_w.shape[0]
    n_tok = bsz * seq
    bm = EXPERT_ROWS
    sub = d // LANES
    xcur = x.reshape(n_tok, d)
    pos = positions.reshape(n_tok, 1).astype(I32)
    for l in range(depth):
        ada = _ada(c, ada_w[l], ada_b[l])
        sh_m, sc_m, gt_m, sh_f, sc_f, gt_f = [a.reshape(bsz, 1, d) for a in jnp.split(ada, 6, axis=-1)]

        u, q, k, v, gs, ga, km = _inproj(xcur, sc_m, sh_m, mix_pre_g[l].reshape(1, d), pos, w_in[l],
                                         seq, tm=min(512, seq))
        ys = _s5(u, ssm_lam_re[l], ssm_lam_im[l], ssm_log_dt[l], ssm_b_re[l], ssm_b_im[l],
                 ssm_c_re[l], ssm_c_im[l], bsz, seq)
        ya = _moba(q, k, v, km, bsz, seq)
        tm = ROUTE_TILE
        x1, h2, tw, loc, before, cnt = _merge(
            ys, u, ya, gs, ga, xcur, ssm_d[l], ssm_w_glu[l], w_ssm_branch[l], w_attn_branch[l], w_out[l],
            mix_post_g[l], gt_m, ffn_pre_g[l], sc_f, sh_f, router_w[l], router_b[l], seq, tm=tm)

        ne = router_w.shape[-1]
        nt = n_tok // tm
        counts = cnt[0, :ne].astype(I32)
        padded = (counts + bm - 1) // bm * bm
        p_ends = jnp.cumsum(padded)
        p_starts = p_ends - padded
        before = before.reshape(nt, LANES)[:, :ne].astype(I32)
        cnt_tile = jnp.concatenate([before[1:], counts[None]], axis=0) - before
        start_tile = p_starts[None, :] + before
        n_blocks = (n_tok * TOP_K) // bm + ne
        n_rows = n_blocks * bm
        blk_start = jnp.arange(n_blocks, dtype=I32)[:, None] * bm
        blk_exp = jnp.minimum(jnp.sum((blk_start >= p_ends[None, :]).astype(I32), axis=1), ne - 1)
        n_valid = (p_ends[-1:] // bm).astype(I32)

        xs = _dispatch(h2, loc[:, :TOP_K], cnt_tile, start_tile, p_starts + counts, padded - counts, p_ends[-1:],
                       n_rows, sub, bm, tm)
        yb = _experts(xs, blk_exp, n_valid, w_gate[l], b_gate[l], w_up[l], b_up[l], w_down[l], b_down[l], bm)
        xcur = _combine(yb, loc, tw, cnt_tile, start_tile, x1, ffn_post_g[l], gt_f, seq, tm)
    return xcur.reshape(bsz, seq, d).astype(x.dtype)
```

```python
import functools
import math

import jax
import jax.numpy as jnp
from jax import lax
from jax.experimental import pallas as pl
from jax.experimental.pallas import tpu as pltpu

F32 = jnp.float32
BF16 = jnp.bfloat16
I32 = jnp.int32

N_HEADS = 8
HEAD_DIM = 64
ROPE_THETA = 10000.0
MOBA_BLOCK = 256
MOBA_TOPK = 3
SSM_GROUP_SIZE = 16
SSM_GROUPS = 32
SSM_STATE = 64
TOP_K = 4
SWIGLU_ALPHA = 1.702
SWIGLU_LIMIT = 7.0
NORM_EPS = 1e-6
NEG_INF = -1e30

LANES = 128
SSM_CHUNK = 64
EXPERT_ROWS = 512
ROUTE_TILE = 512
VMEM_LIMIT = 56 * 1024 * 1024
HIGHEST = lax.Precision.HIGHEST
Q_SCALE = HEAD_DIM ** -0.5 * math.log2(math.e)


def _cparams(sem):
    return pltpu.CompilerParams(dimension_semantics=sem, vmem_limit_bytes=VMEM_LIMIT)


def _dot(a, b, **kw):
    return jnp.dot(a, b, preferred_element_type=F32, **kw)


def _dot_t(a, b, **kw):
    return lax.dot_general(a, b, (((1,), (1,)), ((), ())), preferred_element_type=F32, **kw)


def _ada_kernel(c_ref, w_ref, b_ref, o_ref):
    c = c_ref[...]
    cond = c * jax.nn.sigmoid(c)
    o_ref[...] = _dot(cond, w_ref[...], precision=HIGHEST) + b_ref[...]


def _ada(c, ada_w, ada_b):
    bsz, d = c.shape
    n = ada_w.shape[1]
    c8 = jnp.zeros((8, d), F32).at[:bsz].set(c)
    out = pl.pallas_call(
        _ada_kernel,
        grid=(n // d,),
        in_specs=[pl.BlockSpec((8, d), lambda j: (0, 0)),
                  pl.BlockSpec((d, d), lambda j: (0, j)),
                  pl.BlockSpec((1, d), lambda j: (0, j))],
        out_specs=pl.BlockSpec((8, d), lambda j: (0, j)),
        out_shape=jax.ShapeDtypeStruct((8, n), F32),
        compiler_params=_cparams(("arbitrary",)),
        name="ada",
    )(c8, ada_w, ada_b.reshape(1, n))
    return out[:bsz]


def _transpose_kernel(w_ref, o_ref):
    o_ref[...] = w_ref[...].T.astype(BF16)


def _inproj_kernel(x_ref, sc_ref, sh_ref, g_ref, post_ref, invft_ref, w_ref, wt_ref,
                   u_ref, qt_ref, k_ref, vt_ref, gs_ref, ga_ref, km_ref, *, ssm_w, attn_w, d_model):
    x = x_ref[...]
    ms = jnp.mean(x * x, axis=-1, keepdims=True)
    xn = x * lax.rsqrt(ms + NORM_EPS) * g_ref[...]
    h = (xn * (1.0 + sc_ref[0]) + sh_ref[0]).astype(BF16)

    def proj(lo, width):
        return _dot(h, w_ref[:, lo:lo + width])

    tm = x.shape[0]
    ang_t = invft_ref[...] * post_ref[0].astype(F32)
    cos_t, sin_t = jnp.cos(ang_t), jnp.sin(ang_t)
    fold = LANES // (HEAD_DIM // 2)
    cos128 = jnp.concatenate([cos_t] * fold, axis=0).T
    sin128 = jnp.concatenate([sin_t] * fold, axis=0).T
    reps = attn_w // LANES
    cos = jnp.concatenate([cos128] * reps, axis=1)
    sin = jnp.concatenate([sin128] * reps, axis=1)
    lane = lax.broadcasted_iota(I32, (1, attn_w), 1)
    first = (lane % HEAD_DIM) < (HEAD_DIM // 2)
    sin = jnp.where(first, -sin, sin)

    u_ref[...] = proj(0, ssm_w)
    gs_ref[...] = jax.nn.sigmoid(proj(ssm_w + 3 * attn_w, d_model)).astype(BF16)
    ga_ref[...] = jax.nn.sigmoid(proj(ssm_w + 3 * attn_w + d_model, d_model)).astype(BF16)
    vt = _dot_t(wt_ref[attn_w:2 * attn_w, :], h).astype(BF16)
    nblk = tm // MOBA_BLOCK
    for j in range(nblk):
        vt_ref[j] = vt[:, j * MOBA_BLOCK:(j + 1) * MOBA_BLOCK]

    def rope(t):
        rot = jnp.where(first, pltpu.roll(t, attn_w - HEAD_DIM // 2, axis=1),
                        pltpu.roll(t, HEAD_DIM // 2, axis=1))
        return t * cos + rot * sin

    k = rope(proj(ssm_w + attn_w, attn_w))
    k_ref[...] = k.astype(BF16)
    km_ref[0] = jnp.mean(k.reshape(nblk, MOBA_BLOCK, attn_w), axis=1)

    half = HEAD_DIM // 2
    qt = _dot_t(wt_ref[0:attn_w, :], h).reshape(N_HEADS, 2, half, tm)
    t1, t2 = qt[:, 0], qt[:, 1]
    qt = jnp.stack([t1 * cos_t - t2 * sin_t, t2 * cos_t + t1 * sin_t], axis=1).reshape(attn_w, tm)
    qt_ref[...] = (qt * Q_SCALE).astype(BF16)


def _inproj(x2, sc, sh, g, pos, w_in, seq, tm):
    n_tok, d = x2.shape
    attn_w = N_HEADS * HEAD_DIM
    ssm_w = SSM_GROUPS * SSM_GROUP_SIZE
    in_w = w_in.shape[1]
    half = HEAD_DIM // 2
    inv_freq = ROPE_THETA ** (-jnp.arange(half, dtype=F32) / half)
    nt = n_tok // tm
    per_b = seq // tm
    nblk = tm // MOBA_BLOCK
    bsz = n_tok // seq
    blk = MOBA_BLOCK
    assert ssm_w % attn_w == 0
    q_blk, v_blk = ssm_w // attn_w, ssm_w // attn_w + 2
    wt = pl.pallas_call(
        _transpose_kernel,
        grid=(2,),
        in_specs=[pl.BlockSpec((d, attn_w), lambda j: (0, q_blk + j * (v_blk - q_blk)))],
        out_specs=pl.BlockSpec((attn_w, d), lambda j: (j, 0)),
        out_shape=jax.ShapeDtypeStruct((2 * attn_w, d), BF16),
        compiler_params=_cparams(("arbitrary",)),
        name="wt",
    )(w_in)
    tok = lambda w: pl.BlockSpec((tm, w), lambda i: (i, 0))
    full = lambda a, b: pl.BlockSpec((a, b), lambda i: (0, 0))
    bvec = pl.BlockSpec((1, 1, d), lambda i: (i // per_b, 0, 0))
    outs = pl.pallas_call(
        functools.partial(_inproj_kernel, ssm_w=ssm_w, attn_w=attn_w, d_model=d),
        grid=(nt,),
        in_specs=[tok(d), bvec, bvec, full(1, d),
                  pl.BlockSpec((1, 1, tm), lambda i: (i, 0, 0)), full(half, 1),
                  full(d, in_w), full(2 * attn_w, d)],
        out_specs=[tok(ssm_w),
                   pl.BlockSpec((None, attn_w, tm), lambda i: (i // per_b, 0, i % per_b)),
                   tok(attn_w),
                   pl.BlockSpec((None, nblk, attn_w, blk), lambda i: (i // per_b, i % per_b, 0, 0)),
                   tok(d), tok(d),
                   pl.BlockSpec((1, nblk, attn_w), lambda i: (i, 0, 0))],
        out_shape=[jax.ShapeDtypeStruct((n_tok, ssm_w), F32),
                   jax.ShapeDtypeStruct((bsz, attn_w, seq), BF16),
                   jax.ShapeDtypeStruct((n_tok, attn_w), BF16),
                   jax.ShapeDtypeStruct((bsz, seq // blk, attn_w, blk), BF16),
                   jax.ShapeDtypeStruct((n_tok, d), BF16),
                   jax.ShapeDtypeStruct((n_tok, d), BF16),
                   jax.ShapeDtypeStruct((nt, nblk, attn_w), F32)],
        compiler_params=_cparams(("arbitrary",)),
        name="inproj",
    )(x2, sc, sh, g, pos.reshape(nt, 1, tm), inv_freq.reshape(half, 1), w_in.astype(BF16), wt)
    return outs


def _s5_group(x, zr, zi, lr, li, ca, cb, ba, bb, t_ref, *, chunk, n_chunks):
    gs = SSM_GROUP_SIZE

    def powers(tau):
        mag = jnp.exp(tau * zr)
        return mag * jnp.cos(tau * zi), mag * jnp.sin(tau * zi)

    tau = lax.broadcasted_iota(I32, (chunk + 8, LANES), 0).astype(F32)
    e_re, e_im = powers(tau)
    lb_re, lb_im = e_re[1:2], e_im[1:2]
    den = lr * lr + li * li
    a, b = lb_re - 1.0, lb_im
    cf_re = (a * lr + b * li) / den
    cf_im = (b * lr - a * li) / den
    bri = cf_re * ba + cf_im * bb
    bri_sw = cf_re * bb - cf_im * ba

    cpow =(e_re[:chunk + 1, None, :] * ca[None] + e_im[:chunk + 1, None, :] * cb[None])
    cpow = cpow.reshape((chunk + 1) * gs, LANES)
    width = chunk * gs
    r = _dot_t(bri, cpow[:width], precision=HIGHEST)
    col = lax.broadcasted_iota(I32, (gs, width), 1)
    t_ref[0:gs, :] = r.astype(BF16)
    for j in range(1, chunk):
        shifted = jnp.where(col >= gs * j, pltpu.roll(r, gs * j, axis=1), 0.0)
        t_ref[gs * j:gs * (j + 1), :] = shifted.astype(BF16)

    y = _dot(x, t_ref[...])

    tau_rev = (chunk - 1) - lax.broadcasted_iota(I32, (chunk, LANES), 0)
    r_re, r_im = powers(tau_rev.astype(F32))
    bst = r_re[:, None, :] * bri[None] + r_im[:, None, :] * bri_sw[None]
    bst = bst.reshape(width, LANES).astype(BF16)
    s = _dot(x, bst)

    rows = s.shape[0]
    n_idx = lax.broadcasted_iota(I32, (rows, LANES), 0) % n_chunks
    lane = lax.broadcasted_iota(I32, (1, LANES), 1)
    half = LANES // 2

    def cmul(v, p_re, p_im):
        return v * p_re + pltpu.roll(v, half, axis=1) * jnp.where(lane < half, -p_im, p_im)

    sh = 1
    while sh < n_chunks:
        p_re, p_im = powers(jnp.full((1, LANES), float(chunk * sh), F32))
        prev = jnp.where(n_idx >= sh, pltpu.roll(s, sh, axis=0), 0.0)
        s = s + cmul(prev, p_re, p_im)
        sh *= 2
    s_in = jnp.where(n_idx >= 1, pltpu.roll(s, 1, axis=0), 0.0)
    return y + _dot_t(s_in.astype(BF16), cpow[gs:gs + width].astype(BF16))


def _s5_kernel(u_ref, sel_ref, zr_ref, zi_ref, lr_ref, li_ref, ca_ref, cb_ref, ba_ref, bb_ref,
               y_ref, x_scr, y_scr, t_ref, *, chunk, n_chunks, rows):
    gpt = LANES // SSM_GROUP_SIZE

    def plane(j):
        return pl.ds(j, rows, stride=chunk)

    for jt in range(chunk // gpt):
        planes = jnp.concatenate([u_ref[plane(jt * gpt + jj), :] for jj in range(gpt)], axis=1)
        grouped = _dot(planes.astype(BF16), sel_ref[...])
        for g in range(gpt):
            x_scr[g, :, jt * LANES:(jt + 1) * LANES] = grouped[:, g * LANES:(g + 1) * LANES].astype(BF16)

    def per_group(g, _):
        y = _s5_group(x_scr[g], zr_ref[g], zi_ref[g], lr_ref[g], li_ref[g], ca_ref[g], cb_ref[g], ba_ref[g],
                      bb_ref[g], t_ref, chunk=chunk, n_chunks=n_chunks)
        y_scr[g] = y.astype(BF16)
        return 0

    lax.fori_loop(0, gpt, per_group, 0)

    for jt in range(chunk // gpt):
        grouped = jnp.concatenate([y_scr[g, :, jt * LANES:(jt + 1) * LANES] for g in range(gpt)], axis=1)
        planes = _dot(grouped, sel_ref[...])
        for jj in range(gpt):
            y_ref[plane(jt * gpt + jj), :] = planes[:, jj * LANES:(jj + 1) * LANES]


def _s5(u, lam_re, lam_im, log_dt, b_re, b_im, c_re, c_im, bsz, seq):
    g_n, gs, p = SSM_GROUPS, SSM_GROUP_SIZE, SSM_STATE
    chunk = SSM_CHUNK
    n_chunks = seq // chunk
    rows = bsz * n_chunks
    width = chunk * gs
    gpt = LANES // gs
    assert chunk % gpt == 0 and g_n % gpt == 0 and 2 * p == LANES
    dt = jnp.exp(log_dt.astype(F32))[:, None]
    dup = lambda a: jnp.concatenate([a, a], axis=-1).reshape(g_n, 1, 2 * p)
    zr, zi = dup(lam_re * dt), dup(lam_im * dt)
    lr, li = dup(lam_re), dup(lam_im)
    bt_re, bt_im = b_re.transpose(0, 2, 1), b_im.transpose(0, 2, 1)
    ba = jnp.concatenate([bt_re, bt_im], axis=-1)
    bb = jnp.concatenate([-bt_im, bt_re], axis=-1)
    ca = jnp.concatenate([c_re, -c_im], axis=-1)
    cb = jnp.concatenate([-c_im, -c_re], axis=-1)
    idx = jnp.arange(gpt * LANES, dtype=I32)
    swapped = ((idx // gs) % gpt) * LANES + (idx // LANES) * gs + idx % gs
    sel = (swapped[:, None] == idx[None, :]).astype(BF16)
    tile = pl.BlockSpec((bsz * seq, LANES), lambda t: (0, t))
    vec = pl.BlockSpec((gpt, 1, 2 * p), lambda t: (t, 0, 0))
    mat = pl.BlockSpec((gpt, gs, 2 * p), lambda t: (t, 0, 0))
    return pl.pallas_call(
        functools.partial(_s5_kernel, chunk=chunk, n_chunks=n_chunks, rows=rows),
        grid=(g_n // gpt,),
        in_specs=[tile, pl.BlockSpec((gpt * LANES, gpt * LANES), lambda t: (0, 0)),
                  vec, vec, vec, vec, mat, mat, mat, mat],
        out_specs=tile,
        out_shape=jax.ShapeDtypeStruct((bsz * seq, g_n * gs), F32),
        scratch_shapes=[pltpu.VMEM((gpt, rows, width), BF16), pltpu.VMEM((gpt, rows, width), BF16),
                        pltpu.VMEM((width, width), BF16)],
        compiler_params=_cparams(("arbitrary",)),
        name="s5",
    )(u, sel, zr, zi, lr, li, ca, cb, ba, bb)


def _moba_kernel(qt_ref, k_ref, vt_ref, km_ref, o_ref, s_a, s_b, p_a, p_b, *, nb, nbp, qblocks):
    blk = MOBA_BLOCK
    group = 2
    keys = group * blk
    nq = qblocks * blk
    first = qblocks * pl.program_id(2)
    qt = qt_ref[...]
    km = km_ref[...]
    km_hi = km.astype(BF16)
    km_lo = (km - km_hi.astype(F32)).astype(BF16)
    dim_i = lax.broadcasted_iota(I32, (LANES, nq), 0)
    blk_i = lax.broadcasted_iota(I32, (nbp, nq), 0)
    blk_f = blk_i.astype(F32)
    own = first + lax.broadcasted_iota(I32, (nbp, nq), 1) // blk
    valid = blk_i < own

    q_aug = []
    for h in range(2):
        qh = jnp.where((dim_i >= h * HEAD_DIM) & (dim_i < (h + 1) * HEAD_DIM), qt, jnp.zeros_like(qt))
        g = jnp.where(valid, _dot(km_hi, qh) + _dot(km_lo, qh), NEG_INF)
        sel = jnp.zeros((nbp, nq), jnp.bool_)
        for _ in range(MOBA_TOPK):
            m = jnp.max(g, axis=0, keepdims=True)
            idx = jnp.min(jnp.where(g == m, blk_f, float(nbp)), axis=0, keepdims=True)
            pick = blk_f == idx
            sel = sel | pick
            g = jnp.where(pick, -jnp.inf, g)
        ok = (sel & valid) | (blk_i == own)
        parts = [qh, jnp.where(ok, 0.0, NEG_INF).astype(BF16)]
        if nbp < LANES:
            parts.append(jnp.full((LANES - nbp, nq), NEG_INF, BF16))
        q_aug.append(jnp.concatenate(parts, axis=0))

    def scores(grp, s_ref, causal=False, lo=0):
        kb0 = grp * group
        k_cat = k_ref[pl.ds(jnp.minimum(kb0, nb - group), group)].reshape(keys, LANES)
        blk_id = kb0 + lax.broadcasted_iota(I32, (keys, LANES), 0) // blk
        onehot = jnp.where(lax.broadcasted_iota(I32, (keys, LANES), 1) == blk_id, 1.0, 0.0).astype(BF16)
        k_aug = jnp.concatenate([k_cat, onehot], axis=1)
        mx = []
        for h in range(2):
            s = _dot(k_aug, q_aug[h][:, lo:])
            if causal:
                key_pos = (kb0 - first) * blk + lax.broadcasted_iota(I32, (keys, nq - lo), 0)
                s = jnp.where(key_pos <= lo + lax.broadcasted_iota(I32, (keys, nq - lo), 1), s, NEG_INF)
            s_ref[h, :, lo:] = s
            mx.append(jnp.max(s, axis=0, keepdims=True))
        return tuple(mx)

    def softmax(s_ref, mx, p_ref, st, lo=0):
        out = []
        for h in range(2):
            m_i, _, l_i, acc = st[h]
            m_new = jnp.maximum(m_i[:, lo:], mx[h])
            p_ref[h, :, lo:] = jnp.exp2(s_ref[h, :, lo:] - m_new).astype(BF16)
            alpha = jnp.exp2(m_i[:, lo:] - m_new)
            if lo:
                m_new = jnp.concatenate([m_i[:, :lo], m_new], axis=1)
                alpha = jnp.concatenate([jnp.ones((1, lo), F32), alpha], axis=1)
            out.append((m_new, alpha, l_i, acc))
        return tuple(out)

    def values(grp, p_ref, st, lo=0, live=None):
        out = []
        for h in range(2):
            m_i, alpha, l_i, acc = st[h]
            blocks = [vt_ref[jnp.clip(grp * group + j, 0, nb - 1), h * HEAD_DIM:(h + 1) * HEAD_DIM, :]
                      for j in range(group)]
            ones = jnp.ones((16, keys), BF16)
            pv = _dot(jnp.concatenate([jnp.concatenate(blocks, axis=1), ones], axis=0), p_ref[h, :, lo:])
            if live is not None:
                pv = pv * live
            if lo:
                pv = jnp.concatenate([jnp.zeros((pv.shape[0], lo), F32), pv], axis=1)
            out.append((m_i, alpha, alpha * l_i + pv[HEAD_DIM:HEAD_DIM + 1], alpha * acc + pv[:HEAD_DIM]))
        return tuple(out)

    mx_first = scores(0, s_a)

    @pl.when((pl.program_id(0) == 0) & (pl.program_id(1) == 0) & (pl.program_id(2) == 0))
    def _():
        p_b[...] = jnp.zeros_like(p_b)

    def body(t, carry):
        st, mx_a = carry
        mx_b = scores(2 * t + 1, s_b)
        st = values(2 * t - 1, p_b, st, live=jnp.where(t > 0, 1.0, 0.0))
        st = softmax(s_a, mx_a, p_a, st)
        mx_a = scores(2 * t + 2, s_a)
        st = values(2 * t, p_a, st)
        return softmax(s_b, mx_b, p_b, st), mx_a

    init = (jnp.full((1, nq), -jnp.inf, F32), jnp.ones((1, nq), F32), jnp.zeros((1, nq), F32),
            jnp.zeros((HEAD_DIM, nq), F32))
    g0 = first // group
    st, _ = lax.fori_loop(0, g0 // 2, body, ((init, init), mx_first))
    def mask_in_place(grp, s_ref):
        mx = []
        for h in range(2):
            key_pos = (grp * group - first) * blk + lax.broadcasted_iota(I32, (keys, nq), 0)
            s = jnp.where(key_pos <= lax.broadcasted_iota(I32, (keys, nq), 1), s_ref[h], NEG_INF)
            s_ref[h] = s
            mx.append(jnp.max(s, axis=0, keepdims=True))
        return tuple(mx)

    lo_b = 0
    for pair in range(qblocks // (2 * group)):
        ga = g0 + 2 * pair
        lo = 2 * pair * keys
        mx_a = mask_in_place(ga, s_a) if pair == 0 else scores(ga, s_a, causal=True, lo=lo)
        st = values(ga - 1, p_b, st, lo=lo_b, live=jnp.where(ga > 0, 1.0, 0.0))
        lo_b = lo + keys
        mx_b = scores(ga + 1, s_b, causal=True, lo=lo_b)
        st = softmax(s_a, mx_a, p_a, st, lo=lo)
        st = values(ga, p_a, st, lo=lo)
        st = softmax(s_b, mx_b, p_b, st, lo=lo_b)
    st = values(g0 + qblocks // group - 1, p_b, st, lo=lo_b)
    both = jnp.concatenate([st[h][3] / st[h][2] for h in range(2)], axis=0)
    o_ref[...] = both.T.astype(BF16)


def _moba(qt, k, vt4, km, bsz, seq):
    n_tok, attn_w = k.shape
    blk = MOBA_BLOCK
    nb = seq // blk
    group, qblocks = 2, 4
    nbp = -(-nb // 16) * 16
    assert nb % qblocks == 0 and qblocks % (2 * group) == 0 and nbp <= LANES
    hp = attn_w // LANES
    km_pad = jnp.zeros((bsz, nbp, attn_w), F32).at[:, :nb].set(km.reshape(bsz, nb, attn_w))
    k4 = k.reshape(bsz, nb, blk, attn_w)
    steps = nb // qblocks
    return pl.pallas_call(
        functools.partial(_moba_kernel, nb=nb, nbp=nbp, qblocks=qblocks),
        grid=(bsz, hp, steps),
        in_specs=[pl.BlockSpec((None, LANES, qblocks * blk), lambda b, p, i: (b, p, i)),
                  pl.BlockSpec((None, nb, blk, LANES), lambda b, p, i: (b, 0, 0, p)),
                  pl.BlockSpec((None, nb, LANES, blk), lambda b, p, i: (b, 0, p, 0)),
                  pl.BlockSpec((None, nbp, LANES), lambda b, p, i: (b, 0, p))],
        out_specs=pl.BlockSpec((qblocks * blk, LANES), lambda b, p, i: (b * steps + i, p)),
        out_shape=jax.ShapeDtypeStruct((n_tok, attn_w), BF16),
        scratch_shapes=([pltpu.VMEM((2, group * blk, qblocks * blk), F32)] * 2
                        + [pltpu.VMEM((2, group * blk, qblocks * blk), BF16)] * 2),
        compiler_params=_cparams(("arbitrary", "arbitrary", "arbitrary")),
        name="moba",
    )(qt, k4, vt4, km_pad)


def _rms(t, g):
    return t * lax.rsqrt(jnp.mean(t * t, axis=-1, keepdims=True) + NORM_EPS) * g


def _store_slab(ref, val, base=0):
    rows, d = val.shape
    sub = d // LANES
    for s in range(sub):
        ref[pl.ds(base + s, rows, stride=sub), :] = val[:, s * LANES:(s + 1) * LANES]


def _load_slab(ref, rows, sub, base=0):
    return jnp.concatenate([ref[pl.ds(base + s, rows, stride=sub), :] for s in range(sub)], axis=1)


def _merge_kernel(ys_ref, u_ref, ya_ref, gs_ref, ga_ref, x_ref, d_ref, wglu_ref, wsb_ref, wab_ref,
                  wout_ref, postg_ref, gtm_ref, preg_ref, scf_ref, shf_ref, rw_ref, rb_ref,
                  x1_ref, h2_ref, tw_ref, loc_ref, before_ref, cnt_ref, run_ref):
    @pl.when(pl.program_id(0) == 0)
    def _():
        run_ref[...] = jnp.zeros_like(run_ref)

    tm = x_ref.shape[0]
    hm = tm // 2
    halves = [pl.ds(0, hm), pl.ds(hm, hm)]
    sub = x_ref.shape[1] // LANES

    ys = [jax.nn.gelu(ys_ref[r, :] + d_ref[...] * u_ref[r, :].astype(F32)) for r in halves]
    h2s = []
    for r, y in zip(halves, ys):
        y = y * jax.nn.sigmoid(_dot(y.astype(BF16), wglu_ref[...]))
        bs = _dot(y.astype(BF16), wsb_ref[...])
        ba = _dot(ya_ref[r, :], wab_ref[...])
        merged = gs_ref[r, :].astype(F32) * bs + ga_ref[r, :].astype(F32) * ba
        mix = _dot(merged.astype(BF16), wout_ref[...])
        x1 = x_ref[r, :] + gtm_ref[0] * _rms(mix, postg_ref[...])
        x1_ref[r, :] = x1
        h2s.append(_rms(x1, preg_ref[...]) * (1.0 + scf_ref[0]) + shf_ref[0])
    for j, h2 in enumerate(h2s):
        _store_slab(h2_ref, h2, base=j * hm * sub)

    lane = lax.broadcasted_iota(I32, (hm, LANES), 1)
    lane_f = lane.astype(F32)
    r_i = lax.broadcasted_iota(I32, (hm, hm), 0)
    c_i = lax.broadcasted_iota(I32, (hm, hm), 1)
    tri = jnp.where(c_i < r_i, 1.0, 0.0).astype(BF16)
    before_ref[0] = run_ref[...]
    gs = []
    for h2 in h2s:
        h_hi = h2.astype(BF16)
        h_lo = (h2 - h_hi.astype(F32)).astype(BF16)
        gs.append(_dot(jnp.concatenate([h_hi, h_hi, h_lo], axis=1), rw_ref[...]) + rb_ref[...])
    topk = [([], []) for _ in halves]
    for _ in range(TOP_K):
        for j in range(2):
            m = jnp.max(gs[j], axis=1, keepdims=True)
            idx = jnp.min(jnp.where(gs[j] == m, lane_f, float(LANES)), axis=1, keepdims=True)
            pick = lane_f == idx
            topk[j][0].append(m)
            topk[j][1].append(pick)
            gs[j] = jnp.where(pick, -jnp.inf, gs[j])
    onehots = [jnp.where(p[0] | p[1] | p[2] | p[3], 1.0, 0.0) for _, p in topk]
    cnt_tile = sum(jnp.sum(o, axis=0, keepdims=True) for o in onehots)
    cnt_hi = jnp.floor(cnt_tile * (1.0 / 256.0))
    parts = jnp.concatenate([cnt_hi, cnt_tile - 256.0 * cnt_hi, jnp.zeros((6, LANES), F32)], axis=0)
    before_e = jnp.where(lax.broadcasted_iota(I32, (LANES, LANES), 0) < lax.broadcasted_iota(I32, (LANES, LANES), 1),
                         1.0, 0.0).astype(BF16)
    pref = _dot(parts.astype(BF16), before_e)
    off_tile = 256.0 * pref[0:1] + pref[1:2]
    for r, (vals, picks), onehot in zip(halves, topk, onehots):
        exps = [jnp.exp(v - vals[0]) for v in vals]
        tot = exps[0] + exps[1] + exps[2] + exps[3]

        rank_full = _dot(tri, onehot.astype(BF16)) + (run_ref[...] - before_ref[0]) + off_tile
        run_ref[...] = run_ref[...] + jnp.sum(onehot, axis=0, keepdims=True)

        tw = jnp.zeros((hm, LANES), F32)
        rk = jnp.zeros((hm, LANES), F32)
        for k in range(TOP_K):
            tw = jnp.where(lane == k, exps[k] / tot, tw)
            pos = jnp.sum(jnp.where(picks[k], rank_full, 0.0), axis=1, keepdims=True)
            rk = jnp.where(lane == k, pos, rk)
        tw_ref[r, :] = tw
        loc_ref[r, :] = rk.astype(I32)
    cnt_ref[...] = run_ref[...]


def _merge(ys, u, ya, gs, ga, x2, ssm_d, w_glu, w_sb, w_ab, w_out, post_g, gt_m, pre_g, sc_f, sh_f,
           router_w, router_b, seq, tm):
    n_tok, d = x2.shape
    sw = ys.shape[1]
    aw = ya.shape[1]
    ne = router_w.shape[1]
    per_b = seq // tm
    rw = jnp.zeros((d, LANES), F32).at[:, :ne].set(router_w)
    rw_hi = rw.astype(BF16)
    rw_lo = (rw - rw_hi.astype(F32)).astype(BF16)
    rw = jnp.concatenate([rw_hi, rw_lo, rw_hi], axis=0)
    rb = jnp.full((1, LANES), NEG_INF, F32).at[0, :ne].set(router_b)
    tok = lambda w: pl.BlockSpec((tm, w), lambda i: (i, 0))
    full = lambda a, b: pl.BlockSpec((a, b), lambda i: (0, 0))
    bvec = pl.BlockSpec((1, 1, d), lambda i: (i // per_b, 0, 0))
    return pl.pallas_call(
        _merge_kernel,
        grid=(n_tok // tm,),
        in_specs=[tok(sw), tok(sw), tok(aw), tok(d), tok(d), tok(d),
                  full(1, sw), full(sw, sw), full(sw, d), full(aw, d), full(d, d),
                  full(1, d), bvec, full(1, d), bvec, bvec, full(3 * d, LANES), full(1, LANES)],
        out_specs=[tok(d), pl.BlockSpec((tm * (d // LANES), LANES), lambda i: (i, 0)),
                   tok(LANES), tok(LANES),
                   pl.BlockSpec((1, 1, LANES), lambda i: (i, 0, 0)), full(1, LANES)],
        out_shape=[jax.ShapeDtypeStruct((n_tok, d), F32),
                   jax.ShapeDtypeStruct((n_tok * (d // LANES), LANES), F32),
                   jax.ShapeDtypeStruct((n_tok, LANES), F32),
                   jax.ShapeDtypeStruct((n_tok, LANES), I32),
                   jax.ShapeDtypeStruct((n_tok // tm, 1, LANES), F32),
                   jax.ShapeDtypeStruct((1, LANES), F32)],
        scratch_shapes=[pltpu.VMEM((1, LANES), F32)],
        compiler_params=_cparams(("arbitrary",)),
        name="merge",
    )(ys, u, ya, gs, ga, x2, ssm_d.reshape(1, sw), w_glu.astype(BF16), w_sb.astype(BF16),
      w_ab.astype(BF16), w_out.astype(BF16), post_g.reshape(1, d), gt_m, pre_g.reshape(1, d),
      sc_f, sh_f, rw, rb)


def _chunk_copies(cnt_ref, start_ref, tile, ne, max_rows, make):
    def per_expert(e, off):
        c, start = cnt_ref[tile * ne + e], start_ref[tile * ne + e]
        for b in range(max_rows.bit_length()):
            n = 1 << b

            @pl.when((c >> b) & 1 == 1)
            def _():
                make(off + (c & (n - 1)), start + (c & (n - 1)), n)
        return off + c

    lax.fori_loop(0, ne, per_expert, 0)


def _dispatch_kernel(cnt_ref, start_ref, pad_start_ref, pad_cnt_ref, tail_ref, loct_ref, h_ref, xs_ref,
                     buf, zero_ref, sem, zsem, *, tm, sub, bm, n_rows, ne):
    i = pl.program_id(0)
    last = pl.num_programs(0) - 1
    slot = i % 2
    n_loc = tm * TOP_K

    def row(ref, r, n=1):
        return ref.at[pl.ds(pl.multiple_of(r * sub, sub), n * sub)]

    def wait_slot(s):
        pltpu.make_async_copy(buf.at[s], row(xs_ref, 0, n_loc), sem.at[s]).wait()

    def zero_copies(do):
        def per_expert(e, _):
            start, cnt = pad_start_ref[e], pad_cnt_ref[e]
            for b in range(bm.bit_length() - 1):
                n = 1 << b

                @pl.when((cnt >> b) & 1 == 1)
                def _():
                    do(pltpu.make_async_copy(row(zero_ref, 0, n), row(xs_ref, start + (cnt & (n - 1)), n), zsem))
            return 0

        lax.fori_loop(0, ne, per_expert, 0)

        def per_block(j, _):
            r = tail_ref[0] + j * bm

            @pl.when(r < n_rows)
            def _():
                do(pltpu.make_async_copy(row(zero_ref, 0, bm), row(xs_ref, r, bm), zsem))
            return 0

        lax.fori_loop(0, ne, per_block, 0)

    @pl.when(i == 0)
    def _():
        zero_ref[...] = jnp.zeros_like(zero_ref)
        zero_copies(lambda cp: cp.start())

    @pl.when(i >= 2)
    def _():
        wait_slot(slot)

    loct = loct_ref[0]
    row_i = lax.broadcasted_iota(I32, (n_loc, tm), 0)
    place = row_i == loct[0:1, :]
    for k in range(1, TOP_K):
        place = place | (row_i == loct[k:k + 1, :])
    grouped = _dot(jnp.where(place, 1.0, 0.0).astype(BF16), _load_slab(h_ref, tm, sub).astype(BF16))
    _store_slab(buf.at[slot], grouped)
    _chunk_copies(cnt_ref, start_ref, i, ne, tm,
                  lambda loc, glob, n: pltpu.make_async_copy(row(buf.at[slot], loc, n), row(xs_ref, glob, n),
                                                             sem.at[slot]).start())

    @pl.when(i == last)
    def _():
        wait_slot(slot)

    @pl.when((i == last) & (i >= 1))
    def _():
        wait_slot(1 - slot)

    @pl.when(i == 0)
    def _():
        zero_copies(lambda cp: cp.wait())


def _dispatch(h_slab, loc, cnt_tile, start_tile, pad_start, pad_cnt, tail, n_rows, sub, bm, tm):
    n_tok = loc.shape[0]
    nt = n_tok // tm
    ne = pad_start.shape[0]
    loct = jnp.full((nt, 8, tm), -1, I32).at[:, :TOP_K].set(loc.reshape(nt, tm, TOP_K).transpose(0, 2, 1))
    grid_spec = pltpu.PrefetchScalarGridSpec(
        num_scalar_prefetch=5,
        grid=(nt,),
        in_specs=[pl.BlockSpec((1, 8, tm), lambda i, *_: (i, 0, 0)),
                  pl.BlockSpec((tm * sub, LANES), lambda i, *_: (i, 0))],
        out_specs=pl.BlockSpec(memory_space=pl.ANY),
        scratch_shapes=[pltpu.VMEM((2, TOP_K * tm * sub, LANES), F32), pltpu.VMEM((bm * sub, LANES), F32),
                        pltpu.SemaphoreType.DMA((2,)), pltpu.SemaphoreType.DMA(())],
    )
    return pl.pallas_call(
        functools.partial(_dispatch_kernel, tm=tm, sub=sub, bm=bm, n_rows=n_rows, ne=ne),
        grid_spec=grid_spec,
        out_shape=jax.ShapeDtypeStruct((n_rows * sub, LANES), F32),
        compiler_params=_cparams(("arbitrary",)),
        name="dispatch",
    )(cnt_tile.reshape(-1), start_tile.reshape(-1), pad_start, pad_cnt, tail, loct, h_slab)


def _expert_kernel(be_ref, nv_ref, x_ref, wg_ref, bg_ref, wu_ref, bu_ref, wd_ref, bd_ref,
                   o_ref, wg_s, wu_s, wd_s, *, bm, sub):
    i = pl.program_id(0)
    prev = be_ref[jnp.maximum(i - 1, 0)]
    changed = (i == 0) | (be_ref[i] != prev)

    @pl.when(changed)
    def _():
        wg_s[...] = wg_ref[0].astype(BF16)
        wu_s[...] = wu_ref[0].astype(BF16)
        wd_s[...] = wd_ref[0].astype(BF16)

    @pl.when(i < nv_ref[0])
    def _():
        xb = _load_slab(x_ref, bm, sub).astype(BF16)
        f = wg_s.shape[1]
        y = bd_ref[0]
        for lo in range(0, f, f // 2):
            cols = slice(lo, lo + f // 2)
            g = _dot(xb, wg_s[:, cols]) + bg_ref[0][:, cols]
            u = _dot(xb, wu_s[:, cols]) + bu_ref[0][:, cols]
            g = jnp.minimum(g, SWIGLU_LIMIT)
            u = jnp.clip(u, -SWIGLU_LIMIT, SWIGLU_LIMIT)
            act = g * jax.nn.sigmoid(SWIGLU_ALPHA * g) * (u + 1.0)
            y = y + _dot(act.astype(BF16), wd_s[cols, :])
        _store_slab(o_ref, y)

    @pl.when(i >= nv_ref[0])
    def _():
        o_ref[...] = jnp.zeros_like(o_ref)


def _experts(xs, blk_exp, n_valid, w_gate, b_gate, w_up, b_up, w_down, b_down, bm):
    ne, d, f = w_gate.shape
    sub = d // LANES
    n_blocks = xs.shape[0] // (bm * sub)
    wspec = lambda a, b: pl.BlockSpec((1, a, b), lambda i, be, nv: (be[i], 0, 0))
    grid_spec = pltpu.PrefetchScalarGridSpec(
        num_scalar_prefetch=2,
        grid=(n_blocks,),
        in_specs=[pl.BlockSpec((bm * sub, LANES), lambda i, be, nv: (jnp.clip(nv[0] - 1, 0, i), 0)),
                  wspec(d, f), wspec(1, f), wspec(d, f), wspec(1, f), wspec(f, d), wspec(1, d)],
        out_specs=pl.BlockSpec((bm * sub, LANES), lambda i, be, nv: (i, 0)),
        scratch_shapes=[pltpu.VMEM((d, f), BF16), pltpu.VMEM((d, f), BF16), pltpu.VMEM((f, d), BF16)],
    )
    return pl.pallas_call(
        functools.partial(_expert_kernel, bm=bm, sub=sub),
        grid_spec=grid_spec,
        out_shape=jax.ShapeDtypeStruct(xs.shape, F32),
        compiler_params=_cparams(("arbitrary",)),
        name="experts",
    )(blk_exp, n_valid, xs, w_gate, b_gate.reshape(ne, 1, f), w_up, b_up.reshape(ne, 1, f),
      w_down, b_down.reshape(ne, 1, d))


def _combine_kernel(cnt_ref, start_ref, yb_ref, loc_ref, tw_ref, x1_ref, g_ref, gt_ref, o_ref, buf, sem,
                    *, tm, sub, ne):
    i = pl.program_id(0)
    n_loc = tm * TOP_K

    def start_tile(tile, slot):
        def make(loc, glob, n):
            src = yb_ref.at[pl.ds(pl.multiple_of(glob * sub, sub), n * sub)]
            dst = buf.at[slot, pl.ds(pl.multiple_of(loc * sub, sub), n * sub)]
            pltpu.make_async_copy(src, dst, sem.at[slot]).start()

        _chunk_copies(cnt_ref, start_ref, tile, ne, tm, make)

    slot = i % 2

    @pl.when(i == 0)
    def _():
        start_tile(0, 0)

    @pl.when(i + 1 < pl.num_programs(0))
    def _():
        start_tile(i + 1, 1 - slot)

    pltpu.make_async_copy(yb_ref.at[pl.ds(0, n_loc * sub)], buf.at[slot], sem.at[slot]).wait()
    rows = _load_slab(buf.at[slot], n_loc, sub).astype(BF16)
    loc, tw = loc_ref[...], tw_ref[...]
    col_i = lax.broadcasted_iota(I32, (tm, n_loc), 1)
    w = jnp.zeros((tm, n_loc), F32)
    for k in range(TOP_K):
        w = jnp.where(col_i == loc[:, k:k + 1], tw[:, k:k + 1], w)
    ff = _dot(w.astype(BF16), rows)
    o_ref[...] = x1_ref[...] + gt_ref[0] * _rms(ff, g_ref[...])


def _combine(yb, loc, tw, cnt_tile, start_tile, x1, post_g, gt_f, seq, tm):
    n_tok, d = x1.shape
    sub = d // LANES
    nt = n_tok // tm
    ne = cnt_tile.shape[1]
    per_b = seq // tm
    assert loc.shape == (n_tok, LANES)
    grid_spec = pltpu.PrefetchScalarGridSpec(
        num_scalar_prefetch=2,
        grid=(nt,),
        in_specs=[pl.BlockSpec(memory_space=pl.ANY),
                  pl.BlockSpec((tm, LANES), lambda i, *_: (i, 0)),
                  pl.BlockSpec((tm, LANES), lambda i, *_: (i, 0)),
                  pl.BlockSpec((tm, d), lambda i, *_: (i, 0)),
                  pl.BlockSpec((1, d), lambda i, *_: (0, 0)),
                  pl.BlockSpec((1, 1, d), lambda i, *_: (i // per_b, 0, 0))],
        out_specs=pl.BlockSpec((tm, d), lambda i, *_: (i, 0)),
        scratch_shapes=[pltpu.VMEM((2, TOP_K * tm * sub, LANES), F32), pltpu.SemaphoreType.DMA((2,))],
    )
    return pl.pallas_call(
        functools.partial(_combine_kernel, tm=tm, sub=sub, ne=ne),
        grid_spec=grid_spec,
        out_shape=jax.ShapeDtypeStruct((n_tok, d), F32),
        compiler_params=_cparams(("arbitrary",)),
        name="combine",
    )(cnt_tile.reshape(-1), start_tile.reshape(-1), yb, loc, tw, x1, post_g.reshape(1, d), gt_f)


def kernel(x, c, positions, ada_w, ada_b, mix_pre_g, mix_post_g, ffn_pre_g, ffn_post_g, w_in, ssm_lam_re, ssm_lam_im, ssm_log_dt, ssm_b_re, ssm_b_im, ssm_c_re, ssm_c_im, ssm_d, ssm_w_glu, w_ssm_branch, w_attn_branch, w_out, router_w, router_b, w_gate, b_gate, w_up, b_up, w_down, b_down):
    bsz, seq, d = x.shape
    depth = ada_w.shape[0]
    n_tok = bsz * seq
    bm = EXPERT_ROWS
    sub = d // LANES
    xcur = x.reshape(n_tok, d)
    pos = positions.reshape(n_tok, 1).astype(I32)
    for l in range(depth):
        ada = _ada(c, ada_w[l], ada_b[l])
        sh_m, sc_m, gt_m, sh_f, sc_f, gt_f = [a.reshape(bsz, 1, d) for a in jnp.split(ada, 6, axis=-1)]

        u, q, k, v, gs, ga, km = _inproj(xcur, sc_m, sh_m, mix_pre_g[l].reshape(1, d), pos, w_in[l],
                                         seq, tm=min(512, seq))
        ys = _s5(u, ssm_lam_re[l], ssm_lam_im[l], ssm_log_dt[l], ssm_b_re[l], ssm_b_im[l],
                 ssm_c_re[l], ssm_c_im[l], bsz, seq)
        ya = _moba(q, k, v, km, bsz, seq)
        tm = ROUTE_TILE
        x1, h2, tw, loc, before, cnt = _merge(
            ys, u, ya, gs, ga, xcur, ssm_d[l], ssm_w_glu[l], w_ssm_branch[l], w_attn_branch[l], w_out[l],
            mix_post_g[l], gt_m, ffn_pre_g[l], sc_f, sh_f, router_w[l], router_b[l], seq, tm=tm)

        ne = router_w.shape[-1]
        nt = n_tok // tm
        counts = cnt[0, :ne].astype(I32)
        padded = (counts + bm - 1) // bm * bm
        p_ends = jnp.cumsum(padded)
        p_starts = p_ends - padded
        before = before.reshape(nt, LANES)[:, :ne].astype(I32)
        cnt_tile = jnp.concatenate([before[1:], counts[None]], axis=0) - before
        start_tile = p_starts[None, :] + before
        n_blocks = (n_tok * TOP_K) // bm + ne
        n_rows = n_blocks * bm
        blk_start = jnp.arange(n_blocks, dtype=I32)[:, None] * bm
        blk_exp = jnp.minimum(jnp.sum((blk_start >= p_ends[None, :]).astype(I32), axis=1), ne - 1)
        n_valid = (p_ends[-1:] // bm).astype(I32)

        xs = _dispatch(h2, loc[:, :TOP_K], cnt_tile, start_tile, p_starts + counts, padded - counts, p_ends[-1:],
                       n_rows, sub, bm, tm)
        yb = _experts(xs, blk_exp, n_valid, w_gate[l], b_gate[l], w_up[l], b_up[l], w_down[l], b_down[l], bm)
        xcur = _combine(yb, loc, tw, cnt_tile, start_tile, x1, ffn_post_g[l], gt_f, seq, tm)
    return xcur.reshape(bsz, seq, d).astype(x.dtype)
```
